```python
import jax, jax.numpy as jnp
from jax import lax
import numpy as np

D_MODEL = 1024
BATCH = 8
SEQ = 4096
DEPTH = 2

N_MIXERS = 2
N_LRU = (DEPTH + 1) // 2
N_FOX = DEPTH // 2
D_RNN = D_MODEL
LRU_BLOCKS = 16
LRU_BLOCK_DIM = D_RNN // LRU_BLOCKS
CONV_WIDTH = 4
LRU_C = 8.0
N_HEADS = 16
HEAD_DIM = D_MODEL // N_HEADS
Q_BLOCK = 128
D_FF = 4 * D_MODEL
EPS = 1e-6
NEG_INF = -1e30

kernel_name = "hawk_fox_interleaved_trunk"


def rms_norm(x, g):
    xf = x.astype(jnp.float32)
    y = xf * lax.rsqrt(jnp.mean(xf * xf, axis=-1, keepdims=True) + EPS)
    return (y * g.astype(jnp.float32)).astype(x.dtype)


def causal_depthwise_conv(x, w, b):
    S = x.shape[1]
    xp = jnp.pad(x, ((0, 0), (CONV_WIDTH - 1, 0), (0, 0)))
    y = b
    for k in range(CONV_WIDTH):
        y = y + xp[:, k:k + S] * w[k]
    return y


def rglru_mixer(h, w_in, conv_w, conv_b, w_r, b_r, w_i, b_i, lam, w_out):
    B, S, _ = h.shape
    u = h @ w_in
    gate_branch, x_branch = jnp.split(u, 2, axis=-1)
    gate_branch = jax.nn.gelu(gate_branch)
    xc = causal_depthwise_conv(x_branch, conv_w, conv_b)
    xb = xc.reshape(B, S, LRU_BLOCKS, LRU_BLOCK_DIM)
    r = jax.nn.sigmoid((jnp.einsum('bsnd,nde->bsne', xb, w_r) + b_r).astype(jnp.float32))
    i = jax.nn.sigmoid((jnp.einsum('bsnd,nde->bsne', xb, w_i) + b_i).astype(jnp.float32))
    r = r.reshape(B, S, D_RNN)
    i = i.reshape(B, S, D_RNN)
    log_a = LRU_C * r * jax.nn.log_sigmoid(lam.astype(jnp.float32))
    a = jnp.exp(log_a)
    mult = jnp.sqrt(-jnp.expm1(2.0 * log_a))
    bt = mult * (i * xc.astype(jnp.float32))

    def combine(left, right):
        a_l, b_l = left
        a_r, b_r_ = right
        return a_l * a_r, a_r * b_l + b_r_

    _, hs = lax.associative_scan(combine, (a, bt), axis=1)
    y = gate_branch * hs.astype(h.dtype)
    return y @ w_out


def fox_mixer(h, w_in, b_f, q_gain, k_gain, w_out):
    B, S, _ = h.shape
    u = h @ w_in
    q, k, v, f_logit = jnp.split(u, [D_MODEL, 2 * D_MODEL, 3 * D_MODEL], axis=-1)

    def heads(t):
        return t.reshape(B, S, N_HEADS, HEAD_DIM).transpose(0, 2, 1, 3)

    q = rms_norm(heads(q), q_gain)
    k = rms_norm(heads(k), k_gain)
    v = heads(v)
    log_f = jax.nn.log_sigmoid((f_logit + b_f).astype(jnp.float32))
    c = jnp.cumsum(log_f, axis=1).transpose(0, 2, 1)
    scale = HEAD_DIM ** -0.5
    nb = S // Q_BLOCK
    q_blocks = q.reshape(B, N_HEADS, nb, Q_BLOCK, HEAD_DIM).transpose(2, 0, 1, 3, 4)
    c_blocks = c.reshape(B, N_HEADS, nb, Q_BLOCK).transpose(2, 0, 1, 3)
    k_pos = jnp.arange(S)

    def attend(args):
        qb, cb, blk = args
        q_pos = blk * Q_BLOCK + jnp.arange(Q_BLOCK)
        s = jnp.einsum('bhqd,bhkd->bhqk', qb, k).astype(jnp.float32) * scale
        s = s + cb[..., :, None] - c[:, :, None, :]
        s = jnp.where(k_pos[None, :] <= q_pos[:, None], s, NEG_INF)
        p = jax.nn.softmax(s, axis=-1)
        return jnp.einsum('bhqk,bhkd->bhqd', p.astype(v.dtype), v)

    o = lax.map(attend, (q_blocks, c_blocks, jnp.arange(nb)))
    o = o.transpose(1, 0, 3, 2, 4).reshape(B, S, D_MODEL)
    return o @ w_out


def sq_relu_mlp(h, w1, w2):
    return jnp.square(jax.nn.relu(h @ w1)) @ w2


def _fwd_setup_inputs(seed: int = 0) -> dict:
    key = jax.random.key(seed)
    ks = jax.random.split(key, 20)
    f32 = jnp.float32
    nrm = lambda k, shape, fan_in: jax.random.normal(k, shape, f32) * (fan_in ** -0.5)
    a0 = jax.random.uniform(ks[12], (N_LRU, D_RNN), f32, minval=0.9, maxval=0.999)
    return {
        "x": jax.random.normal(ks[0], (BATCH, SEQ, D_MODEL), f32),
        "mix_norm": 1.0 + 0.02 * jax.random.normal(ks[1], (DEPTH, D_MODEL), f32),
        "mlp_norm": 1.0 + 0.02 * jax.random.normal(ks[2], (DEPTH, D_MODEL), f32),
        "mlp_w1": nrm(ks[3], (DEPTH, D_MODEL, D_FF), D_MODEL),
        "mlp_w2": nrm(ks[4], (DEPTH, D_FF, D_MODEL), D_FF),
        "lru_w_in": nrm(ks[5], (N_LRU, D_MODEL, 2 * D_RNN), D_MODEL),
        "lru_conv_w": nrm(ks[6], (N_LRU, CONV_WIDTH, D_RNN), CONV_WIDTH),
        "lru_conv_b": 0.02 * jax.random.normal(ks[7], (N_LRU, D_RNN), f32),
        "lru_w_r": nrm(ks[8], (N_LRU, LRU_BLOCKS, LRU_BLOCK_DIM, LRU_BLOCK_DIM), LRU_BLOCK_DIM),
        "lru_b_r": 0.1 * jax.random.normal(ks[9], (N_LRU, LRU_BLOCKS, LRU_BLOCK_DIM), f32),
        "lru_w_i": nrm(ks[10], (N_LRU, LRU_BLOCKS, LRU_BLOCK_DIM, LRU_BLOCK_DIM), LRU_BLOCK_DIM),
        "lru_b_i": 0.1 * jax.random.normal(ks[11], (N_LRU, LRU_BLOCKS, LRU_BLOCK_DIM), f32),
        "lru_lambda": jnp.log(a0) - jnp.log1p(-a0),
        "lru_w_out": nrm(ks[13], (N_LRU, D_RNN, D_MODEL), D_RNN),
        "fox_w_in": nrm(ks[14], (N_FOX, D_MODEL, 3 * D_MODEL + N_HEADS), D_MODEL),
        "fox_b_f": jax.random.uniform(ks[15], (N_FOX, N_HEADS), f32, minval=1.0, maxval=6.0),
        "fox_q_gain": 1.0 + 0.02 * jax.random.normal(ks[16], (N_FOX, HEAD_DIM), f32),
        "fox_k_gain": 1.0 + 0.02 * jax.random.normal(ks[17], (N_FOX, HEAD_DIM), f32),
        "fox_w_out": nrm(ks[18], (N_FOX, D_MODEL, D_MODEL), D_MODEL),
    }


def _fwd_reference(x, mix_norm, mlp_norm, mlp_w1, mlp_w2, lru_w_in, lru_conv_w, lru_conv_b,
              lru_w_r, lru_b_r, lru_w_i, lru_b_i, lru_lambda, lru_w_out,
              fox_w_in, fox_b_f, fox_q_gain, fox_k_gain, fox_w_out):
    for layer in range(DEPTH):
        h = rms_norm(x, mix_norm[layer])
        j = layer // N_MIXERS
        if layer % N_MIXERS == 0:
            mixed = rglru_mixer(h, lru_w_in[j], lru_conv_w[j], lru_conv_b[j], lru_w_r[j],
                                lru_b_r[j], lru_w_i[j], lru_b_i[j], lru_lambda[j], lru_w_out[j])
        else:
            mixed = fox_mixer(h, fox_w_in[j], fox_b_f[j], fox_q_gain[j], fox_k_gain[j],
                              fox_w_out[j])
        x = x + mixed
        x = x + sq_relu_mlp(rms_norm(x, mlp_norm[layer]), mlp_w1[layer], mlp_w2[layer])
    return x


import jax as _jax
import jax.numpy as _jnp

TWIN_FORMAT = 'train_step'
FWD_PARAMS = ['x', 'mix_norm', 'mlp_norm', 'mlp_w1', 'mlp_w2', 'lru_w_in', 'lru_conv_w', 'lru_conv_b', 'lru_w_r', 'lru_b_r', 'lru_w_i', 'lru_b_i', 'lru_lambda', 'lru_w_out', 'fox_w_in', 'fox_b_f', 'fox_q_gain', 'fox_k_gain', 'fox_w_out']
TWIN_WEIGHTS = ['mix_norm', 'mlp_norm', 'mlp_w1', 'mlp_w2', 'lru_w_in', 'lru_conv_w', 'lru_conv_b', 'lru_w_r', 'lru_b_r', 'lru_w_i', 'lru_b_i', 'lru_lambda', 'lru_w_out', 'fox_w_in', 'fox_b_f', 'fox_q_gain', 'fox_k_gain', 'fox_w_out']
TWIN_DIFF_INPUT = 'x'
TWIN_INPUTS = ['x', 'mix_norm', 'mlp_norm', 'mlp_w1', 'mlp_w2', 'lru_w_in', 'lru_conv_w', 'lru_conv_b', 'lru_w_r', 'lru_b_r', 'lru_w_i', 'lru_b_i', 'lru_lambda', 'lru_w_out', 'fox_w_in', 'fox_b_f', 'fox_q_gain', 'fox_k_gain', 'fox_w_out', 'loss_target', 'm_mix_norm', 'm_mlp_norm', 'm_mlp_w1', 'm_mlp_w2', 'm_lru_w_in', 'm_lru_conv_w', 'm_lru_conv_b', 'm_lru_w_r', 'm_lru_b_r', 'm_lru_w_i', 'm_lru_b_i', 'm_lru_lambda', 'm_lru_w_out', 'm_fox_w_in', 'm_fox_b_f', 'm_fox_q_gain', 'm_fox_k_gain', 'm_fox_w_out', 'v_mix_norm', 'v_mlp_norm', 'v_mlp_w1', 'v_mlp_w2', 'v_lru_w_in', 'v_lru_conv_w', 'v_lru_conv_b', 'v_lru_w_r', 'v_lru_b_r', 'v_lru_w_i', 'v_lru_b_i', 'v_lru_lambda', 'v_lru_w_out', 'v_fox_w_in', 'v_fox_b_f', 'v_fox_q_gain', 'v_fox_k_gain', 'v_fox_w_out']
TWIN_OUTPUTS = ['loss', 'grad_x', 'grad_mix_norm', 'grad_mlp_norm', 'grad_mlp_w1', 'grad_mlp_w2', 'grad_lru_w_in', 'grad_lru_conv_w', 'grad_lru_conv_b', 'grad_lru_w_r', 'grad_lru_b_r', 'grad_lru_w_i', 'grad_lru_b_i', 'grad_lru_lambda', 'grad_lru_w_out', 'grad_fox_w_in', 'grad_fox_b_f', 'grad_fox_q_gain', 'grad_fox_k_gain', 'grad_fox_w_out', 'delta_mix_norm', 'delta_mlp_norm', 'delta_mlp_w1', 'delta_mlp_w2', 'delta_lru_w_in', 'delta_lru_conv_w', 'delta_lru_conv_b', 'delta_lru_w_r', 'delta_lru_b_r', 'delta_lru_w_i', 'delta_lru_b_i', 'delta_lru_lambda', 'delta_lru_w_out', 'delta_fox_w_in', 'delta_fox_b_f', 'delta_fox_q_gain', 'delta_fox_k_gain', 'delta_fox_w_out', 'new_m_mix_norm', 'new_m_mlp_norm', 'new_m_mlp_w1', 'new_m_mlp_w2', 'new_m_lru_w_in', 'new_m_lru_conv_w', 'new_m_lru_conv_b', 'new_m_lru_w_r', 'new_m_lru_b_r', 'new_m_lru_w_i', 'new_m_lru_b_i', 'new_m_lru_lambda', 'new_m_lru_w_out', 'new_m_fox_w_in', 'new_m_fox_b_f', 'new_m_fox_q_gain', 'new_m_fox_k_gain', 'new_m_fox_w_out', 'new_v_mix_norm', 'new_v_mlp_norm', 'new_v_mlp_w1', 'new_v_mlp_w2', 'new_v_lru_w_in', 'new_v_lru_conv_w', 'new_v_lru_conv_b', 'new_v_lru_w_r', 'new_v_lru_b_r', 'new_v_lru_w_i', 'new_v_lru_b_i', 'new_v_lru_lambda', 'new_v_lru_w_out', 'new_v_fox_w_in', 'new_v_fox_b_f', 'new_v_fox_q_gain', 'new_v_fox_k_gain', 'new_v_fox_w_out']
TWIN_LEAF_KINDS = {'loss': 'loss', 'grad_x': 'grad_x', 'grad_mix_norm': 'grad_w', 'grad_mlp_norm': 'grad_w', 'grad_mlp_w1': 'grad_w', 'grad_mlp_w2': 'grad_w', 'grad_lru_w_in': 'grad_w', 'grad_lru_conv_w': 'grad_w', 'grad_lru_conv_b': 'grad_w', 'grad_lru_w_r': 'grad_w', 'grad_lru_b_r': 'grad_w', 'grad_lru_w_i': 'grad_w', 'grad_lru_b_i': 'grad_w', 'grad_lru_lambda': 'grad_w', 'grad_lru_w_out': 'grad_w', 'grad_fox_w_in': 'grad_w', 'grad_fox_b_f': 'grad_w', 'grad_fox_q_gain': 'grad_w', 'grad_fox_k_gain': 'grad_w', 'grad_fox_w_out': 'grad_w', 'delta_mix_norm': 'delta_w', 'delta_mlp_norm': 'delta_w', 'delta_mlp_w1': 'delta_w', 'delta_mlp_w2': 'delta_w', 'delta_lru_w_in': 'delta_w', 'delta_lru_conv_w': 'delta_w', 'delta_lru_conv_b': 'delta_w', 'delta_lru_w_r': 'delta_w', 'delta_lru_b_r': 'delta_w', 'delta_lru_w_i': 'delta_w', 'delta_lru_b_i': 'delta_w', 'delta_lru_lambda': 'delta_w', 'delta_lru_w_out': 'delta_w', 'delta_fox_w_in': 'delta_w', 'delta_fox_b_f': 'delta_w', 'delta_fox_q_gain': 'delta_w', 'delta_fox_k_gain': 'delta_w', 'delta_fox_w_out': 'delta_w', 'new_m_mix_norm': 'new_m', 'new_m_mlp_norm': 'new_m', 'new_m_mlp_w1': 'new_m', 'new_m_mlp_w2': 'new_m', 'new_m_lru_w_in': 'new_m', 'new_m_lru_conv_w': 'new_m', 'new_m_lru_conv_b': 'new_m', 'new_m_lru_w_r': 'new_m', 'new_m_lru_b_r': 'new_m', 'new_m_lru_w_i': 'new_m', 'new_m_lru_b_i': 'new_m', 'new_m_lru_lambda': 'new_m', 'new_m_lru_w_out': 'new_m', 'new_m_fox_w_in': 'new_m', 'new_m_fox_b_f': 'new_m', 'new_m_fox_q_gain': 'new_m', 'new_m_fox_k_gain': 'new_m', 'new_m_fox_w_out': 'new_m', 'new_v_mix_norm': 'new_v', 'new_v_mlp_norm': 'new_v', 'new_v_mlp_w1': 'new_v', 'new_v_mlp_w2': 'new_v', 'new_v_lru_w_in': 'new_v', 'new_v_lru_conv_w': 'new_v', 'new_v_lru_conv_b': 'new_v', 'new_v_lru_w_r': 'new_v', 'new_v_lru_b_r': 'new_v', 'new_v_lru_w_i': 'new_v', 'new_v_lru_b_i': 'new_v', 'new_v_lru_lambda': 'new_v', 'new_v_lru_w_out': 'new_v', 'new_v_fox_w_in': 'new_v', 'new_v_fox_b_f': 'new_v', 'new_v_fox_q_gain': 'new_v', 'new_v_fox_k_gain': 'new_v', 'new_v_fox_w_out': 'new_v'}


def _forward(args):
    return _fwd_reference(*[args[k] for k in FWD_PARAMS])


def _output_shape():
    out = _jax.eval_shape(lambda: _forward(_fwd_setup_inputs(0)))
    return out.shape, out.dtype

N_MICROBATCH = 1
ADAM_LR = 0.001
ADAM_B1 = 0.9
ADAM_B2 = 0.999
ADAM_EPS = 1e-08
ADAM_WD = 0.01
ADAM_STEP = 10
PER_EXAMPLE_BATCH_AXIS = {'x': 0, 'loss_target': 0}
SHARED_INPUTS = []
_WEIGHT_DTYPES = {'mix_norm': _jnp.float32, 'mlp_norm': _jnp.float32, 'mlp_w1': _jnp.float32, 'mlp_w2': _jnp.float32, 'lru_w_in': _jnp.float32, 'lru_conv_w': _jnp.float32, 'lru_conv_b': _jnp.float32, 'lru_w_r': _jnp.float32, 'lru_b_r': _jnp.float32, 'lru_w_i': _jnp.float32, 'lru_b_i': _jnp.float32, 'lru_lambda': _jnp.float32, 'lru_w_out': _jnp.float32, 'fox_w_in': _jnp.float32, 'fox_b_f': _jnp.float32, 'fox_q_gain': _jnp.float32, 'fox_k_gain': _jnp.float32, 'fox_w_out': _jnp.float32}
MOMENT_SCALE = {'mix_norm': 1.185142e+01, 'mlp_norm': 9.717601e+01, 'mlp_w1': 4.678510e+00, 'mlp_w2': 1.870375e+01, 'lru_w_in': 5.860755e-01, 'lru_conv_w': 3.678281e+00, 'lru_conv_b': 3.561785e+01, 'lru_w_r': 1.348159e+00, 'lru_b_r': 1.052183e+00, 'lru_w_i': 2.639353e+00, 'lru_b_i': 2.286195e+00, 'lru_lambda': 1.560663e+00, 'lru_w_out': 2.359443e+00, 'fox_w_in': 6.544826e+00, 'fox_b_f': 4.492106e+01, 'fox_q_gain': 1.066866e+01, 'fox_k_gain': 1.068983e+01, 'fox_w_out': 1.054585e+01}


def _to_microbatches(a, axis):
    t = _jnp.moveaxis(a, axis, 0)
    t = t.reshape((N_MICROBATCH, t.shape[0] // N_MICROBATCH) + t.shape[1:])
    return _jnp.moveaxis(t, 1, axis + 1)


def setup_inputs(seed: int = 0) -> dict:
    inp = _fwd_setup_inputs(seed)
    key = _jax.random.fold_in(_jax.random.key(seed), 7919)
    shape, _ = _output_shape()
    out = dict(inp)
    out["loss_target"] = _jax.random.normal(_jax.random.fold_in(key, 0), shape, _jnp.float32)
    for i, name in enumerate(TWIN_WEIGHTS):
        w = inp[name].astype(_jnp.float32)
        if MOMENT_SCALE is None:
            s = _jnp.sqrt(_jnp.mean(_jnp.square(w)) + 1e-30)
        else:
            s = MOMENT_SCALE[name]
        km, kv = _jax.random.split(_jax.random.fold_in(key, i + 1))
        out[name] = w
        out["m_" + name] = s * _jax.random.normal(km, w.shape, _jnp.float32)
        out["v_" + name] = (s * s) * _jax.random.uniform(kv, w.shape, _jnp.float32, 0.5, 1.5)
    if N_MICROBATCH > 1:
        for name, axis in PER_EXAMPLE_BATCH_AXIS.items():
            out[name] = _to_microbatches(out[name], axis)
    return {'x': out['x'], 'mix_norm': out['mix_norm'], 'mlp_norm': out['mlp_norm'], 'mlp_w1': out['mlp_w1'], 'mlp_w2': out['mlp_w2'], 'lru_w_in': out['lru_w_in'], 'lru_conv_w': out['lru_conv_w'], 'lru_conv_b': out['lru_conv_b'], 'lru_w_r': out['lru_w_r'], 'lru_b_r': out['lru_b_r'], 'lru_w_i': out['lru_w_i'], 'lru_b_i': out['lru_b_i'], 'lru_lambda': out['lru_lambda'], 'lru_w_out': out['lru_w_out'], 'fox_w_in': out['fox_w_in'], 'fox_b_f': out['fox_b_f'], 'fox_q_gain': out['fox_q_gain'], 'fox_k_gain': out['fox_k_gain'], 'fox_w_out': out['fox_w_out'], 'loss_target': out['loss_target'], 'm_mix_norm': out['m_mix_norm'], 'm_mlp_norm': out['m_mlp_norm'], 'm_mlp_w1': out['m_mlp_w1'], 'm_mlp_w2': out['m_mlp_w2'], 'm_lru_w_in': out['m_lru_w_in'], 'm_lru_conv_w': out['m_lru_conv_w'], 'm_lru_conv_b': out['m_lru_conv_b'], 'm_lru_w_r': out['m_lru_w_r'], 'm_lru_b_r': out['m_lru_b_r'], 'm_lru_w_i': out['m_lru_w_i'], 'm_lru_b_i': out['m_lru_b_i'], 'm_lru_lambda': out['m_lru_lambda'], 'm_lru_w_out': out['m_lru_w_out'], 'm_fox_w_in': out['m_fox_w_in'], 'm_fox_b_f': out['m_fox_b_f'], 'm_fox_q_gain': out['m_fox_q_gain'], 'm_fox_k_gain': out['m_fox_k_gain'], 'm_fox_w_out': out['m_fox_w_out'], 'v_mix_norm': out['v_mix_norm'], 'v_mlp_norm': out['v_mlp_norm'], 'v_mlp_w1': out['v_mlp_w1'], 'v_mlp_w2': out['v_mlp_w2'], 'v_lru_w_in': out['v_lru_w_in'], 'v_lru_conv_w': out['v_lru_conv_w'], 'v_lru_conv_b': out['v_lru_conv_b'], 'v_lru_w_r': out['v_lru_w_r'], 'v_lru_b_r': out['v_lru_b_r'], 'v_lru_w_i': out['v_lru_w_i'], 'v_lru_b_i': out['v_lru_b_i'], 'v_lru_lambda': out['v_lru_lambda'], 'v_lru_w_out': out['v_lru_w_out'], 'v_fox_w_in': out['v_fox_w_in'], 'v_fox_b_f': out['v_fox_b_f'], 'v_fox_q_gain': out['v_fox_q_gain'], 'v_fox_k_gain': out['v_fox_k_gain'], 'v_fox_w_out': out['v_fox_w_out']}


def _loss(weights, diff, rest, loss_target):
    with _jax.named_scope("forward"):
        args = {**rest, TWIN_DIFF_INPUT: diff, **{k: w.astype(_WEIGHT_DTYPES[k]) for k, w in weights.items()}}
        y = _forward(args)
    with _jax.named_scope("loss_head"):
        err = _jnp.square(y.astype(_jnp.float32) - loss_target)
        return 0.5 * _jnp.sum(_jnp.mean(err, axis=-1)) if err.ndim else 0.5 * err


def _adamw(w, g, m, v):
    m = ADAM_B1 * m + (1.0 - ADAM_B1) * g
    v = ADAM_B2 * v + (1.0 - ADAM_B2) * _jnp.square(g)
    m_hat = m / (1.0 - ADAM_B1 ** ADAM_STEP)
    v_hat = v / (1.0 - ADAM_B2 ** ADAM_STEP)
    delta = -ADAM_LR * (m_hat / (_jnp.sqrt(v_hat) + ADAM_EPS) + ADAM_WD * w)
    return delta, m, v


def reference(x, mix_norm, mlp_norm, mlp_w1, mlp_w2, lru_w_in, lru_conv_w, lru_conv_b, lru_w_r, lru_b_r, lru_w_i, lru_b_i, lru_lambda, lru_w_out, fox_w_in, fox_b_f, fox_q_gain, fox_k_gain, fox_w_out, loss_target, m_mix_norm, m_mlp_norm, m_mlp_w1, m_mlp_w2, m_lru_w_in, m_lru_conv_w, m_lru_conv_b, m_lru_w_r, m_lru_b_r, m_lru_w_i, m_lru_b_i, m_lru_lambda, m_lru_w_out, m_fox_w_in, m_fox_b_f, m_fox_q_gain, m_fox_k_gain, m_fox_w_out, v_mix_norm, v_mlp_norm, v_mlp_w1, v_mlp_w2, v_lru_w_in, v_lru_conv_w, v_lru_conv_b, v_lru_w_r, v_lru_b_r, v_lru_w_i, v_lru_b_i, v_lru_lambda, v_lru_w_out, v_fox_w_in, v_fox_b_f, v_fox_q_gain, v_fox_k_gain, v_fox_w_out):
    given = dict(x=x, mix_norm=mix_norm, mlp_norm=mlp_norm, mlp_w1=mlp_w1, mlp_w2=mlp_w2, lru_w_in=lru_w_in, lru_conv_w=lru_conv_w, lru_conv_b=lru_conv_b, lru_w_r=lru_w_r, lru_b_r=lru_b_r, lru_w_i=lru_w_i, lru_b_i=lru_b_i, lru_lambda=lru_lambda, lru_w_out=lru_w_out, fox_w_in=fox_w_in, fox_b_f=fox_b_f, fox_q_gain=fox_q_gain, fox_k_gain=fox_k_gain, fox_w_out=fox_w_out, loss_target=loss_target, m_mix_norm=m_mix_norm, m_mlp_norm=m_mlp_norm, m_mlp_w1=m_mlp_w1, m_mlp_w2=m_mlp_w2, m_lru_w_in=m_lru_w_in, m_lru_conv_w=m_lru_conv_w, m_lru_conv_b=m_lru_conv_b, m_lru_w_r=m_lru_w_r, m_lru_b_r=m_lru_b_r, m_lru_w_i=m_lru_w_i, m_lru_b_i=m_lru_b_i, m_lru_lambda=m_lru_lambda, m_lru_w_out=m_lru_w_out, m_fox_w_in=m_fox_w_in, m_fox_b_f=m_fox_b_f, m_fox_q_gain=m_fox_q_gain, m_fox_k_gain=m_fox_k_gain, m_fox_w_out=m_fox_w_out, v_mix_norm=v_mix_norm, v_mlp_norm=v_mlp_norm, v_mlp_w1=v_mlp_w1, v_mlp_w2=v_mlp_w2, v_lru_w_in=v_lru_w_in, v_lru_conv_w=v_lru_conv_w, v_lru_conv_b=v_lru_conv_b, v_lru_w_r=v_lru_w_r, v_lru_b_r=v_lru_b_r, v_lru_w_i=v_lru_w_i, v_lru_b_i=v_lru_b_i, v_lru_lambda=v_lru_lambda, v_lru_w_out=v_lru_w_out, v_fox_w_in=v_fox_w_in, v_fox_b_f=v_fox_b_f, v_fox_q_gain=v_fox_q_gain, v_fox_k_gain=v_fox_k_gain, v_fox_w_out=v_fox_w_out)
    weights = {n: given[n] for n in TWIN_WEIGHTS}
    shared = {n: given[n] for n in SHARED_INPUTS}
    per_example = {n: given[n] for n in ['x']}
    grad_fn = _jax.value_and_grad(_loss, argnums=(0, 1))

    def one_microbatch(ex, loss_target):
        ex = dict(ex)
        diff = ex.pop(TWIN_DIFF_INPUT)
        return grad_fn(weights, diff, {**shared, **ex}, loss_target)

    if N_MICROBATCH == 1:
        loss, (grad_w, grad_x) = one_microbatch(per_example, given["loss_target"])
    else:
        def body(carry, xs):
            loss_sum, grad_sum = carry
            l_k, (gw_k, gx_k) = one_microbatch(xs[0], xs[1])
            with _jax.named_scope("update"):
                return (loss_sum + l_k, _jax.tree.map(_jnp.add, grad_sum, gw_k)), gx_k

        init = (_jnp.zeros((), _jnp.float32), _jax.tree.map(_jnp.zeros_like, weights))
        (loss, grad_w), grad_x = _jax.lax.scan(body, init, (per_example, given["loss_target"]))
    with _jax.named_scope("update"):
        delta_w, new_m, new_v = {}, {}, {}
        for n in TWIN_WEIGHTS:
            delta_w[n], new_m[n], new_v[n] = _adamw(weights[n], grad_w[n], given["m_" + n], given["v_" + n])
    return (loss, grad_x, *[grad_w[n] for n in TWIN_WEIGHTS], *[delta_w[n] for n in TWIN_WEIGHTS],
            *[new_m[n] for n in TWIN_WEIGHTS], *[new_v[n] for n in TWIN_WEIGHTS])
```

```python
import math

import jax
import jax.numpy as jnp
from jax import lax
from jax.experimental import pallas as pl
from jax.experimental.pallas import tpu as pltpu

F32 = jnp.float32
BF16 = jnp.bfloat16

N_DEV = 8
D_MODEL = 1024
D_FF = 4096
N_HEADS = 16
HEAD_DIM = 64
LRU_BLOCK_DIM = 64
CONV_WIDTH = 4
LRU_C = 8.0
EPS = 1e-6
NEG_INF = -1e30
ATTN_SCALE = HEAD_DIM ** -0.5
LANES = 128
N_CBLK = D_MODEL // LANES
VMEM_LIMIT = 52 * 2 ** 20

ADAM_LR = 0.001
ADAM_B1 = 0.9
ADAM_B2 = 0.999
ADAM_EPS = 1e-08
ADAM_WD = 0.01
ADAM_STEP = 10

_NT = (((1,), (1,)), ((), ()))
_TN = (((0,), (0,)), ((), ()))


def _params(*sem):
    return pltpu.CompilerParams(dimension_semantics=sem, vmem_limit_bytes=VMEM_LIMIT)


def _resident(shape):
    zeros = (0,) * len(shape)
    return pl.BlockSpec(shape, lambda *_: zeros, pipeline_mode=pl.Buffered(1))


def _dot(a, b):
    return jnp.dot(a, b, preferred_element_type=F32)


def _dot_nt(a, b):
    return lax.dot_general(a, b, _NT, preferred_element_type=F32)


def _dot_tn(a, b):
    return lax.dot_general(a, b, _TN, preferred_element_type=F32)


def _sigmoid(x):
    return 1.0 / (1.0 + jnp.exp(-x))


def _log_sigmoid(x):
    return -(jnp.maximum(-x, 0.0) + jnp.log1p(jnp.exp(-jnp.abs(x))))


def _expm1(x):
    poly = x * (1.0 + x * (0.5 + x * (1.0 / 6.0 + x * (1.0 / 24.0 + x * (1.0 / 120.0)))))
    return jnp.where(jnp.abs(x) < 0.1, poly, jnp.exp(x) - 1.0)


_GELU_K = 0.7978845608028654


def _gelu(x):
    return 0.5 * x * (1.0 + jnp.tanh(_GELU_K * (x + 0.044715 * (x * x * x))))


def _gelu_grad(x):
    t = jnp.tanh(_GELU_K * (x + 0.044715 * (x * x * x)))
    return 0.5 * (1.0 + t) + 0.5 * x * (1.0 - t * t) * (_GELU_K * (1.0 + 3 * 0.044715 * x * x))


def _rms_scale(x):
    return lax.rsqrt(jnp.mean(x * x, axis=-1, keepdims=True) + EPS)


def _norm_bwd(dh, x, g):
    rs = _rms_scale(x)
    xhat = x * rs
    dxhat = dh * g
    dx = rs * (dxhat - xhat * jnp.mean(dxhat * xhat, axis=-1, keepdims=True))
    return dx, jnp.sum(dh * xhat, axis=0, keepdims=True)


def _token_tile(S, want):
    tm = min(S, want)
    assert S % tm == 0
    return tm


def _norm_matmul(x, g, ws, name, tm=256):
    S, D = x.shape
    tm = _token_tile(S, tm)
    n = len(ws)

    def body(x_ref, g_ref, *refs):
        w_refs, o_refs, h_ref = refs[:n], refs[n:2 * n], refs[2 * n]
        xv = x_ref[...]
        h = (xv * _rms_scale(xv) * g_ref[...]).astype(BF16)
        h_ref[...] = h
        for w_ref, o_ref in zip(w_refs, o_refs):
            nb, _, nw = w_ref.shape
            for d in range(nb):
                o_ref[:, d * nw:(d + 1) * nw] = _dot(h, w_ref[d])

    widths = [w.shape[0] * w.shape[2] for w in ws]
    outs = pl.pallas_call(
        body, name=name, grid=(S // tm,),
        in_specs=[pl.BlockSpec((tm, D), lambda i: (i, 0)), _resident((1, D))]
        + [_resident(w.shape) for w in ws],
        out_specs=[pl.BlockSpec((tm, n_), lambda i: (i, 0)) for n_ in widths]
        + [pl.BlockSpec((tm, D), lambda i: (i, 0))],
        out_shape=[jax.ShapeDtypeStruct((S, n_), F32) for n_ in widths]
        + [jax.ShapeDtypeStruct((S, D), BF16)],
        compiler_params=_params("parallel"),
    )(x, g, *ws)
    return outs[:n], outs[n]


def _matmul_res(a, w, res, name, tm=512):
    S, K = a.shape
    N = w.shape[1]
    tm = _token_tile(S, tm)

    def body(a_ref, w_ref, r_ref, o_ref):
        o_ref[...] = r_ref[...] + _dot(a_ref[...], w_ref[...])

    return pl.pallas_call(
        body, name=name, grid=(S // tm,),
        in_specs=[pl.BlockSpec((tm, K), lambda i: (i, 0)), _resident((K, N)),
                  pl.BlockSpec((tm, N), lambda i: (i, 0))],
        out_specs=pl.BlockSpec((tm, N), lambda i: (i, 0)),
        out_shape=jax.ShapeDtypeStruct((S, N), F32),
        compiler_params=_params("parallel"),
    )(a, w, res)


def _matmul_nt(a, w, name, out_dtype, tm=512):
    S, N = a.shape
    K = w.shape[0]
    tm = _token_tile(S, tm)

    def body(a_ref, w_ref, o_ref):
        o_ref[...] = _dot_nt(a_ref[...].astype(BF16), w_ref[...]).astype(out_dtype)

    return pl.pallas_call(
        body, name=name, grid=(S // tm,),
        in_specs=[pl.BlockSpec((tm, N), lambda i: (i, 0)), _resident((K, N))],
        out_specs=pl.BlockSpec((tm, K), lambda i: (i, 0)),
        out_shape=jax.ShapeDtypeStruct((S, K), out_dtype),
        compiler_params=_params("parallel"),
    )(a, w)


def _proj_bwd(a_list, w_list, x, g, res, name, tm=256):
    S, D = x.shape
    tm = _token_tile(S, tm)
    n = len(a_list)

    def body(*refs):
        a_refs, w_refs = refs[:n], refs[n:2 * n]
        x_ref, g_ref, r_ref, dx_ref, dg_ref = refs[2 * n:]
        dh = jnp.zeros((tm, D), F32)
        for a_ref, w_ref in zip(a_refs, w_refs):
            nb, _, nw = w_ref.shape
            for d in range(nb):
                dh = dh + _dot_nt(a_ref[:, d * nw:(d + 1) * nw].astype(BF16), w_ref[d])
        dx, dg = _norm_bwd(dh, x_ref[...], g_ref[...])
        dx_ref[...] = r_ref[...] + dx

        @pl.when(pl.program_id(0) == 0)
        def _():
            dg_ref[...] = jnp.zeros_like(dg_ref)
        dg_ref[...] += dg

    tok = lambda width: pl.BlockSpec((tm, width), lambda i: (i, 0))
    return pl.pallas_call(
        body, name=name, grid=(S // tm,),
        in_specs=[tok(a.shape[1]) for a in a_list] + [_resident(w.shape) for w in w_list]
        + [tok(D), _resident((1, D)), tok(D)],
        out_specs=[tok(D), pl.BlockSpec((1, D), lambda i: (0, 0))],
        out_shape=[jax.ShapeDtypeStruct((S, D), F32), jax.ShapeDtypeStruct((1, D), F32)],
        compiler_params=_params("arbitrary"),
    )(*a_list, *w_list, x, g, res)


def _matmul_tn(a, b, nb, block_on, name, a_square=False, tm=512):
    S, K = a.shape
    N = b.shape[1]
    tm = _token_tile(S, tm)
    if block_on == "b":
        ka, nbk = K, N // nb
        a_spec = pl.BlockSpec((tm, K), lambda d, i: (i, 0))
        b_spec = pl.BlockSpec((tm, nbk), lambda d, i: (i, d))
    else:
        ka, nbk = K // nb, N
        a_spec = pl.BlockSpec((tm, ka), lambda d, i: (i, d))
        b_spec = pl.BlockSpec((tm, N), lambda d, i: (i, 0))

    def body(a_ref, b_ref, o_ref):
        av = a_ref[...]
        if a_square:
            av = av.astype(F32)
            av = av * av
        part = _dot_tn(av.astype(BF16), b_ref[...].astype(BF16))

        @pl.when(pl.program_id(1) == 0)
        def _():
            o_ref[...] = part

        @pl.when(pl.program_id(1) > 0)
        def _():
            o_ref[...] += part

    return pl.pallas_call(
        body, name=name, grid=(nb, S // tm),
        in_specs=[a_spec, b_spec],
        out_specs=pl.BlockSpec((None, ka, nbk), lambda d, i: (d, 0, 0)),
        out_shape=jax.ShapeDtypeStruct((nb, ka, nbk), F32),
        compiler_params=_params("parallel", "arbitrary"),
    )(a, b)


def _mlp_fwd(x, g, w1, w2, name, tm=256):
    S, D = x.shape
    nb, _, fb = w1.shape
    tm = _token_tile(S, tm)

    def body(x_ref, g_ref, w1_ref, w2_ref, o_ref, h_ref, r_ref):
        xv = x_ref[...]
        h = (xv * _rms_scale(xv) * g_ref[...]).astype(BF16)
        h_ref[...] = h
        acc = xv
        for d in range(nb):
            r = jnp.maximum(_dot(h, w1_ref[d]), 0.0)
            r_ref[:, d * fb:(d + 1) * fb] = r.astype(BF16)
            acc = acc + _dot((r * r).astype(BF16), w2_ref[d])
        o_ref[...] = acc

    tok = lambda width: pl.BlockSpec((tm, width), lambda i: (i, 0))
    return pl.pallas_call(
        body, name=name, grid=(S // tm,),
        in_specs=[tok(D), _resident((1, D)), _resident(w1.shape), _resident(w2.shape)],
        out_specs=[tok(D), tok(D), tok(nb * fb)],
        out_shape=[jax.ShapeDtypeStruct((S, D), F32), jax.ShapeDtypeStruct((S, D), BF16),
                   jax.ShapeDtypeStruct((S, nb * fb), BF16)],
        compiler_params=_params("parallel"),
    )(x, g, w1, w2)


def _mlp_bwd(dout, x, g, r, w1, w2, name, tm=256):
    S, D = x.shape
    nb, _, fb = w1.shape
    tm = _token_tile(S, tm)

    def body(do_ref, x_ref, g_ref, r_ref, w1_ref, w2_ref, dx_ref, dg_ref, da_ref):
        dov = do_ref[...]
        dob = dov.astype(BF16)
        dh = jnp.zeros((tm, D), F32)
        for d in range(nb):
            dz = _dot_nt(dob, w2_ref[d])
            da = (dz * (2.0 * r_ref[:, d * fb:(d + 1) * fb].astype(F32))).astype(BF16)
            da_ref[:, d * fb:(d + 1) * fb] = da
            dh = dh + _dot_nt(da, w1_ref[d])
        dx, dg = _norm_bwd(dh, x_ref[...], g_ref[...])
        dx_ref[...] = dov + dx

        @pl.when(pl.program_id(0) == 0)
        def _():
            dg_ref[...] = jnp.zeros_like(dg_ref)
        dg_ref[...] += dg

    tok = lambda width: pl.BlockSpec((tm, width), lambda i: (i, 0))
    return pl.pallas_call(
        body, name=name, grid=(S // tm,),
        in_specs=[tok(D), tok(D), _resident((1, D)), tok(nb * fb), _resident(w1.shape),
                  _resident(w2.shape)],
        out_specs=[tok(D), pl.BlockSpec((1, D), lambda i: (0, 0)), tok(nb * fb)],
        out_shape=[jax.ShapeDtypeStruct((S, D), F32), jax.ShapeDtypeStruct((1, D), F32),
                   jax.ShapeDtypeStruct((S, nb * fb), BF16)],
        compiler_params=_params("arbitrary"),
    )(dout, x, g, r, w1, w2)


def _scan_chunk(a, b, row, T, reverse):
    s = 1
    while s < T:
        if reverse:
            keep, shift = row < T - s, T - s
        else:
            keep, shift = row >= s, s
        a_sh = jnp.where(keep, pltpu.roll(a, shift, 0), 1.0)
        b_sh = jnp.where(keep, pltpu.roll(b, shift, 0), 0.0)
        b = a * b_sh + b
        a = a * a_sh
        s *= 2
    return a, b


def _row_of(x, row, r):
    return jnp.sum(jnp.where(row == r, x, 0.0), axis=0, keepdims=True)


def _shift_down(x, prev, row, k):
    if k == 0:
        return x
    return jnp.where(row < k, pltpu.roll(prev, k, 0), pltpu.roll(x, k, 0))


def _shift_up(x, nxt, row, k, T):
    if k == 0:
        return x
    return jnp.where(row < T - k, pltpu.roll(x, T - k, 0), pltpu.roll(nxt, T - k, 0))


def _lru_gates(xb, prev_xb, row, cw_ref, cb, wr, br, wi, bi, ls):
    xc = cb + cw_ref[pl.ds(0, 1), :] * _shift_down(xb, prev_xb, row, 3)
    for k in (2, 1, 0):
        xc = xc + cw_ref[pl.ds(3 - k, 1), :] * _shift_down(xb, prev_xb, row, k)
    xcb = xc.astype(BF16)
    r = _sigmoid(_dot(xcb, wr) + br)
    i = _sigmoid(_dot(xcb, wi) + bi)
    la = (LRU_C * r) * ls
    a = jnp.exp(la)
    m = jnp.sqrt(-_expm1(2.0 * la))
    return xc, xcb, r, i, a, m


def _lru_specs(S):
    col = lambda off: pl.BlockSpec((S, LANES), lambda j: (0, j + off))
    vec = pl.BlockSpec((1, LANES), lambda j: (0, j))
    mat = pl.BlockSpec((None, LANES, LANES), lambda j: (j, 0, 0))
    cwm = pl.BlockSpec((CONV_WIDTH, LANES), lambda j: (0, j))
    return col, vec, mat, cwm


def _lru_fwd(u, conv_w, conv_b, wr, br, wi, bi, lam, name):
    S = u.shape[0]
    T = _token_tile(S, 256)
    col, vec, mat, cwm = _lru_specs(S)

    def body(gp_ref, xb_ref, cw_ref, cb_ref, wr_ref, br_ref, wi_ref, bi_ref, lam_ref,
             y_ref, hs_ref):
        row = lax.broadcasted_iota(jnp.int32, (T, LANES), 0)
        ls = _log_sigmoid(lam_ref[...])
        cb, br, bi = cb_ref[...], br_ref[...], bi_ref[...]
        wr, wi = wr_ref[...], wi_ref[...]

        def chunk(ci, carry):
            prev_xb, hc = carry
            rows = pl.ds(pl.multiple_of(ci * T, T), T)
            xb = xb_ref[rows, :]
            xc, _, _, i, a, m = _lru_gates(xb, prev_xb, row, cw_ref, cb, wr, br, wi, bi, ls)
            ca, cbv = _scan_chunk(a, m * (i * xc), row, T, reverse=False)
            h = ca * hc + cbv
            hs_ref[rows, :] = h
            y_ref[rows, :] = (_gelu(gp_ref[rows, :]) * h).astype(BF16)
            return xb, _row_of(h, row, T - 1)

        lax.fori_loop(0, S // T, chunk,
                      (jnp.zeros((T, LANES), F32), jnp.zeros((1, LANES), F32)))

    return pl.pallas_call(
        body, name=name, grid=(N_CBLK,),
        in_specs=[col(0), col(N_CBLK), cwm, vec, mat, vec, mat, vec, vec],
        out_specs=[col(0), col(0)],
        out_shape=[jax.ShapeDtypeStruct((S, D_MODEL), BF16), jax.ShapeDtypeStruct((S, D_MODEL), F32)],
        compiler_params=_params("parallel"),
    )(u, u, conv_w, conv_b, wr, br, wi, bi, lam)


def _lru_bwd(dy, u, hs, conv_w, conv_b, wr, br, wi, bi, lam, name):
    S = u.shape[0]
    T = _token_tile(S, 256)
    n_chunk = S // T
    col, vec, mat, cwm = _lru_specs(S)

    def body(dy_ref, gp_ref, xb_ref, hs_ref, cw_ref, cb_ref, wr_ref, br_ref, wi_ref, bi_ref,
             lam_ref, du_ref, dcw_ref, dcb_ref, dbr_ref, dbi_ref, dlam_ref, dwr_ref, dwi_ref):
        row = lax.broadcasted_iota(jnp.int32, (T, LANES), 0)
        lam = lam_ref[...]
        ls = _log_sigmoid(lam)
        cb, br, bi = cb_ref[...], br_ref[...], bi_ref[...]
        wr, wi = wr_ref[...], wi_ref[...]
        for ref in (dcw_ref, dcb_ref, dbr_ref, dbi_ref, dlam_ref, dwr_ref, dwi_ref):
            ref[...] = jnp.zeros_like(ref)

        def chunk(it, carry):
            g_next, dxc_next = carry
            ci = n_chunk - 1 - it
            rows = pl.ds(pl.multiple_of(ci * T, T), T)
            before = pl.ds(pl.multiple_of(jnp.maximum(ci - 1, 0) * T, T), T)
            first = ci == 0
            xb = xb_ref[rows, :]
            prev_xb = jnp.where(first, 0.0, xb_ref[before, :])
            xc, xcb, r, i, a, m = _lru_gates(xb, prev_xb, row, cw_ref, cb, wr, br, wi, bi, ls)
            h = hs_ref[rows, :]
            h_prev = _shift_down(h, jnp.where(first, 0.0, hs_ref[before, :]), row, 1)
            gp = gp_ref[rows, :]
            dyv = dy_ref[rows, :]
            du_ref[0, rows, :] = (dyv * h * _gelu_grad(gp)).astype(BF16)
            dh = dyv * _gelu(gp)
            ca, cbv = _scan_chunk(a, a * dh, row, T, reverse=True)
            gp_acc = ca * g_next + cbv
            g = dh + jnp.where(row < T - 1, pltpu.roll(gp_acc, T - 1, 0), g_next)
            da = g * h_prev - (g * (i * xc)) * a / m
            dla = da * a
            dlam_ref[...] += jnp.sum(dla * (LRU_C * r), axis=0, keepdims=True)
            dpr = (dla * (LRU_C * ls)) * r * (1.0 - r)
            dpi = (g * m * xc) * i * (1.0 - i)
            dbr_ref[...] += jnp.sum(dpr, axis=0, keepdims=True)
            dbi_ref[...] += jnp.sum(dpi, axis=0, keepdims=True)
            dprb, dpib = dpr.astype(BF16), dpi.astype(BF16)
            dwr_ref[...] += _dot_tn(xcb, dprb)
            dwi_ref[...] += _dot_tn(xcb, dpib)
            dxc = g * m * i + _dot_nt(dprb, wr) + _dot_nt(dpib, wi)
            dcb_ref[...] += jnp.sum(dxc, axis=0, keepdims=True)
            dxb = jnp.zeros((T, LANES), F32)
            for k in range(CONV_WIDTH):
                tap = pl.ds(CONV_WIDTH - 1 - k, 1)
                dcw_ref[tap, :] += jnp.sum(dxc * _shift_down(xb, prev_xb, row, k), axis=0,
                                           keepdims=True)
                dxb = dxb + cw_ref[tap, :] * _shift_up(dxc, dxc_next, row, k, T)
            du_ref[1, rows, :] = dxb.astype(BF16)
            return _row_of(gp_acc, row, 0), dxc

        lax.fori_loop(0, n_chunk, chunk,
                      (jnp.zeros((1, LANES), F32), jnp.zeros((T, LANES), F32)))
        dlam_ref[...] = dlam_ref[...] * _sigmoid(-lam)

    vec_out = jax.ShapeDtypeStruct((1, D_MODEL), F32)
    mat_out = jax.ShapeDtypeStruct((N_CBLK, LANES, LANES), F32)
    return pl.pallas_call(
        body, name=name, grid=(N_CBLK,),
        in_specs=[col(0), col(0), col(N_CBLK), col(0), cwm, vec, mat, vec, mat, vec, vec],
        out_specs=[pl.BlockSpec((2, S, LANES), lambda j: (0, 0, j)), cwm, vec, vec, vec, vec,
                   mat, mat],
        out_shape=[jax.ShapeDtypeStruct((2, S, D_MODEL), BF16),
                   jax.ShapeDtypeStruct((CONV_WIDTH, D_MODEL), F32),
                   vec_out, vec_out, vec_out, vec_out, mat_out, mat_out],
        compiler_params=_params("parallel"),
    )(dy, u, u, hs, conv_w, conv_b, wr, br, wi, bi, lam)


def _head_mean_matrix():
    r = lax.broadcasted_iota(jnp.int32, (LANES, LANES), 0) // HEAD_DIM
    c = lax.broadcasted_iota(jnp.int32, (LANES, LANES), 1) // HEAD_DIM
    return jnp.where(r == c, 1.0 / HEAD_DIM, 0.0).astype(F32)


def _head_mean(x, p):
    return jnp.dot(x, p, preferred_element_type=F32, precision=lax.Precision.HIGHEST)


def _qk_prep(u, q_gain, k_gain, name, tm=512):
    S = u.shape[0]
    tm = _token_tile(S, tm)

    def body(q_ref, k_ref, v_ref, qg_ref, kg_ref, qn_ref, kn_ref, vb_ref):
        p = _head_mean_matrix()
        for x_ref, g_ref, o_ref in ((q_ref, qg_ref, qn_ref), (k_ref, kg_ref, kn_ref)):
            xv = x_ref[...]
            rs = lax.rsqrt(_head_mean(xv * xv, p) + EPS)
            o_ref[...] = (xv * rs * g_ref[...]).astype(BF16)
        vb_ref[...] = v_ref[...].astype(BF16)

    blk = lambda off: pl.BlockSpec((tm, LANES), lambda i, j: (i, j + off))
    out = jax.ShapeDtypeStruct((S, D_MODEL), BF16)
    return pl.pallas_call(
        body, name=name, grid=(S // tm, N_CBLK),
        in_specs=[blk(0), blk(N_CBLK), blk(2 * N_CBLK), _resident((1, LANES)),
                  _resident((1, LANES))],
        out_specs=[blk(0), blk(0), blk(0)],
        out_shape=[out, out, out],
        compiler_params=_params("parallel", "parallel"),
    )(u, u, u, q_gain, k_gain)


def _qk_bwd(u, dqn, dkn, q_gain, k_gain, name, tm=512):
    S = u.shape[0]
    tm = _token_tile(S, tm)

    def body(q_ref, k_ref, dqn_ref, dkn_ref, qg_ref, kg_ref, dq_ref, dk_ref, dqg_ref, dkg_ref):
        p = _head_mean_matrix()
        first = (pl.program_id(0) == 0) & (pl.program_id(1) == 0)
        last = (pl.program_id(0) == S // tm - 1) & (pl.program_id(1) == N_CBLK - 1)
        for x_ref, dn_ref, g_ref, dx_ref, dg_ref in ((q_ref, dqn_ref, qg_ref, dq_ref, dqg_ref),
                                                     (k_ref, dkn_ref, kg_ref, dk_ref, dkg_ref)):
            xv, dn = x_ref[...], dn_ref[...]
            rs = lax.rsqrt(_head_mean(xv * xv, p) + EPS)
            xhat = xv * rs
            dxhat = dn * g_ref[...]
            dx_ref[...] = (rs * (dxhat - xhat * _head_mean(dxhat * xhat, p))).astype(BF16)

            @pl.when(first)
            def _():
                dg_ref[...] = jnp.zeros_like(dg_ref)
            dg_ref[...] += jnp.sum(dn * xhat, axis=0, keepdims=True)

            @pl.when(last)
            def _():
                dg_ref[...] += pltpu.roll(dg_ref[...], HEAD_DIM, 1)

    blk = lambda off: pl.BlockSpec((tm, LANES), lambda i, j: (i, j + off))
    acc = pl.BlockSpec((1, LANES), lambda i, j: (0, 0))
    out = jax.ShapeDtypeStruct((S, D_MODEL), BF16)
    vec = jax.ShapeDtypeStruct((1, LANES), F32)
    return pl.pallas_call(
        body, name=name, grid=(S // tm, N_CBLK),
        in_specs=[blk(0), blk(N_CBLK), blk(0), blk(0), _resident((1, LANES)),
                  _resident((1, LANES))],
        out_specs=[blk(0), blk(0), acc, acc],
        out_shape=[out, out, vec, vec],
        compiler_params=_params("arbitrary", "arbitrary"),
    )(u, u, dqn, dkn, q_gain, k_gain)


def _forget_fwd(f, b_f, name):
    S = f.shape[0]
    T = _token_tile(S, 256)

    def body(f_ref, b_ref, c_ref):
        row = lax.broadcasted_iota(jnp.int32, (T, LANES), 0)
        ones = jnp.ones((T, LANES), F32)
        bias = b_ref[...]

        def chunk(ci, carry):
            rows = pl.ds(pl.multiple_of(ci * T, T), T)
            _, c = _scan_chunk(ones, _log_sigmoid(f_ref[rows, :] + bias), row, T, reverse=False)
            c = c + carry
            c_ref[rows, :] = c
            return _row_of(c, row, T - 1)

        lax.fori_loop(0, S // T, chunk, jnp.zeros((1, LANES), F32))

    return pl.pallas_call(
        body, name=name,
        in_specs=[pl.BlockSpec(memory_space=pltpu.VMEM)] * 2,
        out_specs=pl.BlockSpec(memory_space=pltpu.VMEM),
        out_shape=jax.ShapeDtypeStruct((S, LANES), F32),
        compiler_params=pltpu.CompilerParams(vmem_limit_bytes=VMEM_LIMIT),
    )(f, b_f)


def _forget_bwd(dc_k, dc_q, f, b_f, name):
    S = f.shape[0]
    T = _token_tile(S, 256)
    n_chunk = S // T

    def body(dck_ref, dcq_ref, f_ref, b_ref, df_ref, db_ref):
        row = lax.broadcasted_iota(jnp.int32, (T, LANES), 0)
        ones = jnp.ones((T, LANES), F32)
        bias = b_ref[...]

        def chunk(it, carry):
            tail, db = carry
            rows = pl.ds(pl.multiple_of((n_chunk - 1 - it) * T, T), T)
            _, dlf = _scan_chunk(ones, dck_ref[rows, :] + dcq_ref[rows, :], row, T, reverse=True)
            dlf = dlf + tail
            df = dlf * _sigmoid(-(f_ref[rows, :] + bias))
            df_ref[rows, :] = df
            return _row_of(dlf, row, 0), db + jnp.sum(df, axis=0, keepdims=True)

        zero = jnp.zeros((1, LANES), F32)
        _, db = lax.fori_loop(0, n_chunk, chunk, (zero, zero))
        db_ref[...] = db

    return pl.pallas_call(
        body, name=name,
        in_specs=[pl.BlockSpec(memory_space=pltpu.VMEM)] * 4,
        out_specs=[pl.BlockSpec(memory_space=pltpu.VMEM)] * 2,
        out_shape=[jax.ShapeDtypeStruct((S, LANES), F32), jax.ShapeDtypeStruct((1, LANES), F32)],
        compiler_params=pltpu.CompilerParams(vmem_limit_bytes=VMEM_LIMIT),
    )(dc_k, dc_q, f, b_f)


def _lane_column(x, lane, idx):
    return jnp.sum(jnp.where(lane == idx, x, 0.0), axis=1, keepdims=True)


def _attn_tiles(S):
    t = _token_tile(S, 256)
    return t, S // t


def _attn_fwd(qn, kn, vb, c_col, c_row, name):
    S = qn.shape[0]
    T, n_t = _attn_tiles(S)

    def body(q_ref, k_ref, v_ref, cc_ref, cr_ref, o_ref, lse_ref):
        hp, qi = pl.program_id(0), pl.program_id(1)
        lane = lax.broadcasted_iota(jnp.int32, (T, LANES), 1)
        causal = (lax.broadcasted_iota(jnp.int32, (T, T), 1)
                  <= lax.broadcasted_iota(jnp.int32, (T, T), 0))
        cblk = cc_ref[...]
        for h2 in range(2):
            hl = slice(h2 * HEAD_DIM, (h2 + 1) * HEAD_DIM)
            cq = _lane_column(cblk, lane, 2 * hp + h2)
            qh = q_ref[:, hl]

            def step(kj, carry, masked):
                m, l, acc = carry
                ks = pl.ds(pl.multiple_of(kj * T, T), T)
                s = _dot_nt(qh, k_ref[ks, hl]) * ATTN_SCALE + cq - cr_ref[h2:h2 + 1, ks]
                if masked:
                    s = jnp.where(causal, s, NEG_INF)
                m_new = jnp.maximum(m, jnp.max(s, axis=1, keepdims=True))
                alpha = jnp.exp(m - m_new)
                p = jnp.exp(s - m_new)
                l = alpha * l + jnp.sum(p, axis=1, keepdims=True)
                acc = alpha * acc + _dot(p.astype(BF16), v_ref[ks, hl])
                return m_new, l, acc

            init = (jnp.full((T, 1), NEG_INF, F32), jnp.zeros((T, 1), F32),
                    jnp.zeros((T, HEAD_DIM), F32))
            carry = lax.fori_loop(0, qi, lambda kj, c: step(kj, c, False), init)
            m, l, acc = step(qi, carry, True)
            o_ref[:, hl] = (acc / l).astype(BF16)
            lse_ref[:, hl] = jnp.broadcast_to(m + jnp.log(l), (T, HEAD_DIM))

    qblk = pl.BlockSpec((T, LANES), lambda h, i: (i, h))
    kv = pl.BlockSpec((S, LANES), lambda h, i: (0, h))
    return pl.pallas_call(
        body, name=name, grid=(N_CBLK, n_t),
        in_specs=[qblk, kv, kv, pl.BlockSpec((T, LANES), lambda h, i: (i, 0)),
                  pl.BlockSpec((None, 2, S), lambda h, i: (h, 0, 0))],
        out_specs=[qblk, qblk],
        out_shape=[jax.ShapeDtypeStruct((S, D_MODEL), BF16),
                   jax.ShapeDtypeStruct((S, D_MODEL), F32)],
        compiler_params=_params("parallel", "parallel"),
    )(qn, kn, vb, c_col, c_row)


def _attn_bwd_q(qn, kn, vb, do, o, lse, c_col, c_row, name):
    S = qn.shape[0]
    T, n_t = _attn_tiles(S)

    def body(q_ref, k_ref, v_ref, do_ref, o_ref, lse_ref, cc_ref, cr_ref, dq_ref, dd_ref, rho_ref):
        hp, qi = pl.program_id(0), pl.program_id(1)
        lane = lax.broadcasted_iota(jnp.int32, (T, LANES), 1)
        causal = (lax.broadcasted_iota(jnp.int32, (T, T), 1)
                  <= lax.broadcasted_iota(jnp.int32, (T, T), 0))
        cblk = cc_ref[...]
        for h2 in range(2):
            hl = slice(h2 * HEAD_DIM, (h2 + 1) * HEAD_DIM)
            cq = _lane_column(cblk, lane, 2 * hp + h2)
            qh, doh = q_ref[:, hl], do_ref[:, hl]
            lse_h = lse_ref[:, h2 * HEAD_DIM:h2 * HEAD_DIM + 1]
            dd = jnp.sum(doh.astype(F32) * o_ref[:, hl].astype(F32), axis=1, keepdims=True)
            dd_ref[:, hl] = jnp.broadcast_to(dd, (T, HEAD_DIM))

            def step(kj, carry, masked):
                dq, rho = carry
                ks = pl.ds(pl.multiple_of(kj * T, T), T)
                kh = k_ref[ks, hl]
                s = _dot_nt(qh, kh) * ATTN_SCALE + cq - cr_ref[h2:h2 + 1, ks]
                if masked:
                    s = jnp.where(causal, s, NEG_INF)
                p = jnp.exp(s - lse_h)
                ds = p * (_dot_nt(doh, v_ref[ks, hl]) - dd)
                return (dq + _dot(ds.astype(BF16), kh),
                        rho + jnp.sum(ds, axis=1, keepdims=True))

            carry = lax.fori_loop(0, qi, lambda kj, c: step(kj, c, False),
                                  (jnp.zeros((T, HEAD_DIM), F32), jnp.zeros((T, 1), F32)))
            dq, rho = step(qi, carry, True)
            dq_ref[:, hl] = dq * ATTN_SCALE
            rho_ref[:, hl] = jnp.broadcast_to(rho, (T, HEAD_DIM))

    qblk = pl.BlockSpec((T, LANES), lambda h, i: (i, h))
    kv = pl.BlockSpec((S, LANES), lambda h, i: (0, h))
    out = jax.ShapeDtypeStruct((S, D_MODEL), F32)
    return pl.pallas_call(
        body, name=name, grid=(N_CBLK, n_t),
        in_specs=[qblk, kv, kv, qblk, qblk, qblk, pl.BlockSpec((T, LANES), lambda h, i: (i, 0)),
                  pl.BlockSpec((None, 2, S), lambda h, i: (h, 0, 0))],
        out_specs=[qblk, qblk, qblk],
        out_shape=[out, out, out],
        compiler_params=_params("parallel", "parallel"),
    )(qn, kn, vb, do, o, lse, c_col, c_row)


def _attn_bwd_kv(qn, kn, vb, do, lse, dd, c_col, c_row, name):
    S = qn.shape[0]
    T, n_t = _attn_tiles(S)

    def body(q_ref, k_ref, v_ref, do_ref, lse_ref, dd_ref, cc_ref, cr_ref, dk_ref, dv_ref, dc_ref):
        hp, kj = pl.program_id(0), pl.program_id(1)
        lane = lax.broadcasted_iota(jnp.int32, (T, LANES), 1)
        causal = (lax.broadcasted_iota(jnp.int32, (T, T), 1)
                  <= lax.broadcasted_iota(jnp.int32, (T, T), 0))
        for h2 in range(2):
            hl = slice(h2 * HEAD_DIM, (h2 + 1) * HEAD_DIM)
            one = slice(h2 * HEAD_DIM, h2 * HEAD_DIM + 1)
            kh, vh = k_ref[:, hl], v_ref[:, hl]
            ck = cr_ref[h2:h2 + 1, :]

            def step(qi, carry, masked):
                dk, dv, dc = carry
                qs = pl.ds(pl.multiple_of(qi * T, T), T)
                qh, doh = q_ref[qs, hl], do_ref[qs, hl]
                cq = _lane_column(cc_ref[qs, :], lane, 2 * hp + h2)
                s = _dot_nt(qh, kh) * ATTN_SCALE + cq - ck
                if masked:
                    s = jnp.where(causal, s, NEG_INF)
                p = jnp.exp(s - lse_ref[qs, one])
                ds = p * (_dot_nt(doh, vh) - dd_ref[qs, one])
                return (dk + _dot_tn(ds.astype(BF16), qh), dv + _dot_tn(p.astype(BF16), doh),
                        dc - jnp.sum(ds, axis=0, keepdims=True))

            init = (jnp.zeros((T, HEAD_DIM), F32), jnp.zeros((T, HEAD_DIM), F32),
                    jnp.zeros((1, T), F32))
            carry = step(kj, init, True)
            dk, dv, dc = lax.fori_loop(kj + 1, n_t, lambda qi, c: step(qi, c, False), carry)
            dk_ref[:, hl] = dk * ATTN_SCALE
            dv_ref[:, hl] = dv.astype(BF16)
            dc_ref[h2:h2 + 1, :] = dc

    kblk = pl.BlockSpec((T, LANES), lambda h, j: (j, h))
    full = pl.BlockSpec((S, LANES), lambda h, j: (0, h))
    crow = pl.BlockSpec((None, 2, T), lambda h, j: (h, 0, j))
    return pl.pallas_call(
        body, name=name, grid=(N_CBLK, n_t),
        in_specs=[full, kblk, kblk, full, full, full,
                  pl.BlockSpec((S, LANES), lambda h, j: (0, 0)), crow],
        out_specs=[kblk, kblk, crow],
        out_shape=[jax.ShapeDtypeStruct((S, D_MODEL), F32),
                   jax.ShapeDtypeStruct((S, D_MODEL), BF16),
                   jax.ShapeDtypeStruct((N_CBLK, 2, S), F32)],
        compiler_params=_params("parallel", "parallel"),
    )(qn, kn, vb, do, lse, dd, c_col, c_row)


def _loss_head(y, target, name, tm=512):
    S, D = y.shape
    tm = _token_tile(S, tm)

    def body(y_ref, t_ref, loss_ref, dy_ref):
        err = y_ref[...] - t_ref[...]
        dy_ref[...] = err / D

        @pl.when(pl.program_id(0) == 0)
        def _():
            loss_ref[...] = jnp.zeros_like(loss_ref)
        row_loss = jnp.mean(err * err, axis=1, keepdims=True)
        loss_ref[...] += 0.5 * jnp.sum(row_loss, axis=0, keepdims=True)

    tok = pl.BlockSpec((tm, D), lambda i: (i, 0))
    return pl.pallas_call(
        body, name=name, grid=(S // tm,),
        in_specs=[tok, tok],
        out_specs=[pl.BlockSpec((1, 1), lambda i: (0, 0)), tok],
        out_shape=[jax.ShapeDtypeStruct((1, 1), F32), jax.ShapeDtypeStruct((S, D), F32)],
        compiler_params=_params("arbitrary"),
    )(y, target)


def _exchange(arrays, gathers, name):
    n = len(arrays)
    n_peer = N_DEV - 1

    def body(*refs):
        ins, outs = refs[:n], refs[n:2 * n]
        send_sems, recv_sems, own_sems = refs[2 * n:]
        x, y, c = lax.axis_index("x"), lax.axis_index("y"), lax.axis_index("c")
        me = 4 * x + 2 * y + c
        own = []
        for a in range(n):
            src = ins[a] if gathers[a] else ins[a].at[me]
            own.append(pltpu.make_async_copy(src, outs[a].at[me], own_sems.at[a]))
            own[-1].start()
        sent = []
        for k in range(1, N_DEV):
            px = 1 - x if k & 4 else x
            py = 1 - y if k & 2 else y
            pc = 1 - c if k & 1 else c
            peer = 4 * px + 2 * py + pc
            for a in range(n):
                sem = a * n_peer + k - 1
                src = ins[a] if gathers[a] else ins[a].at[peer]

                def copy(dst_slot, src=src, a=a, sem=sem, to=(px, py, pc)):
                    return pltpu.make_async_remote_copy(
                        src_ref=src, dst_ref=outs[a].at[dst_slot], send_sem=send_sems.at[sem],
                        recv_sem=recv_sems.at[sem], device_id=to,
                        device_id_type=pl.DeviceIdType.MESH)

                copy(me).start()
                sent.append((copy, me, peer))
        for copy, me_slot, peer_slot in sent:
            copy(peer_slot).wait_recv()
            copy(me_slot).wait_send()
        for cp in own:
            cp.wait()

    out_shape = [jax.ShapeDtypeStruct((N_DEV,) + a.shape if g else a.shape, a.dtype)
                 for a, g in zip(arrays, gathers)]
    hbm = pl.BlockSpec(memory_space=pl.ANY)
    return pl.pallas_call(
        body, name=name,
        in_specs=[hbm] * n, out_specs=[hbm] * n, out_shape=out_shape,
        scratch_shapes=[pltpu.SemaphoreType.DMA((n * n_peer,)),
                        pltpu.SemaphoreType.DMA((n * n_peer,)),
                        pltpu.SemaphoreType.DMA((n,))],
        compiler_params=pltpu.CompilerParams(has_side_effects=True),
    )(*arrays)


def _reduce_adamw(parts, w, m, v, name):
    n, R, C = parts.shape
    tr = 256 if R % 256 == 0 else R

    def body(p_ref, w_ref, m_ref, v_ref, g_ref, d_ref, nm_ref, nv_ref):
        g = p_ref[0].astype(F32)
        for s in range(1, n):
            g = g + p_ref[s].astype(F32)
        g_ref[...] = g
        m_new = ADAM_B1 * m_ref[...] + (1.0 - ADAM_B1) * g
        v_new = ADAM_B2 * v_ref[...] + (1.0 - ADAM_B2) * (g * g)
        nm_ref[...] = m_new
        nv_ref[...] = v_new
        m_hat = m_new / (1.0 - ADAM_B1 ** ADAM_STEP)
        v_hat = v_new / (1.0 - ADAM_B2 ** ADAM_STEP)
        d_ref[...] = -ADAM_LR * (m_hat / (jnp.sqrt(v_hat) + ADAM_EPS) + ADAM_WD * w_ref[...])

    blk = pl.BlockSpec((tr, C), lambda i: (i, 0))
    out = jax.ShapeDtypeStruct((R, C), F32)
    return pl.pallas_call(
        body, name=name, grid=(R // tr,),
        in_specs=[pl.BlockSpec((n, tr, C), lambda i: (0, i, 0)), blk, blk, blk],
        out_specs=[blk] * 4, out_shape=[out] * 4,
        compiler_params=_params("parallel"),
    )(parts, w, m, v)


def _pack(arrays):
    flat = jnp.concatenate([a.reshape(-1).astype(F32) for a in arrays])
    pad = (-flat.shape[0]) % (8 * LANES)
    return jnp.pad(flat, (0, pad)).reshape(-1, LANES)


def _unpack(buf, shapes):
    flat = buf.reshape(-1)
    out, off = [], 0
    for shp in shapes:
        size = 1
        for s in shp:
            size *= s
        out.append(flat[off:off + size].reshape(shp))
        off += size
    return out


def _block_diag_pairs(w):
    w = w.reshape(N_CBLK, 2, LRU_BLOCK_DIM, LRU_BLOCK_DIM)
    z = jnp.zeros_like(w[:, 0])
    top = jnp.concatenate([w[:, 0], z], axis=2)
    bot = jnp.concatenate([z, w[:, 1]], axis=2)
    return jnp.concatenate([top, bot], axis=1)


def _diag_pairs(m):
    h = LRU_BLOCK_DIM
    return jnp.stack([m[:, :h, :h], m[:, h:, h:]], axis=1).reshape(2 * N_CBLK, h, h)


SMALL = ("mix_norm", "mlp_norm", "lru_conv_b", "lru_w_r", "lru_b_r", "lru_w_i", "lru_b_i",
         "lru_lambda", "fox_b_f", "fox_q_gain", "fox_k_gain")
WEIGHTS = ("mix_norm", "mlp_norm", "mlp_w1", "mlp_w2", "lru_w_in", "lru_conv_w", "lru_conv_b",
           "lru_w_r", "lru_b_r", "lru_w_i", "lru_b_i", "lru_lambda", "lru_w_out", "fox_w_in",
           "fox_b_f", "fox_q_gain", "fox_k_gain", "fox_w_out")


def kernel(x, mix_norm, mlp_norm, mlp_w1, mlp_w2, lru_w_in, lru_conv_w, lru_conv_b, lru_w_r, lru_b_r, lru_w_i, lru_b_i, lru_lambda, lru_w_out, fox_w_in, fox_b_f, fox_q_gain, fox_k_gain, fox_w_out, loss_target, m_mix_norm, m_mlp_norm, m_mlp_w1, m_mlp_w2, m_lru_w_in, m_lru_conv_w, m_lru_conv_b, m_lru_w_r, m_lru_b_r, m_lru_w_i, m_lru_b_i, m_lru_lambda, m_lru_w_out, m_fox_w_in, m_fox_b_f, m_fox_q_gain, m_fox_k_gain, m_fox_w_out, v_mix_norm, v_mlp_norm, v_mlp_w1, v_mlp_w2, v_lru_w_in, v_lru_conv_w, v_lru_conv_b, v_lru_w_r, v_lru_b_r, v_lru_w_i, v_lru_b_i, v_lru_lambda, v_lru_w_out, v_fox_w_in, v_fox_b_f, v_fox_q_gain, v_fox_k_gain, v_fox_w_out):
    w_in = dict(mix_norm=mix_norm, mlp_norm=mlp_norm, mlp_w1=mlp_w1, mlp_w2=mlp_w2,
                lru_w_in=lru_w_in, lru_conv_w=lru_conv_w, lru_conv_b=lru_conv_b, lru_w_r=lru_w_r,
                lru_b_r=lru_b_r, lru_w_i=lru_w_i, lru_b_i=lru_b_i, lru_lambda=lru_lambda,
                lru_w_out=lru_w_out, fox_w_in=fox_w_in, fox_b_f=fox_b_f, fox_q_gain=fox_q_gain,
                fox_k_gain=fox_k_gain, fox_w_out=fox_w_out)
    m_in = dict(mix_norm=m_mix_norm, mlp_norm=m_mlp_norm, mlp_w1=m_mlp_w1, mlp_w2=m_mlp_w2,
                lru_w_in=m_lru_w_in, lru_conv_w=m_lru_conv_w, lru_conv_b=m_lru_conv_b,
                lru_w_r=m_lru_w_r, lru_b_r=m_lru_b_r, lru_w_i=m_lru_w_i, lru_b_i=m_lru_b_i,
                lru_lambda=m_lru_lambda, lru_w_out=m_lru_w_out, fox_w_in=m_fox_w_in,
                fox_b_f=m_fox_b_f, fox_q_gain=m_fox_q_gain, fox_k_gain=m_fox_k_gain,
                fox_w_out=m_fox_w_out)
    v_in = dict(mix_norm=v_mix_norm, mlp_norm=v_mlp_norm, mlp_w1=v_mlp_w1, mlp_w2=v_mlp_w2,
                lru_w_in=v_lru_w_in, lru_conv_w=v_lru_conv_w, lru_conv_b=v_lru_conv_b,
                lru_w_r=v_lru_w_r, lru_b_r=v_lru_b_r, lru_w_i=v_lru_w_i, lru_b_i=v_lru_b_i,
                lru_lambda=v_lru_lambda, lru_w_out=v_lru_w_out, fox_w_in=v_fox_w_in,
                fox_b_f=v_fox_b_f, fox_q_gain=v_fox_q_gain, fox_k_gain=v_fox_k_gain,
                fox_w_out=v_fox_w_out)
    D = D_MODEL
    S = x.shape[1]
    x0, target = x[0], loss_target[0]
    me = 4 * lax.axis_index("x") + 2 * lax.axis_index("y") + lax.axis_index("c")

    (w1g0, w1g1, w2g0, w2g1, lru_in_g, lru_out_g, fox_in_g, fox_out_g, conv_g) = _exchange(
        [mlp_w1[0].astype(BF16), mlp_w1[1].astype(BF16), mlp_w2[0].astype(BF16),
         mlp_w2[1].astype(BF16), lru_w_in[0].astype(BF16), lru_w_out[0].astype(BF16),
         fox_w_in[0].astype(BF16), fox_w_out[0].astype(BF16), lru_conv_w[0]],
        [True] * 9, "gather_weights")
    lru_out_w = lru_out_g.reshape(D, D)
    fox_out_w = fox_out_g.reshape(D, D)
    conv_w = conv_g.transpose(1, 0, 2).reshape(CONV_WIDTH, D)
    fox_full = fox_in_g.transpose(1, 0, 2).reshape(D, 3 * D + N_HEADS)
    wqkv = fox_full[:, :3 * D].reshape(D, 3, D).transpose(1, 0, 2)
    wf = jnp.pad(fox_full[:, 3 * D:], ((0, 0), (0, LANES - N_HEADS)))[None]
    wr = _block_diag_pairs(lru_w_r[0]).astype(BF16)
    wi = _block_diag_pairs(lru_w_i[0]).astype(BF16)
    b_r, b_i = lru_b_r.reshape(1, D), lru_b_i.reshape(1, D)
    q_gain, k_gain = jnp.tile(fox_q_gain, (1, 2)), jnp.tile(fox_k_gain, (1, 2))
    b_f = jnp.pad(fox_b_f, ((0, 0), (0, LANES - N_HEADS)))
    g_mix0, g_mix1 = mix_norm[0:1], mix_norm[1:2]
    g_mlp0, g_mlp1 = mlp_norm[0:1], mlp_norm[1:2]

    (u0,), h0 = _norm_matmul(x0, g_mix0, [lru_in_g], "lru_in_proj")
    y_lru, hs = _lru_fwd(u0, conv_w, lru_conv_b, wr, b_r, wi, b_i, lru_lambda, "lru_core")
    x1 = _matmul_res(y_lru, lru_out_w, x0, "lru_out_proj")
    x2, h1, r1 = _mlp_fwd(x1, g_mlp0, w1g0, w2g0, "mlp0")
    (u_qkv, f), h2 = _norm_matmul(x2, g_mix1, [wqkv, wf], "fox_in_proj")
    qn, kn, vb = _qk_prep(u_qkv, q_gain, k_gain, "fox_qk_norm")
    c_col = _forget_fwd(f, b_f, "fox_forget")
    c_row = c_col[:, :N_HEADS].T.reshape(N_CBLK, 2, S)
    o, lse = _attn_fwd(qn, kn, vb, c_col, c_row, "fox_attn")
    x3 = _matmul_res(o, fox_out_w, x2, "fox_out_proj")
    x4, h3, r3 = _mlp_fwd(x3, g_mlp1, w1g1, w2g1, "mlp1")
    loss_local, dx4 = _loss_head(x4, target, "loss_head")

    dx3, dg_mlp1, da3 = _mlp_bwd(dx4, x3, g_mlp1, r3, w1g1, w2g1, "mlp1_bwd")
    dw1_1 = _matmul_tn(h3, da3, N_DEV, "b", "mlp1_dw1")
    dw2_1 = _matmul_tn(r3, dx4, N_DEV, "a", "mlp1_dw2", a_square=True)
    do = _matmul_nt(dx3, fox_out_w, "fox_out_bwd", BF16)
    d_fox_out = _matmul_tn(o, dx3, N_DEV, "a", "fox_out_dw")
    dqn, dd, rho = _attn_bwd_q(qn, kn, vb, do, o, lse, c_col, c_row, "fox_attn_dq")
    dkn, dv, dc_row = _attn_bwd_kv(qn, kn, vb, do, lse, dd, c_col, c_row, "fox_attn_dkv")
    duq, duk, dq_gain, dk_gain = _qk_bwd(u_qkv, dqn, dkn, q_gain, k_gain, "fox_qk_norm_bwd")
    head_pad = ((0, 0), (0, LANES - N_HEADS))
    dc_k = jnp.pad(dc_row.reshape(N_HEADS, S).T, head_pad)
    dc_q = jnp.pad(rho[:, ::HEAD_DIM], head_pad)
    df, db_f = _forget_bwd(dc_k, dc_q, f, b_f, "fox_forget_bwd")
    dx2, dg_mix1 = _proj_bwd([duq, duk, dv, df], [wqkv[0:1], wqkv[1:2], wqkv[2:3], wf],
                             x2, g_mix1, dx3, "fox_in_bwd")
    d_fox_in = jnp.concatenate(
        [_matmul_tn(h2, duq, 1, "b", "fox_in_dwq")[0], _matmul_tn(h2, duk, 1, "b", "fox_in_dwk")[0],
         _matmul_tn(h2, dv, 1, "b", "fox_in_dwv")[0],
         _matmul_tn(h2, df, 1, "b", "fox_in_dwf")[0][:, :N_HEADS]], axis=1)
    d_fox_in = d_fox_in.reshape(D, N_DEV, -1).transpose(1, 0, 2)
    dx1, dg_mlp0, da1 = _mlp_bwd(dx2, x1, g_mlp0, r1, w1g0, w2g0, "mlp0_bwd")
    dw1_0 = _matmul_tn(h1, da1, N_DEV, "b", "mlp0_dw1")
    dw2_0 = _matmul_tn(r1, dx2, N_DEV, "a", "mlp0_dw2", a_square=True)
    dy_lru = _matmul_nt(dx1, lru_out_w, "lru_out_bwd", F32)
    d_lru_out = _matmul_tn(y_lru, dx1, N_DEV, "a", "lru_out_dw")
    du0, d_conv_w, d_conv_b, d_b_r, d_b_i, d_lam, d_wr, d_wi = _lru_bwd(
        dy_lru, u0, hs, conv_w, lru_conv_b, wr, b_r, wi, b_i, lru_lambda, "lru_core_bwd")
    dx0, dg_mix0 = _proj_bwd([du0[0], du0[1]], [lru_in_g[:4], lru_in_g[4:]], x0, g_mix0, dx1,
                             "lru_in_bwd")
    d_lru_in = jnp.concatenate([_matmul_tn(h0, du0[0], 4, "b", "lru_in_dw_gate"),
                                _matmul_tn(h0, du0[1], 4, "b", "lru_in_dw_x")], axis=0)

    small_grads = dict(
        mix_norm=jnp.concatenate([dg_mix0, dg_mix1], axis=0),
        mlp_norm=jnp.concatenate([dg_mlp0, dg_mlp1], axis=0),
        lru_conv_b=d_conv_b, lru_w_r=_diag_pairs(d_wr), lru_b_r=d_b_r, lru_w_i=_diag_pairs(d_wi),
        lru_b_i=d_b_i, lru_lambda=d_lam, fox_b_f=db_f[:, :N_HEADS],
        fox_q_gain=dq_gain[:, :HEAD_DIM], fox_k_gain=dk_gain[:, :HEAD_DIM])
    small_partial = _pack([small_grads[n] for n in SMALL] + [d_conv_w])
    big = [dw1_0, dw1_1, dw2_0, dw2_1, d_lru_in, d_lru_out, d_fox_in, d_fox_out]
    got = _exchange([b.astype(BF16) for b in big] + [small_partial], [False] * 8 + [True],
                    "exchange_grads")
    (p_w1_0, p_w1_1, p_w2_0, p_w2_1, p_lru_in, p_lru_out, p_fox_in, p_fox_out, p_small) = got

    grads, deltas, new_m, new_v = {}, {}, {}, {}

    def update(name, parts, sel=None):
        w, m, v = w_in[name], m_in[name], v_in[name]
        if sel is not None:
            w, m, v = w[sel], m[sel], v[sel]
        shape = w.shape
        two_d = (-1, shape[-1])
        res = _reduce_adamw(parts.reshape((N_DEV,) + w.reshape(two_d).shape), w.reshape(two_d),
                            m.reshape(two_d), v.reshape(two_d),
                            "adamw_" + name + ("" if sel is None else "_%d" % sel))
        return [r.reshape(shape) for r in res]

    def store(name, res):
        grads[name], deltas[name], new_m[name], new_v[name] = res

    store("mlp_w1", [jnp.stack(p) for p in zip(update("mlp_w1", p_w1_0, 0),
                                               update("mlp_w1", p_w1_1, 1))])
    store("mlp_w2", [jnp.stack(p) for p in zip(update("mlp_w2", p_w2_0, 0),
                                               update("mlp_w2", p_w2_1, 1))])
    store("lru_w_in", update("lru_w_in", p_lru_in))
    store("lru_w_out", update("lru_w_out", p_lru_out))
    store("fox_w_in", update("fox_w_in", p_fox_in))
    store("fox_w_out", update("fox_w_out", p_fox_out))

    small_shapes = [w_in[n].shape for n in SMALL]
    n_small = sum(math.prod(s) for s in small_shapes)
    res_small = _reduce_adamw(p_small, _pack([w_in[n] for n in SMALL] + [jnp.zeros((CONV_WIDTH, D))]),
                              _pack([m_in[n] for n in SMALL] + [jnp.zeros((CONV_WIDTH, D))]),
                              _pack([v_in[n] for n in SMALL] + [jnp.zeros((CONV_WIDTH, D))]),
                              "adamw_small")
    for name, *vals in zip(SMALL, *[_unpack(r, small_shapes) for r in res_small]):
        store(name, vals)
    conv_parts = p_small.reshape(N_DEV, -1)[:, n_small:n_small + CONV_WIDTH * D]
    conv_parts = conv_parts.reshape(N_DEV, CONV_WIDTH, N_DEV, LANES)
    conv_parts = lax.dynamic_index_in_dim(conv_parts, me, axis=2, keepdims=False)
    store("lru_conv_w", update("lru_conv_w", conv_parts))

    loss = lax.psum(loss_local[0, 0], ("x", "y", "c"))
    return (loss, dx0[None], *[grads[n] for n in WEIGHTS], *[deltas[n] for n in WEIGHTS],
            *[new_m[n] for n in WEIGHTS], *[new_v[n] for n in WEIGHTS])
```

```python
import math

import jax
import jax.numpy as jnp
from jax import lax
from jax.experimental import pallas as pl
from jax.experimental.pallas import tpu as pltpu

F32 = jnp.float32
BF16 = jnp.bfloat16

N_DEV = 8
D_MODEL = 1024
D_FF = 4096
N_HEADS = 16
HEAD_DIM = 64
LRU_BLOCK_DIM = 64
CONV_WIDTH = 4
LRU_C = 8.0
EPS = 1e-6
NEG_INF = -1e30
ATTN_SCALE = HEAD_DIM ** -0.5
LANES = 128
N_CBLK = D_MODEL // LANES
VMEM_LIMIT = 52 * 2 ** 20

ADAM_LR = 0.001
ADAM_B1 = 0.9
ADAM_B2 = 0.999
ADAM_EPS = 1e-08
ADAM_WD = 0.01
ADAM_STEP = 10

_NT = (((1,), (1,)), ((), ()))
_TN = (((0,), (0,)), ((), ()))


def _params(*sem):
    return pltpu.CompilerParams(dimension_semantics=sem, vmem_limit_bytes=VMEM_LIMIT)


def _resident(shape):
    zeros = (0,) * len(shape)
    return pl.BlockSpec(shape, lambda *_: zeros, pipeline_mode=pl.Buffered(1))


def _dot(a, b):
    return jnp.dot(a, b, preferred_element_type=F32)


def _dot_nt(a, b):
    return lax.dot_general(a, b, _NT, preferred_element_type=F32)


def _dot_tn(a, b):
    return lax.dot_general(a, b, _TN, preferred_element_type=F32)


def _sigmoid(x):
    return 1.0 / (1.0 + jnp.exp(-x))


def _log_sigmoid(x):
    return -(jnp.maximum(-x, 0.0) + jnp.log1p(jnp.exp(-jnp.abs(x))))


def _expm1(x):
    poly = x * (1.0 + x * (0.5 + x * (1.0 / 6.0 + x * (1.0 / 24.0 + x * (1.0 / 120.0)))))
    return jnp.where(jnp.abs(x) < 0.1, poly, jnp.exp(x) - 1.0)


_GELU_K = 0.7978845608028654


def _gelu(x):
    return 0.5 * x * (1.0 + jnp.tanh(_GELU_K * (x + 0.044715 * (x * x * x))))


def _gelu_grad(x):
    t = jnp.tanh(_GELU_K * (x + 0.044715 * (x * x * x)))
    return 0.5 * (1.0 + t) + 0.5 * x * (1.0 - t * t) * (_GELU_K * (1.0 + 3 * 0.044715 * x * x))


def _rms_scale(x):
    return lax.rsqrt(jnp.mean(x * x, axis=-1, keepdims=True) + EPS)


def _norm_bwd(dh, x, g):
    rs = _rms_scale(x)
    xhat = x * rs
    dxhat = dh * g
    dx = rs * (dxhat - xhat * jnp.mean(dxhat * xhat, axis=-1, keepdims=True))
    return dx, jnp.sum(dh * xhat, axis=0, keepdims=True)


def _token_tile(S, want):
    tm = min(S, want)
    assert S % tm == 0
    return tm


def _norm_matmul(x, g, ws, name, tm=256):
    S, D = x.shape
    tm = _token_tile(S, tm)
    n = len(ws)

    def body(x_ref, g_ref, *refs):
        w_refs, o_refs, h_ref = refs[:n], refs[n:2 * n], refs[2 * n]
        xv = x_ref[...]
        h = (xv * _rms_scale(xv) * g_ref[...]).astype(BF16)
        h_ref[...] = h
        for w_ref, o_ref in zip(w_refs, o_refs):
            nb, _, nw = w_ref.shape
            for d in range(nb):
                o_ref[:, d * nw:(d + 1) * nw] = _dot(h, w_ref[d])

    widths = [w.shape[0] * w.shape[2] for w in ws]
    outs = pl.pallas_call(
        body, name=name, grid=(S // tm,),
        in_specs=[pl.BlockSpec((tm, D), lambda i: (i, 0)), _resident((1, D))]
        + [_resident(w.shape) for w in ws],
        out_specs=[pl.BlockSpec((tm, n_), lambda i: (i, 0)) for n_ in widths]
        + [pl.BlockSpec((tm, D), lambda i: (i, 0))],
        out_shape=[jax.ShapeDtypeStruct((S, n_), F32) for n_ in widths]
        + [jax.ShapeDtypeStruct((S, D), BF16)],
        compiler_params=_params("parallel"),
    )(x, g, *ws)
    return outs[:n], outs[n]


def _matmul_res(a, w, res, name, tm=512):
    S, K = a.shape
    N = w.shape[1]
    tm = _token_tile(S, tm)

    def body(a_ref, w_ref, r_ref, o_ref):
        o_ref[...] = r_ref[...] + _dot(a_ref[...], w_ref[...])

    return pl.pallas_call(
        body, name=name, grid=(S // tm,),
        in_specs=[pl.BlockSpec((tm, K), lambda i: (i, 0)), _resident((K, N)),
                  pl.BlockSpec((tm, N), lambda i: (i, 0))],
        out_specs=pl.BlockSpec((tm, N), lambda i: (i, 0)),
        out_shape=jax.ShapeDtypeStruct((S, N), F32),
        compiler_params=_params("parallel"),
    )(a, w, res)


def _matmul_nt(a, w, name, out_dtype, tm=512):
    S, N = a.shape
    K = w.shape[0]
    tm = _token_tile(S, tm)

    def body(a_ref, w_ref, o_ref):
        o_ref[...] = _dot_nt(a_ref[...].astype(BF16), w_ref[...]).astype(out_dtype)

    return pl.pallas_call(
        body, name=name, grid=(S // tm,),
        in_specs=[pl.BlockSpec((tm, N), lambda i: (i, 0)), _resident((K, N))],
        out_specs=pl.BlockSpec((tm, K), lambda i: (i, 0)),
        out_shape=jax.ShapeDtypeStruct((S, K), out_dtype),
        compiler_params=_params("parallel"),
    )(a, w)


def _proj_bwd(a_list, w_list, x, g, res, name, tm=256):
    S, D = x.shape
    tm = _token_tile(S, tm)
    n = len(a_list)

    def body(*refs):
        a_refs, w_refs = refs[:n], refs[n:2 * n]
        x_ref, g_ref, r_ref, dx_ref, dg_ref = refs[2 * n:]
        dh = jnp.zeros((tm, D), F32)
        for a_ref, w_ref in zip(a_refs, w_refs):
            nb, _, nw = w_ref.shape
            for d in range(nb):
                dh = dh + _dot_nt(a_ref[:, d * nw:(d + 1) * nw].astype(BF16), w_ref[d])
        dx, dg = _norm_bwd(dh, x_ref[...], g_ref[...])
        dx_ref[...] = r_ref[...] + dx

        @pl.when(pl.program_id(0) == 0)
        def _():
            dg_ref[...] = jnp.zeros_like(dg_ref)
        dg_ref[...] += dg

    tok = lambda width: pl.BlockSpec((tm, width), lambda i: (i, 0))
    return pl.pallas_call(
        body, name=name, grid=(S // tm,),
        in_specs=[tok(a.shape[1]) for a in a_list] + [_resident(w.shape) for w in w_list]
        + [tok(D), _resident((1, D)), tok(D)],
        out_specs=[tok(D), pl.BlockSpec((1, D), lambda i: (0, 0))],
        out_shape=[jax.ShapeDtypeStruct((S, D), F32), jax.ShapeDtypeStruct((1, D), F32)],
        compiler_params=_params("arbitrary"),
    )(*a_list, *w_list, x, g, res)


def _matmul_tn(a, b, nb, block_on, name, a_square=False, tm=512):
    S, K = a.shape
    N = b.shape[1]
    tm = _token_tile(S, tm)
    if block_on == "b":
        ka, nbk = K, N // nb
        a_spec = pl.BlockSpec((tm, K), lambda d, i: (i, 0))
        b_spec = pl.BlockSpec((tm, nbk), lambda d, i: (i, d))
    else:
        ka, nbk = K // nb, N
        a_spec = pl.BlockSpec((tm, ka), lambda d, i: (i, d))
        b_spec = pl.BlockSpec((tm, N), lambda d, i: (i, 0))

    def body(a_ref, b_ref, o_ref):
        av = a_ref[...]
        if a_square:
            av = av.astype(F32)
            av = av * av
        part = _dot_tn(av.astype(BF16), b_ref[...].astype(BF16))

        @pl.when(pl.program_id(1) == 0)
        def _():
            o_ref[...] = part

        @pl.when(pl.program_id(1) > 0)
        def _():
            o_ref[...] += part

    return pl.pallas_call(
        body, name=name, grid=(nb, S // tm),
        in_specs=[a_spec, b_spec],
        out_specs=pl.BlockSpec((None, ka, nbk), lambda d, i: (d, 0, 0)),
        out_shape=jax.ShapeDtypeStruct((nb, ka, nbk), F32),
        compiler_params=_params("parallel", "arbitrary"),
    )(a, b)


def _mlp_fwd(x, g, w1, w2, name, tm=256):
    S, D = x.shape
    nb, _, fb = w1.shape
    tm = _token_tile(S, tm)

    def body(x_ref, g_ref, w1_ref, w2_ref, o_ref, h_ref, r_ref):
        xv = x_ref[...]
        h = (xv * _rms_scale(xv) * g_ref[...]).astype(BF16)
        h_ref[...] = h
        acc = xv
        for d in range(nb):
            r = jnp.maximum(_dot(h, w1_ref[d]), 0.0)
            r_ref[:, d * fb:(d + 1) * fb] = r.astype(BF16)
            acc = acc + _dot((r * r).astype(BF16), w2_ref[d])
        o_ref[...] = acc

    tok = lambda width: pl.BlockSpec((tm, width), lambda i: (i, 0))
    return pl.pallas_call(
        body, name=name, grid=(S // tm,),
        in_specs=[tok(D), _resident((1, D)), _resident(w1.shape), _resident(w2.shape)],
        out_specs=[tok(D), tok(D), tok(nb * fb)],
        out_shape=[jax.ShapeDtypeStruct((S, D), F32), jax.ShapeDtypeStruct((S, D), BF16),
                   jax.ShapeDtypeStruct((S, nb * fb), BF16)],
        compiler_params=_params("parallel"),
    )(x, g, w1, w2)


def _mlp_bwd(dout, x, g, r, w1, w2, name, tm=256):
    S, D = x.shape
    nb, _, fb = w1.shape
    tm = _token_tile(S, tm)

    def body(do_ref, x_ref, g_ref, r_ref, w1_ref, w2_ref, dx_ref, dg_ref, da_ref):
        dov = do_ref[...]
        dob = dov.astype(BF16)
        dh = jnp.zeros((tm, D), F32)
        for d in range(nb):
            dz = _dot_nt(dob, w2_ref[d])
            da = (dz * (2.0 * r_ref[:, d * fb:(d + 1) * fb].astype(F32))).astype(BF16)
            da_ref[:, d * fb:(d + 1) * fb] = da
            dh = dh + _dot_nt(da, w1_ref[d])
        dx, dg = _norm_bwd(dh, x_ref[...], g_ref[...])
        dx_ref[...] = dov + dx

        @pl.when(pl.program_id(0) == 0)
        def _():
            dg_ref[...] = jnp.zeros_like(dg_ref)
        dg_ref[...] += dg

    tok = lambda width: pl.BlockSpec((tm, width), lambda i: (i, 0))
    return pl.pallas_call(
        body, name=name, grid=(S // tm,),
        in_specs=[tok(D), tok(D), _resident((1, D)), tok(nb * fb), _resident(w1.shape),
                  _resident(w2.shape)],
        out_specs=[tok(D), pl.BlockSpec((1, D), lambda i: (0, 0)), tok(nb * fb)],
        out_shape=[jax.ShapeDtypeStruct((S, D), F32), jax.ShapeDtypeStruct((1, D), F32),
                   jax.ShapeDtypeStruct((S, nb * fb), BF16)],
        compiler_params=_params("arbitrary"),
    )(dout, x, g, r, w1, w2)


def _scan_chunk(a, b, row, T, reverse):
    s = 1
    while s < T:
        if reverse:
            keep, shift = row < T - s, T - s
        else:
            keep, shift = row >= s, s
        a_sh = jnp.where(keep, pltpu.roll(a, shift, 0), 1.0)
        b_sh = jnp.where(keep, pltpu.roll(b, shift, 0), 0.0)
        b = a * b_sh + b
        a = a * a_sh
        s *= 2
    return a, b


def _row_of(x, row, r):
    return jnp.sum(jnp.where(row == r, x, 0.0), axis=0, keepdims=True)


def _shift_down(x, prev, row, k):
    if k == 0:
        return x
    return jnp.where(row < k, pltpu.roll(prev, k, 0), pltpu.roll(x, k, 0))


def _shift_up(x, nxt, row, k, T):
    if k == 0:
        return x
    return jnp.where(row < T - k, pltpu.roll(x, T - k, 0), pltpu.roll(nxt, T - k, 0))


def _lru_gates(xb, prev_xb, row, cw_ref, cb, wr, br, wi, bi, ls):
    xc = cb + cw_ref[pl.ds(0, 1), :] * _shift_down(xb, prev_xb, row, 3)
    for k in (2, 1, 0):
        xc = xc + cw_ref[pl.ds(3 - k, 1), :] * _shift_down(xb, prev_xb, row, k)
    xcb = xc.astype(BF16)
    r = _sigmoid(_dot(xcb, wr) + br)
    i = _sigmoid(_dot(xcb, wi) + bi)
    la = (LRU_C * r) * ls
    a = jnp.exp(la)
    m = jnp.sqrt(-_expm1(2.0 * la))
    return xc, xcb, r, i, a, m


def _lru_specs(S):
    col = lambda off: pl.BlockSpec((S, LANES), lambda j: (0, j + off))
    vec = pl.BlockSpec((1, LANES), lambda j: (0, j))
    mat = pl.BlockSpec((None, LANES, LANES), lambda j: (j, 0, 0))
    cwm = pl.BlockSpec((CONV_WIDTH, LANES), lambda j: (0, j))
    return col, vec, mat, cwm


def _lru_fwd(u, conv_w, conv_b, wr, br, wi, bi, lam, name):
    S = u.shape[0]
    T = _token_tile(S, 256)
    col, vec, mat, cwm = _lru_specs(S)

    def body(gp_ref, xb_ref, cw_ref, cb_ref, wr_ref, br_ref, wi_ref, bi_ref, lam_ref,
             y_ref, hs_ref):
        row = lax.broadcasted_iota(jnp.int32, (T, LANES), 0)
        ls = _log_sigmoid(lam_ref[...])
        cb, br, bi = cb_ref[...], br_ref[...], bi_ref[...]
        wr, wi = wr_ref[...], wi_ref[...]

        def chunk(ci, carry):
            prev_xb, hc = carry
            rows = pl.ds(pl.multiple_of(ci * T, T), T)
            xb = xb_ref[rows, :]
            xc, _, _, i, a, m = _lru_gates(xb, prev_xb, row, cw_ref, cb, wr, br, wi, bi, ls)
            ca, cbv = _scan_chunk(a, m * (i * xc), row, T, reverse=False)
            h = ca * hc + cbv
            hs_ref[rows, :] = h
            y_ref[rows, :] = (_gelu(gp_ref[rows, :]) * h).astype(BF16)
            return xb, _row_of(h, row, T - 1)

        lax.fori_loop(0, S // T, chunk,
                      (jnp.zeros((T, LANES), F32), jnp.zeros((1, LANES), F32)))

    return pl.pallas_call(
        body, name=name, grid=(N_CBLK,),
        in_specs=[col(0), col(N_CBLK), cwm, vec, mat, vec, mat, vec, vec],
        out_specs=[col(0), col(0)],
        out_shape=[jax.ShapeDtypeStruct((S, D_MODEL), BF16), jax.ShapeDtypeStruct((S, D_MODEL), F32)],
        compiler_params=_params("parallel"),
    )(u, u, conv_w, conv_b, wr, br, wi, bi, lam)


def _lru_bwd(dy, u, hs, conv_w, conv_b, wr, br, wi, bi, lam, name):
    S = u.shape[0]
    T = _token_tile(S, 256)
    n_chunk = S // T
    col, vec, mat, cwm = _lru_specs(S)

    def body(dy_ref, gp_ref, xb_ref, hs_ref, cw_ref, cb_ref, wr_ref, br_ref, wi_ref, bi_ref,
             lam_ref, du_ref, dcw_ref, dcb_ref, dbr_ref, dbi_ref, dlam_ref, dwr_ref, dwi_ref):
        row = lax.broadcasted_iota(jnp.int32, (T, LANES), 0)
        lam = lam_ref[...]
        ls = _log_sigmoid(lam)
        cb, br, bi = cb_ref[...], br_ref[...], bi_ref[...]
        wr, wi = wr_ref[...], wi_ref[...]
        for ref in (dcw_ref, dcb_ref, dbr_ref, dbi_ref, dlam_ref, dwr_ref, dwi_ref):
            ref[...] = jnp.zeros_like(ref)

        def chunk(it, carry):
            g_next, dxc_next = carry
            ci = n_chunk - 1 - it
            rows = pl.ds(pl.multiple_of(ci * T, T), T)
            before = pl.ds(pl.multiple_of(jnp.maximum(ci - 1, 0) * T, T), T)
            first = ci == 0
            xb = xb_ref[rows, :]
            prev_xb = jnp.where(first, 0.0, xb_ref[before, :])
            xc, xcb, r, i, a, m = _lru_gates(xb, prev_xb, row, cw_ref, cb, wr, br, wi, bi, ls)
            h = hs_ref[rows, :]
            h_prev = _shift_down(h, jnp.where(first, 0.0, hs_ref[before, :]), row, 1)
            gp = gp_ref[rows, :]
            dyv = dy_ref[rows, :]
            du_ref[0, rows, :] = (dyv * h * _gelu_grad(gp)).astype(BF16)
            dh = dyv * _gelu(gp)
            ca, cbv = _scan_chunk(a, a * dh, row, T, reverse=True)
            gp_acc = ca * g_next + cbv
            g = dh + jnp.where(row < T - 1, pltpu.roll(gp_acc, T - 1, 0), g_next)
            da = g * h_prev - (g * (i * xc)) * a / m
            dla = da * a
            dlam_ref[...] += jnp.sum(dla * (LRU_C * r), axis=0, keepdims=True)
            dpr = (dla * (LRU_C * ls)) * r * (1.0 - r)
            dpi = (g * m * xc) * i * (1.0 - i)
            dbr_ref[...] += jnp.sum(dpr, axis=0, keepdims=True)
            dbi_ref[...] += jnp.sum(dpi, axis=0, keepdims=True)
            dprb, dpib = dpr.astype(BF16), dpi.astype(BF16)
            dwr_ref[...] += _dot_tn(xcb, dprb)
            dwi_ref[...] += _dot_tn(xcb, dpib)
            dxc = g * m * i + _dot_nt(dprb, wr) + _dot_nt(dpib, wi)
            dcb_ref[...] += jnp.sum(dxc, axis=0, keepdims=True)
            dxb = jnp.zeros((T, LANES), F32)
            for k in range(CONV_WIDTH):
                tap = pl.ds(CONV_WIDTH - 1 - k, 1)
                dcw_ref[tap, :] += jnp.sum(dxc * _shift_down(xb, prev_xb, row, k), axis=0,
                                           keepdims=True)
                dxb = dxb + cw_ref[tap, :] * _shift_up(dxc, dxc_next, row, k, T)
            du_ref[1, rows, :] = dxb.astype(BF16)
            return _row_of(gp_acc, row, 0), dxc

        lax.fori_loop(0, n_chunk, chunk,
                      (jnp.zeros((1, LANES), F32), jnp.zeros((T, LANES), F32)))
        dlam_ref[...] = dlam_ref[...] * _sigmoid(-lam)

    vec_out = jax.ShapeDtypeStruct((1, D_MODEL), F32)
    mat_out = jax.ShapeDtypeStruct((N_CBLK, LANES, LANES), F32)
    return pl.pallas_call(
        body, name=name, grid=(N_CBLK,),
        in_specs=[col(0), col(0), col(N_CBLK), col(0), cwm, vec, mat, vec, mat, vec, vec],
        out_specs=[pl.BlockSpec((2, S, LANES), lambda j: (0, 0, j)), cwm, vec, vec, vec, vec,
                   mat, mat],
        out_shape=[jax.ShapeDtypeStruct((2, S, D_MODEL), BF16),
                   jax.ShapeDtypeStruct((CONV_WIDTH, D_MODEL), F32),
                   vec_out, vec_out, vec_out, vec_out, mat_out, mat_out],
        compiler_params=_params("parallel"),
    )(dy, u, u, hs, conv_w, conv_b, wr, br, wi, bi, lam)


def _head_group_matrix(value):
    r = lax.broadcasted_iota(jnp.int32, (LANES, LANES), 0) // HEAD_DIM
    c = lax.broadcasted_iota(jnp.int32, (LANES, LANES), 1) // HEAD_DIM
    return jnp.where(r == c, value, 0.0).astype(BF16)


def _group_dot(x, p):
    hi = x.astype(BF16)
    lo = (x - hi.astype(F32)).astype(BF16)
    return _dot(hi, p) + _dot(lo, p)


def _head_mean(x, p):
    return _group_dot(x, p)


def _qk_prep(u, q_gain, k_gain, name, tm=512):
    S = u.shape[0]
    tm = _token_tile(S, tm)

    def body(q_ref, k_ref, v_ref, qg_ref, kg_ref, qn_ref, kn_ref, vb_ref):
        p = _head_group_matrix(1.0 / HEAD_DIM)
        for x_ref, g_ref, o_ref, scale in ((q_ref, qg_ref, qn_ref, ATTN_SCALE),
                                           (k_ref, kg_ref, kn_ref, 1.0)):
            xv = x_ref[...]
            rs = lax.rsqrt(_head_mean(xv * xv, p) + EPS)
            o_ref[...] = (xv * rs * g_ref[...]).astype(BF16) * scale
        vb_ref[...] = v_ref[...].astype(BF16)

    blk = lambda off: pl.BlockSpec((tm, LANES), lambda i, j: (i, j + off))
    out = jax.ShapeDtypeStruct((S, D_MODEL), BF16)
    return pl.pallas_call(
        body, name=name, grid=(S // tm, N_CBLK),
        in_specs=[blk(0), blk(N_CBLK), blk(2 * N_CBLK), _resident((1, LANES)),
                  _resident((1, LANES))],
        out_specs=[blk(0), blk(0), blk(0)],
        out_shape=[out, out, out],
        compiler_params=_params("parallel", "parallel"),
    )(u, u, u, q_gain, k_gain)


def _qk_bwd(u, dqn, dkn, q_gain, k_gain, name, tm=512):
    S = u.shape[0]
    tm = _token_tile(S, tm)

    def body(q_ref, k_ref, dqn_ref, dkn_ref, qg_ref, kg_ref, dq_ref, dk_ref, dqg_ref, dkg_ref):
        p = _head_group_matrix(1.0 / HEAD_DIM)
        first = (pl.program_id(0) == 0) & (pl.program_id(1) == 0)
        last = (pl.program_id(0) == S // tm - 1) & (pl.program_id(1) == N_CBLK - 1)
        for x_ref, dn_ref, g_ref, dx_ref, dg_ref, scale in (
                (q_ref, dqn_ref, qg_ref, dq_ref, dqg_ref, ATTN_SCALE),
                (k_ref, dkn_ref, kg_ref, dk_ref, dkg_ref, 1.0)):
            xv, dn = x_ref[...], dn_ref[...] * scale
            rs = lax.rsqrt(_head_mean(xv * xv, p) + EPS)
            xhat = xv * rs
            dxhat = dn * g_ref[...]
            dx_ref[...] = (rs * (dxhat - xhat * _head_mean(dxhat * xhat, p))).astype(BF16)

            @pl.when(first)
            def _():
                dg_ref[...] = jnp.zeros_like(dg_ref)
            dg_ref[...] += jnp.sum(dn * xhat, axis=0, keepdims=True)

            @pl.when(last)
            def _():
                dg_ref[...] += pltpu.roll(dg_ref[...], HEAD_DIM, 1)

    blk = lambda off: pl.BlockSpec((tm, LANES), lambda i, j: (i, j + off))
    acc = pl.BlockSpec((1, LANES), lambda i, j: (0, 0))
    out = jax.ShapeDtypeStruct((S, D_MODEL), BF16)
    vec = jax.ShapeDtypeStruct((1, LANES), F32)
    return pl.pallas_call(
        body, name=name, grid=(S // tm, N_CBLK),
        in_specs=[blk(0), blk(N_CBLK), blk(0), blk(0), _resident((1, LANES)),
                  _resident((1, LANES))],
        out_specs=[blk(0), blk(0), acc, acc],
        out_shape=[out, out, vec, vec],
        compiler_params=_params("arbitrary", "arbitrary"),
    )(u, u, dqn, dkn, q_gain, k_gain)


def _forget_fwd(f, b_f, name):
    S = f.shape[0]
    T = _token_tile(S, 256)

    def body(f_ref, b_ref, c_ref):
        row = lax.broadcasted_iota(jnp.int32, (T, LANES), 0)
        ones = jnp.ones((T, LANES), F32)
        bias = b_ref[...]

        def chunk(ci, carry):
            rows = pl.ds(pl.multiple_of(ci * T, T), T)
            _, c = _scan_chunk(ones, _log_sigmoid(f_ref[rows, :] + bias), row, T, reverse=False)
            c = c + carry
            c_ref[rows, :] = c
            return _row_of(c, row, T - 1)

        lax.fori_loop(0, S // T, chunk, jnp.zeros((1, LANES), F32))

    return pl.pallas_call(
        body, name=name,
        in_specs=[pl.BlockSpec(memory_space=pltpu.VMEM)] * 2,
        out_specs=pl.BlockSpec(memory_space=pltpu.VMEM),
        out_shape=jax.ShapeDtypeStruct((S, LANES), F32),
        compiler_params=pltpu.CompilerParams(vmem_limit_bytes=VMEM_LIMIT),
    )(f, b_f)


def _forget_bwd(dc_k, dc_q, f, b_f, name):
    S = f.shape[0]
    T = _token_tile(S, 256)
    n_chunk = S // T

    def body(dck_ref, dcq_ref, f_ref, b_ref, df_ref, db_ref):
        row = lax.broadcasted_iota(jnp.int32, (T, LANES), 0)
        ones = jnp.ones((T, LANES), F32)
        bias = b_ref[...]

        def chunk(it, carry):
            tail, db = carry
            rows = pl.ds(pl.multiple_of((n_chunk - 1 - it) * T, T), T)
            _, dlf = _scan_chunk(ones, dck_ref[rows, :] + dcq_ref[rows, :], row, T, reverse=True)
            dlf = dlf + tail
            df = dlf * _sigmoid(-(f_ref[rows, :] + bias))
            df_ref[rows, :] = df
            return _row_of(dlf, row, 0), db + jnp.sum(df, axis=0, keepdims=True)

        zero = jnp.zeros((1, LANES), F32)
        _, db = lax.fori_loop(0, n_chunk, chunk, (zero, zero))
        db_ref[...] = db

    return pl.pallas_call(
        body, name=name,
        in_specs=[pl.BlockSpec(memory_space=pltpu.VMEM)] * 4,
        out_specs=[pl.BlockSpec(memory_space=pltpu.VMEM)] * 2,
        out_shape=[jax.ShapeDtypeStruct((S, LANES), F32), jax.ShapeDtypeStruct((1, LANES), F32)],
        compiler_params=pltpu.CompilerParams(vmem_limit_bytes=VMEM_LIMIT),
    )(dc_k, dc_q, f, b_f)


ATTN_TILE = 512


def _attn_tiles(S):
    t = _token_tile(S, ATTN_TILE)
    return t, S // t


def _causal(T):
    return (lax.broadcasted_iota(jnp.int32, (T, T), 1)
            <= lax.broadcasted_iota(jnp.int32, (T, T), 0))


def _attn_fwd(qs_, kn, vb, c_row, name):
    S = qs_.shape[0]
    T, n_t = _attn_tiles(S)

    def body(q_ref, k_ref, v_ref, cr_ref, o_ref, lse_ref):
        qi = pl.program_id(1)
        causal = _causal(T)
        lanes = [slice(h2 * HEAD_DIM, (h2 + 1) * HEAD_DIM) for h2 in range(2)]
        qh = [q_ref[:, hl] for hl in lanes]

        def step(kj, carry, masked):
            ks = pl.ds(pl.multiple_of(kj * T, T), T)
            out = []
            for h2, hl in enumerate(lanes):
                m, l, acc = carry[h2]
                s = _dot_nt(qh[h2], k_ref[ks, hl]) - cr_ref[h2:h2 + 1, ks]
                if masked:
                    s = jnp.where(causal, s, NEG_INF)
                m_new = jnp.maximum(m, jnp.max(s, axis=1, keepdims=True))
                alpha = jnp.exp(m - m_new)
                p = jnp.exp(s - m_new)
                l = alpha * l + jnp.sum(p, axis=1, keepdims=True)
                acc = alpha * acc + _dot(p.astype(BF16), v_ref[ks, hl])
                out.append((m_new, l, acc))
            return tuple(out)

        init = tuple((jnp.full((T, 1), NEG_INF, F32), jnp.zeros((T, 1), F32),
                      jnp.zeros((T, HEAD_DIM), F32)) for _ in lanes)
        carry = lax.fori_loop(0, qi, lambda kj, c: step(kj, c, False), init)
        carry = step(qi, carry, True)
        for (m, l, acc), hl in zip(carry, lanes):
            o_ref[:, hl] = (acc / l).astype(BF16)
            lse_ref[:, hl] = jnp.broadcast_to(m + jnp.log(l), (T, HEAD_DIM))

    qblk = pl.BlockSpec((T, LANES), lambda h, i: (i, h))
    kv = pl.BlockSpec((S, LANES), lambda h, i: (0, h))
    return pl.pallas_call(
        body, name=name, grid=(N_CBLK, n_t),
        in_specs=[qblk, kv, kv, pl.BlockSpec((None, 2, S), lambda h, i: (h, 0, 0))],
        out_specs=[qblk, qblk],
        out_shape=[jax.ShapeDtypeStruct((S, D_MODEL), BF16),
                   jax.ShapeDtypeStruct((S, D_MODEL), F32)],
        compiler_params=_params("parallel", "parallel"),
    )(qs_, kn, vb, c_row)


def _attn_bwd(qs_, kn, vb, do, o, lse, c_row, name):
    S = qs_.shape[0]
    T, n_t = _attn_tiles(S)

    def body(q_ref, k_ref, v_ref, do_ref, o_ref, lse_ref, cr_ref,
             dq_ref, dk_ref, dv_ref, dc_ref, rho_ref, dd_ref):
        kj = pl.program_id(1)
        causal = _causal(T)
        lanes = [slice(h2 * HEAD_DIM, (h2 + 1) * HEAD_DIM) for h2 in range(2)]
        ones = [slice(h2 * HEAD_DIM, h2 * HEAD_DIM + 1) for h2 in range(2)]

        @pl.when(kj == 0)
        def _():
            dq_ref[...] = jnp.zeros_like(dq_ref)
            rho_ref[...] = jnp.zeros_like(rho_ref)
            p_sum = _head_group_matrix(1.0)

            def fill(ci, _):
                rows = pl.ds(pl.multiple_of(ci * T, T), T)
                dd_ref[rows, :] = _group_dot(do_ref[rows, :].astype(F32) * o_ref[rows, :].astype(F32),
                                             p_sum)
                return 0

            lax.fori_loop(0, n_t, fill, 0)

        kh = [k_ref[:, hl] for hl in lanes]
        vh = [v_ref[:, hl] for hl in lanes]
        ck = [cr_ref[h2:h2 + 1, :] for h2 in range(2)]

        def step(qi, carry, masked):
            qs = pl.ds(pl.multiple_of(qi * T, T), T)
            out = []
            for h2, hl in enumerate(lanes):
                dk, dv, dc = carry[h2]
                qh, doh = q_ref[qs, hl], do_ref[qs, hl]
                s = _dot_nt(qh, kh[h2]) - ck[h2]
                if masked:
                    s = jnp.where(causal, s, NEG_INF)
                p = jnp.exp(s - lse_ref[qs, ones[h2]])
                ds = p * (_dot_nt(doh, vh[h2]) - dd_ref[qs, ones[h2]])
                dsb = ds.astype(BF16)
                dq_ref[qs, hl] += _dot(dsb, kh[h2])
                rho_ref[qs, hl] += jnp.broadcast_to(jnp.sum(ds, axis=1, keepdims=True),
                                                    (T, HEAD_DIM))
                out.append((dk + _dot_tn(dsb, qh), dv + _dot_tn(p.astype(BF16), doh),
                            dc - jnp.sum(ds, axis=0, keepdims=True)))
            return tuple(out)

        init = tuple((jnp.zeros((T, HEAD_DIM), F32), jnp.zeros((T, HEAD_DIM), F32),
                      jnp.zeros((1, T), F32)) for _ in lanes)
        carry = step(kj, init, True)
        carry = lax.fori_loop(kj + 1, n_t, lambda qi, c: step(qi, c, False), carry)
        for h2, ((dk, dv, dc), hl) in enumerate(zip(carry, lanes)):
            dk_ref[:, hl] = dk
            dv_ref[:, hl] = dv.astype(BF16)
            dc_ref[h2:h2 + 1, :] = dc

    kblk = pl.BlockSpec((T, LANES), lambda h, j: (j, h))
    full = pl.BlockSpec((S, LANES), lambda h, j: (0, h))
    crow = pl.BlockSpec((None, 2, T), lambda h, j: (h, 0, j))
    wide = jax.ShapeDtypeStruct((S, D_MODEL), F32)
    return pl.pallas_call(
        body, name=name, grid=(N_CBLK, n_t),
        in_specs=[full, kblk, kblk, full, full, full, crow],
        out_specs=[full, kblk, kblk, crow, full],
        out_shape=[wide, wide, jax.ShapeDtypeStruct((S, D_MODEL), BF16),
                   jax.ShapeDtypeStruct((N_CBLK, 2, S), F32), wide],
        scratch_shapes=[pltpu.VMEM((S, LANES), F32)],
        compiler_params=_params("parallel", "arbitrary"),
    )(qs_, kn, vb, do, o, lse, c_row)


def _loss_head(y, target, name, tm=512):
    S, D = y.shape
    tm = _token_tile(S, tm)

    def body(y_ref, t_ref, loss_ref, dy_ref):
        err = y_ref[...] - t_ref[...]
        dy_ref[...] = err / D

        @pl.when(pl.program_id(0) == 0)
        def _():
            loss_ref[...] = jnp.zeros_like(loss_ref)
        row_loss = jnp.mean(err * err, axis=1, keepdims=True)
        loss_ref[...] += 0.5 * jnp.sum(row_loss, axis=0, keepdims=True)

    tok = pl.BlockSpec((tm, D), lambda i: (i, 0))
    return pl.pallas_call(
        body, name=name, grid=(S // tm,),
        in_specs=[tok, tok],
        out_specs=[pl.BlockSpec((1, 1), lambda i: (0, 0)), tok],
        out_shape=[jax.ShapeDtypeStruct((1, 1), F32), jax.ShapeDtypeStruct((S, D), F32)],
        compiler_params=_params("arbitrary"),
    )(y, target)


def _exchange(arrays, gathers, name):
    n = len(arrays)
    n_peer = N_DEV - 1

    def body(*refs):
        ins, outs = refs[:n], refs[n:2 * n]
        send_sems, recv_sems, own_sems = refs[2 * n:]
        x, y, c = lax.axis_index("x"), lax.axis_index("y"), lax.axis_index("c")
        me = 4 * x + 2 * y + c
        own = []
        for a in range(n):
            src = ins[a] if gathers[a] else ins[a].at[me]
            own.append(pltpu.make_async_copy(src, outs[a].at[me], own_sems.at[a]))
            own[-1].start()
        sent = []
        for k in range(1, N_DEV):
            px = 1 - x if k & 4 else x
            py = 1 - y if k & 2 else y
            pc = 1 - c if k & 1 else c
            peer = 4 * px + 2 * py + pc
            for a in range(n):
                sem = a * n_peer + k - 1
                src = ins[a] if gathers[a] else ins[a].at[peer]

                def copy(dst_slot, src=src, a=a, sem=sem, to=(px, py, pc)):
                    return pltpu.make_async_remote_copy(
                        src_ref=src, dst_ref=outs[a].at[dst_slot], send_sem=send_sems.at[sem],
                        recv_sem=recv_sems.at[sem], device_id=to,
                        device_id_type=pl.DeviceIdType.MESH)

                copy(me).start()
                sent.append((copy, me, peer))
        for copy, me_slot, peer_slot in sent:
            copy(peer_slot).wait_recv()
            copy(me_slot).wait_send()
        for cp in own:
            cp.wait()

    out_shape = [jax.ShapeDtypeStruct((N_DEV,) + a.shape if g else a.shape, a.dtype)
                 for a, g in zip(arrays, gathers)]
    hbm = pl.BlockSpec(memory_space=pl.ANY)
    return pl.pallas_call(
        body, name=name,
        in_specs=[hbm] * n, out_specs=[hbm] * n, out_shape=out_shape,
        scratch_shapes=[pltpu.SemaphoreType.DMA((n * n_peer,)),
                        pltpu.SemaphoreType.DMA((n * n_peer,)),
                        pltpu.SemaphoreType.DMA((n,))],
        compiler_params=pltpu.CompilerParams(has_side_effects=True),
    )(*arrays)


def _reduce_adamw(parts, w, m, v, name):
    n, R, C = parts.shape
    tr = 256 if R % 256 == 0 else R

    def body(p_ref, w_ref, m_ref, v_ref, g_ref, d_ref, nm_ref, nv_ref):
        g = p_ref[0].astype(F32)
        for s in range(1, n):
            g = g + p_ref[s].astype(F32)
        g_ref[...] = g
        m_new = ADAM_B1 * m_ref[...] + (1.0 - ADAM_B1) * g
        v_new = ADAM_B2 * v_ref[...] + (1.0 - ADAM_B2) * (g * g)
        nm_ref[...] = m_new
        nv_ref[...] = v_new
        m_hat = m_new / (1.0 - ADAM_B1 ** ADAM_STEP)
        v_hat = v_new / (1.0 - ADAM_B2 ** ADAM_STEP)
        d_ref[...] = -ADAM_LR * (m_hat / (jnp.sqrt(v_hat) + ADAM_EPS) + ADAM_WD * w_ref[...])

    blk = pl.BlockSpec((tr, C), lambda i: (i, 0))
    out = jax.ShapeDtypeStruct((R, C), F32)
    return pl.pallas_call(
        body, name=name, grid=(R // tr,),
        in_specs=[pl.BlockSpec((n, tr, C), lambda i: (0, i, 0)), blk, blk, blk],
        out_specs=[blk] * 4, out_shape=[out] * 4,
        compiler_params=_params("parallel"),
    )(parts, w, m, v)


def _pack(arrays):
    flat = jnp.concatenate([a.reshape(-1).astype(F32) for a in arrays])
    pad = (-flat.shape[0]) % (8 * LANES)
    return jnp.pad(flat, (0, pad)).reshape(-1, LANES)


def _unpack(buf, shapes):
    flat = buf.reshape(-1)
    out, off = [], 0
    for shp in shapes:
        size = 1
        for s in shp:
            size *= s
        out.append(flat[off:off + size].reshape(shp))
        off += size
    return out


def _block_diag_pairs(w):
    w = w.reshape(N_CBLK, 2, LRU_BLOCK_DIM, LRU_BLOCK_DIM)
    z = jnp.zeros_like(w[:, 0])
    top = jnp.concatenate([w[:, 0], z], axis=2)
    bot = jnp.concatenate([z, w[:, 1]], axis=2)
    return jnp.concatenate([top, bot], axis=1)


def _diag_pairs(m):
    h = LRU_BLOCK_DIM
    return jnp.stack([m[:, :h, :h], m[:, h:, h:]], axis=1).reshape(2 * N_CBLK, h, h)


SMALL = ("mix_norm", "mlp_norm", "lru_conv_b", "lru_w_r", "lru_b_r", "lru_w_i", "lru_b_i",
         "lru_lambda", "fox_b_f", "fox_q_gain", "fox_k_gain")
WEIGHTS = ("mix_norm", "mlp_norm", "mlp_w1", "mlp_w2", "lru_w_in", "lru_conv_w", "lru_conv_b",
           "lru_w_r", "lru_b_r", "lru_w_i", "lru_b_i", "lru_lambda", "lru_w_out", "fox_w_in",
           "fox_b_f", "fox_q_gain", "fox_k_gain", "fox_w_out")


def kernel(x, mix_norm, mlp_norm, mlp_w1, mlp_w2, lru_w_in, lru_conv_w, lru_conv_b, lru_w_r, lru_b_r, lru_w_i, lru_b_i, lru_lambda, lru_w_out, fox_w_in, fox_b_f, fox_q_gain, fox_k_gain, fox_w_out, loss_target, m_mix_norm, m_mlp_norm, m_mlp_w1, m_mlp_w2, m_lru_w_in, m_lru_conv_w, m_lru_conv_b, m_lru_w_r, m_lru_b_r, m_lru_w_i, m_lru_b_i, m_lru_lambda, m_lru_w_out, m_fox_w_in, m_fox_b_f, m_fox_q_gain, m_fox_k_gain, m_fox_w_out, v_mix_norm, v_mlp_norm, v_mlp_w1, v_mlp_w2, v_lru_w_in, v_lru_conv_w, v_lru_conv_b, v_lru_w_r, v_lru_b_r, v_lru_w_i, v_lru_b_i, v_lru_lambda, v_lru_w_out, v_fox_w_in, v_fox_b_f, v_fox_q_gain, v_fox_k_gain, v_fox_w_out):
    w_in = dict(mix_norm=mix_norm, mlp_norm=mlp_norm, mlp_w1=mlp_w1, mlp_w2=mlp_w2,
                lru_w_in=lru_w_in, lru_conv_w=lru_conv_w, lru_conv_b=lru_conv_b, lru_w_r=lru_w_r,
                lru_b_r=lru_b_r, lru_w_i=lru_w_i, lru_b_i=lru_b_i, lru_lambda=lru_lambda,
                lru_w_out=lru_w_out, fox_w_in=fox_w_in, fox_b_f=fox_b_f, fox_q_gain=fox_q_gain,
                fox_k_gain=fox_k_gain, fox_w_out=fox_w_out)
    m_in = dict(mix_norm=m_mix_norm, mlp_norm=m_mlp_norm, mlp_w1=m_mlp_w1, mlp_w2=m_mlp_w2,
                lru_w_in=m_lru_w_in, lru_conv_w=m_lru_conv_w, lru_conv_b=m_lru_conv_b,
                lru_w_r=m_lru_w_r, lru_b_r=m_lru_b_r, lru_w_i=m_lru_w_i, lru_b_i=m_lru_b_i,
                lru_lambda=m_lru_lambda, lru_w_out=m_lru_w_out, fox_w_in=m_fox_w_in,
                fox_b_f=m_fox_b_f, fox_q_gain=m_fox_q_gain, fox_k_gain=m_fox_k_gain,
                fox_w_out=m_fox_w_out)
    v_in = dict(mix_norm=v_mix_norm, mlp_norm=v_mlp_norm, mlp_w1=v_mlp_w1, mlp_w2=v_mlp_w2,
                lru_w_in=v_lru_w_in, lru_conv_w=v_lru_conv_w, lru_conv_b=v_lru_conv_b,
                lru_w_r=v_lru_w_r, lru_b_r=v_lru_b_r, lru_w_i=v_lru_w_i, lru_b_i=v_lru_b_i,
                lru_lambda=v_lru_lambda, lru_w_out=v_lru_w_out, fox_w_in=v_fox_w_in,
                fox_b_f=v_fox_b_f, fox_q_gain=v_fox_q_gain, fox_k_gain=v_fox_k_gain,
                fox_w_out=v_fox_w_out)
    D = D_MODEL
    S = x.shape[1]
    x0, target = x[0], loss_target[0]
    me = 4 * lax.axis_index("x") + 2 * lax.axis_index("y") + lax.axis_index("c")

    (w1g0, w1g1, w2g0, w2g1, lru_in_g, lru_out_g, fox_in_g, fox_out_g, conv_g) = _exchange(
        [mlp_w1[0].astype(BF16), mlp_w1[1].astype(BF16), mlp_w2[0].astype(BF16),
         mlp_w2[1].astype(BF16), lru_w_in[0].astype(BF16), lru_w_out[0].astype(BF16),
         fox_w_in[0].astype(BF16), fox_w_out[0].astype(BF16), lru_conv_w[0]],
        [True] * 9, "gather_weights")
    lru_out_w = lru_out_g.reshape(D, D)
    fox_out_w = fox_out_g.reshape(D, D)
    conv_w = conv_g.transpose(1, 0, 2).reshape(CONV_WIDTH, D)
    fox_full = fox_in_g.transpose(1, 0, 2).reshape(D, 3 * D + N_HEADS)
    wqkv = fox_full[:, :3 * D].reshape(D, 3, D).transpose(1, 0, 2)
    wf = jnp.pad(fox_full[:, 3 * D:], ((0, 0), (0, LANES - N_HEADS)))[None]
    wr = _block_diag_pairs(lru_w_r[0]).astype(BF16)
    wi = _block_diag_pairs(lru_w_i[0]).astype(BF16)
    b_r, b_i = lru_b_r.reshape(1, D), lru_b_i.reshape(1, D)
    q_gain, k_gain = jnp.tile(fox_q_gain, (1, 2)), jnp.tile(fox_k_gain, (1, 2))
    b_f = jnp.pad(fox_b_f, ((0, 0), (0, LANES - N_HEADS)))
    g_mix0, g_mix1 = mix_norm[0:1], mix_norm[1:2]
    g_mlp0, g_mlp1 = mlp_norm[0:1], mlp_norm[1:2]

    (u0,), h0 = _norm_matmul(x0, g_mix0, [lru_in_g], "lru_in_proj")
    y_lru, hs = _lru_fwd(u0, conv_w, lru_conv_b, wr, b_r, wi, b_i, lru_lambda, "lru_core")
    x1 = _matmul_res(y_lru, lru_out_w, x0, "lru_out_proj")
    x2, h1, r1 = _mlp_fwd(x1, g_mlp0, w1g0, w2g0, "mlp0")
    (u_qkv, f), h2 = _norm_matmul(x2, g_mix1, [wqkv, wf], "fox_in_proj")
    qn, kn, vb = _qk_prep(u_qkv, q_gain, k_gain, "fox_qk_norm")
    c_col = _forget_fwd(f, b_f, "fox_forget")
    c_row = c_col[:, :N_HEADS].T.reshape(N_CBLK, 2, S)
    o, lse = _attn_fwd(qn, kn, vb, c_row, "fox_attn")
    x3 = _matmul_res(o, fox_out_w, x2, "fox_out_proj")
    x4, h3, r3 = _mlp_fwd(x3, g_mlp1, w1g1, w2g1, "mlp1")
    loss_local, dx4 = _loss_head(x4, target, "loss_head")

    dx3, dg_mlp1, da3 = _mlp_bwd(dx4, x3, g_mlp1, r3, w1g1, w2g1, "mlp1_bwd")
    dw1_1 = _matmul_tn(h3, da3, N_DEV, "b", "mlp1_dw1")
    dw2_1 = _matmul_tn(r3, dx4, N_DEV, "a", "mlp1_dw2", a_square=True)
    do = _matmul_nt(dx3, fox_out_w, "fox_out_bwd", BF16)
    d_fox_out = _matmul_tn(o, dx3, N_DEV, "a", "fox_out_dw")
    dqn, dkn, dv, dc_row, rho = _attn_bwd(qn, kn, vb, do, o, lse, c_row, "fox_attn_bwd")
    duq, duk, dq_gain, dk_gain = _qk_bwd(u_qkv, dqn, dkn, q_gain, k_gain, "fox_qk_norm_bwd")
    head_pad = ((0, 0), (0, LANES - N_HEADS))
    dc_k = jnp.pad(dc_row.reshape(N_HEADS, S).T, head_pad)
    dc_q = jnp.pad(rho[:, ::HEAD_DIM], head_pad)
    df, db_f = _forget_bwd(dc_k, dc_q, f, b_f, "fox_forget_bwd")
    dx2, dg_mix1 = _proj_bwd([duq, duk, dv, df], [wqkv[0:1], wqkv[1:2], wqkv[2:3], wf],
                             x2, g_mix1, dx3, "fox_in_bwd")
    d_fox_in = jnp.concatenate(
        [_matmul_tn(h2, duq, 1, "b", "fox_in_dwq")[0], _matmul_tn(h2, duk, 1, "b", "fox_in_dwk")[0],
         _matmul_tn(h2, dv, 1, "b", "fox_in_dwv")[0],
         _matmul_tn(h2, df, 1, "b", "fox_in_dwf")[0][:, :N_HEADS]], axis=1)
    d_fox_in = d_fox_in.reshape(D, N_DEV, -1).transpose(1, 0, 2)
    dx1, dg_mlp0, da1 = _mlp_bwd(dx2, x1, g_mlp0, r1, w1g0, w2g0, "mlp0_bwd")
    dw1_0 = _matmul_tn(h1, da1, N_DEV, "b", "mlp0_dw1")
    dw2_0 = _matmul_tn(r1, dx2, N_DEV, "a", "mlp0_dw2", a_square=True)
    dy_lru = _matmul_nt(dx1, lru_out_w, "lru_out_bwd", F32)
    d_lru_out = _matmul_tn(y_lru, dx1, N_DEV, "a", "lru_out_dw")
    du0, d_conv_w, d_conv_b, d_b_r, d_b_i, d_lam, d_wr, d_wi = _lru_bwd(
        dy_lru, u0, hs, conv_w, lru_conv_b, wr, b_r, wi, b_i, lru_lambda, "lru_core_bwd")
    dx0, dg_mix0 = _proj_bwd([du0[0], du0[1]], [lru_in_g[:4], lru_in_g[4:]], x0, g_mix0, dx1,
                             "lru_in_bwd")
    d_lru_in = jnp.concatenate([_matmul_tn(h0, du0[0], 4, "b", "lru_in_dw_gate"),
                                _matmul_tn(h0, du0[1], 4, "b", "lru_in_dw_x")], axis=0)

    small_grads = dict(
        mix_norm=jnp.concatenate([dg_mix0, dg_mix1], axis=0),
        mlp_norm=jnp.concatenate([dg_mlp0, dg_mlp1], axis=0),
        lru_conv_b=d_conv_b, lru_w_r=_diag_pairs(d_wr), lru_b_r=d_b_r, lru_w_i=_diag_pairs(d_wi),
        lru_b_i=d_b_i, lru_lambda=d_lam, fox_b_f=db_f[:, :N_HEADS],
        fox_q_gain=dq_gain[:, :HEAD_DIM], fox_k_gain=dk_gain[:, :HEAD_DIM])
    small_partial = _pack([small_grads[n] for n in SMALL] + [d_conv_w])
    big = [dw1_0, dw1_1, dw2_0, dw2_1, d_lru_in, d_lru_out, d_fox_in, d_fox_out]
    got = _exchange([b.astype(BF16) for b in big] + [small_partial], [False] * 8 + [True],
                    "exchange_grads")
    (p_w1_0, p_w1_1, p_w2_0, p_w2_1, p_lru_in, p_lru_out, p_fox_in, p_fox_out, p_small) = got

    grads, deltas, new_m, new_v = {}, {}, {}, {}

    def update(name, parts, sel=None):
        w, m, v = w_in[name], m_in[name], v_in[name]
        if sel is not None:
            w, m, v = w[sel], m[sel], v[sel]
        shape = w.shape
        two_d = (-1, shape[-1])
        res = _reduce_adamw(parts.reshape((N_DEV,) + w.reshape(two_d).shape), w.reshape(two_d),
                            m.reshape(two_d), v.reshape(two_d),
                            "adamw_" + name + ("" if sel is None else "_%d" % sel))
        return [r.reshape(shape) for r in res]

    def store(name, res):
        grads[name], deltas[name], new_m[name], new_v[name] = res

    store("mlp_w1", [jnp.stack(p) for p in zip(update("mlp_w1", p_w1_0, 0),
                                               update("mlp_w1", p_w1_1, 1))])
    store("mlp_w2", [jnp.stack(p) for p in zip(update("mlp_w2", p_w2_0, 0),
                                               update("mlp_w2", p_w2_1, 1))])
    store("lru_w_in", update("lru_w_in", p_lru_in))
    store("lru_w_out", update("lru_w_out", p_lru_out))
    store("fox_w_in", update("fox_w_in", p_fox_in))
    store("fox_w_out", update("fox_w_out", p_fox_out))

    small_shapes = [w_in[n].shape for n in SMALL]
    n_small = sum(math.prod(s) for s in small_shapes)
    res_small = _reduce_adamw(p_small, _pack([w_in[n] for n in SMALL] + [jnp.zeros((CONV_WIDTH, D))]),
                              _pack([m_in[n] for n in SMALL] + [jnp.zeros((CONV_WIDTH, D))]),
                              _pack([v_in[n] for n in SMALL] + [jnp.zeros((CONV_WIDTH, D))]),
                              "adamw_small")
    for name, *vals in zip(SMALL, *[_unpack(r, small_shapes) for r in res_small]):
        store(name, vals)
    conv_parts = p_small.reshape(N_DEV, -1)[:, n_small:n_small + CONV_WIDTH * D]
    conv_parts = conv_parts.reshape(N_DEV, CONV_WIDTH, N_DEV, LANES)
    conv_parts = lax.dynamic_index_in_dim(conv_parts, me, axis=2, keepdims=False)
    store("lru_conv_w", update("lru_conv_w", conv_parts))

    loss = lax.psum(loss_local[0, 0], ("x", "y", "c"))
    return (loss, dx0[None], *[grads[n] for n in WEIGHTS], *[deltas[n] for n in WEIGHTS],
            *[new_m[n] for n in WEIGHTS], *[new_v[n] for n in WEIGHTS])
```

```python
import math

import jax
import jax.numpy as jnp
from jax import lax
from jax.experimental import pallas as pl
from jax.experimental.pallas import tpu as pltpu

F32 = jnp.float32
BF16 = jnp.bfloat16

N_DEV = 8
D_MODEL = 1024
D_FF = 4096
N_HEADS = 16
HEAD_DIM = 64
LRU_BLOCK_DIM = 64
CONV_WIDTH = 4
LRU_C = 8.0
EPS = 1e-6
NEG_INF = -1e30
ATTN_SCALE = HEAD_DIM ** -0.5
LANES = 128
N_CBLK = D_MODEL // LANES
VMEM_LIMIT = 52 * 2 ** 20

ADAM_LR = 0.001
ADAM_B1 = 0.9
ADAM_B2 = 0.999
ADAM_EPS = 1e-08
ADAM_WD = 0.01
ADAM_STEP = 10

_NT = (((1,), (1,)), ((), ()))
_TN = (((0,), (0,)), ((), ()))


def _params(*sem):
    return pltpu.CompilerParams(dimension_semantics=sem, vmem_limit_bytes=VMEM_LIMIT)


def _resident(shape):
    zeros = (0,) * len(shape)
    return pl.BlockSpec(shape, lambda *_: zeros, pipeline_mode=pl.Buffered(1))


def _dot(a, b):
    return jnp.dot(a, b, preferred_element_type=F32)


def _dot_nt(a, b):
    return lax.dot_general(a, b, _NT, preferred_element_type=F32)


def _dot_tn(a, b):
    return lax.dot_general(a, b, _TN, preferred_element_type=F32)


def _sigmoid(x):
    return 1.0 / (1.0 + jnp.exp(-x))


def _log_sigmoid(x):
    return -(jnp.maximum(-x, 0.0) + jnp.log1p(jnp.exp(-jnp.abs(x))))


def _expm1(x):
    poly = x * (1.0 + x * (0.5 + x * (1.0 / 6.0 + x * (1.0 / 24.0 + x * (1.0 / 120.0)))))
    return jnp.where(jnp.abs(x) < 0.1, poly, jnp.exp(x) - 1.0)


_GELU_K = 0.7978845608028654


def _gelu(x):
    return 0.5 * x * (1.0 + jnp.tanh(_GELU_K * (x + 0.044715 * (x * x * x))))


def _gelu_grad(x):
    t = jnp.tanh(_GELU_K * (x + 0.044715 * (x * x * x)))
    return 0.5 * (1.0 + t) + 0.5 * x * (1.0 - t * t) * (_GELU_K * (1.0 + 3 * 0.044715 * x * x))


def _rms_scale(x):
    return lax.rsqrt(jnp.mean(x * x, axis=-1, keepdims=True) + EPS)


def _norm_bwd(dh, x, g):
    rs = _rms_scale(x)
    xhat = x * rs
    dxhat = dh * g
    dx = rs * (dxhat - xhat * jnp.mean(dxhat * xhat, axis=-1, keepdims=True))
    return dx, jnp.sum(dh * xhat, axis=0, keepdims=True)


def _token_tile(S, want):
    tm = min(S, want)
    assert S % tm == 0
    return tm


def _norm_matmul(x, g, ws, name, tm=256):
    S, D = x.shape
    tm = _token_tile(S, tm)
    n = len(ws)

    def body(x_ref, g_ref, *refs):
        w_refs, o_refs, h_ref = refs[:n], refs[n:2 * n], refs[2 * n]
        xv = x_ref[...]
        h = (xv * _rms_scale(xv) * g_ref[...]).astype(BF16)
        h_ref[...] = h
        for w_ref, o_ref in zip(w_refs, o_refs):
            nb, _, nw = w_ref.shape
            for d in range(nb):
                o_ref[:, d * nw:(d + 1) * nw] = _dot(h, w_ref[d])

    widths = [w.shape[0] * w.shape[2] for w in ws]
    outs = pl.pallas_call(
        body, name=name, grid=(S // tm,),
        in_specs=[pl.BlockSpec((tm, D), lambda i: (i, 0)), _resident((1, D))]
        + [_resident(w.shape) for w in ws],
        out_specs=[pl.BlockSpec((tm, n_), lambda i: (i, 0)) for n_ in widths]
        + [pl.BlockSpec((tm, D), lambda i: (i, 0))],
        out_shape=[jax.ShapeDtypeStruct((S, n_), F32) for n_ in widths]
        + [jax.ShapeDtypeStruct((S, D), BF16)],
        compiler_params=_params("parallel"),
    )(x, g, *ws)
    return outs[:n], outs[n]


def _matmul_res(a, w, res, name, tm=512):
    S, K = a.shape
    N = w.shape[1]
    tm = _token_tile(S, tm)

    def body(a_ref, w_ref, r_ref, o_ref):
        o_ref[...] = r_ref[...] + _dot(a_ref[...], w_ref[...])

    return pl.pallas_call(
        body, name=name, grid=(S // tm,),
        in_specs=[pl.BlockSpec((tm, K), lambda i: (i, 0)), _resident((K, N)),
                  pl.BlockSpec((tm, N), lambda i: (i, 0))],
        out_specs=pl.BlockSpec((tm, N), lambda i: (i, 0)),
        out_shape=jax.ShapeDtypeStruct((S, N), F32),
        compiler_params=_params("parallel"),
    )(a, w, res)


def _matmul_nt(a, w, name, out_dtype, after, tm=512):
    S, N = a.shape
    K = w.shape[0]
    tm = _token_tile(S, tm)

    def body(a_ref, w_ref, after_ref, o_ref):
        o_ref[...] = _dot_nt(a_ref[...].astype(BF16), w_ref[...]).astype(out_dtype)

    return pl.pallas_call(
        body, name=name, grid=(S // tm,),
        in_specs=[pl.BlockSpec((tm, N), lambda i: (i, 0)), _resident((K, N)),
                  pl.BlockSpec(memory_space=pl.ANY)],
        out_specs=pl.BlockSpec((tm, K), lambda i: (i, 0)),
        out_shape=jax.ShapeDtypeStruct((S, K), out_dtype),
        compiler_params=_params("parallel"),
    )(a, w, after)


def _proj_bwd(a_list, w_list, x, g, res, name, tm=256):
    S, D = x.shape
    tm = _token_tile(S, tm)
    n = len(a_list)

    def body(*refs):
        a_refs, w_refs = refs[:n], refs[n:2 * n]
        x_ref, g_ref, r_ref, dx_ref, dg_ref = refs[2 * n:]
        dh = jnp.zeros((tm, D), F32)
        for a_ref, w_ref in zip(a_refs, w_refs):
            nb, _, nw = w_ref.shape
            for d in range(nb):
                dh = dh + _dot_nt(a_ref[:, d * nw:(d + 1) * nw].astype(BF16), w_ref[d])
        dx, dg = _norm_bwd(dh, x_ref[...], g_ref[...])
        dx_ref[...] = r_ref[...] + dx

        @pl.when(pl.program_id(0) == 0)
        def _():
            dg_ref[...] = jnp.zeros_like(dg_ref)
        dg_ref[...] += dg

    tok = lambda width: pl.BlockSpec((tm, width), lambda i: (i, 0))
    return pl.pallas_call(
        body, name=name, grid=(S // tm,),
        in_specs=[tok(a.shape[1]) for a in a_list] + [_resident(w.shape) for w in w_list]
        + [tok(D), _resident((1, D)), tok(D)],
        out_specs=[tok(D), pl.BlockSpec((1, D), lambda i: (0, 0))],
        out_shape=[jax.ShapeDtypeStruct((S, D), F32), jax.ShapeDtypeStruct((1, D), F32)],
        compiler_params=_params("arbitrary"),
    )(*a_list, *w_list, x, g, res)


def _matmul_tn(a, b, nb, block_on, name, a_square=False, tm=512):
    S, K = a.shape
    N = b.shape[1]
    tm = _token_tile(S, tm)
    if block_on == "b":
        ka, nbk = K, N // nb
        a_spec = pl.BlockSpec((tm, K), lambda d, i: (i, 0))
        b_spec = pl.BlockSpec((tm, nbk), lambda d, i: (i, d))
    else:
        ka, nbk = K // nb, N
        a_spec = pl.BlockSpec((tm, ka), lambda d, i: (i, d))
        b_spec = pl.BlockSpec((tm, N), lambda d, i: (i, 0))

    def body(a_ref, b_ref, o_ref):
        av = a_ref[...]
        if a_square:
            av = av.astype(F32)
            av = av * av
        part = _dot_tn(av.astype(BF16), b_ref[...].astype(BF16))

        @pl.when(pl.program_id(1) == 0)
        def _():
            o_ref[...] = part

        @pl.when(pl.program_id(1) > 0)
        def _():
            o_ref[...] += part

    return pl.pallas_call(
        body, name=name, grid=(nb, S // tm),
        in_specs=[a_spec, b_spec],
        out_specs=pl.BlockSpec((None, ka, nbk), lambda d, i: (d, 0, 0)),
        out_shape=jax.ShapeDtypeStruct((nb, ka, nbk), F32),
        compiler_params=_params("parallel", "arbitrary"),
    )(a, b)


def _mlp_fwd(x, g, w1, w2, name, tm=256):
    S, D = x.shape
    nb, _, fb = w1.shape
    tm = _token_tile(S, tm)

    def body(x_ref, g_ref, w1_ref, w2_ref, o_ref, h_ref, r_ref):
        xv = x_ref[...]
        h = (xv * _rms_scale(xv) * g_ref[...]).astype(BF16)
        h_ref[...] = h
        acc = xv
        for d in range(nb):
            r = jnp.maximum(_dot(h, w1_ref[d]), 0.0)
            r_ref[:, d * fb:(d + 1) * fb] = r.astype(BF16)
            acc = acc + _dot((r * r).astype(BF16), w2_ref[d])
        o_ref[...] = acc

    tok = lambda width: pl.BlockSpec((tm, width), lambda i: (i, 0))
    return pl.pallas_call(
        body, name=name, grid=(S // tm,),
        in_specs=[tok(D), _resident((1, D)), _resident(w1.shape), _resident(w2.shape)],
        out_specs=[tok(D), tok(D), tok(nb * fb)],
        out_shape=[jax.ShapeDtypeStruct((S, D), F32), jax.ShapeDtypeStruct((S, D), BF16),
                   jax.ShapeDtypeStruct((S, nb * fb), BF16)],
        compiler_params=_params("parallel"),
    )(x, g, w1, w2)


def _mlp_bwd(dout, x, g, r, w1, w2, name, tm=256):
    S, D = x.shape
    nb, _, fb = w1.shape
    tm = _token_tile(S, tm)

    def body(do_ref, x_ref, g_ref, r_ref, w1_ref, w2_ref, dx_ref, dg_ref, da_ref):
        dov = do_ref[...]
        dob = dov.astype(BF16)
        dh = jnp.zeros((tm, D), F32)
        for d in range(nb):
            dz = _dot_nt(dob, w2_ref[d])
            da = (dz * (2.0 * r_ref[:, d * fb:(d + 1) * fb].astype(F32))).astype(BF16)
            da_ref[:, d * fb:(d + 1) * fb] = da
            dh = dh + _dot_nt(da, w1_ref[d])
        dx, dg = _norm_bwd(dh, x_ref[...], g_ref[...])
        dx_ref[...] = dov + dx

        @pl.when(pl.program_id(0) == 0)
        def _():
            dg_ref[...] = jnp.zeros_like(dg_ref)
        dg_ref[...] += dg

    tok = lambda width: pl.BlockSpec((tm, width), lambda i: (i, 0))
    return pl.pallas_call(
        body, name=name, grid=(S // tm,),
        in_specs=[tok(D), tok(D), _resident((1, D)), tok(nb * fb), _resident(w1.shape),
                  _resident(w2.shape)],
        out_specs=[tok(D), pl.BlockSpec((1, D), lambda i: (0, 0)), tok(nb * fb)],
        out_shape=[jax.ShapeDtypeStruct((S, D), F32), jax.ShapeDtypeStruct((1, D), F32),
                   jax.ShapeDtypeStruct((S, nb * fb), BF16)],
        compiler_params=_params("arbitrary"),
    )(dout, x, g, r, w1, w2)


def _scan_chunk(a, b, row, T, reverse):
    s = 1
    while s < T:
        if reverse:
            keep, shift = row < T - s, T - s
        else:
            keep, shift = row >= s, s
        a_sh = jnp.where(keep, pltpu.roll(a, shift, 0), 1.0)
        b_sh = jnp.where(keep, pltpu.roll(b, shift, 0), 0.0)
        b = a * b_sh + b
        a = a * a_sh
        s *= 2
    return a, b


def _row_of(x, row, r):
    return jnp.sum(jnp.where(row == r, x, 0.0), axis=0, keepdims=True)


def _shift_down(x, prev, row, k):
    if k == 0:
        return x
    return jnp.where(row < k, pltpu.roll(prev, k, 0), pltpu.roll(x, k, 0))


def _shift_up(x, nxt, row, k, T):
    if k == 0:
        return x
    return jnp.where(row < T - k, pltpu.roll(x, T - k, 0), pltpu.roll(nxt, T - k, 0))


def _lru_gates(xb, prev_xb, row, cw_ref, cb, wr, br, wi, bi, ls):
    xc = cb + cw_ref[pl.ds(0, 1), :] * _shift_down(xb, prev_xb, row, 3)
    for k in (2, 1, 0):
        xc = xc + cw_ref[pl.ds(3 - k, 1), :] * _shift_down(xb, prev_xb, row, k)
    xcb = xc.astype(BF16)
    r = _sigmoid(_dot(xcb, wr) + br)
    i = _sigmoid(_dot(xcb, wi) + bi)
    la = (LRU_C * r) * ls
    a = jnp.exp(la)
    m = jnp.sqrt(-_expm1(2.0 * la))
    return xc, xcb, r, i, a, m


def _lru_specs(S):
    col = lambda off: pl.BlockSpec((S, LANES), lambda j: (0, j + off))
    vec = pl.BlockSpec((1, LANES), lambda j: (0, j))
    mat = pl.BlockSpec((None, LANES, LANES), lambda j: (j, 0, 0))
    cwm = pl.BlockSpec((CONV_WIDTH, LANES), lambda j: (0, j))
    return col, vec, mat, cwm


def _lru_fwd(u, conv_w, conv_b, wr, br, wi, bi, lam, name):
    S = u.shape[0]
    T = _token_tile(S, 256)
    col, vec, mat, cwm = _lru_specs(S)

    def body(gp_ref, xb_ref, cw_ref, cb_ref, wr_ref, br_ref, wi_ref, bi_ref, lam_ref,
             y_ref, hs_ref):
        row = lax.broadcasted_iota(jnp.int32, (T, LANES), 0)
        ls = _log_sigmoid(lam_ref[...])
        cb, br, bi = cb_ref[...], br_ref[...], bi_ref[...]
        wr, wi = wr_ref[...], wi_ref[...]

        def chunk(ci, carry):
            prev_xb, hc = carry
            rows = pl.ds(pl.multiple_of(ci * T, T), T)
            xb = xb_ref[rows, :]
            xc, _, _, i, a, m = _lru_gates(xb, prev_xb, row, cw_ref, cb, wr, br, wi, bi, ls)
            ca, cbv = _scan_chunk(a, m * (i * xc), row, T, reverse=False)
            h = ca * hc + cbv
            hs_ref[rows, :] = h
            y_ref[rows, :] = (_gelu(gp_ref[rows, :]) * h).astype(BF16)
            return xb, _row_of(h, row, T - 1)

        lax.fori_loop(0, S // T, chunk,
                      (jnp.zeros((T, LANES), F32), jnp.zeros((1, LANES), F32)))

    return pl.pallas_call(
        body, name=name, grid=(N_CBLK,),
        in_specs=[col(0), col(N_CBLK), cwm, vec, mat, vec, mat, vec, vec],
        out_specs=[col(0), col(0)],
        out_shape=[jax.ShapeDtypeStruct((S, D_MODEL), BF16), jax.ShapeDtypeStruct((S, D_MODEL), F32)],
        compiler_params=_params("parallel"),
    )(u, u, conv_w, conv_b, wr, br, wi, bi, lam)


def _lru_bwd(dy, u, hs, conv_w, conv_b, wr, br, wi, bi, lam, name):
    S = u.shape[0]
    T = _token_tile(S, 256)
    n_chunk = S // T
    col, vec, mat, cwm = _lru_specs(S)

    def body(dy_ref, gp_ref, xb_ref, hs_ref, cw_ref, cb_ref, wr_ref, br_ref, wi_ref, bi_ref,
             lam_ref, du_ref, dcw_ref, dcb_ref, dbr_ref, dbi_ref, dlam_ref, dwr_ref, dwi_ref):
        row = lax.broadcasted_iota(jnp.int32, (T, LANES), 0)
        lam = lam_ref[...]
        ls = _log_sigmoid(lam)
        cb, br, bi = cb_ref[...], br_ref[...], bi_ref[...]
        wr, wi = wr_ref[...], wi_ref[...]
        for ref in (dcw_ref, dcb_ref, dbr_ref, dbi_ref, dlam_ref, dwr_ref, dwi_ref):
            ref[...] = jnp.zeros_like(ref)

        def chunk(it, carry):
            g_next, dxc_next = carry
            ci = n_chunk - 1 - it
            rows = pl.ds(pl.multiple_of(ci * T, T), T)
            before = pl.ds(pl.multiple_of(jnp.maximum(ci - 1, 0) * T, T), T)
            first = ci == 0
            xb = xb_ref[rows, :]
            prev_xb = jnp.where(first, 0.0, xb_ref[before, :])
            xc, xcb, r, i, a, m = _lru_gates(xb, prev_xb, row, cw_ref, cb, wr, br, wi, bi, ls)
            h = hs_ref[rows, :]
            h_prev = _shift_down(h, jnp.where(first, 0.0, hs_ref[before, :]), row, 1)
            gp = gp_ref[rows, :]
            dyv = dy_ref[rows, :]
            du_ref[0, rows, :] = (dyv * h * _gelu_grad(gp)).astype(BF16)
            dh = dyv * _gelu(gp)
            ca, cbv = _scan_chunk(a, a * dh, row, T, reverse=True)
            gp_acc = ca * g_next + cbv
            g = dh + jnp.where(row < T - 1, pltpu.roll(gp_acc, T - 1, 0), g_next)
            da = g * h_prev - (g * (i * xc)) * a / m
            dla = da * a
            dlam_ref[...] += jnp.sum(dla * (LRU_C * r), axis=0, keepdims=True)
            dpr = (dla * (LRU_C * ls)) * r * (1.0 - r)
            dpi = (g * m * xc) * i * (1.0 - i)
            dbr_ref[...] += jnp.sum(dpr, axis=0, keepdims=True)
            dbi_ref[...] += jnp.sum(dpi, axis=0, keepdims=True)
            dprb, dpib = dpr.astype(BF16), dpi.astype(BF16)
            dwr_ref[...] += _dot_tn(xcb, dprb)
            dwi_ref[...] += _dot_tn(xcb, dpib)
            dxc = g * m * i + _dot_nt(dprb, wr) + _dot_nt(dpib, wi)
            dcb_ref[...] += jnp.sum(dxc, axis=0, keepdims=True)
            dxb = jnp.zeros((T, LANES), F32)
            for k in range(CONV_WIDTH):
                tap = pl.ds(CONV_WIDTH - 1 - k, 1)
                dcw_ref[tap, :] += jnp.sum(dxc * _shift_down(xb, prev_xb, row, k), axis=0,
                                           keepdims=True)
                dxb = dxb + cw_ref[tap, :] * _shift_up(dxc, dxc_next, row, k, T)
            du_ref[1, rows, :] = dxb.astype(BF16)
            return _row_of(gp_acc, row, 0), dxc

        lax.fori_loop(0, n_chunk, chunk,
                      (jnp.zeros((1, LANES), F32), jnp.zeros((T, LANES), F32)))
        dlam_ref[...] = dlam_ref[...] * _sigmoid(-lam)

    vec_out = jax.ShapeDtypeStruct((1, D_MODEL), F32)
    mat_out = jax.ShapeDtypeStruct((N_CBLK, LANES, LANES), F32)
    return pl.pallas_call(
        body, name=name, grid=(N_CBLK,),
        in_specs=[col(0), col(0), col(N_CBLK), col(0), cwm, vec, mat, vec, mat, vec, vec],
        out_specs=[pl.BlockSpec((2, S, LANES), lambda j: (0, 0, j)), cwm, vec, vec, vec, vec,
                   mat, mat],
        out_shape=[jax.ShapeDtypeStruct((2, S, D_MODEL), BF16),
                   jax.ShapeDtypeStruct((CONV_WIDTH, D_MODEL), F32),
                   vec_out, vec_out, vec_out, vec_out, mat_out, mat_out],
        compiler_params=_params("parallel"),
    )(dy, u, u, hs, conv_w, conv_b, wr, br, wi, bi, lam)


def _head_group_matrix(value):
    r = lax.broadcasted_iota(jnp.int32, (LANES, LANES), 0) // HEAD_DIM
    c = lax.broadcasted_iota(jnp.int32, (LANES, LANES), 1) // HEAD_DIM
    return jnp.where(r == c, value, 0.0).astype(BF16)


def _group_dot(x, p):
    hi = x.astype(BF16)
    lo = (x - hi.astype(F32)).astype(BF16)
    return _dot(hi, p) + _dot(lo, p)


def _head_mean(x, p):
    return _group_dot(x, p)


def _qk_prep(u, q_gain, k_gain, name, tm=512):
    S = u.shape[0]
    tm = _token_tile(S, tm)

    def body(q_ref, k_ref, v_ref, qg_ref, kg_ref, qn_ref, kn_ref, vb_ref):
        p = _head_group_matrix(1.0 / HEAD_DIM)
        for x_ref, g_ref, o_ref, scale in ((q_ref, qg_ref, qn_ref, ATTN_SCALE),
                                           (k_ref, kg_ref, kn_ref, 1.0)):
            xv = x_ref[...]
            rs = lax.rsqrt(_head_mean(xv * xv, p) + EPS)
            o_ref[...] = (xv * rs * g_ref[...]).astype(BF16) * scale
        vb_ref[...] = v_ref[...].astype(BF16)

    blk = lambda off: pl.BlockSpec((tm, LANES), lambda i, j: (i, j + off))
    out = jax.ShapeDtypeStruct((S, D_MODEL), BF16)
    return pl.pallas_call(
        body, name=name, grid=(S // tm, N_CBLK),
        in_specs=[blk(0), blk(N_CBLK), blk(2 * N_CBLK), _resident((1, LANES)),
                  _resident((1, LANES))],
        out_specs=[blk(0), blk(0), blk(0)],
        out_shape=[out, out, out],
        compiler_params=_params("parallel", "parallel"),
    )(u, u, u, q_gain, k_gain)


def _qk_bwd(u, dqn, dkn, q_gain, k_gain, name, tm=512):
    S = u.shape[0]
    tm = _token_tile(S, tm)

    def body(q_ref, k_ref, dqn_ref, dkn_ref, qg_ref, kg_ref, dq_ref, dk_ref, dqg_ref, dkg_ref):
        p = _head_group_matrix(1.0 / HEAD_DIM)
        first = (pl.program_id(0) == 0) & (pl.program_id(1) == 0)
        last = (pl.program_id(0) == S // tm - 1) & (pl.program_id(1) == N_CBLK - 1)
        for x_ref, dn_ref, g_ref, dx_ref, dg_ref, scale in (
                (q_ref, dqn_ref, qg_ref, dq_ref, dqg_ref, ATTN_SCALE),
                (k_ref, dkn_ref, kg_ref, dk_ref, dkg_ref, 1.0)):
            xv, dn = x_ref[...], dn_ref[...] * scale
            rs = lax.rsqrt(_head_mean(xv * xv, p) + EPS)
            xhat = xv * rs
            dxhat = dn * g_ref[...]
            dx_ref[...] = (rs * (dxhat - xhat * _head_mean(dxhat * xhat, p))).astype(BF16)

            @pl.when(first)
            def _():
                dg_ref[...] = jnp.zeros_like(dg_ref)
            dg_ref[...] += jnp.sum(dn * xhat, axis=0, keepdims=True)

            @pl.when(last)
            def _():
                dg_ref[...] += pltpu.roll(dg_ref[...], HEAD_DIM, 1)

    blk = lambda off: pl.BlockSpec((tm, LANES), lambda i, j: (i, j + off))
    acc = pl.BlockSpec((1, LANES), lambda i, j: (0, 0))
    out = jax.ShapeDtypeStruct((S, D_MODEL), BF16)
    vec = jax.ShapeDtypeStruct((1, LANES), F32)
    return pl.pallas_call(
        body, name=name, grid=(S // tm, N_CBLK),
        in_specs=[blk(0), blk(N_CBLK), blk(0), blk(0), _resident((1, LANES)),
                  _resident((1, LANES))],
        out_specs=[blk(0), blk(0), acc, acc],
        out_shape=[out, out, vec, vec],
        compiler_params=_params("arbitrary", "arbitrary"),
    )(u, u, dqn, dkn, q_gain, k_gain)


def _forget_fwd(f, b_f, name):
    S = f.shape[0]
    T = _token_tile(S, 256)

    def body(f_ref, b_ref, c_ref):
        row = lax.broadcasted_iota(jnp.int32, (T, LANES), 0)
        ones = jnp.ones((T, LANES), F32)
        bias = b_ref[...]

        def chunk(ci, carry):
            rows = pl.ds(pl.multiple_of(ci * T, T), T)
            _, c = _scan_chunk(ones, _log_sigmoid(f_ref[rows, :] + bias), row, T, reverse=False)
            c = c + carry
            c_ref[rows, :] = c
            return _row_of(c, row, T - 1)

        lax.fori_loop(0, S // T, chunk, jnp.zeros((1, LANES), F32))

    return pl.pallas_call(
        body, name=name,
        in_specs=[pl.BlockSpec(memory_space=pltpu.VMEM)] * 2,
        out_specs=pl.BlockSpec(memory_space=pltpu.VMEM),
        out_shape=jax.ShapeDtypeStruct((S, LANES), F32),
        compiler_params=pltpu.CompilerParams(vmem_limit_bytes=VMEM_LIMIT),
    )(f, b_f)


def _forget_bwd(dc_k, dc_q, f, b_f, name):
    S = f.shape[0]
    T = _token_tile(S, 256)
    n_chunk = S // T

    def body(dck_ref, dcq_ref, f_ref, b_ref, df_ref, db_ref):
        row = lax.broadcasted_iota(jnp.int32, (T, LANES), 0)
        ones = jnp.ones((T, LANES), F32)
        bias = b_ref[...]

        def chunk(it, carry):
            tail, db = carry
            rows = pl.ds(pl.multiple_of((n_chunk - 1 - it) * T, T), T)
            _, dlf = _scan_chunk(ones, dck_ref[rows, :] + dcq_ref[rows, :], row, T, reverse=True)
            dlf = dlf + tail
            df = dlf * _sigmoid(-(f_ref[rows, :] + bias))
            df_ref[rows, :] = df
            return _row_of(dlf, row, 0), db + jnp.sum(df, axis=0, keepdims=True)

        zero = jnp.zeros((1, LANES), F32)
        _, db = lax.fori_loop(0, n_chunk, chunk, (zero, zero))
        db_ref[...] = db

    return pl.pallas_call(
        body, name=name,
        in_specs=[pl.BlockSpec(memory_space=pltpu.VMEM)] * 4,
        out_specs=[pl.BlockSpec(memory_space=pltpu.VMEM)] * 2,
        out_shape=[jax.ShapeDtypeStruct((S, LANES), F32), jax.ShapeDtypeStruct((1, LANES), F32)],
        compiler_params=pltpu.CompilerParams(vmem_limit_bytes=VMEM_LIMIT),
    )(dc_k, dc_q, f, b_f)


ATTN_TILE = 512


def _attn_tiles(S):
    t = _token_tile(S, ATTN_TILE)
    return t, S // t


def _causal(T):
    return (lax.broadcasted_iota(jnp.int32, (T, T), 1)
            <= lax.broadcasted_iota(jnp.int32, (T, T), 0))


def _attn_fwd(qs_, kn, vb, c_row, name):
    S = qs_.shape[0]
    T, n_t = _attn_tiles(S)

    def body(q_ref, k_ref, v_ref, cr_ref, o_ref, lse_ref):
        qi = pl.program_id(1)
        causal = _causal(T)
        lanes = [slice(h2 * HEAD_DIM, (h2 + 1) * HEAD_DIM) for h2 in range(2)]
        qh = [q_ref[:, hl] for hl in lanes]

        def step(kj, carry, masked):
            ks = pl.ds(pl.multiple_of(kj * T, T), T)
            out = []
            for h2, hl in enumerate(lanes):
                m, l, acc = carry[h2]
                s = _dot_nt(qh[h2], k_ref[ks, hl]) - cr_ref[h2:h2 + 1, ks]
                if masked:
                    s = jnp.where(causal, s, NEG_INF)
                m_new = jnp.maximum(m, jnp.max(s, axis=1, keepdims=True))
                alpha = jnp.exp(m - m_new)
                p = jnp.exp(s - m_new)
                l = alpha * l + jnp.sum(p, axis=1, keepdims=True)
                acc = alpha * acc + _dot(p.astype(BF16), v_ref[ks, hl])
                out.append((m_new, l, acc))
            return tuple(out)

        init = tuple((jnp.full((T, 1), NEG_INF, F32), jnp.zeros((T, 1), F32),
                      jnp.zeros((T, HEAD_DIM), F32)) for _ in lanes)
        carry = lax.fori_loop(0, qi, lambda kj, c: step(kj, c, False), init)
        carry = step(qi, carry, True)
        for (m, l, acc), hl in zip(carry, lanes):
            o_ref[:, hl] = (acc / l).astype(BF16)
            lse_ref[:, hl] = jnp.broadcast_to(m + jnp.log(l), (T, HEAD_DIM))

    qblk = pl.BlockSpec((T, LANES), lambda h, i: (i, h))
    kv = pl.BlockSpec((S, LANES), lambda h, i: (0, h))
    return pl.pallas_call(
        body, name=name, grid=(N_CBLK, n_t),
        in_specs=[qblk, kv, kv, pl.BlockSpec((None, 2, S), lambda h, i: (h, 0, 0))],
        out_specs=[qblk, qblk],
        out_shape=[jax.ShapeDtypeStruct((S, D_MODEL), BF16),
                   jax.ShapeDtypeStruct((S, D_MODEL), F32)],
        compiler_params=_params("parallel", "parallel"),
    )(qs_, kn, vb, c_row)


def _attn_bwd(qs_, kn, vb, do, o, lse, c_row, name):
    S = qs_.shape[0]
    T, n_t = _attn_tiles(S)

    def body(q_ref, k_ref, v_ref, do_ref, o_ref, lse_ref, cr_ref,
             dq_ref, dk_ref, dv_ref, dc_ref, rho_ref, dd_ref):
        kj = pl.program_id(1)
        causal = _causal(T)
        lanes = [slice(h2 * HEAD_DIM, (h2 + 1) * HEAD_DIM) for h2 in range(2)]
        ones = [slice(h2 * HEAD_DIM, h2 * HEAD_DIM + 1) for h2 in range(2)]

        @pl.when(kj == 0)
        def _():
            dq_ref[...] = jnp.zeros_like(dq_ref)
            rho_ref[...] = jnp.zeros_like(rho_ref)
            p_sum = _head_group_matrix(1.0)

            def fill(ci, _):
                rows = pl.ds(pl.multiple_of(ci * T, T), T)
                dd_ref[rows, :] = _group_dot(do_ref[rows, :].astype(F32) * o_ref[rows, :].astype(F32),
                                             p_sum)
                return 0

            lax.fori_loop(0, n_t, fill, 0)

        kh = [k_ref[:, hl] for hl in lanes]
        vh = [v_ref[:, hl] for hl in lanes]
        ck = [cr_ref[h2:h2 + 1, :] for h2 in range(2)]

        def step(qi, carry, masked):
            qs = pl.ds(pl.multiple_of(qi * T, T), T)
            out = []
            for h2, hl in enumerate(lanes):
                dk, dv, dc = carry[h2]
                qh, doh = q_ref[qs, hl], do_ref[qs, hl]
                s = _dot_nt(qh, kh[h2]) - ck[h2]
                if masked:
                    s = jnp.where(causal, s, NEG_INF)
                p = jnp.exp(s - lse_ref[qs, ones[h2]])
                ds = p * (_dot_nt(doh, vh[h2]) - dd_ref[qs, ones[h2]])
                dsb = ds.astype(BF16)
                dq_ref[qs, hl] += _dot(dsb, kh[h2])
                rho_ref[qs, hl] += jnp.broadcast_to(jnp.sum(ds, axis=1, keepdims=True),
                                                    (T, HEAD_DIM))
                out.append((dk + _dot_tn(dsb, qh), dv + _dot_tn(p.astype(BF16), doh),
                            dc - jnp.sum(ds, axis=0, keepdims=True)))
            return tuple(out)

        init = tuple((jnp.zeros((T, HEAD_DIM), F32), jnp.zeros((T, HEAD_DIM), F32),
                      jnp.zeros((1, T), F32)) for _ in lanes)
        carry = step(kj, init, True)
        carry = lax.fori_loop(kj + 1, n_t, lambda qi, c: step(qi, c, False), carry)
        for h2, ((dk, dv, dc), hl) in enumerate(zip(carry, lanes)):
            dk_ref[:, hl] = dk
            dv_ref[:, hl] = dv.astype(BF16)
            dc_ref[h2:h2 + 1, :] = dc

    kblk = pl.BlockSpec((T, LANES), lambda h, j: (j, h))
    full = pl.BlockSpec((S, LANES), lambda h, j: (0, h))
    crow = pl.BlockSpec((None, 2, T), lambda h, j: (h, 0, j))
    wide = jax.ShapeDtypeStruct((S, D_MODEL), F32)
    return pl.pallas_call(
        body, name=name, grid=(N_CBLK, n_t),
        in_specs=[full, kblk, kblk, full, full, full, crow],
        out_specs=[full, kblk, kblk, crow, full],
        out_shape=[wide, wide, jax.ShapeDtypeStruct((S, D_MODEL), BF16),
                   jax.ShapeDtypeStruct((N_CBLK, 2, S), F32), wide],
        scratch_shapes=[pltpu.VMEM((S, LANES), F32)],
        compiler_params=_params("parallel", "arbitrary"),
    )(qs_, kn, vb, do, o, lse, c_row)


def _loss_head(y, target, name, tm=512):
    S, D = y.shape
    tm = _token_tile(S, tm)

    def body(y_ref, t_ref, loss_ref, dy_ref):
        err = y_ref[...] - t_ref[...]
        dy_ref[...] = err / D

        @pl.when(pl.program_id(0) == 0)
        def _():
            loss_ref[...] = jnp.zeros_like(loss_ref)
        row_loss = jnp.mean(err * err, axis=1, keepdims=True)
        loss_ref[...] += 0.5 * jnp.sum(row_loss, axis=0, keepdims=True)

    tok = pl.BlockSpec((tm, D), lambda i: (i, 0))
    return pl.pallas_call(
        body, name=name, grid=(S // tm,),
        in_specs=[tok, tok],
        out_specs=[pl.BlockSpec((1, 1), lambda i: (0, 0)), tok],
        out_shape=[jax.ShapeDtypeStruct((1, 1), F32), jax.ShapeDtypeStruct((S, D), F32)],
        compiler_params=_params("arbitrary"),
    )(y, target)


def _exchange(arrays, gathers, name):
    n = len(arrays)

    def body(*refs):
        ins, outs = refs[:n], refs[n:2 * n]
        send_sems, recv_sems, own_sems = refs[2 * n:]
        own = _own_copies(ins, outs, gathers, own_sems)
        for cp in own:
            cp.start()
        copies = _peer_copies(ins, outs, gathers, send_sems, recv_sems)
        for send, _ in copies:
            send.start()
        for send, arrival in copies:
            arrival.wait_recv()
            send.wait_send()
        for cp in own:
            cp.wait()

    hbm = pl.BlockSpec(memory_space=pl.ANY)
    return pl.pallas_call(
        body, name=name,
        in_specs=[hbm] * n, out_specs=[hbm] * n, out_shape=_landing_shapes(arrays, gathers),
        scratch_shapes=[pltpu.SemaphoreType.DMA((n * N_PEER,)),
                        pltpu.SemaphoreType.DMA((n * N_PEER,)),
                        pltpu.SemaphoreType.DMA((n,))],
        compiler_params=pltpu.CompilerParams(has_side_effects=True),
    )(*arrays)


N_PEER = N_DEV - 1


def _landing_shapes(arrays, gathers):
    return [jax.ShapeDtypeStruct((N_DEV,) + a.shape if g else a.shape, a.dtype)
            for a, g in zip(arrays, gathers)]


def _my_index():
    return 4 * lax.axis_index("x") + 2 * lax.axis_index("y") + lax.axis_index("c")


def _own_copies(srcs, lands, gathers, sems):
    me = _my_index()
    return [pltpu.make_async_copy(src if g else src.at[me], land.at[me], sems.at[a])
            for a, (src, land, g) in enumerate(zip(srcs, lands, gathers))]


def _peer_copies(srcs, lands, gathers, send_sems, recv_sems):
    x, y, c = lax.axis_index("x"), lax.axis_index("y"), lax.axis_index("c")
    me = 4 * x + 2 * y + c
    out = []
    for k in range(1, N_DEV):
        to = (1 - x if k & 4 else x, 1 - y if k & 2 else y, 1 - c if k & 1 else c)
        peer = 4 * to[0] + 2 * to[1] + to[2]
        for a, (src, land, g) in enumerate(zip(srcs, lands, gathers)):
            sem = a * N_PEER + k - 1
            src_blk = src if g else src.at[peer]

            def copy(slot, src_blk=src_blk, land=land, sem=sem, to=to):
                return pltpu.make_async_remote_copy(
                    src_ref=src_blk, dst_ref=land.at[slot], send_sem=send_sems.at[sem],
                    recv_sem=recv_sems.at[sem], device_id=to,
                    device_id_type=pl.DeviceIdType.MESH)

            out.append((copy(me), copy(peer)))
    return out


def _place_own(arrays, gathers, name):
    n = len(arrays)

    def body(*refs):
        own = _own_copies(refs[:n], refs[n:2 * n], gathers, refs[2 * n])
        for cp in own:
            cp.start()
        for cp in own:
            cp.wait()

    hbm = pl.BlockSpec(memory_space=pl.ANY)
    return pl.pallas_call(
        body, name=name,
        in_specs=[hbm] * n, out_specs=[hbm] * n, out_shape=_landing_shapes(arrays, gathers),
        scratch_shapes=[pltpu.SemaphoreType.DMA((n,))],
        compiler_params=pltpu.CompilerParams(has_side_effects=True),
    )(*arrays)


_HBM = pl.BlockSpec(memory_space=pltpu.HBM)
_SEM = pl.BlockSpec(memory_space=pltpu.SEMAPHORE)
_ANY = pl.BlockSpec(memory_space=pl.ANY)
_DATAFLOW = pltpu.SideEffectType.DATAFLOW_SIDE_EFFECTING


def _in_hbm(a):
    return pltpu.with_memory_space_constraint(a, pltpu.HBM)


def _exchange_start(arrays, lands, gathers, after, name):
    n = len(arrays)

    def body(*refs):
        srcs, dsts = refs[:n], refs[n:2 * n]
        send_sems, recv_sems = refs[2 * n + 1], refs[2 * n + 2]
        token = refs[-1]
        for send, _ in _peer_copies(srcs, dsts, gathers, send_sems, recv_sems):
            send.start()
        token[...] = jnp.zeros_like(token)

    hbm_like = [pltpu.HBM(a.shape, a.dtype) for a in list(arrays) + list(lands)]
    res = pl.pallas_call(
        body, name=name,
        in_specs=[_HBM] * (2 * n) + [_ANY],
        out_specs=(_SEM, _SEM, *[_HBM] * (2 * n), pl.BlockSpec(memory_space=pltpu.VMEM)),
        out_shape=(pltpu.SemaphoreType.DMA((n * N_PEER,)), pltpu.SemaphoreType.DMA((n * N_PEER,)),
                   *hbm_like, jax.ShapeDtypeStruct((8, LANES), F32)),
        input_output_aliases={i: 2 + i for i in range(2 * n)},
        compiler_params=pltpu.CompilerParams(has_side_effects=_DATAFLOW),
    )(*[_in_hbm(a) for a in list(arrays) + list(lands)], after)
    return res[0], res[1], res[2:2 + n], res[2 + n:2 + 2 * n], res[-1]


def _exchange_wait(started, gathers, after, name):
    send_sems, recv_sems, arrays, lands = started
    n = len(arrays)

    def body(*refs):
        srcs, dsts = refs[:n], refs[n:2 * n]
        for send, arrival in _peer_copies(srcs, dsts, gathers, refs[2 * n], refs[2 * n + 1]):
            arrival.wait_recv()
            send.wait_send()

    hbm_like = [pltpu.HBM(a.shape, a.dtype) for a in list(arrays) + list(lands)]
    res = pl.pallas_call(
        body, name=name,
        in_specs=[_HBM] * (2 * n) + [_SEM, _SEM, _ANY],
        out_specs=[_HBM] * (2 * n), out_shape=hbm_like,
        input_output_aliases={i: i for i in range(2 * n)},
        compiler_params=pltpu.CompilerParams(has_side_effects=_DATAFLOW),
    )(*arrays, *lands, send_sems, recv_sems, after)
    return res[n:]


def _reduce_adamw(parts, w, m, v, name):
    n, R, C = parts.shape
    tr = 256 if R % 256 == 0 else R

    def body(p_ref, w_ref, m_ref, v_ref, g_ref, d_ref, nm_ref, nv_ref):
        g = p_ref[0].astype(F32)
        for s in range(1, n):
            g = g + p_ref[s].astype(F32)
        g_ref[...] = g
        m_new = ADAM_B1 * m_ref[...] + (1.0 - ADAM_B1) * g
        v_new = ADAM_B2 * v_ref[...] + (1.0 - ADAM_B2) * (g * g)
        nm_ref[...] = m_new
        nv_ref[...] = v_new
        m_hat = m_new / (1.0 - ADAM_B1 ** ADAM_STEP)
        v_hat = v_new / (1.0 - ADAM_B2 ** ADAM_STEP)
        d_ref[...] = -ADAM_LR * (m_hat / (jnp.sqrt(v_hat) + ADAM_EPS) + ADAM_WD * w_ref[...])

    blk = pl.BlockSpec((tr, C), lambda i: (i, 0))
    out = jax.ShapeDtypeStruct((R, C), F32)
    return pl.pallas_call(
        body, name=name, grid=(R // tr,),
        in_specs=[pl.BlockSpec((n, tr, C), lambda i: (0, i, 0)), blk, blk, blk],
        out_specs=[blk] * 4, out_shape=[out] * 4,
        compiler_params=_params("parallel"),
    )(parts, w, m, v)


def _pack(arrays):
    flat = jnp.concatenate([a.reshape(-1).astype(F32) for a in arrays])
    pad = (-flat.shape[0]) % (8 * LANES)
    return jnp.pad(flat, (0, pad)).reshape(-1, LANES)


def _unpack(buf, shapes):
    flat = buf.reshape(-1)
    out, off = [], 0
    for shp in shapes:
        size = 1
        for s in shp:
            size *= s
        out.append(flat[off:off + size].reshape(shp))
        off += size
    return out


def _block_diag_pairs(w):
    w = w.reshape(N_CBLK, 2, LRU_BLOCK_DIM, LRU_BLOCK_DIM)
    z = jnp.zeros_like(w[:, 0])
    top = jnp.concatenate([w[:, 0], z], axis=2)
    bot = jnp.concatenate([z, w[:, 1]], axis=2)
    return jnp.concatenate([top, bot], axis=1)


def _diag_pairs(m):
    h = LRU_BLOCK_DIM
    return jnp.stack([m[:, :h, :h], m[:, h:, h:]], axis=1).reshape(2 * N_CBLK, h, h)


SMALL = ("mix_norm", "mlp_norm", "lru_conv_b", "lru_w_r", "lru_b_r", "lru_w_i", "lru_b_i",
         "lru_lambda", "fox_b_f", "fox_q_gain", "fox_k_gain")
WEIGHTS = ("mix_norm", "mlp_norm", "mlp_w1", "mlp_w2", "lru_w_in", "lru_conv_w", "lru_conv_b",
           "lru_w_r", "lru_b_r", "lru_w_i", "lru_b_i", "lru_lambda", "lru_w_out", "fox_w_in",
           "fox_b_f", "fox_q_gain", "fox_k_gain", "fox_w_out")


def kernel(x, mix_norm, mlp_norm, mlp_w1, mlp_w2, lru_w_in, lru_conv_w, lru_conv_b, lru_w_r, lru_b_r, lru_w_i, lru_b_i, lru_lambda, lru_w_out, fox_w_in, fox_b_f, fox_q_gain, fox_k_gain, fox_w_out, loss_target, m_mix_norm, m_mlp_norm, m_mlp_w1, m_mlp_w2, m_lru_w_in, m_lru_conv_w, m_lru_conv_b, m_lru_w_r, m_lru_b_r, m_lru_w_i, m_lru_b_i, m_lru_lambda, m_lru_w_out, m_fox_w_in, m_fox_b_f, m_fox_q_gain, m_fox_k_gain, m_fox_w_out, v_mix_norm, v_mlp_norm, v_mlp_w1, v_mlp_w2, v_lru_w_in, v_lru_conv_w, v_lru_conv_b, v_lru_w_r, v_lru_b_r, v_lru_w_i, v_lru_b_i, v_lru_lambda, v_lru_w_out, v_fox_w_in, v_fox_b_f, v_fox_q_gain, v_fox_k_gain, v_fox_w_out):
    w_in = dict(mix_norm=mix_norm, mlp_norm=mlp_norm, mlp_w1=mlp_w1, mlp_w2=mlp_w2,
                lru_w_in=lru_w_in, lru_conv_w=lru_conv_w, lru_conv_b=lru_conv_b, lru_w_r=lru_w_r,
                lru_b_r=lru_b_r, lru_w_i=lru_w_i, lru_b_i=lru_b_i, lru_lambda=lru_lambda,
                lru_w_out=lru_w_out, fox_w_in=fox_w_in, fox_b_f=fox_b_f, fox_q_gain=fox_q_gain,
                fox_k_gain=fox_k_gain, fox_w_out=fox_w_out)
    m_in = dict(mix_norm=m_mix_norm, mlp_norm=m_mlp_norm, mlp_w1=m_mlp_w1, mlp_w2=m_mlp_w2,
                lru_w_in=m_lru_w_in, lru_conv_w=m_lru_conv_w, lru_conv_b=m_lru_conv_b,
                lru_w_r=m_lru_w_r, lru_b_r=m_lru_b_r, lru_w_i=m_lru_w_i, lru_b_i=m_lru_b_i,
                lru_lambda=m_lru_lambda, lru_w_out=m_lru_w_out, fox_w_in=m_fox_w_in,
                fox_b_f=m_fox_b_f, fox_q_gain=m_fox_q_gain, fox_k_gain=m_fox_k_gain,
                fox_w_out=m_fox_w_out)
    v_in = dict(mix_norm=v_mix_norm, mlp_norm=v_mlp_norm, mlp_w1=v_mlp_w1, mlp_w2=v_mlp_w2,
                lru_w_in=v_lru_w_in, lru_conv_w=v_lru_conv_w, lru_conv_b=v_lru_conv_b,
                lru_w_r=v_lru_w_r, lru_b_r=v_lru_b_r, lru_w_i=v_lru_w_i, lru_b_i=v_lru_b_i,
                lru_lambda=v_lru_lambda, lru_w_out=v_lru_w_out, fox_w_in=v_fox_w_in,
                fox_b_f=v_fox_b_f, fox_q_gain=v_fox_q_gain, fox_k_gain=v_fox_k_gain,
                fox_w_out=v_fox_w_out)
    D = D_MODEL
    S = x.shape[1]
    x0, target = x[0], loss_target[0]
    me = 4 * lax.axis_index("x") + 2 * lax.axis_index("y") + lax.axis_index("c")

    def begin(arrays, gathers, after, name):
        lands = _place_own(arrays, gathers, name + "_own")
        return _exchange_start(arrays, lands, gathers, after, name + "_start")

    def bf16(a):
        return a.astype(BF16)

    lru_in_g, lru_out_g, conv_g = _exchange(
        [bf16(lru_w_in[0]), bf16(lru_w_out[0]), lru_conv_w[0]], [True] * 3, "gather_lru")
    gather_mlp0 = begin([bf16(mlp_w1[0]), bf16(mlp_w2[0])], [True] * 2, lru_in_g, "gather_mlp0")
    gather_fox = begin([bf16(fox_w_in[0]), bf16(fox_w_out[0])], [True] * 2, gather_mlp0[4],
                       "gather_fox")
    gather_mlp1 = begin([bf16(mlp_w1[1]), bf16(mlp_w2[1])], [True] * 2, gather_fox[4],
                        "gather_mlp1")
    lru_out_w = lru_out_g.reshape(D, D)
    conv_w = conv_g.transpose(1, 0, 2).reshape(CONV_WIDTH, D)
    wr = _block_diag_pairs(lru_w_r[0]).astype(BF16)
    wi = _block_diag_pairs(lru_w_i[0]).astype(BF16)
    b_r, b_i = lru_b_r.reshape(1, D), lru_b_i.reshape(1, D)
    q_gain, k_gain = jnp.tile(fox_q_gain, (1, 2)), jnp.tile(fox_k_gain, (1, 2))
    b_f = jnp.pad(fox_b_f, ((0, 0), (0, LANES - N_HEADS)))
    g_mix0, g_mix1 = mix_norm[0:1] + gather_mlp1[4][0, 0], mix_norm[1:2]
    g_mlp0, g_mlp1 = mlp_norm[0:1], mlp_norm[1:2]

    (u0,), h0 = _norm_matmul(x0, g_mix0, [lru_in_g], "lru_in_proj")
    y_lru, hs = _lru_fwd(u0, conv_w, lru_conv_b, wr, b_r, wi, b_i, lru_lambda, "lru_core")
    x1 = _matmul_res(y_lru, lru_out_w, x0, "lru_out_proj")
    w1g0, w2g0 = _exchange_wait(gather_mlp0[:4], [True] * 2, x1, "gather_mlp0_wait")
    x2, h1, r1 = _mlp_fwd(x1, g_mlp0, w1g0, w2g0, "mlp0")
    fox_in_g, fox_out_g = _exchange_wait(gather_fox[:4], [True] * 2, x2, "gather_fox_wait")
    fox_out_w = fox_out_g.reshape(D, D)
    fox_full = fox_in_g.transpose(1, 0, 2).reshape(D, 3 * D + N_HEADS)
    wqkv = fox_full[:, :3 * D].reshape(D, 3, D).transpose(1, 0, 2)
    wf = jnp.pad(fox_full[:, 3 * D:], ((0, 0), (0, LANES - N_HEADS)))[None]
    (u_qkv, f), h2 = _norm_matmul(x2, g_mix1, [wqkv, wf], "fox_in_proj")
    qn, kn, vb = _qk_prep(u_qkv, q_gain, k_gain, "fox_qk_norm")
    c_col = _forget_fwd(f, b_f, "fox_forget")
    c_row = c_col[:, :N_HEADS].T.reshape(N_CBLK, 2, S)
    o, lse = _attn_fwd(qn, kn, vb, c_row, "fox_attn")
    x3 = _matmul_res(o, fox_out_w, x2, "fox_out_proj")
    w1g1, w2g1 = _exchange_wait(gather_mlp1[:4], [True] * 2, x3, "gather_mlp1_wait")
    x4, h3, r3 = _mlp_fwd(x3, g_mlp1, w1g1, w2g1, "mlp1")
    loss_local, dx4 = _loss_head(x4, target, "loss_head")

    dx3, dg_mlp1, da3 = _mlp_bwd(dx4, x3, g_mlp1, r3, w1g1, w2g1, "mlp1_bwd")
    dw1_1 = _matmul_tn(h3, da3, N_DEV, "b", "mlp1_dw1")
    dw2_1 = _matmul_tn(r3, dx4, N_DEV, "a", "mlp1_dw2", a_square=True)
    grads_mlp1 = begin([bf16(dw1_1), bf16(dw2_1)], [False] * 2, dw1_1, "grads_mlp1")
    do = _matmul_nt(dx3, fox_out_w, "fox_out_bwd", BF16, grads_mlp1[4])
    d_fox_out = _matmul_tn(o, dx3, N_DEV, "a", "fox_out_dw")
    dqn, dkn, dv, dc_row, rho = _attn_bwd(qn, kn, vb, do, o, lse, c_row, "fox_attn_bwd")
    duq, duk, dq_gain, dk_gain = _qk_bwd(u_qkv, dqn, dkn, q_gain, k_gain, "fox_qk_norm_bwd")
    head_pad = ((0, 0), (0, LANES - N_HEADS))
    dc_k = jnp.pad(dc_row.reshape(N_HEADS, S).T, head_pad)
    dc_q = jnp.pad(rho[:, ::HEAD_DIM], head_pad)
    df, db_f = _forget_bwd(dc_k, dc_q, f, b_f, "fox_forget_bwd")
    dx2, dg_mix1 = _proj_bwd([duq, duk, dv, df], [wqkv[0:1], wqkv[1:2], wqkv[2:3], wf],
                             x2, g_mix1, dx3, "fox_in_bwd")
    d_fox_in = jnp.concatenate(
        [_matmul_tn(h2, duq, 1, "b", "fox_in_dwq")[0], _matmul_tn(h2, duk, 1, "b", "fox_in_dwk")[0],
         _matmul_tn(h2, dv, 1, "b", "fox_in_dwv")[0],
         _matmul_tn(h2, df, 1, "b", "fox_in_dwf")[0][:, :N_HEADS]], axis=1)
    d_fox_in = d_fox_in.reshape(D, N_DEV, -1).transpose(1, 0, 2)
    grads_fox = begin([bf16(d_fox_in), bf16(d_fox_out)], [False] * 2, d_fox_in, "grads_fox")
    dx1, dg_mlp0, da1 = _mlp_bwd(dx2, x1, g_mlp0 + grads_fox[4][0, 0], r1, w1g0, w2g0, "mlp0_bwd")
    dw1_0 = _matmul_tn(h1, da1, N_DEV, "b", "mlp0_dw1")
    dw2_0 = _matmul_tn(r1, dx2, N_DEV, "a", "mlp0_dw2", a_square=True)
    grads_mlp0 = begin([bf16(dw1_0), bf16(dw2_0)], [False] * 2, dw1_0, "grads_mlp0")
    dy_lru = _matmul_nt(dx1, lru_out_w, "lru_out_bwd", F32, grads_mlp0[4])
    d_lru_out = _matmul_tn(y_lru, dx1, N_DEV, "a", "lru_out_dw")
    du0, d_conv_w, d_conv_b, d_b_r, d_b_i, d_lam, d_wr, d_wi = _lru_bwd(
        dy_lru, u0, hs, conv_w, lru_conv_b, wr, b_r, wi, b_i, lru_lambda, "lru_core_bwd")
    dx0, dg_mix0 = _proj_bwd([du0[0], du0[1]], [lru_in_g[:4], lru_in_g[4:]], x0, g_mix0, dx1,
                             "lru_in_bwd")
    d_lru_in = jnp.concatenate([_matmul_tn(h0, du0[0], 4, "b", "lru_in_dw_gate"),
                                _matmul_tn(h0, du0[1], 4, "b", "lru_in_dw_x")], axis=0)

    small_grads = dict(
        mix_norm=jnp.concatenate([dg_mix0, dg_mix1], axis=0),
        mlp_norm=jnp.concatenate([dg_mlp0, dg_mlp1], axis=0),
        lru_conv_b=d_conv_b, lru_w_r=_diag_pairs(d_wr), lru_b_r=d_b_r, lru_w_i=_diag_pairs(d_wi),
        lru_b_i=d_b_i, lru_lambda=d_lam, fox_b_f=db_f[:, :N_HEADS],
        fox_q_gain=dq_gain[:, :HEAD_DIM], fox_k_gain=dk_gain[:, :HEAD_DIM])
    small_partial = _pack([small_grads[n] for n in SMALL] + [d_conv_w])
    p_lru_in, p_lru_out, p_small = _exchange(
        [bf16(d_lru_in), bf16(d_lru_out), small_partial], [False, False, True], "exchange_lru_small")
    p_w1_1, p_w2_1 = _exchange_wait(grads_mlp1[:4], [False] * 2, p_small, "grads_mlp1_wait")
    p_fox_in, p_fox_out = _exchange_wait(grads_fox[:4], [False] * 2, p_w1_1, "grads_fox_wait")
    p_w1_0, p_w2_0 = _exchange_wait(grads_mlp0[:4], [False] * 2, p_fox_in, "grads_mlp0_wait")

    grads, deltas, new_m, new_v = {}, {}, {}, {}

    def update(name, parts, sel=None):
        w, m, v = w_in[name], m_in[name], v_in[name]
        if sel is not None:
            w, m, v = w[sel], m[sel], v[sel]
        shape = w.shape
        two_d = (-1, shape[-1])
        res = _reduce_adamw(parts.reshape((N_DEV,) + w.reshape(two_d).shape), w.reshape(two_d),
                            m.reshape(two_d), v.reshape(two_d),
                            "adamw_" + name + ("" if sel is None else "_%d" % sel))
        return [r.reshape(shape) for r in res]

    def store(name, res):
        grads[name], deltas[name], new_m[name], new_v[name] = res

    store("mlp_w1", [jnp.stack(p) for p in zip(update("mlp_w1", p_w1_0, 0),
                                               update("mlp_w1", p_w1_1, 1))])
    store("mlp_w2", [jnp.stack(p) for p in zip(update("mlp_w2", p_w2_0, 0),
                                               update("mlp_w2", p_w2_1, 1))])
    store("lru_w_in", update("lru_w_in", p_lru_in))
    store("lru_w_out", update("lru_w_out", p_lru_out))
    store("fox_w_in", update("fox_w_in", p_fox_in))
    store("fox_w_out", update("fox_w_out", p_fox_out))

    small_shapes = [w_in[n].shape for n in SMALL]
    n_small = sum(math.prod(s) for s in small_shapes)
    res_small = _reduce_adamw(p_small, _pack([w_in[n] for n in SMALL] + [jnp.zeros((CONV_WIDTH, D))]),
                              _pack([m_in[n] for n in SMALL] + [jnp.zeros((CONV_WIDTH, D))]),
                              _pack([v_in[n] for n in SMALL] + [jnp.zeros((CONV_WIDTH, D))]),
                              "adamw_small")
    for name, *vals in zip(SMALL, *[_unpack(r, small_shapes) for r in res_small]):
        store(name, vals)
    conv_parts = p_small.reshape(N_DEV, -1)[:, n_small:n_small + CONV_WIDTH * D]
    conv_parts = conv_parts.reshape(N_DEV, CONV_WIDTH, N_DEV, LANES)
    conv_parts = lax.dynamic_index_in_dim(conv_parts, me, axis=2, keepdims=False)
    store("lru_conv_w", update("lru_conv_w", conv_parts))

    loss = lax.psum(loss_local[0, 0], ("x", "y", "c"))
    return (loss, dx0[None], *[grads[n] for n in WEIGHTS], *[deltas[n] for n in WEIGHTS],
            *[new_m[n] for n in WEIGHTS], *[new_v[n] for n in WEIGHTS])
```

```python
import math

import jax
import jax.numpy as jnp
from jax import lax
from jax.experimental import pallas as pl
from jax.experimental.pallas import tpu as pltpu

F32 = jnp.float32
BF16 = jnp.bfloat16

N_DEV = 8
D_MODEL = 1024
D_FF = 4096
N_HEADS = 16
HEAD_DIM = 64
LRU_BLOCK_DIM = 64
CONV_WIDTH = 4
LRU_C = 8.0
EPS = 1e-6
NEG_INF = -1e30
ATTN_SCALE = HEAD_DIM ** -0.5
LANES = 128
N_CBLK = D_MODEL // LANES
VMEM_LIMIT = 52 * 2 ** 20

ADAM_LR = 0.001
ADAM_B1 = 0.9
ADAM_B2 = 0.999
ADAM_EPS = 1e-08
ADAM_WD = 0.01
ADAM_STEP = 10

_NT = (((1,), (1,)), ((), ()))
_TN = (((0,), (0,)), ((), ()))


def _params(*sem):
    return pltpu.CompilerParams(dimension_semantics=sem, vmem_limit_bytes=VMEM_LIMIT)


def _resident(shape):
    zeros = (0,) * len(shape)
    return pl.BlockSpec(shape, lambda *_: zeros, pipeline_mode=pl.Buffered(1))


def _dot(a, b):
    return jnp.dot(a, b, preferred_element_type=F32)


def _dot_nt(a, b):
    return lax.dot_general(a, b, _NT, preferred_element_type=F32)


def _dot_tn(a, b):
    return lax.dot_general(a, b, _TN, preferred_element_type=F32)


def _sigmoid(x):
    return 1.0 / (1.0 + jnp.exp(-x))


def _log_sigmoid(x):
    return -(jnp.maximum(-x, 0.0) + jnp.log1p(jnp.exp(-jnp.abs(x))))


def _expm1(x):
    poly = x * (1.0 + x * (0.5 + x * (1.0 / 6.0 + x * (1.0 / 24.0 + x * (1.0 / 120.0)))))
    return jnp.where(jnp.abs(x) < 0.1, poly, jnp.exp(x) - 1.0)


_GELU_K = 0.7978845608028654


def _gelu(x):
    return 0.5 * x * (1.0 + jnp.tanh(_GELU_K * (x + 0.044715 * (x * x * x))))


def _gelu_grad(x):
    t = jnp.tanh(_GELU_K * (x + 0.044715 * (x * x * x)))
    return 0.5 * (1.0 + t) + 0.5 * x * (1.0 - t * t) * (_GELU_K * (1.0 + 3 * 0.044715 * x * x))


def _rms_scale(x):
    return lax.rsqrt(jnp.mean(x * x, axis=-1, keepdims=True) + EPS)


def _norm_bwd(dh, x, g):
    rs = _rms_scale(x)
    xhat = x * rs
    dxhat = dh * g
    dx = rs * (dxhat - xhat * jnp.mean(dxhat * xhat, axis=-1, keepdims=True))
    return dx, jnp.sum(dh * xhat, axis=0, keepdims=True)


def _token_tile(S, want):
    tm = min(S, want)
    assert S % tm == 0
    return tm


def _norm_matmul(x, g, ws, name, tm=256):
    S, D = x.shape
    tm = _token_tile(S, tm)
    n = len(ws)

    def body(x_ref, g_ref, *refs):
        w_refs, o_refs, h_ref = refs[:n], refs[n:2 * n], refs[2 * n]
        xv = x_ref[...]
        h = (xv * _rms_scale(xv) * g_ref[...]).astype(BF16)
        h_ref[...] = h
        for w_ref, o_ref in zip(w_refs, o_refs):
            nb, _, nw = w_ref.shape
            for d in range(nb):
                o_ref[:, d * nw:(d + 1) * nw] = _dot(h, w_ref[d])

    widths = [w.shape[0] * w.shape[2] for w in ws]
    outs = pl.pallas_call(
        body, name=name, grid=(S // tm,),
        in_specs=[pl.BlockSpec((tm, D), lambda i: (i, 0)), _resident((1, D))]
        + [_resident(w.shape) for w in ws],
        out_specs=[pl.BlockSpec((tm, n_), lambda i: (i, 0)) for n_ in widths]
        + [pl.BlockSpec((tm, D), lambda i: (i, 0))],
        out_shape=[jax.ShapeDtypeStruct((S, n_), F32) for n_ in widths]
        + [jax.ShapeDtypeStruct((S, D), BF16)],
        compiler_params=_params("parallel"),
    )(x, g, *ws)
    return outs[:n], outs[n]


def _matmul_res(a, w, res, name, tm=512):
    S, K = a.shape
    N = w.shape[1]
    tm = _token_tile(S, tm)

    def body(a_ref, w_ref, r_ref, o_ref):
        o_ref[...] = r_ref[...] + _dot(a_ref[...], w_ref[...])

    return pl.pallas_call(
        body, name=name, grid=(S // tm,),
        in_specs=[pl.BlockSpec((tm, K), lambda i: (i, 0)), _resident((K, N)),
                  pl.BlockSpec((tm, N), lambda i: (i, 0))],
        out_specs=pl.BlockSpec((tm, N), lambda i: (i, 0)),
        out_shape=jax.ShapeDtypeStruct((S, N), F32),
        compiler_params=_params("parallel"),
    )(a, w, res)


def _matmul_nt(a, w, name, out_dtype, after, tm=512):
    S, N = a.shape
    K = w.shape[0]
    tm = _token_tile(S, tm)

    def body(a_ref, w_ref, after_ref, o_ref):
        o_ref[...] = _dot_nt(a_ref[...].astype(BF16), w_ref[...]).astype(out_dtype)

    return pl.pallas_call(
        body, name=name, grid=(S // tm,),
        in_specs=[pl.BlockSpec((tm, N), lambda i: (i, 0)), _resident((K, N)),
                  pl.BlockSpec(memory_space=pl.ANY)],
        out_specs=pl.BlockSpec((tm, K), lambda i: (i, 0)),
        out_shape=jax.ShapeDtypeStruct((S, K), out_dtype),
        compiler_params=_params("parallel"),
    )(a, w, after)


def _proj_bwd(a_list, w_list, x, g, res, name, tm=256):
    S, D = x.shape
    tm = _token_tile(S, tm)
    n = len(a_list)

    def body(*refs):
        a_refs, w_refs = refs[:n], refs[n:2 * n]
        x_ref, g_ref, r_ref, dx_ref, dg_ref = refs[2 * n:]
        dh = jnp.zeros((tm, D), F32)
        for a_ref, w_ref in zip(a_refs, w_refs):
            nb, _, nw = w_ref.shape
            for d in range(nb):
                dh = dh + _dot_nt(a_ref[:, d * nw:(d + 1) * nw].astype(BF16), w_ref[d])
        dx, dg = _norm_bwd(dh, x_ref[...], g_ref[...])
        dx_ref[...] = r_ref[...] + dx

        @pl.when(pl.program_id(0) == 0)
        def _():
            dg_ref[...] = jnp.zeros_like(dg_ref)
        dg_ref[...] += dg

    tok = lambda width: pl.BlockSpec((tm, width), lambda i: (i, 0))
    return pl.pallas_call(
        body, name=name, grid=(S // tm,),
        in_specs=[tok(a.shape[1]) for a in a_list] + [_resident(w.shape) for w in w_list]
        + [tok(D), _resident((1, D)), tok(D)],
        out_specs=[tok(D), pl.BlockSpec((1, D), lambda i: (0, 0))],
        out_shape=[jax.ShapeDtypeStruct((S, D), F32), jax.ShapeDtypeStruct((1, D), F32)],
        compiler_params=_params("arbitrary"),
    )(*a_list, *w_list, x, g, res)


def _matmul_tn(a, b, name, rows=1, cols=1, col_blocks=None, a_square=False, tm=512):
    S, K = a.shape
    N = b.shape[1]
    tm = _token_tile(S, tm)
    n_tok = S // tm
    kr, nc = K // rows, N // cols

    def body(a_ref, b_ref, o_ref, acc_ref):
        av = a_ref[...]
        if a_square:
            av = av.astype(F32)
            av = av * av
        part = _dot_tn(av.astype(BF16), b_ref[...].astype(BF16))
        step = pl.program_id(2)

        @pl.when(step == 0)
        def _():
            acc_ref[...] = part

        @pl.when(step > 0)
        def _():
            acc_ref[...] += part

        @pl.when(step == n_tok - 1)
        def _():
            if col_blocks is None:
                o_ref[...] = acc_ref[...].astype(BF16)
            else:
                nw = N // col_blocks
                for d in range(col_blocks // cols):
                    o_ref[d] = acc_ref[:, d * nw:(d + 1) * nw].astype(BF16)

    if col_blocks is None:
        out_spec = pl.BlockSpec((kr, nc), lambda r, c, i: (r, c))
        out_shape = jax.ShapeDtypeStruct((K, N), BF16)
    else:
        assert rows == 1 and col_blocks % cols == 0
        per = col_blocks // cols
        out_spec = pl.BlockSpec((per, K, N // col_blocks), lambda r, c, i: (c, 0, 0))
        out_shape = jax.ShapeDtypeStruct((col_blocks, K, N // col_blocks), BF16)
    return pl.pallas_call(
        body, name=name, grid=(rows, cols, n_tok),
        in_specs=[pl.BlockSpec((tm, kr), lambda r, c, i: (i, r)),
                  pl.BlockSpec((tm, nc), lambda r, c, i: (i, c))],
        out_specs=out_spec, out_shape=out_shape,
        scratch_shapes=[pltpu.VMEM((kr, nc), F32)],
        compiler_params=_params("parallel", "parallel", "arbitrary"),
    )(a, b)


def _mlp_fwd(x, g, w1, w2, name, tm=256):
    S, D = x.shape
    nb, _, fb = w1.shape
    tm = _token_tile(S, tm)

    def body(x_ref, g_ref, w1_ref, w2_ref, o_ref, h_ref, r_ref):
        xv = x_ref[...]
        h = (xv * _rms_scale(xv) * g_ref[...]).astype(BF16)
        h_ref[...] = h
        acc = xv
        for d in range(nb):
            r = jnp.maximum(_dot(h, w1_ref[d]), 0.0)
            r_ref[:, d * fb:(d + 1) * fb] = r.astype(BF16)
            acc = acc + _dot((r * r).astype(BF16), w2_ref[d])
        o_ref[...] = acc

    tok = lambda width: pl.BlockSpec((tm, width), lambda i: (i, 0))
    return pl.pallas_call(
        body, name=name, grid=(S // tm,),
        in_specs=[tok(D), _resident((1, D)), _resident(w1.shape), _resident(w2.shape)],
        out_specs=[tok(D), tok(D), tok(nb * fb)],
        out_shape=[jax.ShapeDtypeStruct((S, D), F32), jax.ShapeDtypeStruct((S, D), BF16),
                   jax.ShapeDtypeStruct((S, nb * fb), BF16)],
        compiler_params=_params("parallel"),
    )(x, g, w1, w2)


def _mlp_bwd(dout, x, g, r, w1, w2, name, tm=256):
    S, D = x.shape
    nb, _, fb = w1.shape
    tm = _token_tile(S, tm)

    def body(do_ref, x_ref, g_ref, r_ref, w1_ref, w2_ref, dx_ref, dg_ref, da_ref):
        dov = do_ref[...]
        dob = dov.astype(BF16)
        dh = jnp.zeros((tm, D), F32)
        for d in range(nb):
            dz = _dot_nt(dob, w2_ref[d])
            da = (dz * (2.0 * r_ref[:, d * fb:(d + 1) * fb].astype(F32))).astype(BF16)
            da_ref[:, d * fb:(d + 1) * fb] = da
            dh = dh + _dot_nt(da, w1_ref[d])
        dx, dg = _norm_bwd(dh, x_ref[...], g_ref[...])
        dx_ref[...] = dov + dx

        @pl.when(pl.program_id(0) == 0)
        def _():
            dg_ref[...] = jnp.zeros_like(dg_ref)
        dg_ref[...] += dg

    tok = lambda width: pl.BlockSpec((tm, width), lambda i: (i, 0))
    return pl.pallas_call(
        body, name=name, grid=(S // tm,),
        in_specs=[tok(D), tok(D), _resident((1, D)), tok(nb * fb), _resident(w1.shape),
                  _resident(w2.shape)],
        out_specs=[tok(D), pl.BlockSpec((1, D), lambda i: (0, 0)), tok(nb * fb)],
        out_shape=[jax.ShapeDtypeStruct((S, D), F32), jax.ShapeDtypeStruct((1, D), F32),
                   jax.ShapeDtypeStruct((S, nb * fb), BF16)],
        compiler_params=_params("arbitrary"),
    )(dout, x, g, r, w1, w2)


def _scan_chunk(a, b, row, T, reverse):
    s = 1
    while s < T:
        if reverse:
            keep, shift = row < T - s, T - s
        else:
            keep, shift = row >= s, s
        a_sh = jnp.where(keep, pltpu.roll(a, shift, 0), 1.0)
        b_sh = jnp.where(keep, pltpu.roll(b, shift, 0), 0.0)
        b = a * b_sh + b
        a = a * a_sh
        s *= 2
    return a, b


def _row_of(x, row, r):
    return jnp.sum(jnp.where(row == r, x, 0.0), axis=0, keepdims=True)


def _shift_down(x, prev, row, k):
    if k == 0:
        return x
    return jnp.where(row < k, pltpu.roll(prev, k, 0), pltpu.roll(x, k, 0))


def _shift_up(x, nxt, row, k, T):
    if k == 0:
        return x
    return jnp.where(row < T - k, pltpu.roll(x, T - k, 0), pltpu.roll(nxt, T - k, 0))


def _lru_gates(xb, prev_xb, row, cw_ref, cb, wr, br, wi, bi, ls):
    xc = cb + cw_ref[pl.ds(0, 1), :] * _shift_down(xb, prev_xb, row, 3)
    for k in (2, 1, 0):
        xc = xc + cw_ref[pl.ds(3 - k, 1), :] * _shift_down(xb, prev_xb, row, k)
    xcb = xc.astype(BF16)
    r = _sigmoid(_dot(xcb, wr) + br)
    i = _sigmoid(_dot(xcb, wi) + bi)
    la = (LRU_C * r) * ls
    a = jnp.exp(la)
    m = jnp.sqrt(-_expm1(2.0 * la))
    return xc, xcb, r, i, a, m


def _lru_specs(S):
    col = lambda off: pl.BlockSpec((S, LANES), lambda j: (0, j + off))
    vec = pl.BlockSpec((1, LANES), lambda j: (0, j))
    mat = pl.BlockSpec((None, LANES, LANES), lambda j: (j, 0, 0))
    cwm = pl.BlockSpec((CONV_WIDTH, LANES), lambda j: (0, j))
    return col, vec, mat, cwm


def _lru_fwd(u, conv_w, conv_b, wr, br, wi, bi, lam, name):
    S = u.shape[0]
    T = _token_tile(S, 256)
    col, vec, mat, cwm = _lru_specs(S)

    def body(gp_ref, xb_ref, cw_ref, cb_ref, wr_ref, br_ref, wi_ref, bi_ref, lam_ref,
             y_ref, hs_ref):
        row = lax.broadcasted_iota(jnp.int32, (T, LANES), 0)
        ls = _log_sigmoid(lam_ref[...])
        cb, br, bi = cb_ref[...], br_ref[...], bi_ref[...]
        wr, wi = wr_ref[...], wi_ref[...]

        def chunk(ci, carry):
            prev_xb, hc = carry
            rows = pl.ds(pl.multiple_of(ci * T, T), T)
            xb = xb_ref[rows, :]
            xc, _, _, i, a, m = _lru_gates(xb, prev_xb, row, cw_ref, cb, wr, br, wi, bi, ls)
            ca, cbv = _scan_chunk(a, m * (i * xc), row, T, reverse=False)
            h = ca * hc + cbv
            hs_ref[rows, :] = h
            y_ref[rows, :] = (_gelu(gp_ref[rows, :]) * h).astype(BF16)
            return xb, _row_of(h, row, T - 1)

        lax.fori_loop(0, S // T, chunk,
                      (jnp.zeros((T, LANES), F32), jnp.zeros((1, LANES), F32)))

    return pl.pallas_call(
        body, name=name, grid=(N_CBLK,),
        in_specs=[col(0), col(N_CBLK), cwm, vec, mat, vec, mat, vec, vec],
        out_specs=[col(0), col(0)],
        out_shape=[jax.ShapeDtypeStruct((S, D_MODEL), BF16), jax.ShapeDtypeStruct((S, D_MODEL), F32)],
        compiler_params=_params("parallel"),
    )(u, u, conv_w, conv_b, wr, br, wi, bi, lam)


def _lru_bwd(dy, u, hs, conv_w, conv_b, wr, br, wi, bi, lam, name):
    S = u.shape[0]
    T = _token_tile(S, 256)
    n_chunk = S // T
    col, vec, mat, cwm = _lru_specs(S)

    def body(dy_ref, gp_ref, xb_ref, hs_ref, cw_ref, cb_ref, wr_ref, br_ref, wi_ref, bi_ref,
             lam_ref, du_ref, dcw_ref, dcb_ref, dbr_ref, dbi_ref, dlam_ref, dwr_ref, dwi_ref):
        row = lax.broadcasted_iota(jnp.int32, (T, LANES), 0)
        lam = lam_ref[...]
        ls = _log_sigmoid(lam)
        cb, br, bi = cb_ref[...], br_ref[...], bi_ref[...]
        wr, wi = wr_ref[...], wi_ref[...]
        for ref in (dcw_ref, dcb_ref, dbr_ref, dbi_ref, dlam_ref, dwr_ref, dwi_ref):
            ref[...] = jnp.zeros_like(ref)

        def chunk(it, carry):
            g_next, dxc_next = carry
            ci = n_chunk - 1 - it
            rows = pl.ds(pl.multiple_of(ci * T, T), T)
            before = pl.ds(pl.multiple_of(jnp.maximum(ci - 1, 0) * T, T), T)
            first = ci == 0
            xb = xb_ref[rows, :]
            prev_xb = jnp.where(first, 0.0, xb_ref[before, :])
            xc, xcb, r, i, a, m = _lru_gates(xb, prev_xb, row, cw_ref, cb, wr, br, wi, bi, ls)
            h = hs_ref[rows, :]
            h_prev = _shift_down(h, jnp.where(first, 0.0, hs_ref[before, :]), row, 1)
            gp = gp_ref[rows, :]
            dyv = dy_ref[rows, :]
            du_ref[0, rows, :] = (dyv * h * _gelu_grad(gp)).astype(BF16)
            dh = dyv * _gelu(gp)
            ca, cbv = _scan_chunk(a, a * dh, row, T, reverse=True)
            gp_acc = ca * g_next + cbv
            g = dh + jnp.where(row < T - 1, pltpu.roll(gp_acc, T - 1, 0), g_next)
            da = g * h_prev - (g * (i * xc)) * a / m
            dla = da * a
            dlam_ref[...] += jnp.sum(dla * (LRU_C * r), axis=0, keepdims=True)
            dpr = (dla * (LRU_C * ls)) * r * (1.0 - r)
            dpi = (g * m * xc) * i * (1.0 - i)
            dbr_ref[...] += jnp.sum(dpr, axis=0, keepdims=True)
            dbi_ref[...] += jnp.sum(dpi, axis=0, keepdims=True)
            dprb, dpib = dpr.astype(BF16), dpi.astype(BF16)
            dwr_ref[...] += _dot_tn(xcb, dprb)
            dwi_ref[...] += _dot_tn(xcb, dpib)
            dxc = g * m * i + _dot_nt(dprb, wr) + _dot_nt(dpib, wi)
            dcb_ref[...] += jnp.sum(dxc, axis=0, keepdims=True)
            dxb = jnp.zeros((T, LANES), F32)
            for k in range(CONV_WIDTH):
                tap = pl.ds(CONV_WIDTH - 1 - k, 1)
                dcw_ref[tap, :] += jnp.sum(dxc * _shift_down(xb, prev_xb, row, k), axis=0,
                                           keepdims=True)
                dxb = dxb + cw_ref[tap, :] * _shift_up(dxc, dxc_next, row, k, T)
            du_ref[1, rows, :] = dxb.astype(BF16)
            return _row_of(gp_acc, row, 0), dxc

        lax.fori_loop(0, n_chunk, chunk,
                      (jnp.zeros((1, LANES), F32), jnp.zeros((T, LANES), F32)))
        dlam_ref[...] = dlam_ref[...] * _sigmoid(-lam)

    vec_out = jax.ShapeDtypeStruct((1, D_MODEL), F32)
    mat_out = jax.ShapeDtypeStruct((N_CBLK, LANES, LANES), F32)
    return pl.pallas_call(
        body, name=name, grid=(N_CBLK,),
        in_specs=[col(0), col(0), col(N_CBLK), col(0), cwm, vec, mat, vec, mat, vec, vec],
        out_specs=[pl.BlockSpec((2, S, LANES), lambda j: (0, 0, j)), cwm, vec, vec, vec, vec,
                   mat, mat],
        out_shape=[jax.ShapeDtypeStruct((2, S, D_MODEL), BF16),
                   jax.ShapeDtypeStruct((CONV_WIDTH, D_MODEL), F32),
                   vec_out, vec_out, vec_out, vec_out, mat_out, mat_out],
        compiler_params=_params("parallel"),
    )(dy, u, u, hs, conv_w, conv_b, wr, br, wi, bi, lam)


def _head_group_matrix(value):
    r = lax.broadcasted_iota(jnp.int32, (LANES, LANES), 0) // HEAD_DIM
    c = lax.broadcasted_iota(jnp.int32, (LANES, LANES), 1) // HEAD_DIM
    return jnp.where(r == c, value, 0.0).astype(BF16)


def _group_dot(x, p):
    hi = x.astype(BF16)
    lo = (x - hi.astype(F32)).astype(BF16)
    return _dot(hi, p) + _dot(lo, p)


def _head_mean(x, p):
    return _group_dot(x, p)


def _qk_prep(u, q_gain, k_gain, name, tm=512):
    S = u.shape[0]
    tm = _token_tile(S, tm)

    def body(q_ref, k_ref, v_ref, qg_ref, kg_ref, qn_ref, kn_ref, vb_ref):
        p = _head_group_matrix(1.0 / HEAD_DIM)
        for x_ref, g_ref, o_ref, scale in ((q_ref, qg_ref, qn_ref, ATTN_SCALE),
                                           (k_ref, kg_ref, kn_ref, 1.0)):
            xv = x_ref[...]
            rs = lax.rsqrt(_head_mean(xv * xv, p) + EPS)
            o_ref[...] = (xv * rs * g_ref[...]).astype(BF16) * scale
        vb_ref[...] = v_ref[...].astype(BF16)

    blk = lambda off: pl.BlockSpec((tm, LANES), lambda i, j: (i, j + off))
    out = jax.ShapeDtypeStruct((S, D_MODEL), BF16)
    return pl.pallas_call(
        body, name=name, grid=(S // tm, N_CBLK),
        in_specs=[blk(0), blk(N_CBLK), blk(2 * N_CBLK), _resident((1, LANES)),
                  _resident((1, LANES))],
        out_specs=[blk(0), blk(0), blk(0)],
        out_shape=[out, out, out],
        compiler_params=_params("parallel", "parallel"),
    )(u, u, u, q_gain, k_gain)


def _qk_bwd(u, dqn, dkn, q_gain, k_gain, name, tm=512):
    S = u.shape[0]
    tm = _token_tile(S, tm)

    def body(q_ref, k_ref, dqn_ref, dkn_ref, qg_ref, kg_ref, dq_ref, dk_ref, dqg_ref, dkg_ref):
        p = _head_group_matrix(1.0 / HEAD_DIM)
        first = (pl.program_id(0) == 0) & (pl.program_id(1) == 0)
        last = (pl.program_id(0) == S // tm - 1) & (pl.program_id(1) == N_CBLK - 1)
        for x_ref, dn_ref, g_ref, dx_ref, dg_ref, scale in (
                (q_ref, dqn_ref, qg_ref, dq_ref, dqg_ref, ATTN_SCALE),
                (k_ref, dkn_ref, kg_ref, dk_ref, dkg_ref, 1.0)):
            xv, dn = x_ref[...], dn_ref[...] * scale
            rs = lax.rsqrt(_head_mean(xv * xv, p) + EPS)
            xhat = xv * rs
            dxhat = dn * g_ref[...]
            dx_ref[...] = (rs * (dxhat - xhat * _head_mean(dxhat * xhat, p))).astype(BF16)

            @pl.when(first)
            def _():
                dg_ref[...] = jnp.zeros_like(dg_ref)
            dg_ref[...] += jnp.sum(dn * xhat, axis=0, keepdims=True)

            @pl.when(last)
            def _():
                dg_ref[...] += pltpu.roll(dg_ref[...], HEAD_DIM, 1)

    blk = lambda off: pl.BlockSpec((tm, LANES), lambda i, j: (i, j + off))
    acc = pl.BlockSpec((1, LANES), lambda i, j: (0, 0))
    out = jax.ShapeDtypeStruct((S, D_MODEL), BF16)
    vec = jax.ShapeDtypeStruct((1, LANES), F32)
    return pl.pallas_call(
        body, name=name, grid=(S // tm, N_CBLK),
        in_specs=[blk(0), blk(N_CBLK), blk(0), blk(0), _resident((1, LANES)),
                  _resident((1, LANES))],
        out_specs=[blk(0), blk(0), acc, acc],
        out_shape=[out, out, vec, vec],
        compiler_params=_params("arbitrary", "arbitrary"),
    )(u, u, dqn, dkn, q_gain, k_gain)


def _forget_fwd(f, b_f, name):
    S = f.shape[0]
    T = _token_tile(S, 256)

    def body(f_ref, b_ref, c_ref):
        row = lax.broadcasted_iota(jnp.int32, (T, LANES), 0)
        ones = jnp.ones((T, LANES), F32)
        bias = b_ref[...]

        def chunk(ci, carry):
            rows = pl.ds(pl.multiple_of(ci * T, T), T)
            _, c = _scan_chunk(ones, _log_sigmoid(f_ref[rows, :] + bias), row, T, reverse=False)
            c = c + carry
            c_ref[rows, :] = c
            return _row_of(c, row, T - 1)

        lax.fori_loop(0, S // T, chunk, jnp.zeros((1, LANES), F32))

    return pl.pallas_call(
        body, name=name,
        in_specs=[pl.BlockSpec(memory_space=pltpu.VMEM)] * 2,
        out_specs=pl.BlockSpec(memory_space=pltpu.VMEM),
        out_shape=jax.ShapeDtypeStruct((S, LANES), F32),
        compiler_params=pltpu.CompilerParams(vmem_limit_bytes=VMEM_LIMIT),
    )(f, b_f)


def _forget_bwd(dc_k, dc_q, f, b_f, name):
    S = f.shape[0]
    T = _token_tile(S, 256)
    n_chunk = S // T

    def body(dck_ref, dcq_ref, f_ref, b_ref, df_ref, db_ref):
        row = lax.broadcasted_iota(jnp.int32, (T, LANES), 0)
        ones = jnp.ones((T, LANES), F32)
        bias = b_ref[...]

        def chunk(it, carry):
            tail, db = carry
            rows = pl.ds(pl.multiple_of((n_chunk - 1 - it) * T, T), T)
            _, dlf = _scan_chunk(ones, dck_ref[rows, :] + dcq_ref[rows, :], row, T, reverse=True)
            dlf = dlf + tail
            df = dlf * _sigmoid(-(f_ref[rows, :] + bias))
            df_ref[rows, :] = df
            return _row_of(dlf, row, 0), db + jnp.sum(df, axis=0, keepdims=True)

        zero = jnp.zeros((1, LANES), F32)
        _, db = lax.fori_loop(0, n_chunk, chunk, (zero, zero))
        db_ref[...] = db

    return pl.pallas_call(
        body, name=name,
        in_specs=[pl.BlockSpec(memory_space=pltpu.VMEM)] * 4,
        out_specs=[pl.BlockSpec(memory_space=pltpu.VMEM)] * 2,
        out_shape=[jax.ShapeDtypeStruct((S, LANES), F32), jax.ShapeDtypeStruct((1, LANES), F32)],
        compiler_params=pltpu.CompilerParams(vmem_limit_bytes=VMEM_LIMIT),
    )(dc_k, dc_q, f, b_f)


ATTN_TILE = 512


def _attn_tiles(S):
    t = _token_tile(S, ATTN_TILE)
    return t, S // t


def _causal(T):
    return (lax.broadcasted_iota(jnp.int32, (T, T), 1)
            <= lax.broadcasted_iota(jnp.int32, (T, T), 0))


def _attn_fwd(qs_, kn, vb, c_row, name):
    S = qs_.shape[0]
    T, n_t = _attn_tiles(S)

    def body(q_ref, k_ref, v_ref, cr_ref, o_ref, lse_ref):
        qi = pl.program_id(1)
        causal = _causal(T)
        lanes = [slice(h2 * HEAD_DIM, (h2 + 1) * HEAD_DIM) for h2 in range(2)]
        qh = [q_ref[:, hl] for hl in lanes]

        def step(kj, carry, masked):
            ks = pl.ds(pl.multiple_of(kj * T, T), T)
            out = []
            for h2, hl in enumerate(lanes):
                m, l, acc = carry[h2]
                s = _dot_nt(qh[h2], k_ref[ks, hl]) - cr_ref[h2:h2 + 1, ks]
                if masked:
                    s = jnp.where(causal, s, NEG_INF)
                m_new = jnp.maximum(m, jnp.max(s, axis=1, keepdims=True))
                alpha = jnp.exp(m - m_new)
                p = jnp.exp(s - m_new)
                l = alpha * l + jnp.sum(p, axis=1, keepdims=True)
                acc = alpha * acc + _dot(p.astype(BF16), v_ref[ks, hl])
                out.append((m_new, l, acc))
            return tuple(out)

        init = tuple((jnp.full((T, 1), NEG_INF, F32), jnp.zeros((T, 1), F32),
                      jnp.zeros((T, HEAD_DIM), F32)) for _ in lanes)
        carry = lax.fori_loop(0, qi, lambda kj, c: step(kj, c, False), init)
        carry = step(qi, carry, True)
        for (m, l, acc), hl in zip(carry, lanes):
            o_ref[:, hl] = (acc / l).astype(BF16)
            lse_ref[:, hl] = jnp.broadcast_to(m + jnp.log(l), (T, HEAD_DIM))

    qblk = pl.BlockSpec((T, LANES), lambda h, i: (i, h))
    kv = pl.BlockSpec((S, LANES), lambda h, i: (0, h))
    return pl.pallas_call(
        body, name=name, grid=(N_CBLK, n_t),
        in_specs=[qblk, kv, kv, pl.BlockSpec((None, 2, S), lambda h, i: (h, 0, 0))],
        out_specs=[qblk, qblk],
        out_shape=[jax.ShapeDtypeStruct((S, D_MODEL), BF16),
                   jax.ShapeDtypeStruct((S, D_MODEL), F32)],
        compiler_params=_params("parallel", "parallel"),
    )(qs_, kn, vb, c_row)


def _attn_bwd(qs_, kn, vb, do, o, lse, c_row, name):
    S = qs_.shape[0]
    T, n_t = _attn_tiles(S)

    def body(q_ref, k_ref, v_ref, do_ref, o_ref, lse_ref, cr_ref,
             dq_ref, dk_ref, dv_ref, dc_ref, rho_ref, dd_ref):
        kj = pl.program_id(1)
        causal = _causal(T)
        lanes = [slice(h2 * HEAD_DIM, (h2 + 1) * HEAD_DIM) for h2 in range(2)]
        ones = [slice(h2 * HEAD_DIM, h2 * HEAD_DIM + 1) for h2 in range(2)]

        @pl.when(kj == 0)
        def _():
            dq_ref[...] = jnp.zeros_like(dq_ref)
            rho_ref[...] = jnp.zeros_like(rho_ref)
            p_sum = _head_group_matrix(1.0)

            def fill(ci, _):
                rows = pl.ds(pl.multiple_of(ci * T, T), T)
                dd_ref[rows, :] = _group_dot(do_ref[rows, :].astype(F32) * o_ref[rows, :].astype(F32),
                                             p_sum)
                return 0

            lax.fori_loop(0, n_t, fill, 0)

        kh = [k_ref[:, hl] for hl in lanes]
        vh = [v_ref[:, hl] for hl in lanes]
        ck = [cr_ref[h2:h2 + 1, :] for h2 in range(2)]

        def step(qi, carry, masked):
            qs = pl.ds(pl.multiple_of(qi * T, T), T)
            out = []
            for h2, hl in enumerate(lanes):
                dk, dv, dc = carry[h2]
                qh, doh = q_ref[qs, hl], do_ref[qs, hl]
                s = _dot_nt(qh, kh[h2]) - ck[h2]
                if masked:
                    s = jnp.where(causal, s, NEG_INF)
                p = jnp.exp(s - lse_ref[qs, ones[h2]])
                ds = p * (_dot_nt(doh, vh[h2]) - dd_ref[qs, ones[h2]])
                dsb = ds.astype(BF16)
                dq_ref[qs, hl] += _dot(dsb, kh[h2])
                rho_ref[qs, hl] += jnp.broadcast_to(jnp.sum(ds, axis=1, keepdims=True),
                                                    (T, HEAD_DIM))
                out.append((dk + _dot_tn(dsb, qh), dv + _dot_tn(p.astype(BF16), doh),
                            dc - jnp.sum(ds, axis=0, keepdims=True)))
            return tuple(out)

        init = tuple((jnp.zeros((T, HEAD_DIM), F32), jnp.zeros((T, HEAD_DIM), F32),
                      jnp.zeros((1, T), F32)) for _ in lanes)
        carry = step(kj, init, True)
        carry = lax.fori_loop(kj + 1, n_t, lambda qi, c: step(qi, c, False), carry)
        for h2, ((dk, dv, dc), hl) in enumerate(zip(carry, lanes)):
            dk_ref[:, hl] = dk
            dv_ref[:, hl] = dv.astype(BF16)
            dc_ref[h2:h2 + 1, :] = dc

    kblk = pl.BlockSpec((T, LANES), lambda h, j: (j, h))
    full = pl.BlockSpec((S, LANES), lambda h, j: (0, h))
    crow = pl.BlockSpec((None, 2, T), lambda h, j: (h, 0, j))
    wide = jax.ShapeDtypeStruct((S, D_MODEL), F32)
    return pl.pallas_call(
        body, name=name, grid=(N_CBLK, n_t),
        in_specs=[full, kblk, kblk, full, full, full, crow],
        out_specs=[full, kblk, kblk, crow, full],
        out_shape=[wide, wide, jax.ShapeDtypeStruct((S, D_MODEL), BF16),
                   jax.ShapeDtypeStruct((N_CBLK, 2, S), F32), wide],
        scratch_shapes=[pltpu.VMEM((S, LANES), F32)],
        compiler_params=_params("parallel", "arbitrary"),
    )(qs_, kn, vb, do, o, lse, c_row)


def _loss_head(y, target, name, tm=512):
    S, D = y.shape
    tm = _token_tile(S, tm)

    def body(y_ref, t_ref, loss_ref, dy_ref):
        err = y_ref[...] - t_ref[...]
        dy_ref[...] = err / D

        @pl.when(pl.program_id(0) == 0)
        def _():
            loss_ref[...] = jnp.zeros_like(loss_ref)
        row_loss = jnp.mean(err * err, axis=1, keepdims=True)
        loss_ref[...] += 0.5 * jnp.sum(row_loss, axis=0, keepdims=True)

    tok = pl.BlockSpec((tm, D), lambda i: (i, 0))
    return pl.pallas_call(
        body, name=name, grid=(S // tm,),
        in_specs=[tok, tok],
        out_specs=[pl.BlockSpec((1, 1), lambda i: (0, 0)), tok],
        out_shape=[jax.ShapeDtypeStruct((1, 1), F32), jax.ShapeDtypeStruct((S, D), F32)],
        compiler_params=_params("arbitrary"),
    )(y, target)


def _exchange(arrays, gathers, name):
    n = len(arrays)

    def body(*refs):
        ins, outs = refs[:n], refs[n:2 * n]
        send_sems, recv_sems, own_sems = refs[2 * n:]
        own = _own_copies(ins, outs, gathers, own_sems)
        for cp in own:
            cp.start()
        copies = _peer_copies(ins, outs, gathers, send_sems, recv_sems)
        for send, _ in copies:
            send.start()
        for send, arrival in copies:
            arrival.wait_recv()
            send.wait_send()
        for cp in own:
            cp.wait()

    hbm = pl.BlockSpec(memory_space=pl.ANY)
    return pl.pallas_call(
        body, name=name,
        in_specs=[hbm] * n, out_specs=[hbm] * n, out_shape=_landing_shapes(arrays, gathers),
        scratch_shapes=[pltpu.SemaphoreType.DMA((n * N_PEER,)),
                        pltpu.SemaphoreType.DMA((n * N_PEER,)),
                        pltpu.SemaphoreType.DMA((n,))],
        compiler_params=pltpu.CompilerParams(has_side_effects=True),
    )(*arrays)


N_PEER = N_DEV - 1


def _landing_shapes(arrays, gathers):
    return [jax.ShapeDtypeStruct((N_DEV,) + a.shape if g else a.shape, a.dtype)
            for a, g in zip(arrays, gathers)]


def _my_index():
    return 4 * lax.axis_index("x") + 2 * lax.axis_index("y") + lax.axis_index("c")


def _own_copies(srcs, lands, gathers, sems):
    me = _my_index()
    return [pltpu.make_async_copy(src if g else src.at[me], land.at[me], sems.at[a])
            for a, (src, land, g) in enumerate(zip(srcs, lands, gathers))]


def _peer_copies(srcs, lands, gathers, send_sems, recv_sems):
    x, y, c = lax.axis_index("x"), lax.axis_index("y"), lax.axis_index("c")
    me = 4 * x + 2 * y + c
    out = []
    for k in range(1, N_DEV):
        to = (1 - x if k & 4 else x, 1 - y if k & 2 else y, 1 - c if k & 1 else c)
        peer = 4 * to[0] + 2 * to[1] + to[2]
        for a, (src, land, g) in enumerate(zip(srcs, lands, gathers)):
            sem = a * N_PEER + k - 1
            src_blk = src if g else src.at[peer]

            def copy(slot, src_blk=src_blk, land=land, sem=sem, to=to):
                return pltpu.make_async_remote_copy(
                    src_ref=src_blk, dst_ref=land.at[slot], send_sem=send_sems.at[sem],
                    recv_sem=recv_sems.at[sem], device_id=to,
                    device_id_type=pl.DeviceIdType.MESH)

            out.append((copy(me), copy(peer)))
    return out


_HBM = pl.BlockSpec(memory_space=pltpu.HBM)
_SEM = pl.BlockSpec(memory_space=pltpu.SEMAPHORE)
_ANY = pl.BlockSpec(memory_space=pl.ANY)
_DATAFLOW = pltpu.SideEffectType.DATAFLOW_SIDE_EFFECTING


def _in_hbm(a):
    return pltpu.with_memory_space_constraint(a, pltpu.HBM)


def _exchange_start(arrays, gathers, after, name):
    n = len(arrays)
    lands = [lax.empty(s.shape, s.dtype) for s in _landing_shapes(arrays, gathers)]

    def body(*refs):
        srcs, dsts = refs[:n], refs[n:2 * n]
        send_sems, recv_sems, own_sems = refs[2 * n + 1:2 * n + 4]
        token = refs[-1]
        for cp in _own_copies(srcs, dsts, gathers, own_sems):
            cp.start()
        for send, _ in _peer_copies(srcs, dsts, gathers, send_sems, recv_sems):
            send.start()
        token[...] = jnp.zeros_like(token)

    hbm_like = [pltpu.HBM(a.shape, a.dtype) for a in list(arrays) + lands]
    res = pl.pallas_call(
        body, name=name,
        in_specs=[_HBM] * (2 * n) + [_ANY],
        out_specs=(_SEM, _SEM, _SEM, *[_HBM] * (2 * n), pl.BlockSpec(memory_space=pltpu.VMEM)),
        out_shape=(pltpu.SemaphoreType.DMA((n * N_PEER,)), pltpu.SemaphoreType.DMA((n * N_PEER,)),
                   pltpu.SemaphoreType.DMA((n,)), *hbm_like,
                   jax.ShapeDtypeStruct((8, LANES), F32)),
        input_output_aliases={i: 3 + i for i in range(2 * n)},
        compiler_params=pltpu.CompilerParams(has_side_effects=_DATAFLOW),
    )(*[_in_hbm(a) for a in list(arrays) + lands], after)
    return (res[0], res[1], res[2], res[3:3 + n], res[3 + n:3 + 2 * n]), res[-1]


def _exchange_wait(started, gathers, after, name):
    send_sems, recv_sems, own_sems, arrays, lands = started
    n = len(arrays)

    def body(*refs):
        srcs, dsts = refs[:n], refs[n:2 * n]
        for send, arrival in _peer_copies(srcs, dsts, gathers, refs[2 * n], refs[2 * n + 1]):
            arrival.wait_recv()
            send.wait_send()
        for cp in _own_copies(srcs, dsts, gathers, refs[2 * n + 2]):
            cp.wait()

    hbm_like = [pltpu.HBM(a.shape, a.dtype) for a in list(arrays) + list(lands)]
    res = pl.pallas_call(
        body, name=name,
        in_specs=[_HBM] * (2 * n) + [_SEM, _SEM, _SEM, _ANY],
        out_specs=[_HBM] * (2 * n), out_shape=hbm_like,
        input_output_aliases={i: i for i in range(2 * n)},
        compiler_params=pltpu.CompilerParams(has_side_effects=_DATAFLOW),
    )(*arrays, *lands, send_sems, recv_sems, own_sems, after)
    return res[n:]


def _reduce_adamw(parts, w, m, v, name):
    n, R, C = parts.shape
    tr = 256 if R % 256 == 0 else R

    def body(p_ref, w_ref, m_ref, v_ref, g_ref, d_ref, nm_ref, nv_ref):
        g = p_ref[0].astype(F32)
        for s in range(1, n):
            g = g + p_ref[s].astype(F32)
        g_ref[...] = g
        m_new = ADAM_B1 * m_ref[...] + (1.0 - ADAM_B1) * g
        v_new = ADAM_B2 * v_ref[...] + (1.0 - ADAM_B2) * (g * g)
        nm_ref[...] = m_new
        nv_ref[...] = v_new
        m_hat = m_new / (1.0 - ADAM_B1 ** ADAM_STEP)
        v_hat = v_new / (1.0 - ADAM_B2 ** ADAM_STEP)
        d_ref[...] = -ADAM_LR * (m_hat / (jnp.sqrt(v_hat) + ADAM_EPS) + ADAM_WD * w_ref[...])

    blk = pl.BlockSpec((tr, C), lambda i: (i, 0))
    out = jax.ShapeDtypeStruct((R, C), F32)
    return pl.pallas_call(
        body, name=name, grid=(R // tr,),
        in_specs=[pl.BlockSpec((n, tr, C), lambda i: (0, i, 0)), blk, blk, blk],
        out_specs=[blk] * 4, out_shape=[out] * 4,
        compiler_params=_params("parallel"),
    )(parts, w, m, v)


def _pack(arrays):
    flat = jnp.concatenate([a.reshape(-1).astype(F32) for a in arrays])
    pad = (-flat.shape[0]) % (8 * LANES)
    return jnp.pad(flat, (0, pad)).reshape(-1, LANES)


def _unpack(buf, shapes):
    flat = buf.reshape(-1)
    out, off = [], 0
    for shp in shapes:
        size = 1
        for s in shp:
            size *= s
        out.append(flat[off:off + size].reshape(shp))
        off += size
    return out


def _block_diag_pairs(w):
    w = w.reshape(N_CBLK, 2, LRU_BLOCK_DIM, LRU_BLOCK_DIM)
    z = jnp.zeros_like(w[:, 0])
    top = jnp.concatenate([w[:, 0], z], axis=2)
    bot = jnp.concatenate([z, w[:, 1]], axis=2)
    return jnp.concatenate([top, bot], axis=1)


def _diag_pairs(m):
    h = LRU_BLOCK_DIM
    return jnp.stack([m[:, :h, :h], m[:, h:, h:]], axis=1).reshape(2 * N_CBLK, h, h)


SMALL = ("mix_norm", "mlp_norm", "lru_conv_b", "lru_w_r", "lru_b_r", "lru_w_i", "lru_b_i",
         "lru_lambda", "fox_b_f", "fox_q_gain", "fox_k_gain")
WEIGHTS = ("mix_norm", "mlp_norm", "mlp_w1", "mlp_w2", "lru_w_in", "lru_conv_w", "lru_conv_b",
           "lru_w_r", "lru_b_r", "lru_w_i", "lru_b_i", "lru_lambda", "lru_w_out", "fox_w_in",
           "fox_b_f", "fox_q_gain", "fox_k_gain", "fox_w_out")


def kernel(x, mix_norm, mlp_norm, mlp_w1, mlp_w2, lru_w_in, lru_conv_w, lru_conv_b, lru_w_r, lru_b_r, lru_w_i, lru_b_i, lru_lambda, lru_w_out, fox_w_in, fox_b_f, fox_q_gain, fox_k_gain, fox_w_out, loss_target, m_mix_norm, m_mlp_norm, m_mlp_w1, m_mlp_w2, m_lru_w_in, m_lru_conv_w, m_lru_conv_b, m_lru_w_r, m_lru_b_r, m_lru_w_i, m_lru_b_i, m_lru_lambda, m_lru_w_out, m_fox_w_in, m_fox_b_f, m_fox_q_gain, m_fox_k_gain, m_fox_w_out, v_mix_norm, v_mlp_norm, v_mlp_w1, v_mlp_w2, v_lru_w_in, v_lru_conv_w, v_lru_conv_b, v_lru_w_r, v_lru_b_r, v_lru_w_i, v_lru_b_i, v_lru_lambda, v_lru_w_out, v_fox_w_in, v_fox_b_f, v_fox_q_gain, v_fox_k_gain, v_fox_w_out):
    w_in = dict(mix_norm=mix_norm, mlp_norm=mlp_norm, mlp_w1=mlp_w1, mlp_w2=mlp_w2,
                lru_w_in=lru_w_in, lru_conv_w=lru_conv_w, lru_conv_b=lru_conv_b, lru_w_r=lru_w_r,
                lru_b_r=lru_b_r, lru_w_i=lru_w_i, lru_b_i=lru_b_i, lru_lambda=lru_lambda,
                lru_w_out=lru_w_out, fox_w_in=fox_w_in, fox_b_f=fox_b_f, fox_q_gain=fox_q_gain,
                fox_k_gain=fox_k_gain, fox_w_out=fox_w_out)
    m_in = dict(mix_norm=m_mix_norm, mlp_norm=m_mlp_norm, mlp_w1=m_mlp_w1, mlp_w2=m_mlp_w2,
                lru_w_in=m_lru_w_in, lru_conv_w=m_lru_conv_w, lru_conv_b=m_lru_conv_b,
                lru_w_r=m_lru_w_r, lru_b_r=m_lru_b_r, lru_w_i=m_lru_w_i, lru_b_i=m_lru_b_i,
                lru_lambda=m_lru_lambda, lru_w_out=m_lru_w_out, fox_w_in=m_fox_w_in,
                fox_b_f=m_fox_b_f, fox_q_gain=m_fox_q_gain, fox_k_gain=m_fox_k_gain,
                fox_w_out=m_fox_w_out)
    v_in = dict(mix_norm=v_mix_norm, mlp_norm=v_mlp_norm, mlp_w1=v_mlp_w1, mlp_w2=v_mlp_w2,
                lru_w_in=v_lru_w_in, lru_conv_w=v_lru_conv_w, lru_conv_b=v_lru_conv_b,
                lru_w_r=v_lru_w_r, lru_b_r=v_lru_b_r, lru_w_i=v_lru_w_i, lru_b_i=v_lru_b_i,
                lru_lambda=v_lru_lambda, lru_w_out=v_lru_w_out, fox_w_in=v_fox_w_in,
                fox_b_f=v_fox_b_f, fox_q_gain=v_fox_q_gain, fox_k_gain=v_fox_k_gain,
                fox_w_out=v_fox_w_out)
    D = D_MODEL
    S = x.shape[1]
    x0, target = x[0], loss_target[0]
    me = 4 * lax.axis_index("x") + 2 * lax.axis_index("y") + lax.axis_index("c")

    def bf16(a):
        return a.astype(BF16)

    lru_in_g, lru_out_g, conv_g = _exchange(
        [bf16(lru_w_in[0]), bf16(lru_w_out[0]), lru_conv_w[0]], [True] * 3, "gather_lru")
    gather_mlp0, tok = _exchange_start([bf16(mlp_w1[0]), bf16(mlp_w2[0])], [True] * 2, lru_in_g,
                                       "gather_mlp0_start")
    gather_fox, tok = _exchange_start([bf16(fox_w_in[0]), bf16(fox_w_out[0])], [True] * 2, tok,
                                      "gather_fox_start")
    gather_mlp1, tok = _exchange_start([bf16(mlp_w1[1]), bf16(mlp_w2[1])], [True] * 2, tok,
                                       "gather_mlp1_start")
    lru_out_w = lru_out_g.reshape(D, D)
    conv_w = conv_g.transpose(1, 0, 2).reshape(CONV_WIDTH, D)
    wr = _block_diag_pairs(lru_w_r[0]).astype(BF16)
    wi = _block_diag_pairs(lru_w_i[0]).astype(BF16)
    b_r, b_i = lru_b_r.reshape(1, D), lru_b_i.reshape(1, D)
    q_gain, k_gain = jnp.tile(fox_q_gain, (1, 2)), jnp.tile(fox_k_gain, (1, 2))
    b_f = jnp.pad(fox_b_f, ((0, 0), (0, LANES - N_HEADS)))
    g_mix0, g_mix1 = mix_norm[0:1] + tok[0, 0], mix_norm[1:2]
    g_mlp0, g_mlp1 = mlp_norm[0:1], mlp_norm[1:2]

    (u0,), h0 = _norm_matmul(x0, g_mix0, [lru_in_g], "lru_in_proj")
    y_lru, hs = _lru_fwd(u0, conv_w, lru_conv_b, wr, b_r, wi, b_i, lru_lambda, "lru_core")
    x1 = _matmul_res(y_lru, lru_out_w, x0, "lru_out_proj")
    w1g0, w2g0 = _exchange_wait(gather_mlp0, [True] * 2, x1, "gather_mlp0_wait")
    x2, h1, r1 = _mlp_fwd(x1, g_mlp0, w1g0, w2g0, "mlp0")
    fox_in_g, fox_out_g = _exchange_wait(gather_fox, [True] * 2, x2, "gather_fox_wait")
    fox_out_w = fox_out_g.reshape(D, D)
    fox_full = fox_in_g.transpose(1, 0, 2).reshape(D, 3 * D + N_HEADS)
    wqkv = fox_full[:, :3 * D].reshape(D, 3, D).transpose(1, 0, 2)
    wf = jnp.pad(fox_full[:, 3 * D:], ((0, 0), (0, LANES - N_HEADS)))[None]
    (u_qkv, f), h2 = _norm_matmul(x2, g_mix1, [wqkv, wf], "fox_in_proj")
    qn, kn, vb = _qk_prep(u_qkv, q_gain, k_gain, "fox_qk_norm")
    c_col = _forget_fwd(f, b_f, "fox_forget")
    c_row = c_col[:, :N_HEADS].T.reshape(N_CBLK, 2, S)
    o, lse = _attn_fwd(qn, kn, vb, c_row, "fox_attn")
    x3 = _matmul_res(o, fox_out_w, x2, "fox_out_proj")
    w1g1, w2g1 = _exchange_wait(gather_mlp1, [True] * 2, x3, "gather_mlp1_wait")
    x4, h3, r3 = _mlp_fwd(x3, g_mlp1, w1g1, w2g1, "mlp1")
    loss_local, dx4 = _loss_head(x4, target, "loss_head")

    dx3, dg_mlp1, da3 = _mlp_bwd(dx4, x3, g_mlp1, r3, w1g1, w2g1, "mlp1_bwd")
    dw1_1 = _matmul_tn(h3, da3, "mlp1_dw1", cols=2, col_blocks=N_DEV)
    dw2_1 = _matmul_tn(r3, dx4, "mlp1_dw2", rows=2, a_square=True).reshape(N_DEV, -1, D)
    grads_mlp1, tok = _exchange_start([dw1_1, dw2_1], [False] * 2, dw1_1, "grads_mlp1_start")
    do = _matmul_nt(dx3, fox_out_w, "fox_out_bwd", BF16, tok)
    d_fox_out = _matmul_tn(o, dx3, "fox_out_dw").reshape(N_DEV, -1, D)
    dqn, dkn, dv, dc_row, rho = _attn_bwd(qn, kn, vb, do, o, lse, c_row, "fox_attn_bwd")
    duq, duk, dq_gain, dk_gain = _qk_bwd(u_qkv, dqn, dkn, q_gain, k_gain, "fox_qk_norm_bwd")
    head_pad = ((0, 0), (0, LANES - N_HEADS))
    dc_k = jnp.pad(dc_row.reshape(N_HEADS, S).T, head_pad)
    dc_q = jnp.pad(rho[:, ::HEAD_DIM], head_pad)
    df, db_f = _forget_bwd(dc_k, dc_q, f, b_f, "fox_forget_bwd")
    dx2, dg_mix1 = _proj_bwd([duq, duk, dv, df], [wqkv[0:1], wqkv[1:2], wqkv[2:3], wf],
                             x2, g_mix1, dx3, "fox_in_bwd")
    d_fox_in = jnp.concatenate(
        [_matmul_tn(h2, duq, "fox_in_dwq"), _matmul_tn(h2, duk, "fox_in_dwk"),
         _matmul_tn(h2, dv, "fox_in_dwv"), _matmul_tn(h2, df, "fox_in_dwf")[:, :N_HEADS]], axis=1)
    d_fox_in = d_fox_in.reshape(D, N_DEV, -1).transpose(1, 0, 2)
    grads_fox, tok = _exchange_start([d_fox_in, d_fox_out], [False] * 2, d_fox_in,
                                     "grads_fox_start")
    dx1, dg_mlp0, da1 = _mlp_bwd(dx2, x1, g_mlp0 + tok[0, 0], r1, w1g0, w2g0, "mlp0_bwd")
    dw1_0 = _matmul_tn(h1, da1, "mlp0_dw1", cols=2, col_blocks=N_DEV)
    dw2_0 = _matmul_tn(r1, dx2, "mlp0_dw2", rows=2, a_square=True).reshape(N_DEV, -1, D)
    grads_mlp0, tok = _exchange_start([dw1_0, dw2_0], [False] * 2, dw1_0, "grads_mlp0_start")
    dy_lru = _matmul_nt(dx1, lru_out_w, "lru_out_bwd", F32, tok)
    d_lru_out = _matmul_tn(y_lru, dx1, "lru_out_dw").reshape(N_DEV, -1, D)
    du0, d_conv_w, d_conv_b, d_b_r, d_b_i, d_lam, d_wr, d_wi = _lru_bwd(
        dy_lru, u0, hs, conv_w, lru_conv_b, wr, b_r, wi, b_i, lru_lambda, "lru_core_bwd")
    dx0, dg_mix0 = _proj_bwd([du0[0], du0[1]], [lru_in_g[:4], lru_in_g[4:]], x0, g_mix0, dx1,
                             "lru_in_bwd")
    d_lru_in = jnp.concatenate([_matmul_tn(h0, du0[0], "lru_in_dw_gate", col_blocks=4),
                                _matmul_tn(h0, du0[1], "lru_in_dw_x", col_blocks=4)], axis=0)

    small_grads = dict(
        mix_norm=jnp.concatenate([dg_mix0, dg_mix1], axis=0),
        mlp_norm=jnp.concatenate([dg_mlp0, dg_mlp1], axis=0),
        lru_conv_b=d_conv_b, lru_w_r=_diag_pairs(d_wr), lru_b_r=d_b_r, lru_w_i=_diag_pairs(d_wi),
        lru_b_i=d_b_i, lru_lambda=d_lam, fox_b_f=db_f[:, :N_HEADS],
        fox_q_gain=dq_gain[:, :HEAD_DIM], fox_k_gain=dk_gain[:, :HEAD_DIM])
    small_partial = _pack([small_grads[n] for n in SMALL] + [d_conv_w])
    p_lru_in, p_lru_out, p_small = _exchange(
        [bf16(d_lru_in), bf16(d_lru_out), small_partial], [False, False, True], "exchange_lru_small")
    p_w1_1, p_w2_1 = _exchange_wait(grads_mlp1, [False] * 2, p_small, "grads_mlp1_wait")
    p_fox_in, p_fox_out = _exchange_wait(grads_fox, [False] * 2, p_w1_1, "grads_fox_wait")
    p_w1_0, p_w2_0 = _exchange_wait(grads_mlp0, [False] * 2, p_fox_in, "grads_mlp0_wait")

    grads, deltas, new_m, new_v = {}, {}, {}, {}

    def update(name, parts, sel=None):
        w, m, v = w_in[name], m_in[name], v_in[name]
        if sel is not None:
            w, m, v = w[sel], m[sel], v[sel]
        shape = w.shape
        two_d = (-1, shape[-1])
        res = _reduce_adamw(parts.reshape((N_DEV,) + w.reshape(two_d).shape), w.reshape(two_d),
                            m.reshape(two_d), v.reshape(two_d),
                            "adamw_" + name + ("" if sel is None else "_%d" % sel))
        return [r.reshape(shape) for r in res]

    def store(name, res):
        grads[name], deltas[name], new_m[name], new_v[name] = res

    store("mlp_w1", [jnp.stack(p) for p in zip(update("mlp_w1", p_w1_0, 0),
                                               update("mlp_w1", p_w1_1, 1))])
    store("mlp_w2", [jnp.stack(p) for p in zip(update("mlp_w2", p_w2_0, 0),
                                               update("mlp_w2", p_w2_1, 1))])
    store("lru_w_in", update("lru_w_in", p_lru_in))
    store("lru_w_out", update("lru_w_out", p_lru_out))
    store("fox_w_in", update("fox_w_in", p_fox_in))
    store("fox_w_out", update("fox_w_out", p_fox_out))

    small_shapes = [w_in[n].shape for n in SMALL]
    n_small = sum(math.prod(s) for s in small_shapes)
    res_small = _reduce_adamw(p_small, _pack([w_in[n] for n in SMALL] + [jnp.zeros((CONV_WIDTH, D))]),
                              _pack([m_in[n] for n in SMALL] + [jnp.zeros((CONV_WIDTH, D))]),
                              _pack([v_in[n] for n in SMALL] + [jnp.zeros((CONV_WIDTH, D))]),
                              "adamw_small")
    for name, *vals in zip(SMALL, *[_unpack(r, small_shapes) for r in res_small]):
        store(name, vals)
    conv_parts = p_small.reshape(N_DEV, -1)[:, n_small:n_small + CONV_WIDTH * D]
    conv_parts = conv_parts.reshape(N_DEV, CONV_WIDTH, N_DEV, LANES)
    conv_parts = lax.dynamic_index_in_dim(conv_parts, me, axis=2, keepdims=False)
    store("lru_conv_w", update("lru_conv_w", conv_parts))

    loss = lax.psum(loss_local[0, 0], ("x", "y", "c"))
    return (loss, dx0[None], *[grads[n] for n in WEIGHTS], *[deltas[n] for n in WEIGHTS],
            *[new_m[n] for n in WEIGHTS], *[new_v[n] for n in WEIGHTS])
```

```python
import math

import jax
import jax.numpy as jnp
from jax import lax
from jax.experimental import pallas as pl
from jax.experimental.pallas import tpu as pltpu

F32 = jnp.float32
BF16 = jnp.bfloat16

N_DEV = 8
D_MODEL = 1024
D_FF = 4096
N_HEADS = 16
HEAD_DIM = 64
LRU_BLOCK_DIM = 64
CONV_WIDTH = 4
LRU_C = 8.0
EPS = 1e-6
NEG_INF = -1e30
ATTN_SCALE = HEAD_DIM ** -0.5
LANES = 128
N_CBLK = D_MODEL // LANES
VMEM_LIMIT = 52 * 2 ** 20

ADAM_LR = 0.001
ADAM_B1 = 0.9
ADAM_B2 = 0.999
ADAM_EPS = 1e-08
ADAM_WD = 0.01
ADAM_STEP = 10

_NT = (((1,), (1,)), ((), ()))
_TN = (((0,), (0,)), ((), ()))


def _params(*sem):
    return pltpu.CompilerParams(dimension_semantics=sem, vmem_limit_bytes=VMEM_LIMIT)


def _resident(shape):
    zeros = (0,) * len(shape)
    return pl.BlockSpec(shape, lambda *_: zeros, pipeline_mode=pl.Buffered(1))


def _dot(a, b):
    return jnp.dot(a, b, preferred_element_type=F32)


def _dot_nt(a, b):
    return lax.dot_general(a, b, _NT, preferred_element_type=F32)


def _dot_tn(a, b):
    return lax.dot_general(a, b, _TN, preferred_element_type=F32)


def _sigmoid(x):
    return 1.0 / (1.0 + jnp.exp(-x))


def _log_sigmoid(x):
    return -(jnp.maximum(-x, 0.0) + jnp.log1p(jnp.exp(-jnp.abs(x))))


def _expm1(x):
    poly = x * (1.0 + x * (0.5 + x * (1.0 / 6.0 + x * (1.0 / 24.0 + x * (1.0 / 120.0)))))
    return jnp.where(jnp.abs(x) < 0.1, poly, jnp.exp(x) - 1.0)


_GELU_K = 0.7978845608028654


def _gelu(x):
    return 0.5 * x * (1.0 + jnp.tanh(_GELU_K * (x + 0.044715 * (x * x * x))))


def _gelu_grad(x):
    t = jnp.tanh(_GELU_K * (x + 0.044715 * (x * x * x)))
    return 0.5 * (1.0 + t) + 0.5 * x * (1.0 - t * t) * (_GELU_K * (1.0 + 3 * 0.044715 * x * x))


def _rms_scale(x):
    return lax.rsqrt(jnp.mean(x * x, axis=-1, keepdims=True) + EPS)


def _norm_bwd(dh, x, g):
    rs = _rms_scale(x)
    xhat = x * rs
    dxhat = dh * g
    dx = rs * (dxhat - xhat * jnp.mean(dxhat * xhat, axis=-1, keepdims=True))
    return dx, jnp.sum(dh * xhat, axis=0, keepdims=True)


def _token_tile(S, want):
    tm = min(S, want)
    assert S % tm == 0
    return tm


def _norm_matmul(x, g, ws, name, tm=256):
    S, D = x.shape
    tm = _token_tile(S, tm)
    n = len(ws)

    def body(x_ref, g_ref, *refs):
        w_refs, o_refs, h_ref = refs[:n], refs[n:2 * n], refs[2 * n]
        xv = x_ref[...]
        h = (xv * _rms_scale(xv) * g_ref[...]).astype(BF16)
        h_ref[...] = h
        for w_ref, o_ref in zip(w_refs, o_refs):
            nb, _, nw = w_ref.shape
            for d in range(nb):
                o_ref[:, d * nw:(d + 1) * nw] = _dot(h, w_ref[d])

    widths = [w.shape[0] * w.shape[2] for w in ws]
    outs = pl.pallas_call(
        body, name=name, grid=(S // tm,),
        in_specs=[pl.BlockSpec((tm, D), lambda i: (i, 0)), _resident((1, D))]
        + [_resident(w.shape) for w in ws],
        out_specs=[pl.BlockSpec((tm, n_), lambda i: (i, 0)) for n_ in widths]
        + [pl.BlockSpec((tm, D), lambda i: (i, 0))],
        out_shape=[jax.ShapeDtypeStruct((S, n_), F32) for n_ in widths]
        + [jax.ShapeDtypeStruct((S, D), BF16)],
        compiler_params=_params("parallel"),
    )(x, g, *ws)
    return outs[:n], outs[n]


def _matmul_res(a, w, res, name, tm=512):
    S, K = a.shape
    N = w.shape[1]
    tm = _token_tile(S, tm)

    def body(a_ref, w_ref, r_ref, o_ref):
        o_ref[...] = r_ref[...] + _dot(a_ref[...], w_ref[...])

    return pl.pallas_call(
        body, name=name, grid=(S // tm,),
        in_specs=[pl.BlockSpec((tm, K), lambda i: (i, 0)), _resident((K, N)),
                  pl.BlockSpec((tm, N), lambda i: (i, 0))],
        out_specs=pl.BlockSpec((tm, N), lambda i: (i, 0)),
        out_shape=jax.ShapeDtypeStruct((S, N), F32),
        compiler_params=_params("parallel"),
    )(a, w, res)


def _matmul_nt(a, w, name, out_dtype, after, tm=512):
    S, N = a.shape
    K = w.shape[0]
    tm = _token_tile(S, tm)

    def body(a_ref, w_ref, after_ref, o_ref):
        o_ref[...] = _dot_nt(a_ref[...].astype(BF16), w_ref[...]).astype(out_dtype)

    return pl.pallas_call(
        body, name=name, grid=(S // tm,),
        in_specs=[pl.BlockSpec((tm, N), lambda i: (i, 0)), _resident((K, N)),
                  pl.BlockSpec(memory_space=pl.ANY)],
        out_specs=pl.BlockSpec((tm, K), lambda i: (i, 0)),
        out_shape=jax.ShapeDtypeStruct((S, K), out_dtype),
        compiler_params=_params("parallel"),
    )(a, w, after)


def _proj_bwd(a_lists, w_list, x, g, res, name, tm=256):
    S, D = x.shape
    tm = _token_tile(S, tm)
    a_list = [a for group in a_lists for a in group]
    n, n_w = len(a_list), len(w_list)

    def body(*refs):
        a_refs, w_refs = list(refs[:n]), refs[n:n + n_w]
        x_ref, g_ref, r_ref, dx_ref, dg_ref = refs[n + n_w:]
        dh = jnp.zeros((tm, D), F32)
        for group, w_ref in zip(a_lists, w_refs):
            nw = w_ref.shape[2]
            d = 0
            for _ in group:
                a_ref = a_refs.pop(0)
                for j in range(a_ref.shape[1] // nw):
                    dh = dh + _dot_nt(a_ref[:, j * nw:(j + 1) * nw].astype(BF16), w_ref[d])
                    d += 1
        dx, dg = _norm_bwd(dh, x_ref[...], g_ref[...])
        dx_ref[...] = r_ref[...] + dx

        @pl.when(pl.program_id(0) == 0)
        def _():
            dg_ref[...] = jnp.zeros_like(dg_ref)
        dg_ref[...] += dg

    tok = lambda width: pl.BlockSpec((tm, width), lambda i: (i, 0))
    return pl.pallas_call(
        body, name=name, grid=(S // tm,),
        in_specs=[tok(a.shape[1]) for a in a_list] + [_resident(w.shape) for w in w_list]
        + [tok(D), _resident((1, D)), tok(D)],
        out_specs=[tok(D), pl.BlockSpec((1, D), lambda i: (0, 0))],
        out_shape=[jax.ShapeDtypeStruct((S, D), F32), jax.ShapeDtypeStruct((1, D), F32)],
        compiler_params=_params("arbitrary"),
    )(*a_list, *w_list, x, g, res)


def _matmul_tn(a, b, name, rows=1, cols=1, col_blocks=None, a_square=False, tm=512):
    S, K = a.shape
    N = b.shape[1]
    tm = _token_tile(S, tm)
    n_tok = S // tm
    kr, nc = K // rows, N // cols

    def body(a_ref, b_ref, o_ref, acc_ref):
        av = a_ref[...]
        if a_square:
            av = av.astype(F32)
            av = av * av
        part = _dot_tn(av.astype(BF16), b_ref[...].astype(BF16))
        step = pl.program_id(2)

        @pl.when(step == 0)
        def _():
            acc_ref[...] = part

        @pl.when(step > 0)
        def _():
            acc_ref[...] += part

        @pl.when(step == n_tok - 1)
        def _():
            if col_blocks is None:
                o_ref[...] = acc_ref[...].astype(BF16)
            else:
                nw = N // col_blocks
                for d in range(col_blocks // cols):
                    o_ref[d] = acc_ref[:, d * nw:(d + 1) * nw].astype(BF16)

    if col_blocks is None:
        out_spec = pl.BlockSpec((kr, nc), lambda r, c, i: (r, c))
        out_shape = jax.ShapeDtypeStruct((K, N), BF16)
    else:
        assert rows == 1 and col_blocks % cols == 0
        per = col_blocks // cols
        out_spec = pl.BlockSpec((per, K, N // col_blocks), lambda r, c, i: (c, 0, 0))
        out_shape = jax.ShapeDtypeStruct((col_blocks, K, N // col_blocks), BF16)
    return pl.pallas_call(
        body, name=name, grid=(rows, cols, n_tok),
        in_specs=[pl.BlockSpec((tm, kr), lambda r, c, i: (i, r)),
                  pl.BlockSpec((tm, nc), lambda r, c, i: (i, c))],
        out_specs=out_spec, out_shape=out_shape,
        scratch_shapes=[pltpu.VMEM((kr, nc), F32)],
        compiler_params=_params("parallel", "parallel", "arbitrary"),
    )(a, b)


def _mlp_fwd(x, g, w1, w2, name, tm=256):
    S, D = x.shape
    nb, _, fb = w1.shape
    tm = _token_tile(S, tm)

    def body(x_ref, g_ref, w1_ref, w2_ref, o_ref, h_ref, r_ref):
        xv = x_ref[...]
        h = (xv * _rms_scale(xv) * g_ref[...]).astype(BF16)
        h_ref[...] = h
        acc = xv
        for d in range(nb):
            r = jnp.maximum(_dot(h, w1_ref[d]), 0.0)
            r_ref[:, d * fb:(d + 1) * fb] = r.astype(BF16)
            acc = acc + _dot((r * r).astype(BF16), w2_ref[d])
        o_ref[...] = acc

    tok = lambda width: pl.BlockSpec((tm, width), lambda i: (i, 0))
    return pl.pallas_call(
        body, name=name, grid=(S // tm,),
        in_specs=[tok(D), _resident((1, D)), _resident(w1.shape), _resident(w2.shape)],
        out_specs=[tok(D), tok(D), tok(nb * fb)],
        out_shape=[jax.ShapeDtypeStruct((S, D), F32), jax.ShapeDtypeStruct((S, D), BF16),
                   jax.ShapeDtypeStruct((S, nb * fb), BF16)],
        compiler_params=_params("parallel"),
    )(x, g, w1, w2)


def _mlp_bwd(dout, x, g, r, w1, w2, name, tm=256):
    S, D = x.shape
    nb, _, fb = w1.shape
    tm = _token_tile(S, tm)

    def body(do_ref, x_ref, g_ref, r_ref, w1_ref, w2_ref, dx_ref, dg_ref, da_ref):
        dov = do_ref[...]
        dob = dov.astype(BF16)
        dh = jnp.zeros((tm, D), F32)
        for d in range(nb):
            dz = _dot_nt(dob, w2_ref[d])
            da = (dz * (2.0 * r_ref[:, d * fb:(d + 1) * fb].astype(F32))).astype(BF16)
            da_ref[:, d * fb:(d + 1) * fb] = da
            dh = dh + _dot_nt(da, w1_ref[d])
        dx, dg = _norm_bwd(dh, x_ref[...], g_ref[...])
        dx_ref[...] = dov + dx

        @pl.when(pl.program_id(0) == 0)
        def _():
            dg_ref[...] = jnp.zeros_like(dg_ref)
        dg_ref[...] += dg

    tok = lambda width: pl.BlockSpec((tm, width), lambda i: (i, 0))
    return pl.pallas_call(
        body, name=name, grid=(S // tm,),
        in_specs=[tok(D), tok(D), _resident((1, D)), tok(nb * fb), _resident(w1.shape),
                  _resident(w2.shape)],
        out_specs=[tok(D), pl.BlockSpec((1, D), lambda i: (0, 0)), tok(nb * fb)],
        out_shape=[jax.ShapeDtypeStruct((S, D), F32), jax.ShapeDtypeStruct((1, D), F32),
                   jax.ShapeDtypeStruct((S, nb * fb), BF16)],
        compiler_params=_params("arbitrary"),
    )(dout, x, g, r, w1, w2)


def _scan_chunk(a, b, row, T, reverse):
    s = 1
    while s < T:
        if reverse:
            keep, shift = row < T - s, T - s
        else:
            keep, shift = row >= s, s
        a_sh = jnp.where(keep, pltpu.roll(a, shift, 0), 1.0)
        b_sh = jnp.where(keep, pltpu.roll(b, shift, 0), 0.0)
        b = a * b_sh + b
        a = a * a_sh
        s *= 2
    return a, b


def _row_of(x, row, r):
    return jnp.sum(jnp.where(row == r, x, 0.0), axis=0, keepdims=True)


def _shift_down(x, prev, row, k):
    if k == 0:
        return x
    return jnp.where(row < k, pltpu.roll(prev, k, 0), pltpu.roll(x, k, 0))


def _shift_up(x, nxt, row, k, T):
    if k == 0:
        return x
    return jnp.where(row < T - k, pltpu.roll(x, T - k, 0), pltpu.roll(nxt, T - k, 0))


def _lru_gates(xb, prev_xb, row, cw_ref, cb, wr, br, wi, bi, ls):
    xc = cb + cw_ref[pl.ds(0, 1), :] * _shift_down(xb, prev_xb, row, 3)
    for k in (2, 1, 0):
        xc = xc + cw_ref[pl.ds(3 - k, 1), :] * _shift_down(xb, prev_xb, row, k)
    xcb = xc.astype(BF16)
    r = _sigmoid(_dot(xcb, wr) + br)
    i = _sigmoid(_dot(xcb, wi) + bi)
    la = (LRU_C * r) * ls
    a = jnp.exp(la)
    m = jnp.sqrt(-_expm1(2.0 * la))
    return xc, xcb, r, i, a, m


def _lru_specs(S):
    col = lambda off: pl.BlockSpec((S, LANES), lambda j: (0, j + off))
    vec = pl.BlockSpec((1, LANES), lambda j: (0, j))
    mat = pl.BlockSpec((None, LANES, LANES), lambda j: (j, 0, 0))
    cwm = pl.BlockSpec((CONV_WIDTH, LANES), lambda j: (0, j))
    return col, vec, mat, cwm


def _lru_fwd(u, conv_w, conv_b, wr, br, wi, bi, lam, name):
    S = u.shape[0]
    T = _token_tile(S, 256)
    col, vec, mat, cwm = _lru_specs(S)

    def body(gp_ref, xb_ref, cw_ref, cb_ref, wr_ref, br_ref, wi_ref, bi_ref, lam_ref,
             y_ref, hs_ref):
        row = lax.broadcasted_iota(jnp.int32, (T, LANES), 0)
        ls = _log_sigmoid(lam_ref[...])
        cb, br, bi = cb_ref[...], br_ref[...], bi_ref[...]
        wr, wi = wr_ref[...], wi_ref[...]

        def chunk(ci, carry):
            prev_xb, hc = carry
            rows = pl.ds(pl.multiple_of(ci * T, T), T)
            xb = xb_ref[rows, :]
            xc, _, _, i, a, m = _lru_gates(xb, prev_xb, row, cw_ref, cb, wr, br, wi, bi, ls)
            ca, cbv = _scan_chunk(a, m * (i * xc), row, T, reverse=False)
            h = ca * hc + cbv
            hs_ref[rows, :] = h
            y_ref[rows, :] = (_gelu(gp_ref[rows, :]) * h).astype(BF16)
            return xb, _row_of(h, row, T - 1)

        lax.fori_loop(0, S // T, chunk,
                      (jnp.zeros((T, LANES), F32), jnp.zeros((1, LANES), F32)))

    return pl.pallas_call(
        body, name=name, grid=(N_CBLK,),
        in_specs=[col(0), col(N_CBLK), cwm, vec, mat, vec, mat, vec, vec],
        out_specs=[col(0), col(0)],
        out_shape=[jax.ShapeDtypeStruct((S, D_MODEL), BF16), jax.ShapeDtypeStruct((S, D_MODEL), F32)],
        compiler_params=_params("parallel"),
    )(u, u, conv_w, conv_b, wr, br, wi, bi, lam)


def _lru_bwd(dy, u, hs, conv_w, conv_b, wr, br, wi, bi, lam, name):
    S = u.shape[0]
    T = _token_tile(S, 256)
    n_chunk = S // T
    col, vec, mat, cwm = _lru_specs(S)

    def body(dy_ref, gp_ref, xb_ref, hs_ref, cw_ref, cb_ref, wr_ref, br_ref, wi_ref, bi_ref,
             lam_ref, dgp_ref, dxb_ref, dcw_ref, dcb_ref, dbr_ref, dbi_ref, dlam_ref, dwr_ref,
             dwi_ref):
        row = lax.broadcasted_iota(jnp.int32, (T, LANES), 0)
        lam = lam_ref[...]
        ls = _log_sigmoid(lam)
        cb, br, bi = cb_ref[...], br_ref[...], bi_ref[...]
        wr, wi = wr_ref[...], wi_ref[...]
        for ref in (dcw_ref, dcb_ref, dbr_ref, dbi_ref, dlam_ref, dwr_ref, dwi_ref):
            ref[...] = jnp.zeros_like(ref)

        def chunk(it, carry):
            g_next, dxc_next = carry
            ci = n_chunk - 1 - it
            rows = pl.ds(pl.multiple_of(ci * T, T), T)
            before = pl.ds(pl.multiple_of(jnp.maximum(ci - 1, 0) * T, T), T)
            first = ci == 0
            xb = xb_ref[rows, :]
            prev_xb = jnp.where(first, 0.0, xb_ref[before, :])
            xc, xcb, r, i, a, m = _lru_gates(xb, prev_xb, row, cw_ref, cb, wr, br, wi, bi, ls)
            h = hs_ref[rows, :]
            h_prev = _shift_down(h, jnp.where(first, 0.0, hs_ref[before, :]), row, 1)
            gp = gp_ref[rows, :]
            dyv = dy_ref[rows, :]
            dgp_ref[rows, :] = (dyv * h * _gelu_grad(gp)).astype(BF16)
            dh = dyv * _gelu(gp)
            ca, cbv = _scan_chunk(a, a * dh, row, T, reverse=True)
            gp_acc = ca * g_next + cbv
            g = dh + jnp.where(row < T - 1, pltpu.roll(gp_acc, T - 1, 0), g_next)
            da = g * h_prev - (g * (i * xc)) * a / m
            dla = da * a
            dlam_ref[...] += jnp.sum(dla * (LRU_C * r), axis=0, keepdims=True)
            dpr = (dla * (LRU_C * ls)) * r * (1.0 - r)
            dpi = (g * m * xc) * i * (1.0 - i)
            dbr_ref[...] += jnp.sum(dpr, axis=0, keepdims=True)
            dbi_ref[...] += jnp.sum(dpi, axis=0, keepdims=True)
            dprb, dpib = dpr.astype(BF16), dpi.astype(BF16)
            dwr_ref[...] += _dot_tn(xcb, dprb)
            dwi_ref[...] += _dot_tn(xcb, dpib)
            dxc = g * m * i + _dot_nt(dprb, wr) + _dot_nt(dpib, wi)
            dcb_ref[...] += jnp.sum(dxc, axis=0, keepdims=True)
            dxb = jnp.zeros((T, LANES), F32)
            for k in range(CONV_WIDTH):
                tap = pl.ds(CONV_WIDTH - 1 - k, 1)
                dcw_ref[tap, :] += jnp.sum(dxc * _shift_down(xb, prev_xb, row, k), axis=0,
                                           keepdims=True)
                dxb = dxb + cw_ref[tap, :] * _shift_up(dxc, dxc_next, row, k, T)
            dxb_ref[rows, :] = dxb.astype(BF16)
            return _row_of(gp_acc, row, 0), dxc

        lax.fori_loop(0, n_chunk, chunk,
                      (jnp.zeros((1, LANES), F32), jnp.zeros((T, LANES), F32)))
        dlam_ref[...] = dlam_ref[...] * _sigmoid(-lam)

    vec_out = jax.ShapeDtypeStruct((1, D_MODEL), F32)
    mat_out = jax.ShapeDtypeStruct((N_CBLK, LANES, LANES), F32)
    return pl.pallas_call(
        body, name=name, grid=(N_CBLK,),
        in_specs=[col(0), col(0), col(N_CBLK), col(0), cwm, vec, mat, vec, mat, vec, vec],
        out_specs=[col(0), col(0), cwm, vec, vec, vec, vec, mat, mat],
        out_shape=[jax.ShapeDtypeStruct((S, D_MODEL), BF16), jax.ShapeDtypeStruct((S, D_MODEL), BF16),
                   jax.ShapeDtypeStruct((CONV_WIDTH, D_MODEL), F32),
                   vec_out, vec_out, vec_out, vec_out, mat_out, mat_out],
        compiler_params=_params("parallel"),
    )(dy, u, u, hs, conv_w, conv_b, wr, br, wi, bi, lam)


def _head_group_matrix(value):
    r = lax.broadcasted_iota(jnp.int32, (LANES, LANES), 0) // HEAD_DIM
    c = lax.broadcasted_iota(jnp.int32, (LANES, LANES), 1) // HEAD_DIM
    return jnp.where(r == c, value, 0.0).astype(BF16)


def _group_dot(x, p):
    hi = x.astype(BF16)
    lo = (x - hi.astype(F32)).astype(BF16)
    return _dot(hi, p) + _dot(lo, p)


def _head_mean(x, p):
    return _group_dot(x, p)


def _qk_prep(u, q_gain, k_gain, name, tm=512):
    S = u.shape[0]
    tm = _token_tile(S, tm)

    def body(q_ref, k_ref, v_ref, qg_ref, kg_ref, qn_ref, kn_ref, vb_ref):
        p = _head_group_matrix(1.0 / HEAD_DIM)
        for x_ref, g_ref, o_ref, scale in ((q_ref, qg_ref, qn_ref, ATTN_SCALE),
                                           (k_ref, kg_ref, kn_ref, 1.0)):
            xv = x_ref[...]
            rs = lax.rsqrt(_head_mean(xv * xv, p) + EPS)
            o_ref[...] = (xv * rs * g_ref[...]).astype(BF16) * scale
        vb_ref[...] = v_ref[...].astype(BF16)

    blk = lambda off: pl.BlockSpec((tm, LANES), lambda i, j: (i, j + off))
    out = jax.ShapeDtypeStruct((S, D_MODEL), BF16)
    return pl.pallas_call(
        body, name=name, grid=(S // tm, N_CBLK),
        in_specs=[blk(0), blk(N_CBLK), blk(2 * N_CBLK), _resident((1, LANES)),
                  _resident((1, LANES))],
        out_specs=[blk(0), blk(0), blk(0)],
        out_shape=[out, out, out],
        compiler_params=_params("parallel", "parallel"),
    )(u, u, u, q_gain, k_gain)


def _qk_bwd(u, dqn, dkn, q_gain, k_gain, name, tm=512):
    S = u.shape[0]
    tm = _token_tile(S, tm)

    def body(q_ref, k_ref, dqn_ref, dkn_ref, qg_ref, kg_ref, dq_ref, dk_ref, dqg_ref, dkg_ref):
        p = _head_group_matrix(1.0 / HEAD_DIM)
        first = (pl.program_id(0) == 0) & (pl.program_id(1) == 0)
        last = (pl.program_id(0) == S // tm - 1) & (pl.program_id(1) == N_CBLK - 1)
        for x_ref, dn_ref, g_ref, dx_ref, dg_ref, scale in (
                (q_ref, dqn_ref, qg_ref, dq_ref, dqg_ref, ATTN_SCALE),
                (k_ref, dkn_ref, kg_ref, dk_ref, dkg_ref, 1.0)):
            xv, dn = x_ref[...], dn_ref[...] * scale
            rs = lax.rsqrt(_head_mean(xv * xv, p) + EPS)
            xhat = xv * rs
            dxhat = dn * g_ref[...]
            dx_ref[...] = (rs * (dxhat - xhat * _head_mean(dxhat * xhat, p))).astype(BF16)

            @pl.when(first)
            def _():
                dg_ref[...] = jnp.zeros_like(dg_ref)
            dg_ref[...] += jnp.sum(dn * xhat, axis=0, keepdims=True)

            @pl.when(last)
            def _():
                dg_ref[...] += pltpu.roll(dg_ref[...], HEAD_DIM, 1)

    blk = lambda off: pl.BlockSpec((tm, LANES), lambda i, j: (i, j + off))
    acc = pl.BlockSpec((1, LANES), lambda i, j: (0, 0))
    out = jax.ShapeDtypeStruct((S, D_MODEL), BF16)
    vec = jax.ShapeDtypeStruct((1, LANES), F32)
    return pl.pallas_call(
        body, name=name, grid=(S // tm, N_CBLK),
        in_specs=[blk(0), blk(N_CBLK), blk(0), blk(0), _resident((1, LANES)),
                  _resident((1, LANES))],
        out_specs=[blk(0), blk(0), acc, acc],
        out_shape=[out, out, vec, vec],
        compiler_params=_params("arbitrary", "arbitrary"),
    )(u, u, dqn, dkn, q_gain, k_gain)


def _forget_fwd(f, b_f, name):
    S = f.shape[0]
    T = _token_tile(S, 256)

    def body(f_ref, b_ref, c_ref):
        row = lax.broadcasted_iota(jnp.int32, (T, LANES), 0)
        ones = jnp.ones((T, LANES), F32)
        bias = b_ref[...]

        def chunk(ci, carry):
            rows = pl.ds(pl.multiple_of(ci * T, T), T)
            _, c = _scan_chunk(ones, _log_sigmoid(f_ref[rows, :] + bias), row, T, reverse=False)
            c = c + carry
            c_ref[rows, :] = c
            return _row_of(c, row, T - 1)

        lax.fori_loop(0, S // T, chunk, jnp.zeros((1, LANES), F32))

    return pl.pallas_call(
        body, name=name,
        in_specs=[pl.BlockSpec(memory_space=pltpu.VMEM)] * 2,
        out_specs=pl.BlockSpec(memory_space=pltpu.VMEM),
        out_shape=jax.ShapeDtypeStruct((S, LANES), F32),
        compiler_params=pltpu.CompilerParams(vmem_limit_bytes=VMEM_LIMIT),
    )(f, b_f)


def _forget_bwd(dc_k, rho, f, b_f, name):
    S = f.shape[0]
    T = _token_tile(S, 256)
    n_chunk = S // T

    def body(dck_ref, rho_ref, f_ref, b_ref, df_ref, db_ref):
        row = lax.broadcasted_iota(jnp.int32, (T, LANES), 0)
        ones = jnp.ones((T, LANES), F32)
        bias = b_ref[...]
        pick = (lax.broadcasted_iota(jnp.int32, (D_MODEL, LANES), 0)
                == HEAD_DIM * lax.broadcasted_iota(jnp.int32, (D_MODEL, LANES), 1))
        pick = jnp.where(pick, 1.0, 0.0).astype(BF16)

        def chunk(it, carry):
            tail, db = carry
            rows = pl.ds(pl.multiple_of((n_chunk - 1 - it) * T, T), T)
            dc = dck_ref[rows, :] + _group_dot(rho_ref[rows, :], pick)
            _, dlf = _scan_chunk(ones, dc, row, T, reverse=True)
            dlf = dlf + tail
            df = dlf * _sigmoid(-(f_ref[rows, :] + bias))
            df_ref[rows, :] = df
            return _row_of(dlf, row, 0), db + jnp.sum(df, axis=0, keepdims=True)

        zero = jnp.zeros((1, LANES), F32)
        _, db = lax.fori_loop(0, n_chunk, chunk, (zero, zero))
        db_ref[...] = db

    return pl.pallas_call(
        body, name=name,
        in_specs=[pl.BlockSpec(memory_space=pltpu.VMEM)] * 4,
        out_specs=[pl.BlockSpec(memory_space=pltpu.VMEM)] * 2,
        out_shape=[jax.ShapeDtypeStruct((S, LANES), F32), jax.ShapeDtypeStruct((1, LANES), F32)],
        compiler_params=pltpu.CompilerParams(vmem_limit_bytes=VMEM_LIMIT),
    )(dc_k, rho, f, b_f)


ATTN_TILE = 512
ATTN_ROWS_FWD = 32
ATTN_ROWS_BWD = 32


def _attn_tiles(S):
    t = _token_tile(S, ATTN_TILE)
    return t, S // t


def _causal(T):
    return (lax.broadcasted_iota(jnp.int32, (T, T), 1)
            <= lax.broadcasted_iota(jnp.int32, (T, T), 0))


def _attn_fwd(qs_, kn, vb, c_row, name):
    S = qs_.shape[0]
    T, n_t = _attn_tiles(S)
    RB = min(T, ATTN_ROWS_FWD)

    def body(q_ref, k_ref, v_ref, cr_ref, o_ref, lse_ref, s_ref, p_ref, m_ref, l_ref, acc_ref,
             a_ref):
        qi = pl.program_id(1)
        lanes = [slice(h2 * HEAD_DIM, (h2 + 1) * HEAD_DIM) for h2 in range(2)]
        col = lax.broadcasted_iota(jnp.int32, (RB, T), 1)
        row = lax.broadcasted_iota(jnp.int32, (RB, T), 0)
        m_ref[...] = jnp.full(m_ref.shape, NEG_INF, F32)
        l_ref[...] = jnp.zeros_like(l_ref)
        acc_ref[...] = jnp.zeros_like(acc_ref)

        def step(kj, masked):
            ks = pl.ds(pl.multiple_of(kj * T, T), T)
            for h2, hl in enumerate(lanes):
                s_ref[h2] = _dot_nt(q_ref[:, hl], k_ref[ks, hl]) - cr_ref[h2:h2 + 1, ks]
            for h2, hl in enumerate(lanes):
                blocks = [slice(i * RB, (i + 1) * RB) for i in range(T // RB)]

                def logits(i, rows):
                    s = s_ref[h2, rows, :]
                    return jnp.where(col <= row + i * RB, s, NEG_INF) if masked else s

                wide = lambda x: jnp.broadcast_to(x, (RB, LANES))
                for i, rows in enumerate(blocks):
                    mx = wide(jnp.max(logits(i, rows), axis=1, keepdims=True))
                    a_ref[h2, rows, :] = m_ref[h2, rows, :]
                    m_ref[h2, rows, :] = jnp.maximum(m_ref[h2, rows, :], mx)
                for i, rows in enumerate(blocks):
                    m_new = m_ref[h2, rows, :]
                    p = jnp.exp(logits(i, rows) - jnp.tile(m_new, (1, T // LANES)))
                    alpha = jnp.exp(a_ref[h2, rows, :] - m_new)
                    a_ref[h2, rows, :] = alpha
                    l_ref[h2, rows, :] = (alpha * l_ref[h2, rows, :]
                                          + wide(jnp.sum(p, axis=1, keepdims=True)))
                    p_ref[h2, rows, :] = p.astype(BF16)
                acc_ref[h2] = (a_ref[h2, :, :HEAD_DIM] * acc_ref[h2]
                               + _dot(p_ref[h2], v_ref[ks, hl]))

        def unmasked(kj, _):
            step(kj, False)
            return 0

        lax.fori_loop(0, qi, unmasked, 0)
        step(qi, True)
        for h2, hl in enumerate(lanes):
            o_ref[:, hl] = (acc_ref[h2] / l_ref[h2, :, :HEAD_DIM]).astype(BF16)
            lse_ref[:, hl] = m_ref[h2, :, :HEAD_DIM] + jnp.log(l_ref[h2, :, :HEAD_DIM])

    qblk = pl.BlockSpec((T, LANES), lambda h, i: (i, h))
    kv = pl.BlockSpec((S, LANES), lambda h, i: (0, h))
    return pl.pallas_call(
        body, name=name, grid=(N_CBLK, n_t),
        in_specs=[qblk, kv, kv, pl.BlockSpec((None, 2, S), lambda h, i: (h, 0, 0))],
        out_specs=[qblk, qblk],
        out_shape=[jax.ShapeDtypeStruct((S, D_MODEL), BF16),
                   jax.ShapeDtypeStruct((S, D_MODEL), F32)],
        scratch_shapes=[pltpu.VMEM((2, T, T), F32), pltpu.VMEM((2, T, T), BF16),
                        pltpu.VMEM((2, T, LANES), F32), pltpu.VMEM((2, T, LANES), F32),
                        pltpu.VMEM((2, T, HEAD_DIM), F32), pltpu.VMEM((2, T, LANES), F32)],
        compiler_params=_params("parallel", "parallel"),
    )(qs_, kn, vb, c_row)


def _attn_bwd(qs_, kn, vb, do, o, lse, c_row, name):
    S = qs_.shape[0]
    T, n_t = _attn_tiles(S)

    def body(q_ref, k_ref, v_ref, do_ref, o_ref, lse_ref, cr_ref,
             dq_ref, dk_ref, dv_ref, dc_ref, rho_ref, dd_ref):
        kj = pl.program_id(1)
        causal = _causal(T)
        lanes = [slice(h2 * HEAD_DIM, (h2 + 1) * HEAD_DIM) for h2 in range(2)]
        ones = [slice(h2 * HEAD_DIM, h2 * HEAD_DIM + 1) for h2 in range(2)]

        @pl.when(kj == 0)
        def _():
            dq_ref[...] = jnp.zeros_like(dq_ref)
            rho_ref[...] = jnp.zeros_like(rho_ref)
            p_sum = _head_group_matrix(1.0)

            def fill(ci, _):
                rows = pl.ds(pl.multiple_of(ci * T, T), T)
                dd_ref[rows, :] = _group_dot(do_ref[rows, :].astype(F32) * o_ref[rows, :].astype(F32),
                                             p_sum)
                return 0

            lax.fori_loop(0, n_t, fill, 0)

        kh = [k_ref[:, hl] for hl in lanes]
        vh = [v_ref[:, hl] for hl in lanes]
        ck = [cr_ref[h2:h2 + 1, :] for h2 in range(2)]

        def step(qi, carry, masked):
            qs = pl.ds(pl.multiple_of(qi * T, T), T)
            out = []
            for h2, hl in enumerate(lanes):
                dk, dv, dc = carry[h2]
                qh, doh = q_ref[qs, hl], do_ref[qs, hl]
                s = _dot_nt(qh, kh[h2]) - ck[h2]
                if masked:
                    s = jnp.where(causal, s, NEG_INF)
                p = jnp.exp(s - lse_ref[qs, ones[h2]])
                ds = p * (_dot_nt(doh, vh[h2]) - dd_ref[qs, ones[h2]])
                dsb = ds.astype(BF16)
                dq_ref[qs, hl] += _dot(dsb, kh[h2])
                rho_ref[qs, hl] += jnp.broadcast_to(jnp.sum(ds, axis=1, keepdims=True),
                                                    (T, HEAD_DIM))
                out.append((dk + _dot_tn(dsb, qh), dv + _dot_tn(p.astype(BF16), doh),
                            dc - jnp.sum(ds, axis=0, keepdims=True)))
            return tuple(out)

        init = tuple((jnp.zeros((T, HEAD_DIM), F32), jnp.zeros((T, HEAD_DIM), F32),
                      jnp.zeros((1, T), F32)) for _ in lanes)
        carry = step(kj, init, True)
        carry = lax.fori_loop(kj + 1, n_t, lambda qi, c: step(qi, c, False), carry)
        for h2, ((dk, dv, dc), hl) in enumerate(zip(carry, lanes)):
            dk_ref[:, hl] = dk
            dv_ref[:, hl] = dv.astype(BF16)
            dc_ref[h2:h2 + 1, :] = dc

    kblk = pl.BlockSpec((T, LANES), lambda h, j: (j, h))
    full = pl.BlockSpec((S, LANES), lambda h, j: (0, h))
    crow = pl.BlockSpec((None, 2, T), lambda h, j: (h, 0, j))
    wide = jax.ShapeDtypeStruct((S, D_MODEL), F32)
    return pl.pallas_call(
        body, name=name, grid=(N_CBLK, n_t),
        in_specs=[full, kblk, kblk, full, full, full, crow],
        out_specs=[full, kblk, kblk, crow, full],
        out_shape=[wide, wide, jax.ShapeDtypeStruct((S, D_MODEL), BF16),
                   jax.ShapeDtypeStruct((N_CBLK, 2, S), F32), wide],
        scratch_shapes=[pltpu.VMEM((S, LANES), F32)],
        compiler_params=_params("parallel", "arbitrary"),
    )(qs_, kn, vb, do, o, lse, c_row)


def _loss_head(y, target, name, tm=512):
    S, D = y.shape
    tm = _token_tile(S, tm)

    def body(y_ref, t_ref, loss_ref, dy_ref):
        err = y_ref[...] - t_ref[...]
        dy_ref[...] = err / D

        @pl.when(pl.program_id(0) == 0)
        def _():
            loss_ref[...] = jnp.zeros_like(loss_ref)
        row_loss = jnp.mean(err * err, axis=1, keepdims=True)
        loss_ref[...] += 0.5 * jnp.sum(row_loss, axis=0, keepdims=True)

    tok = pl.BlockSpec((tm, D), lambda i: (i, 0))
    return pl.pallas_call(
        body, name=name, grid=(S // tm,),
        in_specs=[tok, tok],
        out_specs=[pl.BlockSpec((1, 1), lambda i: (0, 0)), tok],
        out_shape=[jax.ShapeDtypeStruct((1, 1), F32), jax.ShapeDtypeStruct((S, D), F32)],
        compiler_params=_params("arbitrary"),
    )(y, target)


def _exchange(arrays, gathers, name):
    n = len(arrays)

    def body(*refs):
        ins, outs = refs[:n], refs[n:2 * n]
        send_sems, recv_sems, own_sems = refs[2 * n:]
        own = _own_copies(ins, outs, gathers, own_sems)
        for cp in own:
            cp.start()
        copies = _peer_copies(ins, outs, gathers, send_sems, recv_sems)
        for send, _ in copies:
            send.start()
        for send, arrival in copies:
            arrival.wait_recv()
            send.wait_send()
        for cp in own:
            cp.wait()

    hbm = pl.BlockSpec(memory_space=pl.ANY)
    return pl.pallas_call(
        body, name=name,
        in_specs=[hbm] * n, out_specs=[hbm] * n, out_shape=_landing_shapes(arrays, gathers),
        scratch_shapes=[pltpu.SemaphoreType.DMA((n * N_PEER,)),
                        pltpu.SemaphoreType.DMA((n * N_PEER,)),
                        pltpu.SemaphoreType.DMA((n,))],
        compiler_params=pltpu.CompilerParams(has_side_effects=True),
    )(*arrays)


N_PEER = N_DEV - 1


def _landing_shapes(arrays, gathers):
    return [jax.ShapeDtypeStruct((N_DEV,) + a.shape if g else a.shape, a.dtype)
            for a, g in zip(arrays, gathers)]


def _my_index():
    return 4 * lax.axis_index("x") + 2 * lax.axis_index("y") + lax.axis_index("c")


def _own_copies(srcs, lands, gathers, sems):
    me = _my_index()
    return [pltpu.make_async_copy(src if g else src.at[me], land.at[me], sems.at[a])
            for a, (src, land, g) in enumerate(zip(srcs, lands, gathers))]


def _peer_copies(srcs, lands, gathers, send_sems, recv_sems):
    x, y, c = lax.axis_index("x"), lax.axis_index("y"), lax.axis_index("c")
    me = 4 * x + 2 * y + c
    out = []
    for k in range(1, N_DEV):
        to = (1 - x if k & 4 else x, 1 - y if k & 2 else y, 1 - c if k & 1 else c)
        peer = 4 * to[0] + 2 * to[1] + to[2]
        for a, (src, land, g) in enumerate(zip(srcs, lands, gathers)):
            sem = a * N_PEER + k - 1
            src_blk = src if g else src.at[peer]

            def copy(slot, src_blk=src_blk, land=land, sem=sem, to=to):
                return pltpu.make_async_remote_copy(
                    src_ref=src_blk, dst_ref=land.at[slot], send_sem=send_sems.at[sem],
                    recv_sem=recv_sems.at[sem], device_id=to,
                    device_id_type=pl.DeviceIdType.MESH)

            out.append((copy(me), copy(peer)))
    return out


_HBM = pl.BlockSpec(memory_space=pltpu.HBM)
_SEM = pl.BlockSpec(memory_space=pltpu.SEMAPHORE)
_ANY = pl.BlockSpec(memory_space=pl.ANY)
_DATAFLOW = pltpu.SideEffectType.DATAFLOW_SIDE_EFFECTING


def _in_hbm(a):
    return pltpu.with_memory_space_constraint(a, pltpu.HBM)


def _exchange_start(arrays, gathers, after, name):
    n = len(arrays)
    lands = [lax.empty(s.shape, s.dtype) for s in _landing_shapes(arrays, gathers)]

    def body(*refs):
        srcs, dsts = refs[:n], refs[n:2 * n]
        send_sems, recv_sems, own_sems = refs[2 * n + 1:2 * n + 4]
        token = refs[-1]
        for cp in _own_copies(srcs, dsts, gathers, own_sems):
            cp.start()
        for send, _ in _peer_copies(srcs, dsts, gathers, send_sems, recv_sems):
            send.start()
        token[...] = jnp.zeros_like(token)

    hbm_like = [pltpu.HBM(a.shape, a.dtype) for a in list(arrays) + lands]
    res = pl.pallas_call(
        body, name=name,
        in_specs=[_HBM] * (2 * n) + [_ANY],
        out_specs=(_SEM, _SEM, _SEM, *[_HBM] * (2 * n), pl.BlockSpec(memory_space=pltpu.VMEM)),
        out_shape=(pltpu.SemaphoreType.DMA((n * N_PEER,)), pltpu.SemaphoreType.DMA((n * N_PEER,)),
                   pltpu.SemaphoreType.DMA((n,)), *hbm_like,
                   jax.ShapeDtypeStruct((8, LANES), F32)),
        input_output_aliases={i: 3 + i for i in range(2 * n)},
        compiler_params=pltpu.CompilerParams(has_side_effects=_DATAFLOW),
    )(*[_in_hbm(a) for a in list(arrays) + lands], after)
    return (res[0], res[1], res[2], res[3:3 + n], res[3 + n:3 + 2 * n]), res[-1]


def _exchange_wait(started, gathers, after, name):
    send_sems, recv_sems, own_sems, arrays, lands = started
    n = len(arrays)

    def body(*refs):
        srcs, dsts = refs[:n], refs[n:2 * n]
        for send, arrival in _peer_copies(srcs, dsts, gathers, refs[2 * n], refs[2 * n + 1]):
            arrival.wait_recv()
            send.wait_send()
        for cp in _own_copies(srcs, dsts, gathers, refs[2 * n + 2]):
            cp.wait()

    hbm_like = [pltpu.HBM(a.shape, a.dtype) for a in list(arrays) + list(lands)]
    res = pl.pallas_call(
        body, name=name,
        in_specs=[_HBM] * (2 * n) + [_SEM, _SEM, _SEM, _ANY],
        out_specs=[_HBM] * (2 * n), out_shape=hbm_like,
        input_output_aliases={i: i for i in range(2 * n)},
        compiler_params=pltpu.CompilerParams(has_side_effects=_DATAFLOW),
    )(*arrays, *lands, send_sems, recv_sems, own_sems, after)
    return res[n:]


def _reduce_adamw(parts, w, m, v, name):
    n, R, C = parts.shape
    tr = 256 if R % 256 == 0 else R

    def body(p_ref, w_ref, m_ref, v_ref, g_ref, d_ref, nm_ref, nv_ref):
        g = p_ref[0].astype(F32)
        for s in range(1, n):
            g = g + p_ref[s].astype(F32)
        g_ref[...] = g
        m_new = ADAM_B1 * m_ref[...] + (1.0 - ADAM_B1) * g
        v_new = ADAM_B2 * v_ref[...] + (1.0 - ADAM_B2) * (g * g)
        nm_ref[...] = m_new
        nv_ref[...] = v_new
        m_hat = m_new / (1.0 - ADAM_B1 ** ADAM_STEP)
        v_hat = v_new / (1.0 - ADAM_B2 ** ADAM_STEP)
        d_ref[...] = -ADAM_LR * (m_hat / (jnp.sqrt(v_hat) + ADAM_EPS) + ADAM_WD * w_ref[...])

    blk = pl.BlockSpec((tr, C), lambda i: (i, 0))
    out = jax.ShapeDtypeStruct((R, C), F32)
    return pl.pallas_call(
        body, name=name, grid=(R // tr,),
        in_specs=[pl.BlockSpec((n, tr, C), lambda i: (0, i, 0)), blk, blk, blk],
        out_specs=[blk] * 4, out_shape=[out] * 4,
        compiler_params=_params("parallel"),
    )(parts, w, m, v)


def _pack(arrays):
    flat = jnp.concatenate([a.reshape(-1).astype(F32) for a in arrays])
    pad = (-flat.shape[0]) % (8 * LANES)
    return jnp.pad(flat, (0, pad)).reshape(-1, LANES)


def _unpack(buf, shapes):
    flat = buf.reshape(-1)
    out, off = [], 0
    for shp in shapes:
        size = 1
        for s in shp:
            size *= s
        out.append(flat[off:off + size].reshape(shp))
        off += size
    return out


def _block_diag_pairs(w):
    w = w.reshape(N_CBLK, 2, LRU_BLOCK_DIM, LRU_BLOCK_DIM)
    z = jnp.zeros_like(w[:, 0])
    top = jnp.concatenate([w[:, 0], z], axis=2)
    bot = jnp.concatenate([z, w[:, 1]], axis=2)
    return jnp.concatenate([top, bot], axis=1)


def _diag_pairs(m):
    h = LRU_BLOCK_DIM
    return jnp.stack([m[:, :h, :h], m[:, h:, h:]], axis=1).reshape(2 * N_CBLK, h, h)


SMALL = ("mlp_norm", "lru_conv_b", "lru_w_r", "lru_b_r", "lru_w_i", "lru_b_i",
         "lru_lambda", "fox_b_f", "fox_q_gain", "fox_k_gain")
WEIGHTS = ("mix_norm", "mlp_norm", "mlp_w1", "mlp_w2", "lru_w_in", "lru_conv_w", "lru_conv_b",
           "lru_w_r", "lru_b_r", "lru_w_i", "lru_b_i", "lru_lambda", "lru_w_out", "fox_w_in",
           "fox_b_f", "fox_q_gain", "fox_k_gain", "fox_w_out")


def kernel(x, mix_norm, mlp_norm, mlp_w1, mlp_w2, lru_w_in, lru_conv_w, lru_conv_b, lru_w_r, lru_b_r, lru_w_i, lru_b_i, lru_lambda, lru_w_out, fox_w_in, fox_b_f, fox_q_gain, fox_k_gain, fox_w_out, loss_target, m_mix_norm, m_mlp_norm, m_mlp_w1, m_mlp_w2, m_lru_w_in, m_lru_conv_w, m_lru_conv_b, m_lru_w_r, m_lru_b_r, m_lru_w_i, m_lru_b_i, m_lru_lambda, m_lru_w_out, m_fox_w_in, m_fox_b_f, m_fox_q_gain, m_fox_k_gain, m_fox_w_out, v_mix_norm, v_mlp_norm, v_mlp_w1, v_mlp_w2, v_lru_w_in, v_lru_conv_w, v_lru_conv_b, v_lru_w_r, v_lru_b_r, v_lru_w_i, v_lru_b_i, v_lru_lambda, v_lru_w_out, v_fox_w_in, v_fox_b_f, v_fox_q_gain, v_fox_k_gain, v_fox_w_out):
    w_in = dict(mix_norm=mix_norm, mlp_norm=mlp_norm, mlp_w1=mlp_w1, mlp_w2=mlp_w2,
                lru_w_in=lru_w_in, lru_conv_w=lru_conv_w, lru_conv_b=lru_conv_b, lru_w_r=lru_w_r,
                lru_b_r=lru_b_r, lru_w_i=lru_w_i, lru_b_i=lru_b_i, lru_lambda=lru_lambda,
                lru_w_out=lru_w_out, fox_w_in=fox_w_in, fox_b_f=fox_b_f, fox_q_gain=fox_q_gain,
                fox_k_gain=fox_k_gain, fox_w_out=fox_w_out)
    m_in = dict(mix_norm=m_mix_norm, mlp_norm=m_mlp_norm, mlp_w1=m_mlp_w1, mlp_w2=m_mlp_w2,
                lru_w_in=m_lru_w_in, lru_conv_w=m_lru_conv_w, lru_conv_b=m_lru_conv_b,
                lru_w_r=m_lru_w_r, lru_b_r=m_lru_b_r, lru_w_i=m_lru_w_i, lru_b_i=m_lru_b_i,
                lru_lambda=m_lru_lambda, lru_w_out=m_lru_w_out, fox_w_in=m_fox_w_in,
                fox_b_f=m_fox_b_f, fox_q_gain=m_fox_q_gain, fox_k_gain=m_fox_k_gain,
                fox_w_out=m_fox_w_out)
    v_in = dict(mix_norm=v_mix_norm, mlp_norm=v_mlp_norm, mlp_w1=v_mlp_w1, mlp_w2=v_mlp_w2,
                lru_w_in=v_lru_w_in, lru_conv_w=v_lru_conv_w, lru_conv_b=v_lru_conv_b,
                lru_w_r=v_lru_w_r, lru_b_r=v_lru_b_r, lru_w_i=v_lru_w_i, lru_b_i=v_lru_b_i,
                lru_lambda=v_lru_lambda, lru_w_out=v_lru_w_out, fox_w_in=v_fox_w_in,
                fox_b_f=v_fox_b_f, fox_q_gain=v_fox_q_gain, fox_k_gain=v_fox_k_gain,
                fox_w_out=v_fox_w_out)
    D = D_MODEL
    S = x.shape[1]
    x0, target = x[0], loss_target[0]
    me = 4 * lax.axis_index("x") + 2 * lax.axis_index("y") + lax.axis_index("c")

    def bf16(a):
        return a.astype(BF16)

    (lru_in_g,) = _exchange([bf16(lru_w_in[0])], [True], "gather_lru_in")
    gather_lru, tok = _exchange_start([bf16(lru_w_out[0]), lru_conv_w[0]], [True] * 2, lru_in_g,
                                      "gather_lru_start")
    gather_mlp0, tok = _exchange_start([bf16(mlp_w1[0]), bf16(mlp_w2[0])], [True] * 2, tok,
                                       "gather_mlp0_start")
    gather_fox, tok = _exchange_start([bf16(fox_w_in[0]), bf16(fox_w_out[0])], [True] * 2, tok,
                                      "gather_fox_start")
    gather_mlp1, tok = _exchange_start([bf16(mlp_w1[1]), bf16(mlp_w2[1])], [True] * 2, tok,
                                       "gather_mlp1_start")
    wr =_block_diag_pairs(lru_w_r[0]).astype(BF16)
    wi = _block_diag_pairs(lru_w_i[0]).astype(BF16)
    b_r, b_i = lru_b_r.reshape(1, D), lru_b_i.reshape(1, D)
    q_gain, k_gain = jnp.tile(fox_q_gain, (1, 2)), jnp.tile(fox_k_gain, (1, 2))
    b_f = jnp.pad(fox_b_f, ((0, 0), (0, LANES - N_HEADS)))
    g_mix0, g_mix1 = mix_norm[0:1] + tok[0, 0], mix_norm[1:2]
    g_mlp0, g_mlp1 = mlp_norm[0:1], mlp_norm[1:2]

    (u0,), h0 = _norm_matmul(x0, g_mix0, [lru_in_g], "lru_in_proj")
    lru_out_g, conv_g = _exchange_wait(gather_lru, [True] * 2, u0, "gather_lru_wait")
    lru_out_w = lru_out_g.reshape(D, D)
    conv_w = conv_g.transpose(1, 0, 2).reshape(CONV_WIDTH, D)
    y_lru, hs =_lru_fwd(u0, conv_w, lru_conv_b, wr, b_r, wi, b_i, lru_lambda, "lru_core")
    x1 = _matmul_res(y_lru, lru_out_w, x0, "lru_out_proj")
    w1g0, w2g0 = _exchange_wait(gather_mlp0, [True] * 2, x1, "gather_mlp0_wait")
    x2, h1, r1 = _mlp_fwd(x1, g_mlp0, w1g0, w2g0, "mlp0")
    fox_in_g, fox_out_g = _exchange_wait(gather_fox, [True] * 2, x2, "gather_fox_wait")
    fox_out_w = fox_out_g.reshape(D, D)
    fox_full = fox_in_g.transpose(1, 0, 2).reshape(D, 3 * D + N_HEADS)
    wqkv = fox_full[:, :3 * D].reshape(D, 3, D).transpose(1, 0, 2)
    wf = jnp.pad(fox_full[:, 3 * D:], ((0, 0), (0, LANES - N_HEADS)))[None]
    (u_qkv, f), h2 = _norm_matmul(x2, g_mix1, [wqkv, wf], "fox_in_proj")
    qn, kn, vb = _qk_prep(u_qkv, q_gain, k_gain, "fox_qk_norm")
    c_col = _forget_fwd(f, b_f, "fox_forget")
    c_row = c_col[:, :N_HEADS].T.reshape(N_CBLK, 2, S)
    o, lse = _attn_fwd(qn, kn, vb, c_row, "fox_attn")
    x3 = _matmul_res(o, fox_out_w, x2, "fox_out_proj")
    w1g1, w2g1 = _exchange_wait(gather_mlp1, [True] * 2, x3, "gather_mlp1_wait")
    x4, h3, r3 = _mlp_fwd(x3, g_mlp1, w1g1, w2g1, "mlp1")
    loss_local, dx4 = _loss_head(x4, target, "loss_head")

    dx3, dg_mlp1, da3 = _mlp_bwd(dx4, x3, g_mlp1, r3, w1g1, w2g1, "mlp1_bwd")
    dw1_1 = _matmul_tn(h3, da3, "mlp1_dw1", cols=2, col_blocks=N_DEV)
    dw2_1 = _matmul_tn(r3, dx4, "mlp1_dw2", rows=2, a_square=True).reshape(N_DEV, -1, D)
    grads_mlp1, tok = _exchange_start([dw1_1, dw2_1], [False] * 2, tok, "grads_mlp1_start")
    do = _matmul_nt(dx3, fox_out_w, "fox_out_bwd", BF16, tok)
    d_fox_out = _matmul_tn(o, dx3, "fox_out_dw").reshape(N_DEV, -1, D)
    dqn, dkn, dv, dc_row, rho = _attn_bwd(qn, kn, vb, do, o, lse, c_row, "fox_attn_bwd")
    duq, duk, dq_gain, dk_gain = _qk_bwd(u_qkv, dqn, dkn, q_gain, k_gain, "fox_qk_norm_bwd")
    dc_k = jnp.pad(dc_row.reshape(N_HEADS, S).T, ((0, 0), (0, LANES - N_HEADS)))
    df, db_f = _forget_bwd(dc_k, rho, f, b_f, "fox_forget_bwd")
    dx2, dg_mix1 = _proj_bwd([[duq, duk, dv], [df]], [wqkv, wf], x2, g_mix1, dx3, "fox_in_bwd")
    d_fox_in = jnp.concatenate(
        [_matmul_tn(h2, duq, "fox_in_dwq"), _matmul_tn(h2, duk, "fox_in_dwk"),
         _matmul_tn(h2, dv, "fox_in_dwv"), _matmul_tn(h2, df, "fox_in_dwf")[:, :N_HEADS]], axis=1)
    d_fox_in = d_fox_in.reshape(D, N_DEV, -1).transpose(1, 0, 2)
    grads_fox, tok = _exchange_start([d_fox_in, d_fox_out], [False] * 2, tok, "grads_fox_start")
    dx1, dg_mlp0, da1 = _mlp_bwd(dx2, x1, g_mlp0 + tok[0, 0], r1, w1g0, w2g0, "mlp0_bwd")
    dw1_0 = _matmul_tn(h1, da1, "mlp0_dw1", cols=2, col_blocks=N_DEV)
    dw2_0 = _matmul_tn(r1, dx2, "mlp0_dw2", rows=2, a_square=True).reshape(N_DEV, -1, D)
    grads_mlp0, tok = _exchange_start([dw1_0, dw2_0], [False] * 2, tok, "grads_mlp0_start")
    dy_lru = _matmul_nt(dx1, lru_out_w, "lru_out_bwd", F32, tok)
    d_lru_out = _matmul_tn(y_lru, dx1, "lru_out_dw").reshape(N_DEV, -1, D)
    dgp, dxb, d_conv_w, d_conv_b, d_b_r, d_b_i, d_lam, d_wr, d_wi = _lru_bwd(
        dy_lru, u0, hs, conv_w, lru_conv_b, wr, b_r, wi, b_i, lru_lambda, "lru_core_bwd")
    d_lru_in = jnp.concatenate([_matmul_tn(h0, dgp, "lru_in_dw_gate", col_blocks=4),
                                _matmul_tn(h0, dxb, "lru_in_dw_x", col_blocks=4)], axis=0)

    small_grads = dict(
        mlp_norm=jnp.concatenate([dg_mlp0, dg_mlp1], axis=0),
        lru_conv_b=d_conv_b, lru_w_r=_diag_pairs(d_wr), lru_b_r=d_b_r, lru_w_i=_diag_pairs(d_wi),
        lru_b_i=d_b_i, lru_lambda=d_lam, fox_b_f=db_f[:, :N_HEADS],
        fox_q_gain=dq_gain[:, :HEAD_DIM], fox_k_gain=dk_gain[:, :HEAD_DIM])
    small_partial = _pack([dg_mix1] + [small_grads[n] for n in SMALL] + [d_conv_w])
    grads_lru, tok = _exchange_start([d_lru_in, d_lru_out, small_partial], [False, False, True],
                                     tok, "grads_lru_start")
    dx0, dg_mix0 = _proj_bwd([[dgp, dxb]], [lru_in_g], x0, mix_norm[0:1] + tok[0, 0], dx1,
                             "lru_in_bwd")
    (p_mix0,) = _exchange([dg_mix0], [True], "gather_mix0_grad")

    grads, deltas, new_m, new_v = {}, {}, {}, {}

    def update(name, parts, sel=None):
        w, m, v = w_in[name], m_in[name], v_in[name]
        if sel is not None:
            w, m, v = w[sel], m[sel], v[sel]
        shape = w.shape
        two_d = (-1, shape[-1])
        res = _reduce_adamw(parts.reshape((N_DEV,) + w.reshape(two_d).shape), w.reshape(two_d),
                            m.reshape(two_d), v.reshape(two_d),
                            "adamw_" + name + ("" if sel is None else "_%d" % sel))
        return [r.reshape(shape) for r in res]

    def store(name, res):
        grads[name], deltas[name], new_m[name], new_v[name] = res

    p_w1_1, p_w2_1 = _exchange_wait(grads_mlp1, [False] * 2, p_mix0, "grads_mlp1_wait")
    up_w1_1, up_w2_1 = update("mlp_w1", p_w1_1, 1), update("mlp_w2", p_w2_1, 1)
    p_fox_in, p_fox_out = _exchange_wait(grads_fox, [False] * 2, up_w2_1[0], "grads_fox_wait")
    store("fox_w_in", update("fox_w_in", p_fox_in))
    store("fox_w_out", update("fox_w_out", p_fox_out))
    p_w1_0, p_w2_0 = _exchange_wait(grads_mlp0, [False] * 2, grads["fox_w_out"], "grads_mlp0_wait")
    up_w1_0, up_w2_0 = update("mlp_w1", p_w1_0, 0), update("mlp_w2", p_w2_0, 0)
    store("mlp_w1", [jnp.stack(p) for p in zip(up_w1_0, up_w1_1)])
    store("mlp_w2", [jnp.stack(p) for p in zip(up_w2_0, up_w2_1)])
    p_lru_in, p_lru_out, p_small = _exchange_wait(grads_lru, [False, False, True], up_w2_0[0],
                                                  "grads_lru_wait")
    store("lru_w_in", update("lru_w_in", p_lru_in))
    store("lru_w_out", update("lru_w_out", p_lru_out))

    mix0 = _reduce_adamw(p_mix0, mix_norm[0:1], m_mix_norm[0:1], v_mix_norm[0:1], "adamw_mix0")
    packed = lambda src, first: _pack([first] + [src[n] for n in SMALL] + [jnp.zeros((CONV_WIDTH, D))])
    small_shapes = [(1, D)] + [w_in[n].shape for n in SMALL]
    n_small = sum(math.prod(s) for s in small_shapes)
    res_small = _reduce_adamw(p_small, packed(w_in, mix_norm[1:2]), packed(m_in, m_mix_norm[1:2]),
                              packed(v_in, v_mix_norm[1:2]), "adamw_small")
    for name, *vals in zip(("mix1",) + SMALL, *[_unpack(r, small_shapes) for r in res_small]):
        if name == "mix1":
            vals = [jnp.concatenate([r0, r1], axis=0) for r0, r1 in zip(mix0, vals)]
            name = "mix_norm"
        store(name, vals)
    conv_parts = p_small.reshape(N_DEV, -1)[:, n_small:n_small + CONV_WIDTH * D]
    conv_parts = conv_parts.reshape(N_DEV, CONV_WIDTH, N_DEV, LANES)
    conv_parts = lax.dynamic_index_in_dim(conv_parts, me, axis=2, keepdims=False)
    store("lru_conv_w", update("lru_conv_w", conv_parts))

    loss = lax.psum(loss_local[0, 0], ("x", "y", "c"))
    return (loss, dx0[None], *[grads[n] for n in WEIGHTS], *[deltas[n] for n in WEIGHTS],
            *[new_m[n] for n in WEIGHTS], *[new_v[n] for n in WEIGHTS])
```

```python
import math

import jax
import jax.numpy as jnp
from jax import lax
from jax.experimental import pallas as pl
from jax.experimental.pallas import tpu as pltpu

F32 = jnp.float32
BF16 = jnp.bfloat16

N_DEV = 8
D_MODEL = 1024
D_FF = 4096
N_HEADS = 16
HEAD_DIM = 64
LRU_BLOCK_DIM = 64
CONV_WIDTH = 4
LRU_C = 8.0
EPS = 1e-6
NEG_INF = -1e30
ATTN_SCALE = HEAD_DIM ** -0.5
LANES = 128
N_CBLK = D_MODEL // LANES
VMEM_LIMIT = 52 * 2 ** 20

ADAM_LR = 0.001
ADAM_B1 = 0.9
ADAM_B2 = 0.999
ADAM_EPS = 1e-08
ADAM_WD = 0.01
ADAM_STEP = 10

_NT = (((1,), (1,)), ((), ()))
_TN = (((0,), (0,)), ((), ()))


def _params(*sem):
    return pltpu.CompilerParams(dimension_semantics=sem, vmem_limit_bytes=VMEM_LIMIT)


def _resident(shape):
    zeros = (0,) * len(shape)
    return pl.BlockSpec(shape, lambda *_: zeros, pipeline_mode=pl.Buffered(1))


def _dot(a, b):
    return jnp.dot(a, b, preferred_element_type=F32)


def _dot_nt(a, b):
    return lax.dot_general(a, b, _NT, preferred_element_type=F32)


def _dot_tn(a, b):
    return lax.dot_general(a, b, _TN, preferred_element_type=F32)


def _sigmoid(x):
    return 1.0 / (1.0 + jnp.exp(-x))


def _log_sigmoid(x):
    return -(jnp.maximum(-x, 0.0) + jnp.log1p(jnp.exp(-jnp.abs(x))))


def _expm1(x):
    poly = x * (1.0 + x * (0.5 + x * (1.0 / 6.0 + x * (1.0 / 24.0 + x * (1.0 / 120.0)))))
    return jnp.where(jnp.abs(x) < 0.1, poly, jnp.exp(x) - 1.0)


_GELU_K = 0.7978845608028654


def _gelu(x):
    return 0.5 * x * (1.0 + jnp.tanh(_GELU_K * (x + 0.044715 * (x * x * x))))


def _gelu_grad(x):
    t = jnp.tanh(_GELU_K * (x + 0.044715 * (x * x * x)))
    return 0.5 * (1.0 + t) + 0.5 * x * (1.0 - t * t) * (_GELU_K * (1.0 + 3 * 0.044715 * x * x))


def _rms_scale(x):
    return lax.rsqrt(jnp.mean(x * x, axis=-1, keepdims=True) + EPS)


def _norm_bwd(dh, x, g):
    rs = _rms_scale(x)
    xhat = x * rs
    dxhat = dh * g
    dx = rs * (dxhat - xhat * jnp.mean(dxhat * xhat, axis=-1, keepdims=True))
    return dx, jnp.sum(dh * xhat, axis=0, keepdims=True)


def _token_tile(S, want):
    tm = min(S, want)
    assert S % tm == 0
    return tm


def _norm_matmul(x, g, ws, name, tm=256):
    S, D = x.shape
    tm = _token_tile(S, tm)
    n = len(ws)

    def body(x_ref, g_ref, *refs):
        w_refs, o_refs, h_ref = refs[:n], refs[n:2 * n], refs[2 * n]
        xv = x_ref[...]
        h = (xv * _rms_scale(xv) * g_ref[...]).astype(BF16)
        h_ref[...] = h
        for w_ref, o_ref in zip(w_refs, o_refs):
            nb, _, nw = w_ref.shape
            for d in range(nb):
                o_ref[:, d * nw:(d + 1) * nw] = _dot(h, w_ref[d])

    widths = [w.shape[0] * w.shape[2] for w in ws]
    outs = pl.pallas_call(
        body, name=name, grid=(S // tm,),
        in_specs=[pl.BlockSpec((tm, D), lambda i: (i, 0)), _resident((1, D))]
        + [_resident(w.shape) for w in ws],
        out_specs=[pl.BlockSpec((tm, n_), lambda i: (i, 0)) for n_ in widths]
        + [pl.BlockSpec((tm, D), lambda i: (i, 0))],
        out_shape=[jax.ShapeDtypeStruct((S, n_), F32) for n_ in widths]
        + [jax.ShapeDtypeStruct((S, D), BF16)],
        compiler_params=_params("parallel"),
    )(x, g, *ws)
    return outs[:n], outs[n]


def _matmul_res(a, w, res, name, tm=512):
    S, K = a.shape
    N = w.shape[1]
    tm = _token_tile(S, tm)

    def body(a_ref, w_ref, r_ref, o_ref):
        o_ref[...] = r_ref[...] + _dot(a_ref[...], w_ref[...])

    return pl.pallas_call(
        body, name=name, grid=(S // tm,),
        in_specs=[pl.BlockSpec((tm, K), lambda i: (i, 0)), _resident((K, N)),
                  pl.BlockSpec((tm, N), lambda i: (i, 0))],
        out_specs=pl.BlockSpec((tm, N), lambda i: (i, 0)),
        out_shape=jax.ShapeDtypeStruct((S, N), F32),
        compiler_params=_params("parallel"),
    )(a, w, res)


def _matmul_nt(a, w, name, out_dtype, after, tm=512):
    S, N = a.shape
    K = w.shape[0]
    tm = _token_tile(S, tm)

    def body(a_ref, w_ref, after_ref, o_ref):
        o_ref[...] = _dot_nt(a_ref[...].astype(BF16), w_ref[...]).astype(out_dtype)

    return pl.pallas_call(
        body, name=name, grid=(S // tm,),
        in_specs=[pl.BlockSpec((tm, N), lambda i: (i, 0)), _resident((K, N)),
                  pl.BlockSpec(memory_space=pl.ANY)],
        out_specs=pl.BlockSpec((tm, K), lambda i: (i, 0)),
        out_shape=jax.ShapeDtypeStruct((S, K), out_dtype),
        compiler_params=_params("parallel"),
    )(a, w, after)


def _proj_bwd(a_lists, w_list, x, g, res, name, tm=256):
    S, D = x.shape
    tm = _token_tile(S, tm)
    a_list = [a for group in a_lists for a in group]
    n, n_w = len(a_list), len(w_list)

    def body(*refs):
        a_refs, w_refs = list(refs[:n]), refs[n:n + n_w]
        x_ref, g_ref, r_ref, dx_ref, dg_ref = refs[n + n_w:]
        dh = jnp.zeros((tm, D), F32)
        for group, w_ref in zip(a_lists, w_refs):
            nw = w_ref.shape[2]
            d = 0
            for _ in group:
                a_ref = a_refs.pop(0)
                for j in range(a_ref.shape[1] // nw):
                    dh = dh + _dot_nt(a_ref[:, j * nw:(j + 1) * nw].astype(BF16), w_ref[d])
                    d += 1
        dx, dg = _norm_bwd(dh, x_ref[...], g_ref[...])
        dx_ref[...] = r_ref[...] + dx

        @pl.when(pl.program_id(0) == 0)
        def _():
            dg_ref[...] = jnp.zeros_like(dg_ref)
        dg_ref[...] += dg

    tok = lambda width: pl.BlockSpec((tm, width), lambda i: (i, 0))
    return pl.pallas_call(
        body, name=name, grid=(S // tm,),
        in_specs=[tok(a.shape[1]) for a in a_list] + [_resident(w.shape) for w in w_list]
        + [tok(D), _resident((1, D)), tok(D)],
        out_specs=[tok(D), pl.BlockSpec((1, D), lambda i: (0, 0))],
        out_shape=[jax.ShapeDtypeStruct((S, D), F32), jax.ShapeDtypeStruct((1, D), F32)],
        compiler_params=_params("arbitrary"),
    )(*a_list, *w_list, x, g, res)


def _matmul_tn(a, b, name, rows=1, cols=1, col_blocks=None, a_square=False, tm=1024):
    S, K = a.shape
    N = b.shape[1]
    tm = _token_tile(S, tm)
    n_tok = S // tm
    kr, nc = K // rows, N // cols

    def body(a_ref, b_ref, o_ref, acc_ref):
        av = a_ref[...]
        if a_square:
            av = av.astype(F32)
            av = av * av
        part = _dot_tn(av.astype(BF16), b_ref[...].astype(BF16))
        step = pl.program_id(2)

        @pl.when(step == 0)
        def _():
            acc_ref[...] = part

        @pl.when(step > 0)
        def _():
            acc_ref[...] += part

        @pl.when(step == n_tok - 1)
        def _():
            if col_blocks is None:
                o_ref[...] = acc_ref[...].astype(BF16)
            else:
                nw = N // col_blocks
                for d in range(col_blocks // cols):
                    o_ref[d] = acc_ref[:, d * nw:(d + 1) * nw].astype(BF16)

    if col_blocks is None:
        out_spec = pl.BlockSpec((kr, nc), lambda r, c, i: (r, c))
        out_shape = jax.ShapeDtypeStruct((K, N), BF16)
    else:
        assert rows == 1 and col_blocks % cols == 0
        per = col_blocks // cols
        out_spec = pl.BlockSpec((per, K, N // col_blocks), lambda r, c, i: (c, 0, 0))
        out_shape = jax.ShapeDtypeStruct((col_blocks, K, N // col_blocks), BF16)
    return pl.pallas_call(
        body, name=name, grid=(rows, cols, n_tok),
        in_specs=[pl.BlockSpec((tm, kr), lambda r, c, i: (i, r)),
                  pl.BlockSpec((tm, nc), lambda r, c, i: (i, c))],
        out_specs=out_spec, out_shape=out_shape,
        scratch_shapes=[pltpu.VMEM((kr, nc), F32)],
        compiler_params=_params("parallel", "parallel", "arbitrary"),
    )(a, b)


def _mlp_fwd(x, g, w1, w2, name, tm=256):
    S, D = x.shape
    nb, _, fb = w1.shape
    tm = _token_tile(S, tm)

    def body(x_ref, g_ref, w1_ref, w2_ref, o_ref, h_ref, r_ref):
        xv = x_ref[...]
        h = (xv * _rms_scale(xv) * g_ref[...]).astype(BF16)
        h_ref[...] = h
        acc = xv
        for d in range(nb):
            r = jnp.maximum(_dot(h, w1_ref[d]), 0.0)
            r_ref[:, d * fb:(d + 1) * fb] = r.astype(BF16)
            acc = acc + _dot((r * r).astype(BF16), w2_ref[d])
        o_ref[...] = acc

    tok = lambda width: pl.BlockSpec((tm, width), lambda i: (i, 0))
    return pl.pallas_call(
        body, name=name, grid=(S // tm,),
        in_specs=[tok(D), _resident((1, D)), _resident(w1.shape), _resident(w2.shape)],
        out_specs=[tok(D), tok(D), tok(nb * fb)],
        out_shape=[jax.ShapeDtypeStruct((S, D), F32), jax.ShapeDtypeStruct((S, D), BF16),
                   jax.ShapeDtypeStruct((S, nb * fb), BF16)],
        compiler_params=_params("parallel"),
    )(x, g, w1, w2)


def _mlp_bwd(dout, x, g, r, w1, w2, name, tm=256):
    S, D = x.shape
    nb, _, fb = w1.shape
    tm = _token_tile(S, tm)

    def body(do_ref, x_ref, g_ref, r_ref, w1_ref, w2_ref, dx_ref, dg_ref, da_ref):
        dov = do_ref[...]
        dob = dov.astype(BF16)
        dh = jnp.zeros((tm, D), F32)
        for d in range(nb):
            dz = _dot_nt(dob, w2_ref[d])
            da = (dz * (2.0 * r_ref[:, d * fb:(d + 1) * fb].astype(F32))).astype(BF16)
            da_ref[:, d * fb:(d + 1) * fb] = da
            dh = dh + _dot_nt(da, w1_ref[d])
        dx, dg = _norm_bwd(dh, x_ref[...], g_ref[...])
        dx_ref[...] = dov + dx

        @pl.when(pl.program_id(0) == 0)
        def _():
            dg_ref[...] = jnp.zeros_like(dg_ref)
        dg_ref[...] += dg

    tok = lambda width: pl.BlockSpec((tm, width), lambda i: (i, 0))
    return pl.pallas_call(
        body, name=name, grid=(S // tm,),
        in_specs=[tok(D), tok(D), _resident((1, D)), tok(nb * fb), _resident(w1.shape),
                  _resident(w2.shape)],
        out_specs=[tok(D), pl.BlockSpec((1, D), lambda i: (0, 0)), tok(nb * fb)],
        out_shape=[jax.ShapeDtypeStruct((S, D), F32), jax.ShapeDtypeStruct((1, D), F32),
                   jax.ShapeDtypeStruct((S, nb * fb), BF16)],
        compiler_params=_params("arbitrary"),
    )(dout, x, g, r, w1, w2)


def _scan_chunk(a, b, row, T, reverse):
    s = 1
    while s < T:
        if reverse:
            keep, shift = row < T - s, T - s
        else:
            keep, shift = row >= s, s
        a_sh = jnp.where(keep, pltpu.roll(a, shift, 0), 1.0)
        b_sh = jnp.where(keep, pltpu.roll(b, shift, 0), 0.0)
        b = a * b_sh + b
        a = a * a_sh
        s *= 2
    return a, b


def _row_of(x, row, r):
    return jnp.sum(jnp.where(row == r, x, 0.0), axis=0, keepdims=True)


def _shift_down(x, prev, row, k):
    if k == 0:
        return x
    return jnp.where(row < k, pltpu.roll(prev, k, 0), pltpu.roll(x, k, 0))


def _shift_up(x, nxt, row, k, T):
    if k == 0:
        return x
    return jnp.where(row < T - k, pltpu.roll(x, T - k, 0), pltpu.roll(nxt, T - k, 0))


def _lru_gates(xb, prev_xb, row, cw_ref, cb, wr, br, wi, bi, ls):
    xc = cb + cw_ref[pl.ds(0, 1), :] * _shift_down(xb, prev_xb, row, 3)
    for k in (2, 1, 0):
        xc = xc + cw_ref[pl.ds(3 - k, 1), :] * _shift_down(xb, prev_xb, row, k)
    xcb = xc.astype(BF16)
    r = _sigmoid(_dot(xcb, wr) + br)
    i = _sigmoid(_dot(xcb, wi) + bi)
    la = (LRU_C * r) * ls
    a = jnp.exp(la)
    m = jnp.sqrt(-_expm1(2.0 * la))
    return xc, xcb, r, i, a, m


def _lru_specs(S):
    col = lambda off: pl.BlockSpec((S, LANES), lambda j: (0, j + off))
    vec = pl.BlockSpec((1, LANES), lambda j: (0, j))
    mat = pl.BlockSpec((None, LANES, LANES), lambda j: (j, 0, 0))
    cwm = pl.BlockSpec((CONV_WIDTH, LANES), lambda j: (0, j))
    return col, vec, mat, cwm


def _lru_fwd(u, conv_w, conv_b, wr, br, wi, bi, lam, name):
    S = u.shape[0]
    T = _token_tile(S, 256)
    col, vec, mat, cwm = _lru_specs(S)

    def body(gp_ref, xb_ref, cw_ref, cb_ref, wr_ref, br_ref, wi_ref, bi_ref, lam_ref,
             y_ref, hs_ref):
        row = lax.broadcasted_iota(jnp.int32, (T, LANES), 0)
        ls = _log_sigmoid(lam_ref[...])
        cb, br, bi = cb_ref[...], br_ref[...], bi_ref[...]
        wr, wi = wr_ref[...], wi_ref[...]

        def chunk(ci, carry):
            prev_xb, hc = carry
            rows = pl.ds(pl.multiple_of(ci * T, T), T)
            xb = xb_ref[rows, :]
            xc, _, _, i, a, m = _lru_gates(xb, prev_xb, row, cw_ref, cb, wr, br, wi, bi, ls)
            ca, cbv = _scan_chunk(a, m * (i * xc), row, T, reverse=False)
            h = ca * hc + cbv
            hs_ref[rows, :] = h
            y_ref[rows, :] = (_gelu(gp_ref[rows, :]) * h).astype(BF16)
            return xb, _row_of(h, row, T - 1)

        lax.fori_loop(0, S // T, chunk,
                      (jnp.zeros((T, LANES), F32), jnp.zeros((1, LANES), F32)))

    return pl.pallas_call(
        body, name=name, grid=(N_CBLK,),
        in_specs=[col(0), col(N_CBLK), cwm, vec, mat, vec, mat, vec, vec],
        out_specs=[col(0), col(0)],
        out_shape=[jax.ShapeDtypeStruct((S, D_MODEL), BF16), jax.ShapeDtypeStruct((S, D_MODEL), F32)],
        compiler_params=_params("parallel"),
    )(u, u, conv_w, conv_b, wr, br, wi, bi, lam)


def _lru_bwd(dy, u, hs, conv_w, conv_b, wr, br, wi, bi, lam, name):
    S = u.shape[0]
    T = _token_tile(S, 256)
    n_chunk = S // T
    col, vec, mat, cwm = _lru_specs(S)

    def body(dy_ref, gp_ref, xb_ref, hs_ref, cw_ref, cb_ref, wr_ref, br_ref, wi_ref, bi_ref,
             lam_ref, dgp_ref, dxb_ref, dcw_ref, dcb_ref, dbr_ref, dbi_ref, dlam_ref, dwr_ref,
             dwi_ref):
        row = lax.broadcasted_iota(jnp.int32, (T, LANES), 0)
        lam = lam_ref[...]
        ls = _log_sigmoid(lam)
        cb, br, bi = cb_ref[...], br_ref[...], bi_ref[...]
        wr, wi = wr_ref[...], wi_ref[...]
        for ref in (dcw_ref, dcb_ref, dbr_ref, dbi_ref, dlam_ref, dwr_ref, dwi_ref):
            ref[...] = jnp.zeros_like(ref)

        def chunk(it, carry):
            g_next, dxc_next = carry
            ci = n_chunk - 1 - it
            rows = pl.ds(pl.multiple_of(ci * T, T), T)
            before = pl.ds(pl.multiple_of(jnp.maximum(ci - 1, 0) * T, T), T)
            first = ci == 0
            xb = xb_ref[rows, :]
            prev_xb = jnp.where(first, 0.0, xb_ref[before, :])
            xc, xcb, r, i, a, m = _lru_gates(xb, prev_xb, row, cw_ref, cb, wr, br, wi, bi, ls)
            h = hs_ref[rows, :]
            h_prev = _shift_down(h, jnp.where(first, 0.0, hs_ref[before, :]), row, 1)
            gp = gp_ref[rows, :]
            dyv = dy_ref[rows, :]
            dgp_ref[rows, :] = (dyv * h * _gelu_grad(gp)).astype(BF16)
            dh = dyv * _gelu(gp)
            ca, cbv = _scan_chunk(a, a * dh, row, T, reverse=True)
            gp_acc = ca * g_next + cbv
            g = dh + jnp.where(row < T - 1, pltpu.roll(gp_acc, T - 1, 0), g_next)
            da = g * h_prev - (g * (i * xc)) * a / m
            dla = da * a
            dlam_ref[...] += jnp.sum(dla * (LRU_C * r), axis=0, keepdims=True)
            dpr = (dla * (LRU_C * ls)) * r * (1.0 - r)
            dpi = (g * m * xc) * i * (1.0 - i)
            dbr_ref[...] += jnp.sum(dpr, axis=0, keepdims=True)
            dbi_ref[...] += jnp.sum(dpi, axis=0, keepdims=True)
            dprb, dpib = dpr.astype(BF16), dpi.astype(BF16)
            dwr_ref[...] += _dot_tn(xcb, dprb)
            dwi_ref[...] += _dot_tn(xcb, dpib)
            dxc = g * m * i + _dot_nt(dprb, wr) + _dot_nt(dpib, wi)
            dcb_ref[...] += jnp.sum(dxc, axis=0, keepdims=True)
            dxb = jnp.zeros((T, LANES), F32)
            for k in range(CONV_WIDTH):
                tap = pl.ds(CONV_WIDTH - 1 - k, 1)
                dcw_ref[tap, :] += jnp.sum(dxc * _shift_down(xb, prev_xb, row, k), axis=0,
                                           keepdims=True)
                dxb = dxb + cw_ref[tap, :] * _shift_up(dxc, dxc_next, row, k, T)
            dxb_ref[rows, :] = dxb.astype(BF16)
            return _row_of(gp_acc, row, 0), dxc

        lax.fori_loop(0, n_chunk, chunk,
                      (jnp.zeros((1, LANES), F32), jnp.zeros((T, LANES), F32)))
        dlam_ref[...] = dlam_ref[...] * _sigmoid(-lam)

    vec_out = jax.ShapeDtypeStruct((1, D_MODEL), F32)
    mat_out = jax.ShapeDtypeStruct((N_CBLK, LANES, LANES), F32)
    return pl.pallas_call(
        body, name=name, grid=(N_CBLK,),
        in_specs=[col(0), col(0), col(N_CBLK), col(0), cwm, vec, mat, vec, mat, vec, vec],
        out_specs=[col(0), col(0), cwm, vec, vec, vec, vec, mat, mat],
        out_shape=[jax.ShapeDtypeStruct((S, D_MODEL), BF16), jax.ShapeDtypeStruct((S, D_MODEL), BF16),
                   jax.ShapeDtypeStruct((CONV_WIDTH, D_MODEL), F32),
                   vec_out, vec_out, vec_out, vec_out, mat_out, mat_out],
        compiler_params=_params("parallel"),
    )(dy, u, u, hs, conv_w, conv_b, wr, br, wi, bi, lam)


def _head_group_matrix(value):
    r = lax.broadcasted_iota(jnp.int32, (LANES, LANES), 0) // HEAD_DIM
    c = lax.broadcasted_iota(jnp.int32, (LANES, LANES), 1) // HEAD_DIM
    return jnp.where(r == c, value, 0.0).astype(BF16)


def _group_dot(x, p):
    hi = x.astype(BF16)
    lo = (x - hi.astype(F32)).astype(BF16)
    return _dot(hi, p) + _dot(lo, p)


def _head_mean(x, p):
    return _group_dot(x, p)


def _qk_prep(u, q_gain, k_gain, name, tm=256):
    S = u.shape[0]
    tm = _token_tile(S, tm)

    def body(q_ref, k_ref, v_ref, qg_ref, kg_ref, qn_ref, kn_ref, vb_ref):
        p = _head_group_matrix(1.0 / HEAD_DIM)
        for j in range(N_CBLK):
            cl = slice(j * LANES, (j + 1) * LANES)
            for x_ref, g_ref, o_ref, scale in ((q_ref, qg_ref, qn_ref, ATTN_SCALE),
                                               (k_ref, kg_ref, kn_ref, 1.0)):
                xv = x_ref[:, cl]
                rs = lax.rsqrt(_head_mean(xv * xv, p) + EPS)
                o_ref[:, cl] = (xv * rs * g_ref[...]).astype(BF16) * scale
        vb_ref[...] = v_ref[...].astype(BF16)

    blk = lambda off: pl.BlockSpec((tm, D_MODEL), lambda i: (i, off))
    out = jax.ShapeDtypeStruct((S, D_MODEL), BF16)
    return pl.pallas_call(
        body, name=name, grid=(S // tm,),
        in_specs=[blk(0), blk(1), blk(2), _resident((1, LANES)), _resident((1, LANES))],
        out_specs=[blk(0), blk(0), blk(0)],
        out_shape=[out, out, out],
        compiler_params=_params("parallel"),
    )(u, u, u, q_gain, k_gain)


def _qk_bwd(u, dqn, dkn, q_gain, k_gain, name, tm=256):
    S = u.shape[0]
    tm = _token_tile(S, tm)

    def body(q_ref, k_ref, dqn_ref, dkn_ref, qg_ref, kg_ref, dq_ref, dk_ref, dqg_ref, dkg_ref):
        p = _head_group_matrix(1.0 / HEAD_DIM)
        for x_ref, dn_ref, g_ref, dx_ref, dg_ref, scale in (
                (q_ref, dqn_ref, qg_ref, dq_ref, dqg_ref, ATTN_SCALE),
                (k_ref, dkn_ref, kg_ref, dk_ref, dkg_ref, 1.0)):
            dg = jnp.zeros((1, LANES), F32)
            for j in range(N_CBLK):
                cl = slice(j * LANES, (j + 1) * LANES)
                xv, dn = x_ref[:, cl], dn_ref[:, cl] * scale
                rs = lax.rsqrt(_head_mean(xv * xv, p) + EPS)
                xhat = xv * rs
                dxhat = dn * g_ref[...]
                dx_ref[:, cl] = (rs * (dxhat - xhat * _head_mean(dxhat * xhat, p))).astype(BF16)
                dg = dg + jnp.sum(dn * xhat, axis=0, keepdims=True)

            @pl.when(pl.program_id(0) == 0)
            def _():
                dg_ref[...] = jnp.zeros_like(dg_ref)
            dg_ref[...] += dg

            @pl.when(pl.program_id(0) == S // tm - 1)
            def _():
                dg_ref[...] += pltpu.roll(dg_ref[...], HEAD_DIM, 1)

    blk = lambda off: pl.BlockSpec((tm, D_MODEL), lambda i: (i, off))
    acc = pl.BlockSpec((1, LANES), lambda i: (0, 0))
    out = jax.ShapeDtypeStruct((S, D_MODEL), BF16)
    vec = jax.ShapeDtypeStruct((1, LANES), F32)
    return pl.pallas_call(
        body, name=name, grid=(S // tm,),
        in_specs=[blk(0), blk(1), blk(0), blk(0), _resident((1, LANES)), _resident((1, LANES))],
        out_specs=[blk(0), blk(0), acc, acc],
        out_shape=[out, out, vec, vec],
        compiler_params=_params("arbitrary"),
    )(u, u, dqn, dkn, q_gain, k_gain)


def _forget_fwd(f, b_f, name):
    S = f.shape[0]
    T = _token_tile(S, 256)

    def body(f_ref, b_ref, c_ref):
        row = lax.broadcasted_iota(jnp.int32, (T, LANES), 0)
        ones = jnp.ones((T, LANES), F32)
        bias = b_ref[...]

        def chunk(ci, carry):
            rows = pl.ds(pl.multiple_of(ci * T, T), T)
            _, c = _scan_chunk(ones, _log_sigmoid(f_ref[rows, :] + bias), row, T, reverse=False)
            c = c + carry
            c_ref[rows, :] = c
            return _row_of(c, row, T - 1)

        lax.fori_loop(0, S // T, chunk, jnp.zeros((1, LANES), F32))

    return pl.pallas_call(
        body, name=name,
        in_specs=[pl.BlockSpec(memory_space=pltpu.VMEM)] * 2,
        out_specs=pl.BlockSpec(memory_space=pltpu.VMEM),
        out_shape=jax.ShapeDtypeStruct((S, LANES), F32),
        compiler_params=pltpu.CompilerParams(vmem_limit_bytes=VMEM_LIMIT),
    )(f, b_f)


def _forget_bwd(dc_k, rho, f, b_f, name):
    S = f.shape[0]
    T = _token_tile(S, 256)
    n_chunk = S // T

    def body(dck_ref, rho_ref, f_ref, b_ref, df_ref, db_ref):
        row = lax.broadcasted_iota(jnp.int32, (T, LANES), 0)
        ones = jnp.ones((T, LANES), F32)
        bias = b_ref[...]
        pick = (lax.broadcasted_iota(jnp.int32, (D_MODEL, LANES), 0)
                == HEAD_DIM * lax.broadcasted_iota(jnp.int32, (D_MODEL, LANES), 1))
        pick = jnp.where(pick, 1.0, 0.0).astype(BF16)

        def chunk(it, carry):
            tail, db = carry
            rows = pl.ds(pl.multiple_of((n_chunk - 1 - it) * T, T), T)
            dc = dck_ref[rows, :] + _group_dot(rho_ref[rows, :], pick)
            _, dlf = _scan_chunk(ones, dc, row, T, reverse=True)
            dlf = dlf + tail
            df = dlf * _sigmoid(-(f_ref[rows, :] + bias))
            df_ref[rows, :] = df
            return _row_of(dlf, row, 0), db + jnp.sum(df, axis=0, keepdims=True)

        zero = jnp.zeros((1, LANES), F32)
        _, db = lax.fori_loop(0, n_chunk, chunk, (zero, zero))
        db_ref[...] = db

    return pl.pallas_call(
        body, name=name,
        in_specs=[pl.BlockSpec(memory_space=pltpu.VMEM)] * 4,
        out_specs=[pl.BlockSpec(memory_space=pltpu.VMEM)] * 2,
        out_shape=[jax.ShapeDtypeStruct((S, LANES), F32), jax.ShapeDtypeStruct((1, LANES), F32)],
        compiler_params=pltpu.CompilerParams(vmem_limit_bytes=VMEM_LIMIT),
    )(dc_k, rho, f, b_f)


ATTN_TILE = 512
ATTN_ROWS_FWD = 32
ATTN_ROWS_BWD = 32


def _attn_tiles(S):
    t = _token_tile(S, ATTN_TILE)
    return t, S // t


def _causal(T):
    return (lax.broadcasted_iota(jnp.int32, (T, T), 1)
            <= lax.broadcasted_iota(jnp.int32, (T, T), 0))


def _attn_fwd(qs_, kn, vb, c_row, name):
    S = qs_.shape[0]
    T, n_t = _attn_tiles(S)
    RB = min(T, ATTN_ROWS_FWD)

    def body(q_ref, k_ref, v_ref, cr_ref, o_ref, lse_ref, s_ref, p_ref, m_ref, l_ref, acc_ref,
             a_ref):
        qi = pl.program_id(1)
        lanes = [slice(h2 * HEAD_DIM, (h2 + 1) * HEAD_DIM) for h2 in range(2)]
        col = lax.broadcasted_iota(jnp.int32, (RB, T), 1)
        row = lax.broadcasted_iota(jnp.int32, (RB, T), 0)
        m_ref[...] = jnp.full(m_ref.shape, NEG_INF, F32)
        l_ref[...] = jnp.zeros_like(l_ref)
        acc_ref[...] = jnp.zeros_like(acc_ref)

        def step(kj, masked):
            ks = pl.ds(pl.multiple_of(kj * T, T), T)
            for h2, hl in enumerate(lanes):
                s_ref[h2] = _dot_nt(q_ref[:, hl], k_ref[ks, hl]) - cr_ref[h2:h2 + 1, ks]
            for h2, hl in enumerate(lanes):
                blocks = [slice(i * RB, (i + 1) * RB) for i in range(T // RB)]

                def logits(i, rows):
                    s = s_ref[h2, rows, :]
                    return jnp.where(col <= row + i * RB, s, NEG_INF) if masked else s

                wide = lambda x: jnp.broadcast_to(x, (RB, LANES))
                for i, rows in enumerate(blocks):
                    mx = wide(jnp.max(logits(i, rows), axis=1, keepdims=True))
                    a_ref[h2, rows, :] = m_ref[h2, rows, :]
                    m_ref[h2, rows, :] = jnp.maximum(m_ref[h2, rows, :], mx)
                for i, rows in enumerate(blocks):
                    m_new = m_ref[h2, rows, :]
                    p = jnp.exp(logits(i, rows) - jnp.tile(m_new, (1, T // LANES)))
                    alpha = jnp.exp(a_ref[h2, rows, :] - m_new)
                    a_ref[h2, rows, :] = alpha
                    l_ref[h2, rows, :] = (alpha * l_ref[h2, rows, :]
                                          + wide(jnp.sum(p, axis=1, keepdims=True)))
                    p_ref[h2, rows, :] = p.astype(BF16)
                acc_ref[h2] = (a_ref[h2, :, :HEAD_DIM] * acc_ref[h2]
                               + _dot(p_ref[h2], v_ref[ks, hl]))

        def unmasked(kj, _):
            step(kj, False)
            return 0

        lax.fori_loop(0, qi, unmasked, 0)
        step(qi, True)
        for h2, hl in enumerate(lanes):
            o_ref[:, hl] = (acc_ref[h2] / l_ref[h2, :, :HEAD_DIM]).astype(BF16)
            lse_ref[:, hl] = m_ref[h2, :, :HEAD_DIM] + jnp.log(l_ref[h2, :, :HEAD_DIM])

    qblk = pl.BlockSpec((T, LANES), lambda h, i: (i, h))
    kv = pl.BlockSpec((S, LANES), lambda h, i: (0, h))
    return pl.pallas_call(
        body, name=name, grid=(N_CBLK, n_t),
        in_specs=[qblk, kv, kv, pl.BlockSpec((None, 2, S), lambda h, i: (h, 0, 0))],
        out_specs=[qblk, qblk],
        out_shape=[jax.ShapeDtypeStruct((S, D_MODEL), BF16),
                   jax.ShapeDtypeStruct((S, D_MODEL), F32)],
        scratch_shapes=[pltpu.VMEM((2, T, T), F32), pltpu.VMEM((2, T, T), BF16),
                        pltpu.VMEM((2, T, LANES), F32), pltpu.VMEM((2, T, LANES), F32),
                        pltpu.VMEM((2, T, HEAD_DIM), F32), pltpu.VMEM((2, T, LANES), F32)],
        compiler_params=_params("parallel", "parallel"),
    )(qs_, kn, vb, c_row)


def _attn_bwd(qs_, kn, vb, do, o, lse, c_row, name):
    S = qs_.shape[0]
    T, n_t = _attn_tiles(S)

    def body(q_ref, k_ref, v_ref, do_ref, o_ref, lse_ref, cr_ref,
             dq_ref, dk_ref, dv_ref, dc_ref, rho_ref, dd_ref):
        kj = pl.program_id(1)
        causal = _causal(T)
        lanes = [slice(h2 * HEAD_DIM, (h2 + 1) * HEAD_DIM) for h2 in range(2)]
        ones = [slice(h2 * HEAD_DIM, h2 * HEAD_DIM + 1) for h2 in range(2)]

        @pl.when(kj == 0)
        def _():
            dq_ref[...] = jnp.zeros_like(dq_ref)
            rho_ref[...] = jnp.zeros_like(rho_ref)
            p_sum = _head_group_matrix(1.0)

            def fill(ci, _):
                rows = pl.ds(pl.multiple_of(ci * T, T), T)
                dd_ref[rows, :] = _group_dot(do_ref[rows, :].astype(F32) * o_ref[rows, :].astype(F32),
                                             p_sum)
                return 0

            lax.fori_loop(0, n_t, fill, 0)

        kh = [k_ref[:, hl] for hl in lanes]
        vh = [v_ref[:, hl] for hl in lanes]
        ck = [cr_ref[h2:h2 + 1, :] for h2 in range(2)]

        def step(qi, carry, masked):
            qs = pl.ds(pl.multiple_of(qi * T, T), T)
            out = []
            for h2, hl in enumerate(lanes):
                dk, dv, dc = carry[h2]
                qh, doh = q_ref[qs, hl], do_ref[qs, hl]
                s = _dot_nt(qh, kh[h2]) - ck[h2]
                if masked:
                    s = jnp.where(causal, s, NEG_INF)
                p = jnp.exp(s - lse_ref[qs, ones[h2]])
                ds = p * (_dot_nt(doh, vh[h2]) - dd_ref[qs, ones[h2]])
                dsb = ds.astype(BF16)
                dq_ref[qs, hl] += _dot(dsb, kh[h2])
                rho_ref[qs, hl] += jnp.broadcast_to(jnp.sum(ds, axis=1, keepdims=True),
                                                    (T, HEAD_DIM))
                out.append((dk + _dot_tn(dsb, qh), dv + _dot_tn(p.astype(BF16), doh),
                            dc - jnp.sum(ds, axis=0, keepdims=True)))
            return tuple(out)

        init = tuple((jnp.zeros((T, HEAD_DIM), F32), jnp.zeros((T, HEAD_DIM), F32),
                      jnp.zeros((1, T), F32)) for _ in lanes)
        carry = step(kj, init, True)
        carry = lax.fori_loop(kj + 1, n_t, lambda qi, c: step(qi, c, False), carry)
        for h2, ((dk, dv, dc), hl) in enumerate(zip(carry, lanes)):
            dk_ref[:, hl] = dk
            dv_ref[:, hl] = dv.astype(BF16)
            dc_ref[h2:h2 + 1, :] = dc

    kblk = pl.BlockSpec((T, LANES), lambda h, j: (j, h))
    full = pl.BlockSpec((S, LANES), lambda h, j: (0, h))
    crow = pl.BlockSpec((None, 2, T), lambda h, j: (h, 0, j))
    wide = jax.ShapeDtypeStruct((S, D_MODEL), F32)
    return pl.pallas_call(
        body, name=name, grid=(N_CBLK, n_t),
        in_specs=[full, kblk, kblk, full, full, full, crow],
        out_specs=[full, kblk, kblk, crow, full],
        out_shape=[wide, wide, jax.ShapeDtypeStruct((S, D_MODEL), BF16),
                   jax.ShapeDtypeStruct((N_CBLK, 2, S), F32), wide],
        scratch_shapes=[pltpu.VMEM((S, LANES), F32)],
        compiler_params=_params("parallel", "arbitrary"),
    )(qs_, kn, vb, do, o, lse, c_row)


def _loss_head(y, target, name, tm=512):
    S, D = y.shape
    tm = _token_tile(S, tm)

    def body(y_ref, t_ref, loss_ref, dy_ref):
        err = y_ref[...] - t_ref[...]
        dy_ref[...] = err / D

        @pl.when(pl.program_id(0) == 0)
        def _():
            loss_ref[...] = jnp.zeros_like(loss_ref)
        row_loss = jnp.mean(err * err, axis=1, keepdims=True)
        loss_ref[...] += 0.5 * jnp.sum(row_loss, axis=0, keepdims=True)

    tok = pl.BlockSpec((tm, D), lambda i: (i, 0))
    return pl.pallas_call(
        body, name=name, grid=(S // tm,),
        in_specs=[tok, tok],
        out_specs=[pl.BlockSpec((1, 1), lambda i: (0, 0)), tok],
        out_shape=[jax.ShapeDtypeStruct((1, 1), F32), jax.ShapeDtypeStruct((S, D), F32)],
        compiler_params=_params("arbitrary"),
    )(y, target)


def _exchange(arrays, gathers, name):
    n = len(arrays)

    def body(*refs):
        ins, outs = refs[:n], refs[n:2 * n]
        send_sems, recv_sems, own_sems = refs[2 * n:]
        own = _own_copies(ins, outs, gathers, own_sems)
        for cp in own:
            cp.start()
        copies = _peer_copies(ins, outs, gathers, send_sems, recv_sems)
        for send, _ in copies:
            send.start()
        for send, arrival in copies:
            arrival.wait_recv()
            send.wait_send()
        for cp in own:
            cp.wait()

    hbm = pl.BlockSpec(memory_space=pl.ANY)
    return pl.pallas_call(
        body, name=name,
        in_specs=[hbm] * n, out_specs=[hbm] * n, out_shape=_landing_shapes(arrays, gathers),
        scratch_shapes=[pltpu.SemaphoreType.DMA((n * N_PEER,)),
                        pltpu.SemaphoreType.DMA((n * N_PEER,)),
                        pltpu.SemaphoreType.DMA((n,))],
        compiler_params=pltpu.CompilerParams(has_side_effects=True),
    )(*arrays)


N_PEER = N_DEV - 1


def _landing_shapes(arrays, gathers):
    return [jax.ShapeDtypeStruct((N_DEV,) + a.shape if g else a.shape, a.dtype)
            for a, g in zip(arrays, gathers)]


def _my_index():
    return 4 * lax.axis_index("x") + 2 * lax.axis_index("y") + lax.axis_index("c")


def _own_copies(srcs, lands, gathers, sems):
    me = _my_index()
    return [pltpu.make_async_copy(src if g else src.at[me], land.at[me], sems.at[a])
            for a, (src, land, g) in enumerate(zip(srcs, lands, gathers))]


def _peer_copies(srcs, lands, gathers, send_sems, recv_sems):
    x, y, c = lax.axis_index("x"), lax.axis_index("y"), lax.axis_index("c")
    me = 4 * x + 2 * y + c
    out = []
    for k in range(1, N_DEV):
        to = (1 - x if k & 4 else x, 1 - y if k & 2 else y, 1 - c if k & 1 else c)
        peer = 4 * to[0] + 2 * to[1] + to[2]
        for a, (src, land, g) in enumerate(zip(srcs, lands, gathers)):
            sem = a * N_PEER + k - 1
            src_blk = src if g else src.at[peer]

            def copy(slot, src_blk=src_blk, land=land, sem=sem, to=to):
                return pltpu.make_async_remote_copy(
                    src_ref=src_blk, dst_ref=land.at[slot], send_sem=send_sems.at[sem],
                    recv_sem=recv_sems.at[sem], device_id=to,
                    device_id_type=pl.DeviceIdType.MESH)

            out.append((copy(me), copy(peer)))
    return out


_HBM = pl.BlockSpec(memory_space=pltpu.HBM)
_SEM = pl.BlockSpec(memory_space=pltpu.SEMAPHORE)
_ANY = pl.BlockSpec(memory_space=pl.ANY)
_DATAFLOW = pltpu.SideEffectType.DATAFLOW_SIDE_EFFECTING


def _in_hbm(a):
    return pltpu.with_memory_space_constraint(a, pltpu.HBM)


def _exchange_start(arrays, gathers, after, name):
    n = len(arrays)
    lands = [lax.empty(s.shape, s.dtype) for s in _landing_shapes(arrays, gathers)]

    def body(*refs):
        srcs, dsts = refs[:n], refs[n:2 * n]
        send_sems, recv_sems, own_sems = refs[2 * n + 1:2 * n + 4]
        token = refs[-1]
        for cp in _own_copies(srcs, dsts, gathers, own_sems):
            cp.start()
        for send, _ in _peer_copies(srcs, dsts, gathers, send_sems, recv_sems):
            send.start()
        token[...] = jnp.zeros_like(token)

    hbm_like = [pltpu.HBM(a.shape, a.dtype) for a in list(arrays) + lands]
    res = pl.pallas_call(
        body, name=name,
        in_specs=[_HBM] * (2 * n) + [_ANY],
        out_specs=(_SEM, _SEM, _SEM, *[_HBM] * (2 * n), pl.BlockSpec(memory_space=pltpu.VMEM)),
        out_shape=(pltpu.SemaphoreType.DMA((n * N_PEER,)), pltpu.SemaphoreType.DMA((n * N_PEER,)),
                   pltpu.SemaphoreType.DMA((n,)), *hbm_like,
                   jax.ShapeDtypeStruct((8, LANES), F32)),
        input_output_aliases={i: 3 + i for i in range(2 * n)},
        compiler_params=pltpu.CompilerParams(has_side_effects=_DATAFLOW),
    )(*[_in_hbm(a) for a in list(arrays) + lands], after)
    return (res[0], res[1], res[2], res[3:3 + n], res[3 + n:3 + 2 * n]), res[-1]


def _exchange_wait(started, gathers, after, name):
    send_sems, recv_sems, own_sems, arrays, lands = started
    n = len(arrays)

    def body(*refs):
        srcs, dsts = refs[:n], refs[n:2 * n]
        for send, arrival in _peer_copies(srcs, dsts, gathers, refs[2 * n], refs[2 * n + 1]):
            arrival.wait_recv()
            send.wait_send()
        for cp in _own_copies(srcs, dsts, gathers, refs[2 * n + 2]):
            cp.wait()

    hbm_like = [pltpu.HBM(a.shape, a.dtype) for a in list(arrays) + list(lands)]
    res = pl.pallas_call(
        body, name=name,
        in_specs=[_HBM] * (2 * n) + [_SEM, _SEM, _SEM, _ANY],
        out_specs=[_HBM] * (2 * n), out_shape=hbm_like,
        input_output_aliases={i: i for i in range(2 * n)},
        compiler_params=pltpu.CompilerParams(has_side_effects=_DATAFLOW),
    )(*arrays, *lands, send_sems, recv_sems, own_sems, after)
    return res[n:]


def _reduce_adamw(parts, w, m, v, name):
    n, R, C = parts.shape
    tr = 256 if R % 256 == 0 else R

    def body(p_ref, w_ref, m_ref, v_ref, g_ref, d_ref, nm_ref, nv_ref):
        g = p_ref[0].astype(F32)
        for s in range(1, n):
            g = g + p_ref[s].astype(F32)
        g_ref[...] = g
        m_new = ADAM_B1 * m_ref[...] + (1.0 - ADAM_B1) * g
        v_new = ADAM_B2 * v_ref[...] + (1.0 - ADAM_B2) * (g * g)
        nm_ref[...] = m_new
        nv_ref[...] = v_new
        m_hat = m_new / (1.0 - ADAM_B1 ** ADAM_STEP)
        v_hat = v_new / (1.0 - ADAM_B2 ** ADAM_STEP)
        d_ref[...] = -ADAM_LR * (m_hat / (jnp.sqrt(v_hat) + ADAM_EPS) + ADAM_WD * w_ref[...])

    blk = pl.BlockSpec((tr, C), lambda i: (i, 0))
    out = jax.ShapeDtypeStruct((R, C), F32)
    return pl.pallas_call(
        body, name=name, grid=(R // tr,),
        in_specs=[pl.BlockSpec((n, tr, C), lambda i: (0, i, 0)), blk, blk, blk],
        out_specs=[blk] * 4, out_shape=[out] * 4,
        compiler_params=_params("parallel"),
    )(parts, w, m, v)


def _pack(arrays):
    flat = jnp.concatenate([a.reshape(-1).astype(F32) for a in arrays])
    pad = (-flat.shape[0]) % (8 * LANES)
    return jnp.pad(flat, (0, pad)).reshape(-1, LANES)


def _unpack(buf, shapes):
    flat = buf.reshape(-1)
    out, off = [], 0
    for shp in shapes:
        size = 1
        for s in shp:
            size *= s
        out.append(flat[off:off + size].reshape(shp))
        off += size
    return out


def _block_diag_pairs(w):
    w = w.reshape(N_CBLK, 2, LRU_BLOCK_DIM, LRU_BLOCK_DIM)
    z = jnp.zeros_like(w[:, 0])
    top = jnp.concatenate([w[:, 0], z], axis=2)
    bot = jnp.concatenate([z, w[:, 1]], axis=2)
    return jnp.concatenate([top, bot], axis=1)


def _diag_pairs(m):
    h = LRU_BLOCK_DIM
    return jnp.stack([m[:, :h, :h], m[:, h:, h:]], axis=1).reshape(2 * N_CBLK, h, h)


SMALL = ("mlp_norm", "lru_conv_b", "lru_w_r", "lru_b_r", "lru_w_i", "lru_b_i",
         "lru_lambda", "fox_b_f", "fox_q_gain", "fox_k_gain")
WEIGHTS = ("mix_norm", "mlp_norm", "mlp_w1", "mlp_w2", "lru_w_in", "lru_conv_w", "lru_conv_b",
           "lru_w_r", "lru_b_r", "lru_w_i", "lru_b_i", "lru_lambda", "lru_w_out", "fox_w_in",
           "fox_b_f", "fox_q_gain", "fox_k_gain", "fox_w_out")


def kernel(x, mix_norm, mlp_norm, mlp_w1, mlp_w2, lru_w_in, lru_conv_w, lru_conv_b, lru_w_r, lru_b_r, lru_w_i, lru_b_i, lru_lambda, lru_w_out, fox_w_in, fox_b_f, fox_q_gain, fox_k_gain, fox_w_out, loss_target, m_mix_norm, m_mlp_norm, m_mlp_w1, m_mlp_w2, m_lru_w_in, m_lru_conv_w, m_lru_conv_b, m_lru_w_r, m_lru_b_r, m_lru_w_i, m_lru_b_i, m_lru_lambda, m_lru_w_out, m_fox_w_in, m_fox_b_f, m_fox_q_gain, m_fox_k_gain, m_fox_w_out, v_mix_norm, v_mlp_norm, v_mlp_w1, v_mlp_w2, v_lru_w_in, v_lru_conv_w, v_lru_conv_b, v_lru_w_r, v_lru_b_r, v_lru_w_i, v_lru_b_i, v_lru_lambda, v_lru_w_out, v_fox_w_in, v_fox_b_f, v_fox_q_gain, v_fox_k_gain, v_fox_w_out):
    w_in = dict(mix_norm=mix_norm, mlp_norm=mlp_norm, mlp_w1=mlp_w1, mlp_w2=mlp_w2,
                lru_w_in=lru_w_in, lru_conv_w=lru_conv_w, lru_conv_b=lru_conv_b, lru_w_r=lru_w_r,
                lru_b_r=lru_b_r, lru_w_i=lru_w_i, lru_b_i=lru_b_i, lru_lambda=lru_lambda,
                lru_w_out=lru_w_out, fox_w_in=fox_w_in, fox_b_f=fox_b_f, fox_q_gain=fox_q_gain,
                fox_k_gain=fox_k_gain, fox_w_out=fox_w_out)
    m_in = dict(mix_norm=m_mix_norm, mlp_norm=m_mlp_norm, mlp_w1=m_mlp_w1, mlp_w2=m_mlp_w2,
                lru_w_in=m_lru_w_in, lru_conv_w=m_lru_conv_w, lru_conv_b=m_lru_conv_b,
                lru_w_r=m_lru_w_r, lru_b_r=m_lru_b_r, lru_w_i=m_lru_w_i, lru_b_i=m_lru_b_i,
                lru_lambda=m_lru_lambda, lru_w_out=m_lru_w_out, fox_w_in=m_fox_w_in,
                fox_b_f=m_fox_b_f, fox_q_gain=m_fox_q_gain, fox_k_gain=m_fox_k_gain,
                fox_w_out=m_fox_w_out)
    v_in = dict(mix_norm=v_mix_norm, mlp_norm=v_mlp_norm, mlp_w1=v_mlp_w1, mlp_w2=v_mlp_w2,
                lru_w_in=v_lru_w_in, lru_conv_w=v_lru_conv_w, lru_conv_b=v_lru_conv_b,
                lru_w_r=v_lru_w_r, lru_b_r=v_lru_b_r, lru_w_i=v_lru_w_i, lru_b_i=v_lru_b_i,
                lru_lambda=v_lru_lambda, lru_w_out=v_lru_w_out, fox_w_in=v_fox_w_in,
                fox_b_f=v_fox_b_f, fox_q_gain=v_fox_q_gain, fox_k_gain=v_fox_k_gain,
                fox_w_out=v_fox_w_out)
    D = D_MODEL
    S = x.shape[1]
    x0, target = x[0], loss_target[0]
    me = 4 * lax.axis_index("x") + 2 * lax.axis_index("y") + lax.axis_index("c")

    def bf16(a):
        return a.astype(BF16)

    (lru_in_g,) = _exchange([bf16(lru_w_in[0])], [True], "gather_lru_in")
    gather_lru, tok = _exchange_start([bf16(lru_w_out[0]), lru_conv_w[0]], [True] * 2, lru_in_g,
                                      "gather_lru_start")
    gather_mlp0, tok = _exchange_start([bf16(mlp_w1[0]), bf16(mlp_w2[0])], [True] * 2, tok,
                                       "gather_mlp0_start")
    gather_fox, tok = _exchange_start([bf16(fox_w_in[0]), bf16(fox_w_out[0])], [True] * 2, tok,
                                      "gather_fox_start")
    gather_mlp1, tok = _exchange_start([bf16(mlp_w1[1]), bf16(mlp_w2[1])], [True] * 2, tok,
                                       "gather_mlp1_start")
    wr =_block_diag_pairs(lru_w_r[0]).astype(BF16)
    wi = _block_diag_pairs(lru_w_i[0]).astype(BF16)
    b_r, b_i = lru_b_r.reshape(1, D), lru_b_i.reshape(1, D)
    q_gain, k_gain = jnp.tile(fox_q_gain, (1, 2)), jnp.tile(fox_k_gain, (1, 2))
    b_f = jnp.pad(fox_b_f, ((0, 0), (0, LANES - N_HEADS)))
    g_mix0, g_mix1 = mix_norm[0:1] + tok[0, 0], mix_norm[1:2]
    g_mlp0, g_mlp1 = mlp_norm[0:1], mlp_norm[1:2]

    (u0,), h0 = _norm_matmul(x0, g_mix0, [lru_in_g], "lru_in_proj")
    lru_out_g, conv_g = _exchange_wait(gather_lru, [True] * 2, u0, "gather_lru_wait")
    lru_out_w = lru_out_g.reshape(D, D)
    conv_w = conv_g.transpose(1, 0, 2).reshape(CONV_WIDTH, D)
    y_lru, hs =_lru_fwd(u0, conv_w, lru_conv_b, wr, b_r, wi, b_i, lru_lambda, "lru_core")
    x1 = _matmul_res(y_lru, lru_out_w, x0, "lru_out_proj")
    w1g0, w2g0 = _exchange_wait(gather_mlp0, [True] * 2, x1, "gather_mlp0_wait")
    x2, h1, r1 = _mlp_fwd(x1, g_mlp0, w1g0, w2g0, "mlp0")
    fox_in_g, fox_out_g = _exchange_wait(gather_fox, [True] * 2, x2, "gather_fox_wait")
    fox_out_w = fox_out_g.reshape(D, D)
    fox_full = fox_in_g.transpose(1, 0, 2).reshape(D, 3 * D + N_HEADS)
    wqkv = fox_full[:, :3 * D].reshape(D, 3, D).transpose(1, 0, 2)
    wf = jnp.pad(fox_full[:, 3 * D:], ((0, 0), (0, LANES - N_HEADS)))[None]
    (u_qkv, f), h2 = _norm_matmul(x2, g_mix1, [wqkv, wf], "fox_in_proj")
    qn, kn, vb = _qk_prep(u_qkv, q_gain, k_gain, "fox_qk_norm")
    c_col = _forget_fwd(f, b_f, "fox_forget")
    c_row = c_col[:, :N_HEADS].T.reshape(N_CBLK, 2, S)
    o, lse = _attn_fwd(qn, kn, vb, c_row, "fox_attn")
    x3 = _matmul_res(o, fox_out_w, x2, "fox_out_proj")
    w1g1, w2g1 = _exchange_wait(gather_mlp1, [True] * 2, x3, "gather_mlp1_wait")
    x4, h3, r3 = _mlp_fwd(x3, g_mlp1, w1g1, w2g1, "mlp1")
    loss_local, dx4 = _loss_head(x4, target, "loss_head")

    dx3, dg_mlp1, da3 = _mlp_bwd(dx4, x3, g_mlp1, r3, w1g1, w2g1, "mlp1_bwd")
    dw1_1 = _matmul_tn(h3, da3, "mlp1_dw1", cols=2, col_blocks=N_DEV)
    dw2_1 = _matmul_tn(r3, dx4, "mlp1_dw2", rows=2, a_square=True).reshape(N_DEV, -1, D)
    grads_mlp1, tok = _exchange_start([dw1_1, dw2_1], [False] * 2, tok, "grads_mlp1_start")
    do = _matmul_nt(dx3, fox_out_w, "fox_out_bwd", BF16, tok)
    d_fox_out = _matmul_tn(o, dx3, "fox_out_dw").reshape(N_DEV, -1, D)
    dqn, dkn, dv, dc_row, rho = _attn_bwd(qn, kn, vb, do, o, lse, c_row, "fox_attn_bwd")
    duq, duk, dq_gain, dk_gain = _qk_bwd(u_qkv, dqn, dkn, q_gain, k_gain, "fox_qk_norm_bwd")
    dc_k = jnp.pad(dc_row.reshape(N_HEADS, S).T, ((0, 0), (0, LANES - N_HEADS)))
    df, db_f = _forget_bwd(dc_k, rho, f, b_f, "fox_forget_bwd")
    dx2, dg_mix1 = _proj_bwd([[duq, duk, dv], [df]], [wqkv, wf], x2, g_mix1, dx3, "fox_in_bwd")
    d_fox_in = jnp.concatenate(
        [_matmul_tn(h2, duq, "fox_in_dwq"), _matmul_tn(h2, duk, "fox_in_dwk"),
         _matmul_tn(h2, dv, "fox_in_dwv"), _matmul_tn(h2, df, "fox_in_dwf")[:, :N_HEADS]], axis=1)
    d_fox_in = d_fox_in.reshape(D, N_DEV, -1).transpose(1, 0, 2)
    grads_fox, tok = _exchange_start([d_fox_in, d_fox_out], [False] * 2, tok, "grads_fox_start")
    dx1, dg_mlp0, da1 = _mlp_bwd(dx2, x1, g_mlp0 + tok[0, 0], r1, w1g0, w2g0, "mlp0_bwd")
    dw1_0 = _matmul_tn(h1, da1, "mlp0_dw1", cols=2, col_blocks=N_DEV)
    dw2_0 = _matmul_tn(r1, dx2, "mlp0_dw2", rows=2, a_square=True).reshape(N_DEV, -1, D)
    grads_mlp0, tok = _exchange_start([dw1_0, dw2_0], [False] * 2, tok, "grads_mlp0_start")
    dy_lru = _matmul_nt(dx1, lru_out_w, "lru_out_bwd", F32, tok)
    d_lru_out = _matmul_tn(y_lru, dx1, "lru_out_dw").reshape(N_DEV, -1, D)
    dgp, dxb, d_conv_w, d_conv_b, d_b_r, d_b_i, d_lam, d_wr, d_wi = _lru_bwd(
        dy_lru, u0, hs, conv_w, lru_conv_b, wr, b_r, wi, b_i, lru_lambda, "lru_core_bwd")

    small_grads = dict(
        mlp_norm=jnp.concatenate([dg_mlp0, dg_mlp1], axis=0),
        lru_conv_b=d_conv_b, lru_w_r=_diag_pairs(d_wr), lru_b_r=d_b_r, lru_w_i=_diag_pairs(d_wi),
        lru_b_i=d_b_i, lru_lambda=d_lam, fox_b_f=db_f[:, :N_HEADS],
        fox_q_gain=dq_gain[:, :HEAD_DIM], fox_k_gain=dk_gain[:, :HEAD_DIM])
    small_partial = _pack([dg_mix1] + [small_grads[n] for n in SMALL] + [d_conv_w])
    grads_lru_out, tok = _exchange_start([d_lru_out, small_partial], [False, True], tok,
                                         "grads_lru_out_start")
    dx0, dg_mix0 = _proj_bwd([[dgp, dxb]], [lru_in_g], x0, mix_norm[0:1] + tok[0, 0], dx1,
                             "lru_in_bwd")
    d_lru_in = jnp.concatenate([_matmul_tn(h0, dgp, "lru_in_dw_gate", col_blocks=4),
                                _matmul_tn(h0, dxb, "lru_in_dw_x", col_blocks=4)], axis=0)
    grads_lru_in, tok = _exchange_start([d_lru_in, dg_mix0], [False, True], tok,
                                        "grads_lru_in_start")

    grads, deltas, new_m, new_v = {}, {}, {}, {}

    def update(name, parts, sel=None):
        w, m, v = w_in[name], m_in[name], v_in[name]
        if sel is not None:
            w, m, v = w[sel], m[sel], v[sel]
        shape = w.shape
        two_d = (-1, shape[-1])
        res = _reduce_adamw(parts.reshape((N_DEV,) + w.reshape(two_d).shape), w.reshape(two_d),
                            m.reshape(two_d), v.reshape(two_d),
                            "adamw_" + name + ("" if sel is None else "_%d" % sel))
        return [r.reshape(shape) for r in res]

    def store(name, res):
        grads[name], deltas[name], new_m[name], new_v[name] = res

    p_w1_1, p_w2_1 = _exchange_wait(grads_mlp1, [False] * 2, tok, "grads_mlp1_wait")
    up_w1_1, up_w2_1 = update("mlp_w1", p_w1_1, 1), update("mlp_w2", p_w2_1, 1)
    p_fox_in, p_fox_out = _exchange_wait(grads_fox, [False] * 2, up_w2_1[0], "grads_fox_wait")
    store("fox_w_in", update("fox_w_in", p_fox_in))
    store("fox_w_out", update("fox_w_out", p_fox_out))
    p_w1_0, p_w2_0 = _exchange_wait(grads_mlp0, [False] * 2, grads["fox_w_out"], "grads_mlp0_wait")
    up_w1_0, up_w2_0 = update("mlp_w1", p_w1_0, 0), update("mlp_w2", p_w2_0, 0)
    store("mlp_w1", [jnp.stack(p) for p in zip(up_w1_0, up_w1_1)])
    store("mlp_w2", [jnp.stack(p) for p in zip(up_w2_0, up_w2_1)])
    p_lru_out, p_small = _exchange_wait(grads_lru_out, [False, True], up_w2_0[0],
                                        "grads_lru_out_wait")
    store("lru_w_out", update("lru_w_out", p_lru_out))
    p_lru_in, p_mix0 = _exchange_wait(grads_lru_in, [False, True], grads["lru_w_out"],
                                      "grads_lru_in_wait")
    store("lru_w_in", update("lru_w_in", p_lru_in))

    mix0 = _reduce_adamw(p_mix0, mix_norm[0:1], m_mix_norm[0:1], v_mix_norm[0:1], "adamw_mix0")
    packed = lambda src, first: _pack([first] + [src[n] for n in SMALL] + [jnp.zeros((CONV_WIDTH, D))])
    small_shapes = [(1, D)] + [w_in[n].shape for n in SMALL]
    n_small = sum(math.prod(s) for s in small_shapes)
    res_small = _reduce_adamw(p_small, packed(w_in, mix_norm[1:2]), packed(m_in, m_mix_norm[1:2]),
                              packed(v_in, v_mix_norm[1:2]), "adamw_small")
    for name, *vals in zip(("mix1",) + SMALL, *[_unpack(r, small_shapes) for r in res_small]):
        if name == "mix1":
            vals = [jnp.concatenate([r0, r1], axis=0) for r0, r1 in zip(mix0, vals)]
            name = "mix_norm"
        store(name, vals)
    conv_parts = p_small.reshape(N_DEV, -1)[:, n_small:n_small + CONV_WIDTH * D]
    conv_parts = conv_parts.reshape(N_DEV, CONV_WIDTH, N_DEV, LANES)
    conv_parts = lax.dynamic_index_in_dim(conv_parts, me, axis=2, keepdims=False)
    store("lru_conv_w", update("lru_conv_w", conv_parts))

    loss = lax.psum(loss_local[0, 0], ("x", "y", "c"))
    return (loss, dx0[None], *[grads[n] for n in WEIGHTS], *[deltas[n] for n in WEIGHTS],
            *[new_m[n] for n in WEIGHTS], *[new_v[n] for n in WEIGHTS])
```

```python
import math

import jax
import jax.numpy as jnp
from jax import lax
from jax.experimental import pallas as pl
from jax.experimental.pallas import tpu as pltpu

F32 = jnp.float32
BF16 = jnp.bfloat16

N_DEV = 8
D_MODEL = 1024
D_FF = 4096
N_HEADS = 16
HEAD_DIM = 64
LRU_BLOCK_DIM = 64
CONV_WIDTH = 4
LRU_C = 8.0
EPS = 1e-6
NEG_INF = -1e30
ATTN_SCALE = HEAD_DIM ** -0.5
LANES = 128
N_CBLK = D_MODEL // LANES
VMEM_LIMIT = 52 * 2 ** 20

ADAM_LR = 0.001
ADAM_B1 = 0.9
ADAM_B2 = 0.999
ADAM_EPS = 1e-08
ADAM_WD = 0.01
ADAM_STEP = 10

_NT = (((1,), (1,)), ((), ()))
_TN = (((0,), (0,)), ((), ()))


def _params(*sem):
    return pltpu.CompilerParams(dimension_semantics=sem, vmem_limit_bytes=VMEM_LIMIT)


def _resident(shape):
    zeros = (0,) * len(shape)
    return pl.BlockSpec(shape, lambda *_: zeros, pipeline_mode=pl.Buffered(1))


def _dot(a, b):
    return jnp.dot(a, b, preferred_element_type=F32)


def _dot_nt(a, b):
    return lax.dot_general(a, b, _NT, preferred_element_type=F32)


def _dot_tn(a, b):
    return lax.dot_general(a, b, _TN, preferred_element_type=F32)


def _sigmoid(x):
    return 1.0 / (1.0 + jnp.exp(-x))


def _log_sigmoid(x):
    return -(jnp.maximum(-x, 0.0) + jnp.log1p(jnp.exp(-jnp.abs(x))))


def _expm1(x):
    poly = x * (1.0 + x * (0.5 + x * (1.0 / 6.0 + x * (1.0 / 24.0 + x * (1.0 / 120.0)))))
    return jnp.where(jnp.abs(x) < 0.1, poly, jnp.exp(x) - 1.0)


_GELU_K = 0.7978845608028654


def _gelu(x):
    return 0.5 * x * (1.0 + jnp.tanh(_GELU_K * (x + 0.044715 * (x * x * x))))


def _gelu_grad(x):
    t = jnp.tanh(_GELU_K * (x + 0.044715 * (x * x * x)))
    return 0.5 * (1.0 + t) + 0.5 * x * (1.0 - t * t) * (_GELU_K * (1.0 + 3 * 0.044715 * x * x))


def _rms_scale(x):
    return lax.rsqrt(jnp.mean(x * x, axis=-1, keepdims=True) + EPS)


def _norm_bwd(dh, x, g):
    rs = _rms_scale(x)
    xhat = x * rs
    dxhat = dh * g
    dx = rs * (dxhat - xhat * jnp.mean(dxhat * xhat, axis=-1, keepdims=True))
    return dx, jnp.sum(dh * xhat, axis=0, keepdims=True)


def _token_tile(S, want):
    tm = min(S, want)
    assert S % tm == 0
    return tm


def _norm_matmul(x, g, ws, name, tm=256):
    S, D = x.shape
    tm = _token_tile(S, tm)
    n = len(ws)

    def body(x_ref, g_ref, *refs):
        w_refs, o_refs, h_ref = refs[:n], refs[n:2 * n], refs[2 * n]
        xv = x_ref[...]
        h = (xv * _rms_scale(xv) * g_ref[...]).astype(BF16)
        h_ref[...] = h
        for w_ref, o_ref in zip(w_refs, o_refs):
            nb, _, nw = w_ref.shape
            for d in range(nb):
                o_ref[:, d * nw:(d + 1) * nw] = _dot(h, w_ref[d])

    widths = [w.shape[0] * w.shape[2] for w in ws]
    outs = pl.pallas_call(
        body, name=name, grid=(S // tm,),
        in_specs=[pl.BlockSpec((tm, D), lambda i: (i, 0)), _resident((1, D))]
        + [_resident(w.shape) for w in ws],
        out_specs=[pl.BlockSpec((tm, n_), lambda i: (i, 0)) for n_ in widths]
        + [pl.BlockSpec((tm, D), lambda i: (i, 0))],
        out_shape=[jax.ShapeDtypeStruct((S, n_), F32) for n_ in widths]
        + [jax.ShapeDtypeStruct((S, D), BF16)],
        compiler_params=_params("parallel"),
    )(x, g, *ws)
    return outs[:n], outs[n]


def _matmul_res(a, w, res, name, after, tm=512):
    S, K = a.shape
    N = w.shape[1]
    tm = _token_tile(S, tm)

    def body(a_ref, w_ref, r_ref, after_ref, o_ref):
        o_ref[...] = r_ref[...] + _dot(a_ref[...], w_ref[...])

    return pl.pallas_call(
        body, name=name, grid=(S // tm,),
        in_specs=[pl.BlockSpec((tm, K), lambda i: (i, 0)), _resident((K, N)),
                  pl.BlockSpec((tm, N), lambda i: (i, 0)), pl.BlockSpec(memory_space=pl.ANY)],
        out_specs=pl.BlockSpec((tm, N), lambda i: (i, 0)),
        out_shape=jax.ShapeDtypeStruct((S, N), F32),
        compiler_params=_params("parallel"),
    )(a, w, res, after)


def _matmul_nt(a, w, name, out_dtype, after, tm=512):
    S, N = a.shape
    K = w.shape[0]
    tm = _token_tile(S, tm)

    def body(a_ref, w_ref, after_ref, o_ref):
        o_ref[...] = _dot_nt(a_ref[...].astype(BF16), w_ref[...]).astype(out_dtype)

    return pl.pallas_call(
        body, name=name, grid=(S // tm,),
        in_specs=[pl.BlockSpec((tm, N), lambda i: (i, 0)), _resident((K, N)),
                  pl.BlockSpec(memory_space=pl.ANY)],
        out_specs=pl.BlockSpec((tm, K), lambda i: (i, 0)),
        out_shape=jax.ShapeDtypeStruct((S, K), out_dtype),
        compiler_params=_params("parallel"),
    )(a, w, after)


def _proj_bwd(a_lists, w_list, x, g, res, name, tm=256):
    S, D = x.shape
    tm = _token_tile(S, tm)
    a_list = [a for group in a_lists for a in group]
    n, n_w = len(a_list), len(w_list)

    def body(*refs):
        a_refs, w_refs = list(refs[:n]), refs[n:n + n_w]
        x_ref, g_ref, r_ref, dx_ref, dg_ref = refs[n + n_w:]
        dh = jnp.zeros((tm, D), F32)
        for group, w_ref in zip(a_lists, w_refs):
            nw = w_ref.shape[2]
            d = 0
            for _ in group:
                a_ref = a_refs.pop(0)
                for j in range(a_ref.shape[1] // nw):
                    dh = dh + _dot_nt(a_ref[:, j * nw:(j + 1) * nw].astype(BF16), w_ref[d])
                    d += 1
        dx, dg = _norm_bwd(dh, x_ref[...], g_ref[...])
        dx_ref[...] = r_ref[...] + dx

        @pl.when(pl.program_id(0) == 0)
        def _():
            dg_ref[...] = jnp.zeros_like(dg_ref)
        dg_ref[...] += dg

    tok = lambda width: pl.BlockSpec((tm, width), lambda i: (i, 0))
    return pl.pallas_call(
        body, name=name, grid=(S // tm,),
        in_specs=[tok(a.shape[1]) for a in a_list] + [_resident(w.shape) for w in w_list]
        + [tok(D), _resident((1, D)), tok(D)],
        out_specs=[tok(D), pl.BlockSpec((1, D), lambda i: (0, 0))],
        out_shape=[jax.ShapeDtypeStruct((S, D), F32), jax.ShapeDtypeStruct((1, D), F32)],
        compiler_params=_params("arbitrary"),
    )(*a_list, *w_list, x, g, res)


def _matmul_tn(a, b, name, rows=1, cols=1, col_blocks=None, a_square=False, tm=1024):
    S, K = a.shape
    N = b.shape[1]
    tm = _token_tile(S, tm)
    n_tok = S // tm
    kr, nc = K // rows, N // cols

    def body(a_ref, b_ref, o_ref, acc_ref):
        av = a_ref[...]
        if a_square:
            av = av.astype(F32)
            av = av * av
        part = _dot_tn(av.astype(BF16), b_ref[...].astype(BF16))
        step = pl.program_id(2)

        @pl.when(step == 0)
        def _():
            acc_ref[...] = part

        @pl.when(step > 0)
        def _():
            acc_ref[...] += part

        @pl.when(step == n_tok - 1)
        def _():
            if col_blocks is None:
                o_ref[...] = acc_ref[...].astype(BF16)
            else:
                nw = N // col_blocks
                for d in range(col_blocks // cols):
                    o_ref[d] = acc_ref[:, d * nw:(d + 1) * nw].astype(BF16)

    if col_blocks is None:
        out_spec = pl.BlockSpec((kr, nc), lambda r, c, i: (r, c))
        out_shape = jax.ShapeDtypeStruct((K, N), BF16)
    else:
        assert rows == 1 and col_blocks % cols == 0
        per = col_blocks // cols
        out_spec = pl.BlockSpec((per, K, N // col_blocks), lambda r, c, i: (c, 0, 0))
        out_shape = jax.ShapeDtypeStruct((col_blocks, K, N // col_blocks), BF16)
    return pl.pallas_call(
        body, name=name, grid=(rows, cols, n_tok),
        in_specs=[pl.BlockSpec((tm, kr), lambda r, c, i: (i, r)),
                  pl.BlockSpec((tm, nc), lambda r, c, i: (i, c))],
        out_specs=out_spec, out_shape=out_shape,
        scratch_shapes=[pltpu.VMEM((kr, nc), F32)],
        compiler_params=_params("parallel", "parallel", "arbitrary"),
    )(a, b)


def _mlp_fwd(x, g, w1, w2, name, tm=256):
    S, D = x.shape
    nb, _, fb = w1.shape
    tm = _token_tile(S, tm)

    def body(x_ref, g_ref, w1_ref, w2_ref, o_ref, h_ref, r_ref):
        xv = x_ref[...]
        h = (xv * _rms_scale(xv) * g_ref[...]).astype(BF16)
        h_ref[...] = h
        acc = xv
        for d in range(nb):
            r = jnp.maximum(_dot(h, w1_ref[d]), 0.0)
            r_ref[:, d * fb:(d + 1) * fb] = r.astype(BF16)
            acc = acc + _dot((r * r).astype(BF16), w2_ref[d])
        o_ref[...] = acc

    tok = lambda width: pl.BlockSpec((tm, width), lambda i: (i, 0))
    return pl.pallas_call(
        body, name=name, grid=(S // tm,),
        in_specs=[tok(D), _resident((1, D)), _resident(w1.shape), _resident(w2.shape)],
        out_specs=[tok(D), tok(D), tok(nb * fb)],
        out_shape=[jax.ShapeDtypeStruct((S, D), F32), jax.ShapeDtypeStruct((S, D), BF16),
                   jax.ShapeDtypeStruct((S, nb * fb), BF16)],
        compiler_params=_params("parallel"),
    )(x, g, w1, w2)


def _mlp_bwd(dout, x, g, r, w1, w2, name, tm=256):
    S, D = x.shape
    nb, _, fb = w1.shape
    tm = _token_tile(S, tm)

    def body(do_ref, x_ref, g_ref, r_ref, w1_ref, w2_ref, dx_ref, dg_ref, da_ref):
        dov = do_ref[...]
        dob = dov.astype(BF16)
        dh = jnp.zeros((tm, D), F32)
        for d in range(nb):
            dz = _dot_nt(dob, w2_ref[d])
            da = (dz * (2.0 * r_ref[:, d * fb:(d + 1) * fb].astype(F32))).astype(BF16)
            da_ref[:, d * fb:(d + 1) * fb] = da
            dh = dh + _dot_nt(da, w1_ref[d])
        dx, dg = _norm_bwd(dh, x_ref[...], g_ref[...])
        dx_ref[...] = dov + dx

        @pl.when(pl.program_id(0) == 0)
        def _():
            dg_ref[...] = jnp.zeros_like(dg_ref)
        dg_ref[...] += dg

    tok = lambda width: pl.BlockSpec((tm, width), lambda i: (i, 0))
    return pl.pallas_call(
        body, name=name, grid=(S // tm,),
        in_specs=[tok(D), tok(D), _resident((1, D)), tok(nb * fb), _resident(w1.shape),
                  _resident(w2.shape)],
        out_specs=[tok(D), pl.BlockSpec((1, D), lambda i: (0, 0)), tok(nb * fb)],
        out_shape=[jax.ShapeDtypeStruct((S, D), F32), jax.ShapeDtypeStruct((1, D), F32),
                   jax.ShapeDtypeStruct((S, nb * fb), BF16)],
        compiler_params=_params("arbitrary"),
    )(dout, x, g, r, w1, w2)


def _scan_chunk(a, b, row, T, reverse):
    s = 1
    while s < T:
        if reverse:
            keep, shift = row < T - s, T - s
        else:
            keep, shift = row >= s, s
        a_sh = jnp.where(keep, pltpu.roll(a, shift, 0), 1.0)
        b_sh = jnp.where(keep, pltpu.roll(b, shift, 0), 0.0)
        b = a * b_sh + b
        a = a * a_sh
        s *= 2
    return a, b


def _row_of(x, row, r):
    return jnp.sum(jnp.where(row == r, x, 0.0), axis=0, keepdims=True)


def _shift_down(x, prev, row, k):
    if k == 0:
        return x
    return jnp.where(row < k, pltpu.roll(prev, k, 0), pltpu.roll(x, k, 0))


def _shift_up(x, nxt, row, k, T):
    if k == 0:
        return x
    return jnp.where(row < T - k, pltpu.roll(x, T - k, 0), pltpu.roll(nxt, T - k, 0))


def _lru_gates(xb, prev_xb, row, cw_ref, cb, wr, br, wi, bi, ls):
    xc = cb + cw_ref[pl.ds(0, 1), :] * _shift_down(xb, prev_xb, row, 3)
    for k in (2, 1, 0):
        xc = xc + cw_ref[pl.ds(3 - k, 1), :] * _shift_down(xb, prev_xb, row, k)
    xcb = xc.astype(BF16)
    r = _sigmoid(_dot(xcb, wr) + br)
    i = _sigmoid(_dot(xcb, wi) + bi)
    la = (LRU_C * r) * ls
    a = jnp.exp(la)
    m = jnp.sqrt(-_expm1(2.0 * la))
    return xc, xcb, r, i, a, m


def _lru_specs(S):
    col = lambda off: pl.BlockSpec((S, LANES), lambda j: (0, j + off))
    vec = pl.BlockSpec((1, LANES), lambda j: (0, j))
    mat = pl.BlockSpec((None, LANES, LANES), lambda j: (j, 0, 0))
    cwm = pl.BlockSpec((CONV_WIDTH, LANES), lambda j: (0, j))
    return col, vec, mat, cwm


def _lru_fwd(u, conv_w, conv_b, wr, br, wi, bi, lam, name):
    S = u.shape[0]
    T = _token_tile(S, 256)
    col, vec, mat, cwm = _lru_specs(S)

    def body(gp_ref, xb_ref, cw_ref, cb_ref, wr_ref, br_ref, wi_ref, bi_ref, lam_ref,
             y_ref, hs_ref):
        row = lax.broadcasted_iota(jnp.int32, (T, LANES), 0)
        ls = _log_sigmoid(lam_ref[...])
        cb, br, bi = cb_ref[...], br_ref[...], bi_ref[...]
        wr, wi = wr_ref[...], wi_ref[...]

        def chunk(ci, carry):
            prev_xb, hc = carry
            rows = pl.ds(pl.multiple_of(ci * T, T), T)
            xb = xb_ref[rows, :]
            xc, _, _, i, a, m = _lru_gates(xb, prev_xb, row, cw_ref, cb, wr, br, wi, bi, ls)
            ca, cbv = _scan_chunk(a, m * (i * xc), row, T, reverse=False)
            h = ca * hc + cbv
            hs_ref[rows, :] = h
            y_ref[rows, :] = (_gelu(gp_ref[rows, :]) * h).astype(BF16)
            return xb, _row_of(h, row, T - 1)

        lax.fori_loop(0, S // T, chunk,
                      (jnp.zeros((T, LANES), F32), jnp.zeros((1, LANES), F32)))

    return pl.pallas_call(
        body, name=name, grid=(N_CBLK,),
        in_specs=[col(0), col(N_CBLK), cwm, vec, mat, vec, mat, vec, vec],
        out_specs=[col(0), col(0)],
        out_shape=[jax.ShapeDtypeStruct((S, D_MODEL), BF16), jax.ShapeDtypeStruct((S, D_MODEL), F32)],
        compiler_params=_params("parallel"),
    )(u, u, conv_w, conv_b, wr, br, wi, bi, lam)


def _lru_bwd(dy, u, hs, conv_w, conv_b, wr, br, wi, bi, lam, name):
    S = u.shape[0]
    T = _token_tile(S, 256)
    n_chunk = S // T
    col, vec, mat, cwm = _lru_specs(S)

    def body(dy_ref, gp_ref, xb_ref, hs_ref, cw_ref, cb_ref, wr_ref, br_ref, wi_ref, bi_ref,
             lam_ref, dgp_ref, dxb_ref, dcw_ref, dcb_ref, dbr_ref, dbi_ref, dlam_ref, dwr_ref,
             dwi_ref):
        row = lax.broadcasted_iota(jnp.int32, (T, LANES), 0)
        lam = lam_ref[...]
        ls = _log_sigmoid(lam)
        cb, br, bi = cb_ref[...], br_ref[...], bi_ref[...]
        wr, wi = wr_ref[...], wi_ref[...]
        for ref in (dcw_ref, dcb_ref, dbr_ref, dbi_ref, dlam_ref, dwr_ref, dwi_ref):
            ref[...] = jnp.zeros_like(ref)

        def chunk(it, carry):
            g_next, dxc_next = carry
            ci = n_chunk - 1 - it
            rows = pl.ds(pl.multiple_of(ci * T, T), T)
            before = pl.ds(pl.multiple_of(jnp.maximum(ci - 1, 0) * T, T), T)
            first = ci == 0
            xb = xb_ref[rows, :]
            prev_xb = jnp.where(first, 0.0, xb_ref[before, :])
            xc, xcb, r, i, a, m = _lru_gates(xb, prev_xb, row, cw_ref, cb, wr, br, wi, bi, ls)
            h = hs_ref[rows, :]
            h_prev = _shift_down(h, jnp.where(first, 0.0, hs_ref[before, :]), row, 1)
            gp = gp_ref[rows, :]
            dyv = dy_ref[rows, :]
            dgp_ref[rows, :] = (dyv * h * _gelu_grad(gp)).astype(BF16)
            dh = dyv * _gelu(gp)
            ca, cbv = _scan_chunk(a, a * dh, row, T, reverse=True)
            gp_acc = ca * g_next + cbv
            g = dh + jnp.where(row < T - 1, pltpu.roll(gp_acc, T - 1, 0), g_next)
            da = g * h_prev - (g * (i * xc)) * a / m
            dla = da * a
            dlam_ref[...] += jnp.sum(dla * (LRU_C * r), axis=0, keepdims=True)
            dpr = (dla * (LRU_C * ls)) * r * (1.0 - r)
            dpi = (g * m * xc) * i * (1.0 - i)
            dbr_ref[...] += jnp.sum(dpr, axis=0, keepdims=True)
            dbi_ref[...] += jnp.sum(dpi, axis=0, keepdims=True)
            dprb, dpib = dpr.astype(BF16), dpi.astype(BF16)
            dwr_ref[...] += _dot_tn(xcb, dprb)
            dwi_ref[...] += _dot_tn(xcb, dpib)
            dxc = g * m * i + _dot_nt(dprb, wr) + _dot_nt(dpib, wi)
            dcb_ref[...] += jnp.sum(dxc, axis=0, keepdims=True)
            dxb = jnp.zeros((T, LANES), F32)
            for k in range(CONV_WIDTH):
                tap = pl.ds(CONV_WIDTH - 1 - k, 1)
                dcw_ref[tap, :] += jnp.sum(dxc * _shift_down(xb, prev_xb, row, k), axis=0,
                                           keepdims=True)
                dxb = dxb + cw_ref[tap, :] * _shift_up(dxc, dxc_next, row, k, T)
            dxb_ref[rows, :] = dxb.astype(BF16)
            return _row_of(gp_acc, row, 0), dxc

        lax.fori_loop(0, n_chunk, chunk,
                      (jnp.zeros((1, LANES), F32), jnp.zeros((T, LANES), F32)))
        dlam_ref[...] = dlam_ref[...] * _sigmoid(-lam)

    vec_out = jax.ShapeDtypeStruct((1, D_MODEL), F32)
    mat_out = jax.ShapeDtypeStruct((N_CBLK, LANES, LANES), F32)
    return pl.pallas_call(
        body, name=name, grid=(N_CBLK,),
        in_specs=[col(0), col(0), col(N_CBLK), col(0), cwm, vec, mat, vec, mat, vec, vec],
        out_specs=[col(0), col(0), cwm, vec, vec, vec, vec, mat, mat],
        out_shape=[jax.ShapeDtypeStruct((S, D_MODEL), BF16), jax.ShapeDtypeStruct((S, D_MODEL), BF16),
                   jax.ShapeDtypeStruct((CONV_WIDTH, D_MODEL), F32),
                   vec_out, vec_out, vec_out, vec_out, mat_out, mat_out],
        compiler_params=_params("parallel"),
    )(dy, u, u, hs, conv_w, conv_b, wr, br, wi, bi, lam)


def _head_group_matrix(value):
    r = lax.broadcasted_iota(jnp.int32, (LANES, LANES), 0) // HEAD_DIM
    c = lax.broadcasted_iota(jnp.int32, (LANES, LANES), 1) // HEAD_DIM
    return jnp.where(r == c, value, 0.0).astype(BF16)


def _group_dot(x, p):
    hi = x.astype(BF16)
    lo = (x - hi.astype(F32)).astype(BF16)
    return _dot(hi, p) + _dot(lo, p)


def _head_mean(x, p):
    return _group_dot(x, p)


def _qk_prep(u, q_gain, k_gain, name, tm=256):
    S = u.shape[0]
    tm = _token_tile(S, tm)

    def body(q_ref, k_ref, v_ref, qg_ref, kg_ref, qn_ref, kn_ref, vb_ref):
        p = _head_group_matrix(1.0 / HEAD_DIM)
        for j in range(N_CBLK):
            cl = slice(j * LANES, (j + 1) * LANES)
            for x_ref, g_ref, o_ref, scale in ((q_ref, qg_ref, qn_ref, ATTN_SCALE),
                                               (k_ref, kg_ref, kn_ref, 1.0)):
                xv = x_ref[:, cl]
                rs = lax.rsqrt(_head_mean(xv * xv, p) + EPS)
                o_ref[:, cl] = (xv * rs * g_ref[...]).astype(BF16) * scale
        vb_ref[...] = v_ref[...].astype(BF16)

    blk = lambda off: pl.BlockSpec((tm, D_MODEL), lambda i: (i, off))
    out = jax.ShapeDtypeStruct((S, D_MODEL), BF16)
    return pl.pallas_call(
        body, name=name, grid=(S // tm,),
        in_specs=[blk(0), blk(1), blk(2), _resident((1, LANES)), _resident((1, LANES))],
        out_specs=[blk(0), blk(0), blk(0)],
        out_shape=[out, out, out],
        compiler_params=_params("parallel"),
    )(u, u, u, q_gain, k_gain)


def _qk_bwd(u, dqn, dkn, q_gain, k_gain, name, tm=256):
    S = u.shape[0]
    tm = _token_tile(S, tm)

    def body(q_ref, k_ref, dqn_ref, dkn_ref, qg_ref, kg_ref, dq_ref, dk_ref, dqg_ref, dkg_ref):
        p = _head_group_matrix(1.0 / HEAD_DIM)
        for x_ref, dn_ref, g_ref, dx_ref, dg_ref, scale in (
                (q_ref, dqn_ref, qg_ref, dq_ref, dqg_ref, ATTN_SCALE),
                (k_ref, dkn_ref, kg_ref, dk_ref, dkg_ref, 1.0)):
            dg = jnp.zeros((1, LANES), F32)
            for j in range(N_CBLK):
                cl = slice(j * LANES, (j + 1) * LANES)
                xv, dn = x_ref[:, cl], dn_ref[:, cl] * scale
                rs = lax.rsqrt(_head_mean(xv * xv, p) + EPS)
                xhat = xv * rs
                dxhat = dn * g_ref[...]
                dx_ref[:, cl] = (rs * (dxhat - xhat * _head_mean(dxhat * xhat, p))).astype(BF16)
                dg = dg + jnp.sum(dn * xhat, axis=0, keepdims=True)

            @pl.when(pl.program_id(0) == 0)
            def _():
                dg_ref[...] = jnp.zeros_like(dg_ref)
            dg_ref[...] += dg

            @pl.when(pl.program_id(0) == S // tm - 1)
            def _():
                dg_ref[...] += pltpu.roll(dg_ref[...], HEAD_DIM, 1)

    blk = lambda off: pl.BlockSpec((tm, D_MODEL), lambda i: (i, off))
    acc = pl.BlockSpec((1, LANES), lambda i: (0, 0))
    out = jax.ShapeDtypeStruct((S, D_MODEL), BF16)
    vec = jax.ShapeDtypeStruct((1, LANES), F32)
    return pl.pallas_call(
        body, name=name, grid=(S // tm,),
        in_specs=[blk(0), blk(1), blk(0), blk(0), _resident((1, LANES)), _resident((1, LANES))],
        out_specs=[blk(0), blk(0), acc, acc],
        out_shape=[out, out, vec, vec],
        compiler_params=_params("arbitrary"),
    )(u, u, dqn, dkn, q_gain, k_gain)


def _forget_fwd(f, b_f, name):
    S = f.shape[0]
    T = _token_tile(S, 256)

    def body(f_ref, b_ref, c_ref):
        row = lax.broadcasted_iota(jnp.int32, (T, LANES), 0)
        ones = jnp.ones((T, LANES), F32)
        bias = b_ref[...]

        def chunk(ci, carry):
            rows = pl.ds(pl.multiple_of(ci * T, T), T)
            _, c = _scan_chunk(ones, _log_sigmoid(f_ref[rows, :] + bias), row, T, reverse=False)
            c = c + carry
            c_ref[rows, :] = c
            return _row_of(c, row, T - 1)

        lax.fori_loop(0, S // T, chunk, jnp.zeros((1, LANES), F32))

    return pl.pallas_call(
        body, name=name,
        in_specs=[pl.BlockSpec(memory_space=pltpu.VMEM)] * 2,
        out_specs=pl.BlockSpec(memory_space=pltpu.VMEM),
        out_shape=jax.ShapeDtypeStruct((S, LANES), F32),
        compiler_params=pltpu.CompilerParams(vmem_limit_bytes=VMEM_LIMIT),
    )(f, b_f)


def _forget_bwd(dc_k, rho, f, b_f, name):
    S = f.shape[0]
    T = _token_tile(S, 256)
    n_chunk = S // T

    def body(dck_ref, rho_ref, f_ref, b_ref, df_ref, db_ref):
        row = lax.broadcasted_iota(jnp.int32, (T, LANES), 0)
        ones = jnp.ones((T, LANES), F32)
        bias = b_ref[...]
        pick = (lax.broadcasted_iota(jnp.int32, (D_MODEL, LANES), 0)
                == HEAD_DIM * lax.broadcasted_iota(jnp.int32, (D_MODEL, LANES), 1))
        pick = jnp.where(pick, 1.0, 0.0).astype(BF16)

        def chunk(it, carry):
            tail, db = carry
            rows = pl.ds(pl.multiple_of((n_chunk - 1 - it) * T, T), T)
            dc = dck_ref[rows, :] + _group_dot(rho_ref[rows, :], pick)
            _, dlf = _scan_chunk(ones, dc, row, T, reverse=True)
            dlf = dlf + tail
            df = dlf * _sigmoid(-(f_ref[rows, :] + bias))
            df_ref[rows, :] = df
            return _row_of(dlf, row, 0), db + jnp.sum(df, axis=0, keepdims=True)

        zero = jnp.zeros((1, LANES), F32)
        _, db = lax.fori_loop(0, n_chunk, chunk, (zero, zero))
        db_ref[...] = db

    return pl.pallas_call(
        body, name=name,
        in_specs=[pl.BlockSpec(memory_space=pltpu.VMEM)] * 4,
        out_specs=[pl.BlockSpec(memory_space=pltpu.VMEM)] * 2,
        out_shape=[jax.ShapeDtypeStruct((S, LANES), F32), jax.ShapeDtypeStruct((1, LANES), F32)],
        compiler_params=pltpu.CompilerParams(vmem_limit_bytes=VMEM_LIMIT),
    )(dc_k, rho, f, b_f)


ATTN_TILE = 512
ATTN_ROWS_FWD = 32
ATTN_ROWS_BWD = 32


def _attn_tiles(S):
    t = _token_tile(S, ATTN_TILE)
    return t, S // t


def _causal(T):
    return (lax.broadcasted_iota(jnp.int32, (T, T), 1)
            <= lax.broadcasted_iota(jnp.int32, (T, T), 0))


def _attn_fwd(qs_, kn, vb, c_row, name):
    S = qs_.shape[0]
    T, n_t = _attn_tiles(S)
    RB = min(T, ATTN_ROWS_FWD)

    def body(q_ref, k_ref, v_ref, cr_ref, o_ref, lse_ref, s_ref, p_ref, m_ref, l_ref, acc_ref,
             a_ref):
        qi = pl.program_id(1)
        lanes = [slice(h2 * HEAD_DIM, (h2 + 1) * HEAD_DIM) for h2 in range(2)]
        col = lax.broadcasted_iota(jnp.int32, (RB, T), 1)
        row = lax.broadcasted_iota(jnp.int32, (RB, T), 0)
        m_ref[...] = jnp.full(m_ref.shape, NEG_INF, F32)
        l_ref[...] = jnp.zeros_like(l_ref)
        acc_ref[...] = jnp.zeros_like(acc_ref)

        def step(kj, masked):
            ks = pl.ds(pl.multiple_of(kj * T, T), T)
            for h2, hl in enumerate(lanes):
                s_ref[h2] = _dot_nt(q_ref[:, hl], k_ref[ks, hl]) - cr_ref[h2:h2 + 1, ks]
            for h2, hl in enumerate(lanes):
                blocks = [slice(i * RB, (i + 1) * RB) for i in range(T // RB)]

                def logits(i, rows):
                    s = s_ref[h2, rows, :]
                    return jnp.where(col <= row + i * RB, s, NEG_INF) if masked else s

                wide = lambda x: jnp.broadcast_to(x, (RB, LANES))
                for i, rows in enumerate(blocks):
                    mx = wide(jnp.max(logits(i, rows), axis=1, keepdims=True))
                    a_ref[h2, rows, :] = m_ref[h2, rows, :]
                    m_ref[h2, rows, :] = jnp.maximum(m_ref[h2, rows, :], mx)
                for i, rows in enumerate(blocks):
                    m_new = m_ref[h2, rows, :]
                    p = jnp.exp(logits(i, rows) - jnp.tile(m_new, (1, T // LANES)))
                    alpha = jnp.exp(a_ref[h2, rows, :] - m_new)
                    a_ref[h2, rows, :] = alpha
                    l_ref[h2, rows, :] = (alpha * l_ref[h2, rows, :]
                                          + wide(jnp.sum(p, axis=1, keepdims=True)))
                    p_ref[h2, rows, :] = p.astype(BF16)
                acc_ref[h2] = (a_ref[h2, :, :HEAD_DIM] * acc_ref[h2]
                               + _dot(p_ref[h2], v_ref[ks, hl]))

        def unmasked(kj, _):
            step(kj, False)
            return 0

        lax.fori_loop(0, qi, unmasked, 0)
        step(qi, True)
        for h2, hl in enumerate(lanes):
            o_ref[:, hl] = (acc_ref[h2] / l_ref[h2, :, :HEAD_DIM]).astype(BF16)
            lse_ref[:, hl] = m_ref[h2, :, :HEAD_DIM] + jnp.log(l_ref[h2, :, :HEAD_DIM])

    qblk = pl.BlockSpec((T, LANES), lambda h, i: (i, h))
    kv = pl.BlockSpec((S, LANES), lambda h, i: (0, h))
    return pl.pallas_call(
        body, name=name, grid=(N_CBLK, n_t),
        in_specs=[qblk, kv, kv, pl.BlockSpec((None, 2, S), lambda h, i: (h, 0, 0))],
        out_specs=[qblk, qblk],
        out_shape=[jax.ShapeDtypeStruct((S, D_MODEL), BF16),
                   jax.ShapeDtypeStruct((S, D_MODEL), F32)],
        scratch_shapes=[pltpu.VMEM((2, T, T), F32), pltpu.VMEM((2, T, T), BF16),
                        pltpu.VMEM((2, T, LANES), F32), pltpu.VMEM((2, T, LANES), F32),
                        pltpu.VMEM((2, T, HEAD_DIM), F32), pltpu.VMEM((2, T, LANES), F32)],
        compiler_params=_params("parallel", "parallel"),
    )(qs_, kn, vb, c_row)


def _attn_bwd(qs_, kn, vb, do, o, lse, c_row, name):
    S = qs_.shape[0]
    T, n_t = _attn_tiles(S)

    def body(q_ref, k_ref, v_ref, do_ref, o_ref, lse_ref, cr_ref,
             dq_ref, dk_ref, dv_ref, dc_ref, rho_ref, dd_ref):
        kj = pl.program_id(1)
        causal = _causal(T)
        lanes = [slice(h2 * HEAD_DIM, (h2 + 1) * HEAD_DIM) for h2 in range(2)]
        ones = [slice(h2 * HEAD_DIM, h2 * HEAD_DIM + 1) for h2 in range(2)]

        @pl.when(kj == 0)
        def _():
            dq_ref[...] = jnp.zeros_like(dq_ref)
            rho_ref[...] = jnp.zeros_like(rho_ref)
            p_sum = _head_group_matrix(1.0)

            def fill(ci, _):
                rows = pl.ds(pl.multiple_of(ci * T, T), T)
                dd_ref[rows, :] = _group_dot(do_ref[rows, :].astype(F32) * o_ref[rows, :].astype(F32),
                                             p_sum)
                return 0

            lax.fori_loop(0, n_t, fill, 0)

        kh = [k_ref[:, hl] for hl in lanes]
        vh = [v_ref[:, hl] for hl in lanes]
        ck = [cr_ref[h2:h2 + 1, :] for h2 in range(2)]

        def step(qi, carry, masked):
            qs = pl.ds(pl.multiple_of(qi * T, T), T)
            out = []
            for h2, hl in enumerate(lanes):
                dk, dv, dc = carry[h2]
                qh, doh = q_ref[qs, hl], do_ref[qs, hl]
                s = _dot_nt(qh, kh[h2]) - ck[h2]
                if masked:
                    s = jnp.where(causal, s, NEG_INF)
                p = jnp.exp(s - lse_ref[qs, ones[h2]])
                ds = p * (_dot_nt(doh, vh[h2]) - dd_ref[qs, ones[h2]])
                dsb = ds.astype(BF16)
                dq_ref[qs, hl] += _dot(dsb, kh[h2])
                rho_ref[qs, hl] += jnp.broadcast_to(jnp.sum(ds, axis=1, keepdims=True),
                                                    (T, HEAD_DIM))
                out.append((dk + _dot_tn(dsb, qh), dv + _dot_tn(p.astype(BF16), doh),
                            dc - jnp.sum(ds, axis=0, keepdims=True)))
            return tuple(out)

        init = tuple((jnp.zeros((T, HEAD_DIM), F32), jnp.zeros((T, HEAD_DIM), F32),
                      jnp.zeros((1, T), F32)) for _ in lanes)
        carry = step(kj, init, True)
        carry = lax.fori_loop(kj + 1, n_t, lambda qi, c: step(qi, c, False), carry)
        for h2, ((dk, dv, dc), hl) in enumerate(zip(carry, lanes)):
            dk_ref[:, hl] = dk
            dv_ref[:, hl] = dv.astype(BF16)
            dc_ref[h2:h2 + 1, :] = dc

    kblk = pl.BlockSpec((T, LANES), lambda h, j: (j, h))
    full = pl.BlockSpec((S, LANES), lambda h, j: (0, h))
    crow = pl.BlockSpec((None, 2, T), lambda h, j: (h, 0, j))
    wide = jax.ShapeDtypeStruct((S, D_MODEL), F32)
    return pl.pallas_call(
        body, name=name, grid=(N_CBLK, n_t),
        in_specs=[full, kblk, kblk, full, full, full, crow],
        out_specs=[full, kblk, kblk, crow, full],
        out_shape=[wide, wide, jax.ShapeDtypeStruct((S, D_MODEL), BF16),
                   jax.ShapeDtypeStruct((N_CBLK, 2, S), F32), wide],
        scratch_shapes=[pltpu.VMEM((S, LANES), F32)],
        compiler_params=_params("parallel", "arbitrary"),
    )(qs_, kn, vb, do, o, lse, c_row)


def _loss_head(y, target, name, tm=512):
    S, D = y.shape
    tm = _token_tile(S, tm)

    def body(y_ref, t_ref, loss_ref, dy_ref):
        err = y_ref[...] - t_ref[...]
        dy_ref[...] = err / D

        @pl.when(pl.program_id(0) == 0)
        def _():
            loss_ref[...] = jnp.zeros_like(loss_ref)
        row_loss = jnp.mean(err * err, axis=1, keepdims=True)
        loss_ref[...] += 0.5 * jnp.sum(row_loss, axis=0, keepdims=True)

    tok = pl.BlockSpec((tm, D), lambda i: (i, 0))
    return pl.pallas_call(
        body, name=name, grid=(S // tm,),
        in_specs=[tok, tok],
        out_specs=[pl.BlockSpec((1, 1), lambda i: (0, 0)), tok],
        out_shape=[jax.ShapeDtypeStruct((1, 1), F32), jax.ShapeDtypeStruct((S, D), F32)],
        compiler_params=_params("arbitrary"),
    )(y, target)


ALL_PEERS = tuple(range(1, N_DEV))
NEAR_PEERS = (1, 2, 4, 6)
FAR_CHIPS = (2, 4, 6)


def _landing_shapes(arrays, gathers):
    return [jax.ShapeDtypeStruct((N_DEV,) + a.shape if g else a.shape, a.dtype)
            for a, g in zip(arrays, gathers)]


def _my_index():
    return 4 * lax.axis_index("x") + 2 * lax.axis_index("y") + lax.axis_index("c")


def _own_copies(srcs, lands, gathers, sems):
    me = _my_index()
    return [pltpu.make_async_copy(src if g else src.at[me], land.at[me], sems.at[a])
            for a, (src, land, g) in enumerate(zip(srcs, lands, gathers))]


def _peer_copies(srcs, lands, gathers, send_sems, recv_sems, ks=ALL_PEERS):
    x, y, c = lax.axis_index("x"), lax.axis_index("y"), lax.axis_index("c")
    me = 4 * x + 2 * y + c
    out = []
    for j, k in enumerate(ks):
        to = (1 - x if k & 4 else x, 1 - y if k & 2 else y, 1 - c if k & 1 else c)
        peer = 4 * to[0] + 2 * to[1] + to[2]
        for a, (src, land, g) in enumerate(zip(srcs, lands, gathers)):
            sem = a * len(ks) + j
            src_blk = src if g else src.at[peer]

            def copy(slot, src_blk=src_blk, land=land, sem=sem, to=to):
                return pltpu.make_async_remote_copy(
                    src_ref=src_blk, dst_ref=land.at[slot], send_sem=send_sems.at[sem],
                    recv_sem=recv_sems.at[sem], device_id=to,
                    device_id_type=pl.DeviceIdType.MESH)

            out.append((k, a, copy(me), copy(peer)))
    return out


def _forward_copies(lands, send_sems, recv_sems):
    x, y, c = lax.axis_index("x"), lax.axis_index("y"), lax.axis_index("c")
    out = []
    for j, f in enumerate(FAR_CHIPS):
        chip = 4 * (1 - x if f & 4 else x) + 2 * (1 - y if f & 2 else y)
        for a, land in enumerate(lands):
            sem = a * len(FAR_CHIPS) + j

            def copy(slot, land=land, sem=sem):
                return pltpu.make_async_remote_copy(
                    src_ref=land.at[slot], dst_ref=land.at[slot], send_sem=send_sems.at[sem],
                    recv_sem=recv_sems.at[sem], device_id=(x, y, 1 - c),
                    device_id_type=pl.DeviceIdType.MESH)

            out.append((f, a, copy(chip + c), copy(chip + 1 - c)))
    return out


def _exchange(arrays, gathers, name, two_level=False):
    n = len(arrays)
    ks = NEAR_PEERS if two_level else ALL_PEERS
    assert not two_level or all(gathers)

    def body(*refs):
        ins, outs = refs[:n], refs[n:2 * n]
        send_sems, recv_sems, own_sems, fwd_send_sems, fwd_recv_sems = refs[2 * n:]
        own = _own_copies(ins, outs, gathers, own_sems)
        for cp in own:
            cp.start()
        copies = _peer_copies(ins, outs, gathers, send_sems, recv_sems, ks)
        for _, _, send, _ in copies:
            send.start()
        passed = {}
        if two_level:
            passed = {(f, a): (send, arrival)
                      for f, a, send, arrival in _forward_copies(outs, fwd_send_sems, fwd_recv_sems)}
        for k, a, _, arrival in copies:
            arrival.wait_recv()
            if (k, a) in passed:
                passed[k, a][0].start()
        for send, arrival in passed.values():
            arrival.wait_recv()
            send.wait_send()
        for _, _, send, _ in copies:
            send.wait_send()
        for cp in own:
            cp.wait()

    hbm = pl.BlockSpec(memory_space=pl.ANY)
    return pl.pallas_call(
        body, name=name,
        in_specs=[hbm] * n, out_specs=[hbm] * n, out_shape=_landing_shapes(arrays, gathers),
        scratch_shapes=[pltpu.SemaphoreType.DMA((n * len(ks),)),
                        pltpu.SemaphoreType.DMA((n * len(ks),)),
                        pltpu.SemaphoreType.DMA((n,)),
                        pltpu.SemaphoreType.DMA((n * len(FAR_CHIPS),)),
                        pltpu.SemaphoreType.DMA((n * len(FAR_CHIPS),))],
        compiler_params=pltpu.CompilerParams(has_side_effects=True),
    )(*arrays)


_HBM = pl.BlockSpec(memory_space=pltpu.HBM)
_SEM = pl.BlockSpec(memory_space=pltpu.SEMAPHORE)
_ANY = pl.BlockSpec(memory_space=pl.ANY)
_DATAFLOW = pltpu.SideEffectType.DATAFLOW_SIDE_EFFECTING


def _in_hbm(a):
    return pltpu.with_memory_space_constraint(a, pltpu.HBM)


def _exchange_start(arrays, gathers, after, name, ks=ALL_PEERS):
    n = len(arrays)
    lands = [lax.empty(s.shape, s.dtype) for s in _landing_shapes(arrays, gathers)]

    def body(*refs):
        srcs, dsts = refs[:n], refs[n:2 * n]
        send_sems, recv_sems, own_sems = refs[2 * n + 1:2 * n + 4]
        token = refs[-1]
        for cp in _own_copies(srcs, dsts, gathers, own_sems):
            cp.start()
        for _, _, send, _ in _peer_copies(srcs, dsts, gathers, send_sems, recv_sems, ks):
            send.start()
        token[...] = jnp.zeros_like(token)

    hbm_like = [pltpu.HBM(a.shape, a.dtype) for a in list(arrays) + lands]
    res = pl.pallas_call(
        body, name=name,
        in_specs=[_HBM] * (2 * n) + [_ANY],
        out_specs=(_SEM, _SEM, _SEM, *[_HBM] * (2 * n), pl.BlockSpec(memory_space=pltpu.VMEM)),
        out_shape=(pltpu.SemaphoreType.DMA((n * len(ks),)), pltpu.SemaphoreType.DMA((n * len(ks),)),
                   pltpu.SemaphoreType.DMA((n,)), *hbm_like,
                   jax.ShapeDtypeStruct((8, LANES), F32)),
        input_output_aliases={i: 3 + i for i in range(2 * n)},
        compiler_params=pltpu.CompilerParams(has_side_effects=_DATAFLOW),
    )(*[_in_hbm(a) for a in list(arrays) + lands], after)
    return (res[0], res[1], res[2], res[3:3 + n], res[3 + n:3 + 2 * n]), res[-1]


def _exchange_wait(started, gathers, after, name, ks=ALL_PEERS):
    send_sems, recv_sems, own_sems, arrays, lands = started
    n = len(arrays)

    def body(*refs):
        srcs, dsts = refs[:n], refs[n:2 * n]
        for _, _, send, arrival in _peer_copies(srcs, dsts, gathers, refs[2 * n], refs[2 * n + 1],
                                                ks):
            arrival.wait_recv()
            send.wait_send()
        for cp in _own_copies(srcs, dsts, gathers, refs[2 * n + 2]):
            cp.wait()

    hbm_like = [pltpu.HBM(a.shape, a.dtype) for a in list(arrays) + list(lands)]
    res = pl.pallas_call(
        body, name=name,
        in_specs=[_HBM] * (2 * n) + [_SEM, _SEM, _SEM, _ANY],
        out_specs=[_HBM] * (2 * n), out_shape=hbm_like,
        input_output_aliases={i: i for i in range(2 * n)},
        compiler_params=pltpu.CompilerParams(has_side_effects=_DATAFLOW),
    )(*arrays, *lands, send_sems, recv_sems, own_sems, after)
    return res[n:]


def _forward_start(lands, after, name):
    n = len(lands)

    def body(*refs):
        send_sems, recv_sems = refs[n + 1:n + 3]
        for _, _, send, _ in _forward_copies(refs[:n], send_sems, recv_sems):
            send.start()
        refs[-1][...] = jnp.zeros_like(refs[-1])

    n_sem = n * len(FAR_CHIPS)
    res = pl.pallas_call(
        body, name=name,
        in_specs=[_HBM] * n + [_ANY],
        out_specs=(_SEM, _SEM, *[_HBM] * n, pl.BlockSpec(memory_space=pltpu.VMEM)),
        out_shape=(pltpu.SemaphoreType.DMA((n_sem,)), pltpu.SemaphoreType.DMA((n_sem,)),
                   *[pltpu.HBM(a.shape, a.dtype) for a in lands],
                   jax.ShapeDtypeStruct((8, LANES), F32)),
        input_output_aliases={i: 2 + i for i in range(n)},
        compiler_params=pltpu.CompilerParams(has_side_effects=_DATAFLOW),
    )(*[_in_hbm(a) for a in lands], after)
    return (res[0], res[1], res[2:2 + n]), res[-1]


def _forward_wait(started, after, name):
    send_sems, recv_sems, lands = started
    n = len(lands)

    def body(*refs):
        for _, _, send, arrival in _forward_copies(refs[:n], refs[n], refs[n + 1]):
            arrival.wait_recv()
            send.wait_send()

    return pl.pallas_call(
        body, name=name,
        in_specs=[_HBM] * n + [_SEM, _SEM, _ANY],
        out_specs=[_HBM] * n, out_shape=[pltpu.HBM(a.shape, a.dtype) for a in lands],
        input_output_aliases={i: i for i in range(n)},
        compiler_params=pltpu.CompilerParams(has_side_effects=_DATAFLOW),
    )(*lands, send_sems, recv_sems, after)


def _reduce_adamw(parts, w, m, v, name):
    n, R, C = parts.shape
    tr = 256 if R % 256 == 0 else R

    def body(p_ref, w_ref, m_ref, v_ref, g_ref, d_ref, nm_ref, nv_ref):
        g = p_ref[0].astype(F32)
        for s in range(1, n):
            g = g + p_ref[s].astype(F32)
        g_ref[...] = g
        m_new = ADAM_B1 * m_ref[...] + (1.0 - ADAM_B1) * g
        v_new = ADAM_B2 * v_ref[...] + (1.0 - ADAM_B2) * (g * g)
        nm_ref[...] = m_new
        nv_ref[...] = v_new
        m_hat = m_new / (1.0 - ADAM_B1 ** ADAM_STEP)
        v_hat = v_new / (1.0 - ADAM_B2 ** ADAM_STEP)
        d_ref[...] = -ADAM_LR * (m_hat / (jnp.sqrt(v_hat) + ADAM_EPS) + ADAM_WD * w_ref[...])

    blk = pl.BlockSpec((tr, C), lambda i: (i, 0))
    out = jax.ShapeDtypeStruct((R, C), F32)
    return pl.pallas_call(
        body, name=name, grid=(R // tr,),
        in_specs=[pl.BlockSpec((n, tr, C), lambda i: (0, i, 0)), blk, blk, blk],
        out_specs=[blk] * 4, out_shape=[out] * 4,
        compiler_params=_params("parallel"),
    )(parts, w, m, v)


def _pack(arrays):
    flat = jnp.concatenate([a.reshape(-1).astype(F32) for a in arrays])
    pad = (-flat.shape[0]) % (8 * LANES)
    return jnp.pad(flat, (0, pad)).reshape(-1, LANES)


def _unpack(buf, shapes):
    flat = buf.reshape(-1)
    out, off = [], 0
    for shp in shapes:
        size = 1
        for s in shp:
            size *= s
        out.append(flat[off:off + size].reshape(shp))
        off += size
    return out


def _block_diag_pairs(w):
    w = w.reshape(N_CBLK, 2, LRU_BLOCK_DIM, LRU_BLOCK_DIM)
    z = jnp.zeros_like(w[:, 0])
    top = jnp.concatenate([w[:, 0], z], axis=2)
    bot = jnp.concatenate([z, w[:, 1]], axis=2)
    return jnp.concatenate([top, bot], axis=1)


def _diag_pairs(m):
    h = LRU_BLOCK_DIM
    return jnp.stack([m[:, :h, :h], m[:, h:, h:]], axis=1).reshape(2 * N_CBLK, h, h)


SMALL = ("mlp_norm", "lru_conv_b", "lru_w_r", "lru_b_r", "lru_w_i", "lru_b_i",
         "lru_lambda", "fox_b_f", "fox_q_gain", "fox_k_gain")
WEIGHTS = ("mix_norm", "mlp_norm", "mlp_w1", "mlp_w2", "lru_w_in", "lru_conv_w", "lru_conv_b",
           "lru_w_r", "lru_b_r", "lru_w_i", "lru_b_i", "lru_lambda", "lru_w_out", "fox_w_in",
           "fox_b_f", "fox_q_gain", "fox_k_gain", "fox_w_out")


def kernel(x, mix_norm, mlp_norm, mlp_w1, mlp_w2, lru_w_in, lru_conv_w, lru_conv_b, lru_w_r, lru_b_r, lru_w_i, lru_b_i, lru_lambda, lru_w_out, fox_w_in, fox_b_f, fox_q_gain, fox_k_gain, fox_w_out, loss_target, m_mix_norm, m_mlp_norm, m_mlp_w1, m_mlp_w2, m_lru_w_in, m_lru_conv_w, m_lru_conv_b, m_lru_w_r, m_lru_b_r, m_lru_w_i, m_lru_b_i, m_lru_lambda, m_lru_w_out, m_fox_w_in, m_fox_b_f, m_fox_q_gain, m_fox_k_gain, m_fox_w_out, v_mix_norm, v_mlp_norm, v_mlp_w1, v_mlp_w2, v_lru_w_in, v_lru_conv_w, v_lru_conv_b, v_lru_w_r, v_lru_b_r, v_lru_w_i, v_lru_b_i, v_lru_lambda, v_lru_w_out, v_fox_w_in, v_fox_b_f, v_fox_q_gain, v_fox_k_gain, v_fox_w_out):
    w_in = dict(mix_norm=mix_norm, mlp_norm=mlp_norm, mlp_w1=mlp_w1, mlp_w2=mlp_w2,
                lru_w_in=lru_w_in, lru_conv_w=lru_conv_w, lru_conv_b=lru_conv_b, lru_w_r=lru_w_r,
                lru_b_r=lru_b_r, lru_w_i=lru_w_i, lru_b_i=lru_b_i, lru_lambda=lru_lambda,
                lru_w_out=lru_w_out, fox_w_in=fox_w_in, fox_b_f=fox_b_f, fox_q_gain=fox_q_gain,
                fox_k_gain=fox_k_gain, fox_w_out=fox_w_out)
    m_in = dict(mix_norm=m_mix_norm, mlp_norm=m_mlp_norm, mlp_w1=m_mlp_w1, mlp_w2=m_mlp_w2,
                lru_w_in=m_lru_w_in, lru_conv_w=m_lru_conv_w, lru_conv_b=m_lru_conv_b,
                lru_w_r=m_lru_w_r, lru_b_r=m_lru_b_r, lru_w_i=m_lru_w_i, lru_b_i=m_lru_b_i,
                lru_lambda=m_lru_lambda, lru_w_out=m_lru_w_out, fox_w_in=m_fox_w_in,
                fox_b_f=m_fox_b_f, fox_q_gain=m_fox_q_gain, fox_k_gain=m_fox_k_gain,
                fox_w_out=m_fox_w_out)
    v_in = dict(mix_norm=v_mix_norm, mlp_norm=v_mlp_norm, mlp_w1=v_mlp_w1, mlp_w2=v_mlp_w2,
                lru_w_in=v_lru_w_in, lru_conv_w=v_lru_conv_w, lru_conv_b=v_lru_conv_b,
                lru_w_r=v_lru_w_r, lru_b_r=v_lru_b_r, lru_w_i=v_lru_w_i, lru_b_i=v_lru_b_i,
                lru_lambda=v_lru_lambda, lru_w_out=v_lru_w_out, fox_w_in=v_fox_w_in,
                fox_b_f=v_fox_b_f, fox_q_gain=v_fox_q_gain, fox_k_gain=v_fox_k_gain,
                fox_w_out=v_fox_w_out)
    D = D_MODEL
    S = x.shape[1]
    x0, target = x[0], loss_target[0]
    me = 4 * lax.axis_index("x") + 2 * lax.axis_index("y") + lax.axis_index("c")

    def bf16(a):
        return a.astype(BF16)

    (lru_in_g,) = _exchange([bf16(lru_w_in[0])], [True], "gather_lru_in", two_level=True)
    gather_lru, tok = _exchange_start([bf16(lru_w_out[0]), lru_conv_w[0]], [True] * 2, lru_in_g,
                                      "gather_lru_start")
    gather_mlp0, tok = _exchange_start([bf16(mlp_w1[0]), bf16(mlp_w2[0])], [True] * 2, tok,
                                       "gather_mlp0_start", NEAR_PEERS)
    gather_fox, tok = _exchange_start([bf16(fox_w_in[0]), bf16(fox_w_out[0])], [True] * 2, tok,
                                      "gather_fox_start")
    gather_mlp1, tok = _exchange_start([bf16(mlp_w1[1]), bf16(mlp_w2[1])], [True] * 2, tok,
                                       "gather_mlp1_start", NEAR_PEERS)

    def pass_on(started, after, name):
        lands = _exchange_wait(started, [True] * 2, after, name + "_wait", NEAR_PEERS)
        return _forward_start(lands, after, name + "_pass_start")
    wr =_block_diag_pairs(lru_w_r[0]).astype(BF16)
    wi = _block_diag_pairs(lru_w_i[0]).astype(BF16)
    b_r, b_i = lru_b_r.reshape(1, D), lru_b_i.reshape(1, D)
    q_gain, k_gain = jnp.tile(fox_q_gain, (1, 2)), jnp.tile(fox_k_gain, (1, 2))
    b_f = jnp.pad(fox_b_f, ((0, 0), (0, LANES - N_HEADS)))
    g_mix0, g_mix1 = mix_norm[0:1] + tok[0, 0], mix_norm[1:2]
    g_mlp0, g_mlp1 = mlp_norm[0:1], mlp_norm[1:2]

    (u0,), h0 = _norm_matmul(x0, g_mix0, [lru_in_g], "lru_in_proj")
    lru_out_g, conv_g = _exchange_wait(gather_lru, [True] * 2, u0, "gather_lru_wait")
    lru_out_w = lru_out_g.reshape(D, D)
    conv_w = conv_g.transpose(1, 0, 2).reshape(CONV_WIDTH, D)
    y_lru, hs =_lru_fwd(u0, conv_w, lru_conv_b, wr, b_r, wi, b_i, lru_lambda, "lru_core")
    pass_mlp0, tok = pass_on(gather_mlp0, y_lru, "gather_mlp0")
    x1 = _matmul_res(y_lru, lru_out_w, x0, "lru_out_proj", tok)
    w1g0, w2g0 = _forward_wait(pass_mlp0, x1, "gather_mlp0_pass_wait")
    x2, h1, r1 = _mlp_fwd(x1, g_mlp0, w1g0, w2g0, "mlp0")
    fox_in_g, fox_out_g = _exchange_wait(gather_fox, [True] * 2, x2, "gather_fox_wait")
    fox_out_w = fox_out_g.reshape(D, D)
    fox_full = fox_in_g.transpose(1, 0, 2).reshape(D, 3 * D + N_HEADS)
    wqkv = fox_full[:, :3 * D].reshape(D, 3, D).transpose(1, 0, 2)
    wf = jnp.pad(fox_full[:, 3 * D:], ((0, 0), (0, LANES - N_HEADS)))[None]
    (u_qkv, f), h2 = _norm_matmul(x2, g_mix1, [wqkv, wf], "fox_in_proj")
    qn, kn, vb = _qk_prep(u_qkv, q_gain, k_gain, "fox_qk_norm")
    c_col = _forget_fwd(f, b_f, "fox_forget")
    c_row = c_col[:, :N_HEADS].T.reshape(N_CBLK, 2, S)
    o, lse = _attn_fwd(qn, kn, vb, c_row, "fox_attn")
    pass_mlp1, tok = pass_on(gather_mlp1, o, "gather_mlp1")
    x3 = _matmul_res(o, fox_out_w, x2, "fox_out_proj", tok)
    w1g1, w2g1 = _forward_wait(pass_mlp1, x3, "gather_mlp1_pass_wait")
    x4, h3, r3 = _mlp_fwd(x3, g_mlp1, w1g1, w2g1, "mlp1")
    loss_local, dx4 = _loss_head(x4, target, "loss_head")

    dx3, dg_mlp1, da3 = _mlp_bwd(dx4, x3, g_mlp1, r3, w1g1, w2g1, "mlp1_bwd")
    dw1_1 = _matmul_tn(h3, da3, "mlp1_dw1", cols=2, col_blocks=N_DEV)
    dw2_1 = _matmul_tn(r3, dx4, "mlp1_dw2", rows=2, a_square=True).reshape(N_DEV, -1, D)
    grads_mlp1, tok = _exchange_start([dw1_1, dw2_1], [False] * 2, tok, "grads_mlp1_start")
    do = _matmul_nt(dx3, fox_out_w, "fox_out_bwd", BF16, tok)
    d_fox_out = _matmul_tn(o, dx3, "fox_out_dw").reshape(N_DEV, -1, D)
    dqn, dkn, dv, dc_row, rho = _attn_bwd(qn, kn, vb, do, o, lse, c_row, "fox_attn_bwd")
    duq, duk, dq_gain, dk_gain = _qk_bwd(u_qkv, dqn, dkn, q_gain, k_gain, "fox_qk_norm_bwd")
    dc_k = jnp.pad(dc_row.reshape(N_HEADS, S).T, ((0, 0), (0, LANES - N_HEADS)))
    df, db_f = _forget_bwd(dc_k, rho, f, b_f, "fox_forget_bwd")
    dx2, dg_mix1 = _proj_bwd([[duq, duk, dv], [df]], [wqkv, wf], x2, g_mix1, dx3, "fox_in_bwd")
    d_fox_in = jnp.concatenate(
        [_matmul_tn(h2, duq, "fox_in_dwq"), _matmul_tn(h2, duk, "fox_in_dwk"),
         _matmul_tn(h2, dv, "fox_in_dwv"), _matmul_tn(h2, df, "fox_in_dwf")[:, :N_HEADS]], axis=1)
    d_fox_in = d_fox_in.reshape(D, N_DEV, -1).transpose(1, 0, 2)
    grads_fox, tok = _exchange_start([d_fox_in, d_fox_out], [False] * 2, tok, "grads_fox_start")
    dx1, dg_mlp0, da1 = _mlp_bwd(dx2, x1, g_mlp0 + tok[0, 0], r1, w1g0, w2g0, "mlp0_bwd")
    dw1_0 = _matmul_tn(h1, da1, "mlp0_dw1", cols=2, col_blocks=N_DEV)
    dw2_0 = _matmul_tn(r1, dx2, "mlp0_dw2", rows=2, a_square=True).reshape(N_DEV, -1, D)
    grads_mlp0, tok = _exchange_start([dw1_0, dw2_0], [False] * 2, tok, "grads_mlp0_start")
    dy_lru = _matmul_nt(dx1, lru_out_w, "lru_out_bwd", F32, tok)
    d_lru_out = _matmul_tn(y_lru, dx1, "lru_out_dw").reshape(N_DEV, -1, D)
    dgp, dxb, d_conv_w, d_conv_b, d_b_r, d_b_i, d_lam, d_wr, d_wi = _lru_bwd(
        dy_lru, u0, hs, conv_w, lru_conv_b, wr, b_r, wi, b_i, lru_lambda, "lru_core_bwd")

    small_grads = dict(
        mlp_norm=jnp.concatenate([dg_mlp0, dg_mlp1], axis=0),
        lru_conv_b=d_conv_b, lru_w_r=_diag_pairs(d_wr), lru_b_r=d_b_r, lru_w_i=_diag_pairs(d_wi),
        lru_b_i=d_b_i, lru_lambda=d_lam, fox_b_f=db_f[:, :N_HEADS],
        fox_q_gain=dq_gain[:, :HEAD_DIM], fox_k_gain=dk_gain[:, :HEAD_DIM])
    small_partial = _pack([dg_mix1] + [small_grads[n] for n in SMALL] + [d_conv_w])
    grads_lru_out, tok = _exchange_start([d_lru_out, small_partial], [False, True], tok,
                                         "grads_lru_out_start")
    dx0, dg_mix0 = _proj_bwd([[dgp, dxb]], [lru_in_g], x0, mix_norm[0:1] + tok[0, 0], dx1,
                             "lru_in_bwd")
    d_lru_in = jnp.concatenate([_matmul_tn(h0, dgp, "lru_in_dw_gate", col_blocks=4),
                                _matmul_tn(h0, dxb, "lru_in_dw_x", col_blocks=4)], axis=0)
    grads_lru_in, tok = _exchange_start([d_lru_in, dg_mix0], [False, True], tok,
                                        "grads_lru_in_start")

    grads, deltas, new_m, new_v = {}, {}, {}, {}

    def update(name, parts, sel=None):
        w, m, v = w_in[name], m_in[name], v_in[name]
        if sel is not None:
            w, m, v = w[sel], m[sel], v[sel]
        shape = w.shape
        two_d = (-1, shape[-1])
        res = _reduce_adamw(parts.reshape((N_DEV,) + w.reshape(two_d).shape), w.reshape(two_d),
                            m.reshape(two_d), v.reshape(two_d),
                            "adamw_" + name + ("" if sel is None else "_%d" % sel))
        return [r.reshape(shape) for r in res]

    def store(name, res):
        grads[name], deltas[name], new_m[name], new_v[name] = res

    p_w1_1, p_w2_1 = _exchange_wait(grads_mlp1, [False] * 2, tok, "grads_mlp1_wait")
    up_w1_1, up_w2_1 = update("mlp_w1", p_w1_1, 1), update("mlp_w2", p_w2_1, 1)
    p_fox_in, p_fox_out = _exchange_wait(grads_fox, [False] * 2, up_w2_1[0], "grads_fox_wait")
    store("fox_w_in", update("fox_w_in", p_fox_in))
    store("fox_w_out", update("fox_w_out", p_fox_out))
    p_w1_0, p_w2_0 = _exchange_wait(grads_mlp0, [False] * 2, grads["fox_w_out"], "grads_mlp0_wait")
    up_w1_0, up_w2_0 = update("mlp_w1", p_w1_0, 0), update("mlp_w2", p_w2_0, 0)
    store("mlp_w1", [jnp.stack(p) for p in zip(up_w1_0, up_w1_1)])
    store("mlp_w2", [jnp.stack(p) for p in zip(up_w2_0, up_w2_1)])
    p_lru_out, p_small = _exchange_wait(grads_lru_out, [False, True], up_w2_0[0],
                                        "grads_lru_out_wait")
    store("lru_w_out", update("lru_w_out", p_lru_out))
    p_lru_in, p_mix0 = _exchange_wait(grads_lru_in, [False, True], grads["lru_w_out"],
                                      "grads_lru_in_wait")
    store("lru_w_in", update("lru_w_in", p_lru_in))

    mix0 = _reduce_adamw(p_mix0, mix_norm[0:1], m_mix_norm[0:1], v_mix_norm[0:1], "adamw_mix0")
    packed = lambda src, first: _pack([first] + [src[n] for n in SMALL] + [jnp.zeros((CONV_WIDTH, D))])
    small_shapes = [(1, D)] + [w_in[n].shape for n in SMALL]
    n_small = sum(math.prod(s) for s in small_shapes)
    res_small = _reduce_adamw(p_small, packed(w_in, mix_norm[1:2]), packed(m_in, m_mix_norm[1:2]),
                              packed(v_in, v_mix_norm[1:2]), "adamw_small")
    for name, *vals in zip(("mix1",) + SMALL, *[_unpack(r, small_shapes) for r in res_small]):
        if name == "mix1":
            vals = [jnp.concatenate([r0, r1], axis=0) for r0, r1 in zip(mix0, vals)]
            name = "mix_norm"
        store(name, vals)
    conv_parts = p_small.reshape(N_DEV, -1)[:, n_small:n_small + CONV_WIDTH * D]
    conv_parts = conv_parts.reshape(N_DEV, CONV_WIDTH, N_DEV, LANES)
    conv_parts = lax.dynamic_index_in_dim(conv_parts, me, axis=2, keepdims=False)
    store("lru_conv_w", update("lru_conv_w", conv_parts))

    loss = lax.psum(loss_local[0, 0], ("x", "y", "c"))
    return (loss, dx0[None], *[grads[n] for n in WEIGHTS], *[deltas[n] for n in WEIGHTS],
            *[new_m[n] for n in WEIGHTS], *[new_v[n] for n in WEIGHTS])
```

```python
import functools
import math

import jax
import jax.numpy as jnp
from jax import lax
from jax.experimental import pallas as pl
from jax.experimental.pallas import tpu as pltpu

F32 = jnp.float32
BF16 = jnp.bfloat16

N_DEV = 8
D_MODEL = 1024
D_FF = 4096
N_HEADS = 16
HEAD_DIM = 64
LRU_BLOCK_DIM = 64
CONV_WIDTH = 4
LRU_C = 8.0
EPS = 1e-6
NEG_INF = -1e30
ATTN_SCALE = HEAD_DIM ** -0.5
LANES = 128
N_CBLK = D_MODEL // LANES
VMEM_LIMIT = 52 * 2 ** 20

ADAM_LR = 0.001
ADAM_B1 = 0.9
ADAM_B2 = 0.999
ADAM_EPS = 1e-08
ADAM_WD = 0.01
ADAM_STEP = 10

_NT = (((1,), (1,)), ((), ()))
_TN = (((0,), (0,)), ((), ()))


def _params(*sem):
    return pltpu.CompilerParams(dimension_semantics=sem, vmem_limit_bytes=VMEM_LIMIT)


def _resident(shape):
    zeros = (0,) * len(shape)
    return pl.BlockSpec(shape, lambda *_: zeros, pipeline_mode=pl.Buffered(1))


def _dot(a, b):
    return jnp.dot(a, b, preferred_element_type=F32)


def _dot_nt(a, b):
    return lax.dot_general(a, b, _NT, preferred_element_type=F32)


def _dot_tn(a, b):
    return lax.dot_general(a, b, _TN, preferred_element_type=F32)


def _sigmoid(x):
    return 1.0 / (1.0 + jnp.exp(-x))


def _log_sigmoid(x):
    return -(jnp.maximum(-x, 0.0) + jnp.log1p(jnp.exp(-jnp.abs(x))))


def _expm1(x):
    poly = x * (1.0 + x * (0.5 + x * (1.0 / 6.0 + x * (1.0 / 24.0 + x * (1.0 / 120.0)))))
    return jnp.where(jnp.abs(x) < 0.1, poly, jnp.exp(x) - 1.0)


_GELU_K = 0.7978845608028654


def _gelu(x):
    return 0.5 * x * (1.0 + jnp.tanh(_GELU_K * (x + 0.044715 * (x * x * x))))


def _gelu_grad(x):
    t = jnp.tanh(_GELU_K * (x + 0.044715 * (x * x * x)))
    return 0.5 * (1.0 + t) + 0.5 * x * (1.0 - t * t) * (_GELU_K * (1.0 + 3 * 0.044715 * x * x))


def _rms_scale(x):
    return lax.rsqrt(jnp.mean(x * x, axis=-1, keepdims=True) + EPS)


def _norm_bwd(dh, x, g):
    rs = _rms_scale(x)
    xhat = x * rs
    dxhat = dh * g
    dx = rs * (dxhat - xhat * jnp.mean(dxhat * xhat, axis=-1, keepdims=True))
    return dx, jnp.sum(dh * xhat, axis=0, keepdims=True)


def _token_tile(S, want):
    tm = min(S, want)
    assert S % tm == 0
    return tm


def _norm_matmul(x, g, ws, name, tm=256):
    S, D = x.shape
    tm = _token_tile(S, tm)
    n = len(ws)

    def body(x_ref, g_ref, *refs):
        w_refs, o_refs, h_ref = refs[:n], refs[n:2 * n], refs[2 * n]
        xv = x_ref[...]
        h = (xv * _rms_scale(xv) * g_ref[...]).astype(BF16)
        h_ref[...] = h
        for w_ref, o_ref in zip(w_refs, o_refs):
            nb, _, nw = w_ref.shape
            for d in range(nb):
                o_ref[:, d * nw:(d + 1) * nw] = _dot(h, w_ref[d])

    widths = [w.shape[0] * w.shape[2] for w in ws]
    outs = pl.pallas_call(
        body, name=name, grid=(S // tm,),
        in_specs=[pl.BlockSpec((tm, D), lambda i: (i, 0)), _resident((1, D))]
        + [_resident(w.shape) for w in ws],
        out_specs=[pl.BlockSpec((tm, n_), lambda i: (i, 0)) for n_ in widths]
        + [pl.BlockSpec((tm, D), lambda i: (i, 0))],
        out_shape=[jax.ShapeDtypeStruct((S, n_), F32) for n_ in widths]
        + [jax.ShapeDtypeStruct((S, D), BF16)],
        compiler_params=_params("parallel"),
    )(x, g, *ws)
    return outs[:n], outs[n]


def _matmul_res(a, w, res, name, after, tm=512):
    S, K = a.shape
    N = w.shape[1]
    tm = _token_tile(S, tm)

    def body(a_ref, w_ref, r_ref, after_ref, o_ref):
        o_ref[...] = r_ref[...] + _dot(a_ref[...], w_ref[...])

    return pl.pallas_call(
        body, name=name, grid=(S // tm,),
        in_specs=[pl.BlockSpec((tm, K), lambda i: (i, 0)), _resident((K, N)),
                  pl.BlockSpec((tm, N), lambda i: (i, 0)), pl.BlockSpec(memory_space=pl.ANY)],
        out_specs=pl.BlockSpec((tm, N), lambda i: (i, 0)),
        out_shape=jax.ShapeDtypeStruct((S, N), F32),
        compiler_params=_params("parallel"),
    )(a, w, res, after)


def _matmul_nt(a, w, name, out_dtype, after, tm=512):
    S, N = a.shape
    K = w.shape[0]
    tm = _token_tile(S, tm)

    def body(a_ref, w_ref, after_ref, o_ref):
        o_ref[...] = _dot_nt(a_ref[...].astype(BF16), w_ref[...]).astype(out_dtype)

    return pl.pallas_call(
        body, name=name, grid=(S // tm,),
        in_specs=[pl.BlockSpec((tm, N), lambda i: (i, 0)), _resident((K, N)),
                  pl.BlockSpec(memory_space=pl.ANY)],
        out_specs=pl.BlockSpec((tm, K), lambda i: (i, 0)),
        out_shape=jax.ShapeDtypeStruct((S, K), out_dtype),
        compiler_params=_params("parallel"),
    )(a, w, after)


def _proj_bwd(a_lists, w_list, x, g, res, name, tm=256):
    S, D = x.shape
    tm = _token_tile(S, tm)
    a_list = [a for group in a_lists for a in group]
    n, n_w = len(a_list), len(w_list)

    def body(*refs):
        a_refs, w_refs = list(refs[:n]), refs[n:n + n_w]
        x_ref, g_ref, r_ref, dx_ref, dg_ref = refs[n + n_w:]
        dh = jnp.zeros((tm, D), F32)
        for group, w_ref in zip(a_lists, w_refs):
            nw = w_ref.shape[2]
            d = 0
            for _ in group:
                a_ref = a_refs.pop(0)
                for j in range(a_ref.shape[1] // nw):
                    dh = dh + _dot_nt(a_ref[:, j * nw:(j + 1) * nw].astype(BF16), w_ref[d])
                    d += 1
        dx, dg = _norm_bwd(dh, x_ref[...], g_ref[...])
        dx_ref[...] = r_ref[...] + dx

        @pl.when(pl.program_id(0) == 0)
        def _():
            dg_ref[...] = jnp.zeros_like(dg_ref)
        dg_ref[...] += dg

    tok = lambda width: pl.BlockSpec((tm, width), lambda i: (i, 0))
    return pl.pallas_call(
        body, name=name, grid=(S // tm,),
        in_specs=[tok(a.shape[1]) for a in a_list] + [_resident(w.shape) for w in w_list]
        + [tok(D), _resident((1, D)), tok(D)],
        out_specs=[tok(D), pl.BlockSpec((1, D), lambda i: (0, 0))],
        out_shape=[jax.ShapeDtypeStruct((S, D), F32), jax.ShapeDtypeStruct((1, D), F32)],
        compiler_params=_params("arbitrary"),
    )(*a_list, *w_list, x, g, res)


def _matmul_tn(a, b, name, rows=1, cols=1, col_blocks=None, a_square=False, tm=1024):
    S, K = a.shape
    N = b.shape[1]
    tm = _token_tile(S, tm)
    n_tok = S // tm
    kr, nc = K // rows, N // cols

    def body(a_ref, b_ref, o_ref, acc_ref):
        av = a_ref[...]
        if a_square:
            av = av.astype(F32)
            av = av * av
        part = _dot_tn(av.astype(BF16), b_ref[...].astype(BF16))
        step = pl.program_id(2)

        @pl.when(step == 0)
        def _():
            acc_ref[...] = part

        @pl.when(step > 0)
        def _():
            acc_ref[...] += part

        @pl.when(step == n_tok - 1)
        def _():
            if col_blocks is None:
                o_ref[...] = acc_ref[...].astype(BF16)
            else:
                nw = N // col_blocks
                for d in range(col_blocks // cols):
                    o_ref[d] = acc_ref[:, d * nw:(d + 1) * nw].astype(BF16)

    if col_blocks is None:
        out_spec = pl.BlockSpec((kr, nc), lambda r, c, i: (r, c))
        out_shape = jax.ShapeDtypeStruct((K, N), BF16)
    else:
        assert rows == 1 and col_blocks % cols == 0
        per = col_blocks // cols
        out_spec = pl.BlockSpec((per, K, N // col_blocks), lambda r, c, i: (c, 0, 0))
        out_shape = jax.ShapeDtypeStruct((col_blocks, K, N // col_blocks), BF16)
    return pl.pallas_call(
        body, name=name, grid=(rows, cols, n_tok),
        in_specs=[pl.BlockSpec((tm, kr), lambda r, c, i: (i, r)),
                  pl.BlockSpec((tm, nc), lambda r, c, i: (i, c))],
        out_specs=out_spec, out_shape=out_shape,
        scratch_shapes=[pltpu.VMEM((kr, nc), F32)],
        compiler_params=_params("parallel", "parallel", "arbitrary"),
    )(a, b)


def _mlp_fwd(x, g, w1, w2, name, target=None, tm=256):
    S, D = x.shape
    nb, _, fb = w1.shape
    tm = _token_tile(S, tm)
    with_loss = target is not None

    def body(x_ref, g_ref, w1_ref, w2_ref, *refs):
        h_ref, r_ref = refs[-2:]
        xv = x_ref[...]
        h = (xv * _rms_scale(xv) * g_ref[...]).astype(BF16)
        h_ref[...] = h
        acc = xv
        for d in range(nb):
            r = jnp.maximum(_dot(h, w1_ref[d]), 0.0)
            r_ref[:, d * fb:(d + 1) * fb] = r.astype(BF16)
            acc = acc + _dot((r * r).astype(BF16), w2_ref[d])
        if not with_loss:
            refs[0][...] = acc
            return
        t_ref, loss_ref, dy_ref = refs[:3]
        err = acc - t_ref[...]
        dy_ref[...] = err / D

        @pl.when(pl.program_id(0) == 0)
        def _():
            loss_ref[...] = jnp.zeros_like(loss_ref)
        row_loss = jnp.mean(err * err, axis=1, keepdims=True)
        loss_ref[...] += 0.5 * jnp.sum(row_loss, axis=0, keepdims=True)

    tok = lambda width: pl.BlockSpec((tm, width), lambda i: (i, 0))
    saved_specs = [tok(D), tok(nb * fb)]
    saved_shapes = [jax.ShapeDtypeStruct((S, D), BF16), jax.ShapeDtypeStruct((S, nb * fb), BF16)]
    wide = jax.ShapeDtypeStruct((S, D), F32)
    if with_loss:
        head_specs = [pl.BlockSpec((1, 1), lambda i: (0, 0)), tok(D)]
        head_shapes = [jax.ShapeDtypeStruct((1, 1), F32), wide]
    else:
        head_specs, head_shapes = [tok(D)], [wide]
    return pl.pallas_call(
        body, name=name, grid=(S // tm,),
        in_specs=[tok(D), _resident((1, D)), _resident(w1.shape), _resident(w2.shape)]
        + ([tok(D)] if with_loss else []),
        out_specs=head_specs + saved_specs, out_shape=head_shapes + saved_shapes,
        compiler_params=_params("arbitrary" if with_loss else "parallel"),
    )(x, g, w1, w2, *([target] if with_loss else []))


def _mlp_bwd(dout, x, g, r, w1, w2, name, tm=256):
    S, D = x.shape
    nb, _, fb = w1.shape
    tm = _token_tile(S, tm)

    def body(do_ref, x_ref, g_ref, r_ref, w1_ref, w2_ref, dx_ref, dg_ref, da_ref):
        dov = do_ref[...]
        dob = dov.astype(BF16)
        dh = jnp.zeros((tm, D), F32)
        for d in range(nb):
            dz = _dot_nt(dob, w2_ref[d])
            da = (dz * (2.0 * r_ref[:, d * fb:(d + 1) * fb].astype(F32))).astype(BF16)
            da_ref[:, d * fb:(d + 1) * fb] = da
            dh = dh + _dot_nt(da, w1_ref[d])
        dx, dg = _norm_bwd(dh, x_ref[...], g_ref[...])
        dx_ref[...] = dov + dx

        @pl.when(pl.program_id(0) == 0)
        def _():
            dg_ref[...] = jnp.zeros_like(dg_ref)
        dg_ref[...] += dg

    tok = lambda width: pl.BlockSpec((tm, width), lambda i: (i, 0))
    return pl.pallas_call(
        body, name=name, grid=(S // tm,),
        in_specs=[tok(D), tok(D), _resident((1, D)), tok(nb * fb), _resident(w1.shape),
                  _resident(w2.shape)],
        out_specs=[tok(D), pl.BlockSpec((1, D), lambda i: (0, 0)), tok(nb * fb)],
        out_shape=[jax.ShapeDtypeStruct((S, D), F32), jax.ShapeDtypeStruct((1, D), F32),
                   jax.ShapeDtypeStruct((S, nb * fb), BF16)],
        compiler_params=_params("arbitrary"),
    )(dout, x, g, r, w1, w2)


def _scan_chunk(a, b, row, T, reverse):
    s = 1
    while s < T:
        if reverse:
            keep, shift = row < T - s, T - s
        else:
            keep, shift = row >= s, s
        a_sh = jnp.where(keep, pltpu.roll(a, shift, 0), 1.0)
        b_sh = jnp.where(keep, pltpu.roll(b, shift, 0), 0.0)
        b = a * b_sh + b
        a = a * a_sh
        s *= 2
    return a, b


def _row_of(x, row, r):
    return jnp.sum(jnp.where(row == r, x, 0.0), axis=0, keepdims=True)


def _shift_down(x, prev, row, k):
    if k == 0:
        return x
    return jnp.where(row < k, pltpu.roll(prev, k, 0), pltpu.roll(x, k, 0))


def _shift_up(x, nxt, row, k, T):
    if k == 0:
        return x
    return jnp.where(row < T - k, pltpu.roll(x, T - k, 0), pltpu.roll(nxt, T - k, 0))


def _lru_gates(xb, prev_xb, row, cw_ref, cb, wr, br, wi, bi, ls):
    xc = cb + cw_ref[pl.ds(0, 1), :] * _shift_down(xb, prev_xb, row, 3)
    for k in (2, 1, 0):
        xc = xc + cw_ref[pl.ds(3 - k, 1), :] * _shift_down(xb, prev_xb, row, k)
    xcb = xc.astype(BF16)
    r = _sigmoid(_dot(xcb, wr) + br)
    i = _sigmoid(_dot(xcb, wi) + bi)
    la = (LRU_C * r) * ls
    a = jnp.exp(la)
    m = jnp.sqrt(-_expm1(2.0 * la))
    return xc, xcb, r, i, a, m


def _lru_specs(S):
    col = lambda off: pl.BlockSpec((S, LANES), lambda j: (0, j + off))
    vec = pl.BlockSpec((1, LANES), lambda j: (0, j))
    mat = pl.BlockSpec((None, LANES, LANES), lambda j: (j, 0, 0))
    cwm = pl.BlockSpec((CONV_WIDTH, LANES), lambda j: (0, j))
    return col, vec, mat, cwm


def _lru_fwd(u, conv_w, conv_b, wr, br, wi, bi, lam, name):
    S = u.shape[0]
    T = _token_tile(S, 256)
    col, vec, mat, cwm = _lru_specs(S)

    def body(gp_ref, xb_ref, cw_ref, cb_ref, wr_ref, br_ref, wi_ref, bi_ref, lam_ref,
             y_ref, hs_ref):
        row = lax.broadcasted_iota(jnp.int32, (T, LANES), 0)
        ls = _log_sigmoid(lam_ref[...])
        cb, br, bi = cb_ref[...], br_ref[...], bi_ref[...]
        wr, wi = wr_ref[...], wi_ref[...]

        def chunk(ci, carry):
            prev_xb, hc = carry
            rows = pl.ds(pl.multiple_of(ci * T, T), T)
            xb = xb_ref[rows, :]
            xc, _, _, i, a, m = _lru_gates(xb, prev_xb, row, cw_ref, cb, wr, br, wi, bi, ls)
            ca, cbv = _scan_chunk(a, m * (i * xc), row, T, reverse=False)
            h = ca * hc + cbv
            hs_ref[rows, :] = h
            y_ref[rows, :] = (_gelu(gp_ref[rows, :]) * h).astype(BF16)
            return xb, _row_of(h, row, T - 1)

        lax.fori_loop(0, S // T, chunk,
                      (jnp.zeros((T, LANES), F32), jnp.zeros((1, LANES), F32)))

    return pl.pallas_call(
        body, name=name, grid=(N_CBLK,),
        in_specs=[col(0), col(N_CBLK), cwm, vec, mat, vec, mat, vec, vec],
        out_specs=[col(0), col(0)],
        out_shape=[jax.ShapeDtypeStruct((S, D_MODEL), BF16), jax.ShapeDtypeStruct((S, D_MODEL), F32)],
        compiler_params=_params("parallel"),
    )(u, u, conv_w, conv_b, wr, br, wi, bi, lam)


def _lru_bwd(dy, u, hs, conv_w, conv_b, wr, br, wi, bi, lam, name):
    S = u.shape[0]
    T = _token_tile(S, 256)
    n_chunk = S // T
    col, vec, mat, cwm = _lru_specs(S)

    def body(dy_ref, gp_ref, xb_ref, hs_ref, cw_ref, cb_ref, wr_ref, br_ref, wi_ref, bi_ref,
             lam_ref, dgp_ref, dxb_ref, dcw_ref, dcb_ref, dbr_ref, dbi_ref, dlam_ref, dwr_ref,
             dwi_ref):
        row = lax.broadcasted_iota(jnp.int32, (T, LANES), 0)
        lam = lam_ref[...]
        ls = _log_sigmoid(lam)
        cb, br, bi = cb_ref[...], br_ref[...], bi_ref[...]
        wr, wi = wr_ref[...], wi_ref[...]
        for ref in (dcw_ref, dcb_ref, dbr_ref, dbi_ref, dlam_ref, dwr_ref, dwi_ref):
            ref[...] = jnp.zeros_like(ref)

        def chunk(it, carry):
            g_next, dxc_next = carry
            ci = n_chunk - 1 - it
            rows = pl.ds(pl.multiple_of(ci * T, T), T)
            before = pl.ds(pl.multiple_of(jnp.maximum(ci - 1, 0) * T, T), T)
            first = ci == 0
            xb = xb_ref[rows, :]
            prev_xb = jnp.where(first, 0.0, xb_ref[before, :])
            xc, xcb, r, i, a, m = _lru_gates(xb, prev_xb, row, cw_ref, cb, wr, br, wi, bi, ls)
            h = hs_ref[rows, :]
            h_prev = _shift_down(h, jnp.where(first, 0.0, hs_ref[before, :]), row, 1)
            gp = gp_ref[rows, :]
            dyv = dy_ref[rows, :]
            dgp_ref[rows, :] = (dyv * h * _gelu_grad(gp)).astype(BF16)
            dh = dyv * _gelu(gp)
            ca, cbv = _scan_chunk(a, a * dh, row, T, reverse=True)
            gp_acc = ca * g_next + cbv
            g = dh + jnp.where(row < T - 1, pltpu.roll(gp_acc, T - 1, 0), g_next)
            da = g * h_prev - (g * (i * xc)) * a / m
            dla = da * a
            dlam_ref[...] += jnp.sum(dla * (LRU_C * r), axis=0, keepdims=True)
            dpr = (dla * (LRU_C * ls)) * r * (1.0 - r)
            dpi = (g * m * xc) * i * (1.0 - i)
            dbr_ref[...] += jnp.sum(dpr, axis=0, keepdims=True)
            dbi_ref[...] += jnp.sum(dpi, axis=0, keepdims=True)
            dprb, dpib = dpr.astype(BF16), dpi.astype(BF16)
            dwr_ref[...] += _dot_tn(xcb, dprb)
            dwi_ref[...] += _dot_tn(xcb, dpib)
            dxc = g * m * i + _dot_nt(dprb, wr) + _dot_nt(dpib, wi)
            dcb_ref[...] += jnp.sum(dxc, axis=0, keepdims=True)
            dxb = jnp.zeros((T, LANES), F32)
            for k in range(CONV_WIDTH):
                tap = pl.ds(CONV_WIDTH - 1 - k, 1)
                dcw_ref[tap, :] += jnp.sum(dxc * _shift_down(xb, prev_xb, row, k), axis=0,
                                           keepdims=True)
                dxb = dxb + cw_ref[tap, :] * _shift_up(dxc, dxc_next, row, k, T)
            dxb_ref[rows, :] = dxb.astype(BF16)
            return _row_of(gp_acc, row, 0), dxc

        lax.fori_loop(0, n_chunk, chunk,
                      (jnp.zeros((1, LANES), F32), jnp.zeros((T, LANES), F32)))
        dlam_ref[...] = dlam_ref[...] * _sigmoid(-lam)

    vec_out = jax.ShapeDtypeStruct((1, D_MODEL), F32)
    mat_out = jax.ShapeDtypeStruct((N_CBLK, LANES, LANES), F32)
    return pl.pallas_call(
        body, name=name, grid=(N_CBLK,),
        in_specs=[col(0), col(0), col(N_CBLK), col(0), cwm, vec, mat, vec, mat, vec, vec],
        out_specs=[col(0), col(0), cwm, vec, vec, vec, vec, mat, mat],
        out_shape=[jax.ShapeDtypeStruct((S, D_MODEL), BF16), jax.ShapeDtypeStruct((S, D_MODEL), BF16),
                   jax.ShapeDtypeStruct((CONV_WIDTH, D_MODEL), F32),
                   vec_out, vec_out, vec_out, vec_out, mat_out, mat_out],
        compiler_params=_params("parallel"),
    )(dy, u, u, hs, conv_w, conv_b, wr, br, wi, bi, lam)


def _head_group_matrix(value):
    r = lax.broadcasted_iota(jnp.int32, (LANES, LANES), 0) // HEAD_DIM
    c = lax.broadcasted_iota(jnp.int32, (LANES, LANES), 1) // HEAD_DIM
    return jnp.where(r == c, value, 0.0).astype(BF16)


def _group_dot(x, p):
    hi = x.astype(BF16)
    lo = (x - hi.astype(F32)).astype(BF16)
    return _dot(hi, p) + _dot(lo, p)


def _head_mean(x, p):
    return _group_dot(x, p)


def _qk_prep(u, q_gain, k_gain, name, tm=256):
    S = u.shape[0]
    tm = _token_tile(S, tm)

    def body(q_ref, k_ref, v_ref, qg_ref, kg_ref, qn_ref, kn_ref, vb_ref):
        p = _head_group_matrix(1.0 / HEAD_DIM)
        for j in range(N_CBLK):
            cl = slice(j * LANES, (j + 1) * LANES)
            for x_ref, g_ref, o_ref, scale in ((q_ref, qg_ref, qn_ref, ATTN_SCALE),
                                               (k_ref, kg_ref, kn_ref, 1.0)):
                xv = x_ref[:, cl]
                rs = lax.rsqrt(_head_mean(xv * xv, p) + EPS)
                o_ref[:, cl] = (xv * rs * g_ref[...]).astype(BF16) * scale
        vb_ref[...] = v_ref[...].astype(BF16)

    blk = lambda off: pl.BlockSpec((tm, D_MODEL), lambda i: (i, off))
    out = jax.ShapeDtypeStruct((S, D_MODEL), BF16)
    return pl.pallas_call(
        body, name=name, grid=(S // tm,),
        in_specs=[blk(0), blk(1), blk(2), _resident((1, LANES)), _resident((1, LANES))],
        out_specs=[blk(0), blk(0), blk(0)],
        out_shape=[out, out, out],
        compiler_params=_params("parallel"),
    )(u, u, u, q_gain, k_gain)


def _qk_bwd(u, dqn, dkn, q_gain, k_gain, name, tm=256):
    S = u.shape[0]
    tm = _token_tile(S, tm)

    def body(q_ref, k_ref, dqn_ref, dkn_ref, qg_ref, kg_ref, dq_ref, dk_ref, dqg_ref, dkg_ref):
        p = _head_group_matrix(1.0 / HEAD_DIM)
        for x_ref, dn_ref, g_ref, dx_ref, dg_ref, scale in (
                (q_ref, dqn_ref, qg_ref, dq_ref, dqg_ref, ATTN_SCALE),
                (k_ref, dkn_ref, kg_ref, dk_ref, dkg_ref, 1.0)):
            dg = jnp.zeros((1, LANES), F32)
            for j in range(N_CBLK):
                cl = slice(j * LANES, (j + 1) * LANES)
                xv, dn = x_ref[:, cl], dn_ref[:, cl] * scale
                rs = lax.rsqrt(_head_mean(xv * xv, p) + EPS)
                xhat = xv * rs
                dxhat = dn * g_ref[...]
                dx_ref[:, cl] = (rs * (dxhat - xhat * _head_mean(dxhat * xhat, p))).astype(BF16)
                dg = dg + jnp.sum(dn * xhat, axis=0, keepdims=True)

            @pl.when(pl.program_id(0) == 0)
            def _():
                dg_ref[...] = jnp.zeros_like(dg_ref)
            dg_ref[...] += dg

            @pl.when(pl.program_id(0) == S // tm - 1)
            def _():
                dg_ref[...] += pltpu.roll(dg_ref[...], HEAD_DIM, 1)

    blk = lambda off: pl.BlockSpec((tm, D_MODEL), lambda i: (i, off))
    acc = pl.BlockSpec((1, LANES), lambda i: (0, 0))
    out = jax.ShapeDtypeStruct((S, D_MODEL), BF16)
    vec = jax.ShapeDtypeStruct((1, LANES), F32)
    return pl.pallas_call(
        body, name=name, grid=(S // tm,),
        in_specs=[blk(0), blk(1), blk(0), blk(0), _resident((1, LANES)), _resident((1, LANES))],
        out_specs=[blk(0), blk(0), acc, acc],
        out_shape=[out, out, vec, vec],
        compiler_params=_params("arbitrary"),
    )(u, u, dqn, dkn, q_gain, k_gain)


def _forget_fwd(f, b_f, name):
    S = f.shape[0]
    T = _token_tile(S, 256)

    def body(f_ref, b_ref, c_ref):
        row = lax.broadcasted_iota(jnp.int32, (T, LANES), 0)
        ones = jnp.ones((T, LANES), F32)
        bias = b_ref[...]

        def chunk(ci, carry):
            rows = pl.ds(pl.multiple_of(ci * T, T), T)
            _, c = _scan_chunk(ones, _log_sigmoid(f_ref[rows, :] + bias), row, T, reverse=False)
            c = c + carry
            c_ref[rows, :] = c
            return _row_of(c, row, T - 1)

        lax.fori_loop(0, S // T, chunk, jnp.zeros((1, LANES), F32))

    return pl.pallas_call(
        body, name=name,
        in_specs=[pl.BlockSpec(memory_space=pltpu.VMEM)] * 2,
        out_specs=pl.BlockSpec(memory_space=pltpu.VMEM),
        out_shape=jax.ShapeDtypeStruct((S, LANES), F32),
        compiler_params=pltpu.CompilerParams(vmem_limit_bytes=VMEM_LIMIT),
    )(f, b_f)


def _forget_bwd(dc_k, rho, f, b_f, name):
    S = f.shape[0]
    T = _token_tile(S, 256)
    n_chunk = S // T

    def body(dck_ref, rho_ref, f_ref, b_ref, df_ref, db_ref):
        row = lax.broadcasted_iota(jnp.int32, (T, LANES), 0)
        ones = jnp.ones((T, LANES), F32)
        bias = b_ref[...]
        pick = (lax.broadcasted_iota(jnp.int32, (D_MODEL, LANES), 0)
                == HEAD_DIM * lax.broadcasted_iota(jnp.int32, (D_MODEL, LANES), 1))
        pick = jnp.where(pick, 1.0, 0.0).astype(BF16)

        def chunk(it, carry):
            tail, db = carry
            rows = pl.ds(pl.multiple_of((n_chunk - 1 - it) * T, T), T)
            dc = dck_ref[rows, :] + _group_dot(rho_ref[rows, :], pick)
            _, dlf = _scan_chunk(ones, dc, row, T, reverse=True)
            dlf = dlf + tail
            df = dlf * _sigmoid(-(f_ref[rows, :] + bias))
            df_ref[rows, :] = df
            return _row_of(dlf, row, 0), db + jnp.sum(df, axis=0, keepdims=True)

        zero = jnp.zeros((1, LANES), F32)
        _, db = lax.fori_loop(0, n_chunk, chunk, (zero, zero))
        db_ref[...] = db

    return pl.pallas_call(
        body, name=name,
        in_specs=[pl.BlockSpec(memory_space=pltpu.VMEM)] * 4,
        out_specs=[pl.BlockSpec(memory_space=pltpu.VMEM)] * 2,
        out_shape=[jax.ShapeDtypeStruct((S, LANES), F32), jax.ShapeDtypeStruct((1, LANES), F32)],
        compiler_params=pltpu.CompilerParams(vmem_limit_bytes=VMEM_LIMIT),
    )(dc_k, rho, f, b_f)


ATTN_TILE = 512
ATTN_ROWS_FWD = 32
ATTN_ROWS_BWD = 32


def _attn_tiles(S):
    t = _token_tile(S, ATTN_TILE)
    return t, S // t


def _causal(T):
    return (lax.broadcasted_iota(jnp.int32, (T, T), 1)
            <= lax.broadcasted_iota(jnp.int32, (T, T), 0))


def _attn_fwd(qs_, kn, vb, c_row, name):
    S = qs_.shape[0]
    T, n_t = _attn_tiles(S)
    RB = min(T, ATTN_ROWS_FWD)

    def body(q_ref, k_ref, v_ref, cr_ref, o_ref, lse_ref, s_ref, p_ref, m_ref, l_ref, acc_ref,
             a_ref):
        qi = pl.program_id(1)
        lanes = [slice(h2 * HEAD_DIM, (h2 + 1) * HEAD_DIM) for h2 in range(2)]
        col = lax.broadcasted_iota(jnp.int32, (RB, T), 1)
        row = lax.broadcasted_iota(jnp.int32, (RB, T), 0)
        m_ref[...] = jnp.full(m_ref.shape, NEG_INF, F32)
        l_ref[...] = jnp.zeros_like(l_ref)
        acc_ref[...] = jnp.zeros_like(acc_ref)

        def step(kj, masked):
            ks = pl.ds(pl.multiple_of(kj * T, T), T)
            for h2, hl in enumerate(lanes):
                s_ref[h2] = _dot_nt(q_ref[:, hl], k_ref[ks, hl]) - cr_ref[h2:h2 + 1, ks]
            for h2, hl in enumerate(lanes):
                blocks = [slice(i * RB, (i + 1) * RB) for i in range(T // RB)]

                def logits(i, rows):
                    s = s_ref[h2, rows, :]
                    return jnp.where(col <= row + i * RB, s, NEG_INF) if masked else s

                wide = lambda x: jnp.broadcast_to(x, (RB, LANES))
                for i, rows in enumerate(blocks):
                    mx = wide(jnp.max(logits(i, rows), axis=1, keepdims=True))
                    a_ref[h2, rows, :] = m_ref[h2, rows, :]
                    m_ref[h2, rows, :] = jnp.maximum(m_ref[h2, rows, :], mx)
                for i, rows in enumerate(blocks):
                    m_new = m_ref[h2, rows, :]
                    p = jnp.exp(logits(i, rows) - jnp.tile(m_new, (1, T // LANES)))
                    alpha = jnp.exp(a_ref[h2, rows, :] - m_new)
                    a_ref[h2, rows, :] = alpha
                    l_ref[h2, rows, :] = (alpha * l_ref[h2, rows, :]
                                          + wide(jnp.sum(p, axis=1, keepdims=True)))
                    p_ref[h2, rows, :] = p.astype(BF16)
                acc_ref[h2] = (a_ref[h2, :, :HEAD_DIM] * acc_ref[h2]
                               + _dot(p_ref[h2], v_ref[ks, hl]))

        def unmasked(kj, _):
            step(kj, False)
            return 0

        lax.fori_loop(0, qi, unmasked, 0)
        step(qi, True)
        for h2, hl in enumerate(lanes):
            o_ref[:, hl] = (acc_ref[h2] / l_ref[h2, :, :HEAD_DIM]).astype(BF16)
            lse_ref[:, hl] = m_ref[h2, :, :HEAD_DIM] + jnp.log(l_ref[h2, :, :HEAD_DIM])

    qblk = pl.BlockSpec((T, LANES), lambda h, i: (i, h))
    kv = pl.BlockSpec((S, LANES), lambda h, i: (0, h))
    return pl.pallas_call(
        body, name=name, grid=(N_CBLK, n_t),
        in_specs=[qblk, kv, kv, pl.BlockSpec((None, 2, S), lambda h, i: (h, 0, 0))],
        out_specs=[qblk, qblk],
        out_shape=[jax.ShapeDtypeStruct((S, D_MODEL), BF16),
                   jax.ShapeDtypeStruct((S, D_MODEL), F32)],
        scratch_shapes=[pltpu.VMEM((2, T, T), F32), pltpu.VMEM((2, T, T), BF16),
                        pltpu.VMEM((2, T, LANES), F32), pltpu.VMEM((2, T, LANES), F32),
                        pltpu.VMEM((2, T, HEAD_DIM), F32), pltpu.VMEM((2, T, LANES), F32)],
        compiler_params=_params("parallel", "parallel"),
    )(qs_, kn, vb, c_row)


def _attn_bwd(qs_, kn, vb, do, o, lse, c_row, name):
    S = qs_.shape[0]
    T, n_t = _attn_tiles(S)

    def body(q_ref, k_ref, v_ref, do_ref, o_ref, lse_ref, cr_ref,
             dq_ref, dk_ref, dv_ref, dc_ref, rho_ref, dd_ref):
        kj = pl.program_id(1)
        causal = _causal(T)
        lanes = [slice(h2 * HEAD_DIM, (h2 + 1) * HEAD_DIM) for h2 in range(2)]
        ones = [slice(h2 * HEAD_DIM, h2 * HEAD_DIM + 1) for h2 in range(2)]

        @pl.when(kj == 0)
        def _():
            dq_ref[...] = jnp.zeros_like(dq_ref)
            rho_ref[...] = jnp.zeros_like(rho_ref)
            p_sum = _head_group_matrix(1.0)

            def fill(ci, _):
                rows = pl.ds(pl.multiple_of(ci * T, T), T)
                dd_ref[rows, :] = _group_dot(do_ref[rows, :].astype(F32) * o_ref[rows, :].astype(F32),
                                             p_sum)
                return 0

            lax.fori_loop(0, n_t, fill, 0)

        kh = [k_ref[:, hl] for hl in lanes]
        vh = [v_ref[:, hl] for hl in lanes]
        ck = [cr_ref[h2:h2 + 1, :] for h2 in range(2)]

        def step(qi, carry, masked):
            qs = pl.ds(pl.multiple_of(qi * T, T), T)
            out = []
            for h2, hl in enumerate(lanes):
                dk, dv, dc = carry[h2]
                qh, doh = q_ref[qs, hl], do_ref[qs, hl]
                s = _dot_nt(qh, kh[h2]) - ck[h2]
                if masked:
                    s = jnp.where(causal, s, NEG_INF)
                p = jnp.exp(s - lse_ref[qs, ones[h2]])
                ds = p * (_dot_nt(doh, vh[h2]) - dd_ref[qs, ones[h2]])
                dsb = ds.astype(BF16)
                dq_ref[qs, hl] += _dot(dsb, kh[h2])
                rho_ref[qs, hl] += jnp.broadcast_to(jnp.sum(ds, axis=1, keepdims=True),
                                                    (T, HEAD_DIM))
                out.append((dk + _dot_tn(dsb, qh), dv + _dot_tn(p.astype(BF16), doh),
                            dc - jnp.sum(ds, axis=0, keepdims=True)))
            return tuple(out)

        init = tuple((jnp.zeros((T, HEAD_DIM), F32), jnp.zeros((T, HEAD_DIM), F32),
                      jnp.zeros((1, T), F32)) for _ in lanes)
        carry = step(kj, init, True)
        carry = lax.fori_loop(kj + 1, n_t, lambda qi, c: step(qi, c, False), carry)
        for h2, ((dk, dv, dc), hl) in enumerate(zip(carry, lanes)):
            dk_ref[:, hl] = dk
            dv_ref[:, hl] = dv.astype(BF16)
            dc_ref[h2:h2 + 1, :] = dc

    kblk = pl.BlockSpec((T, LANES), lambda h, j: (j, h))
    full = pl.BlockSpec((S, LANES), lambda h, j: (0, h))
    crow = pl.BlockSpec((None, 2, T), lambda h, j: (h, 0, j))
    wide = jax.ShapeDtypeStruct((S, D_MODEL), F32)
    return pl.pallas_call(
        body, name=name, grid=(N_CBLK, n_t),
        in_specs=[full, kblk, kblk, full, full, full, crow],
        out_specs=[full, kblk, kblk, crow, full],
        out_shape=[wide, wide, jax.ShapeDtypeStruct((S, D_MODEL), BF16),
                   jax.ShapeDtypeStruct((N_CBLK, 2, S), F32), wide],
        scratch_shapes=[pltpu.VMEM((S, LANES), F32)],
        compiler_params=_params("parallel", "arbitrary"),
    )(qs_, kn, vb, do, o, lse, c_row)


ALL_PEERS = tuple(range(1, N_DEV))
NEAR_PEERS = (1, 2, 4, 6)
FAR_CHIPS = (2, 4, 6)


def _landing_shapes(arrays, gathers):
    return [jax.ShapeDtypeStruct((N_DEV,) + a.shape if g else a.shape, a.dtype)
            for a, g in zip(arrays, gathers)]


def _my_index():
    return 4 * lax.axis_index("x") + 2 * lax.axis_index("y") + lax.axis_index("c")


def _own_copies(srcs, lands, gathers, sems):
    me = _my_index()
    return [pltpu.make_async_copy(src if g else src.at[me], land.at[me], sems.at[a])
            for a, (src, land, g) in enumerate(zip(srcs, lands, gathers))]


def _peer_copies(srcs, lands, gathers, send_sems, recv_sems, ks=ALL_PEERS):
    x, y, c = lax.axis_index("x"), lax.axis_index("y"), lax.axis_index("c")
    me = 4 * x + 2 * y + c
    out = []
    for j, k in enumerate(ks):
        to = (1 - x if k & 4 else x, 1 - y if k & 2 else y, 1 - c if k & 1 else c)
        peer = 4 * to[0] + 2 * to[1] + to[2]
        for a, (src, land, g) in enumerate(zip(srcs, lands, gathers)):
            sem = a * len(ks) + j
            src_blk = src if g else src.at[peer]

            def copy(slot, src_blk=src_blk, land=land, sem=sem, to=to):
                return pltpu.make_async_remote_copy(
                    src_ref=src_blk, dst_ref=land.at[slot], send_sem=send_sems.at[sem],
                    recv_sem=recv_sems.at[sem], device_id=to,
                    device_id_type=pl.DeviceIdType.MESH)

            out.append((k, a, copy(me), copy(peer)))
    return out


def _forward_copies(lands, send_sems, recv_sems):
    x, y, c = lax.axis_index("x"), lax.axis_index("y"), lax.axis_index("c")
    out = []
    for j, f in enumerate(FAR_CHIPS):
        chip = 4 * (1 - x if f & 4 else x) + 2 * (1 - y if f & 2 else y)
        for a, land in enumerate(lands):
            sem = a * len(FAR_CHIPS) + j

            def copy(slot, land=land, sem=sem):
                return pltpu.make_async_remote_copy(
                    src_ref=land.at[slot], dst_ref=land.at[slot], send_sem=send_sems.at[sem],
                    recv_sem=recv_sems.at[sem], device_id=(x, y, 1 - c),
                    device_id_type=pl.DeviceIdType.MESH)

            out.append((f, a, copy(chip + c), copy(chip + 1 - c)))
    return out


def _exchange(arrays, gathers, name, two_level=False):
    n = len(arrays)
    ks = NEAR_PEERS if two_level else ALL_PEERS
    assert not two_level or all(gathers)

    def body(*refs):
        ins, outs = refs[:n], refs[n:2 * n]
        send_sems, recv_sems, own_sems, fwd_send_sems, fwd_recv_sems = refs[2 * n:]
        own = _own_copies(ins, outs, gathers, own_sems)
        for cp in own:
            cp.start()
        copies = _peer_copies(ins, outs, gathers, send_sems, recv_sems, ks)
        for _, _, send, _ in copies:
            send.start()
        passed = {}
        if two_level:
            passed = {(f, a): (send, arrival)
                      for f, a, send, arrival in _forward_copies(outs, fwd_send_sems, fwd_recv_sems)}
        for k, a, _, arrival in copies:
            arrival.wait_recv()
            if (k, a) in passed:
                passed[k, a][0].start()
        for send, arrival in passed.values():
            arrival.wait_recv()
            send.wait_send()
        for _, _, send, _ in copies:
            send.wait_send()
        for cp in own:
            cp.wait()

    hbm = pl.BlockSpec(memory_space=pl.ANY)
    return pl.pallas_call(
        body, name=name,
        in_specs=[hbm] * n, out_specs=[hbm] * n, out_shape=_landing_shapes(arrays, gathers),
        scratch_shapes=[pltpu.SemaphoreType.DMA((n * len(ks),)),
                        pltpu.SemaphoreType.DMA((n * len(ks),)),
                        pltpu.SemaphoreType.DMA((n,)),
                        pltpu.SemaphoreType.DMA((n * len(FAR_CHIPS),)),
                        pltpu.SemaphoreType.DMA((n * len(FAR_CHIPS),))],
        compiler_params=pltpu.CompilerParams(has_side_effects=True),
    )(*arrays)


_HBM = pl.BlockSpec(memory_space=pltpu.HBM)
_SEM = pl.BlockSpec(memory_space=pltpu.SEMAPHORE)
_ANY = pl.BlockSpec(memory_space=pl.ANY)
_DATAFLOW = pltpu.SideEffectType.DATAFLOW_SIDE_EFFECTING


def _in_hbm(a):
    return pltpu.with_memory_space_constraint(a, pltpu.HBM)


def _exchange_start(arrays, gathers, after, name, ks=ALL_PEERS):
    n = len(arrays)
    lands = [lax.empty(s.shape, s.dtype) for s in _landing_shapes(arrays, gathers)]

    def body(*refs):
        srcs, dsts = refs[:n], refs[n:2 * n]
        send_sems, recv_sems, own_sems = refs[2 * n + 1:2 * n + 4]
        token = refs[-1]
        for cp in _own_copies(srcs, dsts, gathers, own_sems):
            cp.start()
        for _, _, send, _ in _peer_copies(srcs, dsts, gathers, send_sems, recv_sems, ks):
            send.start()
        token[...] = jnp.zeros_like(token)

    hbm_like = [pltpu.HBM(a.shape, a.dtype) for a in list(arrays) + lands]
    res = pl.pallas_call(
        body, name=name,
        in_specs=[_HBM] * (2 * n) + [_ANY],
        out_specs=(_SEM, _SEM, _SEM, *[_HBM] * (2 * n), pl.BlockSpec(memory_space=pltpu.VMEM)),
        out_shape=(pltpu.SemaphoreType.DMA((n * len(ks),)), pltpu.SemaphoreType.DMA((n * len(ks),)),
                   pltpu.SemaphoreType.DMA((n,)), *hbm_like,
                   jax.ShapeDtypeStruct((8, LANES), F32)),
        input_output_aliases={i: 3 + i for i in range(2 * n)},
        compiler_params=pltpu.CompilerParams(has_side_effects=_DATAFLOW),
    )(*[_in_hbm(a) for a in list(arrays) + lands], after)
    return (res[0], res[1], res[2], res[3:3 + n], res[3 + n:3 + 2 * n]), res[-1]


def _exchange_wait(started, gathers, after, name, ks=ALL_PEERS):
    send_sems, recv_sems, own_sems, arrays, lands = started
    n = len(arrays)

    def body(*refs):
        srcs, dsts = refs[:n], refs[n:2 * n]
        for _, _, send, arrival in _peer_copies(srcs, dsts, gathers, refs[2 * n], refs[2 * n + 1],
                                                ks):
            arrival.wait_recv()
            send.wait_send()
        for cp in _own_copies(srcs, dsts, gathers, refs[2 * n + 2]):
            cp.wait()

    hbm_like = [pltpu.HBM(a.shape, a.dtype) for a in list(arrays) + list(lands)]
    res = pl.pallas_call(
        body, name=name,
        in_specs=[_HBM] * (2 * n) + [_SEM, _SEM, _SEM, _ANY],
        out_specs=[_HBM] * (2 * n), out_shape=hbm_like,
        input_output_aliases={i: i for i in range(2 * n)},
        compiler_params=pltpu.CompilerParams(has_side_effects=_DATAFLOW),
    )(*arrays, *lands, send_sems, recv_sems, own_sems, after)
    return res[n:]


def _forward_start(lands, after, name):
    n = len(lands)

    def body(*refs):
        send_sems, recv_sems = refs[n + 1:n + 3]
        for _, _, send, _ in _forward_copies(refs[:n], send_sems, recv_sems):
            send.start()
        refs[-1][...] = jnp.zeros_like(refs[-1])

    n_sem = n * len(FAR_CHIPS)
    res = pl.pallas_call(
        body, name=name,
        in_specs=[_HBM] * n + [_ANY],
        out_specs=(_SEM, _SEM, *[_HBM] * n, pl.BlockSpec(memory_space=pltpu.VMEM)),
        out_shape=(pltpu.SemaphoreType.DMA((n_sem,)), pltpu.SemaphoreType.DMA((n_sem,)),
                   *[pltpu.HBM(a.shape, a.dtype) for a in lands],
                   jax.ShapeDtypeStruct((8, LANES), F32)),
        input_output_aliases={i: 2 + i for i in range(n)},
        compiler_params=pltpu.CompilerParams(has_side_effects=_DATAFLOW),
    )(*[_in_hbm(a) for a in lands], after)
    return (res[0], res[1], res[2:2 + n]), res[-1]


def _forward_wait(started, after, name):
    send_sems, recv_sems, lands = started
    n = len(lands)

    def body(*refs):
        for _, _, send, arrival in _forward_copies(refs[:n], refs[n], refs[n + 1]):
            arrival.wait_recv()
            send.wait_send()

    return pl.pallas_call(
        body, name=name,
        in_specs=[_HBM] * n + [_SEM, _SEM, _ANY],
        out_specs=[_HBM] * n, out_shape=[pltpu.HBM(a.shape, a.dtype) for a in lands],
        input_output_aliases={i: i for i in range(n)},
        compiler_params=pltpu.CompilerParams(has_side_effects=_DATAFLOW),
    )(*lands, send_sems, recv_sems, after)


def _reduce_adamw(parts, w, m, v, name):
    n_layer = len(parts)
    n, R, C = parts[0].shape
    tr = 256 if R % 256 == 0 else R
    n_t = R // tr

    def body(*refs):
        p_refs = refs[:n_layer]
        w_ref, m_ref, v_ref, g_ref, d_ref, nm_ref, nv_ref = refs[n_layer:]

        def update(p_ref):
            g = p_ref[0].astype(F32)
            for s in range(1, n):
                g = g + p_ref[s].astype(F32)
            g_ref[...] = g
            m_new = ADAM_B1 * m_ref[...] + (1.0 - ADAM_B1) * g
            v_new = ADAM_B2 * v_ref[...] + (1.0 - ADAM_B2) * (g * g)
            nm_ref[...] = m_new
            nv_ref[...] = v_new
            m_hat = m_new / (1.0 - ADAM_B1 ** ADAM_STEP)
            v_hat = v_new / (1.0 - ADAM_B2 ** ADAM_STEP)
            d_ref[...] = -ADAM_LR * (m_hat / (jnp.sqrt(v_hat) + ADAM_EPS) + ADAM_WD * w_ref[...])

        for layer, p_ref in enumerate(p_refs):
            pl.when(pl.program_id(0) == layer)(functools.partial(update, p_ref))

    def parts_spec(layer):
        def index(l, i):
            return 0, jnp.where(l < layer, 0, jnp.where(l > layer, n_t - 1, i)), 0
        return pl.BlockSpec((n, tr, C), index)

    blk = pl.BlockSpec((None, tr, C), lambda l, i: (l, i, 0))
    out = jax.ShapeDtypeStruct((n_layer, R, C), F32)
    return pl.pallas_call(
        body, name=name, grid=(n_layer, n_t),
        in_specs=[parts_spec(layer) for layer in range(n_layer)] + [blk, blk, blk],
        out_specs=[blk] * 4, out_shape=[out] * 4,
        compiler_params=_params("arbitrary", "arbitrary"),
    )(*parts, w, m, v)


def _pack(arrays):
    flat = jnp.concatenate([a.reshape(-1).astype(F32) for a in arrays])
    pad = (-flat.shape[0]) % (8 * LANES)
    return jnp.pad(flat, (0, pad)).reshape(-1, LANES)


def _unpack(buf, shapes):
    flat = buf.reshape(-1)
    out, off = [], 0
    for shp in shapes:
        size = 1
        for s in shp:
            size *= s
        out.append(flat[off:off + size].reshape(shp))
        off += size
    return out


def _block_diag_pairs(w):
    w = w.reshape(N_CBLK, 2, LRU_BLOCK_DIM, LRU_BLOCK_DIM)
    z = jnp.zeros_like(w[:, 0])
    top = jnp.concatenate([w[:, 0], z], axis=2)
    bot = jnp.concatenate([z, w[:, 1]], axis=2)
    return jnp.concatenate([top, bot], axis=1)


def _diag_pairs(m):
    h = LRU_BLOCK_DIM
    return jnp.stack([m[:, :h, :h], m[:, h:, h:]], axis=1).reshape(2 * N_CBLK, h, h)


SMALL = ("mlp_norm", "lru_conv_b", "lru_w_r", "lru_b_r", "lru_w_i", "lru_b_i",
         "lru_lambda", "fox_b_f", "fox_q_gain", "fox_k_gain")
WEIGHTS = ("mix_norm", "mlp_norm", "mlp_w1", "mlp_w2", "lru_w_in", "lru_conv_w", "lru_conv_b",
           "lru_w_r", "lru_b_r", "lru_w_i", "lru_b_i", "lru_lambda", "lru_w_out", "fox_w_in",
           "fox_b_f", "fox_q_gain", "fox_k_gain", "fox_w_out")


def kernel(x, mix_norm, mlp_norm, mlp_w1, mlp_w2, lru_w_in, lru_conv_w, lru_conv_b, lru_w_r, lru_b_r, lru_w_i, lru_b_i, lru_lambda, lru_w_out, fox_w_in, fox_b_f, fox_q_gain, fox_k_gain, fox_w_out, loss_target, m_mix_norm, m_mlp_norm, m_mlp_w1, m_mlp_w2, m_lru_w_in, m_lru_conv_w, m_lru_conv_b, m_lru_w_r, m_lru_b_r, m_lru_w_i, m_lru_b_i, m_lru_lambda, m_lru_w_out, m_fox_w_in, m_fox_b_f, m_fox_q_gain, m_fox_k_gain, m_fox_w_out, v_mix_norm, v_mlp_norm, v_mlp_w1, v_mlp_w2, v_lru_w_in, v_lru_conv_w, v_lru_conv_b, v_lru_w_r, v_lru_b_r, v_lru_w_i, v_lru_b_i, v_lru_lambda, v_lru_w_out, v_fox_w_in, v_fox_b_f, v_fox_q_gain, v_fox_k_gain, v_fox_w_out):
    w_in = dict(mix_norm=mix_norm, mlp_norm=mlp_norm, mlp_w1=mlp_w1, mlp_w2=mlp_w2,
                lru_w_in=lru_w_in, lru_conv_w=lru_conv_w, lru_conv_b=lru_conv_b, lru_w_r=lru_w_r,
                lru_b_r=lru_b_r, lru_w_i=lru_w_i, lru_b_i=lru_b_i, lru_lambda=lru_lambda,
                lru_w_out=lru_w_out, fox_w_in=fox_w_in, fox_b_f=fox_b_f, fox_q_gain=fox_q_gain,
                fox_k_gain=fox_k_gain, fox_w_out=fox_w_out)
    m_in = dict(mix_norm=m_mix_norm, mlp_norm=m_mlp_norm, mlp_w1=m_mlp_w1, mlp_w2=m_mlp_w2,
                lru_w_in=m_lru_w_in, lru_conv_w=m_lru_conv_w, lru_conv_b=m_lru_conv_b,
                lru_w_r=m_lru_w_r, lru_b_r=m_lru_b_r, lru_w_i=m_lru_w_i, lru_b_i=m_lru_b_i,
                lru_lambda=m_lru_lambda, lru_w_out=m_lru_w_out, fox_w_in=m_fox_w_in,
                fox_b_f=m_fox_b_f, fox_q_gain=m_fox_q_gain, fox_k_gain=m_fox_k_gain,
                fox_w_out=m_fox_w_out)
    v_in = dict(mix_norm=v_mix_norm, mlp_norm=v_mlp_norm, mlp_w1=v_mlp_w1, mlp_w2=v_mlp_w2,
                lru_w_in=v_lru_w_in, lru_conv_w=v_lru_conv_w, lru_conv_b=v_lru_conv_b,
                lru_w_r=v_lru_w_r, lru_b_r=v_lru_b_r, lru_w_i=v_lru_w_i, lru_b_i=v_lru_b_i,
                lru_lambda=v_lru_lambda, lru_w_out=v_lru_w_out, fox_w_in=v_fox_w_in,
                fox_b_f=v_fox_b_f, fox_q_gain=v_fox_q_gain, fox_k_gain=v_fox_k_gain,
                fox_w_out=v_fox_w_out)
    D = D_MODEL
    S = x.shape[1]
    x0, target = x[0], loss_target[0]
    me = 4 * lax.axis_index("x") + 2 * lax.axis_index("y") + lax.axis_index("c")

    def bf16(a):
        return a.astype(BF16)

    (lru_in_g,) = _exchange([bf16(lru_w_in[0])], [True], "gather_lru_in", two_level=True)
    gather_lru, tok = _exchange_start([bf16(lru_w_out[0]), lru_conv_w[0]], [True] * 2, lru_in_g,
                                      "gather_lru_start")
    gather_mlp0, tok = _exchange_start([bf16(mlp_w1[0]), bf16(mlp_w2[0])], [True] * 2, tok,
                                       "gather_mlp0_start", NEAR_PEERS)
    gather_fox, tok = _exchange_start([bf16(fox_w_in[0]), bf16(fox_w_out[0])], [True] * 2, tok,
                                      "gather_fox_start")
    gather_mlp1, tok = _exchange_start([bf16(mlp_w1[1]), bf16(mlp_w2[1])], [True] * 2, tok,
                                       "gather_mlp1_start", NEAR_PEERS)

    def pass_on(started, after, name):
        lands = _exchange_wait(started, [True] * 2, after, name + "_wait", NEAR_PEERS)
        return _forward_start(lands, after, name + "_pass_start")
    wr =_block_diag_pairs(lru_w_r[0]).astype(BF16)
    wi = _block_diag_pairs(lru_w_i[0]).astype(BF16)
    b_r, b_i = lru_b_r.reshape(1, D), lru_b_i.reshape(1, D)
    q_gain, k_gain = jnp.tile(fox_q_gain, (1, 2)), jnp.tile(fox_k_gain, (1, 2))
    b_f = jnp.pad(fox_b_f, ((0, 0), (0, LANES - N_HEADS)))
    g_mix0, g_mix1 = mix_norm[0:1] + tok[0, 0], mix_norm[1:2]
    g_mlp0, g_mlp1 = mlp_norm[0:1], mlp_norm[1:2]

    (u0,), h0 = _norm_matmul(x0, g_mix0, [lru_in_g], "lru_in_proj")
    lru_out_g, conv_g = _exchange_wait(gather_lru, [True] * 2, u0, "gather_lru_wait")
    lru_out_w = lru_out_g.reshape(D, D)
    conv_w = conv_g.transpose(1, 0, 2).reshape(CONV_WIDTH, D)
    y_lru, hs =_lru_fwd(u0, conv_w, lru_conv_b, wr, b_r, wi, b_i, lru_lambda, "lru_core")
    pass_mlp0, tok = pass_on(gather_mlp0, y_lru, "gather_mlp0")
    x1 = _matmul_res(y_lru, lru_out_w, x0, "lru_out_proj", tok)
    w1g0, w2g0 = _forward_wait(pass_mlp0, x1, "gather_mlp0_pass_wait")
    x2, h1, r1 = _mlp_fwd(x1, g_mlp0, w1g0, w2g0, "mlp0")
    fox_in_g, fox_out_g = _exchange_wait(gather_fox, [True] * 2, x2, "gather_fox_wait")
    fox_out_w = fox_out_g.reshape(D, D)
    fox_full = fox_in_g.transpose(1, 0, 2).reshape(D, 3 * D + N_HEADS)
    wqkv = fox_full[:, :3 * D].reshape(D, 3, D).transpose(1, 0, 2)
    wf = jnp.pad(fox_full[:, 3 * D:], ((0, 0), (0, LANES - N_HEADS)))[None]
    (u_qkv, f), h2 = _norm_matmul(x2, g_mix1, [wqkv, wf], "fox_in_proj")
    qn, kn, vb = _qk_prep(u_qkv, q_gain, k_gain, "fox_qk_norm")
    c_col = _forget_fwd(f, b_f, "fox_forget")
    c_row = c_col[:, :N_HEADS].T.reshape(N_CBLK, 2, S)
    o, lse = _attn_fwd(qn, kn, vb, c_row, "fox_attn")
    pass_mlp1, tok = pass_on(gather_mlp1, o, "gather_mlp1")
    x3 = _matmul_res(o, fox_out_w, x2, "fox_out_proj", tok)
    w1g1, w2g1 = _forward_wait(pass_mlp1, x3, "gather_mlp1_pass_wait")
    loss_local, dx4, h3, r3 = _mlp_fwd(x3, g_mlp1, w1g1, w2g1, "mlp1", target)

    dx3, dg_mlp1, da3 = _mlp_bwd(dx4, x3, g_mlp1, r3, w1g1, w2g1, "mlp1_bwd")
    dw1_1 = _matmul_tn(h3, da3, "mlp1_dw1", cols=2, col_blocks=N_DEV)
    dw2_1 = _matmul_tn(r3, dx4, "mlp1_dw2", rows=2, a_square=True).reshape(N_DEV, -1, D)
    grads_mlp1, tok = _exchange_start([dw1_1, dw2_1], [False] * 2, tok, "grads_mlp1_start")
    do = _matmul_nt(dx3, fox_out_w, "fox_out_bwd", BF16, tok)
    d_fox_out = _matmul_tn(o, dx3, "fox_out_dw").reshape(N_DEV, -1, D)
    dqn, dkn, dv, dc_row, rho = _attn_bwd(qn, kn, vb, do, o, lse, c_row, "fox_attn_bwd")
    duq, duk, dq_gain, dk_gain = _qk_bwd(u_qkv, dqn, dkn, q_gain, k_gain, "fox_qk_norm_bwd")
    dc_k = jnp.pad(dc_row.reshape(N_HEADS, S).T, ((0, 0), (0, LANES - N_HEADS)))
    df, db_f = _forget_bwd(dc_k, rho, f, b_f, "fox_forget_bwd")
    dx2, dg_mix1 = _proj_bwd([[duq, duk, dv], [df]], [wqkv, wf], x2, g_mix1, dx3, "fox_in_bwd")
    d_fox_in = jnp.concatenate(
        [_matmul_tn(h2, duq, "fox_in_dwq"), _matmul_tn(h2, duk, "fox_in_dwk"),
         _matmul_tn(h2, dv, "fox_in_dwv"), _matmul_tn(h2, df, "fox_in_dwf")[:, :N_HEADS]], axis=1)
    d_fox_in = d_fox_in.reshape(D, N_DEV, -1).transpose(1, 0, 2)
    grads_fox, tok = _exchange_start([d_fox_in, d_fox_out], [False] * 2, tok, "grads_fox_start")
    dx1, dg_mlp0, da1 = _mlp_bwd(dx2, x1, g_mlp0 + tok[0, 0], r1, w1g0, w2g0, "mlp0_bwd")
    dw1_0 = _matmul_tn(h1, da1, "mlp0_dw1", cols=2, col_blocks=N_DEV)
    dw2_0 = _matmul_tn(r1, dx2, "mlp0_dw2", rows=2, a_square=True).reshape(N_DEV, -1, D)
    grads_mlp0, tok = _exchange_start([dw1_0, dw2_0], [False] * 2, tok, "grads_mlp0_start")
    dy_lru = _matmul_nt(dx1, lru_out_w, "lru_out_bwd", F32, tok)
    d_lru_out = _matmul_tn(y_lru, dx1, "lru_out_dw").reshape(N_DEV, -1, D)
    dgp, dxb, d_conv_w, d_conv_b, d_b_r, d_b_i, d_lam, d_wr, d_wi = _lru_bwd(
        dy_lru, u0, hs, conv_w, lru_conv_b, wr, b_r, wi, b_i, lru_lambda, "lru_core_bwd")

    small_grads = dict(
        mlp_norm=jnp.concatenate([dg_mlp0, dg_mlp1], axis=0),
        lru_conv_b=d_conv_b, lru_w_r=_diag_pairs(d_wr), lru_b_r=d_b_r, lru_w_i=_diag_pairs(d_wi),
        lru_b_i=d_b_i, lru_lambda=d_lam, fox_b_f=db_f[:, :N_HEADS],
        fox_q_gain=dq_gain[:, :HEAD_DIM], fox_k_gain=dk_gain[:, :HEAD_DIM])
    small_partial = _pack([dg_mix1] + [small_grads[n] for n in SMALL] + [d_conv_w])
    grads_lru_out, tok = _exchange_start([d_lru_out, small_partial], [False, True], tok,
                                         "grads_lru_out_start")
    dx0, dg_mix0 = _proj_bwd([[dgp, dxb]], [lru_in_g], x0, mix_norm[0:1] + tok[0, 0], dx1,
                             "lru_in_bwd")
    d_lru_in = jnp.concatenate([_matmul_tn(h0, dgp, "lru_in_dw_gate", col_blocks=4),
                                _matmul_tn(h0, dxb, "lru_in_dw_x", col_blocks=4)], axis=0)
    grads_lru_in, tok = _exchange_start([d_lru_in, dg_mix0], [False, True], tok,
                                        "grads_lru_in_start")

    grads, deltas, new_m, new_v = {}, {}, {}, {}

    def update(name, parts):
        w, m, v = w_in[name], m_in[name], v_in[name]
        shape = w.shape
        stacked = (len(parts), -1, shape[-1])
        w3 = w.reshape(stacked)
        res = _reduce_adamw([p.reshape((N_DEV,) + w3.shape[1:]) for p in parts], w3,
                            m.reshape(stacked), v.reshape(stacked), "adamw_" + name)
        return [r.reshape(shape) for r in res]

    def store(name, res):
        grads[name], deltas[name], new_m[name], new_v[name] = res

    p_w1_1, p_w2_1 = _exchange_wait(grads_mlp1, [False] * 2, tok, "grads_mlp1_wait")
    p_fox_in, p_fox_out = _exchange_wait(grads_fox, [False] * 2, p_w1_1, "grads_fox_wait")
    store("fox_w_in", update("fox_w_in", [p_fox_in]))
    store("fox_w_out", update("fox_w_out", [p_fox_out]))
    p_w1_0, p_w2_0 = _exchange_wait(grads_mlp0, [False] * 2, grads["fox_w_out"], "grads_mlp0_wait")
    store("mlp_w1", update("mlp_w1", [p_w1_0, p_w1_1]))
    store("mlp_w2", update("mlp_w2", [p_w2_0, p_w2_1]))
    p_lru_out, p_small = _exchange_wait(grads_lru_out, [False, True], grads["mlp_w2"],
                                        "grads_lru_out_wait")
    store("lru_w_out", update("lru_w_out", [p_lru_out]))
    p_lru_in, p_mix0 = _exchange_wait(grads_lru_in, [False, True], grads["lru_w_out"],
                                      "grads_lru_in_wait")
    store("lru_w_in", update("lru_w_in", [p_lru_in]))

    mix0 = [r[0] for r in _reduce_adamw([p_mix0], mix_norm[None, 0:1], m_mix_norm[None, 0:1],
                                        v_mix_norm[None, 0:1], "adamw_mix0")]
    packed = lambda src, first: _pack([first] + [src[n] for n in SMALL]
                                      + [jnp.zeros((CONV_WIDTH, D))])[None]
    small_shapes = [(1, D)] + [w_in[n].shape for n in SMALL]
    n_small = sum(math.prod(s) for s in small_shapes)
    res_small = _reduce_adamw([p_small], packed(w_in, mix_norm[1:2]), packed(m_in, m_mix_norm[1:2]),
                              packed(v_in, v_mix_norm[1:2]), "adamw_small")
    for name, *vals in zip(("mix1",) + SMALL, *[_unpack(r, small_shapes) for r in res_small]):
        if name == "mix1":
            vals = [jnp.concatenate([r0, r1], axis=0) for r0, r1 in zip(mix0, vals)]
            name = "mix_norm"
        store(name, vals)
    conv_parts = p_small.reshape(N_DEV, -1)[:, n_small:n_small + CONV_WIDTH * D]
    conv_parts = conv_parts.reshape(N_DEV, CONV_WIDTH, N_DEV, LANES)
    conv_parts = lax.dynamic_index_in_dim(conv_parts, me, axis=2, keepdims=False)
    store("lru_conv_w", update("lru_conv_w", [conv_parts]))

    loss = lax.psum(loss_local[0, 0], ("x", "y", "c"))
    return (loss, dx0[None], *[grads[n] for n in WEIGHTS], *[deltas[n] for n in WEIGHTS],
            *[new_m[n] for n in WEIGHTS], *[new_v[n] for n in WEIGHTS])
```

```python
import functools
import math

import jax
import jax.numpy as jnp
from jax import lax
from jax.experimental import pallas as pl
from jax.experimental.pallas import tpu as pltpu

F32 = jnp.float32
BF16 = jnp.bfloat16

N_DEV = 8
D_MODEL = 1024
D_FF = 4096
N_HEADS = 16
HEAD_DIM = 64
LRU_BLOCK_DIM = 64
CONV_WIDTH = 4
LRU_C = 8.0
EPS = 1e-6
NEG_INF = -1e30
ATTN_SCALE = HEAD_DIM ** -0.5
LANES = 128
N_CBLK = D_MODEL // LANES
VMEM_LIMIT = 52 * 2 ** 20

ADAM_LR = 0.001
ADAM_B1 = 0.9
ADAM_B2 = 0.999
ADAM_EPS = 1e-08
ADAM_WD = 0.01
ADAM_STEP = 10

_NT = (((1,), (1,)), ((), ()))
_TN = (((0,), (0,)), ((), ()))


def _params(*sem):
    return pltpu.CompilerParams(dimension_semantics=sem, vmem_limit_bytes=VMEM_LIMIT)


def _resident(shape):
    zeros = (0,) * len(shape)
    return pl.BlockSpec(shape, lambda *_: zeros, pipeline_mode=pl.Buffered(1))


def _dot(a, b):
    return jnp.dot(a, b, preferred_element_type=F32)


def _dot_nt(a, b):
    return lax.dot_general(a, b, _NT, preferred_element_type=F32)


def _dot_tn(a, b):
    return lax.dot_general(a, b, _TN, preferred_element_type=F32)


def _sigmoid(x):
    return 1.0 / (1.0 + jnp.exp(-x))


def _log_sigmoid(x):
    return -(jnp.maximum(-x, 0.0) + jnp.log1p(jnp.exp(-jnp.abs(x))))


def _expm1(x):
    poly = x * (1.0 + x * (0.5 + x * (1.0 / 6.0 + x * (1.0 / 24.0 + x * (1.0 / 120.0)))))
    return jnp.where(jnp.abs(x) < 0.1, poly, jnp.exp(x) - 1.0)


_GELU_K = 0.7978845608028654


def _gelu(x):
    return 0.5 * x * (1.0 + jnp.tanh(_GELU_K * (x + 0.044715 * (x * x * x))))


def _gelu_grad(x):
    t = jnp.tanh(_GELU_K * (x + 0.044715 * (x * x * x)))
    return 0.5 * (1.0 + t) + 0.5 * x * (1.0 - t * t) * (_GELU_K * (1.0 + 3 * 0.044715 * x * x))


def _rms_scale(x):
    return lax.rsqrt(jnp.mean(x * x, axis=-1, keepdims=True) + EPS)


def _norm_bwd(dh, x, g):
    rs = _rms_scale(x)
    xhat = x * rs
    dxhat = dh * g
    dx = rs * (dxhat - xhat * jnp.mean(dxhat * xhat, axis=-1, keepdims=True))
    return dx, jnp.sum(dh * xhat, axis=0, keepdims=True)


def _token_tile(S, want):
    tm = min(S, want)
    assert S % tm == 0
    return tm


def _norm_matmul(x, g, ws, name, tm=256):
    S, D = x.shape
    tm = _token_tile(S, tm)
    n = len(ws)

    def body(x_ref, g_ref, *refs):
        w_refs, o_refs, h_ref = refs[:n], refs[n:2 * n], refs[2 * n]
        xv = x_ref[...]
        h = (xv * _rms_scale(xv) * g_ref[...]).astype(BF16)
        h_ref[...] = h
        for w_ref, o_ref in zip(w_refs, o_refs):
            nb, _, nw = w_ref.shape
            for d in range(nb):
                o_ref[:, d * nw:(d + 1) * nw] = _dot(h, w_ref[d])

    widths = [w.shape[0] * w.shape[2] for w in ws]
    outs = pl.pallas_call(
        body, name=name, grid=(S // tm,),
        in_specs=[pl.BlockSpec((tm, D), lambda i: (i, 0)), _resident((1, D))]
        + [_resident(w.shape) for w in ws],
        out_specs=[pl.BlockSpec((tm, n_), lambda i: (i, 0)) for n_ in widths]
        + [pl.BlockSpec((tm, D), lambda i: (i, 0))],
        out_shape=[jax.ShapeDtypeStruct((S, n_), F32) for n_ in widths]
        + [jax.ShapeDtypeStruct((S, D), BF16)],
        compiler_params=_params("parallel"),
    )(x, g, *ws)
    return outs[:n], outs[n]


def _matmul_res(a, w, res, name, after, tm=512):
    S, K = a.shape
    N = w.shape[1]
    tm = _token_tile(S, tm)

    def body(a_ref, w_ref, r_ref, after_ref, o_ref):
        o_ref[...] = r_ref[...] + _dot(a_ref[...], w_ref[...])

    return pl.pallas_call(
        body, name=name, grid=(S // tm,),
        in_specs=[pl.BlockSpec((tm, K), lambda i: (i, 0)), _resident((K, N)),
                  pl.BlockSpec((tm, N), lambda i: (i, 0)), pl.BlockSpec(memory_space=pl.ANY)],
        out_specs=pl.BlockSpec((tm, N), lambda i: (i, 0)),
        out_shape=jax.ShapeDtypeStruct((S, N), F32),
        compiler_params=_params("parallel"),
    )(a, w, res, after)


def _matmul_nt(a, w, name, out_dtype, after, tm=512):
    S, N = a.shape
    K = w.shape[0]
    tm = _token_tile(S, tm)

    def body(a_ref, w_ref, after_ref, o_ref):
        o_ref[...] = _dot_nt(a_ref[...].astype(BF16), w_ref[...]).astype(out_dtype)

    return pl.pallas_call(
        body, name=name, grid=(S // tm,),
        in_specs=[pl.BlockSpec((tm, N), lambda i: (i, 0)), _resident((K, N)),
                  pl.BlockSpec(memory_space=pl.ANY)],
        out_specs=pl.BlockSpec((tm, K), lambda i: (i, 0)),
        out_shape=jax.ShapeDtypeStruct((S, K), out_dtype),
        compiler_params=_params("parallel"),
    )(a, w, after)


def _proj_bwd(a_lists, w_list, x, g, res, name, tm=256):
    S, D = x.shape
    tm = _token_tile(S, tm)
    a_list = [a for group in a_lists for a in group]
    n, n_w = len(a_list), len(w_list)

    def body(*refs):
        a_refs, w_refs = list(refs[:n]), refs[n:n + n_w]
        x_ref, g_ref, r_ref, dx_ref, dg_ref = refs[n + n_w:]
        dh = jnp.zeros((tm, D), F32)
        for group, w_ref in zip(a_lists, w_refs):
            nw = w_ref.shape[2]
            d = 0
            for _ in group:
                a_ref = a_refs.pop(0)
                for j in range(a_ref.shape[1] // nw):
                    dh = dh + _dot_nt(a_ref[:, j * nw:(j + 1) * nw].astype(BF16), w_ref[d])
                    d += 1
        dx, dg = _norm_bwd(dh, x_ref[...], g_ref[...])
        dx_ref[...] = r_ref[...] + dx

        @pl.when(pl.program_id(0) == 0)
        def _():
            dg_ref[...] = jnp.zeros_like(dg_ref)
        dg_ref[...] += dg

    tok = lambda width: pl.BlockSpec((tm, width), lambda i: (i, 0))
    return pl.pallas_call(
        body, name=name, grid=(S // tm,),
        in_specs=[tok(a.shape[1]) for a in a_list] + [_resident(w.shape) for w in w_list]
        + [tok(D), _resident((1, D)), tok(D)],
        out_specs=[tok(D), pl.BlockSpec((1, D), lambda i: (0, 0))],
        out_shape=[jax.ShapeDtypeStruct((S, D), F32), jax.ShapeDtypeStruct((1, D), F32)],
        compiler_params=_params("arbitrary"),
    )(*a_list, *w_list, x, g, res)


def _matmul_tn(a, b, name, rows=1, cols=1, col_blocks=None, a_square=False, tm=1024):
    S, K = a.shape
    N = b.shape[1]
    tm = _token_tile(S, tm)
    n_tok = S // tm
    kr, nc = K // rows, N // cols

    def body(a_ref, b_ref, o_ref, acc_ref):
        av = a_ref[...]
        if a_square:
            av = av.astype(F32)
            av = av * av
        part = _dot_tn(av.astype(BF16), b_ref[...].astype(BF16))
        step = pl.program_id(2)

        @pl.when(step == 0)
        def _():
            acc_ref[...] = part

        @pl.when(step > 0)
        def _():
            acc_ref[...] += part

        @pl.when(step == n_tok - 1)
        def _():
            if col_blocks is None:
                o_ref[...] = acc_ref[...].astype(BF16)
            else:
                nw = N // col_blocks
                for d in range(col_blocks // cols):
                    o_ref[d] = acc_ref[:, d * nw:(d + 1) * nw].astype(BF16)

    if col_blocks is None:
        out_spec = pl.BlockSpec((kr, nc), lambda r, c, i: (r, c))
        out_shape = jax.ShapeDtypeStruct((K, N), BF16)
    else:
        assert rows == 1 and col_blocks % cols == 0
        per = col_blocks // cols
        out_spec = pl.BlockSpec((per, K, N // col_blocks), lambda r, c, i: (c, 0, 0))
        out_shape = jax.ShapeDtypeStruct((col_blocks, K, N // col_blocks), BF16)
    return pl.pallas_call(
        body, name=name, grid=(rows, cols, n_tok),
        in_specs=[pl.BlockSpec((tm, kr), lambda r, c, i: (i, r)),
                  pl.BlockSpec((tm, nc), lambda r, c, i: (i, c))],
        out_specs=out_spec, out_shape=out_shape,
        scratch_shapes=[pltpu.VMEM((kr, nc), F32)],
        compiler_params=_params("parallel", "parallel", "arbitrary"),
    )(a, b)


def _mlp_fwd(x, g, w1, w2, name, target=None, tm=256):
    S, D = x.shape
    nb, _, fb = w1.shape
    tm = _token_tile(S, tm)
    with_loss = target is not None

    def body(x_ref, g_ref, w1_ref, w2_ref, *refs):
        h_ref, r_ref = refs[-2:]
        xv = x_ref[...]
        h = (xv * _rms_scale(xv) * g_ref[...]).astype(BF16)
        h_ref[...] = h
        acc = xv
        for d in range(nb):
            r = jnp.maximum(_dot(h, w1_ref[d]), 0.0)
            r_ref[:, d * fb:(d + 1) * fb] = r.astype(BF16)
            acc = acc + _dot((r * r).astype(BF16), w2_ref[d])
        if not with_loss:
            refs[0][...] = acc
            return
        t_ref, loss_ref, dy_ref = refs[:3]
        err = acc - t_ref[...]
        dy_ref[...] = err / D

        @pl.when(pl.program_id(0) == 0)
        def _():
            loss_ref[...] = jnp.zeros_like(loss_ref)
        row_loss = jnp.mean(err * err, axis=1, keepdims=True)
        loss_ref[...] += 0.5 * jnp.sum(row_loss, axis=0, keepdims=True)

    tok = lambda width: pl.BlockSpec((tm, width), lambda i: (i, 0))
    saved_specs = [tok(D), tok(nb * fb)]
    saved_shapes = [jax.ShapeDtypeStruct((S, D), BF16), jax.ShapeDtypeStruct((S, nb * fb), BF16)]
    wide = jax.ShapeDtypeStruct((S, D), F32)
    if with_loss:
        head_specs = [pl.BlockSpec((1, 1), lambda i: (0, 0)), tok(D)]
        head_shapes = [jax.ShapeDtypeStruct((1, 1), F32), wide]
    else:
        head_specs, head_shapes = [tok(D)], [wide]
    return pl.pallas_call(
        body, name=name, grid=(S // tm,),
        in_specs=[tok(D), _resident((1, D)), _resident(w1.shape), _resident(w2.shape)]
        + ([tok(D)] if with_loss else []),
        out_specs=head_specs + saved_specs, out_shape=head_shapes + saved_shapes,
        compiler_params=_params("arbitrary" if with_loss else "parallel"),
    )(x, g, w1, w2, *([target] if with_loss else []))


def _mlp_bwd(dout, x, g, r, w1, w2, name, tm=256):
    S, D = x.shape
    nb, _, fb = w1.shape
    tm = _token_tile(S, tm)

    def body(do_ref, x_ref, g_ref, r_ref, w1_ref, w2_ref, dx_ref, dg_ref, da_ref):
        dov = do_ref[...]
        dob = dov.astype(BF16)
        dh = jnp.zeros((tm, D), F32)
        for d in range(nb):
            dz = _dot_nt(dob, w2_ref[d])
            da = (dz * (2.0 * r_ref[:, d * fb:(d + 1) * fb].astype(F32))).astype(BF16)
            da_ref[:, d * fb:(d + 1) * fb] = da
            dh = dh + _dot_nt(da, w1_ref[d])
        dx, dg = _norm_bwd(dh, x_ref[...], g_ref[...])
        dx_ref[...] = dov + dx

        @pl.when(pl.program_id(0) == 0)
        def _():
            dg_ref[...] = jnp.zeros_like(dg_ref)
        dg_ref[...] += dg

    tok = lambda width: pl.BlockSpec((tm, width), lambda i: (i, 0))
    return pl.pallas_call(
        body, name=name, grid=(S // tm,),
        in_specs=[tok(D), tok(D), _resident((1, D)), tok(nb * fb), _resident(w1.shape),
                  _resident(w2.shape)],
        out_specs=[tok(D), pl.BlockSpec((1, D), lambda i: (0, 0)), tok(nb * fb)],
        out_shape=[jax.ShapeDtypeStruct((S, D), F32), jax.ShapeDtypeStruct((1, D), F32),
                   jax.ShapeDtypeStruct((S, nb * fb), BF16)],
        compiler_params=_params("arbitrary"),
    )(dout, x, g, r, w1, w2)


def _scan_chunk(a, b, row, T, reverse):
    s = 1
    while s < T:
        if reverse:
            keep, shift = row < T - s, T - s
        else:
            keep, shift = row >= s, s
        a_sh = jnp.where(keep, pltpu.roll(a, shift, 0), 1.0)
        b_sh = jnp.where(keep, pltpu.roll(b, shift, 0), 0.0)
        b = a * b_sh + b
        a = a * a_sh
        s *= 2
    return a, b


def _row_of(x, row, r):
    return jnp.sum(jnp.where(row == r, x, 0.0), axis=0, keepdims=True)


def _shift_down(x, prev, row, k):
    if k == 0:
        return x
    return jnp.where(row < k, pltpu.roll(prev, k, 0), pltpu.roll(x, k, 0))


def _shift_up(x, nxt, row, k, T):
    if k == 0:
        return x
    return jnp.where(row < T - k, pltpu.roll(x, T - k, 0), pltpu.roll(nxt, T - k, 0))


def _lru_gates(xb, prev_xb, row, cw_ref, cb, wr, br, wi, bi, ls):
    xc = cb + cw_ref[pl.ds(0, 1), :] * _shift_down(xb, prev_xb, row, 3)
    for k in (2, 1, 0):
        xc = xc + cw_ref[pl.ds(3 - k, 1), :] * _shift_down(xb, prev_xb, row, k)
    xcb = xc.astype(BF16)
    r = _sigmoid(_dot(xcb, wr) + br)
    i = _sigmoid(_dot(xcb, wi) + bi)
    la = (LRU_C * r) * ls
    a = jnp.exp(la)
    m = jnp.sqrt(-_expm1(2.0 * la))
    return xc, xcb, r, i, a, m


def _lru_specs(S):
    col = lambda off: pl.BlockSpec((S, LANES), lambda j: (0, j + off))
    vec = pl.BlockSpec((1, LANES), lambda j: (0, j))
    mat = pl.BlockSpec((None, LANES, LANES), lambda j: (j, 0, 0))
    cwm = pl.BlockSpec((CONV_WIDTH, LANES), lambda j: (0, j))
    return col, vec, mat, cwm


def _lru_fwd(u, conv_w, conv_b, wr, br, wi, bi, lam, name):
    S = u.shape[0]
    T = _token_tile(S, 256)
    col, vec, mat, cwm = _lru_specs(S)

    def body(gp_ref, xb_ref, cw_ref, cb_ref, wr_ref, br_ref, wi_ref, bi_ref, lam_ref,
             y_ref, hs_ref):
        row = lax.broadcasted_iota(jnp.int32, (T, LANES), 0)
        ls = _log_sigmoid(lam_ref[...])
        cb, br, bi = cb_ref[...], br_ref[...], bi_ref[...]
        wr, wi = wr_ref[...], wi_ref[...]

        def chunk(ci, carry):
            prev_xb, hc = carry
            rows = pl.ds(pl.multiple_of(ci * T, T), T)
            xb = xb_ref[rows, :]
            xc, _, _, i, a, m = _lru_gates(xb, prev_xb, row, cw_ref, cb, wr, br, wi, bi, ls)
            ca, cbv = _scan_chunk(a, m * (i * xc), row, T, reverse=False)
            h = ca * hc + cbv
            hs_ref[rows, :] = h
            y_ref[rows, :] = (_gelu(gp_ref[rows, :]) * h).astype(BF16)
            return xb, _row_of(h, row, T - 1)

        lax.fori_loop(0, S // T, chunk,
                      (jnp.zeros((T, LANES), F32), jnp.zeros((1, LANES), F32)))

    return pl.pallas_call(
        body, name=name, grid=(N_CBLK,),
        in_specs=[col(0), col(N_CBLK), cwm, vec, mat, vec, mat, vec, vec],
        out_specs=[col(0), col(0)],
        out_shape=[jax.ShapeDtypeStruct((S, D_MODEL), BF16), jax.ShapeDtypeStruct((S, D_MODEL), F32)],
        compiler_params=_params("parallel"),
    )(u, u, conv_w, conv_b, wr, br, wi, bi, lam)


def _lru_bwd(dy, u, hs, conv_w, conv_b, wr, br, wi, bi, lam, name):
    S = u.shape[0]
    T = _token_tile(S, 256)
    n_chunk = S // T
    col, vec, mat, cwm = _lru_specs(S)

    def body(dy_ref, gp_ref, xb_ref, hs_ref, cw_ref, cb_ref, wr_ref, br_ref, wi_ref, bi_ref,
             lam_ref, dgp_ref, dxb_ref, dcw_ref, dcb_ref, dbr_ref, dbi_ref, dlam_ref, dwr_ref,
             dwi_ref):
        row = lax.broadcasted_iota(jnp.int32, (T, LANES), 0)
        lam = lam_ref[...]
        ls = _log_sigmoid(lam)
        cb, br, bi = cb_ref[...], br_ref[...], bi_ref[...]
        wr, wi = wr_ref[...], wi_ref[...]
        for ref in (dcw_ref, dcb_ref, dbr_ref, dbi_ref, dlam_ref, dwr_ref, dwi_ref):
            ref[...] = jnp.zeros_like(ref)

        def chunk(it, carry):
            g_next, dxc_next = carry
            ci = n_chunk - 1 - it
            rows = pl.ds(pl.multiple_of(ci * T, T), T)
            before = pl.ds(pl.multiple_of(jnp.maximum(ci - 1, 0) * T, T), T)
            first = ci == 0
            xb = xb_ref[rows, :]
            prev_xb = jnp.where(first, 0.0, xb_ref[before, :])
            xc, xcb, r, i, a, m = _lru_gates(xb, prev_xb, row, cw_ref, cb, wr, br, wi, bi, ls)
            h = hs_ref[rows, :]
            h_prev = _shift_down(h, jnp.where(first, 0.0, hs_ref[before, :]), row, 1)
            gp = gp_ref[rows, :]
            dyv = dy_ref[rows, :]
            dgp_ref[rows, :] = (dyv * h * _gelu_grad(gp)).astype(BF16)
            dh = dyv * _gelu(gp)
            ca, cbv = _scan_chunk(a, a * dh, row, T, reverse=True)
            gp_acc = ca * g_next + cbv
            g = dh + jnp.where(row < T - 1, pltpu.roll(gp_acc, T - 1, 0), g_next)
            da = g * h_prev - (g * (i * xc)) * a / m
            dla = da * a
            dlam_ref[...] += jnp.sum(dla * (LRU_C * r), axis=0, keepdims=True)
            dpr = (dla * (LRU_C * ls)) * r * (1.0 - r)
            dpi = (g * m * xc) * i * (1.0 - i)
            dbr_ref[...] += jnp.sum(dpr, axis=0, keepdims=True)
            dbi_ref[...] += jnp.sum(dpi, axis=0, keepdims=True)
            dprb, dpib = dpr.astype(BF16), dpi.astype(BF16)
            dwr_ref[...] += _dot_tn(xcb, dprb)
            dwi_ref[...] += _dot_tn(xcb, dpib)
            dxc = g * m * i + _dot_nt(dprb, wr) + _dot_nt(dpib, wi)
            dcb_ref[...] += jnp.sum(dxc, axis=0, keepdims=True)
            dxb = jnp.zeros((T, LANES), F32)
            for k in range(CONV_WIDTH):
                tap = pl.ds(CONV_WIDTH - 1 - k, 1)
                dcw_ref[tap, :] += jnp.sum(dxc * _shift_down(xb, prev_xb, row, k), axis=0,
                                           keepdims=True)
                dxb = dxb + cw_ref[tap, :] * _shift_up(dxc, dxc_next, row, k, T)
            dxb_ref[rows, :] = dxb.astype(BF16)
            return _row_of(gp_acc, row, 0), dxc

        lax.fori_loop(0, n_chunk, chunk,
                      (jnp.zeros((1, LANES), F32), jnp.zeros((T, LANES), F32)))
        dlam_ref[...] = dlam_ref[...] * _sigmoid(-lam)

    vec_out = jax.ShapeDtypeStruct((1, D_MODEL), F32)
    mat_out = jax.ShapeDtypeStruct((N_CBLK, LANES, LANES), F32)
    return pl.pallas_call(
        body, name=name, grid=(N_CBLK,),
        in_specs=[col(0), col(0), col(N_CBLK), col(0), cwm, vec, mat, vec, mat, vec, vec],
        out_specs=[col(0), col(0), cwm, vec, vec, vec, vec, mat, mat],
        out_shape=[jax.ShapeDtypeStruct((S, D_MODEL), BF16), jax.ShapeDtypeStruct((S, D_MODEL), BF16),
                   jax.ShapeDtypeStruct((CONV_WIDTH, D_MODEL), F32),
                   vec_out, vec_out, vec_out, vec_out, mat_out, mat_out],
        compiler_params=_params("parallel"),
    )(dy, u, u, hs, conv_w, conv_b, wr, br, wi, bi, lam)


def _head_group_matrix(value):
    r = lax.broadcasted_iota(jnp.int32, (LANES, LANES), 0) // HEAD_DIM
    c = lax.broadcasted_iota(jnp.int32, (LANES, LANES), 1) // HEAD_DIM
    return jnp.where(r == c, value, 0.0).astype(BF16)


def _group_dot(x, p):
    hi = x.astype(BF16)
    lo = (x - hi.astype(F32)).astype(BF16)
    return _dot(hi, p) + _dot(lo, p)


def _head_mean(x, p):
    return _group_dot(x, p)


def _qk_prep(u, q_gain, k_gain, name, tm=256):
    S = u.shape[0]
    tm = _token_tile(S, tm)

    def body(q_ref, k_ref, v_ref, qg_ref, kg_ref, qn_ref, kn_ref, vb_ref):
        p = _head_group_matrix(1.0 / HEAD_DIM)
        for j in range(N_CBLK):
            cl = slice(j * LANES, (j + 1) * LANES)
            for x_ref, g_ref, o_ref, scale in ((q_ref, qg_ref, qn_ref, ATTN_SCALE),
                                               (k_ref, kg_ref, kn_ref, 1.0)):
                xv = x_ref[:, cl]
                rs = lax.rsqrt(_head_mean(xv * xv, p) + EPS)
                o_ref[:, cl] = (xv * rs * g_ref[...]).astype(BF16) * scale
        vb_ref[...] = v_ref[...].astype(BF16)

    blk = lambda off: pl.BlockSpec((tm, D_MODEL), lambda i: (i, off))
    out = jax.ShapeDtypeStruct((S, D_MODEL), BF16)
    return pl.pallas_call(
        body, name=name, grid=(S // tm,),
        in_specs=[blk(0), blk(1), blk(2), _resident((1, LANES)), _resident((1, LANES))],
        out_specs=[blk(0), blk(0), blk(0)],
        out_shape=[out, out, out],
        compiler_params=_params("parallel"),
    )(u, u, u, q_gain, k_gain)


def _qk_bwd(u, dqn, dkn, q_gain, k_gain, name, tm=256):
    S = u.shape[0]
    tm = _token_tile(S, tm)

    def body(q_ref, k_ref, dqn_ref, dkn_ref, qg_ref, kg_ref, dq_ref, dk_ref, dqg_ref, dkg_ref):
        p = _head_group_matrix(1.0 / HEAD_DIM)
        for x_ref, dn_ref, g_ref, dx_ref, dg_ref, scale in (
                (q_ref, dqn_ref, qg_ref, dq_ref, dqg_ref, ATTN_SCALE),
                (k_ref, dkn_ref, kg_ref, dk_ref, dkg_ref, 1.0)):
            dg = jnp.zeros((1, LANES), F32)
            for j in range(N_CBLK):
                cl = slice(j * LANES, (j + 1) * LANES)
                xv, dn = x_ref[:, cl], dn_ref[:, cl] * scale
                rs = lax.rsqrt(_head_mean(xv * xv, p) + EPS)
                xhat = xv * rs
                dxhat = dn * g_ref[...]
                dx_ref[:, cl] = (rs * (dxhat - xhat * _head_mean(dxhat * xhat, p))).astype(BF16)
                dg = dg + jnp.sum(dn * xhat, axis=0, keepdims=True)

            @pl.when(pl.program_id(0) == 0)
            def _():
                dg_ref[...] = jnp.zeros_like(dg_ref)
            dg_ref[...] += dg

            @pl.when(pl.program_id(0) == S // tm - 1)
            def _():
                dg_ref[...] += pltpu.roll(dg_ref[...], HEAD_DIM, 1)

    blk = lambda off: pl.BlockSpec((tm, D_MODEL), lambda i: (i, off))
    acc = pl.BlockSpec((1, LANES), lambda i: (0, 0))
    out = jax.ShapeDtypeStruct((S, D_MODEL), BF16)
    vec = jax.ShapeDtypeStruct((1, LANES), F32)
    return pl.pallas_call(
        body, name=name, grid=(S // tm,),
        in_specs=[blk(0), blk(1), blk(0), blk(0), _resident((1, LANES)), _resident((1, LANES))],
        out_specs=[blk(0), blk(0), acc, acc],
        out_shape=[out, out, vec, vec],
        compiler_params=_params("arbitrary"),
    )(u, u, dqn, dkn, q_gain, k_gain)


def _forget_fwd(f, b_f, name):
    S = f.shape[0]
    T = _token_tile(S, 256)

    def body(f_ref, b_ref, c_ref):
        row = lax.broadcasted_iota(jnp.int32, (T, LANES), 0)
        ones = jnp.ones((T, LANES), F32)
        bias = b_ref[...]

        def chunk(ci, carry):
            rows = pl.ds(pl.multiple_of(ci * T, T), T)
            _, c = _scan_chunk(ones, _log_sigmoid(f_ref[rows, :] + bias), row, T, reverse=False)
            c = c + carry
            c_ref[rows, :] = c
            return _row_of(c, row, T - 1)

        lax.fori_loop(0, S // T, chunk, jnp.zeros((1, LANES), F32))

    return pl.pallas_call(
        body, name=name,
        in_specs=[pl.BlockSpec(memory_space=pltpu.VMEM)] * 2,
        out_specs=pl.BlockSpec(memory_space=pltpu.VMEM),
        out_shape=jax.ShapeDtypeStruct((S, LANES), F32),
        compiler_params=pltpu.CompilerParams(vmem_limit_bytes=VMEM_LIMIT),
    )(f, b_f)


def _forget_bwd(dc_k, rho, f, b_f, name):
    S = f.shape[0]
    T = _token_tile(S, 256)
    n_chunk = S // T

    def body(dck_ref, rho_ref, f_ref, b_ref, df_ref, db_ref):
        row = lax.broadcasted_iota(jnp.int32, (T, LANES), 0)
        ones = jnp.ones((T, LANES), F32)
        bias = b_ref[...]
        pick = (lax.broadcasted_iota(jnp.int32, (D_MODEL, LANES), 0)
                == HEAD_DIM * lax.broadcasted_iota(jnp.int32, (D_MODEL, LANES), 1))
        pick = jnp.where(pick, 1.0, 0.0).astype(BF16)

        def chunk(it, carry):
            tail, db = carry
            rows = pl.ds(pl.multiple_of((n_chunk - 1 - it) * T, T), T)
            dc = dck_ref[rows, :] + _group_dot(rho_ref[rows, :], pick)
            _, dlf = _scan_chunk(ones, dc, row, T, reverse=True)
            dlf = dlf + tail
            df = dlf * _sigmoid(-(f_ref[rows, :] + bias))
            df_ref[rows, :] = df
            return _row_of(dlf, row, 0), db + jnp.sum(df, axis=0, keepdims=True)

        zero = jnp.zeros((1, LANES), F32)
        _, db = lax.fori_loop(0, n_chunk, chunk, (zero, zero))
        db_ref[...] = db

    return pl.pallas_call(
        body, name=name,
        in_specs=[pl.BlockSpec(memory_space=pltpu.VMEM)] * 4,
        out_specs=[pl.BlockSpec(memory_space=pltpu.VMEM)] * 2,
        out_shape=[jax.ShapeDtypeStruct((S, LANES), F32), jax.ShapeDtypeStruct((1, LANES), F32)],
        compiler_params=pltpu.CompilerParams(vmem_limit_bytes=VMEM_LIMIT),
    )(dc_k, rho, f, b_f)


ATTN_TILE = 512
ATTN_ROWS_FWD = 32


def _attn_tiles(S):
    t = _token_tile(S, ATTN_TILE)
    return t, S // t


def _causal(T):
    return (lax.broadcasted_iota(jnp.int32, (T, T), 1)
            <= lax.broadcasted_iota(jnp.int32, (T, T), 0))


def _attn_fwd(qs_, kn, vb, c_row, name):
    S = qs_.shape[0]
    T, n_t = _attn_tiles(S)
    RB = min(T, ATTN_ROWS_FWD)

    def body(q_ref, k_ref, v_ref, cr_ref, o_ref, lse_ref, sa_ref, sb_ref, p_ref, m_ref, l_ref,
             acc_ref, a_ref):
        qi = pl.program_id(1)
        lanes = [slice(h2 * HEAD_DIM, (h2 + 1) * HEAD_DIM) for h2 in range(2)]
        col = lax.broadcasted_iota(jnp.int32, (RB, T), 1)
        row = lax.broadcasted_iota(jnp.int32, (RB, T), 0)
        m_ref[...] = jnp.full(m_ref.shape, NEG_INF, F32)
        l_ref[...] = jnp.zeros_like(l_ref)
        acc_ref[...] = jnp.zeros_like(acc_ref)

        def logits_into(s_ref, kj):
            ks = pl.ds(pl.multiple_of(kj * T, T), T)
            for h2, hl in enumerate(lanes):
                s_ref[h2] = _dot_nt(q_ref[:, hl], k_ref[ks, hl]) - cr_ref[h2:h2 + 1, ks]

        def consume(s_ref, kj, masked):
            ks = pl.ds(pl.multiple_of(kj * T, T), T)
            for h2, hl in enumerate(lanes):
                blocks = [slice(i * RB, (i + 1) * RB) for i in range(T // RB)]

                def logits(i, rows):
                    s = s_ref[h2, rows, :]
                    return jnp.where(col <= row + i * RB, s, NEG_INF) if masked else s

                wide = lambda x: jnp.broadcast_to(x, (RB, LANES))
                for i, rows in enumerate(blocks):
                    mx = wide(jnp.max(logits(i, rows), axis=1, keepdims=True))
                    a_ref[h2, rows, :] = m_ref[h2, rows, :]
                    m_ref[h2, rows, :] = jnp.maximum(m_ref[h2, rows, :], mx)
                for i, rows in enumerate(blocks):
                    m_new = m_ref[h2, rows, :]
                    p = jnp.exp(logits(i, rows) - jnp.tile(m_new, (1, T // LANES)))
                    alpha = jnp.exp(a_ref[h2, rows, :] - m_new)
                    a_ref[h2, rows, :] = alpha
                    l_ref[h2, rows, :] = (alpha * l_ref[h2, rows, :]
                                          + wide(jnp.sum(p, axis=1, keepdims=True)))
                    p_ref[h2, rows, :] = p.astype(BF16)
                acc_ref[h2] = (a_ref[h2, :, :HEAD_DIM] * acc_ref[h2]
                               + _dot(p_ref[h2], v_ref[ks, hl]))

        logits_into(sa_ref, 0)

        def pair(i, _):
            logits_into(sb_ref, 2 * i + 1)
            consume(sa_ref, 2 * i, False)
            logits_into(sa_ref, 2 * i + 2)
            consume(sb_ref, 2 * i + 1, False)
            return 0

        lax.fori_loop(0, qi // 2, pair, 0)

        @pl.when(qi % 2 == 1)
        def _():
            logits_into(sb_ref, qi)
            consume(sa_ref, qi - 1, False)
            consume(sb_ref, qi, True)

        @pl.when(qi % 2 == 0)
        def _():
            consume(sa_ref, qi, True)

        for h2, hl in enumerate(lanes):
            o_ref[:, hl] = (acc_ref[h2] / l_ref[h2, :, :HEAD_DIM]).astype(BF16)
            lse_ref[:, hl] = m_ref[h2, :, :HEAD_DIM] + jnp.log(l_ref[h2, :, :HEAD_DIM])

    qblk = pl.BlockSpec((T, LANES), lambda h, i: (i, h))
    kv = pl.BlockSpec((S, LANES), lambda h, i: (0, h))
    return pl.pallas_call(
        body, name=name, grid=(N_CBLK, n_t),
        in_specs=[qblk, kv, kv, pl.BlockSpec((None, 2, S), lambda h, i: (h, 0, 0))],
        out_specs=[qblk, qblk],
        out_shape=[jax.ShapeDtypeStruct((S, D_MODEL), BF16),
                   jax.ShapeDtypeStruct((S, D_MODEL), F32)],
        scratch_shapes=[pltpu.VMEM((2, T, T), F32), pltpu.VMEM((2, T, T), F32),
                        pltpu.VMEM((2, T, T), BF16),
                        pltpu.VMEM((2, T, LANES), F32), pltpu.VMEM((2, T, LANES), F32),
                        pltpu.VMEM((2, T, HEAD_DIM), F32), pltpu.VMEM((2, T, LANES), F32)],
        compiler_params=_params("parallel", "parallel"),
    )(qs_, kn, vb, c_row)


def _attn_bwd(qs_, kn, vb, do, o, lse, c_row, name):
    S = qs_.shape[0]
    T, n_t = _attn_tiles(S)

    def body(q_ref, k_ref, v_ref, do_ref, o_ref, lse_ref, cr_ref,
             dq_ref, dk_ref, dv_ref, dc_ref, rho_ref, dd_ref):
        kj = pl.program_id(1)
        causal = _causal(T)
        lanes = [slice(h2 * HEAD_DIM, (h2 + 1) * HEAD_DIM) for h2 in range(2)]
        ones = [slice(h2 * HEAD_DIM, h2 * HEAD_DIM + 1) for h2 in range(2)]

        @pl.when(kj == 0)
        def _():
            dq_ref[...] = jnp.zeros_like(dq_ref)
            rho_ref[...] = jnp.zeros_like(rho_ref)
            p_sum = _head_group_matrix(1.0)

            def fill(ci, _):
                rows = pl.ds(pl.multiple_of(ci * T, T), T)
                dd_ref[rows, :] = _group_dot(do_ref[rows, :].astype(F32) * o_ref[rows, :].astype(F32),
                                             p_sum)
                return 0

            lax.fori_loop(0, n_t, fill, 0)

        kh = [k_ref[:, hl] for hl in lanes]
        vh = [v_ref[:, hl] for hl in lanes]
        ck = [cr_ref[h2:h2 + 1, :] for h2 in range(2)]

        def step(qi, carry, masked):
            qs = pl.ds(pl.multiple_of(qi * T, T), T)
            out = []
            for h2, hl in enumerate(lanes):
                dk, dv, dc = carry[h2]
                qh, doh = q_ref[qs, hl], do_ref[qs, hl]
                s = _dot_nt(qh, kh[h2]) - ck[h2]
                if masked:
                    s = jnp.where(causal, s, NEG_INF)
                p = jnp.exp(s - lse_ref[qs, ones[h2]])
                ds = p * (_dot_nt(doh, vh[h2]) - dd_ref[qs, ones[h2]])
                dsb = ds.astype(BF16)
                dq_ref[qs, hl] += _dot(dsb, kh[h2])
                rho_ref[qs, hl] += jnp.broadcast_to(jnp.sum(ds, axis=1, keepdims=True),
                                                    (T, HEAD_DIM))
                out.append((dk + _dot_tn(dsb, qh), dv + _dot_tn(p.astype(BF16), doh),
                            dc - jnp.sum(ds, axis=0, keepdims=True)))
            return tuple(out)

        init = tuple((jnp.zeros((T, HEAD_DIM), F32), jnp.zeros((T, HEAD_DIM), F32),
                      jnp.zeros((1, T), F32)) for _ in lanes)
        carry = step(kj, init, True)
        carry = lax.fori_loop(kj + 1, n_t, lambda qi, c: step(qi, c, False), carry)
        for h2, ((dk, dv, dc), hl) in enumerate(zip(carry, lanes)):
            dk_ref[:, hl] = dk
            dv_ref[:, hl] = dv.astype(BF16)
            dc_ref[h2:h2 + 1, :] = dc

    kblk = pl.BlockSpec((T, LANES), lambda h, j: (j, h))
    full = pl.BlockSpec((S, LANES), lambda h, j: (0, h))
    crow = pl.BlockSpec((None, 2, T), lambda h, j: (h, 0, j))
    wide = jax.ShapeDtypeStruct((S, D_MODEL), F32)
    return pl.pallas_call(
        body, name=name, grid=(N_CBLK, n_t),
        in_specs=[full, kblk, kblk, full, full, full, crow],
        out_specs=[full, kblk, kblk, crow, full],
        out_shape=[wide, wide, jax.ShapeDtypeStruct((S, D_MODEL), BF16),
                   jax.ShapeDtypeStruct((N_CBLK, 2, S), F32), wide],
        scratch_shapes=[pltpu.VMEM((S, LANES), F32)],
        compiler_params=_params("parallel", "arbitrary"),
    )(qs_, kn, vb, do, o, lse, c_row)


ALL_PEERS = tuple(range(1, N_DEV))
NEAR_PEERS = (1, 2, 4, 6)
FAR_CHIPS = (2, 4, 6)


def _landing_shapes(arrays, gathers):
    return [jax.ShapeDtypeStruct((N_DEV,) + a.shape if g else a.shape, a.dtype)
            for a, g in zip(arrays, gathers)]


def _my_index():
    return 4 * lax.axis_index("x") + 2 * lax.axis_index("y") + lax.axis_index("c")


def _own_copies(srcs, lands, gathers, sems):
    me = _my_index()
    return [pltpu.make_async_copy(src if g else src.at[me], land.at[me], sems.at[a])
            for a, (src, land, g) in enumerate(zip(srcs, lands, gathers))]


def _peer_copies(srcs, lands, gathers, send_sems, recv_sems, ks=ALL_PEERS):
    x, y, c = lax.axis_index("x"), lax.axis_index("y"), lax.axis_index("c")
    me = 4 * x + 2 * y + c
    out = []
    for j, k in enumerate(ks):
        to = (1 - x if k & 4 else x, 1 - y if k & 2 else y, 1 - c if k & 1 else c)
        peer = 4 * to[0] + 2 * to[1] + to[2]
        for a, (src, land, g) in enumerate(zip(srcs, lands, gathers)):
            sem = a * len(ks) + j
            src_blk = src if g else src.at[peer]

            def copy(slot, src_blk=src_blk, land=land, sem=sem, to=to):
                return pltpu.make_async_remote_copy(
                    src_ref=src_blk, dst_ref=land.at[slot], send_sem=send_sems.at[sem],
                    recv_sem=recv_sems.at[sem], device_id=to,
                    device_id_type=pl.DeviceIdType.MESH)

            out.append((k, a, copy(me), copy(peer)))
    return out


def _forward_copies(lands, send_sems, recv_sems):
    x, y, c = lax.axis_index("x"), lax.axis_index("y"), lax.axis_index("c")
    out = []
    for j, f in enumerate(FAR_CHIPS):
        chip = 4 * (1 - x if f & 4 else x) + 2 * (1 - y if f & 2 else y)
        for a, land in enumerate(lands):
            sem = a * len(FAR_CHIPS) + j

            def copy(slot, land=land, sem=sem):
                return pltpu.make_async_remote_copy(
                    src_ref=land.at[slot], dst_ref=land.at[slot], send_sem=send_sems.at[sem],
                    recv_sem=recv_sems.at[sem], device_id=(x, y, 1 - c),
                    device_id_type=pl.DeviceIdType.MESH)

            out.append((f, a, copy(chip + c), copy(chip + 1 - c)))
    return out


def _exchange(arrays, gathers, name, two_level=False):
    n = len(arrays)
    ks = NEAR_PEERS if two_level else ALL_PEERS
    assert not two_level or all(gathers)

    def body(*refs):
        ins, outs = refs[:n], refs[n:2 * n]
        send_sems, recv_sems, own_sems, fwd_send_sems, fwd_recv_sems = refs[2 * n:]
        own = _own_copies(ins, outs, gathers, own_sems)
        for cp in own:
            cp.start()
        copies = _peer_copies(ins, outs, gathers, send_sems, recv_sems, ks)
        for _, _, send, _ in copies:
            send.start()
        passed = {}
        if two_level:
            passed = {(f, a): (send, arrival)
                      for f, a, send, arrival in _forward_copies(outs, fwd_send_sems, fwd_recv_sems)}
        for k, a, _, arrival in copies:
            arrival.wait_recv()
            if (k, a) in passed:
                passed[k, a][0].start()
        for send, arrival in passed.values():
            arrival.wait_recv()
            send.wait_send()
        for _, _, send, _ in copies:
            send.wait_send()
        for cp in own:
            cp.wait()

    hbm = pl.BlockSpec(memory_space=pl.ANY)
    return pl.pallas_call(
        body, name=name,
        in_specs=[hbm] * n, out_specs=[hbm] * n, out_shape=_landing_shapes(arrays, gathers),
        scratch_shapes=[pltpu.SemaphoreType.DMA((n * len(ks),)),
                        pltpu.SemaphoreType.DMA((n * len(ks),)),
                        pltpu.SemaphoreType.DMA((n,)),
                        pltpu.SemaphoreType.DMA((n * len(FAR_CHIPS),)),
                        pltpu.SemaphoreType.DMA((n * len(FAR_CHIPS),))],
        compiler_params=pltpu.CompilerParams(has_side_effects=True),
    )(*arrays)


_HBM = pl.BlockSpec(memory_space=pltpu.HBM)
_SEM = pl.BlockSpec(memory_space=pltpu.SEMAPHORE)
_ANY = pl.BlockSpec(memory_space=pl.ANY)
_DATAFLOW = pltpu.SideEffectType.DATAFLOW_SIDE_EFFECTING


def _in_hbm(a):
    return pltpu.with_memory_space_constraint(a, pltpu.HBM)


def _exchange_start(arrays, gathers, after, name, ks=ALL_PEERS):
    n = len(arrays)
    lands = [lax.empty(s.shape, s.dtype) for s in _landing_shapes(arrays, gathers)]

    def body(*refs):
        srcs, dsts = refs[:n], refs[n:2 * n]
        send_sems, recv_sems, own_sems = refs[2 * n + 1:2 * n + 4]
        token = refs[-1]
        for cp in _own_copies(srcs, dsts, gathers, own_sems):
            cp.start()
        for _, _, send, _ in _peer_copies(srcs, dsts, gathers, send_sems, recv_sems, ks):
            send.start()
        token[...] = jnp.zeros_like(token)

    hbm_like = [pltpu.HBM(a.shape, a.dtype) for a in list(arrays) + lands]
    res = pl.pallas_call(
        body, name=name,
        in_specs=[_HBM] * (2 * n) + [_ANY],
        out_specs=(_SEM, _SEM, _SEM, *[_HBM] * (2 * n), pl.BlockSpec(memory_space=pltpu.VMEM)),
        out_shape=(pltpu.SemaphoreType.DMA((n * len(ks),)), pltpu.SemaphoreType.DMA((n * len(ks),)),
                   pltpu.SemaphoreType.DMA((n,)), *hbm_like,
                   jax.ShapeDtypeStruct((8, LANES), F32)),
        input_output_aliases={i: 3 + i for i in range(2 * n)},
        compiler_params=pltpu.CompilerParams(has_side_effects=_DATAFLOW),
    )(*[_in_hbm(a) for a in list(arrays) + lands], after)
    return (res[0], res[1], res[2], res[3:3 + n], res[3 + n:3 + 2 * n]), res[-1]


def _exchange_wait(started, gathers, after, name, ks=ALL_PEERS):
    send_sems, recv_sems, own_sems, arrays, lands = started
    n = len(arrays)

    def body(*refs):
        srcs, dsts = refs[:n], refs[n:2 * n]
        for _, _, send, arrival in _peer_copies(srcs, dsts, gathers, refs[2 * n], refs[2 * n + 1],
                                                ks):
            arrival.wait_recv()
            send.wait_send()
        for cp in _own_copies(srcs, dsts, gathers, refs[2 * n + 2]):
            cp.wait()

    hbm_like = [pltpu.HBM(a.shape, a.dtype) for a in list(arrays) + list(lands)]
    res = pl.pallas_call(
        body, name=name,
        in_specs=[_HBM] * (2 * n) + [_SEM, _SEM, _SEM, _ANY],
        out_specs=[_HBM] * (2 * n), out_shape=hbm_like,
        input_output_aliases={i: i for i in range(2 * n)},
        compiler_params=pltpu.CompilerParams(has_side_effects=_DATAFLOW),
    )(*arrays, *lands, send_sems, recv_sems, own_sems, after)
    return res[n:]


def _forward_start(lands, after, name):
    n = len(lands)

    def body(*refs):
        send_sems, recv_sems = refs[n + 1:n + 3]
        for _, _, send, _ in _forward_copies(refs[:n], send_sems, recv_sems):
            send.start()
        refs[-1][...] = jnp.zeros_like(refs[-1])

    n_sem = n * len(FAR_CHIPS)
    res = pl.pallas_call(
        body, name=name,
        in_specs=[_HBM] * n + [_ANY],
        out_specs=(_SEM, _SEM, *[_HBM] * n, pl.BlockSpec(memory_space=pltpu.VMEM)),
        out_shape=(pltpu.SemaphoreType.DMA((n_sem,)), pltpu.SemaphoreType.DMA((n_sem,)),
                   *[pltpu.HBM(a.shape, a.dtype) for a in lands],
                   jax.ShapeDtypeStruct((8, LANES), F32)),
        input_output_aliases={i: 2 + i for i in range(n)},
        compiler_params=pltpu.CompilerParams(has_side_effects=_DATAFLOW),
    )(*[_in_hbm(a) for a in lands], after)
    return (res[0], res[1], res[2:2 + n]), res[-1]


def _forward_wait(started, after, name):
    send_sems, recv_sems, lands = started
    n = len(lands)

    def body(*refs):
        for _, _, send, arrival in _forward_copies(refs[:n], refs[n], refs[n + 1]):
            arrival.wait_recv()
            send.wait_send()

    return pl.pallas_call(
        body, name=name,
        in_specs=[_HBM] * n + [_SEM, _SEM, _ANY],
        out_specs=[_HBM] * n, out_shape=[pltpu.HBM(a.shape, a.dtype) for a in lands],
        input_output_aliases={i: i for i in range(n)},
        compiler_params=pltpu.CompilerParams(has_side_effects=_DATAFLOW),
    )(*lands, send_sems, recv_sems, after)


def _reduce_adamw(parts, w, m, v, name):
    n_layer = len(parts)
    n, R, C = parts[0].shape
    tr = 256 if R % 256 == 0 else R
    n_t = R // tr

    def body(*refs):
        p_refs = refs[:n_layer]
        w_ref, m_ref, v_ref, g_ref, d_ref, nm_ref, nv_ref = refs[n_layer:]

        def update(p_ref):
            g = p_ref[0].astype(F32)
            for s in range(1, n):
                g = g + p_ref[s].astype(F32)
            g_ref[...] = g
            m_new = ADAM_B1 * m_ref[...] + (1.0 - ADAM_B1) * g
            v_new = ADAM_B2 * v_ref[...] + (1.0 - ADAM_B2) * (g * g)
            nm_ref[...] = m_new
            nv_ref[...] = v_new
            m_hat = m_new / (1.0 - ADAM_B1 ** ADAM_STEP)
            v_hat = v_new / (1.0 - ADAM_B2 ** ADAM_STEP)
            d_ref[...] = -ADAM_LR * (m_hat / (jnp.sqrt(v_hat) + ADAM_EPS) + ADAM_WD * w_ref[...])

        for layer, p_ref in enumerate(p_refs):
            pl.when(pl.program_id(0) == layer)(functools.partial(update, p_ref))

    def parts_spec(layer):
        def index(l, i):
            return 0, jnp.where(l < layer, 0, jnp.where(l > layer, n_t - 1, i)), 0
        return pl.BlockSpec((n, tr, C), index)

    blk = pl.BlockSpec((None, tr, C), lambda l, i: (l, i, 0))
    out = jax.ShapeDtypeStruct((n_layer, R, C), F32)
    return pl.pallas_call(
        body, name=name, grid=(n_layer, n_t),
        in_specs=[parts_spec(layer) for layer in range(n_layer)] + [blk, blk, blk],
        out_specs=[blk] * 4, out_shape=[out] * 4,
        compiler_params=_params("arbitrary", "arbitrary"),
    )(*parts, w, m, v)


def _pack(arrays):
    flat = jnp.concatenate([a.reshape(-1).astype(F32) for a in arrays])
    pad = (-flat.shape[0]) % (8 * LANES)
    return jnp.pad(flat, (0, pad)).reshape(-1, LANES)


def _unpack(buf, shapes):
    flat = buf.reshape(-1)
    out, off = [], 0
    for shp in shapes:
        size = 1
        for s in shp:
            size *= s
        out.append(flat[off:off + size].reshape(shp))
        off += size
    return out


def _block_diag_pairs(w):
    w = w.reshape(N_CBLK, 2, LRU_BLOCK_DIM, LRU_BLOCK_DIM)
    z = jnp.zeros_like(w[:, 0])
    top = jnp.concatenate([w[:, 0], z], axis=2)
    bot = jnp.concatenate([z, w[:, 1]], axis=2)
    return jnp.concatenate([top, bot], axis=1)


def _diag_pairs(m):
    h = LRU_BLOCK_DIM
    return jnp.stack([m[:, :h, :h], m[:, h:, h:]], axis=1).reshape(2 * N_CBLK, h, h)


SMALL = ("mlp_norm", "lru_conv_b", "lru_w_r", "lru_b_r", "lru_w_i", "lru_b_i",
         "lru_lambda", "fox_b_f", "fox_q_gain", "fox_k_gain")
WEIGHTS = ("mix_norm", "mlp_norm", "mlp_w1", "mlp_w2", "lru_w_in", "lru_conv_w", "lru_conv_b",
           "lru_w_r", "lru_b_r", "lru_w_i", "lru_b_i", "lru_lambda", "lru_w_out", "fox_w_in",
           "fox_b_f", "fox_q_gain", "fox_k_gain", "fox_w_out")


def kernel(x, mix_norm, mlp_norm, mlp_w1, mlp_w2, lru_w_in, lru_conv_w, lru_conv_b, lru_w_r, lru_b_r, lru_w_i, lru_b_i, lru_lambda, lru_w_out, fox_w_in, fox_b_f, fox_q_gain, fox_k_gain, fox_w_out, loss_target, m_mix_norm, m_mlp_norm, m_mlp_w1, m_mlp_w2, m_lru_w_in, m_lru_conv_w, m_lru_conv_b, m_lru_w_r, m_lru_b_r, m_lru_w_i, m_lru_b_i, m_lru_lambda, m_lru_w_out, m_fox_w_in, m_fox_b_f, m_fox_q_gain, m_fox_k_gain, m_fox_w_out, v_mix_norm, v_mlp_norm, v_mlp_w1, v_mlp_w2, v_lru_w_in, v_lru_conv_w, v_lru_conv_b, v_lru_w_r, v_lru_b_r, v_lru_w_i, v_lru_b_i, v_lru_lambda, v_lru_w_out, v_fox_w_in, v_fox_b_f, v_fox_q_gain, v_fox_k_gain, v_fox_w_out):
    w_in = dict(mix_norm=mix_norm, mlp_norm=mlp_norm, mlp_w1=mlp_w1, mlp_w2=mlp_w2,
                lru_w_in=lru_w_in, lru_conv_w=lru_conv_w, lru_conv_b=lru_conv_b, lru_w_r=lru_w_r,
                lru_b_r=lru_b_r, lru_w_i=lru_w_i, lru_b_i=lru_b_i, lru_lambda=lru_lambda,
                lru_w_out=lru_w_out, fox_w_in=fox_w_in, fox_b_f=fox_b_f, fox_q_gain=fox_q_gain,
                fox_k_gain=fox_k_gain, fox_w_out=fox_w_out)
    m_in = dict(mix_norm=m_mix_norm, mlp_norm=m_mlp_norm, mlp_w1=m_mlp_w1, mlp_w2=m_mlp_w2,
                lru_w_in=m_lru_w_in, lru_conv_w=m_lru_conv_w, lru_conv_b=m_lru_conv_b,
                lru_w_r=m_lru_w_r, lru_b_r=m_lru_b_r, lru_w_i=m_lru_w_i, lru_b_i=m_lru_b_i,
                lru_lambda=m_lru_lambda, lru_w_out=m_lru_w_out, fox_w_in=m_fox_w_in,
                fox_b_f=m_fox_b_f, fox_q_gain=m_fox_q_gain, fox_k_gain=m_fox_k_gain,
                fox_w_out=m_fox_w_out)
    v_in = dict(mix_norm=v_mix_norm, mlp_norm=v_mlp_norm, mlp_w1=v_mlp_w1, mlp_w2=v_mlp_w2,
                lru_w_in=v_lru_w_in, lru_conv_w=v_lru_conv_w, lru_conv_b=v_lru_conv_b,
                lru_w_r=v_lru_w_r, lru_b_r=v_lru_b_r, lru_w_i=v_lru_w_i, lru_b_i=v_lru_b_i,
                lru_lambda=v_lru_lambda, lru_w_out=v_lru_w_out, fox_w_in=v_fox_w_in,
                fox_b_f=v_fox_b_f, fox_q_gain=v_fox_q_gain, fox_k_gain=v_fox_k_gain,
                fox_w_out=v_fox_w_out)
    D = D_MODEL
    S = x.shape[1]
    x0, target = x[0], loss_target[0]
    me = 4 * lax.axis_index("x") + 2 * lax.axis_index("y") + lax.axis_index("c")

    def bf16(a):
        return a.astype(BF16)

    (lru_in_g,) = _exchange([bf16(lru_w_in[0])], [True], "gather_lru_in", two_level=True)
    gather_lru, tok = _exchange_start([bf16(lru_w_out[0]), lru_conv_w[0]], [True] * 2, lru_in_g,
                                      "gather_lru_start")
    gather_mlp0, tok = _exchange_start([bf16(mlp_w1[0]), bf16(mlp_w2[0])], [True] * 2, tok,
                                       "gather_mlp0_start", NEAR_PEERS)
    gather_fox, tok = _exchange_start([bf16(fox_w_in[0]), bf16(fox_w_out[0])], [True] * 2, tok,
                                      "gather_fox_start")
    gather_mlp1, tok = _exchange_start([bf16(mlp_w1[1]), bf16(mlp_w2[1])], [True] * 2, tok,
                                       "gather_mlp1_start", NEAR_PEERS)

    def pass_on(started, after, name):
        lands = _exchange_wait(started, [True] * 2, after, name + "_wait", NEAR_PEERS)
        return _forward_start(lands, after, name + "_pass_start")
    wr =_block_diag_pairs(lru_w_r[0]).astype(BF16)
    wi = _block_diag_pairs(lru_w_i[0]).astype(BF16)
    b_r, b_i = lru_b_r.reshape(1, D), lru_b_i.reshape(1, D)
    q_gain, k_gain = jnp.tile(fox_q_gain, (1, 2)), jnp.tile(fox_k_gain, (1, 2))
    b_f = jnp.pad(fox_b_f, ((0, 0), (0, LANES - N_HEADS)))
    g_mix0, g_mix1 = mix_norm[0:1] + tok[0, 0], mix_norm[1:2]
    g_mlp0, g_mlp1 = mlp_norm[0:1], mlp_norm[1:2]

    (u0,), h0 = _norm_matmul(x0, g_mix0, [lru_in_g], "lru_in_proj")
    lru_out_g, conv_g = _exchange_wait(gather_lru, [True] * 2, u0, "gather_lru_wait")
    lru_out_w = lru_out_g.reshape(D, D)
    conv_w = conv_g.transpose(1, 0, 2).reshape(CONV_WIDTH, D)
    y_lru, hs =_lru_fwd(u0, conv_w, lru_conv_b, wr, b_r, wi, b_i, lru_lambda, "lru_core")
    pass_mlp0, tok = pass_on(gather_mlp0, y_lru, "gather_mlp0")
    x1 = _matmul_res(y_lru, lru_out_w, x0, "lru_out_proj", tok)
    w1g0, w2g0 = _forward_wait(pass_mlp0, x1, "gather_mlp0_pass_wait")
    x2, h1, r1 = _mlp_fwd(x1, g_mlp0, w1g0, w2g0, "mlp0")
    fox_in_g, fox_out_g = _exchange_wait(gather_fox, [True] * 2, x2, "gather_fox_wait")
    fox_out_w = fox_out_g.reshape(D, D)
    fox_full = fox_in_g.transpose(1, 0, 2).reshape(D, 3 * D + N_HEADS)
    wqkv = fox_full[:, :3 * D].reshape(D, 3, D).transpose(1, 0, 2)
    wf = jnp.pad(fox_full[:, 3 * D:], ((0, 0), (0, LANES - N_HEADS)))[None]
    (u_qkv, f), h2 = _norm_matmul(x2, g_mix1, [wqkv, wf], "fox_in_proj")
    qn, kn, vb = _qk_prep(u_qkv, q_gain, k_gain, "fox_qk_norm")
    c_col = _forget_fwd(f, b_f, "fox_forget")
    c_row = c_col[:, :N_HEADS].T.reshape(N_CBLK, 2, S)
    o, lse = _attn_fwd(qn, kn, vb, c_row, "fox_attn")
    pass_mlp1, tok = pass_on(gather_mlp1, o, "gather_mlp1")
    x3 = _matmul_res(o, fox_out_w, x2, "fox_out_proj", tok)
    w1g1, w2g1 = _forward_wait(pass_mlp1, x3, "gather_mlp1_pass_wait")
    loss_local, dx4, h3, r3 = _mlp_fwd(x3, g_mlp1, w1g1, w2g1, "mlp1", target)

    dx3, dg_mlp1, da3 = _mlp_bwd(dx4, x3, g_mlp1, r3, w1g1, w2g1, "mlp1_bwd")
    dw1_1 = _matmul_tn(h3, da3, "mlp1_dw1", cols=2, col_blocks=N_DEV)
    dw2_1 = _matmul_tn(r3, dx4, "mlp1_dw2", rows=2, a_square=True).reshape(N_DEV, -1, D)
    grads_mlp1, tok = _exchange_start([dw1_1, dw2_1], [False] * 2, tok, "grads_mlp1_start")
    do = _matmul_nt(dx3, fox_out_w, "fox_out_bwd", BF16, tok)
    d_fox_out = _matmul_tn(o, dx3, "fox_out_dw").reshape(N_DEV, -1, D)
    dqn, dkn, dv, dc_row, rho = _attn_bwd(qn, kn, vb, do, o, lse, c_row, "fox_attn_bwd")
    duq, duk, dq_gain, dk_gain = _qk_bwd(u_qkv, dqn, dkn, q_gain, k_gain, "fox_qk_norm_bwd")
    dc_k = jnp.pad(dc_row.reshape(N_HEADS, S).T, ((0, 0), (0, LANES - N_HEADS)))
    df, db_f = _forget_bwd(dc_k, rho, f, b_f, "fox_forget_bwd")
    dx2, dg_mix1 = _proj_bwd([[duq, duk, dv], [df]], [wqkv, wf], x2, g_mix1, dx3, "fox_in_bwd")
    d_fox_in = jnp.concatenate(
        [_matmul_tn(h2, duq, "fox_in_dwq"), _matmul_tn(h2, duk, "fox_in_dwk"),
         _matmul_tn(h2, dv, "fox_in_dwv"), _matmul_tn(h2, df, "fox_in_dwf")[:, :N_HEADS]], axis=1)
    d_fox_in = d_fox_in.reshape(D, N_DEV, -1).transpose(1, 0, 2)
    grads_fox, tok = _exchange_start([d_fox_in, d_fox_out], [False] * 2, tok, "grads_fox_start")
    dx1, dg_mlp0, da1 = _mlp_bwd(dx2, x1, g_mlp0 + tok[0, 0], r1, w1g0, w2g0, "mlp0_bwd")
    dw1_0 = _matmul_tn(h1, da1, "mlp0_dw1", cols=2, col_blocks=N_DEV)
    dw2_0 = _matmul_tn(r1, dx2, "mlp0_dw2", rows=2, a_square=True).reshape(N_DEV, -1, D)
    grads_mlp0, tok = _exchange_start([dw1_0, dw2_0], [False] * 2, tok, "grads_mlp0_start")
    dy_lru = _matmul_nt(dx1, lru_out_w, "lru_out_bwd", F32, tok)
    d_lru_out = _matmul_tn(y_lru, dx1, "lru_out_dw").reshape(N_DEV, -1, D)
    dgp, dxb, d_conv_w, d_conv_b, d_b_r, d_b_i, d_lam, d_wr, d_wi = _lru_bwd(
        dy_lru, u0, hs, conv_w, lru_conv_b, wr, b_r, wi, b_i, lru_lambda, "lru_core_bwd")

    small_grads = dict(
        mlp_norm=jnp.concatenate([dg_mlp0, dg_mlp1], axis=0),
        lru_conv_b=d_conv_b, lru_w_r=_diag_pairs(d_wr), lru_b_r=d_b_r, lru_w_i=_diag_pairs(d_wi),
        lru_b_i=d_b_i, lru_lambda=d_lam, fox_b_f=db_f[:, :N_HEADS],
        fox_q_gain=dq_gain[:, :HEAD_DIM], fox_k_gain=dk_gain[:, :HEAD_DIM])
    small_partial = _pack([dg_mix1] + [small_grads[n] for n in SMALL] + [d_conv_w])
    grads_lru_out, tok = _exchange_start([d_lru_out, small_partial], [False, True], tok,
                                         "grads_lru_out_start")
    dx0, dg_mix0 = _proj_bwd([[dgp, dxb]], [lru_in_g], x0, mix_norm[0:1] + tok[0, 0], dx1,
                             "lru_in_bwd")
    d_lru_in = jnp.concatenate([_matmul_tn(h0, dgp, "lru_in_dw_gate", col_blocks=4),
                                _matmul_tn(h0, dxb, "lru_in_dw_x", col_blocks=4)], axis=0)
    grads_lru_in, tok = _exchange_start([d_lru_in, dg_mix0], [False, True], tok,
                                        "grads_lru_in_start")

    grads, deltas, new_m, new_v = {}, {}, {}, {}

    def update(name, parts):
        w, m, v = w_in[name], m_in[name], v_in[name]
        shape = w.shape
        stacked = (len(parts), -1, shape[-1])
        w3 = w.reshape(stacked)
        res = _reduce_adamw([p.reshape((N_DEV,) + w3.shape[1:]) for p in parts], w3,
                            m.reshape(stacked), v.reshape(stacked), "adamw_" + name)
        return [r.reshape(shape) for r in res]

    def store(name, res):
        grads[name], deltas[name], new_m[name], new_v[name] = res

    p_w1_1, p_w2_1 = _exchange_wait(grads_mlp1, [False] * 2, tok, "grads_mlp1_wait")
    p_fox_in, p_fox_out = _exchange_wait(grads_fox, [False] * 2, p_w1_1, "grads_fox_wait")
    store("fox_w_in", update("fox_w_in", [p_fox_in]))
    store("fox_w_out", update("fox_w_out", [p_fox_out]))
    p_w1_0, p_w2_0 = _exchange_wait(grads_mlp0, [False] * 2, grads["fox_w_out"], "grads_mlp0_wait")
    store("mlp_w1", update("mlp_w1", [p_w1_0, p_w1_1]))
    store("mlp_w2", update("mlp_w2", [p_w2_0, p_w2_1]))
    p_lru_out, p_small = _exchange_wait(grads_lru_out, [False, True], grads["mlp_w2"],
                                        "grads_lru_out_wait")
    store("lru_w_out", update("lru_w_out", [p_lru_out]))
    p_lru_in, p_mix0 = _exchange_wait(grads_lru_in, [False, True], grads["lru_w_out"],
                                      "grads_lru_in_wait")
    store("lru_w_in", update("lru_w_in", [p_lru_in]))

    mix0 = [r[0] for r in _reduce_adamw([p_mix0], mix_norm[None, 0:1], m_mix_norm[None, 0:1],
                                        v_mix_norm[None, 0:1], "adamw_mix0")]
    packed = lambda src, first: _pack([first] + [src[n] for n in SMALL]
                                      + [jnp.zeros((CONV_WIDTH, D))])[None]
    small_shapes = [(1, D)] + [w_in[n].shape for n in SMALL]
    n_small = sum(math.prod(s) for s in small_shapes)
    res_small = _reduce_adamw([p_small], packed(w_in, mix_norm[1:2]), packed(m_in, m_mix_norm[1:2]),
                              packed(v_in, v_mix_norm[1:2]), "adamw_small")
    for name, *vals in zip(("mix1",) + SMALL, *[_unpack(r, small_shapes) for r in res_small]):
        if name == "mix1":
            vals = [jnp.concatenate([r0, r1], axis=0) for r0, r1 in zip(mix0, vals)]
            name = "mix_norm"
        store(name, vals)
    conv_parts = p_small.reshape(N_DEV, -1)[:, n_small:n_small + CONV_WIDTH * D]
    conv_parts = conv_parts.reshape(N_DEV, CONV_WIDTH, N_DEV, LANES)
    conv_parts = lax.dynamic_index_in_dim(conv_parts, me, axis=2, keepdims=False)
    store("lru_conv_w", update("lru_conv_w", [conv_parts]))

    loss = lax.psum(loss_local[0, 0], ("x", "y", "c"))
    return (loss, dx0[None], *[grads[n] for n in WEIGHTS], *[deltas[n] for n in WEIGHTS],
            *[new_m[n] for n in WEIGHTS], *[new_v[n] for n in WEIGHTS])
```

```python
import functools
import math

import jax
import jax.numpy as jnp
from jax import lax
from jax.experimental import pallas as pl
from jax.experimental.pallas import tpu as pltpu

F32 = jnp.float32
BF16 = jnp.bfloat16

N_DEV = 8
D_MODEL = 1024
D_FF = 4096
N_HEADS = 16
HEAD_DIM = 64
LRU_BLOCK_DIM = 64
CONV_WIDTH = 4
LRU_C = 8.0
EPS = 1e-6
NEG_INF = -1e30
ATTN_SCALE = HEAD_DIM ** -0.5
LANES = 128
N_CBLK = D_MODEL // LANES
VMEM_LIMIT = 52 * 2 ** 20

ADAM_LR = 0.001
ADAM_B1 = 0.9
ADAM_B2 = 0.999
ADAM_EPS = 1e-08
ADAM_WD = 0.01
ADAM_STEP = 10

_NT = (((1,), (1,)), ((), ()))
_TN = (((0,), (0,)), ((), ()))


def _params(*sem):
    return pltpu.CompilerParams(dimension_semantics=sem, vmem_limit_bytes=VMEM_LIMIT)


def _resident(shape):
    zeros = (0,) * len(shape)
    return pl.BlockSpec(shape, lambda *_: zeros, pipeline_mode=pl.Buffered(1))


def _dot(a, b):
    return jnp.dot(a, b, preferred_element_type=F32)


def _dot_nt(a, b):
    return lax.dot_general(a, b, _NT, preferred_element_type=F32)


def _dot_tn(a, b):
    return lax.dot_general(a, b, _TN, preferred_element_type=F32)


def _sigmoid(x):
    return 1.0 / (1.0 + jnp.exp(-x))


def _log_sigmoid(x):
    return -(jnp.maximum(-x, 0.0) + jnp.log1p(jnp.exp(-jnp.abs(x))))


def _expm1(x):
    poly = x * (1.0 + x * (0.5 + x * (1.0 / 6.0 + x * (1.0 / 24.0 + x * (1.0 / 120.0)))))
    return jnp.where(jnp.abs(x) < 0.1, poly, jnp.exp(x) - 1.0)


_GELU_K = 0.7978845608028654


def _gelu(x):
    return 0.5 * x * (1.0 + jnp.tanh(_GELU_K * (x + 0.044715 * (x * x * x))))


def _gelu_grad(x):
    t = jnp.tanh(_GELU_K * (x + 0.044715 * (x * x * x)))
    return 0.5 * (1.0 + t) + 0.5 * x * (1.0 - t * t) * (_GELU_K * (1.0 + 3 * 0.044715 * x * x))


def _rms_scale(x):
    return lax.rsqrt(jnp.mean(x * x, axis=-1, keepdims=True) + EPS)


def _norm_bwd(dh, x, g):
    rs = _rms_scale(x)
    xhat = x * rs
    dxhat = dh * g
    dx = rs * (dxhat - xhat * jnp.mean(dxhat * xhat, axis=-1, keepdims=True))
    return dx, jnp.sum(dh * xhat, axis=0, keepdims=True)


def _token_tile(S, want):
    tm = min(S, want)
    assert S % tm == 0
    return tm


def _norm_matmul(x, g, ws, name, tm=256):
    S, D = x.shape
    tm = _token_tile(S, tm)
    n = len(ws)

    def body(x_ref, g_ref, *refs):
        w_refs, o_refs, h_ref = refs[:n], refs[n:2 * n], refs[2 * n]
        xv = x_ref[...]
        h = (xv * _rms_scale(xv) * g_ref[...]).astype(BF16)
        h_ref[...] = h
        for w_ref, o_ref in zip(w_refs, o_refs):
            nb, _, nw = w_ref.shape
            for d in range(nb):
                o_ref[:, d * nw:(d + 1) * nw] = _dot(h, w_ref[d])

    widths = [w.shape[0] * w.shape[2] for w in ws]
    outs = pl.pallas_call(
        body, name=name, grid=(S // tm,),
        in_specs=[pl.BlockSpec((tm, D), lambda i: (i, 0)), _resident((1, D))]
        + [_resident(w.shape) for w in ws],
        out_specs=[pl.BlockSpec((tm, n_), lambda i: (i, 0)) for n_ in widths]
        + [pl.BlockSpec((tm, D), lambda i: (i, 0))],
        out_shape=[jax.ShapeDtypeStruct((S, n_), F32) for n_ in widths]
        + [jax.ShapeDtypeStruct((S, D), BF16)],
        compiler_params=_params("parallel"),
    )(x, g, *ws)
    return outs[:n], outs[n]


def _matmul_res(a, w, res, name, after, tm=256):
    S, K = a.shape
    N = w.shape[1]
    tm = _token_tile(S, tm)

    def body(a_ref, w_ref, r_ref, after_ref, o_ref):
        o_ref[...] = r_ref[...] + _dot(a_ref[...], w_ref[...])

    return pl.pallas_call(
        body, name=name, grid=(S // tm,),
        in_specs=[pl.BlockSpec((tm, K), lambda i: (i, 0)), _resident((K, N)),
                  pl.BlockSpec((tm, N), lambda i: (i, 0)), pl.BlockSpec(memory_space=pl.ANY)],
        out_specs=pl.BlockSpec((tm, N), lambda i: (i, 0)),
        out_shape=jax.ShapeDtypeStruct((S, N), F32),
        compiler_params=_params("parallel"),
    )(a, w, res, after)


def _matmul_nt(a, w, name, out_dtype, after, tm=256):
    S, N = a.shape
    K = w.shape[0]
    tm = _token_tile(S, tm)

    def body(a_ref, w_ref, after_ref, o_ref):
        o_ref[...] = _dot_nt(a_ref[...].astype(BF16), w_ref[...]).astype(out_dtype)

    return pl.pallas_call(
        body, name=name, grid=(S // tm,),
        in_specs=[pl.BlockSpec((tm, N), lambda i: (i, 0)), _resident((K, N)),
                  pl.BlockSpec(memory_space=pl.ANY)],
        out_specs=pl.BlockSpec((tm, K), lambda i: (i, 0)),
        out_shape=jax.ShapeDtypeStruct((S, K), out_dtype),
        compiler_params=_params("parallel"),
    )(a, w, after)


def _proj_bwd(a_lists, w_list, x, g, res, name, tm=256):
    S, D = x.shape
    tm = _token_tile(S, tm)
    a_list = [a for group in a_lists for a in group]
    n, n_w = len(a_list), len(w_list)

    def body(*refs):
        a_refs, w_refs = list(refs[:n]), refs[n:n + n_w]
        x_ref, g_ref, r_ref, dx_ref, dg_ref = refs[n + n_w:]
        dh = jnp.zeros((tm, D), F32)
        for group, w_ref in zip(a_lists, w_refs):
            nw = w_ref.shape[2]
            d = 0
            for _ in group:
                a_ref = a_refs.pop(0)
                for j in range(a_ref.shape[1] // nw):
                    dh = dh + _dot_nt(a_ref[:, j * nw:(j + 1) * nw].astype(BF16), w_ref[d])
                    d += 1
        dx, dg = _norm_bwd(dh, x_ref[...], g_ref[...])
        dx_ref[...] = r_ref[...] + dx

        @pl.when(pl.program_id(0) == 0)
        def _():
            dg_ref[...] = jnp.zeros_like(dg_ref)
        dg_ref[...] += dg

    tok = lambda width: pl.BlockSpec((tm, width), lambda i: (i, 0))
    return pl.pallas_call(
        body, name=name, grid=(S // tm,),
        in_specs=[tok(a.shape[1]) for a in a_list] + [_resident(w.shape) for w in w_list]
        + [tok(D), _resident((1, D)), tok(D)],
        out_specs=[tok(D), pl.BlockSpec((1, D), lambda i: (0, 0))],
        out_shape=[jax.ShapeDtypeStruct((S, D), F32), jax.ShapeDtypeStruct((1, D), F32)],
        compiler_params=_params("arbitrary"),
    )(*a_list, *w_list, x, g, res)


def _matmul_tn(a, b, name, rows=1, cols=1, col_blocks=None, a_square=False, tm=1024):
    S, K = a.shape
    N = b.shape[1]
    tm = _token_tile(S, tm)
    n_tok = S // tm
    kr, nc = K // rows, N // cols

    def body(a_ref, b_ref, o_ref, acc_ref):
        av = a_ref[...]
        if a_square:
            av = av.astype(F32)
            av = av * av
        part = _dot_tn(av.astype(BF16), b_ref[...].astype(BF16))
        step = pl.program_id(2)

        @pl.when(step == 0)
        def _():
            acc_ref[...] = part

        @pl.when(step > 0)
        def _():
            acc_ref[...] += part

        @pl.when(step == n_tok - 1)
        def _():
            if col_blocks is None:
                o_ref[...] = acc_ref[...].astype(BF16)
            else:
                nw = N // col_blocks
                for d in range(col_blocks // cols):
                    o_ref[d] = acc_ref[:, d * nw:(d + 1) * nw].astype(BF16)

    if col_blocks is None:
        out_spec = pl.BlockSpec((kr, nc), lambda r, c, i: (r, c))
        out_shape = jax.ShapeDtypeStruct((K, N), BF16)
    else:
        assert rows == 1 and col_blocks % cols == 0
        per = col_blocks // cols
        out_spec = pl.BlockSpec((per, K, N // col_blocks), lambda r, c, i: (c, 0, 0))
        out_shape = jax.ShapeDtypeStruct((col_blocks, K, N // col_blocks), BF16)
    return pl.pallas_call(
        body, name=name, grid=(rows, cols, n_tok),
        in_specs=[pl.BlockSpec((tm, kr), lambda r, c, i: (i, r)),
                  pl.BlockSpec((tm, nc), lambda r, c, i: (i, c))],
        out_specs=out_spec, out_shape=out_shape,
        scratch_shapes=[pltpu.VMEM((kr, nc), F32)],
        compiler_params=_params("parallel", "parallel", "arbitrary"),
    )(a, b)


def _mlp_fwd(x, g, w1, w2, name, target=None, tm=256):
    S, D = x.shape
    nb, _, fb = w1.shape
    tm = _token_tile(S, tm)
    with_loss = target is not None

    def body(x_ref, g_ref, w1_ref, w2_ref, *refs):
        h_ref, r_ref = refs[-2:]
        xv = x_ref[...]
        h = (xv * _rms_scale(xv) * g_ref[...]).astype(BF16)
        h_ref[...] = h
        acc = xv
        for d in range(nb):
            r = jnp.maximum(_dot(h, w1_ref[d]), 0.0)
            r_ref[:, d * fb:(d + 1) * fb] = r.astype(BF16)
            acc = acc + _dot((r * r).astype(BF16), w2_ref[d])
        if not with_loss:
            refs[0][...] = acc
            return
        t_ref, loss_ref, dy_ref = refs[:3]
        err = acc - t_ref[...]
        dy_ref[...] = err / D

        @pl.when(pl.program_id(0) == 0)
        def _():
            loss_ref[...] = jnp.zeros_like(loss_ref)
        row_loss = jnp.mean(err * err, axis=1, keepdims=True)
        loss_ref[...] += 0.5 * jnp.sum(row_loss, axis=0, keepdims=True)

    tok = lambda width: pl.BlockSpec((tm, width), lambda i: (i, 0))
    saved_specs = [tok(D), tok(nb * fb)]
    saved_shapes = [jax.ShapeDtypeStruct((S, D), BF16), jax.ShapeDtypeStruct((S, nb * fb), BF16)]
    wide = jax.ShapeDtypeStruct((S, D), F32)
    if with_loss:
        head_specs = [pl.BlockSpec((1, 1), lambda i: (0, 0)), tok(D)]
        head_shapes = [jax.ShapeDtypeStruct((1, 1), F32), wide]
    else:
        head_specs, head_shapes = [tok(D)], [wide]
    return pl.pallas_call(
        body, name=name, grid=(S // tm,),
        in_specs=[tok(D), _resident((1, D)), _resident(w1.shape), _resident(w2.shape)]
        + ([tok(D)] if with_loss else []),
        out_specs=head_specs + saved_specs, out_shape=head_shapes + saved_shapes,
        compiler_params=_params("arbitrary" if with_loss else "parallel"),
    )(x, g, w1, w2, *([target] if with_loss else []))


def _mlp_bwd(dout, x, g, r, w1, w2, name, tm=256):
    S, D = x.shape
    nb, _, fb = w1.shape
    tm = _token_tile(S, tm)

    def body(do_ref, x_ref, g_ref, r_ref, w1_ref, w2_ref, dx_ref, dg_ref, da_ref):
        dov = do_ref[...]
        dob = dov.astype(BF16)
        dh = jnp.zeros((tm, D), F32)
        for d in range(nb):
            dz = _dot_nt(dob, w2_ref[d])
            da = (dz * (2.0 * r_ref[:, d * fb:(d + 1) * fb].astype(F32))).astype(BF16)
            da_ref[:, d * fb:(d + 1) * fb] = da
            dh = dh + _dot_nt(da, w1_ref[d])
        dx, dg = _norm_bwd(dh, x_ref[...], g_ref[...])
        dx_ref[...] = dov + dx

        @pl.when(pl.program_id(0) == 0)
        def _():
            dg_ref[...] = jnp.zeros_like(dg_ref)
        dg_ref[...] += dg

    tok = lambda width: pl.BlockSpec((tm, width), lambda i: (i, 0))
    return pl.pallas_call(
        body, name=name, grid=(S // tm,),
        in_specs=[tok(D), tok(D), _resident((1, D)), tok(nb * fb), _resident(w1.shape),
                  _resident(w2.shape)],
        out_specs=[tok(D), pl.BlockSpec((1, D), lambda i: (0, 0)), tok(nb * fb)],
        out_shape=[jax.ShapeDtypeStruct((S, D), F32), jax.ShapeDtypeStruct((1, D), F32),
                   jax.ShapeDtypeStruct((S, nb * fb), BF16)],
        compiler_params=_params("arbitrary"),
    )(dout, x, g, r, w1, w2)


def _scan_chunk(a, b, row, T, reverse):
    s = 1
    while s < T:
        if reverse:
            keep, shift = row < T - s, T - s
        else:
            keep, shift = row >= s, s
        a_sh = jnp.where(keep, pltpu.roll(a, shift, 0), 1.0)
        b_sh = jnp.where(keep, pltpu.roll(b, shift, 0), 0.0)
        b = a * b_sh + b
        a = a * a_sh
        s *= 2
    return a, b


def _row_of(x, row, r):
    return jnp.sum(jnp.where(row == r, x, 0.0), axis=0, keepdims=True)


def _shift_down(x, prev, row, k):
    if k == 0:
        return x
    return jnp.where(row < k, pltpu.roll(prev, k, 0), pltpu.roll(x, k, 0))


def _shift_up(x, nxt, row, k, T):
    if k == 0:
        return x
    return jnp.where(row < T - k, pltpu.roll(x, T - k, 0), pltpu.roll(nxt, T - k, 0))


def _lru_gates(xb, prev_xb, row, cw_ref, cb, wr, br, wi, bi, ls):
    xc = cb + cw_ref[pl.ds(0, 1), :] * _shift_down(xb, prev_xb, row, 3)
    for k in (2, 1, 0):
        xc = xc + cw_ref[pl.ds(3 - k, 1), :] * _shift_down(xb, prev_xb, row, k)
    xcb = xc.astype(BF16)
    r = _sigmoid(_dot(xcb, wr) + br)
    i = _sigmoid(_dot(xcb, wi) + bi)
    la = (LRU_C * r) * ls
    a = jnp.exp(la)
    m = jnp.sqrt(-_expm1(2.0 * la))
    return xc, xcb, r, i, a, m


def _lru_specs(S):
    col = lambda off: pl.BlockSpec((S, LANES), lambda j: (0, j + off))
    vec = pl.BlockSpec((1, LANES), lambda j: (0, j))
    mat = pl.BlockSpec((None, LANES, LANES), lambda j: (j, 0, 0))
    cwm = pl.BlockSpec((CONV_WIDTH, LANES), lambda j: (0, j))
    return col, vec, mat, cwm


def _lru_fwd(u, conv_w, conv_b, wr, br, wi, bi, lam, name):
    S = u.shape[0]
    T = _token_tile(S, 512)
    col, vec, mat, cwm = _lru_specs(S)

    def body(gp_ref, xb_ref, cw_ref, cb_ref, wr_ref, br_ref, wi_ref, bi_ref, lam_ref,
             y_ref, hs_ref):
        row = lax.broadcasted_iota(jnp.int32, (T, LANES), 0)
        ls = _log_sigmoid(lam_ref[...])
        cb, br, bi = cb_ref[...], br_ref[...], bi_ref[...]
        wr, wi = wr_ref[...], wi_ref[...]

        def chunk(ci, carry):
            prev_xb, hc = carry
            rows = pl.ds(pl.multiple_of(ci * T, T), T)
            xb = xb_ref[rows, :]
            xc, _, _, i, a, m = _lru_gates(xb, prev_xb, row, cw_ref, cb, wr, br, wi, bi, ls)
            ca, cbv = _scan_chunk(a, m * (i * xc), row, T, reverse=False)
            h = ca * hc + cbv
            hs_ref[rows, :] = h
            y_ref[rows, :] = (_gelu(gp_ref[rows, :]) * h).astype(BF16)
            return xb, _row_of(h, row, T - 1)

        lax.fori_loop(0, S // T, chunk,
                      (jnp.zeros((T, LANES), F32), jnp.zeros((1, LANES), F32)))

    return pl.pallas_call(
        body, name=name, grid=(N_CBLK,),
        in_specs=[col(0), col(N_CBLK), cwm, vec, mat, vec, mat, vec, vec],
        out_specs=[col(0), col(0)],
        out_shape=[jax.ShapeDtypeStruct((S, D_MODEL), BF16), jax.ShapeDtypeStruct((S, D_MODEL), F32)],
        compiler_params=_params("parallel"),
    )(u, u, conv_w, conv_b, wr, br, wi, bi, lam)


def _lru_bwd(dy, u, hs, conv_w, conv_b, wr, br, wi, bi, lam, name):
    S = u.shape[0]
    T = _token_tile(S, 512)
    n_chunk = S // T
    col, vec, mat, cwm = _lru_specs(S)

    def body(dy_ref, gp_ref, xb_ref, hs_ref, cw_ref, cb_ref, wr_ref, br_ref, wi_ref, bi_ref,
             lam_ref, dgp_ref, dxb_ref, dcw_ref, dcb_ref, dbr_ref, dbi_ref, dlam_ref, dwr_ref,
             dwi_ref):
        row = lax.broadcasted_iota(jnp.int32, (T, LANES), 0)
        lam = lam_ref[...]
        ls = _log_sigmoid(lam)
        cb, br, bi = cb_ref[...], br_ref[...], bi_ref[...]
        wr, wi = wr_ref[...], wi_ref[...]
        for ref in (dcw_ref, dcb_ref, dbr_ref, dbi_ref, dlam_ref, dwr_ref, dwi_ref):
            ref[...] = jnp.zeros_like(ref)

        def chunk(it, carry):
            g_next, dxc_next = carry
            ci = n_chunk - 1 - it
            rows = pl.ds(pl.multiple_of(ci * T, T), T)
            before = pl.ds(pl.multiple_of(jnp.maximum(ci - 1, 0) * T, T), T)
            first = ci == 0
            xb = xb_ref[rows, :]
            prev_xb = jnp.where(first, 0.0, xb_ref[before, :])
            xc, xcb, r, i, a, m = _lru_gates(xb, prev_xb, row, cw_ref, cb, wr, br, wi, bi, ls)
            h = hs_ref[rows, :]
            h_prev = _shift_down(h, jnp.where(first, 0.0, hs_ref[before, :]), row, 1)
            gp = gp_ref[rows, :]
            dyv = dy_ref[rows, :]
            dgp_ref[rows, :] = (dyv * h * _gelu_grad(gp)).astype(BF16)
            dh = dyv * _gelu(gp)
            ca, cbv = _scan_chunk(a, a * dh, row, T, reverse=True)
            gp_acc = ca * g_next + cbv
            g = dh + jnp.where(row < T - 1, pltpu.roll(gp_acc, T - 1, 0), g_next)
            da = g * h_prev - (g * (i * xc)) * a / m
            dla = da * a
            dlam_ref[...] += jnp.sum(dla * (LRU_C * r), axis=0, keepdims=True)
            dpr = (dla * (LRU_C * ls)) * r * (1.0 - r)
            dpi = (g * m * xc) * i * (1.0 - i)
            dbr_ref[...] += jnp.sum(dpr, axis=0, keepdims=True)
            dbi_ref[...] += jnp.sum(dpi, axis=0, keepdims=True)
            dprb, dpib = dpr.astype(BF16), dpi.astype(BF16)
            dwr_ref[...] += _dot_tn(xcb, dprb)
            dwi_ref[...] += _dot_tn(xcb, dpib)
            dxc = g * m * i + _dot_nt(dprb, wr) + _dot_nt(dpib, wi)
            dcb_ref[...] += jnp.sum(dxc, axis=0, keepdims=True)
            dxb = jnp.zeros((T, LANES), F32)
            for k in range(CONV_WIDTH):
                tap = pl.ds(CONV_WIDTH - 1 - k, 1)
                dcw_ref[tap, :] += jnp.sum(dxc * _shift_down(xb, prev_xb, row, k), axis=0,
                                           keepdims=True)
                dxb = dxb + cw_ref[tap, :] * _shift_up(dxc, dxc_next, row, k, T)
            dxb_ref[rows, :] = dxb.astype(BF16)
            return _row_of(gp_acc, row, 0), dxc

        lax.fori_loop(0, n_chunk, chunk,
                      (jnp.zeros((1, LANES), F32), jnp.zeros((T, LANES), F32)))
        dlam_ref[...] = dlam_ref[...] * _sigmoid(-lam)

    vec_out = jax.ShapeDtypeStruct((1, D_MODEL), F32)
    mat_out = jax.ShapeDtypeStruct((N_CBLK, LANES, LANES), F32)
    return pl.pallas_call(
        body, name=name, grid=(N_CBLK,),
        in_specs=[col(0), col(0), col(N_CBLK), col(0), cwm, vec, mat, vec, mat, vec, vec],
        out_specs=[col(0), col(0), cwm, vec, vec, vec, vec, mat, mat],
        out_shape=[jax.ShapeDtypeStruct((S, D_MODEL), BF16), jax.ShapeDtypeStruct((S, D_MODEL), BF16),
                   jax.ShapeDtypeStruct((CONV_WIDTH, D_MODEL), F32),
                   vec_out, vec_out, vec_out, vec_out, mat_out, mat_out],
        compiler_params=_params("parallel"),
    )(dy, u, u, hs, conv_w, conv_b, wr, br, wi, bi, lam)


def _head_group_matrix(value):
    r = lax.broadcasted_iota(jnp.int32, (LANES, LANES), 0) // HEAD_DIM
    c = lax.broadcasted_iota(jnp.int32, (LANES, LANES), 1) // HEAD_DIM
    return jnp.where(r == c, value, 0.0).astype(BF16)


def _group_dot(x, p):
    hi = x.astype(BF16)
    lo = (x - hi.astype(F32)).astype(BF16)
    return _dot(hi, p) + _dot(lo, p)


def _head_mean(x, p):
    return _group_dot(x, p)


def _qk_prep(u, q_gain, k_gain, name, tm=256):
    S = u.shape[0]
    tm = _token_tile(S, tm)

    def body(q_ref, k_ref, v_ref, qg_ref, kg_ref, qn_ref, kn_ref, vb_ref):
        p = _head_group_matrix(1.0 / HEAD_DIM)
        for j in range(N_CBLK):
            cl = slice(j * LANES, (j + 1) * LANES)
            for x_ref, g_ref, o_ref, scale in ((q_ref, qg_ref, qn_ref, ATTN_SCALE),
                                               (k_ref, kg_ref, kn_ref, 1.0)):
                xv = x_ref[:, cl]
                rs = lax.rsqrt(_head_mean(xv * xv, p) + EPS)
                o_ref[:, cl] = (xv * rs * g_ref[...]).astype(BF16) * scale
        vb_ref[...] = v_ref[...].astype(BF16)

    blk = lambda off: pl.BlockSpec((tm, D_MODEL), lambda i: (i, off))
    out = jax.ShapeDtypeStruct((S, D_MODEL), BF16)
    return pl.pallas_call(
        body, name=name, grid=(S // tm,),
        in_specs=[blk(0), blk(1), blk(2), _resident((1, LANES)), _resident((1, LANES))],
        out_specs=[blk(0), blk(0), blk(0)],
        out_shape=[out, out, out],
        compiler_params=_params("parallel"),
    )(u, u, u, q_gain, k_gain)


def _qk_bwd(u, dqn, dkn, q_gain, k_gain, name, tm=256):
    S = u.shape[0]
    tm = _token_tile(S, tm)

    def body(q_ref, k_ref, dqn_ref, dkn_ref, qg_ref, kg_ref, dq_ref, dk_ref, dqg_ref, dkg_ref):
        p = _head_group_matrix(1.0 / HEAD_DIM)
        for x_ref, dn_ref, g_ref, dx_ref, dg_ref, scale in (
                (q_ref, dqn_ref, qg_ref, dq_ref, dqg_ref, ATTN_SCALE),
                (k_ref, dkn_ref, kg_ref, dk_ref, dkg_ref, 1.0)):
            dg = jnp.zeros((1, LANES), F32)
            for j in range(N_CBLK):
                cl = slice(j * LANES, (j + 1) * LANES)
                xv, dn = x_ref[:, cl], dn_ref[:, cl] * scale
                rs = lax.rsqrt(_head_mean(xv * xv, p) + EPS)
                xhat = xv * rs
                dxhat = dn * g_ref[...]
                dx_ref[:, cl] = (rs * (dxhat - xhat * _head_mean(dxhat * xhat, p))).astype(BF16)
                dg = dg + jnp.sum(dn * xhat, axis=0, keepdims=True)

            @pl.when(pl.program_id(0) == 0)
            def _():
                dg_ref[...] = jnp.zeros_like(dg_ref)
            dg_ref[...] += dg

            @pl.when(pl.program_id(0) == S // tm - 1)
            def _():
                dg_ref[...] += pltpu.roll(dg_ref[...], HEAD_DIM, 1)

    blk = lambda off: pl.BlockSpec((tm, D_MODEL), lambda i: (i, off))
    acc = pl.BlockSpec((1, LANES), lambda i: (0, 0))
    out = jax.ShapeDtypeStruct((S, D_MODEL), BF16)
    vec = jax.ShapeDtypeStruct((1, LANES), F32)
    return pl.pallas_call(
        body, name=name, grid=(S // tm,),
        in_specs=[blk(0), blk(1), blk(0), blk(0), _resident((1, LANES)), _resident((1, LANES))],
        out_specs=[blk(0), blk(0), acc, acc],
        out_shape=[out, out, vec, vec],
        compiler_params=_params("arbitrary"),
    )(u, u, dqn, dkn, q_gain, k_gain)


def _forget_fwd(f, b_f, name):
    S = f.shape[0]
    T = _token_tile(S, 256)

    def body(f_ref, b_ref, c_ref):
        row = lax.broadcasted_iota(jnp.int32, (T, LANES), 0)
        ones = jnp.ones((T, LANES), F32)
        bias = b_ref[...]

        def chunk(ci, carry):
            rows = pl.ds(pl.multiple_of(ci * T, T), T)
            _, c = _scan_chunk(ones, _log_sigmoid(f_ref[rows, :] + bias), row, T, reverse=False)
            c = c + carry
            c_ref[rows, :] = c
            return _row_of(c, row, T - 1)

        lax.fori_loop(0, S // T, chunk, jnp.zeros((1, LANES), F32))

    return pl.pallas_call(
        body, name=name,
        in_specs=[pl.BlockSpec(memory_space=pltpu.VMEM)] * 2,
        out_specs=pl.BlockSpec(memory_space=pltpu.VMEM),
        out_shape=jax.ShapeDtypeStruct((S, LANES), F32),
        compiler_params=pltpu.CompilerParams(vmem_limit_bytes=VMEM_LIMIT),
    )(f, b_f)


def _forget_bwd(dc_k, rho, f, b_f, name):
    S = f.shape[0]
    T = _token_tile(S, 256)
    n_chunk = S // T

    def body(dck_ref, rho_ref, f_ref, b_ref, df_ref, db_ref):
        row = lax.broadcasted_iota(jnp.int32, (T, LANES), 0)
        ones = jnp.ones((T, LANES), F32)
        bias = b_ref[...]
        pick = (lax.broadcasted_iota(jnp.int32, (D_MODEL, LANES), 0)
                == HEAD_DIM * lax.broadcasted_iota(jnp.int32, (D_MODEL, LANES), 1))
        pick = jnp.where(pick, 1.0, 0.0).astype(BF16)

        def chunk(it, carry):
            tail, db = carry
            rows = pl.ds(pl.multiple_of((n_chunk - 1 - it) * T, T), T)
            dc = dck_ref[rows, :] + _group_dot(rho_ref[rows, :], pick)
            _, dlf = _scan_chunk(ones, dc, row, T, reverse=True)
            dlf = dlf + tail
            df = dlf * _sigmoid(-(f_ref[rows, :] + bias))
            df_ref[rows, :] = df
            return _row_of(dlf, row, 0), db + jnp.sum(df, axis=0, keepdims=True)

        zero = jnp.zeros((1, LANES), F32)
        _, db = lax.fori_loop(0, n_chunk, chunk, (zero, zero))
        db_ref[...] = db

    return pl.pallas_call(
        body, name=name,
        in_specs=[pl.BlockSpec(memory_space=pltpu.VMEM)] * 4,
        out_specs=[pl.BlockSpec(memory_space=pltpu.VMEM)] * 2,
        out_shape=[jax.ShapeDtypeStruct((S, LANES), F32), jax.ShapeDtypeStruct((1, LANES), F32)],
        compiler_params=pltpu.CompilerParams(vmem_limit_bytes=VMEM_LIMIT),
    )(dc_k, rho, f, b_f)


ATTN_TILE = 512
ATTN_ROWS_FWD = 32


def _attn_tiles(S):
    t = _token_tile(S, ATTN_TILE)
    return t, S // t


def _causal(T):
    return (lax.broadcasted_iota(jnp.int32, (T, T), 1)
            <= lax.broadcasted_iota(jnp.int32, (T, T), 0))


def _attn_fwd(qs_, kn, vb, c_row, name):
    S = qs_.shape[0]
    T, n_t = _attn_tiles(S)
    RB = min(T, ATTN_ROWS_FWD)

    def body(q_ref, k_ref, v_ref, cr_ref, o_ref, lse_ref, sa_ref, sb_ref, p_ref, m_ref, l_ref,
             acc_ref, a_ref):
        qi = pl.program_id(1)
        lanes = [slice(h2 * HEAD_DIM, (h2 + 1) * HEAD_DIM) for h2 in range(2)]
        col = lax.broadcasted_iota(jnp.int32, (RB, T), 1)
        row = lax.broadcasted_iota(jnp.int32, (RB, T), 0)
        m_ref[...] = jnp.full(m_ref.shape, NEG_INF, F32)
        l_ref[...] = jnp.zeros_like(l_ref)
        acc_ref[...] = jnp.zeros_like(acc_ref)

        def logits_into(s_ref, kj):
            ks = pl.ds(pl.multiple_of(kj * T, T), T)
            for h2, hl in enumerate(lanes):
                s_ref[h2] = _dot_nt(q_ref[:, hl], k_ref[ks, hl]) - cr_ref[h2:h2 + 1, ks]

        def consume(s_ref, kj, masked):
            ks = pl.ds(pl.multiple_of(kj * T, T), T)
            for h2, hl in enumerate(lanes):
                blocks = [slice(i * RB, (i + 1) * RB) for i in range(T // RB)]

                def logits(i, rows):
                    s = s_ref[h2, rows, :]
                    return jnp.where(col <= row + i * RB, s, NEG_INF) if masked else s

                wide = lambda x: jnp.broadcast_to(x, (RB, LANES))
                for i, rows in enumerate(blocks):
                    mx = wide(jnp.max(logits(i, rows), axis=1, keepdims=True))
                    a_ref[h2, rows, :] = m_ref[h2, rows, :]
                    m_ref[h2, rows, :] = jnp.maximum(m_ref[h2, rows, :], mx)
                for i, rows in enumerate(blocks):
                    m_new = m_ref[h2, rows, :]
                    p = jnp.exp(logits(i, rows) - jnp.tile(m_new, (1, T // LANES)))
                    alpha = jnp.exp(a_ref[h2, rows, :] - m_new)
                    a_ref[h2, rows, :] = alpha
                    l_ref[h2, rows, :] = (alpha * l_ref[h2, rows, :]
                                          + wide(jnp.sum(p, axis=1, keepdims=True)))
                    p_ref[h2, rows, :] = p.astype(BF16)
                acc_ref[h2] = (a_ref[h2, :, :HEAD_DIM] * acc_ref[h2]
                               + _dot(p_ref[h2], v_ref[ks, hl]))

        logits_into(sa_ref, 0)

        def pair(i, _):
            logits_into(sb_ref, 2 * i + 1)
            consume(sa_ref, 2 * i, False)
            logits_into(sa_ref, 2 * i + 2)
            consume(sb_ref, 2 * i + 1, False)
            return 0

        lax.fori_loop(0, qi // 2, pair, 0)

        @pl.when(qi % 2 == 1)
        def _():
            logits_into(sb_ref, qi)
            consume(sa_ref, qi - 1, False)
            consume(sb_ref, qi, True)

        @pl.when(qi % 2 == 0)
        def _():
            consume(sa_ref, qi, True)

        for h2, hl in enumerate(lanes):
            o_ref[:, hl] = (acc_ref[h2] / l_ref[h2, :, :HEAD_DIM]).astype(BF16)
            lse_ref[:, hl] = m_ref[h2, :, :HEAD_DIM] + jnp.log(l_ref[h2, :, :HEAD_DIM])

    qblk = pl.BlockSpec((T, LANES), lambda h, i: (i, h))
    kv = pl.BlockSpec((S, LANES), lambda h, i: (0, h))
    return pl.pallas_call(
        body, name=name, grid=(N_CBLK, n_t),
        in_specs=[qblk, kv, kv, pl.BlockSpec((None, 2, S), lambda h, i: (h, 0, 0))],
        out_specs=[qblk, qblk],
        out_shape=[jax.ShapeDtypeStruct((S, D_MODEL), BF16),
                   jax.ShapeDtypeStruct((S, D_MODEL), F32)],
        scratch_shapes=[pltpu.VMEM((2, T, T), F32), pltpu.VMEM((2, T, T), F32),
                        pltpu.VMEM((2, T, T), BF16),
                        pltpu.VMEM((2, T, LANES), F32), pltpu.VMEM((2, T, LANES), F32),
                        pltpu.VMEM((2, T, HEAD_DIM), F32), pltpu.VMEM((2, T, LANES), F32)],
        compiler_params=_params("parallel", "parallel"),
    )(qs_, kn, vb, c_row)


def _attn_bwd(qs_, kn, vb, do, o, lse, c_row, name):
    S = qs_.shape[0]
    T, n_t = _attn_tiles(S)

    def body(q_ref, k_ref, v_ref, do_ref, o_ref, lse_ref, cr_ref,
             dq_ref, dk_ref, dv_ref, dc_ref, rho_ref, dd_ref):
        kj = pl.program_id(1)
        causal = _causal(T)
        lanes = [slice(h2 * HEAD_DIM, (h2 + 1) * HEAD_DIM) for h2 in range(2)]
        ones = [slice(h2 * HEAD_DIM, h2 * HEAD_DIM + 1) for h2 in range(2)]

        @pl.when(kj == 0)
        def _():
            dq_ref[...] = jnp.zeros_like(dq_ref)
            rho_ref[...] = jnp.zeros_like(rho_ref)
            p_sum = _head_group_matrix(1.0)

            def fill(ci, _):
                rows = pl.ds(pl.multiple_of(ci * T, T), T)
                dd_ref[rows, :] = _group_dot(do_ref[rows, :].astype(F32) * o_ref[rows, :].astype(F32),
                                             p_sum)
                return 0

            lax.fori_loop(0, n_t, fill, 0)

        kh = [k_ref[:, hl] for hl in lanes]
        vh = [v_ref[:, hl] for hl in lanes]
        ck = [cr_ref[h2:h2 + 1, :] for h2 in range(2)]

        def step(qi, carry, masked):
            qs = pl.ds(pl.multiple_of(qi * T, T), T)
            out = []
            for h2, hl in enumerate(lanes):
                dk, dv, dc = carry[h2]
                qh, doh = q_ref[qs, hl], do_ref[qs, hl]
                s = _dot_nt(qh, kh[h2]) - ck[h2]
                if masked:
                    s = jnp.where(causal, s, NEG_INF)
                p = jnp.exp(s - lse_ref[qs, ones[h2]])
                ds = p * (_dot_nt(doh, vh[h2]) - dd_ref[qs, ones[h2]])
                dsb = ds.astype(BF16)
                dq_ref[qs, hl] += _dot(dsb, kh[h2])
                rho_ref[qs, hl] += jnp.broadcast_to(jnp.sum(ds, axis=1, keepdims=True),
                                                    (T, HEAD_DIM))
                out.append((dk + _dot_tn(dsb, qh), dv + _dot_tn(p.astype(BF16), doh),
                            dc - jnp.sum(ds, axis=0, keepdims=True)))
            return tuple(out)

        init = tuple((jnp.zeros((T, HEAD_DIM), F32), jnp.zeros((T, HEAD_DIM), F32),
                      jnp.zeros((1, T), F32)) for _ in lanes)
        carry = step(kj, init, True)
        carry = lax.fori_loop(kj + 1, n_t, lambda qi, c: step(qi, c, False), carry)
        for h2, ((dk, dv, dc), hl) in enumerate(zip(carry, lanes)):
            dk_ref[:, hl] = dk
            dv_ref[:, hl] = dv.astype(BF16)
            dc_ref[h2:h2 + 1, :] = dc

    kblk = pl.BlockSpec((T, LANES), lambda h, j: (j, h))
    full = pl.BlockSpec((S, LANES), lambda h, j: (0, h))
    crow = pl.BlockSpec((None, 2, T), lambda h, j: (h, 0, j))
    wide = jax.ShapeDtypeStruct((S, D_MODEL), F32)
    return pl.pallas_call(
        body, name=name, grid=(N_CBLK, n_t),
        in_specs=[full, kblk, kblk, full, full, full, crow],
        out_specs=[full, kblk, kblk, crow, full],
        out_shape=[wide, wide, jax.ShapeDtypeStruct((S, D_MODEL), BF16),
                   jax.ShapeDtypeStruct((N_CBLK, 2, S), F32), wide],
        scratch_shapes=[pltpu.VMEM((S, LANES), F32)],
        compiler_params=_params("parallel", "arbitrary"),
    )(qs_, kn, vb, do, o, lse, c_row)


ALL_PEERS = tuple(range(1, N_DEV))
NEAR_PEERS = (1, 2, 4, 6)
FAR_CHIPS = (2, 4, 6)


def _landing_shapes(arrays, gathers):
    return [jax.ShapeDtypeStruct((N_DEV,) + a.shape if g else a.shape, a.dtype)
            for a, g in zip(arrays, gathers)]


def _my_index():
    return 4 * lax.axis_index("x") + 2 * lax.axis_index("y") + lax.axis_index("c")


def _own_copies(srcs, lands, gathers, sems):
    me = _my_index()
    return [pltpu.make_async_copy(src if g else src.at[me], land.at[me], sems.at[a])
            for a, (src, land, g) in enumerate(zip(srcs, lands, gathers))]


def _peer_copies(srcs, lands, gathers, send_sems, recv_sems, ks=ALL_PEERS):
    x, y, c = lax.axis_index("x"), lax.axis_index("y"), lax.axis_index("c")
    me = 4 * x + 2 * y + c
    out = []
    for j, k in enumerate(ks):
        to = (1 - x if k & 4 else x, 1 - y if k & 2 else y, 1 - c if k & 1 else c)
        peer = 4 * to[0] + 2 * to[1] + to[2]
        for a, (src, land, g) in enumerate(zip(srcs, lands, gathers)):
            sem = a * len(ks) + j
            src_blk = src if g else src.at[peer]

            def copy(slot, src_blk=src_blk, land=land, sem=sem, to=to):
                return pltpu.make_async_remote_copy(
                    src_ref=src_blk, dst_ref=land.at[slot], send_sem=send_sems.at[sem],
                    recv_sem=recv_sems.at[sem], device_id=to,
                    device_id_type=pl.DeviceIdType.MESH)

            out.append((k, a, copy(me), copy(peer)))
    return out


def _forward_copies(lands, send_sems, recv_sems):
    x, y, c = lax.axis_index("x"), lax.axis_index("y"), lax.axis_index("c")
    out = []
    for j, f in enumerate(FAR_CHIPS):
        chip = 4 * (1 - x if f & 4 else x) + 2 * (1 - y if f & 2 else y)
        for a, land in enumerate(lands):
            sem = a * len(FAR_CHIPS) + j

            def copy(slot, land=land, sem=sem):
                return pltpu.make_async_remote_copy(
                    src_ref=land.at[slot], dst_ref=land.at[slot], send_sem=send_sems.at[sem],
                    recv_sem=recv_sems.at[sem], device_id=(x, y, 1 - c),
                    device_id_type=pl.DeviceIdType.MESH)

            out.append((f, a, copy(chip + c), copy(chip + 1 - c)))
    return out


def _exchange(arrays, gathers, name, two_level=False):
    n = len(arrays)
    ks = NEAR_PEERS if two_level else ALL_PEERS
    assert not two_level or all(gathers)

    def body(*refs):
        ins, outs = refs[:n], refs[n:2 * n]
        send_sems, recv_sems, own_sems, fwd_send_sems, fwd_recv_sems = refs[2 * n:]
        own = _own_copies(ins, outs, gathers, own_sems)
        for cp in own:
            cp.start()
        copies = _peer_copies(ins, outs, gathers, send_sems, recv_sems, ks)
        for _, _, send, _ in copies:
            send.start()
        passed = {}
        if two_level:
            passed = {(f, a): (send, arrival)
                      for f, a, send, arrival in _forward_copies(outs, fwd_send_sems, fwd_recv_sems)}
        for k, a, _, arrival in copies:
            arrival.wait_recv()
            if (k, a) in passed:
                passed[k, a][0].start()
        for send, arrival in passed.values():
            arrival.wait_recv()
            send.wait_send()
        for _, _, send, _ in copies:
            send.wait_send()
        for cp in own:
            cp.wait()

    hbm = pl.BlockSpec(memory_space=pl.ANY)
    return pl.pallas_call(
        body, name=name,
        in_specs=[hbm] * n, out_specs=[hbm] * n, out_shape=_landing_shapes(arrays, gathers),
        scratch_shapes=[pltpu.SemaphoreType.DMA((n * len(ks),)),
                        pltpu.SemaphoreType.DMA((n * len(ks),)),
                        pltpu.SemaphoreType.DMA((n,)),
                        pltpu.SemaphoreType.DMA((n * len(FAR_CHIPS),)),
                        pltpu.SemaphoreType.DMA((n * len(FAR_CHIPS),))],
        compiler_params=pltpu.CompilerParams(has_side_effects=True),
    )(*arrays)


_HBM = pl.BlockSpec(memory_space=pltpu.HBM)
_SEM = pl.BlockSpec(memory_space=pltpu.SEMAPHORE)
_ANY = pl.BlockSpec(memory_space=pl.ANY)
_DATAFLOW = pltpu.SideEffectType.DATAFLOW_SIDE_EFFECTING


def _in_hbm(a):
    return pltpu.with_memory_space_constraint(a, pltpu.HBM)


def _exchange_start(arrays, gathers, after, name, ks=ALL_PEERS):
    n = len(arrays)
    lands = [lax.empty(s.shape, s.dtype) for s in _landing_shapes(arrays, gathers)]

    def body(*refs):
        srcs, dsts = refs[:n], refs[n:2 * n]
        send_sems, recv_sems, own_sems = refs[2 * n + 1:2 * n + 4]
        token = refs[-1]
        for cp in _own_copies(srcs, dsts, gathers, own_sems):
            cp.start()
        for _, _, send, _ in _peer_copies(srcs, dsts, gathers, send_sems, recv_sems, ks):
            send.start()
        token[...] = jnp.zeros_like(token)

    hbm_like = [pltpu.HBM(a.shape, a.dtype) for a in list(arrays) + lands]
    res = pl.pallas_call(
        body, name=name,
        in_specs=[_HBM] * (2 * n) + [_ANY],
        out_specs=(_SEM, _SEM, _SEM, *[_HBM] * (2 * n), pl.BlockSpec(memory_space=pltpu.VMEM)),
        out_shape=(pltpu.SemaphoreType.DMA((n * len(ks),)), pltpu.SemaphoreType.DMA((n * len(ks),)),
                   pltpu.SemaphoreType.DMA((n,)), *hbm_like,
                   jax.ShapeDtypeStruct((8, LANES), F32)),
        input_output_aliases={i: 3 + i for i in range(2 * n)},
        compiler_params=pltpu.CompilerParams(has_side_effects=_DATAFLOW),
    )(*[_in_hbm(a) for a in list(arrays) + lands], after)
    return (res[0], res[1], res[2], res[3:3 + n], res[3 + n:3 + 2 * n]), res[-1]


def _exchange_wait(started, gathers, after, name, ks=ALL_PEERS):
    send_sems, recv_sems, own_sems, arrays, lands = started
    n = len(arrays)

    def body(*refs):
        srcs, dsts = refs[:n], refs[n:2 * n]
        for _, _, send, arrival in _peer_copies(srcs, dsts, gathers, refs[2 * n], refs[2 * n + 1],
                                                ks):
            arrival.wait_recv()
            send.wait_send()
        for cp in _own_copies(srcs, dsts, gathers, refs[2 * n + 2]):
            cp.wait()

    hbm_like = [pltpu.HBM(a.shape, a.dtype) for a in list(arrays) + list(lands)]
    res = pl.pallas_call(
        body, name=name,
        in_specs=[_HBM] * (2 * n) + [_SEM, _SEM, _SEM, _ANY],
        out_specs=[_HBM] * (2 * n), out_shape=hbm_like,
        input_output_aliases={i: i for i in range(2 * n)},
        compiler_params=pltpu.CompilerParams(has_side_effects=_DATAFLOW),
    )(*arrays, *lands, send_sems, recv_sems, own_sems, after)
    return res[n:]


def _forward_start(lands, after, name):
    n = len(lands)

    def body(*refs):
        send_sems, recv_sems = refs[n + 1:n + 3]
        for _, _, send, _ in _forward_copies(refs[:n], send_sems, recv_sems):
            send.start()
        refs[-1][...] = jnp.zeros_like(refs[-1])

    n_sem = n * len(FAR_CHIPS)
    res = pl.pallas_call(
        body, name=name,
        in_specs=[_HBM] * n + [_ANY],
        out_specs=(_SEM, _SEM, *[_HBM] * n, pl.BlockSpec(memory_space=pltpu.VMEM)),
        out_shape=(pltpu.SemaphoreType.DMA((n_sem,)), pltpu.SemaphoreType.DMA((n_sem,)),
                   *[pltpu.HBM(a.shape, a.dtype) for a in lands],
                   jax.ShapeDtypeStruct((8, LANES), F32)),
        input_output_aliases={i: 2 + i for i in range(n)},
        compiler_params=pltpu.CompilerParams(has_side_effects=_DATAFLOW),
    )(*[_in_hbm(a) for a in lands], after)
    return (res[0], res[1], res[2:2 + n]), res[-1]


def _forward_wait(started, after, name):
    send_sems, recv_sems, lands = started
    n = len(lands)

    def body(*refs):
        for _, _, send, arrival in _forward_copies(refs[:n], refs[n], refs[n + 1]):
            arrival.wait_recv()
            send.wait_send()

    return pl.pallas_call(
        body, name=name,
        in_specs=[_HBM] * n + [_SEM, _SEM, _ANY],
        out_specs=[_HBM] * n, out_shape=[pltpu.HBM(a.shape, a.dtype) for a in lands],
        input_output_aliases={i: i for i in range(n)},
        compiler_params=pltpu.CompilerParams(has_side_effects=_DATAFLOW),
    )(*lands, send_sems, recv_sems, after)


def _reduce_adamw(parts, w, m, v, name):
    n_layer = len(parts)
    n, R, C = parts[0].shape
    tr = 256 if R % 256 == 0 else R
    n_t = R // tr

    def body(*refs):
        p_refs = refs[:n_layer]
        w_ref, m_ref, v_ref, g_ref, d_ref, nm_ref, nv_ref = refs[n_layer:]

        def update(p_ref):
            g = p_ref[0].astype(F32)
            for s in range(1, n):
                g = g + p_ref[s].astype(F32)
            g_ref[...] = g
            m_new = ADAM_B1 * m_ref[...] + (1.0 - ADAM_B1) * g
            v_new = ADAM_B2 * v_ref[...] + (1.0 - ADAM_B2) * (g * g)
            nm_ref[...] = m_new
            nv_ref[...] = v_new
            m_hat = m_new / (1.0 - ADAM_B1 ** ADAM_STEP)
            v_hat = v_new / (1.0 - ADAM_B2 ** ADAM_STEP)
            d_ref[...] = -ADAM_LR * (m_hat / (jnp.sqrt(v_hat) + ADAM_EPS) + ADAM_WD * w_ref[...])

        for layer, p_ref in enumerate(p_refs):
            pl.when(pl.program_id(0) == layer)(functools.partial(update, p_ref))

    def parts_spec(layer):
        def index(l, i):
            return 0, jnp.where(l < layer, 0, jnp.where(l > layer, n_t - 1, i)), 0
        return pl.BlockSpec((n, tr, C), index)

    blk = pl.BlockSpec((None, tr, C), lambda l, i: (l, i, 0))
    out = jax.ShapeDtypeStruct((n_layer, R, C), F32)
    return pl.pallas_call(
        body, name=name, grid=(n_layer, n_t),
        in_specs=[parts_spec(layer) for layer in range(n_layer)] + [blk, blk, blk],
        out_specs=[blk] * 4, out_shape=[out] * 4,
        compiler_params=_params("arbitrary", "arbitrary"),
    )(*parts, w, m, v)


def _pack(arrays):
    flat = jnp.concatenate([a.reshape(-1).astype(F32) for a in arrays])
    pad = (-flat.shape[0]) % (8 * LANES)
    return jnp.pad(flat, (0, pad)).reshape(-1, LANES)


def _unpack(buf, shapes):
    flat = buf.reshape(-1)
    out, off = [], 0
    for shp in shapes:
        size = 1
        for s in shp:
            size *= s
        out.append(flat[off:off + size].reshape(shp))
        off += size
    return out


def _block_diag_pairs(w):
    w = w.reshape(N_CBLK, 2, LRU_BLOCK_DIM, LRU_BLOCK_DIM)
    z = jnp.zeros_like(w[:, 0])
    top = jnp.concatenate([w[:, 0], z], axis=2)
    bot = jnp.concatenate([z, w[:, 1]], axis=2)
    return jnp.concatenate([top, bot], axis=1)


def _diag_pairs(m):
    h = LRU_BLOCK_DIM
    return jnp.stack([m[:, :h, :h], m[:, h:, h:]], axis=1).reshape(2 * N_CBLK, h, h)


SMALL = ("mlp_norm", "lru_conv_b", "lru_w_r", "lru_b_r", "lru_w_i", "lru_b_i",
         "lru_lambda", "fox_b_f", "fox_q_gain", "fox_k_gain")
WEIGHTS = ("mix_norm", "mlp_norm", "mlp_w1", "mlp_w2", "lru_w_in", "lru_conv_w", "lru_conv_b",
           "lru_w_r", "lru_b_r", "lru_w_i", "lru_b_i", "lru_lambda", "lru_w_out", "fox_w_in",
           "fox_b_f", "fox_q_gain", "fox_k_gain", "fox_w_out")


def kernel(x, mix_norm, mlp_norm, mlp_w1, mlp_w2, lru_w_in, lru_conv_w, lru_conv_b, lru_w_r, lru_b_r, lru_w_i, lru_b_i, lru_lambda, lru_w_out, fox_w_in, fox_b_f, fox_q_gain, fox_k_gain, fox_w_out, loss_target, m_mix_norm, m_mlp_norm, m_mlp_w1, m_mlp_w2, m_lru_w_in, m_lru_conv_w, m_lru_conv_b, m_lru_w_r, m_lru_b_r, m_lru_w_i, m_lru_b_i, m_lru_lambda, m_lru_w_out, m_fox_w_in, m_fox_b_f, m_fox_q_gain, m_fox_k_gain, m_fox_w_out, v_mix_norm, v_mlp_norm, v_mlp_w1, v_mlp_w2, v_lru_w_in, v_lru_conv_w, v_lru_conv_b, v_lru_w_r, v_lru_b_r, v_lru_w_i, v_lru_b_i, v_lru_lambda, v_lru_w_out, v_fox_w_in, v_fox_b_f, v_fox_q_gain, v_fox_k_gain, v_fox_w_out):
    w_in = dict(mix_norm=mix_norm, mlp_norm=mlp_norm, mlp_w1=mlp_w1, mlp_w2=mlp_w2,
                lru_w_in=lru_w_in, lru_conv_w=lru_conv_w, lru_conv_b=lru_conv_b, lru_w_r=lru_w_r,
                lru_b_r=lru_b_r, lru_w_i=lru_w_i, lru_b_i=lru_b_i, lru_lambda=lru_lambda,
                lru_w_out=lru_w_out, fox_w_in=fox_w_in, fox_b_f=fox_b_f, fox_q_gain=fox_q_gain,
                fox_k_gain=fox_k_gain, fox_w_out=fox_w_out)
    m_in = dict(mix_norm=m_mix_norm, mlp_norm=m_mlp_norm, mlp_w1=m_mlp_w1, mlp_w2=m_mlp_w2,
                lru_w_in=m_lru_w_in, lru_conv_w=m_lru_conv_w, lru_conv_b=m_lru_conv_b,
                lru_w_r=m_lru_w_r, lru_b_r=m_lru_b_r, lru_w_i=m_lru_w_i, lru_b_i=m_lru_b_i,
                lru_lambda=m_lru_lambda, lru_w_out=m_lru_w_out, fox_w_in=m_fox_w_in,
                fox_b_f=m_fox_b_f, fox_q_gain=m_fox_q_gain, fox_k_gain=m_fox_k_gain,
                fox_w_out=m_fox_w_out)
    v_in = dict(mix_norm=v_mix_norm, mlp_norm=v_mlp_norm, mlp_w1=v_mlp_w1, mlp_w2=v_mlp_w2,
                lru_w_in=v_lru_w_in, lru_conv_w=v_lru_conv_w, lru_conv_b=v_lru_conv_b,
                lru_w_r=v_lru_w_r, lru_b_r=v_lru_b_r, lru_w_i=v_lru_w_i, lru_b_i=v_lru_b_i,
                lru_lambda=v_lru_lambda, lru_w_out=v_lru_w_out, fox_w_in=v_fox_w_in,
                fox_b_f=v_fox_b_f, fox_q_gain=v_fox_q_gain, fox_k_gain=v_fox_k_gain,
                fox_w_out=v_fox_w_out)
    D = D_MODEL
    S = x.shape[1]
    x0, target = x[0], loss_target[0]
    me = 4 * lax.axis_index("x") + 2 * lax.axis_index("y") + lax.axis_index("c")

    def bf16(a):
        return a.astype(BF16)

    (lru_in_g,) = _exchange([bf16(lru_w_in[0])], [True], "gather_lru_in", two_level=True)
    gather_lru, tok = _exchange_start([bf16(lru_w_out[0]), lru_conv_w[0]], [True] * 2, lru_in_g,
                                      "gather_lru_start")
    gather_mlp0, tok = _exchange_start([bf16(mlp_w1[0]), bf16(mlp_w2[0])], [True] * 2, tok,
                                       "gather_mlp0_start", NEAR_PEERS)
    gather_fox, tok = _exchange_start([bf16(fox_w_in[0]), bf16(fox_w_out[0])], [True] * 2, tok,
                                      "gather_fox_start")
    gather_mlp1, tok = _exchange_start([bf16(mlp_w1[1]), bf16(mlp_w2[1])], [True] * 2, tok,
                                       "gather_mlp1_start", NEAR_PEERS)

    def pass_on(started, after, name):
        lands = _exchange_wait(started, [True] * 2, after, name + "_wait", NEAR_PEERS)
        return _forward_start(lands, after, name + "_pass_start")
    wr =_block_diag_pairs(lru_w_r[0]).astype(BF16)
    wi = _block_diag_pairs(lru_w_i[0]).astype(BF16)
    b_r, b_i = lru_b_r.reshape(1, D), lru_b_i.reshape(1, D)
    q_gain, k_gain = jnp.tile(fox_q_gain, (1, 2)), jnp.tile(fox_k_gain, (1, 2))
    b_f = jnp.pad(fox_b_f, ((0, 0), (0, LANES - N_HEADS)))
    g_mix0, g_mix1 = mix_norm[0:1] + tok[0, 0], mix_norm[1:2]
    g_mlp0, g_mlp1 = mlp_norm[0:1], mlp_norm[1:2]

    (u0,), h0 = _norm_matmul(x0, g_mix0, [lru_in_g], "lru_in_proj")
    lru_out_g, conv_g = _exchange_wait(gather_lru, [True] * 2, u0, "gather_lru_wait")
    lru_out_w = lru_out_g.reshape(D, D)
    conv_w = conv_g.transpose(1, 0, 2).reshape(CONV_WIDTH, D)
    y_lru, hs =_lru_fwd(u0, conv_w, lru_conv_b, wr, b_r, wi, b_i, lru_lambda, "lru_core")
    pass_mlp0, tok = pass_on(gather_mlp0, y_lru, "gather_mlp0")
    x1 = _matmul_res(y_lru, lru_out_w, x0, "lru_out_proj", tok)
    w1g0, w2g0 = _forward_wait(pass_mlp0, x1, "gather_mlp0_pass_wait")
    x2, h1, r1 = _mlp_fwd(x1, g_mlp0, w1g0, w2g0, "mlp0")
    fox_in_g, fox_out_g = _exchange_wait(gather_fox, [True] * 2, x2, "gather_fox_wait")
    fox_out_w = fox_out_g.reshape(D, D)
    fox_full = fox_in_g.transpose(1, 0, 2).reshape(D, 3 * D + N_HEADS)
    wqkv = fox_full[:, :3 * D].reshape(D, 3, D).transpose(1, 0, 2)
    wf = jnp.pad(fox_full[:, 3 * D:], ((0, 0), (0, LANES - N_HEADS)))[None]
    (u_qkv, f), h2 = _norm_matmul(x2, g_mix1, [wqkv, wf], "fox_in_proj")
    qn, kn, vb = _qk_prep(u_qkv, q_gain, k_gain, "fox_qk_norm")
    c_col = _forget_fwd(f, b_f, "fox_forget")
    c_row = c_col[:, :N_HEADS].T.reshape(N_CBLK, 2, S)
    o, lse = _attn_fwd(qn, kn, vb, c_row, "fox_attn")
    pass_mlp1, tok = pass_on(gather_mlp1, o, "gather_mlp1")
    x3 = _matmul_res(o, fox_out_w, x2, "fox_out_proj", tok)
    w1g1, w2g1 = _forward_wait(pass_mlp1, x3, "gather_mlp1_pass_wait")
    loss_local, dx4, h3, r3 = _mlp_fwd(x3, g_mlp1, w1g1, w2g1, "mlp1", target)

    dx3, dg_mlp1, da3 = _mlp_bwd(dx4, x3, g_mlp1, r3, w1g1, w2g1, "mlp1_bwd")
    dw1_1 = _matmul_tn(h3, da3, "mlp1_dw1", cols=2, col_blocks=N_DEV)
    dw2_1 = _matmul_tn(r3, dx4, "mlp1_dw2", rows=2, a_square=True).reshape(N_DEV, -1, D)
    grads_mlp1, tok = _exchange_start([dw1_1, dw2_1], [False] * 2, tok, "grads_mlp1_start")
    do = _matmul_nt(dx3, fox_out_w, "fox_out_bwd", BF16, tok)
    d_fox_out = _matmul_tn(o, dx3, "fox_out_dw").reshape(N_DEV, -1, D)
    dqn, dkn, dv, dc_row, rho = _attn_bwd(qn, kn, vb, do, o, lse, c_row, "fox_attn_bwd")
    duq, duk, dq_gain, dk_gain = _qk_bwd(u_qkv, dqn, dkn, q_gain, k_gain, "fox_qk_norm_bwd")
    dc_k = jnp.pad(dc_row.reshape(N_HEADS, S).T, ((0, 0), (0, LANES - N_HEADS)))
    df, db_f = _forget_bwd(dc_k, rho, f, b_f, "fox_forget_bwd")
    dx2, dg_mix1 = _proj_bwd([[duq, duk, dv], [df]], [wqkv, wf], x2, g_mix1, dx3, "fox_in_bwd")
    d_fox_in = jnp.concatenate(
        [_matmul_tn(h2, duq, "fox_in_dwq"), _matmul_tn(h2, duk, "fox_in_dwk"),
         _matmul_tn(h2, dv, "fox_in_dwv"), _matmul_tn(h2, df, "fox_in_dwf")[:, :N_HEADS]], axis=1)
    d_fox_in = d_fox_in.reshape(D, N_DEV, -1).transpose(1, 0, 2)
    grads_fox, tok = _exchange_start([d_fox_in, d_fox_out], [False] * 2, tok, "grads_fox_start")
    dx1, dg_mlp0, da1 = _mlp_bwd(dx2, x1, g_mlp0 + tok[0, 0], r1, w1g0, w2g0, "mlp0_bwd")
    dw1_0 = _matmul_tn(h1, da1, "mlp0_dw1", cols=2, col_blocks=N_DEV)
    dw2_0 = _matmul_tn(r1, dx2, "mlp0_dw2", rows=2, a_square=True).reshape(N_DEV, -1, D)
    grads_mlp0, tok = _exchange_start([dw1_0, dw2_0], [False] * 2, tok, "grads_mlp0_start")
    dy_lru = _matmul_nt(dx1, lru_out_w, "lru_out_bwd", F32, tok)
    d_lru_out = _matmul_tn(y_lru, dx1, "lru_out_dw").reshape(N_DEV, -1, D)
    dgp, dxb, d_conv_w, d_conv_b, d_b_r, d_b_i, d_lam, d_wr, d_wi = _lru_bwd(
        dy_lru, u0, hs, conv_w, lru_conv_b, wr, b_r, wi, b_i, lru_lambda, "lru_core_bwd")

    small_grads = dict(
        mlp_norm=jnp.concatenate([dg_mlp0, dg_mlp1], axis=0),
        lru_conv_b=d_conv_b, lru_w_r=_diag_pairs(d_wr), lru_b_r=d_b_r, lru_w_i=_diag_pairs(d_wi),
        lru_b_i=d_b_i, lru_lambda=d_lam, fox_b_f=db_f[:, :N_HEADS],
        fox_q_gain=dq_gain[:, :HEAD_DIM], fox_k_gain=dk_gain[:, :HEAD_DIM])
    small_partial = _pack([dg_mix1] + [small_grads[n] for n in SMALL] + [d_conv_w])
    grads_lru_out, tok = _exchange_start([d_lru_out, small_partial], [False, True], tok,
                                         "grads_lru_out_start")
    dx0, dg_mix0 = _proj_bwd([[dgp, dxb]], [lru_in_g], x0, mix_norm[0:1] + tok[0, 0], dx1,
                             "lru_in_bwd")
    d_lru_in = jnp.concatenate([_matmul_tn(h0, dgp, "lru_in_dw_gate", col_blocks=4),
                                _matmul_tn(h0, dxb, "lru_in_dw_x", col_blocks=4)], axis=0)
    grads_lru_in, tok = _exchange_start([d_lru_in, dg_mix0], [False, True], tok,
                                        "grads_lru_in_start")

    grads, deltas, new_m, new_v = {}, {}, {}, {}

    def update(name, parts):
        w, m, v = w_in[name], m_in[name], v_in[name]
        shape = w.shape
        stacked = (len(parts), -1, shape[-1])
        w3 = w.reshape(stacked)
        res = _reduce_adamw([p.reshape((N_DEV,) + w3.shape[1:]) for p in parts], w3,
                            m.reshape(stacked), v.reshape(stacked), "adamw_" + name)
        return [r.reshape(shape) for r in res]

    def store(name, res):
        grads[name], deltas[name], new_m[name], new_v[name] = res

    p_w1_1, p_w2_1 = _exchange_wait(grads_mlp1, [False] * 2, tok, "grads_mlp1_wait")
    p_fox_in, p_fox_out = _exchange_wait(grads_fox, [False] * 2, p_w1_1, "grads_fox_wait")
    store("fox_w_in", update("fox_w_in", [p_fox_in]))
    store("fox_w_out", update("fox_w_out", [p_fox_out]))
    p_w1_0, p_w2_0 = _exchange_wait(grads_mlp0, [False] * 2, grads["fox_w_out"], "grads_mlp0_wait")
    store("mlp_w1", update("mlp_w1", [p_w1_0, p_w1_1]))
    store("mlp_w2", update("mlp_w2", [p_w2_0, p_w2_1]))
    p_lru_out, p_small = _exchange_wait(grads_lru_out, [False, True], grads["mlp_w2"],
                                        "grads_lru_out_wait")
    store("lru_w_out", update("lru_w_out", [p_lru_out]))
    p_lru_in, p_mix0 = _exchange_wait(grads_lru_in, [False, True], grads["lru_w_out"],
                                      "grads_lru_in_wait")
    store("lru_w_in", update("lru_w_in", [p_lru_in]))

    mix0 = [r[0] for r in _reduce_adamw([p_mix0], mix_norm[None, 0:1], m_mix_norm[None, 0:1],
                                        v_mix_norm[None, 0:1], "adamw_mix0")]
    packed = lambda src, first: _pack([first] + [src[n] for n in SMALL]
                                      + [jnp.zeros((CONV_WIDTH, D))])[None]
    small_shapes = [(1, D)] + [w_in[n].shape for n in SMALL]
    n_small = sum(math.prod(s) for s in small_shapes)
    res_small = _reduce_adamw([p_small], packed(w_in, mix_norm[1:2]), packed(m_in, m_mix_norm[1:2]),
                              packed(v_in, v_mix_norm[1:2]), "adamw_small")
    for name, *vals in zip(("mix1",) + SMALL, *[_unpack(r, small_shapes) for r in res_small]):
        if name == "mix1":
            vals = [jnp.concatenate([r0, r1], axis=0) for r0, r1 in zip(mix0, vals)]
            name = "mix_norm"
        store(name, vals)
    conv_parts = p_small.reshape(N_DEV, -1)[:, n_small:n_small + CONV_WIDTH * D]
    conv_parts = conv_parts.reshape(N_DEV, CONV_WIDTH, N_DEV, LANES)
    conv_parts = lax.dynamic_index_in_dim(conv_parts, me, axis=2, keepdims=False)
    store("lru_conv_w", update("lru_conv_w", [conv_parts]))

    loss = lax.psum(loss_local[0, 0], ("x", "y", "c"))
    return (loss, dx0[None], *[grads[n] for n in WEIGHTS], *[deltas[n] for n in WEIGHTS],
            *[new_m[n] for n in WEIGHTS], *[new_v[n] for n in WEIGHTS])
```

```python
import functools
import math

import jax
import jax.numpy as jnp
from jax import lax
from jax.experimental import pallas as pl
from jax.experimental.pallas import tpu as pltpu

F32 = jnp.float32
BF16 = jnp.bfloat16

N_DEV = 8
D_MODEL = 1024
D_FF = 4096
N_HEADS = 16
HEAD_DIM = 64
LRU_BLOCK_DIM = 64
CONV_WIDTH = 4
LRU_C = 8.0
EPS = 1e-6
NEG_INF = -1e30
ATTN_SCALE = HEAD_DIM ** -0.5
LANES = 128
N_CBLK = D_MODEL // LANES
VMEM_LIMIT = 52 * 2 ** 20

ADAM_LR = 0.001
ADAM_B1 = 0.9
ADAM_B2 = 0.999
ADAM_EPS = 1e-08
ADAM_WD = 0.01
ADAM_STEP = 10

_NT = (((1,), (1,)), ((), ()))
_TN = (((0,), (0,)), ((), ()))


def _params(*sem):
    return pltpu.CompilerParams(dimension_semantics=sem, vmem_limit_bytes=VMEM_LIMIT)


def _resident(shape):
    zeros = (0,) * len(shape)
    return pl.BlockSpec(shape, lambda *_: zeros, pipeline_mode=pl.Buffered(1))


def _dot(a, b):
    return jnp.dot(a, b, preferred_element_type=F32)


def _dot_nt(a, b):
    return lax.dot_general(a, b, _NT, preferred_element_type=F32)


def _dot_tn(a, b):
    return lax.dot_general(a, b, _TN, preferred_element_type=F32)


def _sigmoid(x):
    return 1.0 / (1.0 + jnp.exp(-x))


def _log_sigmoid(x):
    return -(jnp.maximum(-x, 0.0) + jnp.log1p(jnp.exp(-jnp.abs(x))))


def _expm1(x):
    poly = x * (1.0 + x * (0.5 + x * (1.0 / 6.0 + x * (1.0 / 24.0 + x * (1.0 / 120.0)))))
    return jnp.where(jnp.abs(x) < 0.1, poly, jnp.exp(x) - 1.0)


_GELU_K = 0.7978845608028654


def _gelu(x):
    return 0.5 * x * (1.0 + jnp.tanh(_GELU_K * (x + 0.044715 * (x * x * x))))


def _gelu_grad(x):
    t = jnp.tanh(_GELU_K * (x + 0.044715 * (x * x * x)))
    return 0.5 * (1.0 + t) + 0.5 * x * (1.0 - t * t) * (_GELU_K * (1.0 + 3 * 0.044715 * x * x))


def _rms_scale(x):
    return lax.rsqrt(jnp.mean(x * x, axis=-1, keepdims=True) + EPS)


def _norm_bwd(dh, x, g):
    rs = _rms_scale(x)
    xhat = x * rs
    dxhat = dh * g
    dx = rs * (dxhat - xhat * jnp.mean(dxhat * xhat, axis=-1, keepdims=True))
    return dx, jnp.sum(dh * xhat, axis=0, keepdims=True)


def _token_tile(S, want):
    tm = min(S, want)
    assert S % tm == 0
    return tm


def _norm_matmul(x, g, ws, name, tm=512):
    S, D = x.shape
    tm = _token_tile(S, tm)
    n = len(ws)

    def body(x_ref, g_ref, *refs):
        w_refs, o_refs, h_ref = refs[:n], refs[n:2 * n], refs[2 * n]
        xv = x_ref[...]
        h = (xv * _rms_scale(xv) * g_ref[...]).astype(BF16)
        h_ref[...] = h
        for w_ref, o_ref in zip(w_refs, o_refs):
            nb, _, nw = w_ref.shape
            for d in range(nb):
                o_ref[:, d * nw:(d + 1) * nw] = _dot(h, w_ref[d])

    widths = [w.shape[0] * w.shape[2] for w in ws]
    outs = pl.pallas_call(
        body, name=name, grid=(S // tm,),
        in_specs=[pl.BlockSpec((tm, D), lambda i: (i, 0)), _resident((1, D))]
        + [_resident(w.shape) for w in ws],
        out_specs=[pl.BlockSpec((tm, n_), lambda i: (i, 0)) for n_ in widths]
        + [pl.BlockSpec((tm, D), lambda i: (i, 0))],
        out_shape=[jax.ShapeDtypeStruct((S, n_), F32) for n_ in widths]
        + [jax.ShapeDtypeStruct((S, D), BF16)],
        compiler_params=_params("parallel"),
    )(x, g, *ws)
    return outs[:n], outs[n]


def _matmul_res(a, w, res, name, after, tm=1024):
    S, K = a.shape
    N = w.shape[1]
    tm = _token_tile(S, tm)

    def body(a_ref, w_ref, r_ref, after_ref, o_ref):
        o_ref[...] = r_ref[...] + _dot(a_ref[...], w_ref[...])

    return pl.pallas_call(
        body, name=name, grid=(S // tm,),
        in_specs=[pl.BlockSpec((tm, K), lambda i: (i, 0)), _resident((K, N)),
                  pl.BlockSpec((tm, N), lambda i: (i, 0)), pl.BlockSpec(memory_space=pl.ANY)],
        out_specs=pl.BlockSpec((tm, N), lambda i: (i, 0)),
        out_shape=jax.ShapeDtypeStruct((S, N), F32),
        compiler_params=_params("parallel"),
    )(a, w, res, after)


def _matmul_nt(a, w, name, out_dtype, after, tm=1024):
    S, N = a.shape
    K = w.shape[0]
    tm = _token_tile(S, tm)

    def body(a_ref, w_ref, after_ref, o_ref):
        o_ref[...] = _dot_nt(a_ref[...].astype(BF16), w_ref[...]).astype(out_dtype)

    return pl.pallas_call(
        body, name=name, grid=(S // tm,),
        in_specs=[pl.BlockSpec((tm, N), lambda i: (i, 0)), _resident((K, N)),
                  pl.BlockSpec(memory_space=pl.ANY)],
        out_specs=pl.BlockSpec((tm, K), lambda i: (i, 0)),
        out_shape=jax.ShapeDtypeStruct((S, K), out_dtype),
        compiler_params=_params("parallel"),
    )(a, w, after)


def _proj_bwd(a_lists, w_list, x, g, res, name, tm=512):
    S, D = x.shape
    tm = _token_tile(S, tm)
    a_list = [a for group in a_lists for a in group]
    n, n_w = len(a_list), len(w_list)

    def body(*refs):
        a_refs, w_refs = list(refs[:n]), refs[n:n + n_w]
        x_ref, g_ref, r_ref, dx_ref, dg_ref = refs[n + n_w:]
        dh = jnp.zeros((tm, D), F32)
        for group, w_ref in zip(a_lists, w_refs):
            nw = w_ref.shape[2]
            d = 0
            for _ in group:
                a_ref = a_refs.pop(0)
                for j in range(a_ref.shape[1] // nw):
                    dh = dh + _dot_nt(a_ref[:, j * nw:(j + 1) * nw].astype(BF16), w_ref[d])
                    d += 1
        dx, dg = _norm_bwd(dh, x_ref[...], g_ref[...])
        dx_ref[...] = r_ref[...] + dx

        @pl.when(pl.program_id(0) == 0)
        def _():
            dg_ref[...] = jnp.zeros_like(dg_ref)
        dg_ref[...] += dg

    tok = lambda width: pl.BlockSpec((tm, width), lambda i: (i, 0))
    return pl.pallas_call(
        body, name=name, grid=(S // tm,),
        in_specs=[tok(a.shape[1]) for a in a_list] + [_resident(w.shape) for w in w_list]
        + [tok(D), _resident((1, D)), tok(D)],
        out_specs=[tok(D), pl.BlockSpec((1, D), lambda i: (0, 0))],
        out_shape=[jax.ShapeDtypeStruct((S, D), F32), jax.ShapeDtypeStruct((1, D), F32)],
        compiler_params=_params("arbitrary"),
    )(*a_list, *w_list, x, g, res)


def _matmul_tn(a, b, name, rows=1, cols=1, col_blocks=None, a_square=False, tm=1024):
    S, K = a.shape
    N = b.shape[1]
    tm = _token_tile(S, tm)
    n_tok = S // tm
    kr, nc = K // rows, N // cols

    def body(a_ref, b_ref, o_ref, acc_ref):
        av = a_ref[...]
        if a_square:
            av = av.astype(F32)
            av = av * av
        part = _dot_tn(av.astype(BF16), b_ref[...].astype(BF16))
        step = pl.program_id(2)

        @pl.when(step == 0)
        def _():
            acc_ref[...] = part

        @pl.when(step > 0)
        def _():
            acc_ref[...] += part

        @pl.when(step == n_tok - 1)
        def _():
            if col_blocks is None:
                o_ref[...] = acc_ref[...].astype(BF16)
            else:
                nw = N // col_blocks
                for d in range(col_blocks // cols):
                    o_ref[d] = acc_ref[:, d * nw:(d + 1) * nw].astype(BF16)

    if col_blocks is None:
        out_spec = pl.BlockSpec((kr, nc), lambda r, c, i: (r, c))
        out_shape = jax.ShapeDtypeStruct((K, N), BF16)
    else:
        assert rows == 1 and col_blocks % cols == 0
        per = col_blocks // cols
        out_spec = pl.BlockSpec((per, K, N // col_blocks), lambda r, c, i: (c, 0, 0))
        out_shape = jax.ShapeDtypeStruct((col_blocks, K, N // col_blocks), BF16)
    return pl.pallas_call(
        body, name=name, grid=(rows, cols, n_tok),
        in_specs=[pl.BlockSpec((tm, kr), lambda r, c, i: (i, r)),
                  pl.BlockSpec((tm, nc), lambda r, c, i: (i, c))],
        out_specs=out_spec, out_shape=out_shape,
        scratch_shapes=[pltpu.VMEM((kr, nc), F32)],
        compiler_params=_params("parallel", "parallel", "arbitrary"),
    )(a, b)


def _mlp_fwd(x, g, w1, w2, name, target=None, tm=512):
    S, D = x.shape
    nb, _, fb = w1.shape
    tm = _token_tile(S, tm)
    with_loss = target is not None

    def body(x_ref, g_ref, w1_ref, w2_ref, *refs):
        h_ref, r_ref = refs[-2:]
        xv = x_ref[...]
        h = (xv * _rms_scale(xv) * g_ref[...]).astype(BF16)
        h_ref[...] = h
        acc = xv
        for d in range(nb):
            r = jnp.maximum(_dot(h, w1_ref[d]), 0.0)
            r_ref[:, d * fb:(d + 1) * fb] = r.astype(BF16)
            acc = acc + _dot((r * r).astype(BF16), w2_ref[d])
        if not with_loss:
            refs[0][...] = acc
            return
        t_ref, loss_ref, dy_ref = refs[:3]
        err = acc - t_ref[...]
        dy_ref[...] = err / D

        @pl.when(pl.program_id(0) == 0)
        def _():
            loss_ref[...] = jnp.zeros_like(loss_ref)
        row_loss = jnp.mean(err * err, axis=1, keepdims=True)
        loss_ref[...] += 0.5 * jnp.sum(row_loss, axis=0, keepdims=True)

    tok = lambda width: pl.BlockSpec((tm, width), lambda i: (i, 0))
    saved_specs = [tok(D), tok(nb * fb)]
    saved_shapes = [jax.ShapeDtypeStruct((S, D), BF16), jax.ShapeDtypeStruct((S, nb * fb), BF16)]
    wide = jax.ShapeDtypeStruct((S, D), F32)
    if with_loss:
        head_specs = [pl.BlockSpec((1, 1), lambda i: (0, 0)), tok(D)]
        head_shapes = [jax.ShapeDtypeStruct((1, 1), F32), wide]
    else:
        head_specs, head_shapes = [tok(D)], [wide]
    return pl.pallas_call(
        body, name=name, grid=(S // tm,),
        in_specs=[tok(D), _resident((1, D)), _resident(w1.shape), _resident(w2.shape)]
        + ([tok(D)] if with_loss else []),
        out_specs=head_specs + saved_specs, out_shape=head_shapes + saved_shapes,
        compiler_params=_params("arbitrary" if with_loss else "parallel"),
    )(x, g, w1, w2, *([target] if with_loss else []))


def _mlp_bwd(dout, x, g, r, w1, w2, name, tm=512):
    S, D = x.shape
    nb, _, fb = w1.shape
    tm = _token_tile(S, tm)

    def body(do_ref, x_ref, g_ref, r_ref, w1_ref, w2_ref, dx_ref, dg_ref, da_ref):
        dov = do_ref[...]
        dob = dov.astype(BF16)
        dh = jnp.zeros((tm, D), F32)
        for d in range(nb):
            dz = _dot_nt(dob, w2_ref[d])
            da = (dz * (2.0 * r_ref[:, d * fb:(d + 1) * fb].astype(F32))).astype(BF16)
            da_ref[:, d * fb:(d + 1) * fb] = da
            dh = dh + _dot_nt(da, w1_ref[d])
        dx, dg = _norm_bwd(dh, x_ref[...], g_ref[...])
        dx_ref[...] = dov + dx

        @pl.when(pl.program_id(0) == 0)
        def _():
            dg_ref[...] = jnp.zeros_like(dg_ref)
        dg_ref[...] += dg

    tok = lambda width: pl.BlockSpec((tm, width), lambda i: (i, 0))
    return pl.pallas_call(
        body, name=name, grid=(S // tm,),
        in_specs=[tok(D), tok(D), _resident((1, D)), tok(nb * fb), _resident(w1.shape),
                  _resident(w2.shape)],
        out_specs=[tok(D), pl.BlockSpec((1, D), lambda i: (0, 0)), tok(nb * fb)],
        out_shape=[jax.ShapeDtypeStruct((S, D), F32), jax.ShapeDtypeStruct((1, D), F32),
                   jax.ShapeDtypeStruct((S, nb * fb), BF16)],
        compiler_params=_params("arbitrary"),
    )(dout, x, g, r, w1, w2)


def _scan_chunk(a, b, row, T, reverse):
    s = 1
    while s < T:
        if reverse:
            keep, shift = row < T - s, T - s
        else:
            keep, shift = row >= s, s
        a_sh = jnp.where(keep, pltpu.roll(a, shift, 0), 1.0)
        b_sh = jnp.where(keep, pltpu.roll(b, shift, 0), 0.0)
        b = a * b_sh + b
        a = a * a_sh
        s *= 2
    return a, b


def _row_of(x, row, r):
    return jnp.sum(jnp.where(row == r, x, 0.0), axis=0, keepdims=True)


def _shift_down(x, prev, row, k):
    if k == 0:
        return x
    return jnp.where(row < k, pltpu.roll(prev, k, 0), pltpu.roll(x, k, 0))


def _shift_up(x, nxt, row, k, T):
    if k == 0:
        return x
    return jnp.where(row < T - k, pltpu.roll(x, T - k, 0), pltpu.roll(nxt, T - k, 0))


def _lru_gates(xb, prev_xb, row, cw_ref, cb, wr, br, wi, bi, ls):
    xc = cb + cw_ref[pl.ds(0, 1), :] * _shift_down(xb, prev_xb, row, 3)
    for k in (2, 1, 0):
        xc = xc + cw_ref[pl.ds(3 - k, 1), :] * _shift_down(xb, prev_xb, row, k)
    xcb = xc.astype(BF16)
    r = _sigmoid(_dot(xcb, wr) + br)
    i = _sigmoid(_dot(xcb, wi) + bi)
    la = (LRU_C * r) * ls
    a = jnp.exp(la)
    m = jnp.sqrt(-_expm1(2.0 * la))
    return xc, xcb, r, i, a, m


def _lru_specs(S):
    col = lambda off: pl.BlockSpec((S, LANES), lambda j: (0, j + off))
    vec = pl.BlockSpec((1, LANES), lambda j: (0, j))
    mat = pl.BlockSpec((None, LANES, LANES), lambda j: (j, 0, 0))
    cwm = pl.BlockSpec((CONV_WIDTH, LANES), lambda j: (0, j))
    return col, vec, mat, cwm


def _lru_fwd(u, conv_w, conv_b, wr, br, wi, bi, lam, name):
    S = u.shape[0]
    T = _token_tile(S, 512)
    col, vec, mat, cwm = _lru_specs(S)

    def body(gp_ref, xb_ref, cw_ref, cb_ref, wr_ref, br_ref, wi_ref, bi_ref, lam_ref,
             y_ref, hs_ref):
        row = lax.broadcasted_iota(jnp.int32, (T, LANES), 0)
        ls = _log_sigmoid(lam_ref[...])
        cb, br, bi = cb_ref[...], br_ref[...], bi_ref[...]
        wr, wi = wr_ref[...], wi_ref[...]

        def chunk(ci, carry):
            prev_xb, hc = carry
            rows = pl.ds(pl.multiple_of(ci * T, T), T)
            xb = xb_ref[rows, :]
            xc, _, _, i, a, m = _lru_gates(xb, prev_xb, row, cw_ref, cb, wr, br, wi, bi, ls)
            ca, cbv = _scan_chunk(a, m * (i * xc), row, T, reverse=False)
            h = ca * hc + cbv
            hs_ref[rows, :] = h
            y_ref[rows, :] = (_gelu(gp_ref[rows, :]) * h).astype(BF16)
            return xb, _row_of(h, row, T - 1)

        lax.fori_loop(0, S // T, chunk,
                      (jnp.zeros((T, LANES), F32), jnp.zeros((1, LANES), F32)))

    return pl.pallas_call(
        body, name=name, grid=(N_CBLK,),
        in_specs=[col(0), col(N_CBLK), cwm, vec, mat, vec, mat, vec, vec],
        out_specs=[col(0), col(0)],
        out_shape=[jax.ShapeDtypeStruct((S, D_MODEL), BF16), jax.ShapeDtypeStruct((S, D_MODEL), F32)],
        compiler_params=_params("parallel"),
    )(u, u, conv_w, conv_b, wr, br, wi, bi, lam)


def _lru_bwd(dy, u, hs, conv_w, conv_b, wr, br, wi, bi, lam, name):
    S = u.shape[0]
    T = _token_tile(S, 512)
    n_chunk = S // T
    col, vec, mat, cwm = _lru_specs(S)

    def body(dy_ref, gp_ref, xb_ref, hs_ref, cw_ref, cb_ref, wr_ref, br_ref, wi_ref, bi_ref,
             lam_ref, dgp_ref, dxb_ref, dcw_ref, dcb_ref, dbr_ref, dbi_ref, dlam_ref, dwr_ref,
             dwi_ref):
        row = lax.broadcasted_iota(jnp.int32, (T, LANES), 0)
        lam = lam_ref[...]
        ls = _log_sigmoid(lam)
        cb, br, bi = cb_ref[...], br_ref[...], bi_ref[...]
        wr, wi = wr_ref[...], wi_ref[...]
        for ref in (dcw_ref, dcb_ref, dbr_ref, dbi_ref, dlam_ref, dwr_ref, dwi_ref):
            ref[...] = jnp.zeros_like(ref)

        def chunk(it, carry):
            g_next, dxc_next = carry
            ci = n_chunk - 1 - it
            rows = pl.ds(pl.multiple_of(ci * T, T), T)
            before = pl.ds(pl.multiple_of(jnp.maximum(ci - 1, 0) * T, T), T)
            first = ci == 0
            xb = xb_ref[rows, :]
            prev_xb = jnp.where(first, 0.0, xb_ref[before, :])
            xc, xcb, r, i, a, m = _lru_gates(xb, prev_xb, row, cw_ref, cb, wr, br, wi, bi, ls)
            h = hs_ref[rows, :]
            h_prev = _shift_down(h, jnp.where(first, 0.0, hs_ref[before, :]), row, 1)
            gp = gp_ref[rows, :]
            dyv = dy_ref[rows, :]
            dgp_ref[rows, :] = (dyv * h * _gelu_grad(gp)).astype(BF16)
            dh = dyv * _gelu(gp)
            ca, cbv = _scan_chunk(a, a * dh, row, T, reverse=True)
            gp_acc = ca * g_next + cbv
            g = dh + jnp.where(row < T - 1, pltpu.roll(gp_acc, T - 1, 0), g_next)
            da = g * h_prev - (g * (i * xc)) * a / m
            dla = da * a
            dlam_ref[...] += jnp.sum(dla * (LRU_C * r), axis=0, keepdims=True)
            dpr = (dla * (LRU_C * ls)) * r * (1.0 - r)
            dpi = (g * m * xc) * i * (1.0 - i)
            dbr_ref[...] += jnp.sum(dpr, axis=0, keepdims=True)
            dbi_ref[...] += jnp.sum(dpi, axis=0, keepdims=True)
            dprb, dpib = dpr.astype(BF16), dpi.astype(BF16)
            dwr_ref[...] += _dot_tn(xcb, dprb)
            dwi_ref[...] += _dot_tn(xcb, dpib)
            dxc = g * m * i + _dot_nt(dprb, wr) + _dot_nt(dpib, wi)
            dcb_ref[...] += jnp.sum(dxc, axis=0, keepdims=True)
            dxb = jnp.zeros((T, LANES), F32)
            for k in range(CONV_WIDTH):
                tap = pl.ds(CONV_WIDTH - 1 - k, 1)
                dcw_ref[tap, :] += jnp.sum(dxc * _shift_down(xb, prev_xb, row, k), axis=0,
                                           keepdims=True)
                dxb = dxb + cw_ref[tap, :] * _shift_up(dxc, dxc_next, row, k, T)
            dxb_ref[rows, :] = dxb.astype(BF16)
            return _row_of(gp_acc, row, 0), dxc

        lax.fori_loop(0, n_chunk, chunk,
                      (jnp.zeros((1, LANES), F32), jnp.zeros((T, LANES), F32)))
        dlam_ref[...] = dlam_ref[...] * _sigmoid(-lam)

    vec_out = jax.ShapeDtypeStruct((1, D_MODEL), F32)
    mat_out = jax.ShapeDtypeStruct((N_CBLK, LANES, LANES), F32)
    return pl.pallas_call(
        body, name=name, grid=(N_CBLK,),
        in_specs=[col(0), col(0), col(N_CBLK), col(0), cwm, vec, mat, vec, mat, vec, vec],
        out_specs=[col(0), col(0), cwm, vec, vec, vec, vec, mat, mat],
        out_shape=[jax.ShapeDtypeStruct((S, D_MODEL), BF16), jax.ShapeDtypeStruct((S, D_MODEL), BF16),
                   jax.ShapeDtypeStruct((CONV_WIDTH, D_MODEL), F32),
                   vec_out, vec_out, vec_out, vec_out, mat_out, mat_out],
        compiler_params=_params("parallel"),
    )(dy, u, u, hs, conv_w, conv_b, wr, br, wi, bi, lam)


def _head_group_matrix(value):
    r = lax.broadcasted_iota(jnp.int32, (LANES, LANES), 0) // HEAD_DIM
    c = lax.broadcasted_iota(jnp.int32, (LANES, LANES), 1) // HEAD_DIM
    return jnp.where(r == c, value, 0.0).astype(BF16)


def _group_dot(x, p):
    hi = x.astype(BF16)
    lo = (x - hi.astype(F32)).astype(BF16)
    return _dot(hi, p) + _dot(lo, p)


def _head_mean(x, p):
    return _group_dot(x, p)


def _qk_prep(u, q_gain, k_gain, name, tm=256):
    S = u.shape[0]
    tm = _token_tile(S, tm)

    def body(q_ref, k_ref, v_ref, qg_ref, kg_ref, qn_ref, kn_ref, vb_ref):
        p = _head_group_matrix(1.0 / HEAD_DIM)
        for j in range(N_CBLK):
            cl = slice(j * LANES, (j + 1) * LANES)
            for x_ref, g_ref, o_ref, scale in ((q_ref, qg_ref, qn_ref, ATTN_SCALE),
                                               (k_ref, kg_ref, kn_ref, 1.0)):
                xv = x_ref[:, cl]
                rs = lax.rsqrt(_head_mean(xv * xv, p) + EPS)
                o_ref[:, cl] = (xv * rs * g_ref[...]).astype(BF16) * scale
        vb_ref[...] = v_ref[...].astype(BF16)

    blk = lambda off: pl.BlockSpec((tm, D_MODEL), lambda i: (i, off))
    out = jax.ShapeDtypeStruct((S, D_MODEL), BF16)
    return pl.pallas_call(
        body, name=name, grid=(S // tm,),
        in_specs=[blk(0), blk(1), blk(2), _resident((1, LANES)), _resident((1, LANES))],
        out_specs=[blk(0), blk(0), blk(0)],
        out_shape=[out, out, out],
        compiler_params=_params("parallel"),
    )(u, u, u, q_gain, k_gain)


def _qk_bwd(u, dqn, dkn, q_gain, k_gain, name, tm=256):
    S = u.shape[0]
    tm = _token_tile(S, tm)

    def body(q_ref, k_ref, dqn_ref, dkn_ref, qg_ref, kg_ref, dq_ref, dk_ref, dqg_ref, dkg_ref):
        p = _head_group_matrix(1.0 / HEAD_DIM)
        for x_ref, dn_ref, g_ref, dx_ref, dg_ref, scale in (
                (q_ref, dqn_ref, qg_ref, dq_ref, dqg_ref, ATTN_SCALE),
                (k_ref, dkn_ref, kg_ref, dk_ref, dkg_ref, 1.0)):
            dg = jnp.zeros((1, LANES), F32)
            for j in range(N_CBLK):
                cl = slice(j * LANES, (j + 1) * LANES)
                xv, dn = x_ref[:, cl], dn_ref[:, cl] * scale
                rs = lax.rsqrt(_head_mean(xv * xv, p) + EPS)
                xhat = xv * rs
                dxhat = dn * g_ref[...]
                dx_ref[:, cl] = (rs * (dxhat - xhat * _head_mean(dxhat * xhat, p))).astype(BF16)
                dg = dg + jnp.sum(dn * xhat, axis=0, keepdims=True)

            @pl.when(pl.program_id(0) == 0)
            def _():
                dg_ref[...] = jnp.zeros_like(dg_ref)
            dg_ref[...] += dg

            @pl.when(pl.program_id(0) == S // tm - 1)
            def _():
                dg_ref[...] += pltpu.roll(dg_ref[...], HEAD_DIM, 1)

    blk = lambda off: pl.BlockSpec((tm, D_MODEL), lambda i: (i, off))
    acc = pl.BlockSpec((1, LANES), lambda i: (0, 0))
    out = jax.ShapeDtypeStruct((S, D_MODEL), BF16)
    vec = jax.ShapeDtypeStruct((1, LANES), F32)
    return pl.pallas_call(
        body, name=name, grid=(S // tm,),
        in_specs=[blk(0), blk(1), blk(0), blk(0), _resident((1, LANES)), _resident((1, LANES))],
        out_specs=[blk(0), blk(0), acc, acc],
        out_shape=[out, out, vec, vec],
        compiler_params=_params("arbitrary"),
    )(u, u, dqn, dkn, q_gain, k_gain)


def _forget_fwd(f, b_f, name):
    S = f.shape[0]
    T = _token_tile(S, 256)

    def body(f_ref, b_ref, c_ref):
        row = lax.broadcasted_iota(jnp.int32, (T, LANES), 0)
        ones = jnp.ones((T, LANES), F32)
        bias = b_ref[...]

        def chunk(ci, carry):
            rows = pl.ds(pl.multiple_of(ci * T, T), T)
            _, c = _scan_chunk(ones, _log_sigmoid(f_ref[rows, :] + bias), row, T, reverse=False)
            c = c + carry
            c_ref[rows, :] = c
            return _row_of(c, row, T - 1)

        lax.fori_loop(0, S // T, chunk, jnp.zeros((1, LANES), F32))

    return pl.pallas_call(
        body, name=name,
        in_specs=[pl.BlockSpec(memory_space=pltpu.VMEM)] * 2,
        out_specs=pl.BlockSpec(memory_space=pltpu.VMEM),
        out_shape=jax.ShapeDtypeStruct((S, LANES), F32),
        compiler_params=pltpu.CompilerParams(vmem_limit_bytes=VMEM_LIMIT),
    )(f, b_f)


def _forget_bwd(dc_k, rho, f, b_f, name):
    S = f.shape[0]
    T = _token_tile(S, 256)
    n_chunk = S // T

    def body(dck_ref, rho_ref, f_ref, b_ref, df_ref, db_ref):
        row = lax.broadcasted_iota(jnp.int32, (T, LANES), 0)
        ones = jnp.ones((T, LANES), F32)
        bias = b_ref[...]
        pick = (lax.broadcasted_iota(jnp.int32, (D_MODEL, LANES), 0)
                == HEAD_DIM * lax.broadcasted_iota(jnp.int32, (D_MODEL, LANES), 1))
        pick = jnp.where(pick, 1.0, 0.0).astype(BF16)

        def chunk(it, carry):
            tail, db = carry
            rows = pl.ds(pl.multiple_of((n_chunk - 1 - it) * T, T), T)
            dc = dck_ref[rows, :] + _group_dot(rho_ref[rows, :], pick)
            _, dlf = _scan_chunk(ones, dc, row, T, reverse=True)
            dlf = dlf + tail
            df = dlf * _sigmoid(-(f_ref[rows, :] + bias))
            df_ref[rows, :] = df
            return _row_of(dlf, row, 0), db + jnp.sum(df, axis=0, keepdims=True)

        zero = jnp.zeros((1, LANES), F32)
        _, db = lax.fori_loop(0, n_chunk, chunk, (zero, zero))
        db_ref[...] = db

    return pl.pallas_call(
        body, name=name,
        in_specs=[pl.BlockSpec(memory_space=pltpu.VMEM)] * 4,
        out_specs=[pl.BlockSpec(memory_space=pltpu.VMEM)] * 2,
        out_shape=[jax.ShapeDtypeStruct((S, LANES), F32), jax.ShapeDtypeStruct((1, LANES), F32)],
        compiler_params=pltpu.CompilerParams(vmem_limit_bytes=VMEM_LIMIT),
    )(dc_k, rho, f, b_f)


ATTN_TILE = 512
ATTN_ROWS_FWD = 32


def _attn_tiles(S):
    t = _token_tile(S, ATTN_TILE)
    return t, S // t


def _causal(T):
    return (lax.broadcasted_iota(jnp.int32, (T, T), 1)
            <= lax.broadcasted_iota(jnp.int32, (T, T), 0))


def _attn_fwd(qs_, kn, vb, c_row, name):
    S = qs_.shape[0]
    T, n_t = _attn_tiles(S)
    RB = min(T, ATTN_ROWS_FWD)

    def body(q_ref, k_ref, v_ref, cr_ref, o_ref, lse_ref, sa_ref, sb_ref, p_ref, m_ref, l_ref,
             acc_ref, a_ref):
        qi = pl.program_id(1)
        lanes = [slice(h2 * HEAD_DIM, (h2 + 1) * HEAD_DIM) for h2 in range(2)]
        col = lax.broadcasted_iota(jnp.int32, (RB, T), 1)
        row = lax.broadcasted_iota(jnp.int32, (RB, T), 0)
        m_ref[...] = jnp.full(m_ref.shape, NEG_INF, F32)
        l_ref[...] = jnp.zeros_like(l_ref)
        acc_ref[...] = jnp.zeros_like(acc_ref)

        def logits_into(s_ref, kj):
            ks = pl.ds(pl.multiple_of(kj * T, T), T)
            for h2, hl in enumerate(lanes):
                s_ref[h2] = _dot_nt(q_ref[:, hl], k_ref[ks, hl]) - cr_ref[h2:h2 + 1, ks]

        def consume(s_ref, kj, masked):
            ks = pl.ds(pl.multiple_of(kj * T, T), T)
            for h2, hl in enumerate(lanes):
                blocks = [slice(i * RB, (i + 1) * RB) for i in range(T // RB)]

                def logits(i, rows):
                    s = s_ref[h2, rows, :]
                    return jnp.where(col <= row + i * RB, s, NEG_INF) if masked else s

                wide = lambda x: jnp.broadcast_to(x, (RB, LANES))
                for i, rows in enumerate(blocks):
                    mx = wide(jnp.max(logits(i, rows), axis=1, keepdims=True))
                    a_ref[h2, rows, :] = m_ref[h2, rows, :]
                    m_ref[h2, rows, :] = jnp.maximum(m_ref[h2, rows, :], mx)
                for i, rows in enumerate(blocks):
                    m_new = m_ref[h2, rows, :]
                    p = jnp.exp(logits(i, rows) - jnp.tile(m_new, (1, T // LANES)))
                    alpha = jnp.exp(a_ref[h2, rows, :] - m_new)
                    a_ref[h2, rows, :] = alpha
                    l_ref[h2, rows, :] = (alpha * l_ref[h2, rows, :]
                                          + wide(jnp.sum(p, axis=1, keepdims=True)))
                    p_ref[h2, rows, :] = p.astype(BF16)
                acc_ref[h2] = (a_ref[h2, :, :HEAD_DIM] * acc_ref[h2]
                               + _dot(p_ref[h2], v_ref[ks, hl]))

        logits_into(sa_ref, 0)

        def pair(i, _):
            logits_into(sb_ref, 2 * i + 1)
            consume(sa_ref, 2 * i, False)
            logits_into(sa_ref, 2 * i + 2)
            consume(sb_ref, 2 * i + 1, False)
            return 0

        lax.fori_loop(0, qi // 2, pair, 0)

        @pl.when(qi % 2 == 1)
        def _():
            logits_into(sb_ref, qi)
            consume(sa_ref, qi - 1, False)
            consume(sb_ref, qi, True)

        @pl.when(qi % 2 == 0)
        def _():
            consume(sa_ref, qi, True)

        for h2, hl in enumerate(lanes):
            o_ref[:, hl] = (acc_ref[h2] / l_ref[h2, :, :HEAD_DIM]).astype(BF16)
            lse_ref[:, hl] = m_ref[h2, :, :HEAD_DIM] + jnp.log(l_ref[h2, :, :HEAD_DIM])

    qblk = pl.BlockSpec((T, LANES), lambda h, i: (i, h))
    kv = pl.BlockSpec((S, LANES), lambda h, i: (0, h))
    return pl.pallas_call(
        body, name=name, grid=(N_CBLK, n_t),
        in_specs=[qblk, kv, kv, pl.BlockSpec((None, 2, S), lambda h, i: (h, 0, 0))],
        out_specs=[qblk, qblk],
        out_shape=[jax.ShapeDtypeStruct((S, D_MODEL), BF16),
                   jax.ShapeDtypeStruct((S, D_MODEL), F32)],
        scratch_shapes=[pltpu.VMEM((2, T, T), F32), pltpu.VMEM((2, T, T), F32),
                        pltpu.VMEM((2, T, T), BF16),
                        pltpu.VMEM((2, T, LANES), F32), pltpu.VMEM((2, T, LANES), F32),
                        pltpu.VMEM((2, T, HEAD_DIM), F32), pltpu.VMEM((2, T, LANES), F32)],
        compiler_params=_params("parallel", "parallel"),
    )(qs_, kn, vb, c_row)


def _attn_bwd(qs_, kn, vb, do, o, lse, c_row, name):
    S = qs_.shape[0]
    T, n_t = _attn_tiles(S)

    def body(q_ref, k_ref, v_ref, do_ref, o_ref, lse_ref, cr_ref,
             dq_ref, dk_ref, dv_ref, dc_ref, rho_ref, dd_ref):
        kj = pl.program_id(1)
        causal = _causal(T)
        lanes = [slice(h2 * HEAD_DIM, (h2 + 1) * HEAD_DIM) for h2 in range(2)]
        ones = [slice(h2 * HEAD_DIM, h2 * HEAD_DIM + 1) for h2 in range(2)]

        @pl.when(kj == 0)
        def _():
            dq_ref[...] = jnp.zeros_like(dq_ref)
            rho_ref[...] = jnp.zeros_like(rho_ref)
            p_sum = _head_group_matrix(1.0)

            def fill(ci, _):
                rows = pl.ds(pl.multiple_of(ci * T, T), T)
                dd_ref[rows, :] = _group_dot(do_ref[rows, :].astype(F32) * o_ref[rows, :].astype(F32),
                                             p_sum)
                return 0

            lax.fori_loop(0, n_t, fill, 0)

        kh = [k_ref[:, hl] for hl in lanes]
        vh = [v_ref[:, hl] for hl in lanes]
        ck = [cr_ref[h2:h2 + 1, :] for h2 in range(2)]

        def step(qi, carry, masked):
            qs = pl.ds(pl.multiple_of(qi * T, T), T)
            out = []
            for h2, hl in enumerate(lanes):
                dk, dv, dc = carry[h2]
                qh, doh = q_ref[qs, hl], do_ref[qs, hl]
                s = _dot_nt(qh, kh[h2]) - ck[h2]
                if masked:
                    s = jnp.where(causal, s, NEG_INF)
                p = jnp.exp(s - lse_ref[qs, ones[h2]])
                ds = p * (_dot_nt(doh, vh[h2]) - dd_ref[qs, ones[h2]])
                dsb = ds.astype(BF16)
                dq_ref[qs, hl] += _dot(dsb, kh[h2])
                rho_ref[qs, hl] += jnp.broadcast_to(jnp.sum(ds, axis=1, keepdims=True),
                                                    (T, HEAD_DIM))
                out.append((dk + _dot_tn(dsb, qh), dv + _dot_tn(p.astype(BF16), doh),
                            dc - jnp.sum(ds, axis=0, keepdims=True)))
            return tuple(out)

        init = tuple((jnp.zeros((T, HEAD_DIM), F32), jnp.zeros((T, HEAD_DIM), F32),
                      jnp.zeros((1, T), F32)) for _ in lanes)
        carry = step(kj, init, True)
        carry = lax.fori_loop(kj + 1, n_t, lambda qi, c: step(qi, c, False), carry)
        for h2, ((dk, dv, dc), hl) in enumerate(zip(carry, lanes)):
            dk_ref[:, hl] = dk
            dv_ref[:, hl] = dv.astype(BF16)
            dc_ref[h2:h2 + 1, :] = dc

    kblk = pl.BlockSpec((T, LANES), lambda h, j: (j, h))
    full = pl.BlockSpec((S, LANES), lambda h, j: (0, h))
    crow = pl.BlockSpec((None, 2, T), lambda h, j: (h, 0, j))
    wide = jax.ShapeDtypeStruct((S, D_MODEL), F32)
    return pl.pallas_call(
        body, name=name, grid=(N_CBLK, n_t),
        in_specs=[full, kblk, kblk, full, full, full, crow],
        out_specs=[full, kblk, kblk, crow, full],
        out_shape=[wide, wide, jax.ShapeDtypeStruct((S, D_MODEL), BF16),
                   jax.ShapeDtypeStruct((N_CBLK, 2, S), F32), wide],
        scratch_shapes=[pltpu.VMEM((S, LANES), F32)],
        compiler_params=_params("parallel", "arbitrary"),
    )(qs_, kn, vb, do, o, lse, c_row)


ALL_PEERS = tuple(range(1, N_DEV))
NEAR_PEERS = (1, 2, 4, 6)
FAR_CHIPS = (2, 4, 6)


def _landing_shapes(arrays, gathers):
    return [jax.ShapeDtypeStruct((N_DEV,) + a.shape if g else a.shape, a.dtype)
            for a, g in zip(arrays, gathers)]


def _my_index():
    return 4 * lax.axis_index("x") + 2 * lax.axis_index("y") + lax.axis_index("c")


def _own_copies(srcs, lands, gathers, sems):
    me = _my_index()
    return [pltpu.make_async_copy(src if g else src.at[me], land.at[me], sems.at[a])
            for a, (src, land, g) in enumerate(zip(srcs, lands, gathers))]


def _peer_copies(srcs, lands, gathers, send_sems, recv_sems, ks=ALL_PEERS):
    x, y, c = lax.axis_index("x"), lax.axis_index("y"), lax.axis_index("c")
    me = 4 * x + 2 * y + c
    out = []
    for j, k in enumerate(ks):
        to = (1 - x if k & 4 else x, 1 - y if k & 2 else y, 1 - c if k & 1 else c)
        peer = 4 * to[0] + 2 * to[1] + to[2]
        for a, (src, land, g) in enumerate(zip(srcs, lands, gathers)):
            sem = a * len(ks) + j
            src_blk = src if g else src.at[peer]

            def copy(slot, src_blk=src_blk, land=land, sem=sem, to=to):
                return pltpu.make_async_remote_copy(
                    src_ref=src_blk, dst_ref=land.at[slot], send_sem=send_sems.at[sem],
                    recv_sem=recv_sems.at[sem], device_id=to,
                    device_id_type=pl.DeviceIdType.MESH)

            out.append((k, a, copy(me), copy(peer)))
    return out


def _forward_copies(lands, send_sems, recv_sems):
    x, y, c = lax.axis_index("x"), lax.axis_index("y"), lax.axis_index("c")
    out = []
    for j, f in enumerate(FAR_CHIPS):
        chip = 4 * (1 - x if f & 4 else x) + 2 * (1 - y if f & 2 else y)
        for a, land in enumerate(lands):
            sem = a * len(FAR_CHIPS) + j

            def copy(slot, land=land, sem=sem):
                return pltpu.make_async_remote_copy(
                    src_ref=land.at[slot], dst_ref=land.at[slot], send_sem=send_sems.at[sem],
                    recv_sem=recv_sems.at[sem], device_id=(x, y, 1 - c),
                    device_id_type=pl.DeviceIdType.MESH)

            out.append((f, a, copy(chip + c), copy(chip + 1 - c)))
    return out


def _exchange(arrays, gathers, name, two_level=False):
    n = len(arrays)
    ks = NEAR_PEERS if two_level else ALL_PEERS
    assert not two_level or all(gathers)

    def body(*refs):
        ins, outs = refs[:n], refs[n:2 * n]
        send_sems, recv_sems, own_sems, fwd_send_sems, fwd_recv_sems = refs[2 * n:]
        own = _own_copies(ins, outs, gathers, own_sems)
        for cp in own:
            cp.start()
        copies = _peer_copies(ins, outs, gathers, send_sems, recv_sems, ks)
        for _, _, send, _ in copies:
            send.start()
        passed = {}
        if two_level:
            passed = {(f, a): (send, arrival)
                      for f, a, send, arrival in _forward_copies(outs, fwd_send_sems, fwd_recv_sems)}
        for k, a, _, arrival in copies:
            arrival.wait_recv()
            if (k, a) in passed:
                passed[k, a][0].start()
        for send, arrival in passed.values():
            arrival.wait_recv()
            send.wait_send()
        for _, _, send, _ in copies:
            send.wait_send()
        for cp in own:
            cp.wait()

    hbm = pl.BlockSpec(memory_space=pl.ANY)
    return pl.pallas_call(
        body, name=name,
        in_specs=[hbm] * n, out_specs=[hbm] * n, out_shape=_landing_shapes(arrays, gathers),
        scratch_shapes=[pltpu.SemaphoreType.DMA((n * len(ks),)),
                        pltpu.SemaphoreType.DMA((n * len(ks),)),
                        pltpu.SemaphoreType.DMA((n,)),
                        pltpu.SemaphoreType.DMA((n * len(FAR_CHIPS),)),
                        pltpu.SemaphoreType.DMA((n * len(FAR_CHIPS),))],
        compiler_params=pltpu.CompilerParams(has_side_effects=True),
    )(*arrays)


_HBM = pl.BlockSpec(memory_space=pltpu.HBM)
_SEM = pl.BlockSpec(memory_space=pltpu.SEMAPHORE)
_ANY = pl.BlockSpec(memory_space=pl.ANY)
_DATAFLOW = pltpu.SideEffectType.DATAFLOW_SIDE_EFFECTING


def _in_hbm(a):
    return pltpu.with_memory_space_constraint(a, pltpu.HBM)


def _exchange_start(arrays, gathers, after, name, ks=ALL_PEERS):
    n = len(arrays)
    lands = [lax.empty(s.shape, s.dtype) for s in _landing_shapes(arrays, gathers)]

    def body(*refs):
        srcs, dsts = refs[:n], refs[n:2 * n]
        send_sems, recv_sems, own_sems = refs[2 * n + 1:2 * n + 4]
        token = refs[-1]
        for cp in _own_copies(srcs, dsts, gathers, own_sems):
            cp.start()
        for _, _, send, _ in _peer_copies(srcs, dsts, gathers, send_sems, recv_sems, ks):
            send.start()
        token[...] = jnp.zeros_like(token)

    hbm_like = [pltpu.HBM(a.shape, a.dtype) for a in list(arrays) + lands]
    res = pl.pallas_call(
        body, name=name,
        in_specs=[_HBM] * (2 * n) + [_ANY],
        out_specs=(_SEM, _SEM, _SEM, *[_HBM] * (2 * n), pl.BlockSpec(memory_space=pltpu.VMEM)),
        out_shape=(pltpu.SemaphoreType.DMA((n * len(ks),)), pltpu.SemaphoreType.DMA((n * len(ks),)),
                   pltpu.SemaphoreType.DMA((n,)), *hbm_like,
                   jax.ShapeDtypeStruct((8, LANES), F32)),
        input_output_aliases={i: 3 + i for i in range(2 * n)},
        compiler_params=pltpu.CompilerParams(has_side_effects=_DATAFLOW),
    )(*[_in_hbm(a) for a in list(arrays) + lands], after)
    return (res[0], res[1], res[2], res[3:3 + n], res[3 + n:3 + 2 * n]), res[-1]


def _exchange_wait(started, gathers, after, name, ks=ALL_PEERS):
    send_sems, recv_sems, own_sems, arrays, lands = started
    n = len(arrays)

    def body(*refs):
        srcs, dsts = refs[:n], refs[n:2 * n]
        for _, _, send, arrival in _peer_copies(srcs, dsts, gathers, refs[2 * n], refs[2 * n + 1],
                                                ks):
            arrival.wait_recv()
            send.wait_send()
        for cp in _own_copies(srcs, dsts, gathers, refs[2 * n + 2]):
            cp.wait()

    hbm_like = [pltpu.HBM(a.shape, a.dtype) for a in list(arrays) + list(lands)]
    res = pl.pallas_call(
        body, name=name,
        in_specs=[_HBM] * (2 * n) + [_SEM, _SEM, _SEM, _ANY],
        out_specs=[_HBM] * (2 * n), out_shape=hbm_like,
        input_output_aliases={i: i for i in range(2 * n)},
        compiler_params=pltpu.CompilerParams(has_side_effects=_DATAFLOW),
    )(*arrays, *lands, send_sems, recv_sems, own_sems, after)
    return res[n:]


def _forward_start(lands, after, name):
    n = len(lands)

    def body(*refs):
        send_sems, recv_sems = refs[n + 1:n + 3]
        for _, _, send, _ in _forward_copies(refs[:n], send_sems, recv_sems):
            send.start()
        refs[-1][...] = jnp.zeros_like(refs[-1])

    n_sem = n * len(FAR_CHIPS)
    res = pl.pallas_call(
        body, name=name,
        in_specs=[_HBM] * n + [_ANY],
        out_specs=(_SEM, _SEM, *[_HBM] * n, pl.BlockSpec(memory_space=pltpu.VMEM)),
        out_shape=(pltpu.SemaphoreType.DMA((n_sem,)), pltpu.SemaphoreType.DMA((n_sem,)),
                   *[pltpu.HBM(a.shape, a.dtype) for a in lands],
                   jax.ShapeDtypeStruct((8, LANES), F32)),
        input_output_aliases={i: 2 + i for i in range(n)},
        compiler_params=pltpu.CompilerParams(has_side_effects=_DATAFLOW),
    )(*[_in_hbm(a) for a in lands], after)
    return (res[0], res[1], res[2:2 + n]), res[-1]


def _forward_wait(started, after, name):
    send_sems, recv_sems, lands = started
    n = len(lands)

    def body(*refs):
        for _, _, send, arrival in _forward_copies(refs[:n], refs[n], refs[n + 1]):
            arrival.wait_recv()
            send.wait_send()

    return pl.pallas_call(
        body, name=name,
        in_specs=[_HBM] * n + [_SEM, _SEM, _ANY],
        out_specs=[_HBM] * n, out_shape=[pltpu.HBM(a.shape, a.dtype) for a in lands],
        input_output_aliases={i: i for i in range(n)},
        compiler_params=pltpu.CompilerParams(has_side_effects=_DATAFLOW),
    )(*lands, send_sems, recv_sems, after)


def _reduce_adamw(parts, w, m, v, name):
    n_layer = len(parts)
    n, R, C = parts[0].shape
    tr = 256 if R % 256 == 0 else R
    n_t = R // tr

    def body(*refs):
        p_refs = refs[:n_layer]
        w_ref, m_ref, v_ref, g_ref, d_ref, nm_ref, nv_ref = refs[n_layer:]

        def update(p_ref):
            g = p_ref[0].astype(F32)
            for s in range(1, n):
                g = g + p_ref[s].astype(F32)
            g_ref[...] = g
            m_new = ADAM_B1 * m_ref[...] + (1.0 - ADAM_B1) * g
            v_new = ADAM_B2 * v_ref[...] + (1.0 - ADAM_B2) * (g * g)
            nm_ref[...] = m_new
            nv_ref[...] = v_new
            m_hat = m_new / (1.0 - ADAM_B1 ** ADAM_STEP)
            v_hat = v_new / (1.0 - ADAM_B2 ** ADAM_STEP)
            d_ref[...] = -ADAM_LR * (m_hat / (jnp.sqrt(v_hat) + ADAM_EPS) + ADAM_WD * w_ref[...])

        for layer, p_ref in enumerate(p_refs):
            pl.when(pl.program_id(0) == layer)(functools.partial(update, p_ref))

    def parts_spec(layer):
        def index(l, i):
            return 0, jnp.where(l < layer, 0, jnp.where(l > layer, n_t - 1, i)), 0
        return pl.BlockSpec((n, tr, C), index)

    blk = pl.BlockSpec((None, tr, C), lambda l, i: (l, i, 0))
    out = jax.ShapeDtypeStruct((n_layer, R, C), F32)
    return pl.pallas_call(
        body, name=name, grid=(n_layer, n_t),
        in_specs=[parts_spec(layer) for layer in range(n_layer)] + [blk, blk, blk],
        out_specs=[blk] * 4, out_shape=[out] * 4,
        compiler_params=_params("arbitrary", "arbitrary"),
    )(*parts, w, m, v)


def _pack(arrays):
    flat = jnp.concatenate([a.reshape(-1).astype(F32) for a in arrays])
    pad = (-flat.shape[0]) % (8 * LANES)
    return jnp.pad(flat, (0, pad)).reshape(-1, LANES)


def _unpack(buf, shapes):
    flat = buf.reshape(-1)
    out, off = [], 0
    for shp in shapes:
        size = 1
        for s in shp:
            size *= s
        out.append(flat[off:off + size].reshape(shp))
        off += size
    return out


def _block_diag_pairs(w):
    w = w.reshape(N_CBLK, 2, LRU_BLOCK_DIM, LRU_BLOCK_DIM)
    z = jnp.zeros_like(w[:, 0])
    top = jnp.concatenate([w[:, 0], z], axis=2)
    bot = jnp.concatenate([z, w[:, 1]], axis=2)
    return jnp.concatenate([top, bot], axis=1)


def _diag_pairs(m):
    h = LRU_BLOCK_DIM
    return jnp.stack([m[:, :h, :h], m[:, h:, h:]], axis=1).reshape(2 * N_CBLK, h, h)


SMALL = ("mlp_norm", "lru_conv_b", "lru_w_r", "lru_b_r", "lru_w_i", "lru_b_i",
         "lru_lambda", "fox_b_f", "fox_q_gain", "fox_k_gain")
WEIGHTS = ("mix_norm", "mlp_norm", "mlp_w1", "mlp_w2", "lru_w_in", "lru_conv_w", "lru_conv_b",
           "lru_w_r", "lru_b_r", "lru_w_i", "lru_b_i", "lru_lambda", "lru_w_out", "fox_w_in",
           "fox_b_f", "fox_q_gain", "fox_k_gain", "fox_w_out")


def kernel(x, mix_norm, mlp_norm, mlp_w1, mlp_w2, lru_w_in, lru_conv_w, lru_conv_b, lru_w_r, lru_b_r, lru_w_i, lru_b_i, lru_lambda, lru_w_out, fox_w_in, fox_b_f, fox_q_gain, fox_k_gain, fox_w_out, loss_target, m_mix_norm, m_mlp_norm, m_mlp_w1, m_mlp_w2, m_lru_w_in, m_lru_conv_w, m_lru_conv_b, m_lru_w_r, m_lru_b_r, m_lru_w_i, m_lru_b_i, m_lru_lambda, m_lru_w_out, m_fox_w_in, m_fox_b_f, m_fox_q_gain, m_fox_k_gain, m_fox_w_out, v_mix_norm, v_mlp_norm, v_mlp_w1, v_mlp_w2, v_lru_w_in, v_lru_conv_w, v_lru_conv_b, v_lru_w_r, v_lru_b_r, v_lru_w_i, v_lru_b_i, v_lru_lambda, v_lru_w_out, v_fox_w_in, v_fox_b_f, v_fox_q_gain, v_fox_k_gain, v_fox_w_out):
    w_in = dict(mix_norm=mix_norm, mlp_norm=mlp_norm, mlp_w1=mlp_w1, mlp_w2=mlp_w2,
                lru_w_in=lru_w_in, lru_conv_w=lru_conv_w, lru_conv_b=lru_conv_b, lru_w_r=lru_w_r,
                lru_b_r=lru_b_r, lru_w_i=lru_w_i, lru_b_i=lru_b_i, lru_lambda=lru_lambda,
                lru_w_out=lru_w_out, fox_w_in=fox_w_in, fox_b_f=fox_b_f, fox_q_gain=fox_q_gain,
                fox_k_gain=fox_k_gain, fox_w_out=fox_w_out)
    m_in = dict(mix_norm=m_mix_norm, mlp_norm=m_mlp_norm, mlp_w1=m_mlp_w1, mlp_w2=m_mlp_w2,
                lru_w_in=m_lru_w_in, lru_conv_w=m_lru_conv_w, lru_conv_b=m_lru_conv_b,
                lru_w_r=m_lru_w_r, lru_b_r=m_lru_b_r, lru_w_i=m_lru_w_i, lru_b_i=m_lru_b_i,
                lru_lambda=m_lru_lambda, lru_w_out=m_lru_w_out, fox_w_in=m_fox_w_in,
                fox_b_f=m_fox_b_f, fox_q_gain=m_fox_q_gain, fox_k_gain=m_fox_k_gain,
                fox_w_out=m_fox_w_out)
    v_in = dict(mix_norm=v_mix_norm, mlp_norm=v_mlp_norm, mlp_w1=v_mlp_w1, mlp_w2=v_mlp_w2,
                lru_w_in=v_lru_w_in, lru_conv_w=v_lru_conv_w, lru_conv_b=v_lru_conv_b,
                lru_w_r=v_lru_w_r, lru_b_r=v_lru_b_r, lru_w_i=v_lru_w_i, lru_b_i=v_lru_b_i,
                lru_lambda=v_lru_lambda, lru_w_out=v_lru_w_out, fox_w_in=v_fox_w_in,
                fox_b_f=v_fox_b_f, fox_q_gain=v_fox_q_gain, fox_k_gain=v_fox_k_gain,
                fox_w_out=v_fox_w_out)
    D = D_MODEL
    S = x.shape[1]
    x0, target = x[0], loss_target[0]
    me = 4 * lax.axis_index("x") + 2 * lax.axis_index("y") + lax.axis_index("c")

    def bf16(a):
        return a.astype(BF16)

    (lru_in_g,) = _exchange([bf16(lru_w_in[0])], [True], "gather_lru_in", two_level=True)
    gather_lru, tok = _exchange_start([bf16(lru_w_out[0]), lru_conv_w[0]], [True] * 2, lru_in_g,
                                      "gather_lru_start")
    gather_mlp0, tok = _exchange_start([bf16(mlp_w1[0]), bf16(mlp_w2[0])], [True] * 2, tok,
                                       "gather_mlp0_start", NEAR_PEERS)
    gather_fox, tok = _exchange_start([bf16(fox_w_in[0]), bf16(fox_w_out[0])], [True] * 2, tok,
                                      "gather_fox_start")
    gather_mlp1, tok = _exchange_start([bf16(mlp_w1[1]), bf16(mlp_w2[1])], [True] * 2, tok,
                                       "gather_mlp1_start", NEAR_PEERS)

    def pass_on(started, after, name):
        lands = _exchange_wait(started, [True] * 2, after, name + "_wait", NEAR_PEERS)
        return _forward_start(lands, after, name + "_pass_start")
    wr =_block_diag_pairs(lru_w_r[0]).astype(BF16)
    wi = _block_diag_pairs(lru_w_i[0]).astype(BF16)
    b_r, b_i = lru_b_r.reshape(1, D), lru_b_i.reshape(1, D)
    q_gain, k_gain = jnp.tile(fox_q_gain, (1, 2)), jnp.tile(fox_k_gain, (1, 2))
    b_f = jnp.pad(fox_b_f, ((0, 0), (0, LANES - N_HEADS)))
    g_mix0, g_mix1 = mix_norm[0:1] + tok[0, 0], mix_norm[1:2]
    g_mlp0, g_mlp1 = mlp_norm[0:1], mlp_norm[1:2]

    (u0,), h0 = _norm_matmul(x0, g_mix0, [lru_in_g], "lru_in_proj")
    lru_out_g, conv_g = _exchange_wait(gather_lru, [True] * 2, u0, "gather_lru_wait")
    lru_out_w = lru_out_g.reshape(D, D)
    conv_w = conv_g.transpose(1, 0, 2).reshape(CONV_WIDTH, D)
    y_lru, hs =_lru_fwd(u0, conv_w, lru_conv_b, wr, b_r, wi, b_i, lru_lambda, "lru_core")
    pass_mlp0, tok = pass_on(gather_mlp0, y_lru, "gather_mlp0")
    x1 = _matmul_res(y_lru, lru_out_w, x0, "lru_out_proj", tok)
    w1g0, w2g0 = _forward_wait(pass_mlp0, x1, "gather_mlp0_pass_wait")
    x2, h1, r1 = _mlp_fwd(x1, g_mlp0, w1g0, w2g0, "mlp0")
    fox_in_g, fox_out_g = _exchange_wait(gather_fox, [True] * 2, x2, "gather_fox_wait")
    fox_out_w = fox_out_g.reshape(D, D)
    fox_full = fox_in_g.transpose(1, 0, 2).reshape(D, 3 * D + N_HEADS)
    wqkv = fox_full[:, :3 * D].reshape(D, 3, D).transpose(1, 0, 2)
    wf = jnp.pad(fox_full[:, 3 * D:], ((0, 0), (0, LANES - N_HEADS)))[None]
    (u_qkv, f), h2 = _norm_matmul(x2, g_mix1, [wqkv, wf], "fox_in_proj")
    qn, kn, vb = _qk_prep(u_qkv, q_gain, k_gain, "fox_qk_norm")
    c_col = _forget_fwd(f, b_f, "fox_forget")
    c_row = c_col[:, :N_HEADS].T.reshape(N_CBLK, 2, S)
    o, lse = _attn_fwd(qn, kn, vb, c_row, "fox_attn")
    pass_mlp1, tok = pass_on(gather_mlp1, o, "gather_mlp1")
    x3 = _matmul_res(o, fox_out_w, x2, "fox_out_proj", tok)
    w1g1, w2g1 = _forward_wait(pass_mlp1, x3, "gather_mlp1_pass_wait")
    loss_local, dx4, h3, r3 = _mlp_fwd(x3, g_mlp1, w1g1, w2g1, "mlp1", target)

    dx3, dg_mlp1, da3 = _mlp_bwd(dx4, x3, g_mlp1, r3, w1g1, w2g1, "mlp1_bwd")
    dw1_1 = _matmul_tn(h3, da3, "mlp1_dw1", cols=2, col_blocks=N_DEV)
    dw2_1 = _matmul_tn(r3, dx4, "mlp1_dw2", rows=2, a_square=True).reshape(N_DEV, -1, D)
    grads_mlp1, tok = _exchange_start([dw1_1, dw2_1], [False] * 2, tok, "grads_mlp1_start")
    do = _matmul_nt(dx3, fox_out_w, "fox_out_bwd", BF16, tok)
    d_fox_out = _matmul_tn(o, dx3, "fox_out_dw").reshape(N_DEV, -1, D)
    dqn, dkn, dv, dc_row, rho = _attn_bwd(qn, kn, vb, do, o, lse, c_row, "fox_attn_bwd")
    duq, duk, dq_gain, dk_gain = _qk_bwd(u_qkv, dqn, dkn, q_gain, k_gain, "fox_qk_norm_bwd")
    dc_k = jnp.pad(dc_row.reshape(N_HEADS, S).T, ((0, 0), (0, LANES - N_HEADS)))
    df, db_f = _forget_bwd(dc_k, rho, f, b_f, "fox_forget_bwd")
    dx2, dg_mix1 = _proj_bwd([[duq, duk, dv], [df]], [wqkv, wf], x2, g_mix1, dx3, "fox_in_bwd")
    d_fox_in = jnp.concatenate(
        [_matmul_tn(h2, duq, "fox_in_dwq"), _matmul_tn(h2, duk, "fox_in_dwk"),
         _matmul_tn(h2, dv, "fox_in_dwv"), _matmul_tn(h2, df, "fox_in_dwf")[:, :N_HEADS]], axis=1)
    d_fox_in = d_fox_in.reshape(D, N_DEV, -1).transpose(1, 0, 2)
    grads_fox, tok = _exchange_start([d_fox_in, d_fox_out], [False] * 2, tok, "grads_fox_start")
    dx1, dg_mlp0, da1 = _mlp_bwd(dx2, x1, g_mlp0 + tok[0, 0], r1, w1g0, w2g0, "mlp0_bwd")
    dw1_0 = _matmul_tn(h1, da1, "mlp0_dw1", cols=2, col_blocks=N_DEV)
    dw2_0 = _matmul_tn(r1, dx2, "mlp0_dw2", rows=2, a_square=True).reshape(N_DEV, -1, D)
    grads_mlp0, tok = _exchange_start([dw1_0, dw2_0], [False] * 2, tok, "grads_mlp0_start")
    dy_lru = _matmul_nt(dx1, lru_out_w, "lru_out_bwd", F32, tok)
    d_lru_out = _matmul_tn(y_lru, dx1, "lru_out_dw").reshape(N_DEV, -1, D)
    dgp, dxb, d_conv_w, d_conv_b, d_b_r, d_b_i, d_lam, d_wr, d_wi = _lru_bwd(
        dy_lru, u0, hs, conv_w, lru_conv_b, wr, b_r, wi, b_i, lru_lambda, "lru_core_bwd")

    small_grads = dict(
        mlp_norm=jnp.concatenate([dg_mlp0, dg_mlp1], axis=0),
        lru_conv_b=d_conv_b, lru_w_r=_diag_pairs(d_wr), lru_b_r=d_b_r, lru_w_i=_diag_pairs(d_wi),
        lru_b_i=d_b_i, lru_lambda=d_lam, fox_b_f=db_f[:, :N_HEADS],
        fox_q_gain=dq_gain[:, :HEAD_DIM], fox_k_gain=dk_gain[:, :HEAD_DIM])
    small_partial = _pack([dg_mix1] + [small_grads[n] for n in SMALL] + [d_conv_w])
    grads_lru_out, tok = _exchange_start([d_lru_out, small_partial], [False, True], tok,
                                         "grads_lru_out_start")
    dx0, dg_mix0 = _proj_bwd([[dgp, dxb]], [lru_in_g], x0, mix_norm[0:1] + tok[0, 0], dx1,
                             "lru_in_bwd")
    d_lru_in = jnp.concatenate([_matmul_tn(h0, dgp, "lru_in_dw_gate", col_blocks=4),
                                _matmul_tn(h0, dxb, "lru_in_dw_x", col_blocks=4)], axis=0)
    grads_lru_in, tok = _exchange_start([d_lru_in, dg_mix0], [False, True], tok,
                                        "grads_lru_in_start")

    grads, deltas, new_m, new_v = {}, {}, {}, {}

    def update(name, parts):
        w, m, v = w_in[name], m_in[name], v_in[name]
        shape = w.shape
        stacked = (len(parts), -1, shape[-1])
        w3 = w.reshape(stacked)
        res = _reduce_adamw([p.reshape((N_DEV,) + w3.shape[1:]) for p in parts], w3,
                            m.reshape(stacked), v.reshape(stacked), "adamw_" + name)
        return [r.reshape(shape) for r in res]

    def store(name, res):
        grads[name], deltas[name], new_m[name], new_v[name] = res

    p_w1_1, p_w2_1 = _exchange_wait(grads_mlp1, [False] * 2, tok, "grads_mlp1_wait")
    p_fox_in, p_fox_out = _exchange_wait(grads_fox, [False] * 2, p_w1_1, "grads_fox_wait")
    store("fox_w_in", update("fox_w_in", [p_fox_in]))
    store("fox_w_out", update("fox_w_out", [p_fox_out]))
    p_w1_0, p_w2_0 = _exchange_wait(grads_mlp0, [False] * 2, grads["fox_w_out"], "grads_mlp0_wait")
    store("mlp_w1", update("mlp_w1", [p_w1_0, p_w1_1]))
    store("mlp_w2", update("mlp_w2", [p_w2_0, p_w2_1]))
    p_lru_out, p_small = _exchange_wait(grads_lru_out, [False, True], grads["mlp_w2"],
                                        "grads_lru_out_wait")
    store("lru_w_out", update("lru_w_out", [p_lru_out]))
    p_lru_in, p_mix0 = _exchange_wait(grads_lru_in, [False, True], grads["lru_w_out"],
                                      "grads_lru_in_wait")
    store("lru_w_in", update("lru_w_in", [p_lru_in]))

    mix0 = [r[0] for r in _reduce_adamw([p_mix0], mix_norm[None, 0:1], m_mix_norm[None, 0:1],
                                        v_mix_norm[None, 0:1], "adamw_mix0")]
    packed = lambda src, first: _pack([first] + [src[n] for n in SMALL]
                                      + [jnp.zeros((CONV_WIDTH, D))])[None]
    small_shapes = [(1, D)] + [w_in[n].shape for n in SMALL]
    n_small = sum(math.prod(s) for s in small_shapes)
    res_small = _reduce_adamw([p_small], packed(w_in, mix_norm[1:2]), packed(m_in, m_mix_norm[1:2]),
                              packed(v_in, v_mix_norm[1:2]), "adamw_small")
    for name, *vals in zip(("mix1",) + SMALL, *[_unpack(r, small_shapes) for r in res_small]):
        if name == "mix1":
            vals = [jnp.concatenate([r0, r1], axis=0) for r0, r1 in zip(mix0, vals)]
            name = "mix_norm"
        store(name, vals)
    conv_parts = p_small.reshape(N_DEV, -1)[:, n_small:n_small + CONV_WIDTH * D]
    conv_parts = conv_parts.reshape(N_DEV, CONV_WIDTH, N_DEV, LANES)
    conv_parts = lax.dynamic_index_in_dim(conv_parts, me, axis=2, keepdims=False)
    store("lru_conv_w", update("lru_conv_w", [conv_parts]))

    loss = lax.psum(loss_local[0, 0], ("x", "y", "c"))
    return (loss, dx0[None], *[grads[n] for n in WEIGHTS], *[deltas[n] for n in WEIGHTS],
            *[new_m[n] for n in WEIGHTS], *[new_v[n] for n in WEIGHTS])
```

```python
import functools
import math

import jax
import jax.numpy as jnp
from jax import lax
from jax.experimental import pallas as pl
from jax.experimental.pallas import tpu as pltpu

F32 = jnp.float32
BF16 = jnp.bfloat16

N_DEV = 8
D_MODEL = 1024
D_FF = 4096
N_HEADS = 16
HEAD_DIM = 64
LRU_BLOCK_DIM = 64
CONV_WIDTH = 4
LRU_C = 8.0
EPS = 1e-6
NEG_INF = -1e30
ATTN_SCALE = HEAD_DIM ** -0.5
LANES = 128
N_CBLK = D_MODEL // LANES
VMEM_LIMIT = 52 * 2 ** 20

ADAM_LR = 0.001
ADAM_B1 = 0.9
ADAM_B2 = 0.999
ADAM_EPS = 1e-08
ADAM_WD = 0.01
ADAM_STEP = 10

_NT = (((1,), (1,)), ((), ()))
_TN = (((0,), (0,)), ((), ()))


def _params(*sem):
    return pltpu.CompilerParams(dimension_semantics=sem, vmem_limit_bytes=VMEM_LIMIT)


def _resident(shape):
    zeros = (0,) * len(shape)
    return pl.BlockSpec(shape, lambda *_: zeros, pipeline_mode=pl.Buffered(1))


def _dot(a, b):
    return jnp.dot(a, b, preferred_element_type=F32)


def _dot_nt(a, b):
    return lax.dot_general(a, b, _NT, preferred_element_type=F32)


def _dot_tn(a, b):
    return lax.dot_general(a, b, _TN, preferred_element_type=F32)


def _sigmoid(x):
    return 1.0 / (1.0 + jnp.exp(-x))


def _log_sigmoid(x):
    return -(jnp.maximum(-x, 0.0) + jnp.log1p(jnp.exp(-jnp.abs(x))))


def _expm1(x):
    poly = x * (1.0 + x * (0.5 + x * (1.0 / 6.0 + x * (1.0 / 24.0 + x * (1.0 / 120.0)))))
    return jnp.where(jnp.abs(x) < 0.1, poly, jnp.exp(x) - 1.0)


_GELU_K = 0.7978845608028654


def _gelu(x):
    return 0.5 * x * (1.0 + jnp.tanh(_GELU_K * (x + 0.044715 * (x * x * x))))


def _gelu_grad(x):
    t = jnp.tanh(_GELU_K * (x + 0.044715 * (x * x * x)))
    return 0.5 * (1.0 + t) + 0.5 * x * (1.0 - t * t) * (_GELU_K * (1.0 + 3 * 0.044715 * x * x))


def _rms_scale(x):
    return lax.rsqrt(jnp.mean(x * x, axis=-1, keepdims=True) + EPS)


def _norm_bwd(dh, x, g):
    rs = _rms_scale(x)
    xhat = x * rs
    dxhat = dh * g
    dx = rs * (dxhat - xhat * jnp.mean(dxhat * xhat, axis=-1, keepdims=True))
    return dx, jnp.sum(dh * xhat, axis=0, keepdims=True)


def _token_tile(S, want):
    tm = min(S, want)
    assert S % tm == 0
    return tm


def _norm_matmul(x, g, ws, name, tm=512):
    S, D = x.shape
    tm = _token_tile(S, tm)
    n = len(ws)

    def body(x_ref, g_ref, *refs):
        w_refs, o_refs, h_ref = refs[:n], refs[n:2 * n], refs[2 * n]
        xv = x_ref[...]
        h = (xv * _rms_scale(xv) * g_ref[...]).astype(BF16)
        h_ref[...] = h
        for w_ref, o_ref in zip(w_refs, o_refs):
            nb, _, nw = w_ref.shape
            for d in range(nb):
                o_ref[:, d * nw:(d + 1) * nw] = _dot(h, w_ref[d])

    widths = [w.shape[0] * w.shape[2] for w in ws]
    outs = pl.pallas_call(
        body, name=name, grid=(S // tm,),
        in_specs=[pl.BlockSpec((tm, D), lambda i: (i, 0)), _resident((1, D))]
        + [_resident(w.shape) for w in ws],
        out_specs=[pl.BlockSpec((tm, n_), lambda i: (i, 0)) for n_ in widths]
        + [pl.BlockSpec((tm, D), lambda i: (i, 0))],
        out_shape=[jax.ShapeDtypeStruct((S, n_), F32) for n_ in widths]
        + [jax.ShapeDtypeStruct((S, D), BF16)],
        compiler_params=_params("parallel"),
    )(x, g, *ws)
    return outs[:n], outs[n]


def _matmul_res(a, w, res, name, after, tm=512):
    S, K = a.shape
    N = w.shape[1]
    tm = _token_tile(S, tm)

    def body(a_ref, w_ref, r_ref, after_ref, o_ref):
        o_ref[...] = r_ref[...] + _dot(a_ref[...], w_ref[...])

    return pl.pallas_call(
        body, name=name, grid=(S // tm,),
        in_specs=[pl.BlockSpec((tm, K), lambda i: (i, 0)), _resident((K, N)),
                  pl.BlockSpec((tm, N), lambda i: (i, 0)), pl.BlockSpec(memory_space=pl.ANY)],
        out_specs=pl.BlockSpec((tm, N), lambda i: (i, 0)),
        out_shape=jax.ShapeDtypeStruct((S, N), F32),
        compiler_params=_params("parallel"),
    )(a, w, res, after)


def _matmul_nt(a, w, name, out_dtype, after, tm=1024):
    S, N = a.shape
    K = w.shape[0]
    tm = _token_tile(S, tm)

    def body(a_ref, w_ref, after_ref, o_ref):
        o_ref[...] = _dot_nt(a_ref[...].astype(BF16), w_ref[...]).astype(out_dtype)

    return pl.pallas_call(
        body, name=name, grid=(S // tm,),
        in_specs=[pl.BlockSpec((tm, N), lambda i: (i, 0)), _resident((K, N)),
                  pl.BlockSpec(memory_space=pl.ANY)],
        out_specs=pl.BlockSpec((tm, K), lambda i: (i, 0)),
        out_shape=jax.ShapeDtypeStruct((S, K), out_dtype),
        compiler_params=_params("parallel"),
    )(a, w, after)


def _proj_bwd(a_lists, w_list, x, g, res, name, tm=512):
    S, D = x.shape
    tm = _token_tile(S, tm)
    a_list = [a for group in a_lists for a in group]
    n, n_w = len(a_list), len(w_list)

    def body(*refs):
        a_refs, w_refs = list(refs[:n]), refs[n:n + n_w]
        x_ref, g_ref, r_ref, dx_ref, dg_ref = refs[n + n_w:]
        dh = jnp.zeros((tm, D), F32)
        for group, w_ref in zip(a_lists, w_refs):
            nw = w_ref.shape[2]
            d = 0
            for _ in group:
                a_ref = a_refs.pop(0)
                for j in range(a_ref.shape[1] // nw):
                    dh = dh + _dot_nt(a_ref[:, j * nw:(j + 1) * nw].astype(BF16), w_ref[d])
                    d += 1
        dx, dg = _norm_bwd(dh, x_ref[...], g_ref[...])
        dx_ref[...] = r_ref[...] + dx

        @pl.when(pl.program_id(0) == 0)
        def _():
            dg_ref[...] = jnp.zeros_like(dg_ref)
        dg_ref[...] += dg

    tok = lambda width: pl.BlockSpec((tm, width), lambda i: (i, 0))
    return pl.pallas_call(
        body, name=name, grid=(S // tm,),
        in_specs=[tok(a.shape[1]) for a in a_list] + [_resident(w.shape) for w in w_list]
        + [tok(D), _resident((1, D)), tok(D)],
        out_specs=[tok(D), pl.BlockSpec((1, D), lambda i: (0, 0))],
        out_shape=[jax.ShapeDtypeStruct((S, D), F32), jax.ShapeDtypeStruct((1, D), F32)],
        compiler_params=_params("arbitrary"),
    )(*a_list, *w_list, x, g, res)


def _matmul_tn(a, b, name, rows=1, cols=1, col_blocks=None, a_square=False, tm=1024):
    S, K = a.shape
    N = b.shape[1]
    tm = _token_tile(S, tm)
    n_tok = S // tm
    kr, nc = K // rows, N // cols

    def body(a_ref, b_ref, o_ref, acc_ref):
        av = a_ref[...]
        if a_square:
            av = av.astype(F32)
            av = av * av
        part = _dot_tn(av.astype(BF16), b_ref[...].astype(BF16))
        step = pl.program_id(2)

        @pl.when(step == 0)
        def _():
            acc_ref[...] = part

        @pl.when(step > 0)
        def _():
            acc_ref[...] += part

        @pl.when(step == n_tok - 1)
        def _():
            if col_blocks is None:
                o_ref[...] = acc_ref[...].astype(BF16)
            else:
                nw = N // col_blocks
                for d in range(col_blocks // cols):
                    o_ref[d] = acc_ref[:, d * nw:(d + 1) * nw].astype(BF16)

    if col_blocks is None:
        out_spec = pl.BlockSpec((kr, nc), lambda r, c, i: (r, c))
        out_shape = jax.ShapeDtypeStruct((K, N), BF16)
    else:
        assert rows == 1 and col_blocks % cols == 0
        per = col_blocks // cols
        out_spec = pl.BlockSpec((per, K, N // col_blocks), lambda r, c, i: (c, 0, 0))
        out_shape = jax.ShapeDtypeStruct((col_blocks, K, N // col_blocks), BF16)
    return pl.pallas_call(
        body, name=name, grid=(rows, cols, n_tok),
        in_specs=[pl.BlockSpec((tm, kr), lambda r, c, i: (i, r)),
                  pl.BlockSpec((tm, nc), lambda r, c, i: (i, c))],
        out_specs=out_spec, out_shape=out_shape,
        scratch_shapes=[pltpu.VMEM((kr, nc), F32)],
        compiler_params=_params("parallel", "parallel", "arbitrary"),
    )(a, b)


def _mlp_fwd(x, g, w1, w2, name, target=None, tm=512):
    S, D = x.shape
    nb, _, fb = w1.shape
    tm = _token_tile(S, tm)
    with_loss = target is not None

    def body(x_ref, g_ref, w1_ref, w2_ref, *refs):
        h_ref, r_ref = refs[-2:]
        xv = x_ref[...]
        h = (xv * _rms_scale(xv) * g_ref[...]).astype(BF16)
        h_ref[...] = h
        acc = xv
        for d in range(nb):
            r = jnp.maximum(_dot(h, w1_ref[d]), 0.0)
            r_ref[:, d * fb:(d + 1) * fb] = r.astype(BF16)
            acc = acc + _dot((r * r).astype(BF16), w2_ref[d])
        if not with_loss:
            refs[0][...] = acc
            return
        t_ref, loss_ref, dy_ref = refs[:3]
        err = acc - t_ref[...]
        dy_ref[...] = err / D

        @pl.when(pl.program_id(0) == 0)
        def _():
            loss_ref[...] = jnp.zeros_like(loss_ref)
        row_loss = jnp.mean(err * err, axis=1, keepdims=True)
        loss_ref[...] += 0.5 * jnp.sum(row_loss, axis=0, keepdims=True)

    tok = lambda width: pl.BlockSpec((tm, width), lambda i: (i, 0))
    saved_specs = [tok(D), tok(nb * fb)]
    saved_shapes = [jax.ShapeDtypeStruct((S, D), BF16), jax.ShapeDtypeStruct((S, nb * fb), BF16)]
    wide = jax.ShapeDtypeStruct((S, D), F32)
    if with_loss:
        head_specs = [pl.BlockSpec((1, 1), lambda i: (0, 0)), tok(D)]
        head_shapes = [jax.ShapeDtypeStruct((1, 1), F32), wide]
    else:
        head_specs, head_shapes = [tok(D)], [wide]
    return pl.pallas_call(
        body, name=name, grid=(S // tm,),
        in_specs=[tok(D), _resident((1, D)), _resident(w1.shape), _resident(w2.shape)]
        + ([tok(D)] if with_loss else []),
        out_specs=head_specs + saved_specs, out_shape=head_shapes + saved_shapes,
        compiler_params=_params("arbitrary" if with_loss else "parallel"),
    )(x, g, w1, w2, *([target] if with_loss else []))


def _mlp_bwd(dout, x, g, r, w1, w2, name, tm=512):
    S, D = x.shape
    nb, _, fb = w1.shape
    tm = _token_tile(S, tm)

    def body(do_ref, x_ref, g_ref, r_ref, w1_ref, w2_ref, dx_ref, dg_ref, da_ref):
        dov = do_ref[...]
        dob = dov.astype(BF16)
        dh = jnp.zeros((tm, D), F32)
        for d in range(nb):
            dz = _dot_nt(dob, w2_ref[d])
            da = (dz * (2.0 * r_ref[:, d * fb:(d + 1) * fb].astype(F32))).astype(BF16)
            da_ref[:, d * fb:(d + 1) * fb] = da
            dh = dh + _dot_nt(da, w1_ref[d])
        dx, dg = _norm_bwd(dh, x_ref[...], g_ref[...])
        dx_ref[...] = dov + dx

        @pl.when(pl.program_id(0) == 0)
        def _():
            dg_ref[...] = jnp.zeros_like(dg_ref)
        dg_ref[...] += dg

    tok = lambda width: pl.BlockSpec((tm, width), lambda i: (i, 0))
    return pl.pallas_call(
        body, name=name, grid=(S // tm,),
        in_specs=[tok(D), tok(D), _resident((1, D)), tok(nb * fb), _resident(w1.shape),
                  _resident(w2.shape)],
        out_specs=[tok(D), pl.BlockSpec((1, D), lambda i: (0, 0)), tok(nb * fb)],
        out_shape=[jax.ShapeDtypeStruct((S, D), F32), jax.ShapeDtypeStruct((1, D), F32),
                   jax.ShapeDtypeStruct((S, nb * fb), BF16)],
        compiler_params=_params("arbitrary"),
    )(dout, x, g, r, w1, w2)


def _scan_chunk(a, b, row, T, reverse):
    s = 1
    while s < T:
        if reverse:
            keep, shift = row < T - s, T - s
        else:
            keep, shift = row >= s, s
        a_sh = jnp.where(keep, pltpu.roll(a, shift, 0), 1.0)
        b_sh = jnp.where(keep, pltpu.roll(b, shift, 0), 0.0)
        b = a * b_sh + b
        a = a * a_sh
        s *= 2
    return a, b


def _row_of(x, row, r):
    return jnp.sum(jnp.where(row == r, x, 0.0), axis=0, keepdims=True)


def _shift_down(x, prev, row, k):
    if k == 0:
        return x
    return jnp.where(row < k, pltpu.roll(prev, k, 0), pltpu.roll(x, k, 0))


def _shift_up(x, nxt, row, k, T):
    if k == 0:
        return x
    return jnp.where(row < T - k, pltpu.roll(x, T - k, 0), pltpu.roll(nxt, T - k, 0))


def _lru_gates(xb, prev_xb, row, cw_ref, cb, wr, br, wi, bi, ls):
    xc = cb + cw_ref[pl.ds(0, 1), :] * _shift_down(xb, prev_xb, row, 3)
    for k in (2, 1, 0):
        xc = xc + cw_ref[pl.ds(3 - k, 1), :] * _shift_down(xb, prev_xb, row, k)
    xcb = xc.astype(BF16)
    r = _sigmoid(_dot(xcb, wr) + br)
    i = _sigmoid(_dot(xcb, wi) + bi)
    la = (LRU_C * r) * ls
    a = jnp.exp(la)
    m = jnp.sqrt(-_expm1(2.0 * la))
    return xc, xcb, r, i, a, m


def _lru_specs(S):
    col = lambda off: pl.BlockSpec((S, LANES), lambda j: (0, j + off))
    vec = pl.BlockSpec((1, LANES), lambda j: (0, j))
    mat = pl.BlockSpec((None, LANES, LANES), lambda j: (j, 0, 0))
    cwm = pl.BlockSpec((CONV_WIDTH, LANES), lambda j: (0, j))
    return col, vec, mat, cwm


def _lru_fwd(u, conv_w, conv_b, wr, br, wi, bi, lam, name):
    S = u.shape[0]
    T = _token_tile(S, 512)
    col, vec, mat, cwm = _lru_specs(S)

    def body(gp_ref, xb_ref, cw_ref, cb_ref, wr_ref, br_ref, wi_ref, bi_ref, lam_ref,
             y_ref, hs_ref):
        row = lax.broadcasted_iota(jnp.int32, (T, LANES), 0)
        ls = _log_sigmoid(lam_ref[...])
        cb, br, bi = cb_ref[...], br_ref[...], bi_ref[...]
        wr, wi = wr_ref[...], wi_ref[...]

        def chunk(ci, carry):
            prev_xb, hc = carry
            rows = pl.ds(pl.multiple_of(ci * T, T), T)
            xb = xb_ref[rows, :]
            xc, _, _, i, a, m = _lru_gates(xb, prev_xb, row, cw_ref, cb, wr, br, wi, bi, ls)
            ca, cbv = _scan_chunk(a, m * (i * xc), row, T, reverse=False)
            h = ca * hc + cbv
            hs_ref[rows, :] = h
            y_ref[rows, :] = (_gelu(gp_ref[rows, :]) * h).astype(BF16)
            return xb, _row_of(h, row, T - 1)

        lax.fori_loop(0, S // T, chunk,
                      (jnp.zeros((T, LANES), F32), jnp.zeros((1, LANES), F32)))

    return pl.pallas_call(
        body, name=name, grid=(N_CBLK,),
        in_specs=[col(0), col(N_CBLK), cwm, vec, mat, vec, mat, vec, vec],
        out_specs=[col(0), col(0)],
        out_shape=[jax.ShapeDtypeStruct((S, D_MODEL), BF16), jax.ShapeDtypeStruct((S, D_MODEL), F32)],
        compiler_params=_params("parallel"),
    )(u, u, conv_w, conv_b, wr, br, wi, bi, lam)


def _lru_bwd(dy, u, hs, conv_w, conv_b, wr, br, wi, bi, lam, name):
    S = u.shape[0]
    T = _token_tile(S, 512)
    n_chunk = S // T
    col, vec, mat, cwm = _lru_specs(S)

    def body(dy_ref, gp_ref, xb_ref, hs_ref, cw_ref, cb_ref, wr_ref, br_ref, wi_ref, bi_ref,
             lam_ref, dgp_ref, dxb_ref, dcw_ref, dcb_ref, dbr_ref, dbi_ref, dlam_ref, dwr_ref,
             dwi_ref):
        row = lax.broadcasted_iota(jnp.int32, (T, LANES), 0)
        lam = lam_ref[...]
        ls = _log_sigmoid(lam)
        cb, br, bi = cb_ref[...], br_ref[...], bi_ref[...]
        wr, wi = wr_ref[...], wi_ref[...]
        for ref in (dcw_ref, dcb_ref, dbr_ref, dbi_ref, dlam_ref, dwr_ref, dwi_ref):
            ref[...] = jnp.zeros_like(ref)

        def chunk(it, carry):
            g_next, dxc_next = carry
            ci = n_chunk - 1 - it
            rows = pl.ds(pl.multiple_of(ci * T, T), T)
            before = pl.ds(pl.multiple_of(jnp.maximum(ci - 1, 0) * T, T), T)
            first = ci == 0
            xb = xb_ref[rows, :]
            prev_xb = jnp.where(first, 0.0, xb_ref[before, :])
            xc, xcb, r, i, a, m = _lru_gates(xb, prev_xb, row, cw_ref, cb, wr, br, wi, bi, ls)
            h = hs_ref[rows, :]
            h_prev = _shift_down(h, jnp.where(first, 0.0, hs_ref[before, :]), row, 1)
            gp = gp_ref[rows, :]
            dyv = dy_ref[rows, :]
            dgp_ref[rows, :] = (dyv * h * _gelu_grad(gp)).astype(BF16)
            dh = dyv * _gelu(gp)
            ca, cbv = _scan_chunk(a, a * dh, row, T, reverse=True)
            gp_acc = ca * g_next + cbv
            g = dh + jnp.where(row < T - 1, pltpu.roll(gp_acc, T - 1, 0), g_next)
            da = g * h_prev - (g * (i * xc)) * a / m
            dla = da * a
            dlam_ref[...] += jnp.sum(dla * (LRU_C * r), axis=0, keepdims=True)
            dpr = (dla * (LRU_C * ls)) * r * (1.0 - r)
            dpi = (g * m * xc) * i * (1.0 - i)
            dbr_ref[...] += jnp.sum(dpr, axis=0, keepdims=True)
            dbi_ref[...] += jnp.sum(dpi, axis=0, keepdims=True)
            dprb, dpib = dpr.astype(BF16), dpi.astype(BF16)
            dwr_ref[...] += _dot_tn(xcb, dprb)
            dwi_ref[...] += _dot_tn(xcb, dpib)
            dxc = g * m * i + _dot_nt(dprb, wr) + _dot_nt(dpib, wi)
            dcb_ref[...] += jnp.sum(dxc, axis=0, keepdims=True)
            dxb = jnp.zeros((T, LANES), F32)
            for k in range(CONV_WIDTH):
                tap = pl.ds(CONV_WIDTH - 1 - k, 1)
                dcw_ref[tap, :] += jnp.sum(dxc * _shift_down(xb, prev_xb, row, k), axis=0,
                                           keepdims=True)
                dxb = dxb + cw_ref[tap, :] * _shift_up(dxc, dxc_next, row, k, T)
            dxb_ref[rows, :] = dxb.astype(BF16)
            return _row_of(gp_acc, row, 0), dxc

        lax.fori_loop(0, n_chunk, chunk,
                      (jnp.zeros((1, LANES), F32), jnp.zeros((T, LANES), F32)))
        dlam_ref[...] = dlam_ref[...] * _sigmoid(-lam)

    vec_out = jax.ShapeDtypeStruct((1, D_MODEL), F32)
    mat_out = jax.ShapeDtypeStruct((N_CBLK, LANES, LANES), F32)
    return pl.pallas_call(
        body, name=name, grid=(N_CBLK,),
        in_specs=[col(0), col(0), col(N_CBLK), col(0), cwm, vec, mat, vec, mat, vec, vec],
        out_specs=[col(0), col(0), cwm, vec, vec, vec, vec, mat, mat],
        out_shape=[jax.ShapeDtypeStruct((S, D_MODEL), BF16), jax.ShapeDtypeStruct((S, D_MODEL), BF16),
                   jax.ShapeDtypeStruct((CONV_WIDTH, D_MODEL), F32),
                   vec_out, vec_out, vec_out, vec_out, mat_out, mat_out],
        compiler_params=_params("parallel"),
    )(dy, u, u, hs, conv_w, conv_b, wr, br, wi, bi, lam)


def _head_group_matrix(value):
    r = lax.broadcasted_iota(jnp.int32, (LANES, LANES), 0) // HEAD_DIM
    c = lax.broadcasted_iota(jnp.int32, (LANES, LANES), 1) // HEAD_DIM
    return jnp.where(r == c, value, 0.0).astype(BF16)


def _group_dot(x, p):
    hi = x.astype(BF16)
    lo = (x - hi.astype(F32)).astype(BF16)
    return _dot(hi, p) + _dot(lo, p)


def _head_mean(x, p):
    return _group_dot(x, p)


def _qk_prep(u, q_gain, k_gain, name, tm=512):
    S = u.shape[0]
    tm = _token_tile(S, tm)

    def body(q_ref, k_ref, v_ref, qg_ref, kg_ref, qn_ref, kn_ref, vb_ref):
        p = _head_group_matrix(1.0 / HEAD_DIM)
        for j in range(N_CBLK):
            cl = slice(j * LANES, (j + 1) * LANES)
            for x_ref, g_ref, o_ref, scale in ((q_ref, qg_ref, qn_ref, ATTN_SCALE),
                                               (k_ref, kg_ref, kn_ref, 1.0)):
                xv = x_ref[:, cl]
                rs = lax.rsqrt(_head_mean(xv * xv, p) + EPS)
                o_ref[:, cl] = (xv * rs * g_ref[...]).astype(BF16) * scale
        vb_ref[...] = v_ref[...].astype(BF16)

    blk = lambda off: pl.BlockSpec((tm, D_MODEL), lambda i: (i, off))
    out = jax.ShapeDtypeStruct((S, D_MODEL), BF16)
    return pl.pallas_call(
        body, name=name, grid=(S // tm,),
        in_specs=[blk(0), blk(1), blk(2), _resident((1, LANES)), _resident((1, LANES))],
        out_specs=[blk(0), blk(0), blk(0)],
        out_shape=[out, out, out],
        compiler_params=_params("parallel"),
    )(u, u, u, q_gain, k_gain)


def _qk_bwd(u, dqn, dkn, q_gain, k_gain, name, tm=512):
    S = u.shape[0]
    tm = _token_tile(S, tm)

    def body(q_ref, k_ref, dqn_ref, dkn_ref, qg_ref, kg_ref, dq_ref, dk_ref, dqg_ref, dkg_ref):
        p = _head_group_matrix(1.0 / HEAD_DIM)
        for x_ref, dn_ref, g_ref, dx_ref, dg_ref, scale in (
                (q_ref, dqn_ref, qg_ref, dq_ref, dqg_ref, ATTN_SCALE),
                (k_ref, dkn_ref, kg_ref, dk_ref, dkg_ref, 1.0)):
            dg = jnp.zeros((1, LANES), F32)
            for j in range(N_CBLK):
                cl = slice(j * LANES, (j + 1) * LANES)
                xv, dn = x_ref[:, cl], dn_ref[:, cl] * scale
                rs = lax.rsqrt(_head_mean(xv * xv, p) + EPS)
                xhat = xv * rs
                dxhat = dn * g_ref[...]
                dx_ref[:, cl] = (rs * (dxhat - xhat * _head_mean(dxhat * xhat, p))).astype(BF16)
                dg = dg + jnp.sum(dn * xhat, axis=0, keepdims=True)

            @pl.when(pl.program_id(0) == 0)
            def _():
                dg_ref[...] = jnp.zeros_like(dg_ref)
            dg_ref[...] += dg

            @pl.when(pl.program_id(0) == S // tm - 1)
            def _():
                dg_ref[...] += pltpu.roll(dg_ref[...], HEAD_DIM, 1)

    blk = lambda off: pl.BlockSpec((tm, D_MODEL), lambda i: (i, off))
    acc = pl.BlockSpec((1, LANES), lambda i: (0, 0))
    out = jax.ShapeDtypeStruct((S, D_MODEL), BF16)
    vec = jax.ShapeDtypeStruct((1, LANES), F32)
    return pl.pallas_call(
        body, name=name, grid=(S // tm,),
        in_specs=[blk(0), blk(1), blk(0), blk(0), _resident((1, LANES)), _resident((1, LANES))],
        out_specs=[blk(0), blk(0), acc, acc],
        out_shape=[out, out, vec, vec],
        compiler_params=_params("arbitrary"),
    )(u, u, dqn, dkn, q_gain, k_gain)


def _forget_fwd(f, b_f, name):
    S = f.shape[0]
    T = _token_tile(S, 256)

    def body(f_ref, b_ref, c_ref):
        row = lax.broadcasted_iota(jnp.int32, (T, LANES), 0)
        ones = jnp.ones((T, LANES), F32)
        bias = b_ref[...]

        def chunk(ci, carry):
            rows = pl.ds(pl.multiple_of(ci * T, T), T)
            _, c = _scan_chunk(ones, _log_sigmoid(f_ref[rows, :] + bias), row, T, reverse=False)
            c = c + carry
            c_ref[rows, :] = c
            return _row_of(c, row, T - 1)

        lax.fori_loop(0, S // T, chunk, jnp.zeros((1, LANES), F32))

    return pl.pallas_call(
        body, name=name,
        in_specs=[pl.BlockSpec(memory_space=pltpu.VMEM)] * 2,
        out_specs=pl.BlockSpec(memory_space=pltpu.VMEM),
        out_shape=jax.ShapeDtypeStruct((S, LANES), F32),
        compiler_params=pltpu.CompilerParams(vmem_limit_bytes=VMEM_LIMIT),
    )(f, b_f)


def _forget_bwd(dc_k, rho, f, b_f, name):
    S = f.shape[0]
    T = _token_tile(S, 256)
    n_chunk = S // T

    def body(dck_ref, rho_ref, f_ref, b_ref, df_ref, db_ref):
        row = lax.broadcasted_iota(jnp.int32, (T, LANES), 0)
        ones = jnp.ones((T, LANES), F32)
        bias = b_ref[...]
        pick = (lax.broadcasted_iota(jnp.int32, (D_MODEL, LANES), 0)
                == HEAD_DIM * lax.broadcasted_iota(jnp.int32, (D_MODEL, LANES), 1))
        pick = jnp.where(pick, 1.0, 0.0).astype(BF16)

        def chunk(it, carry):
            tail, db = carry
            rows = pl.ds(pl.multiple_of((n_chunk - 1 - it) * T, T), T)
            dc = dck_ref[rows, :] + _group_dot(rho_ref[rows, :], pick)
            _, dlf = _scan_chunk(ones, dc, row, T, reverse=True)
            dlf = dlf + tail
            df = dlf * _sigmoid(-(f_ref[rows, :] + bias))
            df_ref[rows, :] = df
            return _row_of(dlf, row, 0), db + jnp.sum(df, axis=0, keepdims=True)

        zero = jnp.zeros((1, LANES), F32)
        _, db = lax.fori_loop(0, n_chunk, chunk, (zero, zero))
        db_ref[...] = db

    return pl.pallas_call(
        body, name=name,
        in_specs=[pl.BlockSpec(memory_space=pltpu.VMEM)] * 4,
        out_specs=[pl.BlockSpec(memory_space=pltpu.VMEM)] * 2,
        out_shape=[jax.ShapeDtypeStruct((S, LANES), F32), jax.ShapeDtypeStruct((1, LANES), F32)],
        compiler_params=pltpu.CompilerParams(vmem_limit_bytes=VMEM_LIMIT),
    )(dc_k, rho, f, b_f)


ATTN_TILE = 512
ATTN_ROWS_FWD = 32


def _attn_tiles(S):
    t = _token_tile(S, ATTN_TILE)
    return t, S // t


def _causal(T):
    return (lax.broadcasted_iota(jnp.int32, (T, T), 1)
            <= lax.broadcasted_iota(jnp.int32, (T, T), 0))


def _attn_fwd(qs_, kn, vb, c_row, name):
    S = qs_.shape[0]
    T, n_t = _attn_tiles(S)
    RB = min(T, ATTN_ROWS_FWD)

    def body(q_ref, k_ref, v_ref, cr_ref, o_ref, lse_ref, sa_ref, sb_ref, p_ref, m_ref, l_ref,
             acc_ref, a_ref):
        qi = pl.program_id(1)
        lanes = [slice(h2 * HEAD_DIM, (h2 + 1) * HEAD_DIM) for h2 in range(2)]
        col = lax.broadcasted_iota(jnp.int32, (RB, T), 1)
        row = lax.broadcasted_iota(jnp.int32, (RB, T), 0)
        m_ref[...] = jnp.full(m_ref.shape, NEG_INF, F32)
        l_ref[...] = jnp.zeros_like(l_ref)
        acc_ref[...] = jnp.zeros_like(acc_ref)

        def logits_into(s_ref, kj):
            ks = pl.ds(pl.multiple_of(kj * T, T), T)
            for h2, hl in enumerate(lanes):
                s_ref[h2] = _dot_nt(q_ref[:, hl], k_ref[ks, hl]) - cr_ref[h2:h2 + 1, ks]

        def consume(s_ref, kj, masked):
            ks = pl.ds(pl.multiple_of(kj * T, T), T)
            for h2, hl in enumerate(lanes):
                blocks = [slice(i * RB, (i + 1) * RB) for i in range(T // RB)]

                def logits(i, rows):
                    s = s_ref[h2, rows, :]
                    return jnp.where(col <= row + i * RB, s, NEG_INF) if masked else s

                wide = lambda x: jnp.broadcast_to(x, (RB, LANES))
                for i, rows in enumerate(blocks):
                    mx = wide(jnp.max(logits(i, rows), axis=1, keepdims=True))
                    a_ref[h2, rows, :] = m_ref[h2, rows, :]
                    m_ref[h2, rows, :] = jnp.maximum(m_ref[h2, rows, :], mx)
                for i, rows in enumerate(blocks):
                    m_new = m_ref[h2, rows, :]
                    p = jnp.exp(logits(i, rows) - jnp.tile(m_new, (1, T // LANES)))
                    alpha = jnp.exp(a_ref[h2, rows, :] - m_new)
                    a_ref[h2, rows, :] = alpha
                    l_ref[h2, rows, :] = (alpha * l_ref[h2, rows, :]
                                          + wide(jnp.sum(p, axis=1, keepdims=True)))
                    p_ref[h2, rows, :] = p.astype(BF16)
                acc_ref[h2] = (a_ref[h2, :, :HEAD_DIM] * acc_ref[h2]
                               + _dot(p_ref[h2], v_ref[ks, hl]))

        logits_into(sa_ref, 0)

        def pair(i, _):
            logits_into(sb_ref, 2 * i + 1)
            consume(sa_ref, 2 * i, False)
            logits_into(sa_ref, 2 * i + 2)
            consume(sb_ref, 2 * i + 1, False)
            return 0

        lax.fori_loop(0, qi // 2, pair, 0)

        @pl.when(qi % 2 == 1)
        def _():
            logits_into(sb_ref, qi)
            consume(sa_ref, qi - 1, False)
            consume(sb_ref, qi, True)

        @pl.when(qi % 2 == 0)
        def _():
            consume(sa_ref, qi, True)

        for h2, hl in enumerate(lanes):
            o_ref[:, hl] = (acc_ref[h2] / l_ref[h2, :, :HEAD_DIM]).astype(BF16)
            lse_ref[:, hl] = m_ref[h2, :, :HEAD_DIM] + jnp.log(l_ref[h2, :, :HEAD_DIM])

    qblk = pl.BlockSpec((T, LANES), lambda h, i: (i, h))
    kv = pl.BlockSpec((S, LANES), lambda h, i: (0, h))
    return pl.pallas_call(
        body, name=name, grid=(N_CBLK, n_t),
        in_specs=[qblk, kv, kv, pl.BlockSpec((None, 2, S), lambda h, i: (h, 0, 0))],
        out_specs=[qblk, qblk],
        out_shape=[jax.ShapeDtypeStruct((S, D_MODEL), BF16),
                   jax.ShapeDtypeStruct((S, D_MODEL), F32)],
        scratch_shapes=[pltpu.VMEM((2, T, T), F32), pltpu.VMEM((2, T, T), F32),
                        pltpu.VMEM((2, T, T), BF16),
                        pltpu.VMEM((2, T, LANES), F32), pltpu.VMEM((2, T, LANES), F32),
                        pltpu.VMEM((2, T, HEAD_DIM), F32), pltpu.VMEM((2, T, LANES), F32)],
        compiler_params=_params("parallel", "parallel"),
    )(qs_, kn, vb, c_row)


def _attn_bwd(qs_, kn, vb, do, o, lse, c_row, name):
    S = qs_.shape[0]
    T, n_t = _attn_tiles(S)

    def body(q_ref, k_ref, v_ref, do_ref, o_ref, lse_ref, cr_ref,
             dq_ref, dk_ref, dv_ref, dc_ref, rho_ref, dd_ref):
        kj = pl.program_id(1)
        causal = _causal(T)
        lanes = [slice(h2 * HEAD_DIM, (h2 + 1) * HEAD_DIM) for h2 in range(2)]
        ones = [slice(h2 * HEAD_DIM, h2 * HEAD_DIM + 1) for h2 in range(2)]

        @pl.when(kj == 0)
        def _():
            dq_ref[...] = jnp.zeros_like(dq_ref)
            rho_ref[...] = jnp.zeros_like(rho_ref)
            p_sum = _head_group_matrix(1.0)

            def fill(ci, _):
                rows = pl.ds(pl.multiple_of(ci * T, T), T)
                dd_ref[rows, :] = _group_dot(do_ref[rows, :].astype(F32) * o_ref[rows, :].astype(F32),
                                             p_sum)
                return 0

            lax.fori_loop(0, n_t, fill, 0)

        kh = [k_ref[:, hl] for hl in lanes]
        vh = [v_ref[:, hl] for hl in lanes]
        ck = [cr_ref[h2:h2 + 1, :] for h2 in range(2)]

        def step(qi, carry, masked):
            qs = pl.ds(pl.multiple_of(qi * T, T), T)
            out = []
            for h2, hl in enumerate(lanes):
                dk, dv, dc = carry[h2]
                qh, doh = q_ref[qs, hl], do_ref[qs, hl]
                s = _dot_nt(qh, kh[h2]) - ck[h2]
                if masked:
                    s = jnp.where(causal, s, NEG_INF)
                p = jnp.exp(s - lse_ref[qs, ones[h2]])
                ds = p * (_dot_nt(doh, vh[h2]) - dd_ref[qs, ones[h2]])
                dsb = ds.astype(BF16)
                dq_ref[qs, hl] += _dot(dsb, kh[h2])
                rho_ref[qs, hl] += jnp.broadcast_to(jnp.sum(ds, axis=1, keepdims=True),
                                                    (T, HEAD_DIM))
                out.append((dk + _dot_tn(dsb, qh), dv + _dot_tn(p.astype(BF16), doh),
                            dc - jnp.sum(ds, axis=0, keepdims=True)))
            return tuple(out)

        init = tuple((jnp.zeros((T, HEAD_DIM), F32), jnp.zeros((T, HEAD_DIM), F32),
                      jnp.zeros((1, T), F32)) for _ in lanes)
        carry = step(kj, init, True)
        carry = lax.fori_loop(kj + 1, n_t, lambda qi, c: step(qi, c, False), carry)
        for h2, ((dk, dv, dc), hl) in enumerate(zip(carry, lanes)):
            dk_ref[:, hl] = dk
            dv_ref[:, hl] = dv.astype(BF16)
            dc_ref[h2:h2 + 1, :] = dc

    kblk = pl.BlockSpec((T, LANES), lambda h, j: (j, h))
    full = pl.BlockSpec((S, LANES), lambda h, j: (0, h))
    crow = pl.BlockSpec((None, 2, T), lambda h, j: (h, 0, j))
    wide = jax.ShapeDtypeStruct((S, D_MODEL), F32)
    return pl.pallas_call(
        body, name=name, grid=(N_CBLK, n_t),
        in_specs=[full, kblk, kblk, full, full, full, crow],
        out_specs=[full, kblk, kblk, crow, full],
        out_shape=[wide, wide, jax.ShapeDtypeStruct((S, D_MODEL), BF16),
                   jax.ShapeDtypeStruct((N_CBLK, 2, S), F32), wide],
        scratch_shapes=[pltpu.VMEM((S, LANES), F32)],
        compiler_params=_params("parallel", "arbitrary"),
    )(qs_, kn, vb, do, o, lse, c_row)


ALL_PEERS = tuple(range(1, N_DEV))
NEAR_PEERS = (1, 2, 4, 6)
FAR_CHIPS = (2, 4, 6)


def _landing_shapes(arrays, gathers):
    return [jax.ShapeDtypeStruct((N_DEV,) + a.shape if g else a.shape, a.dtype)
            for a, g in zip(arrays, gathers)]


def _my_index():
    return 4 * lax.axis_index("x") + 2 * lax.axis_index("y") + lax.axis_index("c")


def _own_copies(srcs, lands, gathers, sems):
    me = _my_index()
    return [pltpu.make_async_copy(src if g else src.at[me], land.at[me], sems.at[a])
            for a, (src, land, g) in enumerate(zip(srcs, lands, gathers))]


def _peer_copies(srcs, lands, gathers, send_sems, recv_sems, ks=ALL_PEERS):
    x, y, c = lax.axis_index("x"), lax.axis_index("y"), lax.axis_index("c")
    me = 4 * x + 2 * y + c
    out = []
    for j, k in enumerate(ks):
        to = (1 - x if k & 4 else x, 1 - y if k & 2 else y, 1 - c if k & 1 else c)
        peer = 4 * to[0] + 2 * to[1] + to[2]
        for a, (src, land, g) in enumerate(zip(srcs, lands, gathers)):
            sem = a * len(ks) + j
            src_blk = src if g else src.at[peer]

            def copy(slot, src_blk=src_blk, land=land, sem=sem, to=to):
                return pltpu.make_async_remote_copy(
                    src_ref=src_blk, dst_ref=land.at[slot], send_sem=send_sems.at[sem],
                    recv_sem=recv_sems.at[sem], device_id=to,
                    device_id_type=pl.DeviceIdType.MESH)

            out.append((k, a, copy(me), copy(peer)))
    return out


def _forward_copies(lands, send_sems, recv_sems):
    x, y, c = lax.axis_index("x"), lax.axis_index("y"), lax.axis_index("c")
    out = []
    for j, f in enumerate(FAR_CHIPS):
        chip = 4 * (1 - x if f & 4 else x) + 2 * (1 - y if f & 2 else y)
        for a, land in enumerate(lands):
            sem = a * len(FAR_CHIPS) + j

            def copy(slot, land=land, sem=sem):
                return pltpu.make_async_remote_copy(
                    src_ref=land.at[slot], dst_ref=land.at[slot], send_sem=send_sems.at[sem],
                    recv_sem=recv_sems.at[sem], device_id=(x, y, 1 - c),
                    device_id_type=pl.DeviceIdType.MESH)

            out.append((f, a, copy(chip + c), copy(chip + 1 - c)))
    return out


def _exchange(arrays, gathers, name, two_level=False):
    n = len(arrays)
    ks = NEAR_PEERS if two_level else ALL_PEERS
    assert not two_level or all(gathers)

    def body(*refs):
        ins, outs = refs[:n], refs[n:2 * n]
        send_sems, recv_sems, own_sems, fwd_send_sems, fwd_recv_sems = refs[2 * n:]
        own = _own_copies(ins, outs, gathers, own_sems)
        for cp in own:
            cp.start()
        copies = _peer_copies(ins, outs, gathers, send_sems, recv_sems, ks)
        for _, _, send, _ in copies:
            send.start()
        passed = {}
        if two_level:
            passed = {(f, a): (send, arrival)
                      for f, a, send, arrival in _forward_copies(outs, fwd_send_sems, fwd_recv_sems)}
        for k, a, _, arrival in copies:
            arrival.wait_recv()
            if (k, a) in passed:
                passed[k, a][0].start()
        for send, arrival in passed.values():
            arrival.wait_recv()
            send.wait_send()
        for _, _, send, _ in copies:
            send.wait_send()
        for cp in own:
            cp.wait()

    hbm = pl.BlockSpec(memory_space=pl.ANY)
    return pl.pallas_call(
        body, name=name,
        in_specs=[hbm] * n, out_specs=[hbm] * n, out_shape=_landing_shapes(arrays, gathers),
        scratch_shapes=[pltpu.SemaphoreType.DMA((n * len(ks),)),
                        pltpu.SemaphoreType.DMA((n * len(ks),)),
                        pltpu.SemaphoreType.DMA((n,)),
                        pltpu.SemaphoreType.DMA((n * len(FAR_CHIPS),)),
                        pltpu.SemaphoreType.DMA((n * len(FAR_CHIPS),))],
        compiler_params=pltpu.CompilerParams(has_side_effects=True),
    )(*arrays)


_HBM = pl.BlockSpec(memory_space=pltpu.HBM)
_SEM = pl.BlockSpec(memory_space=pltpu.SEMAPHORE)
_ANY = pl.BlockSpec(memory_space=pl.ANY)
_DATAFLOW = pltpu.SideEffectType.DATAFLOW_SIDE_EFFECTING


def _in_hbm(a):
    return pltpu.with_memory_space_constraint(a, pltpu.HBM)


def _exchange_start(arrays, gathers, after, name, ks=ALL_PEERS):
    n = len(arrays)
    lands = [lax.empty(s.shape, s.dtype) for s in _landing_shapes(arrays, gathers)]

    def body(*refs):
        srcs, dsts = refs[:n], refs[n:2 * n]
        send_sems, recv_sems, own_sems = refs[2 * n + 1:2 * n + 4]
        token = refs[-1]
        for cp in _own_copies(srcs, dsts, gathers, own_sems):
            cp.start()
        for _, _, send, _ in _peer_copies(srcs, dsts, gathers, send_sems, recv_sems, ks):
            send.start()
        token[...] = jnp.zeros_like(token)

    hbm_like = [pltpu.HBM(a.shape, a.dtype) for a in list(arrays) + lands]
    res = pl.pallas_call(
        body, name=name,
        in_specs=[_HBM] * (2 * n) + [_ANY],
        out_specs=(_SEM, _SEM, _SEM, *[_HBM] * (2 * n), pl.BlockSpec(memory_space=pltpu.VMEM)),
        out_shape=(pltpu.SemaphoreType.DMA((n * len(ks),)), pltpu.SemaphoreType.DMA((n * len(ks),)),
                   pltpu.SemaphoreType.DMA((n,)), *hbm_like,
                   jax.ShapeDtypeStruct((8, LANES), F32)),
        input_output_aliases={i: 3 + i for i in range(2 * n)},
        compiler_params=pltpu.CompilerParams(has_side_effects=_DATAFLOW),
    )(*[_in_hbm(a) for a in list(arrays) + lands], after)
    return (res[0], res[1], res[2], res[3:3 + n], res[3 + n:3 + 2 * n]), res[-1]


def _exchange_wait(started, gathers, after, name, ks=ALL_PEERS):
    send_sems, recv_sems, own_sems, arrays, lands = started
    n = len(arrays)

    def body(*refs):
        srcs, dsts = refs[:n], refs[n:2 * n]
        for _, _, send, arrival in _peer_copies(srcs, dsts, gathers, refs[2 * n], refs[2 * n + 1],
                                                ks):
            arrival.wait_recv()
            send.wait_send()
        for cp in _own_copies(srcs, dsts, gathers, refs[2 * n + 2]):
            cp.wait()

    hbm_like = [pltpu.HBM(a.shape, a.dtype) for a in list(arrays) + list(lands)]
    res = pl.pallas_call(
        body, name=name,
        in_specs=[_HBM] * (2 * n) + [_SEM, _SEM, _SEM, _ANY],
        out_specs=[_HBM] * (2 * n), out_shape=hbm_like,
        input_output_aliases={i: i for i in range(2 * n)},
        compiler_params=pltpu.CompilerParams(has_side_effects=_DATAFLOW),
    )(*arrays, *lands, send_sems, recv_sems, own_sems, after)
    return res[n:]


def _forward_start(lands, after, name):
    n = len(lands)

    def body(*refs):
        send_sems, recv_sems = refs[n + 1:n + 3]
        for _, _, send, _ in _forward_copies(refs[:n], send_sems, recv_sems):
            send.start()
        refs[-1][...] = jnp.zeros_like(refs[-1])

    n_sem = n * len(FAR_CHIPS)
    res = pl.pallas_call(
        body, name=name,
        in_specs=[_HBM] * n + [_ANY],
        out_specs=(_SEM, _SEM, *[_HBM] * n, pl.BlockSpec(memory_space=pltpu.VMEM)),
        out_shape=(pltpu.SemaphoreType.DMA((n_sem,)), pltpu.SemaphoreType.DMA((n_sem,)),
                   *[pltpu.HBM(a.shape, a.dtype) for a in lands],
                   jax.ShapeDtypeStruct((8, LANES), F32)),
        input_output_aliases={i: 2 + i for i in range(n)},
        compiler_params=pltpu.CompilerParams(has_side_effects=_DATAFLOW),
    )(*[_in_hbm(a) for a in lands], after)
    return (res[0], res[1], res[2:2 + n]), res[-1]


def _forward_wait(started, after, name):
    send_sems, recv_sems, lands = started
    n = len(lands)

    def body(*refs):
        for _, _, send, arrival in _forward_copies(refs[:n], refs[n], refs[n + 1]):
            arrival.wait_recv()
            send.wait_send()

    return pl.pallas_call(
        body, name=name,
        in_specs=[_HBM] * n + [_SEM, _SEM, _ANY],
        out_specs=[_HBM] * n, out_shape=[pltpu.HBM(a.shape, a.dtype) for a in lands],
        input_output_aliases={i: i for i in range(n)},
        compiler_params=pltpu.CompilerParams(has_side_effects=_DATAFLOW),
    )(*lands, send_sems, recv_sems, after)


def _reduce_adamw(parts, w, m, v, name):
    n_layer = len(parts)
    n, R, C = parts[0].shape
    tr = 256 if R % 256 == 0 else R
    n_t = R // tr

    def body(*refs):
        p_refs = refs[:n_layer]
        w_ref, m_ref, v_ref, g_ref, d_ref, nm_ref, nv_ref = refs[n_layer:]

        def update(p_ref):
            g = p_ref[0].astype(F32)
            for s in range(1, n):
                g = g + p_ref[s].astype(F32)
            g_ref[...] = g
            m_new = ADAM_B1 * m_ref[...] + (1.0 - ADAM_B1) * g
            v_new = ADAM_B2 * v_ref[...] + (1.0 - ADAM_B2) * (g * g)
            nm_ref[...] = m_new
            nv_ref[...] = v_new
            m_hat = m_new / (1.0 - ADAM_B1 ** ADAM_STEP)
            v_hat = v_new / (1.0 - ADAM_B2 ** ADAM_STEP)
            d_ref[...] = -ADAM_LR * (m_hat / (jnp.sqrt(v_hat) + ADAM_EPS) + ADAM_WD * w_ref[...])

        for layer, p_ref in enumerate(p_refs):
            pl.when(pl.program_id(0) == layer)(functools.partial(update, p_ref))

    def parts_spec(layer):
        def index(l, i):
            return 0, jnp.where(l < layer, 0, jnp.where(l > layer, n_t - 1, i)), 0
        return pl.BlockSpec((n, tr, C), index)

    blk = pl.BlockSpec((None, tr, C), lambda l, i: (l, i, 0))
    out = jax.ShapeDtypeStruct((n_layer, R, C), F32)
    return pl.pallas_call(
        body, name=name, grid=(n_layer, n_t),
        in_specs=[parts_spec(layer) for layer in range(n_layer)] + [blk, blk, blk],
        out_specs=[blk] * 4, out_shape=[out] * 4,
        compiler_params=_params("arbitrary", "arbitrary"),
    )(*parts, w, m, v)


def _pack(arrays):
    flat = jnp.concatenate([a.reshape(-1).astype(F32) for a in arrays])
    pad = (-flat.shape[0]) % (8 * LANES)
    return jnp.pad(flat, (0, pad)).reshape(-1, LANES)


def _unpack(buf, shapes):
    flat = buf.reshape(-1)
    out, off = [], 0
    for shp in shapes:
        size = 1
        for s in shp:
            size *= s
        out.append(flat[off:off + size].reshape(shp))
        off += size
    return out


def _block_diag_pairs(w):
    w = w.reshape(N_CBLK, 2, LRU_BLOCK_DIM, LRU_BLOCK_DIM)
    z = jnp.zeros_like(w[:, 0])
    top = jnp.concatenate([w[:, 0], z], axis=2)
    bot = jnp.concatenate([z, w[:, 1]], axis=2)
    return jnp.concatenate([top, bot], axis=1)


def _diag_pairs(m):
    h = LRU_BLOCK_DIM
    return jnp.stack([m[:, :h, :h], m[:, h:, h:]], axis=1).reshape(2 * N_CBLK, h, h)


SMALL = ("mlp_norm", "lru_conv_b", "lru_w_r", "lru_b_r", "lru_w_i", "lru_b_i",
         "lru_lambda", "fox_b_f", "fox_q_gain", "fox_k_gain")
WEIGHTS = ("mix_norm", "mlp_norm", "mlp_w1", "mlp_w2", "lru_w_in", "lru_conv_w", "lru_conv_b",
           "lru_w_r", "lru_b_r", "lru_w_i", "lru_b_i", "lru_lambda", "lru_w_out", "fox_w_in",
           "fox_b_f", "fox_q_gain", "fox_k_gain", "fox_w_out")


def kernel(x, mix_norm, mlp_norm, mlp_w1, mlp_w2, lru_w_in, lru_conv_w, lru_conv_b, lru_w_r, lru_b_r, lru_w_i, lru_b_i, lru_lambda, lru_w_out, fox_w_in, fox_b_f, fox_q_gain, fox_k_gain, fox_w_out, loss_target, m_mix_norm, m_mlp_norm, m_mlp_w1, m_mlp_w2, m_lru_w_in, m_lru_conv_w, m_lru_conv_b, m_lru_w_r, m_lru_b_r, m_lru_w_i, m_lru_b_i, m_lru_lambda, m_lru_w_out, m_fox_w_in, m_fox_b_f, m_fox_q_gain, m_fox_k_gain, m_fox_w_out, v_mix_norm, v_mlp_norm, v_mlp_w1, v_mlp_w2, v_lru_w_in, v_lru_conv_w, v_lru_conv_b, v_lru_w_r, v_lru_b_r, v_lru_w_i, v_lru_b_i, v_lru_lambda, v_lru_w_out, v_fox_w_in, v_fox_b_f, v_fox_q_gain, v_fox_k_gain, v_fox_w_out):
    w_in = dict(mix_norm=mix_norm, mlp_norm=mlp_norm, mlp_w1=mlp_w1, mlp_w2=mlp_w2,
                lru_w_in=lru_w_in, lru_conv_w=lru_conv_w, lru_conv_b=lru_conv_b, lru_w_r=lru_w_r,
                lru_b_r=lru_b_r, lru_w_i=lru_w_i, lru_b_i=lru_b_i, lru_lambda=lru_lambda,
                lru_w_out=lru_w_out, fox_w_in=fox_w_in, fox_b_f=fox_b_f, fox_q_gain=fox_q_gain,
                fox_k_gain=fox_k_gain, fox_w_out=fox_w_out)
    m_in = dict(mix_norm=m_mix_norm, mlp_norm=m_mlp_norm, mlp_w1=m_mlp_w1, mlp_w2=m_mlp_w2,
                lru_w_in=m_lru_w_in, lru_conv_w=m_lru_conv_w, lru_conv_b=m_lru_conv_b,
                lru_w_r=m_lru_w_r, lru_b_r=m_lru_b_r, lru_w_i=m_lru_w_i, lru_b_i=m_lru_b_i,
                lru_lambda=m_lru_lambda, lru_w_out=m_lru_w_out, fox_w_in=m_fox_w_in,
                fox_b_f=m_fox_b_f, fox_q_gain=m_fox_q_gain, fox_k_gain=m_fox_k_gain,
                fox_w_out=m_fox_w_out)
    v_in = dict(mix_norm=v_mix_norm, mlp_norm=v_mlp_norm, mlp_w1=v_mlp_w1, mlp_w2=v_mlp_w2,
                lru_w_in=v_lru_w_in, lru_conv_w=v_lru_conv_w, lru_conv_b=v_lru_conv_b,
                lru_w_r=v_lru_w_r, lru_b_r=v_lru_b_r, lru_w_i=v_lru_w_i, lru_b_i=v_lru_b_i,
                lru_lambda=v_lru_lambda, lru_w_out=v_lru_w_out, fox_w_in=v_fox_w_in,
                fox_b_f=v_fox_b_f, fox_q_gain=v_fox_q_gain, fox_k_gain=v_fox_k_gain,
                fox_w_out=v_fox_w_out)
    D = D_MODEL
    S = x.shape[1]
    x0, target = x[0], loss_target[0]
    me = 4 * lax.axis_index("x") + 2 * lax.axis_index("y") + lax.axis_index("c")

    def bf16(a):
        return a.astype(BF16)

    (lru_in_g,) = _exchange([bf16(lru_w_in[0])], [True], "gather_lru_in", two_level=True)
    gather_lru, tok = _exchange_start([bf16(lru_w_out[0]), lru_conv_w[0]], [True] * 2, lru_in_g,
                                      "gather_lru_start")
    gather_mlp0, tok = _exchange_start([bf16(mlp_w1[0]), bf16(mlp_w2[0])], [True] * 2, tok,
                                       "gather_mlp0_start", NEAR_PEERS)
    gather_fox, tok = _exchange_start([bf16(fox_w_in[0]), bf16(fox_w_out[0])], [True] * 2, tok,
                                      "gather_fox_start")
    gather_mlp1, tok = _exchange_start([bf16(mlp_w1[1]), bf16(mlp_w2[1])], [True] * 2, tok,
                                       "gather_mlp1_start", NEAR_PEERS)

    def pass_on(started, after, name):
        lands = _exchange_wait(started, [True] * 2, after, name + "_wait", NEAR_PEERS)
        return _forward_start(lands, after, name + "_pass_start")
    wr =_block_diag_pairs(lru_w_r[0]).astype(BF16)
    wi = _block_diag_pairs(lru_w_i[0]).astype(BF16)
    b_r, b_i = lru_b_r.reshape(1, D), lru_b_i.reshape(1, D)
    q_gain, k_gain = jnp.tile(fox_q_gain, (1, 2)), jnp.tile(fox_k_gain, (1, 2))
    b_f = jnp.pad(fox_b_f, ((0, 0), (0, LANES - N_HEADS)))
    g_mix0, g_mix1 = mix_norm[0:1] + tok[0, 0], mix_norm[1:2]
    g_mlp0, g_mlp1 = mlp_norm[0:1], mlp_norm[1:2]

    (u0,), h0 = _norm_matmul(x0, g_mix0, [lru_in_g], "lru_in_proj")
    lru_out_g, conv_g = _exchange_wait(gather_lru, [True] * 2, u0, "gather_lru_wait")
    lru_out_w = lru_out_g.reshape(D, D)
    conv_w = conv_g.transpose(1, 0, 2).reshape(CONV_WIDTH, D)
    y_lru, hs =_lru_fwd(u0, conv_w, lru_conv_b, wr, b_r, wi, b_i, lru_lambda, "lru_core")
    pass_mlp0, tok = pass_on(gather_mlp0, y_lru, "gather_mlp0")
    x1 = _matmul_res(y_lru, lru_out_w, x0, "lru_out_proj", tok)
    w1g0, w2g0 = _forward_wait(pass_mlp0, x1, "gather_mlp0_pass_wait")
    x2, h1, r1 = _mlp_fwd(x1, g_mlp0, w1g0, w2g0, "mlp0")
    fox_in_g, fox_out_g = _exchange_wait(gather_fox, [True] * 2, x2, "gather_fox_wait")
    fox_out_w = fox_out_g.reshape(D, D)
    fox_full = fox_in_g.transpose(1, 0, 2).reshape(D, 3 * D + N_HEADS)
    wqkv = fox_full[:, :3 * D].reshape(D, 3, D).transpose(1, 0, 2)
    wf = jnp.pad(fox_full[:, 3 * D:], ((0, 0), (0, LANES - N_HEADS)))[None]
    (u_qkv, f), h2 = _norm_matmul(x2, g_mix1, [wqkv, wf], "fox_in_proj")
    qn, kn, vb = _qk_prep(u_qkv, q_gain, k_gain, "fox_qk_norm")
    c_col = _forget_fwd(f, b_f, "fox_forget")
    c_row = c_col[:, :N_HEADS].T.reshape(N_CBLK, 2, S)
    o, lse = _attn_fwd(qn, kn, vb, c_row, "fox_attn")
    pass_mlp1, tok = pass_on(gather_mlp1, o, "gather_mlp1")
    x3 = _matmul_res(o, fox_out_w, x2, "fox_out_proj", tok)
    w1g1, w2g1 = _forward_wait(pass_mlp1, x3, "gather_mlp1_pass_wait")
    loss_local, dx4, h3, r3 = _mlp_fwd(x3, g_mlp1, w1g1, w2g1, "mlp1", target)

    dx3, dg_mlp1, da3 = _mlp_bwd(dx4, x3, g_mlp1, r3, w1g1, w2g1, "mlp1_bwd")
    dw1_1 = _matmul_tn(h3, da3, "mlp1_dw1", cols=2, col_blocks=N_DEV)
    dw2_1 = _matmul_tn(r3, dx4, "mlp1_dw2", rows=2, a_square=True).reshape(N_DEV, -1, D)
    grads_mlp1, tok = _exchange_start([dw1_1, dw2_1], [False] * 2, tok, "grads_mlp1_start")
    do = _matmul_nt(dx3, fox_out_w, "fox_out_bwd", BF16, tok)
    d_fox_out = _matmul_tn(o, dx3, "fox_out_dw").reshape(N_DEV, -1, D)
    dqn, dkn, dv, dc_row, rho = _attn_bwd(qn, kn, vb, do, o, lse, c_row, "fox_attn_bwd")
    duq, duk, dq_gain, dk_gain = _qk_bwd(u_qkv, dqn, dkn, q_gain, k_gain, "fox_qk_norm_bwd")
    dc_k = jnp.pad(dc_row.reshape(N_HEADS, S).T, ((0, 0), (0, LANES - N_HEADS)))
    df, db_f = _forget_bwd(dc_k, rho, f, b_f, "fox_forget_bwd")
    dx2, dg_mix1 = _proj_bwd([[duq, duk, dv], [df]], [wqkv, wf], x2, g_mix1, dx3, "fox_in_bwd")
    d_fox_in = jnp.concatenate(
        [_matmul_tn(h2, duq, "fox_in_dwq"), _matmul_tn(h2, duk, "fox_in_dwk"),
         _matmul_tn(h2, dv, "fox_in_dwv"), _matmul_tn(h2, df, "fox_in_dwf")[:, :N_HEADS]], axis=1)
    d_fox_in = d_fox_in.reshape(D, N_DEV, -1).transpose(1, 0, 2)
    grads_fox, tok = _exchange_start([d_fox_in, d_fox_out], [False] * 2, tok, "grads_fox_start")
    dx1, dg_mlp0, da1 = _mlp_bwd(dx2, x1, g_mlp0 + tok[0, 0], r1, w1g0, w2g0, "mlp0_bwd")
    dw1_0 = _matmul_tn(h1, da1, "mlp0_dw1", cols=2, col_blocks=N_DEV)
    dw2_0 = _matmul_tn(r1, dx2, "mlp0_dw2", rows=2, a_square=True).reshape(N_DEV, -1, D)
    grads_mlp0, tok = _exchange_start([dw1_0, dw2_0], [False] * 2, tok, "grads_mlp0_start")
    dy_lru = _matmul_nt(dx1, lru_out_w, "lru_out_bwd", F32, tok)
    d_lru_out = _matmul_tn(y_lru, dx1, "lru_out_dw").reshape(N_DEV, -1, D)
    dgp, dxb, d_conv_w, d_conv_b, d_b_r, d_b_i, d_lam, d_wr, d_wi = _lru_bwd(
        dy_lru, u0, hs, conv_w, lru_conv_b, wr, b_r, wi, b_i, lru_lambda, "lru_core_bwd")

    small_grads = dict(
        mlp_norm=jnp.concatenate([dg_mlp0, dg_mlp1], axis=0),
        lru_conv_b=d_conv_b, lru_w_r=_diag_pairs(d_wr), lru_b_r=d_b_r, lru_w_i=_diag_pairs(d_wi),
        lru_b_i=d_b_i, lru_lambda=d_lam, fox_b_f=db_f[:, :N_HEADS],
        fox_q_gain=dq_gain[:, :HEAD_DIM], fox_k_gain=dk_gain[:, :HEAD_DIM])
    small_partial = _pack([dg_mix1] + [small_grads[n] for n in SMALL] + [d_conv_w])
    grads_lru_out, tok = _exchange_start([d_lru_out, small_partial], [False, True], tok,
                                         "grads_lru_out_start")
    dx0, dg_mix0 = _proj_bwd([[dgp, dxb]], [lru_in_g], x0, mix_norm[0:1] + tok[0, 0], dx1,
                             "lru_in_bwd")
    d_lru_in = jnp.concatenate([_matmul_tn(h0, dgp, "lru_in_dw_gate", col_blocks=4),
                                _matmul_tn(h0, dxb, "lru_in_dw_x", col_blocks=4)], axis=0)
    grads_lru_in, tok = _exchange_start([d_lru_in, dg_mix0], [False, True], tok,
                                        "grads_lru_in_start")

    grads, deltas, new_m, new_v = {}, {}, {}, {}

    def update(name, parts):
        w, m, v = w_in[name], m_in[name], v_in[name]
        shape = w.shape
        stacked = (len(parts), -1, shape[-1])
        w3 = w.reshape(stacked)
        res = _reduce_adamw([p.reshape((N_DEV,) + w3.shape[1:]) for p in parts], w3,
                            m.reshape(stacked), v.reshape(stacked), "adamw_" + name)
        return [r.reshape(shape) for r in res]

    def store(name, res):
        grads[name], deltas[name], new_m[name], new_v[name] = res

    p_w1_1, p_w2_1 = _exchange_wait(grads_mlp1, [False] * 2, tok, "grads_mlp1_wait")
    p_fox_in, p_fox_out = _exchange_wait(grads_fox, [False] * 2, p_w1_1, "grads_fox_wait")
    store("fox_w_in", update("fox_w_in", [p_fox_in]))
    store("fox_w_out", update("fox_w_out", [p_fox_out]))
    p_w1_0, p_w2_0 = _exchange_wait(grads_mlp0, [False] * 2, grads["fox_w_out"], "grads_mlp0_wait")
    store("mlp_w1", update("mlp_w1", [p_w1_0, p_w1_1]))
    store("mlp_w2", update("mlp_w2", [p_w2_0, p_w2_1]))
    p_lru_out, p_small = _exchange_wait(grads_lru_out, [False, True], grads["mlp_w2"],
                                        "grads_lru_out_wait")
    store("lru_w_out", update("lru_w_out", [p_lru_out]))
    p_lru_in, p_mix0 = _exchange_wait(grads_lru_in, [False, True], grads["lru_w_out"],
                                      "grads_lru_in_wait")
    store("lru_w_in", update("lru_w_in", [p_lru_in]))

    mix0 = [r[0] for r in _reduce_adamw([p_mix0], mix_norm[None, 0:1], m_mix_norm[None, 0:1],
                                        v_mix_norm[None, 0:1], "adamw_mix0")]
    packed = lambda src, first: _pack([first] + [src[n] for n in SMALL]
                                      + [jnp.zeros((CONV_WIDTH, D))])[None]
    small_shapes = [(1, D)] + [w_in[n].shape for n in SMALL]
    n_small = sum(math.prod(s) for s in small_shapes)
    res_small = _reduce_adamw([p_small], packed(w_in, mix_norm[1:2]), packed(m_in, m_mix_norm[1:2]),
                              packed(v_in, v_mix_norm[1:2]), "adamw_small")
    for name, *vals in zip(("mix1",) + SMALL, *[_unpack(r, small_shapes) for r in res_small]):
        if name == "mix1":
            vals = [jnp.concatenate([r0, r1], axis=0) for r0, r1 in zip(mix0, vals)]
            name = "mix_norm"
        store(name, vals)
    conv_parts = p_small.reshape(N_DEV, -1)[:, n_small:n_small + CONV_WIDTH * D]
    conv_parts = conv_parts.reshape(N_DEV, CONV_WIDTH, N_DEV, LANES)
    conv_parts = lax.dynamic_index_in_dim(conv_parts, me, axis=2, keepdims=False)
    store("lru_conv_w", update("lru_conv_w", [conv_parts]))

    loss = lax.psum(loss_local[0, 0], ("x", "y", "c"))
    return (loss, dx0[None], *[grads[n] for n in WEIGHTS], *[deltas[n] for n in WEIGHTS],
            *[new_m[n] for n in WEIGHTS], *[new_v[n] for n in WEIGHTS])
```

```python
import functools
import math

import jax
import jax.numpy as jnp
from jax import lax
from jax.experimental import pallas as pl
from jax.experimental.pallas import tpu as pltpu

F32 = jnp.float32
BF16 = jnp.bfloat16

N_DEV = 8
D_MODEL = 1024
D_FF = 4096
N_HEADS = 16
HEAD_DIM = 64
LRU_BLOCK_DIM = 64
CONV_WIDTH = 4
LRU_C = 8.0
EPS = 1e-6
NEG_INF = -1e30
ATTN_SCALE = HEAD_DIM ** -0.5
LANES = 128
N_CBLK = D_MODEL // LANES
VMEM_LIMIT = 52 * 2 ** 20

ADAM_LR = 0.001
ADAM_B1 = 0.9
ADAM_B2 = 0.999
ADAM_EPS = 1e-08
ADAM_WD = 0.01
ADAM_STEP = 10

_NT = (((1,), (1,)), ((), ()))
_TN = (((0,), (0,)), ((), ()))


def _params(*sem):
    return pltpu.CompilerParams(dimension_semantics=sem, vmem_limit_bytes=VMEM_LIMIT)


def _resident(shape):
    zeros = (0,) * len(shape)
    return pl.BlockSpec(shape, lambda *_: zeros, pipeline_mode=pl.Buffered(1))


def _dot(a, b):
    return jnp.dot(a, b, preferred_element_type=F32)


def _dot_nt(a, b):
    return lax.dot_general(a, b, _NT, preferred_element_type=F32)


def _dot_tn(a, b):
    return lax.dot_general(a, b, _TN, preferred_element_type=F32)


def _sigmoid(x):
    return 1.0 / (1.0 + jnp.exp(-x))


def _log_sigmoid(x):
    return -(jnp.maximum(-x, 0.0) + jnp.log1p(jnp.exp(-jnp.abs(x))))


def _expm1(x):
    poly = x * (1.0 + x * (0.5 + x * (1.0 / 6.0 + x * (1.0 / 24.0 + x * (1.0 / 120.0)))))
    return jnp.where(jnp.abs(x) < 0.1, poly, jnp.exp(x) - 1.0)


_GELU_K = 0.7978845608028654


def _gelu(x):
    return 0.5 * x * (1.0 + jnp.tanh(_GELU_K * (x + 0.044715 * (x * x * x))))


def _gelu_grad(x):
    t = jnp.tanh(_GELU_K * (x + 0.044715 * (x * x * x)))
    return 0.5 * (1.0 + t) + 0.5 * x * (1.0 - t * t) * (_GELU_K * (1.0 + 3 * 0.044715 * x * x))


def _rms_scale(x):
    return lax.rsqrt(jnp.mean(x * x, axis=-1, keepdims=True) + EPS)


def _norm_bwd(dh, x, g):
    rs = _rms_scale(x)
    xhat = x * rs
    dxhat = dh * g
    dx = rs * (dxhat - xhat * jnp.mean(dxhat * xhat, axis=-1, keepdims=True))
    return dx, jnp.sum(dh * xhat, axis=0, keepdims=True)


def _token_tile(S, want):
    tm = min(S, want)
    assert S % tm == 0
    return tm


def _norm_matmul(x, g, ws, name, tm=512):
    S, D = x.shape
    tm = _token_tile(S, tm)
    n = len(ws)

    def body(x_ref, g_ref, *refs):
        w_refs, o_refs, h_ref = refs[:n], refs[n:2 * n], refs[2 * n]
        xv = x_ref[...]
        h = (xv * _rms_scale(xv) * g_ref[...]).astype(BF16)
        h_ref[...] = h
        for w_ref, o_ref in zip(w_refs, o_refs):
            nb, _, nw = w_ref.shape
            for d in range(nb):
                o_ref[:, d * nw:(d + 1) * nw] = _dot(h, w_ref[d])

    widths = [w.shape[0] * w.shape[2] for w in ws]
    outs = pl.pallas_call(
        body, name=name, grid=(S // tm,),
        in_specs=[pl.BlockSpec((tm, D), lambda i: (i, 0)), _resident((1, D))]
        + [_resident(w.shape) for w in ws],
        out_specs=[pl.BlockSpec((tm, n_), lambda i: (i, 0)) for n_ in widths]
        + [pl.BlockSpec((tm, D), lambda i: (i, 0))],
        out_shape=[jax.ShapeDtypeStruct((S, n_), F32) for n_ in widths]
        + [jax.ShapeDtypeStruct((S, D), BF16)],
        compiler_params=_params("parallel"),
    )(x, g, *ws)
    return outs[:n], outs[n]


def _matmul_res(a, w, res, name, after, tm=512):
    S, K = a.shape
    N = w.shape[1]
    tm = _token_tile(S, tm)

    def body(a_ref, w_ref, r_ref, after_ref, o_ref):
        o_ref[...] = r_ref[...] + _dot(a_ref[...], w_ref[...])

    return pl.pallas_call(
        body, name=name, grid=(S // tm,),
        in_specs=[pl.BlockSpec((tm, K), lambda i: (i, 0)), _resident((K, N)),
                  pl.BlockSpec((tm, N), lambda i: (i, 0)), pl.BlockSpec(memory_space=pl.ANY)],
        out_specs=pl.BlockSpec((tm, N), lambda i: (i, 0)),
        out_shape=jax.ShapeDtypeStruct((S, N), F32),
        compiler_params=_params("parallel"),
    )(a, w, res, after)


def _matmul_nt(a, w, name, out_dtype, after, tm=1024):
    S, N = a.shape
    K = w.shape[0]
    tm = _token_tile(S, tm)

    def body(a_ref, w_ref, after_ref, o_ref):
        o_ref[...] = _dot_nt(a_ref[...].astype(BF16), w_ref[...]).astype(out_dtype)

    return pl.pallas_call(
        body, name=name, grid=(S // tm,),
        in_specs=[pl.BlockSpec((tm, N), lambda i: (i, 0)), _resident((K, N)),
                  pl.BlockSpec(memory_space=pl.ANY)],
        out_specs=pl.BlockSpec((tm, K), lambda i: (i, 0)),
        out_shape=jax.ShapeDtypeStruct((S, K), out_dtype),
        compiler_params=_params("parallel"),
    )(a, w, after)


def _proj_bwd(a_lists, w_list, x, g, res, name, tm=512):
    S, D = x.shape
    tm = _token_tile(S, tm)
    a_list = [a for group in a_lists for a in group]
    n, n_w = len(a_list), len(w_list)

    def body(*refs):
        a_refs, w_refs = list(refs[:n]), refs[n:n + n_w]
        x_ref, g_ref, r_ref, dx_ref, dg_ref = refs[n + n_w:]
        dh = jnp.zeros((tm, D), F32)
        for group, w_ref in zip(a_lists, w_refs):
            nw = w_ref.shape[2]
            d = 0
            for _ in group:
                a_ref = a_refs.pop(0)
                for j in range(a_ref.shape[1] // nw):
                    dh = dh + _dot_nt(a_ref[:, j * nw:(j + 1) * nw].astype(BF16), w_ref[d])
                    d += 1
        dx, dg = _norm_bwd(dh, x_ref[...], g_ref[...])
        dx_ref[...] = r_ref[...] + dx

        @pl.when(pl.program_id(0) == 0)
        def _():
            dg_ref[...] = jnp.zeros_like(dg_ref)
        dg_ref[...] += dg

    tok = lambda width: pl.BlockSpec((tm, width), lambda i: (i, 0))
    return pl.pallas_call(
        body, name=name, grid=(S // tm,),
        in_specs=[tok(a.shape[1]) for a in a_list] + [_resident(w.shape) for w in w_list]
        + [tok(D), _resident((1, D)), tok(D)],
        out_specs=[tok(D), pl.BlockSpec((1, D), lambda i: (0, 0))],
        out_shape=[jax.ShapeDtypeStruct((S, D), F32), jax.ShapeDtypeStruct((1, D), F32)],
        compiler_params=_params("arbitrary"),
    )(*a_list, *w_list, x, g, res)


def _matmul_tn(a, b, name, rows=1, cols=1, col_blocks=None, a_square=False, tm=1024):
    S, K = a.shape
    N = b.shape[1]
    tm = _token_tile(S, tm)
    n_tok = S // tm
    kr, nc = K // rows, N // cols

    def body(a_ref, b_ref, o_ref, acc_ref):
        av = a_ref[...]
        if a_square:
            av = av.astype(F32)
            av = av * av
        part = _dot_tn(av.astype(BF16), b_ref[...].astype(BF16))
        step = pl.program_id(2)

        @pl.when(step == 0)
        def _():
            acc_ref[...] = part

        @pl.when(step > 0)
        def _():
            acc_ref[...] += part

        @pl.when(step == n_tok - 1)
        def _():
            if col_blocks is None:
                o_ref[...] = acc_ref[...].astype(BF16)
            else:
                nw = N // col_blocks
                for d in range(col_blocks // cols):
                    o_ref[d] = acc_ref[:, d * nw:(d + 1) * nw].astype(BF16)

    if col_blocks is None:
        out_spec = pl.BlockSpec((kr, nc), lambda r, c, i: (r, c))
        out_shape = jax.ShapeDtypeStruct((K, N), BF16)
    else:
        assert rows == 1 and col_blocks % cols == 0
        per = col_blocks // cols
        out_spec = pl.BlockSpec((per, K, N // col_blocks), lambda r, c, i: (c, 0, 0))
        out_shape = jax.ShapeDtypeStruct((col_blocks, K, N // col_blocks), BF16)
    return pl.pallas_call(
        body, name=name, grid=(rows, cols, n_tok),
        in_specs=[pl.BlockSpec((tm, kr), lambda r, c, i: (i, r)),
                  pl.BlockSpec((tm, nc), lambda r, c, i: (i, c))],
        out_specs=out_spec, out_shape=out_shape,
        scratch_shapes=[pltpu.VMEM((kr, nc), F32)],
        compiler_params=_params("parallel", "parallel", "arbitrary"),
    )(a, b)


def _mlp_fwd(x, g, w1, w2, name, target=None, tm=512):
    S, D = x.shape
    nb, _, fb = w1.shape
    tm = _token_tile(S, tm)
    with_loss = target is not None

    def body(x_ref, g_ref, w1_ref, w2_ref, *refs):
        h_ref, r_ref = refs[-2:]
        xv = x_ref[...]
        h = (xv * _rms_scale(xv) * g_ref[...]).astype(BF16)
        h_ref[...] = h
        acc = xv
        for d in range(nb):
            r = jnp.maximum(_dot(h, w1_ref[d]), 0.0)
            r_ref[:, d * fb:(d + 1) * fb] = r.astype(BF16)
            acc = acc + _dot((r * r).astype(BF16), w2_ref[d])
        if not with_loss:
            refs[0][...] = acc
            return
        t_ref, loss_ref, dy_ref = refs[:3]
        err = acc - t_ref[...]
        dy_ref[...] = err / D

        @pl.when(pl.program_id(0) == 0)
        def _():
            loss_ref[...] = jnp.zeros_like(loss_ref)
        row_loss = jnp.mean(err * err, axis=1, keepdims=True)
        loss_ref[...] += 0.5 * jnp.sum(row_loss, axis=0, keepdims=True)

    tok = lambda width: pl.BlockSpec((tm, width), lambda i: (i, 0))
    saved_specs = [tok(D), tok(nb * fb)]
    saved_shapes = [jax.ShapeDtypeStruct((S, D), BF16), jax.ShapeDtypeStruct((S, nb * fb), BF16)]
    wide = jax.ShapeDtypeStruct((S, D), F32)
    if with_loss:
        head_specs = [pl.BlockSpec((1, 1), lambda i: (0, 0)), tok(D)]
        head_shapes = [jax.ShapeDtypeStruct((1, 1), F32), wide]
    else:
        head_specs, head_shapes = [tok(D)], [wide]
    return pl.pallas_call(
        body, name=name, grid=(S // tm,),
        in_specs=[tok(D), _resident((1, D)), _resident(w1.shape), _resident(w2.shape)]
        + ([tok(D)] if with_loss else []),
        out_specs=head_specs + saved_specs, out_shape=head_shapes + saved_shapes,
        compiler_params=_params("arbitrary" if with_loss else "parallel"),
    )(x, g, w1, w2, *([target] if with_loss else []))


def _mlp_bwd(dout, x, g, r, w1, w2, name, tm=512):
    S, D = x.shape
    nb, _, fb = w1.shape
    tm = _token_tile(S, tm)

    def body(do_ref, x_ref, g_ref, r_ref, w1_ref, w2_ref, dx_ref, dg_ref, da_ref):
        dov = do_ref[...]
        dob = dov.astype(BF16)
        dh = jnp.zeros((tm, D), F32)
        for d in range(nb):
            dz = _dot_nt(dob, w2_ref[d])
            da = (dz * (2.0 * r_ref[:, d * fb:(d + 1) * fb].astype(F32))).astype(BF16)
            da_ref[:, d * fb:(d + 1) * fb] = da
            dh = dh + _dot_nt(da, w1_ref[d])
        dx, dg = _norm_bwd(dh, x_ref[...], g_ref[...])
        dx_ref[...] = dov + dx

        @pl.when(pl.program_id(0) == 0)
        def _():
            dg_ref[...] = jnp.zeros_like(dg_ref)
        dg_ref[...] += dg

    tok = lambda width: pl.BlockSpec((tm, width), lambda i: (i, 0))
    return pl.pallas_call(
        body, name=name, grid=(S // tm,),
        in_specs=[tok(D), tok(D), _resident((1, D)), tok(nb * fb), _resident(w1.shape),
                  _resident(w2.shape)],
        out_specs=[tok(D), pl.BlockSpec((1, D), lambda i: (0, 0)), tok(nb * fb)],
        out_shape=[jax.ShapeDtypeStruct((S, D), F32), jax.ShapeDtypeStruct((1, D), F32),
                   jax.ShapeDtypeStruct((S, nb * fb), BF16)],
        compiler_params=_params("arbitrary"),
    )(dout, x, g, r, w1, w2)


def _scan_chunk(a, b, row, T, reverse):
    s = 1
    while s < T:
        if reverse:
            keep, shift = row < T - s, T - s
        else:
            keep, shift = row >= s, s
        a_sh = jnp.where(keep, pltpu.roll(a, shift, 0), 1.0)
        b_sh = jnp.where(keep, pltpu.roll(b, shift, 0), 0.0)
        b = a * b_sh + b
        a = a * a_sh
        s *= 2
    return a, b


def _row_of(x, row, r):
    return jnp.sum(jnp.where(row == r, x, 0.0), axis=0, keepdims=True)


def _shift_down(x, prev, row, k):
    if k == 0:
        return x
    return jnp.where(row < k, pltpu.roll(prev, k, 0), pltpu.roll(x, k, 0))


def _shift_up(x, nxt, row, k, T):
    if k == 0:
        return x
    return jnp.where(row < T - k, pltpu.roll(x, T - k, 0), pltpu.roll(nxt, T - k, 0))


def _lru_gates(xb, prev_xb, row, cw_ref, cb, wr, br, wi, bi, ls):
    xc = cb + cw_ref[pl.ds(0, 1), :] * _shift_down(xb, prev_xb, row, 3)
    for k in (2, 1, 0):
        xc = xc + cw_ref[pl.ds(3 - k, 1), :] * _shift_down(xb, prev_xb, row, k)
    xcb = xc.astype(BF16)
    r = _sigmoid(_dot(xcb, wr) + br)
    i = _sigmoid(_dot(xcb, wi) + bi)
    la = (LRU_C * r) * ls
    a = jnp.exp(la)
    m = jnp.sqrt(-_expm1(2.0 * la))
    return xc, xcb, r, i, a, m


def _lru_specs(S):
    col = lambda off: pl.BlockSpec((S, LANES), lambda j: (0, j + off))
    vec = pl.BlockSpec((1, LANES), lambda j: (0, j))
    mat = pl.BlockSpec((None, LANES, LANES), lambda j: (j, 0, 0))
    cwm = pl.BlockSpec((CONV_WIDTH, LANES), lambda j: (0, j))
    return col, vec, mat, cwm


def _lru_fwd(u, conv_w, conv_b, wr, br, wi, bi, lam, name):
    S = u.shape[0]
    T = _token_tile(S, 512)
    col, vec, mat, cwm = _lru_specs(S)

    def body(gp_ref, xb_ref, cw_ref, cb_ref, wr_ref, br_ref, wi_ref, bi_ref, lam_ref,
             y_ref, hs_ref):
        row = lax.broadcasted_iota(jnp.int32, (T, LANES), 0)
        ls = _log_sigmoid(lam_ref[...])
        cb, br, bi = cb_ref[...], br_ref[...], bi_ref[...]
        wr, wi = wr_ref[...], wi_ref[...]

        def chunk(ci, carry):
            prev_xb, hc = carry
            rows = pl.ds(pl.multiple_of(ci * T, T), T)
            xb = xb_ref[rows, :]
            xc, _, _, i, a, m = _lru_gates(xb, prev_xb, row, cw_ref, cb, wr, br, wi, bi, ls)
            ca, cbv = _scan_chunk(a, m * (i * xc), row, T, reverse=False)
            h = ca * hc + cbv
            hs_ref[rows, :] = h
            y_ref[rows, :] = (_gelu(gp_ref[rows, :]) * h).astype(BF16)
            return xb, _row_of(h, row, T - 1)

        lax.fori_loop(0, S // T, chunk,
                      (jnp.zeros((T, LANES), F32), jnp.zeros((1, LANES), F32)))

    return pl.pallas_call(
        body, name=name, grid=(N_CBLK,),
        in_specs=[col(0), col(N_CBLK), cwm, vec, mat, vec, mat, vec, vec],
        out_specs=[col(0), col(0)],
        out_shape=[jax.ShapeDtypeStruct((S, D_MODEL), BF16), jax.ShapeDtypeStruct((S, D_MODEL), F32)],
        compiler_params=_params("parallel"),
    )(u, u, conv_w, conv_b, wr, br, wi, bi, lam)


def _lru_bwd(dy, u, hs, conv_w, conv_b, wr, br, wi, bi, lam, name):
    S = u.shape[0]
    T = _token_tile(S, 512)
    n_chunk = S // T
    col, vec, mat, cwm = _lru_specs(S)

    def body(dy_ref, gp_ref, xb_ref, hs_ref, cw_ref, cb_ref, wr_ref, br_ref, wi_ref, bi_ref,
             lam_ref, dgp_ref, dxb_ref, dcw_ref, dcb_ref, dbr_ref, dbi_ref, dlam_ref, dwr_ref,
             dwi_ref):
        row = lax.broadcasted_iota(jnp.int32, (T, LANES), 0)
        lam = lam_ref[...]
        ls = _log_sigmoid(lam)
        cb, br, bi = cb_ref[...], br_ref[...], bi_ref[...]
        wr, wi = wr_ref[...], wi_ref[...]
        for ref in (dcw_ref, dcb_ref, dbr_ref, dbi_ref, dlam_ref, dwr_ref, dwi_ref):
            ref[...] = jnp.zeros_like(ref)

        def chunk(it, carry):
            g_next, dxc_next = carry
            ci = n_chunk - 1 - it
            rows = pl.ds(pl.multiple_of(ci * T, T), T)
            before = pl.ds(pl.multiple_of(jnp.maximum(ci - 1, 0) * T, T), T)
            first = ci == 0
            xb = xb_ref[rows, :]
            prev_xb = jnp.where(first, 0.0, xb_ref[before, :])
            xc, xcb, r, i, a, m = _lru_gates(xb, prev_xb, row, cw_ref, cb, wr, br, wi, bi, ls)
            h = hs_ref[rows, :]
            h_prev = _shift_down(h, jnp.where(first, 0.0, hs_ref[before, :]), row, 1)
            gp = gp_ref[rows, :]
            dyv = dy_ref[rows, :]
            dgp_ref[rows, :] = (dyv * h * _gelu_grad(gp)).astype(BF16)
            dh = dyv * _gelu(gp)
            ca, cbv = _scan_chunk(a, a * dh, row, T, reverse=True)
            gp_acc = ca * g_next + cbv
            g = dh + jnp.where(row < T - 1, pltpu.roll(gp_acc, T - 1, 0), g_next)
            da = g * h_prev - (g * (i * xc)) * a / m
            dla = da * a
            dlam_ref[...] += jnp.sum(dla * (LRU_C * r), axis=0, keepdims=True)
            dpr = (dla * (LRU_C * ls)) * r * (1.0 - r)
            dpi = (g * m * xc) * i * (1.0 - i)
            dbr_ref[...] += jnp.sum(dpr, axis=0, keepdims=True)
            dbi_ref[...] += jnp.sum(dpi, axis=0, keepdims=True)
            dprb, dpib = dpr.astype(BF16), dpi.astype(BF16)
            dwr_ref[...] += _dot_tn(xcb, dprb)
            dwi_ref[...] += _dot_tn(xcb, dpib)
            dxc = g * m * i + _dot_nt(dprb, wr) + _dot_nt(dpib, wi)
            dcb_ref[...] += jnp.sum(dxc, axis=0, keepdims=True)
            dxb = jnp.zeros((T, LANES), F32)
            for k in range(CONV_WIDTH):
                tap = pl.ds(CONV_WIDTH - 1 - k, 1)
                dcw_ref[tap, :] += jnp.sum(dxc * _shift_down(xb, prev_xb, row, k), axis=0,
                                           keepdims=True)
                dxb = dxb + cw_ref[tap, :] * _shift_up(dxc, dxc_next, row, k, T)
            dxb_ref[rows, :] = dxb.astype(BF16)
            return _row_of(gp_acc, row, 0), dxc

        lax.fori_loop(0, n_chunk, chunk,
                      (jnp.zeros((1, LANES), F32), jnp.zeros((T, LANES), F32)))
        dlam_ref[...] = dlam_ref[...] * _sigmoid(-lam)

    vec_out = jax.ShapeDtypeStruct((1, D_MODEL), F32)
    mat_out = jax.ShapeDtypeStruct((N_CBLK, LANES, LANES), F32)
    return pl.pallas_call(
        body, name=name, grid=(N_CBLK,),
        in_specs=[col(0), col(0), col(N_CBLK), col(0), cwm, vec, mat, vec, mat, vec, vec],
        out_specs=[col(0), col(0), cwm, vec, vec, vec, vec, mat, mat],
        out_shape=[jax.ShapeDtypeStruct((S, D_MODEL), BF16), jax.ShapeDtypeStruct((S, D_MODEL), BF16),
                   jax.ShapeDtypeStruct((CONV_WIDTH, D_MODEL), F32),
                   vec_out, vec_out, vec_out, vec_out, mat_out, mat_out],
        compiler_params=_params("parallel"),
    )(dy, u, u, hs, conv_w, conv_b, wr, br, wi, bi, lam)


def _head_group_matrix(value):
    r = lax.broadcasted_iota(jnp.int32, (LANES, LANES), 0) // HEAD_DIM
    c = lax.broadcasted_iota(jnp.int32, (LANES, LANES), 1) // HEAD_DIM
    return jnp.where(r == c, value, 0.0).astype(BF16)


def _group_dot(x, p):
    hi = x.astype(BF16)
    lo = (x - hi.astype(F32)).astype(BF16)
    return _dot(hi, p) + _dot(lo, p)


def _head_mean(x, p):
    return _group_dot(x, p)


def _qk_prep(u, q_gain, k_gain, name, tm=512):
    S = u.shape[0]
    tm = _token_tile(S, tm)

    def body(q_ref, k_ref, v_ref, qg_ref, kg_ref, qn_ref, kn_ref, vb_ref):
        p = _head_group_matrix(1.0 / HEAD_DIM)
        for j in range(N_CBLK):
            cl = slice(j * LANES, (j + 1) * LANES)
            for x_ref, g_ref, o_ref, scale in ((q_ref, qg_ref, qn_ref, ATTN_SCALE),
                                               (k_ref, kg_ref, kn_ref, 1.0)):
                xv = x_ref[:, cl]
                rs = lax.rsqrt(_head_mean(xv * xv, p) + EPS)
                o_ref[:, cl] = (xv * rs * g_ref[...]).astype(BF16) * scale
        vb_ref[...] = v_ref[...].astype(BF16)

    blk = lambda off: pl.BlockSpec((tm, D_MODEL), lambda i: (i, off))
    out = jax.ShapeDtypeStruct((S, D_MODEL), BF16)
    return pl.pallas_call(
        body, name=name, grid=(S // tm,),
        in_specs=[blk(0), blk(1), blk(2), _resident((1, LANES)), _resident((1, LANES))],
        out_specs=[blk(0), blk(0), blk(0)],
        out_shape=[out, out, out],
        compiler_params=_params("parallel"),
    )(u, u, u, q_gain, k_gain)


def _qk_bwd(u, dqn, dkn, q_gain, k_gain, name, tm=512):
    S = u.shape[0]
    tm = _token_tile(S, tm)

    def body(q_ref, k_ref, dqn_ref, dkn_ref, qg_ref, kg_ref, dq_ref, dk_ref, dqg_ref, dkg_ref):
        p = _head_group_matrix(1.0 / HEAD_DIM)
        for x_ref, dn_ref, g_ref, dx_ref, dg_ref, scale in (
                (q_ref, dqn_ref, qg_ref, dq_ref, dqg_ref, ATTN_SCALE),
                (k_ref, dkn_ref, kg_ref, dk_ref, dkg_ref, 1.0)):
            dg = jnp.zeros((1, LANES), F32)
            for j in range(N_CBLK):
                cl = slice(j * LANES, (j + 1) * LANES)
                xv, dn = x_ref[:, cl], dn_ref[:, cl] * scale
                rs = lax.rsqrt(_head_mean(xv * xv, p) + EPS)
                xhat = xv * rs
                dxhat = dn * g_ref[...]
                dx_ref[:, cl] = (rs * (dxhat - xhat * _head_mean(dxhat * xhat, p))).astype(BF16)
                dg = dg + jnp.sum(dn * xhat, axis=0, keepdims=True)

            @pl.when(pl.program_id(0) == 0)
            def _():
                dg_ref[...] = jnp.zeros_like(dg_ref)
            dg_ref[...] += dg

            @pl.when(pl.program_id(0) == S // tm - 1)
            def _():
                dg_ref[...] += pltpu.roll(dg_ref[...], HEAD_DIM, 1)

    blk = lambda off: pl.BlockSpec((tm, D_MODEL), lambda i: (i, off))
    acc = pl.BlockSpec((1, LANES), lambda i: (0, 0))
    out = jax.ShapeDtypeStruct((S, D_MODEL), BF16)
    vec = jax.ShapeDtypeStruct((1, LANES), F32)
    return pl.pallas_call(
        body, name=name, grid=(S // tm,),
        in_specs=[blk(0), blk(1), blk(0), blk(0), _resident((1, LANES)), _resident((1, LANES))],
        out_specs=[blk(0), blk(0), acc, acc],
        out_shape=[out, out, vec, vec],
        compiler_params=_params("arbitrary"),
    )(u, u, dqn, dkn, q_gain, k_gain)


def _forget_fwd(f, b_f, name):
    S = f.shape[0]
    T = _token_tile(S, 256)

    def body(f_ref, b_ref, c_ref):
        row = lax.broadcasted_iota(jnp.int32, (T, LANES), 0)
        ones = jnp.ones((T, LANES), F32)
        bias = b_ref[...]

        def chunk(ci, carry):
            rows = pl.ds(pl.multiple_of(ci * T, T), T)
            _, c = _scan_chunk(ones, _log_sigmoid(f_ref[rows, :] + bias), row, T, reverse=False)
            c = c + carry
            c_ref[rows, :] = c
            return _row_of(c, row, T - 1)

        lax.fori_loop(0, S // T, chunk, jnp.zeros((1, LANES), F32))

    return pl.pallas_call(
        body, name=name,
        in_specs=[pl.BlockSpec(memory_space=pltpu.VMEM)] * 2,
        out_specs=pl.BlockSpec(memory_space=pltpu.VMEM),
        out_shape=jax.ShapeDtypeStruct((S, LANES), F32),
        compiler_params=pltpu.CompilerParams(vmem_limit_bytes=VMEM_LIMIT),
    )(f, b_f)


def _forget_bwd(dc_k, rho, f, b_f, name):
    S = f.shape[0]
    T = _token_tile(S, 256)
    n_chunk = S // T

    def body(dck_ref, rho_ref, f_ref, b_ref, df_ref, db_ref):
        row = lax.broadcasted_iota(jnp.int32, (T, LANES), 0)
        ones = jnp.ones((T, LANES), F32)
        bias = b_ref[...]
        pick = (lax.broadcasted_iota(jnp.int32, (D_MODEL, LANES), 0)
                == HEAD_DIM * lax.broadcasted_iota(jnp.int32, (D_MODEL, LANES), 1))
        pick = jnp.where(pick, 1.0, 0.0).astype(BF16)

        def chunk(it, carry):
            tail, db = carry
            rows = pl.ds(pl.multiple_of((n_chunk - 1 - it) * T, T), T)
            dc = dck_ref[rows, :] + _group_dot(rho_ref[rows, :], pick)
            _, dlf = _scan_chunk(ones, dc, row, T, reverse=True)
            dlf = dlf + tail
            df = dlf * _sigmoid(-(f_ref[rows, :] + bias))
            df_ref[rows, :] = df
            return _row_of(dlf, row, 0), db + jnp.sum(df, axis=0, keepdims=True)

        zero = jnp.zeros((1, LANES), F32)
        _, db = lax.fori_loop(0, n_chunk, chunk, (zero, zero))
        db_ref[...] = db

    return pl.pallas_call(
        body, name=name,
        in_specs=[pl.BlockSpec(memory_space=pltpu.VMEM)] * 4,
        out_specs=[pl.BlockSpec(memory_space=pltpu.VMEM)] * 2,
        out_shape=[jax.ShapeDtypeStruct((S, LANES), F32), jax.ShapeDtypeStruct((1, LANES), F32)],
        compiler_params=pltpu.CompilerParams(vmem_limit_bytes=VMEM_LIMIT),
    )(dc_k, rho, f, b_f)


ATTN_TILE = 512
ATTN_ROWS_FWD = 32


def _attn_tiles(S):
    t = _token_tile(S, ATTN_TILE)
    return t, S // t


def _causal(T):
    return (lax.broadcasted_iota(jnp.int32, (T, T), 1)
            <= lax.broadcasted_iota(jnp.int32, (T, T), 0))


def _attn_fwd(qs_, kn, vb, c_row, name):
    S = qs_.shape[0]
    T, n_t = _attn_tiles(S)
    RB = min(T, ATTN_ROWS_FWD)

    def body(q_ref, k_ref, v_ref, cr_ref, o_ref, lse_ref, sa_ref, sb_ref, p_ref, m_ref, l_ref,
             acc_ref, a_ref):
        qi = pl.program_id(1)
        lanes = [slice(h2 * HEAD_DIM, (h2 + 1) * HEAD_DIM) for h2 in range(2)]
        col = lax.broadcasted_iota(jnp.int32, (RB, T), 1)
        row = lax.broadcasted_iota(jnp.int32, (RB, T), 0)
        m_ref[...] = jnp.full(m_ref.shape, NEG_INF, F32)
        l_ref[...] = jnp.zeros_like(l_ref)
        acc_ref[...] = jnp.zeros_like(acc_ref)

        def logits_into(s_ref, kj):
            ks = pl.ds(pl.multiple_of(kj * T, T), T)
            for h2, hl in enumerate(lanes):
                s_ref[h2] = _dot_nt(q_ref[:, hl], k_ref[ks, hl]) - cr_ref[h2:h2 + 1, ks]

        def consume(s_ref, kj, masked):
            ks = pl.ds(pl.multiple_of(kj * T, T), T)
            for h2, hl in enumerate(lanes):
                blocks = [slice(i * RB, (i + 1) * RB) for i in range(T // RB)]

                def logits(i, rows):
                    s = s_ref[h2, rows, :]
                    return jnp.where(col <= row + i * RB, s, NEG_INF) if masked else s

                wide = lambda x: jnp.broadcast_to(x, (RB, LANES))
                for i, rows in enumerate(blocks):
                    mx = wide(jnp.max(logits(i, rows), axis=1, keepdims=True))
                    a_ref[h2, rows, :] = m_ref[h2, rows, :]
                    m_ref[h2, rows, :] = jnp.maximum(m_ref[h2, rows, :], mx)
                for i, rows in enumerate(blocks):
                    m_new = m_ref[h2, rows, :]
                    p = jnp.exp(logits(i, rows) - jnp.tile(m_new, (1, T // LANES)))
                    alpha = jnp.exp(a_ref[h2, rows, :] - m_new)
                    a_ref[h2, rows, :] = alpha
                    l_ref[h2, rows, :] = (alpha * l_ref[h2, rows, :]
                                          + wide(jnp.sum(p, axis=1, keepdims=True)))
                    p_ref[h2, rows, :] = p.astype(BF16)
                acc_ref[h2] = (a_ref[h2, :, :HEAD_DIM] * acc_ref[h2]
                               + _dot(p_ref[h2], v_ref[ks, hl]))

        logits_into(sa_ref, 0)

        def pair(i, _):
            logits_into(sb_ref, 2 * i + 1)
            consume(sa_ref, 2 * i, False)
            logits_into(sa_ref, 2 * i + 2)
            consume(sb_ref, 2 * i + 1, False)
            return 0

        lax.fori_loop(0, qi // 2, pair, 0)

        @pl.when(qi % 2 == 1)
        def _():
            logits_into(sb_ref, qi)
            consume(sa_ref, qi - 1, False)
            consume(sb_ref, qi, True)

        @pl.when(qi % 2 == 0)
        def _():
            consume(sa_ref, qi, True)

        for h2, hl in enumerate(lanes):
            o_ref[:, hl] = (acc_ref[h2] / l_ref[h2, :, :HEAD_DIM]).astype(BF16)
            lse_ref[:, hl] = m_ref[h2, :, :HEAD_DIM] + jnp.log(l_ref[h2, :, :HEAD_DIM])

    qblk = pl.BlockSpec((T, LANES), lambda h, i: (i, h))
    kv = pl.BlockSpec((S, LANES), lambda h, i: (0, h))
    return pl.pallas_call(
        body, name=name, grid=(N_CBLK, n_t),
        in_specs=[qblk, kv, kv, pl.BlockSpec((None, 2, S), lambda h, i: (h, 0, 0))],
        out_specs=[qblk, qblk],
        out_shape=[jax.ShapeDtypeStruct((S, D_MODEL), BF16),
                   jax.ShapeDtypeStruct((S, D_MODEL), F32)],
        scratch_shapes=[pltpu.VMEM((2, T, T), F32), pltpu.VMEM((2, T, T), F32),
                        pltpu.VMEM((2, T, T), BF16),
                        pltpu.VMEM((2, T, LANES), F32), pltpu.VMEM((2, T, LANES), F32),
                        pltpu.VMEM((2, T, HEAD_DIM), F32), pltpu.VMEM((2, T, LANES), F32)],
        compiler_params=_params("parallel", "parallel"),
    )(qs_, kn, vb, c_row)


def _attn_bwd(qs_, kn, vb, do, o, lse, c_row, name):
    S = qs_.shape[0]
    T, n_t = _attn_tiles(S)

    def body(q_ref, k_ref, v_ref, do_ref, o_ref, lse_ref, cr_ref,
             dq_ref, dk_ref, dv_ref, dc_ref, rho_ref, dd_ref):
        kj = pl.program_id(1)
        causal = _causal(T)
        lanes = [slice(h2 * HEAD_DIM, (h2 + 1) * HEAD_DIM) for h2 in range(2)]
        ones = [slice(h2 * HEAD_DIM, h2 * HEAD_DIM + 1) for h2 in range(2)]

        @pl.when(kj == 0)
        def _():
            dq_ref[...] = jnp.zeros_like(dq_ref)
            rho_ref[...] = jnp.zeros_like(rho_ref)
            p_sum = _head_group_matrix(1.0)

            def fill(ci, _):
                rows = pl.ds(pl.multiple_of(ci * T, T), T)
                dd_ref[rows, :] = _group_dot(do_ref[rows, :].astype(F32) * o_ref[rows, :].astype(F32),
                                             p_sum)
                return 0

            lax.fori_loop(0, n_t, fill, 0)

        kh = [k_ref[:, hl] for hl in lanes]
        vh = [v_ref[:, hl] for hl in lanes]
        ck = [cr_ref[h2:h2 + 1, :] for h2 in range(2)]

        def step(qi, carry, masked):
            qs = pl.ds(pl.multiple_of(qi * T, T), T)
            out = []
            for h2, hl in enumerate(lanes):
                dk, dv, dc = carry[h2]
                qh, doh = q_ref[qs, hl], do_ref[qs, hl]
                s = _dot_nt(qh, kh[h2]) - ck[h2]
                if masked:
                    s = jnp.where(causal, s, NEG_INF)
                p = jnp.exp(s - lse_ref[qs, ones[h2]])
                ds = p * (_dot_nt(doh, vh[h2]) - dd_ref[qs, ones[h2]])
                dsb = ds.astype(BF16)
                dq_ref[qs, hl] += _dot(dsb, kh[h2])
                rho_ref[qs, hl] += jnp.broadcast_to(jnp.sum(ds, axis=1, keepdims=True),
                                                    (T, HEAD_DIM))
                out.append((dk + _dot_tn(dsb, qh), dv + _dot_tn(p.astype(BF16), doh),
                            dc - jnp.sum(ds, axis=0, keepdims=True)))
            return tuple(out)

        init = tuple((jnp.zeros((T, HEAD_DIM), F32), jnp.zeros((T, HEAD_DIM), F32),
                      jnp.zeros((1, T), F32)) for _ in lanes)
        carry = step(kj, init, True)
        carry = lax.fori_loop(kj + 1, n_t, lambda qi, c: step(qi, c, False), carry)
        for h2, ((dk, dv, dc), hl) in enumerate(zip(carry, lanes)):
            dk_ref[:, hl] = dk
            dv_ref[:, hl] = dv.astype(BF16)
            dc_ref[h2:h2 + 1, :] = dc

    kblk = pl.BlockSpec((T, LANES), lambda h, j: (j, h))
    full = pl.BlockSpec((S, LANES), lambda h, j: (0, h))
    crow = pl.BlockSpec((None, 2, T), lambda h, j: (h, 0, j))
    wide = jax.ShapeDtypeStruct((S, D_MODEL), F32)
    return pl.pallas_call(
        body, name=name, grid=(N_CBLK, n_t),
        in_specs=[full, kblk, kblk, full, full, full, crow],
        out_specs=[full, kblk, kblk, crow, full],
        out_shape=[wide, wide, jax.ShapeDtypeStruct((S, D_MODEL), BF16),
                   jax.ShapeDtypeStruct((N_CBLK, 2, S), F32), wide],
        scratch_shapes=[pltpu.VMEM((S, LANES), F32)],
        compiler_params=_params("parallel", "arbitrary"),
    )(qs_, kn, vb, do, o, lse, c_row)


ALL_PEERS = tuple(range(1, N_DEV))
NEAR_PEERS = (1, 2, 4, 6)
FAR_CHIPS = (2, 4, 6)


def _landing_shapes(arrays, gathers):
    return [jax.ShapeDtypeStruct((N_DEV,) + a.shape if g else a.shape, a.dtype)
            for a, g in zip(arrays, gathers)]


def _my_index():
    return 4 * lax.axis_index("x") + 2 * lax.axis_index("y") + lax.axis_index("c")


def _own_copies(srcs, lands, gathers, sems):
    me = _my_index()
    return [pltpu.make_async_copy(src if g else src.at[me], land.at[me], sems.at[a])
            for a, (src, land, g) in enumerate(zip(srcs, lands, gathers))]


def _peer_copies(srcs, lands, gathers, send_sems, recv_sems, ks=ALL_PEERS):
    x, y, c = lax.axis_index("x"), lax.axis_index("y"), lax.axis_index("c")
    me = 4 * x + 2 * y + c
    out = []
    for j, k in enumerate(ks):
        to = (1 - x if k & 4 else x, 1 - y if k & 2 else y, 1 - c if k & 1 else c)
        peer = 4 * to[0] + 2 * to[1] + to[2]
        for a, (src, land, g) in enumerate(zip(srcs, lands, gathers)):
            sem = a * len(ks) + j
            src_blk = src if g else src.at[peer]

            def copy(slot, src_blk=src_blk, land=land, sem=sem, to=to):
                return pltpu.make_async_remote_copy(
                    src_ref=src_blk, dst_ref=land.at[slot], send_sem=send_sems.at[sem],
                    recv_sem=recv_sems.at[sem], device_id=to,
                    device_id_type=pl.DeviceIdType.MESH)

            out.append((k, a, copy(me), copy(peer)))
    return out


def _forward_copies(lands, send_sems, recv_sems):
    x, y, c = lax.axis_index("x"), lax.axis_index("y"), lax.axis_index("c")
    out = []
    for j, f in enumerate(FAR_CHIPS):
        chip = 4 * (1 - x if f & 4 else x) + 2 * (1 - y if f & 2 else y)
        for a, land in enumerate(lands):
            sem = a * len(FAR_CHIPS) + j

            def copy(slot, land=land, sem=sem):
                return pltpu.make_async_remote_copy(
                    src_ref=land.at[slot], dst_ref=land.at[slot], send_sem=send_sems.at[sem],
                    recv_sem=recv_sems.at[sem], device_id=(x, y, 1 - c),
                    device_id_type=pl.DeviceIdType.MESH)

            out.append((f, a, copy(chip + c), copy(chip + 1 - c)))
    return out


def _exchange(arrays, gathers, name, two_level=False):
    n = len(arrays)
    ks = NEAR_PEERS if two_level else ALL_PEERS
    assert not two_level or all(gathers)

    def body(*refs):
        ins, outs = refs[:n], refs[n:2 * n]
        send_sems, recv_sems, own_sems, fwd_send_sems, fwd_recv_sems = refs[2 * n:]
        own = _own_copies(ins, outs, gathers, own_sems)
        for cp in own:
            cp.start()
        copies = _peer_copies(ins, outs, gathers, send_sems, recv_sems, ks)
        for _, _, send, _ in copies:
            send.start()
        passed = {}
        if two_level:
            passed = {(f, a): (send, arrival)
                      for f, a, send, arrival in _forward_copies(outs, fwd_send_sems, fwd_recv_sems)}
        for k, a, _, arrival in copies:
            arrival.wait_recv()
            if (k, a) in passed:
                passed[k, a][0].start()
        for send, arrival in passed.values():
            arrival.wait_recv()
            send.wait_send()
        for _, _, send, _ in copies:
            send.wait_send()
        for cp in own:
            cp.wait()

    hbm = pl.BlockSpec(memory_space=pl.ANY)
    return pl.pallas_call(
        body, name=name,
        in_specs=[hbm] * n, out_specs=[hbm] * n, out_shape=_landing_shapes(arrays, gathers),
        scratch_shapes=[pltpu.SemaphoreType.DMA((n * len(ks),)),
                        pltpu.SemaphoreType.DMA((n * len(ks),)),
                        pltpu.SemaphoreType.DMA((n,)),
                        pltpu.SemaphoreType.DMA((n * len(FAR_CHIPS),)),
                        pltpu.SemaphoreType.DMA((n * len(FAR_CHIPS),))],
        compiler_params=pltpu.CompilerParams(has_side_effects=True),
    )(*arrays)


_HBM = pl.BlockSpec(memory_space=pltpu.HBM)
_SEM = pl.BlockSpec(memory_space=pltpu.SEMAPHORE)
_ANY = pl.BlockSpec(memory_space=pl.ANY)
_DATAFLOW = pltpu.SideEffectType.DATAFLOW_SIDE_EFFECTING


def _in_hbm(a):
    return pltpu.with_memory_space_constraint(a, pltpu.HBM)


def _exchange_start(arrays, gathers, after, name, ks=ALL_PEERS):
    n = len(arrays)
    lands = [lax.empty(s.shape, s.dtype) for s in _landing_shapes(arrays, gathers)]

    def body(*refs):
        srcs, dsts = refs[:n], refs[n:2 * n]
        send_sems, recv_sems, own_sems = refs[2 * n + 1:2 * n + 4]
        token = refs[-1]
        for cp in _own_copies(srcs, dsts, gathers, own_sems):
            cp.start()
        for _, _, send, _ in _peer_copies(srcs, dsts, gathers, send_sems, recv_sems, ks):
            send.start()
        token[...] = jnp.zeros_like(token)

    hbm_like = [pltpu.HBM(a.shape, a.dtype) for a in list(arrays) + lands]
    res = pl.pallas_call(
        body, name=name,
        in_specs=[_HBM] * (2 * n) + [_ANY],
        out_specs=(_SEM, _SEM, _SEM, *[_HBM] * (2 * n), pl.BlockSpec(memory_space=pltpu.VMEM)),
        out_shape=(pltpu.SemaphoreType.DMA((n * len(ks),)), pltpu.SemaphoreType.DMA((n * len(ks),)),
                   pltpu.SemaphoreType.DMA((n,)), *hbm_like,
                   jax.ShapeDtypeStruct((8, LANES), F32)),
        input_output_aliases={i: 3 + i for i in range(2 * n)},
        compiler_params=pltpu.CompilerParams(has_side_effects=_DATAFLOW),
    )(*[_in_hbm(a) for a in list(arrays) + lands], after)
    return (res[0], res[1], res[2], res[3:3 + n], res[3 + n:3 + 2 * n]), res[-1]


def _exchange_wait(started, gathers, after, name, ks=ALL_PEERS):
    send_sems, recv_sems, own_sems, arrays, lands = started
    n = len(arrays)

    def body(*refs):
        srcs, dsts = refs[:n], refs[n:2 * n]
        for _, _, send, arrival in _peer_copies(srcs, dsts, gathers, refs[2 * n], refs[2 * n + 1],
                                                ks):
            arrival.wait_recv()
            send.wait_send()
        for cp in _own_copies(srcs, dsts, gathers, refs[2 * n + 2]):
            cp.wait()

    hbm_like = [pltpu.HBM(a.shape, a.dtype) for a in list(arrays) + list(lands)]
    res = pl.pallas_call(
        body, name=name,
        in_specs=[_HBM] * (2 * n) + [_SEM, _SEM, _SEM, _ANY],
        out_specs=[_HBM] * (2 * n), out_shape=hbm_like,
        input_output_aliases={i: i for i in range(2 * n)},
        compiler_params=pltpu.CompilerParams(has_side_effects=_DATAFLOW),
    )(*arrays, *lands, send_sems, recv_sems, own_sems, after)
    return res[n:]


def _forward_start(lands, after, name):
    n = len(lands)

    def body(*refs):
        send_sems, recv_sems = refs[n + 1:n + 3]
        for _, _, send, _ in _forward_copies(refs[:n], send_sems, recv_sems):
            send.start()
        refs[-1][...] = jnp.zeros_like(refs[-1])

    n_sem = n * len(FAR_CHIPS)
    res = pl.pallas_call(
        body, name=name,
        in_specs=[_HBM] * n + [_ANY],
        out_specs=(_SEM, _SEM, *[_HBM] * n, pl.BlockSpec(memory_space=pltpu.VMEM)),
        out_shape=(pltpu.SemaphoreType.DMA((n_sem,)), pltpu.SemaphoreType.DMA((n_sem,)),
                   *[pltpu.HBM(a.shape, a.dtype) for a in lands],
                   jax.ShapeDtypeStruct((8, LANES), F32)),
        input_output_aliases={i: 2 + i for i in range(n)},
        compiler_params=pltpu.CompilerParams(has_side_effects=_DATAFLOW),
    )(*[_in_hbm(a) for a in lands], after)
    return (res[0], res[1], res[2:2 + n]), res[-1]


def _forward_wait(started, after, name):
    send_sems, recv_sems, lands = started
    n = len(lands)

    def body(*refs):
        for _, _, send, arrival in _forward_copies(refs[:n], refs[n], refs[n + 1]):
            arrival.wait_recv()
            send.wait_send()

    return pl.pallas_call(
        body, name=name,
        in_specs=[_HBM] * n + [_SEM, _SEM, _ANY],
        out_specs=[_HBM] * n, out_shape=[pltpu.HBM(a.shape, a.dtype) for a in lands],
        input_output_aliases={i: i for i in range(n)},
        compiler_params=pltpu.CompilerParams(has_side_effects=_DATAFLOW),
    )(*lands, send_sems, recv_sems, after)


def _reduce_adamw(parts, w, m, v, name):
    n_layer = len(parts)
    n, R, C = parts[0].shape
    tr = 256 if R % 256 == 0 else R
    n_t = R // tr

    def body(*refs):
        p_refs = refs[:n_layer]
        w_ref, m_ref, v_ref, g_ref, d_ref, nm_ref, nv_ref = refs[n_layer:]

        def update(p_ref):
            g = p_ref[0].astype(F32)
            for s in range(1, n):
                g = g + p_ref[s].astype(F32)
            g_ref[...] = g
            m_new = ADAM_B1 * m_ref[...] + (1.0 - ADAM_B1) * g
            v_new = ADAM_B2 * v_ref[...] + (1.0 - ADAM_B2) * (g * g)
            nm_ref[...] = m_new
            nv_ref[...] = v_new
            m_hat = m_new / (1.0 - ADAM_B1 ** ADAM_STEP)
            v_hat = v_new / (1.0 - ADAM_B2 ** ADAM_STEP)
            d_ref[...] = -ADAM_LR * (m_hat / (jnp.sqrt(v_hat) + ADAM_EPS) + ADAM_WD * w_ref[...])

        for layer, p_ref in enumerate(p_refs):
            pl.when(pl.program_id(0) == layer)(functools.partial(update, p_ref))

    def parts_spec(layer):
        def index(l, i):
            return 0, jnp.where(l < layer, 0, jnp.where(l > layer, n_t - 1, i)), 0
        return pl.BlockSpec((n, tr, C), index)

    blk = pl.BlockSpec((None, tr, C), lambda l, i: (l, i, 0))
    out = jax.ShapeDtypeStruct((n_layer, R, C), F32)
    return pl.pallas_call(
        body, name=name, grid=(n_layer, n_t),
        in_specs=[parts_spec(layer) for layer in range(n_layer)] + [blk, blk, blk],
        out_specs=[blk] * 4, out_shape=[out] * 4,
        compiler_params=_params("arbitrary", "arbitrary"),
    )(*parts, w, m, v)


def _pack(arrays):
    flat = jnp.concatenate([a.reshape(-1).astype(F32) for a in arrays])
    pad = (-flat.shape[0]) % (8 * LANES)
    return jnp.pad(flat, (0, pad)).reshape(-1, LANES)


def _unpack(buf, shapes):
    flat = buf.reshape(-1)
    out, off = [], 0
    for shp in shapes:
        size = 1
        for s in shp:
            size *= s
        out.append(flat[off:off + size].reshape(shp))
        off += size
    return out


def _block_diag_pairs(w):
    w = w.reshape(N_CBLK, 2, LRU_BLOCK_DIM, LRU_BLOCK_DIM)
    z = jnp.zeros_like(w[:, 0])
    top = jnp.concatenate([w[:, 0], z], axis=2)
    bot = jnp.concatenate([z, w[:, 1]], axis=2)
    return jnp.concatenate([top, bot], axis=1)


def _diag_pairs(m):
    h = LRU_BLOCK_DIM
    return jnp.stack([m[:, :h, :h], m[:, h:, h:]], axis=1).reshape(2 * N_CBLK, h, h)


SMALL = ("mlp_norm", "lru_conv_b", "lru_w_r", "lru_b_r", "lru_w_i", "lru_b_i",
         "lru_lambda", "fox_b_f", "fox_q_gain", "fox_k_gain")
WEIGHTS = ("mix_norm", "mlp_norm", "mlp_w1", "mlp_w2", "lru_w_in", "lru_conv_w", "lru_conv_b",
           "lru_w_r", "lru_b_r", "lru_w_i", "lru_b_i", "lru_lambda", "lru_w_out", "fox_w_in",
           "fox_b_f", "fox_q_gain", "fox_k_gain", "fox_w_out")


def kernel(x, mix_norm, mlp_norm, mlp_w1, mlp_w2, lru_w_in, lru_conv_w, lru_conv_b, lru_w_r, lru_b_r, lru_w_i, lru_b_i, lru_lambda, lru_w_out, fox_w_in, fox_b_f, fox_q_gain, fox_k_gain, fox_w_out, loss_target, m_mix_norm, m_mlp_norm, m_mlp_w1, m_mlp_w2, m_lru_w_in, m_lru_conv_w, m_lru_conv_b, m_lru_w_r, m_lru_b_r, m_lru_w_i, m_lru_b_i, m_lru_lambda, m_lru_w_out, m_fox_w_in, m_fox_b_f, m_fox_q_gain, m_fox_k_gain, m_fox_w_out, v_mix_norm, v_mlp_norm, v_mlp_w1, v_mlp_w2, v_lru_w_in, v_lru_conv_w, v_lru_conv_b, v_lru_w_r, v_lru_b_r, v_lru_w_i, v_lru_b_i, v_lru_lambda, v_lru_w_out, v_fox_w_in, v_fox_b_f, v_fox_q_gain, v_fox_k_gain, v_fox_w_out):
    w_in = dict(mix_norm=mix_norm, mlp_norm=mlp_norm, mlp_w1=mlp_w1, mlp_w2=mlp_w2,
                lru_w_in=lru_w_in, lru_conv_w=lru_conv_w, lru_conv_b=lru_conv_b, lru_w_r=lru_w_r,
                lru_b_r=lru_b_r, lru_w_i=lru_w_i, lru_b_i=lru_b_i, lru_lambda=lru_lambda,
                lru_w_out=lru_w_out, fox_w_in=fox_w_in, fox_b_f=fox_b_f, fox_q_gain=fox_q_gain,
                fox_k_gain=fox_k_gain, fox_w_out=fox_w_out)
    m_in = dict(mix_norm=m_mix_norm, mlp_norm=m_mlp_norm, mlp_w1=m_mlp_w1, mlp_w2=m_mlp_w2,
                lru_w_in=m_lru_w_in, lru_conv_w=m_lru_conv_w, lru_conv_b=m_lru_conv_b,
                lru_w_r=m_lru_w_r, lru_b_r=m_lru_b_r, lru_w_i=m_lru_w_i, lru_b_i=m_lru_b_i,
                lru_lambda=m_lru_lambda, lru_w_out=m_lru_w_out, fox_w_in=m_fox_w_in,
                fox_b_f=m_fox_b_f, fox_q_gain=m_fox_q_gain, fox_k_gain=m_fox_k_gain,
                fox_w_out=m_fox_w_out)
    v_in = dict(mix_norm=v_mix_norm, mlp_norm=v_mlp_norm, mlp_w1=v_mlp_w1, mlp_w2=v_mlp_w2,
                lru_w_in=v_lru_w_in, lru_conv_w=v_lru_conv_w, lru_conv_b=v_lru_conv_b,
                lru_w_r=v_lru_w_r, lru_b_r=v_lru_b_r, lru_w_i=v_lru_w_i, lru_b_i=v_lru_b_i,
                lru_lambda=v_lru_lambda, lru_w_out=v_lru_w_out, fox_w_in=v_fox_w_in,
                fox_b_f=v_fox_b_f, fox_q_gain=v_fox_q_gain, fox_k_gain=v_fox_k_gain,
                fox_w_out=v_fox_w_out)
    D = D_MODEL
    S = x.shape[1]
    x0, target = x[0], loss_target[0]
    me = 4 * lax.axis_index("x") + 2 * lax.axis_index("y") + lax.axis_index("c")

    def bf16(a):
        return a.astype(BF16)

    (lru_in_g,) = _exchange([bf16(lru_w_in[0])], [True], "gather_lru_in", two_level=True)
    gather_lru, tok = _exchange_start([bf16(lru_w_out[0]), lru_conv_w[0]], [True] * 2, lru_in_g,
                                      "gather_lru_start")
    gather_mlp0, tok = _exchange_start([bf16(mlp_w1[0]), bf16(mlp_w2[0])], [True] * 2, tok,
                                       "gather_mlp0_start", NEAR_PEERS)
    gather_fox, tok = _exchange_start([bf16(fox_w_in[0]), bf16(fox_w_out[0])], [True] * 2, tok,
                                      "gather_fox_start")
    gather_mlp1, tok = _exchange_start([bf16(mlp_w1[1]), bf16(mlp_w2[1])], [True] * 2, tok,
                                       "gather_mlp1_start", NEAR_PEERS)

    def pass_on(started, after, name):
        lands = _exchange_wait(started, [True] * 2, after, name + "_wait", NEAR_PEERS)
        return _forward_start(lands, after, name + "_pass_start")
    wr =_block_diag_pairs(lru_w_r[0]).astype(BF16)
    wi = _block_diag_pairs(lru_w_i[0]).astype(BF16)
    b_r, b_i = lru_b_r.reshape(1, D), lru_b_i.reshape(1, D)
    q_gain, k_gain = jnp.tile(fox_q_gain, (1, 2)), jnp.tile(fox_k_gain, (1, 2))
    b_f = jnp.pad(fox_b_f, ((0, 0), (0, LANES - N_HEADS)))
    g_mix0, g_mix1 = mix_norm[0:1] + tok[0, 0], mix_norm[1:2]
    g_mlp0, g_mlp1 = mlp_norm[0:1], mlp_norm[1:2]

    (u0,), h0 = _norm_matmul(x0, g_mix0, [lru_in_g], "lru_in_proj")
    lru_out_g, conv_g = _exchange_wait(gather_lru, [True] * 2, u0, "gather_lru_wait")
    lru_out_w = lru_out_g.reshape(D, D)
    conv_w = conv_g.transpose(1, 0, 2).reshape(CONV_WIDTH, D)
    y_lru, hs =_lru_fwd(u0, conv_w, lru_conv_b, wr, b_r, wi, b_i, lru_lambda, "lru_core")
    pass_mlp0, tok = pass_on(gather_mlp0, y_lru, "gather_mlp0")
    x1 = _matmul_res(y_lru, lru_out_w, x0, "lru_out_proj", tok)
    w1g0, w2g0 = _forward_wait(pass_mlp0, x1, "gather_mlp0_pass_wait")
    x2, h1, r1 = _mlp_fwd(x1, g_mlp0, w1g0, w2g0, "mlp0")
    fox_in_g, fox_out_g = _exchange_wait(gather_fox, [True] * 2, x2, "gather_fox_wait")
    fox_out_w = fox_out_g.reshape(D, D)
    fox_full = jnp.concatenate([fox_in_g[d] for d in range(N_DEV)], axis=1)
    wqkv = fox_full[:, :3 * D].reshape(D, 3, D).transpose(1, 0, 2)
    wf = jnp.pad(fox_full[:, 3 * D:], ((0, 0), (0, LANES - N_HEADS)))[None]
    (u_qkv, f), h2 = _norm_matmul(x2, g_mix1, [wqkv, wf], "fox_in_proj")
    qn, kn, vb = _qk_prep(u_qkv, q_gain, k_gain, "fox_qk_norm")
    c_col = _forget_fwd(f, b_f, "fox_forget")
    c_row = c_col[:, :N_HEADS].T.reshape(N_CBLK, 2, S)
    o, lse = _attn_fwd(qn, kn, vb, c_row, "fox_attn")
    pass_mlp1, tok = pass_on(gather_mlp1, o, "gather_mlp1")
    x3 = _matmul_res(o, fox_out_w, x2, "fox_out_proj", tok)
    w1g1, w2g1 = _forward_wait(pass_mlp1, x3, "gather_mlp1_pass_wait")
    loss_local, dx4, h3, r3 = _mlp_fwd(x3, g_mlp1, w1g1, w2g1, "mlp1", target)

    dx3, dg_mlp1, da3 = _mlp_bwd(dx4, x3, g_mlp1, r3, w1g1, w2g1, "mlp1_bwd")
    dw1_1 = _matmul_tn(h3, da3, "mlp1_dw1", cols=2, col_blocks=N_DEV)
    dw2_1 = _matmul_tn(r3, dx4, "mlp1_dw2", rows=2, a_square=True).reshape(N_DEV, -1, D)
    grads_mlp1, tok = _exchange_start([dw1_1, dw2_1], [False] * 2, tok, "grads_mlp1_start")
    do = _matmul_nt(dx3, fox_out_w, "fox_out_bwd", BF16, tok)
    d_fox_out = _matmul_tn(o, dx3, "fox_out_dw").reshape(N_DEV, -1, D)
    dqn, dkn, dv, dc_row, rho = _attn_bwd(qn, kn, vb, do, o, lse, c_row, "fox_attn_bwd")
    duq, duk, dq_gain, dk_gain = _qk_bwd(u_qkv, dqn, dkn, q_gain, k_gain, "fox_qk_norm_bwd")
    dc_k = jnp.pad(dc_row.reshape(N_HEADS, S).T, ((0, 0), (0, LANES - N_HEADS)))
    df, db_f = _forget_bwd(dc_k, rho, f, b_f, "fox_forget_bwd")
    dx2, dg_mix1 = _proj_bwd([[duq, duk, dv], [df]], [wqkv, wf], x2, g_mix1, dx3, "fox_in_bwd")
    d_fox_in = jnp.concatenate(
        [_matmul_tn(h2, duq, "fox_in_dwq"), _matmul_tn(h2, duk, "fox_in_dwk"),
         _matmul_tn(h2, dv, "fox_in_dwv"), _matmul_tn(h2, df, "fox_in_dwf")[:, :N_HEADS]], axis=1)
    shard = (3 * D + N_HEADS) // N_DEV
    d_fox_in = jnp.stack([d_fox_in[:, d * shard:(d + 1) * shard] for d in range(N_DEV)])
    grads_fox, tok = _exchange_start([d_fox_in, d_fox_out], [False] * 2, tok, "grads_fox_start")
    dx1, dg_mlp0, da1 = _mlp_bwd(dx2, x1, g_mlp0 + tok[0, 0], r1, w1g0, w2g0, "mlp0_bwd")
    dw1_0 = _matmul_tn(h1, da1, "mlp0_dw1", cols=2, col_blocks=N_DEV)
    dw2_0 = _matmul_tn(r1, dx2, "mlp0_dw2", rows=2, a_square=True).reshape(N_DEV, -1, D)
    grads_mlp0, tok = _exchange_start([dw1_0, dw2_0], [False] * 2, tok, "grads_mlp0_start")
    dy_lru = _matmul_nt(dx1, lru_out_w, "lru_out_bwd", F32, tok)
    d_lru_out = _matmul_tn(y_lru, dx1, "lru_out_dw").reshape(N_DEV, -1, D)
    dgp, dxb, d_conv_w, d_conv_b, d_b_r, d_b_i, d_lam, d_wr, d_wi = _lru_bwd(
        dy_lru, u0, hs, conv_w, lru_conv_b, wr, b_r, wi, b_i, lru_lambda, "lru_core_bwd")

    small_grads = dict(
        mlp_norm=jnp.concatenate([dg_mlp0, dg_mlp1], axis=0),
        lru_conv_b=d_conv_b, lru_w_r=_diag_pairs(d_wr), lru_b_r=d_b_r, lru_w_i=_diag_pairs(d_wi),
        lru_b_i=d_b_i, lru_lambda=d_lam, fox_b_f=db_f[:, :N_HEADS],
        fox_q_gain=dq_gain[:, :HEAD_DIM], fox_k_gain=dk_gain[:, :HEAD_DIM])
    small_partial = _pack([dg_mix1] + [small_grads[n] for n in SMALL] + [d_conv_w])
    grads_lru_out, tok = _exchange_start([d_lru_out, small_partial], [False, True], tok,
                                         "grads_lru_out_start")
    dx0, dg_mix0 = _proj_bwd([[dgp, dxb]], [lru_in_g], x0, mix_norm[0:1] + tok[0, 0], dx1,
                             "lru_in_bwd")
    d_lru_in = jnp.concatenate([_matmul_tn(h0, dgp, "lru_in_dw_gate", col_blocks=4),
                                _matmul_tn(h0, dxb, "lru_in_dw_x", col_blocks=4)], axis=0)
    grads_lru_in, tok = _exchange_start([d_lru_in, dg_mix0], [False, True], tok,
                                        "grads_lru_in_start")

    grads, deltas, new_m, new_v = {}, {}, {}, {}

    def update(name, parts):
        w, m, v = w_in[name], m_in[name], v_in[name]
        shape = w.shape
        stacked = (len(parts), -1, shape[-1])
        w3 = w.reshape(stacked)
        res = _reduce_adamw([p.reshape((N_DEV,) + w3.shape[1:]) for p in parts], w3,
                            m.reshape(stacked), v.reshape(stacked), "adamw_" + name)
        return [r.reshape(shape) for r in res]

    def store(name, res):
        grads[name], deltas[name], new_m[name], new_v[name] = res

    p_w1_1, p_w2_1 = _exchange_wait(grads_mlp1, [False] * 2, tok, "grads_mlp1_wait")
    p_fox_in, p_fox_out = _exchange_wait(grads_fox, [False] * 2, p_w1_1, "grads_fox_wait")
    store("fox_w_in", update("fox_w_in", [p_fox_in]))
    store("fox_w_out", update("fox_w_out", [p_fox_out]))
    p_w1_0, p_w2_0 = _exchange_wait(grads_mlp0, [False] * 2, grads["fox_w_out"], "grads_mlp0_wait")
    store("mlp_w1", update("mlp_w1", [p_w1_0, p_w1_1]))
    store("mlp_w2", update("mlp_w2", [p_w2_0, p_w2_1]))
    p_lru_out, p_small = _exchange_wait(grads_lru_out, [False, True], grads["mlp_w2"],
                                        "grads_lru_out_wait")
    store("lru_w_out", update("lru_w_out", [p_lru_out]))
    p_lru_in, p_mix0 = _exchange_wait(grads_lru_in, [False, True], grads["lru_w_out"],
                                      "grads_lru_in_wait")
    store("lru_w_in", update("lru_w_in", [p_lru_in]))

    mix0 = [r[0] for r in _reduce_adamw([p_mix0], mix_norm[None, 0:1], m_mix_norm[None, 0:1],
                                        v_mix_norm[None, 0:1], "adamw_mix0")]
    packed = lambda src, first: _pack([first] + [src[n] for n in SMALL]
                                      + [jnp.zeros((CONV_WIDTH, D))])[None]
    small_shapes = [(1, D)] + [w_in[n].shape for n in SMALL]
    n_small = sum(math.prod(s) for s in small_shapes)
    res_small = _reduce_adamw([p_small], packed(w_in, mix_norm[1:2]), packed(m_in, m_mix_norm[1:2]),
                              packed(v_in, v_mix_norm[1:2]), "adamw_small")
    for name, *vals in zip(("mix1",) + SMALL, *[_unpack(r, small_shapes) for r in res_small]):
        if name == "mix1":
            vals = [jnp.concatenate([r0, r1], axis=0) for r0, r1 in zip(mix0, vals)]
            name = "mix_norm"
        store(name, vals)
    conv_parts = p_small.reshape(N_DEV, -1)[:, n_small:n_small + CONV_WIDTH * D]
    conv_parts = conv_parts.reshape(N_DEV, CONV_WIDTH, N_DEV, LANES)
    conv_parts = lax.dynamic_index_in_dim(conv_parts, me, axis=2, keepdims=False)
    store("lru_conv_w", update("lru_conv_w", [conv_parts]))

    loss = lax.psum(loss_local[0, 0], ("x", "y", "c"))
    return (loss, dx0[None], *[grads[n] for n in WEIGHTS], *[deltas[n] for n in WEIGHTS],
            *[new_m[n] for n in WEIGHTS], *[new_v[n] for n in WEIGHTS])
```

```python
import functools
import math

import jax
import jax.numpy as jnp
from jax import lax
from jax.experimental import pallas as pl
from jax.experimental.pallas import tpu as pltpu

F32 = jnp.float32
BF16 = jnp.bfloat16

N_DEV = 8
D_MODEL = 1024
D_FF = 4096
N_HEADS = 16
HEAD_DIM = 64
LRU_BLOCK_DIM = 64
CONV_WIDTH = 4
LRU_C = 8.0
EPS = 1e-6
NEG_INF = -1e30
ATTN_SCALE = HEAD_DIM ** -0.5
LANES = 128
N_CBLK = D_MODEL // LANES
VMEM_LIMIT = 52 * 2 ** 20

ADAM_LR = 0.001
ADAM_B1 = 0.9
ADAM_B2 = 0.999
ADAM_EPS = 1e-08
ADAM_WD = 0.01
ADAM_STEP = 10

_NT = (((1,), (1,)), ((), ()))
_TN = (((0,), (0,)), ((), ()))


def _params(*sem):
    return pltpu.CompilerParams(dimension_semantics=sem, vmem_limit_bytes=VMEM_LIMIT)


def _resident(shape):
    zeros = (0,) * len(shape)
    return pl.BlockSpec(shape, lambda *_: zeros, pipeline_mode=pl.Buffered(1))


def _dot(a, b):
    return jnp.dot(a, b, preferred_element_type=F32)


def _dot_nt(a, b):
    return lax.dot_general(a, b, _NT, preferred_element_type=F32)


def _dot_tn(a, b):
    return lax.dot_general(a, b, _TN, preferred_element_type=F32)


def _sigmoid(x):
    return 1.0 / (1.0 + jnp.exp(-x))


def _log_sigmoid(x):
    return -(jnp.maximum(-x, 0.0) + jnp.log1p(jnp.exp(-jnp.abs(x))))


def _expm1(x):
    poly = x * (1.0 + x * (0.5 + x * (1.0 / 6.0 + x * (1.0 / 24.0 + x * (1.0 / 120.0)))))
    return jnp.where(jnp.abs(x) < 0.1, poly, jnp.exp(x) - 1.0)


_GELU_K = 0.7978845608028654


def _gelu_parts(x):
    x2 = x * x
    t = jnp.tanh(_GELU_K * (x + 0.044715 * (x2 * x)))
    half = 0.5 * (1.0 + t)
    return x * half, half + 0.5 * x * (1.0 - t * t) * (_GELU_K * (1.0 + 3 * 0.044715 * x2))


def _gelu(x):
    return _gelu_parts(x)[0]


def _rms_scale(x):
    return lax.rsqrt(jnp.mean(x * x, axis=-1, keepdims=True) + EPS)


def _norm_bwd(dh, x, g):
    rs = _rms_scale(x)
    xhat = x * rs
    dxhat = dh * g
    dx = rs * (dxhat - xhat * jnp.mean(dxhat * xhat, axis=-1, keepdims=True))
    return dx, jnp.sum(dh * xhat, axis=0, keepdims=True)


def _token_tile(S, want):
    tm = min(S, want)
    assert S % tm == 0
    return tm


def _norm_matmul(x, g, ws, name, tm=512):
    S, D = x.shape
    tm = _token_tile(S, tm)
    n = len(ws)

    def body(x_ref, g_ref, *refs):
        w_refs, o_refs, h_ref = refs[:n], refs[n:2 * n], refs[2 * n]
        xv = x_ref[...]
        h = (xv * _rms_scale(xv) * g_ref[...]).astype(BF16)
        h_ref[...] = h
        for w_ref, o_ref in zip(w_refs, o_refs):
            nb, _, nw = w_ref.shape
            for d in range(nb):
                o_ref[:, d * nw:(d + 1) * nw] = _dot(h, w_ref[d])

    widths = [w.shape[0] * w.shape[2] for w in ws]
    outs = pl.pallas_call(
        body, name=name, grid=(S // tm,),
        in_specs=[pl.BlockSpec((tm, D), lambda i: (i, 0)), _resident((1, D))]
        + [_resident(w.shape) for w in ws],
        out_specs=[pl.BlockSpec((tm, n_), lambda i: (i, 0)) for n_ in widths]
        + [pl.BlockSpec((tm, D), lambda i: (i, 0))],
        out_shape=[jax.ShapeDtypeStruct((S, n_), F32) for n_ in widths]
        + [jax.ShapeDtypeStruct((S, D), BF16)],
        compiler_params=_params("parallel"),
    )(x, g, *ws)
    return outs[:n], outs[n]


def _matmul_res(a, w, res, name, after, tm=512):
    S, K = a.shape
    N = w.shape[1]
    tm = _token_tile(S, tm)

    def body(a_ref, w_ref, r_ref, after_ref, o_ref):
        o_ref[...] = r_ref[...] + _dot(a_ref[...], w_ref[...])

    return pl.pallas_call(
        body, name=name, grid=(S // tm,),
        in_specs=[pl.BlockSpec((tm, K), lambda i: (i, 0)), _resident((K, N)),
                  pl.BlockSpec((tm, N), lambda i: (i, 0)), pl.BlockSpec(memory_space=pl.ANY)],
        out_specs=pl.BlockSpec((tm, N), lambda i: (i, 0)),
        out_shape=jax.ShapeDtypeStruct((S, N), F32),
        compiler_params=_params("parallel"),
    )(a, w, res, after)


def _matmul_nt(a, w, name, out_dtype, after, tm=1024):
    S, N = a.shape
    K = w.shape[0]
    tm = _token_tile(S, tm)

    def body(a_ref, w_ref, after_ref, o_ref):
        o_ref[...] = _dot_nt(a_ref[...].astype(BF16), w_ref[...]).astype(out_dtype)

    return pl.pallas_call(
        body, name=name, grid=(S // tm,),
        in_specs=[pl.BlockSpec((tm, N), lambda i: (i, 0)), _resident((K, N)),
                  pl.BlockSpec(memory_space=pl.ANY)],
        out_specs=pl.BlockSpec((tm, K), lambda i: (i, 0)),
        out_shape=jax.ShapeDtypeStruct((S, K), out_dtype),
        compiler_params=_params("parallel"),
    )(a, w, after)


def _proj_bwd(a_lists, w_list, x, g, res, name, tm=512):
    S, D = x.shape
    tm = _token_tile(S, tm)
    a_list = [a for group in a_lists for a in group]
    n, n_w = len(a_list), len(w_list)

    def body(*refs):
        a_refs, w_refs = list(refs[:n]), refs[n:n + n_w]
        x_ref, g_ref, r_ref, dx_ref, dg_ref = refs[n + n_w:]
        dh = jnp.zeros((tm, D), F32)
        for group, w_ref in zip(a_lists, w_refs):
            nw = w_ref.shape[2]
            d = 0
            for _ in group:
                a_ref = a_refs.pop(0)
                for j in range(a_ref.shape[1] // nw):
                    dh = dh + _dot_nt(a_ref[:, j * nw:(j + 1) * nw].astype(BF16), w_ref[d])
                    d += 1
        dx, dg = _norm_bwd(dh, x_ref[...], g_ref[...])
        dx_ref[...] = r_ref[...] + dx

        @pl.when(pl.program_id(0) == 0)
        def _():
            dg_ref[...] = jnp.zeros_like(dg_ref)
        dg_ref[...] += dg

    tok = lambda width: pl.BlockSpec((tm, width), lambda i: (i, 0))
    return pl.pallas_call(
        body, name=name, grid=(S // tm,),
        in_specs=[tok(a.shape[1]) for a in a_list] + [_resident(w.shape) for w in w_list]
        + [tok(D), _resident((1, D)), tok(D)],
        out_specs=[tok(D), pl.BlockSpec((1, D), lambda i: (0, 0))],
        out_shape=[jax.ShapeDtypeStruct((S, D), F32), jax.ShapeDtypeStruct((1, D), F32)],
        compiler_params=_params("arbitrary"),
    )(*a_list, *w_list, x, g, res)


def _matmul_tn(a, b, name, rows=1, cols=1, col_blocks=None, a_square=False, tm=1024):
    S, K = a.shape
    N = b.shape[1]
    tm = _token_tile(S, tm)
    n_tok = S // tm
    kr, nc = K // rows, N // cols

    def body(a_ref, b_ref, o_ref, acc_ref):
        av = a_ref[...]
        if a_square:
            av = av.astype(F32)
            av = av * av
        part = _dot_tn(av.astype(BF16), b_ref[...].astype(BF16))
        step = pl.program_id(2)

        @pl.when(step == 0)
        def _():
            acc_ref[...] = part

        @pl.when(step > 0)
        def _():
            acc_ref[...] += part

        @pl.when(step == n_tok - 1)
        def _():
            if col_blocks is None:
                o_ref[...] = acc_ref[...].astype(BF16)
            else:
                nw = N // col_blocks
                for d in range(col_blocks // cols):
                    o_ref[d] = acc_ref[:, d * nw:(d + 1) * nw].astype(BF16)

    if col_blocks is None:
        out_spec = pl.BlockSpec((kr, nc), lambda r, c, i: (r, c))
        out_shape = jax.ShapeDtypeStruct((K, N), BF16)
    else:
        assert rows == 1 and col_blocks % cols == 0
        per = col_blocks // cols
        out_spec = pl.BlockSpec((per, K, N // col_blocks), lambda r, c, i: (c, 0, 0))
        out_shape = jax.ShapeDtypeStruct((col_blocks, K, N // col_blocks), BF16)
    return pl.pallas_call(
        body, name=name, grid=(rows, cols, n_tok),
        in_specs=[pl.BlockSpec((tm, kr), lambda r, c, i: (i, r)),
                  pl.BlockSpec((tm, nc), lambda r, c, i: (i, c))],
        out_specs=out_spec, out_shape=out_shape,
        scratch_shapes=[pltpu.VMEM((kr, nc), F32)],
        compiler_params=_params("parallel", "parallel", "arbitrary"),
    )(a, b)


def _mlp_fwd(x, g, w1, w2, name, target=None, tm=512):
    S, D = x.shape
    nb, _, fb = w1.shape
    tm = _token_tile(S, tm)
    with_loss = target is not None

    def body(x_ref, g_ref, w1_ref, w2_ref, *refs):
        h_ref, r_ref = refs[-2:]
        xv = x_ref[...]
        h = (xv * _rms_scale(xv) * g_ref[...]).astype(BF16)
        h_ref[...] = h
        acc = xv
        for d in range(nb):
            r = jnp.maximum(_dot(h, w1_ref[d]), 0.0)
            r_ref[:, d * fb:(d + 1) * fb] = r.astype(BF16)
            acc = acc + _dot((r * r).astype(BF16), w2_ref[d])
        if not with_loss:
            refs[0][...] = acc
            return
        t_ref, loss_ref, dy_ref = refs[:3]
        err = acc - t_ref[...]
        dy_ref[...] = err / D

        @pl.when(pl.program_id(0) == 0)
        def _():
            loss_ref[...] = jnp.zeros_like(loss_ref)
        row_loss = jnp.mean(err * err, axis=1, keepdims=True)
        loss_ref[...] += 0.5 * jnp.sum(row_loss, axis=0, keepdims=True)

    tok = lambda width: pl.BlockSpec((tm, width), lambda i: (i, 0))
    saved_specs = [tok(D), tok(nb * fb)]
    saved_shapes = [jax.ShapeDtypeStruct((S, D), BF16), jax.ShapeDtypeStruct((S, nb * fb), BF16)]
    wide = jax.ShapeDtypeStruct((S, D), F32)
    if with_loss:
        head_specs = [pl.BlockSpec((1, 1), lambda i: (0, 0)), tok(D)]
        head_shapes = [jax.ShapeDtypeStruct((1, 1), F32), wide]
    else:
        head_specs, head_shapes = [tok(D)], [wide]
    return pl.pallas_call(
        body, name=name, grid=(S // tm,),
        in_specs=[tok(D), _resident((1, D)), _resident(w1.shape), _resident(w2.shape)]
        + ([tok(D)] if with_loss else []),
        out_specs=head_specs + saved_specs, out_shape=head_shapes + saved_shapes,
        compiler_params=_params("arbitrary" if with_loss else "parallel"),
    )(x, g, w1, w2, *([target] if with_loss else []))


def _mlp_bwd(dout, x, g, r, w1, w2, name, tm=512):
    S, D = x.shape
    nb, _, fb = w1.shape
    tm = _token_tile(S, tm)

    def body(do_ref, x_ref, g_ref, r_ref, w1_ref, w2_ref, dx_ref, dg_ref, da_ref):
        dov = do_ref[...]
        dob = dov.astype(BF16)
        dh = jnp.zeros((tm, D), F32)
        for d in range(nb):
            dz = _dot_nt(dob, w2_ref[d])
            da = (dz * (2.0 * r_ref[:, d * fb:(d + 1) * fb].astype(F32))).astype(BF16)
            da_ref[:, d * fb:(d + 1) * fb] = da
            dh = dh + _dot_nt(da, w1_ref[d])
        dx, dg = _norm_bwd(dh, x_ref[...], g_ref[...])
        dx_ref[...] = dov + dx

        @pl.when(pl.program_id(0) == 0)
        def _():
            dg_ref[...] = jnp.zeros_like(dg_ref)
        dg_ref[...] += dg

    tok = lambda width: pl.BlockSpec((tm, width), lambda i: (i, 0))
    return pl.pallas_call(
        body, name=name, grid=(S // tm,),
        in_specs=[tok(D), tok(D), _resident((1, D)), tok(nb * fb), _resident(w1.shape),
                  _resident(w2.shape)],
        out_specs=[tok(D), pl.BlockSpec((1, D), lambda i: (0, 0)), tok(nb * fb)],
        out_shape=[jax.ShapeDtypeStruct((S, D), F32), jax.ShapeDtypeStruct((1, D), F32),
                   jax.ShapeDtypeStruct((S, nb * fb), BF16)],
        compiler_params=_params("arbitrary"),
    )(dout, x, g, r, w1, w2)


def _scan_chunk(a, b, row, T, reverse):
    s = 1
    while s < T:
        if reverse:
            keep, shift = row < T - s, T - s
        else:
            keep, shift = row >= s, s
        a_sh = jnp.where(keep, pltpu.roll(a, shift, 0), 1.0)
        b_sh = jnp.where(keep, pltpu.roll(b, shift, 0), 0.0)
        b = a * b_sh + b
        a = a * a_sh
        s *= 2
    return a, b


def _row_of(x, row, r):
    return jnp.sum(jnp.where(row == r, x, 0.0), axis=0, keepdims=True)


def _shift_down(x, prev, row, k):
    if k == 0:
        return x
    return jnp.where(row < k, pltpu.roll(prev, k, 0), pltpu.roll(x, k, 0))


def _shift_up(x, nxt, row, k, T):
    if k == 0:
        return x
    return jnp.where(row < T - k, pltpu.roll(x, T - k, 0), pltpu.roll(nxt, T - k, 0))


def _lru_gates(xb, prev_xb, row, cw_ref, cb, wr, br, wi, bi, ls):
    xc = cb + cw_ref[pl.ds(0, 1), :] * _shift_down(xb, prev_xb, row, 3)
    for k in (2, 1, 0):
        xc = xc + cw_ref[pl.ds(3 - k, 1), :] * _shift_down(xb, prev_xb, row, k)
    xcb = xc.astype(BF16)
    r = _sigmoid(_dot(xcb, wr) + br)
    i = _sigmoid(_dot(xcb, wi) + bi)
    la = (LRU_C * r) * ls
    a = jnp.exp(la)
    m = jnp.sqrt(-_expm1(2.0 * la))
    return xc, xcb, r, i, a, m


def _lru_specs(S):
    col = lambda off: pl.BlockSpec((S, LANES), lambda j: (0, j + off))
    vec = pl.BlockSpec((1, LANES), lambda j: (0, j))
    mat = pl.BlockSpec((None, LANES, LANES), lambda j: (j, 0, 0))
    cwm = pl.BlockSpec((CONV_WIDTH, LANES), lambda j: (0, j))
    return col, vec, mat, cwm


def _lru_fwd(u, conv_w, conv_b, wr, br, wi, bi, lam, name):
    S = u.shape[0]
    T = _token_tile(S, 512)
    col, vec, mat, cwm = _lru_specs(S)

    def body(gp_ref, xb_ref, cw_ref, cb_ref, wr_ref, br_ref, wi_ref, bi_ref, lam_ref,
             y_ref, hs_ref):
        row = lax.broadcasted_iota(jnp.int32, (T, LANES), 0)
        ls = _log_sigmoid(lam_ref[...])
        cb, br, bi = cb_ref[...], br_ref[...], bi_ref[...]
        wr, wi = wr_ref[...], wi_ref[...]

        def chunk(ci, carry):
            prev_xb, hc = carry
            rows = pl.ds(pl.multiple_of(ci * T, T), T)
            xb = xb_ref[rows, :]
            xc, _, _, i, a, m = _lru_gates(xb, prev_xb, row, cw_ref, cb, wr, br, wi, bi, ls)
            ca, cbv = _scan_chunk(a, m * (i * xc), row, T, reverse=False)
            h = ca * hc + cbv
            hs_ref[rows, :] = h
            y_ref[rows, :] = (_gelu(gp_ref[rows, :]) * h).astype(BF16)
            return xb, _row_of(h, row, T - 1)

        lax.fori_loop(0, S // T, chunk,
                      (jnp.zeros((T, LANES), F32), jnp.zeros((1, LANES), F32)))

    return pl.pallas_call(
        body, name=name, grid=(N_CBLK,),
        in_specs=[col(0), col(N_CBLK), cwm, vec, mat, vec, mat, vec, vec],
        out_specs=[col(0), col(0)],
        out_shape=[jax.ShapeDtypeStruct((S, D_MODEL), BF16), jax.ShapeDtypeStruct((S, D_MODEL), F32)],
        compiler_params=_params("parallel"),
    )(u, u, conv_w, conv_b, wr, br, wi, bi, lam)


def _lru_bwd(dy, u, hs, conv_w, conv_b, wr, br, wi, bi, lam, name):
    S = u.shape[0]
    T = _token_tile(S, 512)
    n_chunk = S // T
    col, vec, mat, cwm = _lru_specs(S)

    def body(dy_ref, gp_ref, xb_ref, hs_ref, cw_ref, cb_ref, wr_ref, br_ref, wi_ref, bi_ref,
             lam_ref, dgp_ref, dxb_ref, dcw_ref, dcb_ref, dbr_ref, dbi_ref, dlam_ref, dwr_ref,
             dwi_ref):
        row = lax.broadcasted_iota(jnp.int32, (T, LANES), 0)
        lam = lam_ref[...]
        ls = _log_sigmoid(lam)
        cb, br, bi = cb_ref[...], br_ref[...], bi_ref[...]
        wr, wi = wr_ref[...], wi_ref[...]
        for ref in (dcw_ref, dcb_ref, dbr_ref, dbi_ref, dlam_ref, dwr_ref, dwi_ref):
            ref[...] = jnp.zeros_like(ref)

        def chunk(it, carry):
            g_next, dxc_next = carry
            ci = n_chunk - 1 - it
            rows = pl.ds(pl.multiple_of(ci * T, T), T)
            before = pl.ds(pl.multiple_of(jnp.maximum(ci - 1, 0) * T, T), T)
            first = ci == 0
            xb = xb_ref[rows, :]
            prev_xb = jnp.where(first, 0.0, xb_ref[before, :])
            xc, xcb, r, i, a, m = _lru_gates(xb, prev_xb, row, cw_ref, cb, wr, br, wi, bi, ls)
            h = hs_ref[rows, :]
            h_prev = _shift_down(h, jnp.where(first, 0.0, hs_ref[before, :]), row, 1)
            gp = gp_ref[rows, :]
            dyv = dy_ref[rows, :]
            gelu, gelu_grad = _gelu_parts(gp)
            dgp_ref[rows, :] = (dyv * h * gelu_grad).astype(BF16)
            dh = dyv * gelu
            ca, cbv = _scan_chunk(a, a * dh, row, T, reverse=True)
            gp_acc = ca * g_next + cbv
            g = dh + jnp.where(row < T - 1, pltpu.roll(gp_acc, T - 1, 0), g_next)
            da = g * h_prev - (g * (i * xc)) * a / m
            dla = da * a
            dlam_ref[...] += jnp.sum(dla * (LRU_C * r), axis=0, keepdims=True)
            dpr = (dla * (LRU_C * ls)) * r * (1.0 - r)
            dpi = (g * m * xc) * i * (1.0 - i)
            dbr_ref[...] += jnp.sum(dpr, axis=0, keepdims=True)
            dbi_ref[...] += jnp.sum(dpi, axis=0, keepdims=True)
            dprb, dpib = dpr.astype(BF16), dpi.astype(BF16)
            dwr_ref[...] += _dot_tn(xcb, dprb)
            dwi_ref[...] += _dot_tn(xcb, dpib)
            dxc = g * m * i + _dot_nt(dprb, wr) + _dot_nt(dpib, wi)
            dcb_ref[...] += jnp.sum(dxc, axis=0, keepdims=True)
            dxb = jnp.zeros((T, LANES), F32)
            for k in range(CONV_WIDTH):
                tap = pl.ds(CONV_WIDTH - 1 - k, 1)
                dcw_ref[tap, :] += jnp.sum(dxc * _shift_down(xb, prev_xb, row, k), axis=0,
                                           keepdims=True)
                dxb = dxb + cw_ref[tap, :] * _shift_up(dxc, dxc_next, row, k, T)
            dxb_ref[rows, :] = dxb.astype(BF16)
            return _row_of(gp_acc, row, 0), dxc

        lax.fori_loop(0, n_chunk, chunk,
                      (jnp.zeros((1, LANES), F32), jnp.zeros((T, LANES), F32)))
        dlam_ref[...] = dlam_ref[...] * _sigmoid(-lam)

    vec_out = jax.ShapeDtypeStruct((1, D_MODEL), F32)
    mat_out = jax.ShapeDtypeStruct((N_CBLK, LANES, LANES), F32)
    return pl.pallas_call(
        body, name=name, grid=(N_CBLK,),
        in_specs=[col(0), col(0), col(N_CBLK), col(0), cwm, vec, mat, vec, mat, vec, vec],
        out_specs=[col(0), col(0), cwm, vec, vec, vec, vec, mat, mat],
        out_shape=[jax.ShapeDtypeStruct((S, D_MODEL), BF16), jax.ShapeDtypeStruct((S, D_MODEL), BF16),
                   jax.ShapeDtypeStruct((CONV_WIDTH, D_MODEL), F32),
                   vec_out, vec_out, vec_out, vec_out, mat_out, mat_out],
        compiler_params=_params("parallel"),
    )(dy, u, u, hs, conv_w, conv_b, wr, br, wi, bi, lam)


def _head_group_matrix(value):
    r = lax.broadcasted_iota(jnp.int32, (LANES, LANES), 0) // HEAD_DIM
    c = lax.broadcasted_iota(jnp.int32, (LANES, LANES), 1) // HEAD_DIM
    return jnp.where(r == c, value, 0.0).astype(BF16)


def _group_dot(x, p):
    hi = x.astype(BF16)
    lo = (x - hi.astype(F32)).astype(BF16)
    return _dot(hi, p) + _dot(lo, p)


def _head_mean(x, p):
    return _group_dot(x, p)


def _qk_prep(u, q_gain, k_gain, name, tm=512):
    S = u.shape[0]
    tm = _token_tile(S, tm)

    def body(q_ref, k_ref, v_ref, qg_ref, kg_ref, qn_ref, kn_ref, vb_ref):
        p = _head_group_matrix(1.0 / HEAD_DIM)
        for j in range(N_CBLK):
            cl = slice(j * LANES, (j + 1) * LANES)
            for x_ref, g_ref, o_ref, scale in ((q_ref, qg_ref, qn_ref, ATTN_SCALE),
                                               (k_ref, kg_ref, kn_ref, 1.0)):
                xv = x_ref[:, cl]
                rs = lax.rsqrt(_head_mean(xv * xv, p) + EPS)
                o_ref[:, cl] = (xv * rs * g_ref[...]).astype(BF16) * scale
        vb_ref[...] = v_ref[...].astype(BF16)

    blk = lambda off: pl.BlockSpec((tm, D_MODEL), lambda i: (i, off))
    out = jax.ShapeDtypeStruct((S, D_MODEL), BF16)
    return pl.pallas_call(
        body, name=name, grid=(S // tm,),
        in_specs=[blk(0), blk(1), blk(2), _resident((1, LANES)), _resident((1, LANES))],
        out_specs=[blk(0), blk(0), blk(0)],
        out_shape=[out, out, out],
        compiler_params=_params("parallel"),
    )(u, u, u, q_gain, k_gain)


def _qk_bwd(u, dqn, dkn, q_gain, k_gain, name, tm=512):
    S = u.shape[0]
    tm = _token_tile(S, tm)

    def body(q_ref, k_ref, dqn_ref, dkn_ref, qg_ref, kg_ref, dq_ref, dk_ref, dqg_ref, dkg_ref):
        p = _head_group_matrix(1.0 / HEAD_DIM)
        for x_ref, dn_ref, g_ref, dx_ref, dg_ref, scale in (
                (q_ref, dqn_ref, qg_ref, dq_ref, dqg_ref, ATTN_SCALE),
                (k_ref, dkn_ref, kg_ref, dk_ref, dkg_ref, 1.0)):
            dg = jnp.zeros((1, LANES), F32)
            for j in range(N_CBLK):
                cl = slice(j * LANES, (j + 1) * LANES)
                xv, dn = x_ref[:, cl], dn_ref[:, cl] * scale
                rs = lax.rsqrt(_head_mean(xv * xv, p) + EPS)
                xhat = xv * rs
                dxhat = dn * g_ref[...]
                dx_ref[:, cl] = (rs * (dxhat - xhat * _head_mean(dxhat * xhat, p))).astype(BF16)
                dg = dg + jnp.sum(dn * xhat, axis=0, keepdims=True)

            @pl.when(pl.program_id(0) == 0)
            def _():
                dg_ref[...] = jnp.zeros_like(dg_ref)
            dg_ref[...] += dg

            @pl.when(pl.program_id(0) == S // tm - 1)
            def _():
                dg_ref[...] += pltpu.roll(dg_ref[...], HEAD_DIM, 1)

    blk = lambda off: pl.BlockSpec((tm, D_MODEL), lambda i: (i, off))
    acc = pl.BlockSpec((1, LANES), lambda i: (0, 0))
    out = jax.ShapeDtypeStruct((S, D_MODEL), BF16)
    vec = jax.ShapeDtypeStruct((1, LANES), F32)
    return pl.pallas_call(
        body, name=name, grid=(S // tm,),
        in_specs=[blk(0), blk(1), blk(0), blk(0), _resident((1, LANES)), _resident((1, LANES))],
        out_specs=[blk(0), blk(0), acc, acc],
        out_shape=[out, out, vec, vec],
        compiler_params=_params("arbitrary"),
    )(u, u, dqn, dkn, q_gain, k_gain)


def _forget_fwd(f, b_f, name):
    S = f.shape[0]
    T = _token_tile(S, 256)

    def body(f_ref, b_ref, c_ref):
        row = lax.broadcasted_iota(jnp.int32, (T, LANES), 0)
        ones = jnp.ones((T, LANES), F32)
        bias = b_ref[...]

        def chunk(ci, carry):
            rows = pl.ds(pl.multiple_of(ci * T, T), T)
            _, c = _scan_chunk(ones, _log_sigmoid(f_ref[rows, :] + bias), row, T, reverse=False)
            c = c + carry
            c_ref[rows, :] = c
            return _row_of(c, row, T - 1)

        lax.fori_loop(0, S // T, chunk, jnp.zeros((1, LANES), F32))

    return pl.pallas_call(
        body, name=name,
        in_specs=[pl.BlockSpec(memory_space=pltpu.VMEM)] * 2,
        out_specs=pl.BlockSpec(memory_space=pltpu.VMEM),
        out_shape=jax.ShapeDtypeStruct((S, LANES), F32),
        compiler_params=pltpu.CompilerParams(vmem_limit_bytes=VMEM_LIMIT),
    )(f, b_f)


def _forget_bwd(dc_k, rho, f, b_f, name):
    S = f.shape[0]
    T = _token_tile(S, 256)
    n_chunk = S // T

    def body(dck_ref, rho_ref, f_ref, b_ref, df_ref, db_ref):
        row = lax.broadcasted_iota(jnp.int32, (T, LANES), 0)
        ones = jnp.ones((T, LANES), F32)
        bias = b_ref[...]
        pick = (lax.broadcasted_iota(jnp.int32, (D_MODEL, LANES), 0)
                == HEAD_DIM * lax.broadcasted_iota(jnp.int32, (D_MODEL, LANES), 1))
        pick = jnp.where(pick, 1.0, 0.0).astype(BF16)

        def chunk(it, carry):
            tail, db = carry
            rows = pl.ds(pl.multiple_of((n_chunk - 1 - it) * T, T), T)
            dc = dck_ref[rows, :] + _group_dot(rho_ref[rows, :], pick)
            _, dlf = _scan_chunk(ones, dc, row, T, reverse=True)
            dlf = dlf + tail
            df = dlf * _sigmoid(-(f_ref[rows, :] + bias))
            df_ref[rows, :] = df
            return _row_of(dlf, row, 0), db + jnp.sum(df, axis=0, keepdims=True)

        zero = jnp.zeros((1, LANES), F32)
        _, db = lax.fori_loop(0, n_chunk, chunk, (zero, zero))
        db_ref[...] = db

    return pl.pallas_call(
        body, name=name,
        in_specs=[pl.BlockSpec(memory_space=pltpu.VMEM)] * 4,
        out_specs=[pl.BlockSpec(memory_space=pltpu.VMEM)] * 2,
        out_shape=[jax.ShapeDtypeStruct((S, LANES), F32), jax.ShapeDtypeStruct((1, LANES), F32)],
        compiler_params=pltpu.CompilerParams(vmem_limit_bytes=VMEM_LIMIT),
    )(dc_k, rho, f, b_f)


ATTN_TILE = 512
ATTN_ROWS_FWD = 32


def _attn_tiles(S):
    t = _token_tile(S, ATTN_TILE)
    return t, S // t


def _causal(T):
    return (lax.broadcasted_iota(jnp.int32, (T, T), 1)
            <= lax.broadcasted_iota(jnp.int32, (T, T), 0))


def _attn_fwd(qs_, kn, vb, c_row, name):
    S = qs_.shape[0]
    T, n_t = _attn_tiles(S)
    RB = min(T, ATTN_ROWS_FWD)

    def body(q_ref, k_ref, v_ref, cr_ref, o_ref, lse_ref, sa_ref, sb_ref, p_ref, m_ref, l_ref,
             acc_ref, a_ref):
        qi = pl.program_id(1)
        lanes = [slice(h2 * HEAD_DIM, (h2 + 1) * HEAD_DIM) for h2 in range(2)]
        col = lax.broadcasted_iota(jnp.int32, (RB, T), 1)
        row = lax.broadcasted_iota(jnp.int32, (RB, T), 0)
        m_ref[...] = jnp.full(m_ref.shape, NEG_INF, F32)
        l_ref[...] = jnp.zeros_like(l_ref)
        acc_ref[...] = jnp.zeros_like(acc_ref)

        def logits_into(s_ref, kj):
            ks = pl.ds(pl.multiple_of(kj * T, T), T)
            for h2, hl in enumerate(lanes):
                s_ref[h2] = _dot_nt(q_ref[:, hl], k_ref[ks, hl]) - cr_ref[h2:h2 + 1, ks]

        def consume(s_ref, kj, masked):
            ks = pl.ds(pl.multiple_of(kj * T, T), T)
            for h2, hl in enumerate(lanes):
                blocks = [slice(i * RB, (i + 1) * RB) for i in range(T // RB)]

                def logits(i, rows):
                    s = s_ref[h2, rows, :]
                    return jnp.where(col <= row + i * RB, s, NEG_INF) if masked else s

                wide = lambda x: jnp.broadcast_to(x, (RB, LANES))
                for i, rows in enumerate(blocks):
                    mx = wide(jnp.max(logits(i, rows), axis=1, keepdims=True))
                    a_ref[h2, rows, :] = m_ref[h2, rows, :]
                    m_ref[h2, rows, :] = jnp.maximum(m_ref[h2, rows, :], mx)
                for i, rows in enumerate(blocks):
                    m_new = m_ref[h2, rows, :]
                    p = jnp.exp(logits(i, rows) - jnp.tile(m_new, (1, T // LANES)))
                    alpha = jnp.exp(a_ref[h2, rows, :] - m_new)
                    a_ref[h2, rows, :] = alpha
                    l_ref[h2, rows, :] = (alpha * l_ref[h2, rows, :]
                                          + wide(jnp.sum(p, axis=1, keepdims=True)))
                    p_ref[h2, rows, :] = p.astype(BF16)
                acc_ref[h2] = (a_ref[h2, :, :HEAD_DIM] * acc_ref[h2]
                               + _dot(p_ref[h2], v_ref[ks, hl]))

        logits_into(sa_ref, 0)

        def pair(i, _):
            logits_into(sb_ref, 2 * i + 1)
            consume(sa_ref, 2 * i, False)
            logits_into(sa_ref, 2 * i + 2)
            consume(sb_ref, 2 * i + 1, False)
            return 0

        lax.fori_loop(0, qi // 2, pair, 0)

        @pl.when(qi % 2 == 1)
        def _():
            logits_into(sb_ref, qi)
            consume(sa_ref, qi - 1, False)
            consume(sb_ref, qi, True)

        @pl.when(qi % 2 == 0)
        def _():
            consume(sa_ref, qi, True)

        for h2, hl in enumerate(lanes):
            o_ref[:, hl] = (acc_ref[h2] / l_ref[h2, :, :HEAD_DIM]).astype(BF16)
            lse_ref[:, hl] = m_ref[h2, :, :HEAD_DIM] + jnp.log(l_ref[h2, :, :HEAD_DIM])

    qblk = pl.BlockSpec((T, LANES), lambda h, i: (i, h))
    kv = pl.BlockSpec((S, LANES), lambda h, i: (0, h))
    return pl.pallas_call(
        body, name=name, grid=(N_CBLK, n_t),
        in_specs=[qblk, kv, kv, pl.BlockSpec((None, 2, S), lambda h, i: (h, 0, 0))],
        out_specs=[qblk, qblk],
        out_shape=[jax.ShapeDtypeStruct((S, D_MODEL), BF16),
                   jax.ShapeDtypeStruct((S, D_MODEL), F32)],
        scratch_shapes=[pltpu.VMEM((2, T, T), F32), pltpu.VMEM((2, T, T), F32),
                        pltpu.VMEM((2, T, T), BF16),
                        pltpu.VMEM((2, T, LANES), F32), pltpu.VMEM((2, T, LANES), F32),
                        pltpu.VMEM((2, T, HEAD_DIM), F32), pltpu.VMEM((2, T, LANES), F32)],
        compiler_params=_params("parallel", "parallel"),
    )(qs_, kn, vb, c_row)


def _attn_bwd(qs_, kn, vb, do, o, lse, c_row, name):
    S = qs_.shape[0]
    T, n_t = _attn_tiles(S)

    def body(q_ref, k_ref, v_ref, do_ref, o_ref, lse_ref, cr_ref,
             dq_ref, dk_ref, dv_ref, dc_ref, rho_ref, dd_ref):
        kj = pl.program_id(1)
        causal = _causal(T)
        lanes = [slice(h2 * HEAD_DIM, (h2 + 1) * HEAD_DIM) for h2 in range(2)]
        ones = [slice(h2 * HEAD_DIM, h2 * HEAD_DIM + 1) for h2 in range(2)]

        @pl.when(kj == 0)
        def _():
            dq_ref[...] = jnp.zeros_like(dq_ref)
            rho_ref[...] = jnp.zeros_like(rho_ref)
            p_sum = _head_group_matrix(1.0)

            def fill(ci, _):
                rows = pl.ds(pl.multiple_of(ci * T, T), T)
                dd_ref[rows, :] = _group_dot(do_ref[rows, :].astype(F32) * o_ref[rows, :].astype(F32),
                                             p_sum)
                return 0

            lax.fori_loop(0, n_t, fill, 0)

        kh = [k_ref[:, hl] for hl in lanes]
        vh = [v_ref[:, hl] for hl in lanes]
        ck = [cr_ref[h2:h2 + 1, :] for h2 in range(2)]

        def step(qi, carry, masked):
            qs = pl.ds(pl.multiple_of(qi * T, T), T)
            out = []
            for h2, hl in enumerate(lanes):
                dk, dv, dc = carry[h2]
                qh, doh = q_ref[qs, hl], do_ref[qs, hl]
                s = _dot_nt(qh, kh[h2]) - ck[h2]
                if masked:
                    s = jnp.where(causal, s, NEG_INF)
                p = jnp.exp(s - lse_ref[qs, ones[h2]])
                ds = p * (_dot_nt(doh, vh[h2]) - dd_ref[qs, ones[h2]])
                dsb = ds.astype(BF16)
                dq_ref[qs, hl] += _dot(dsb, kh[h2])
                rho_ref[qs, hl] += jnp.broadcast_to(jnp.sum(ds, axis=1, keepdims=True),
                                                    (T, HEAD_DIM))
                out.append((dk + _dot_tn(dsb, qh), dv + _dot_tn(p.astype(BF16), doh),
                            dc - jnp.sum(ds, axis=0, keepdims=True)))
            return tuple(out)

        init = tuple((jnp.zeros((T, HEAD_DIM), F32), jnp.zeros((T, HEAD_DIM), F32),
                      jnp.zeros((1, T), F32)) for _ in lanes)
        carry = step(kj, init, True)
        carry = lax.fori_loop(kj + 1, n_t, lambda qi, c: step(qi, c, False), carry)
        for h2, ((dk, dv, dc), hl) in enumerate(zip(carry, lanes)):
            dk_ref[:, hl] = dk
            dv_ref[:, hl] = dv.astype(BF16)
            dc_ref[h2:h2 + 1, :] = dc

    kblk = pl.BlockSpec((T, LANES), lambda h, j: (j, h))
    full = pl.BlockSpec((S, LANES), lambda h, j: (0, h))
    crow = pl.BlockSpec((None, 2, T), lambda h, j: (h, 0, j))
    wide = jax.ShapeDtypeStruct((S, D_MODEL), F32)
    return pl.pallas_call(
        body, name=name, grid=(N_CBLK, n_t),
        in_specs=[full, kblk, kblk, full, full, full, crow],
        out_specs=[full, kblk, kblk, crow, full],
        out_shape=[wide, wide, jax.ShapeDtypeStruct((S, D_MODEL), BF16),
                   jax.ShapeDtypeStruct((N_CBLK, 2, S), F32), wide],
        scratch_shapes=[pltpu.VMEM((S, LANES), F32)],
        compiler_params=_params("parallel", "arbitrary"),
    )(qs_, kn, vb, do, o, lse, c_row)


ALL_PEERS = tuple(range(1, N_DEV))
NEAR_PEERS = (1, 2, 4, 6)
FAR_CHIPS = (2, 4, 6)


def _landing_shapes(arrays, gathers):
    return [jax.ShapeDtypeStruct((N_DEV,) + a.shape if g else a.shape, a.dtype)
            for a, g in zip(arrays, gathers)]


def _my_index():
    return 4 * lax.axis_index("x") + 2 * lax.axis_index("y") + lax.axis_index("c")


def _own_copies(srcs, lands, gathers, sems):
    me = _my_index()
    return [pltpu.make_async_copy(src if g else src.at[me], land.at[me], sems.at[a])
            for a, (src, land, g) in enumerate(zip(srcs, lands, gathers))]


def _peer_copies(srcs, lands, gathers, send_sems, recv_sems, ks=ALL_PEERS):
    x, y, c = lax.axis_index("x"), lax.axis_index("y"), lax.axis_index("c")
    me = 4 * x + 2 * y + c
    out = []
    for j, k in enumerate(ks):
        to = (1 - x if k & 4 else x, 1 - y if k & 2 else y, 1 - c if k & 1 else c)
        peer = 4 * to[0] + 2 * to[1] + to[2]
        for a, (src, land, g) in enumerate(zip(srcs, lands, gathers)):
            sem = a * len(ks) + j
            src_blk = src if g else src.at[peer]

            def copy(slot, src_blk=src_blk, land=land, sem=sem, to=to):
                return pltpu.make_async_remote_copy(
                    src_ref=src_blk, dst_ref=land.at[slot], send_sem=send_sems.at[sem],
                    recv_sem=recv_sems.at[sem], device_id=to,
                    device_id_type=pl.DeviceIdType.MESH)

            out.append((k, a, copy(me), copy(peer)))
    return out


def _forward_copies(lands, send_sems, recv_sems):
    x, y, c = lax.axis_index("x"), lax.axis_index("y"), lax.axis_index("c")
    out = []
    for j, f in enumerate(FAR_CHIPS):
        chip = 4 * (1 - x if f & 4 else x) + 2 * (1 - y if f & 2 else y)
        for a, land in enumerate(lands):
            sem = a * len(FAR_CHIPS) + j

            def copy(slot, land=land, sem=sem):
                return pltpu.make_async_remote_copy(
                    src_ref=land.at[slot], dst_ref=land.at[slot], send_sem=send_sems.at[sem],
                    recv_sem=recv_sems.at[sem], device_id=(x, y, 1 - c),
                    device_id_type=pl.DeviceIdType.MESH)

            out.append((f, a, copy(chip + c), copy(chip + 1 - c)))
    return out


def _exchange(arrays, gathers, name, two_level=False):
    n = len(arrays)
    ks = NEAR_PEERS if two_level else ALL_PEERS
    assert not two_level or all(gathers)

    def body(*refs):
        ins, outs = refs[:n], refs[n:2 * n]
        send_sems, recv_sems, own_sems, fwd_send_sems, fwd_recv_sems = refs[2 * n:]
        own = _own_copies(ins, outs, gathers, own_sems)
        for cp in own:
            cp.start()
        copies = _peer_copies(ins, outs, gathers, send_sems, recv_sems, ks)
        for _, _, send, _ in copies:
            send.start()
        passed = {}
        if two_level:
            passed = {(f, a): (send, arrival)
                      for f, a, send, arrival in _forward_copies(outs, fwd_send_sems, fwd_recv_sems)}
        for k, a, _, arrival in copies:
            arrival.wait_recv()
            if (k, a) in passed:
                passed[k, a][0].start()
        for send, arrival in passed.values():
            arrival.wait_recv()
            send.wait_send()
        for _, _, send, _ in copies:
            send.wait_send()
        for cp in own:
            cp.wait()

    hbm = pl.BlockSpec(memory_space=pl.ANY)
    return pl.pallas_call(
        body, name=name,
        in_specs=[hbm] * n, out_specs=[hbm] * n, out_shape=_landing_shapes(arrays, gathers),
        scratch_shapes=[pltpu.SemaphoreType.DMA((n * len(ks),)),
                        pltpu.SemaphoreType.DMA((n * len(ks),)),
                        pltpu.SemaphoreType.DMA((n,)),
                        pltpu.SemaphoreType.DMA((n * len(FAR_CHIPS),)),
                        pltpu.SemaphoreType.DMA((n * len(FAR_CHIPS),))],
        compiler_params=pltpu.CompilerParams(has_side_effects=True),
    )(*arrays)


_HBM = pl.BlockSpec(memory_space=pltpu.HBM)
_SEM = pl.BlockSpec(memory_space=pltpu.SEMAPHORE)
_ANY = pl.BlockSpec(memory_space=pl.ANY)
_DATAFLOW = pltpu.SideEffectType.DATAFLOW_SIDE_EFFECTING


def _in_hbm(a):
    return pltpu.with_memory_space_constraint(a, pltpu.HBM)


def _exchange_start(arrays, gathers, after, name, ks=ALL_PEERS):
    n = len(arrays)
    lands = [lax.empty(s.shape, s.dtype) for s in _landing_shapes(arrays, gathers)]

    def body(*refs):
        srcs, dsts = refs[:n], refs[n:2 * n]
        send_sems, recv_sems, own_sems = refs[2 * n + 1:2 * n + 4]
        token = refs[-1]
        for cp in _own_copies(srcs, dsts, gathers, own_sems):
            cp.start()
        for _, _, send, _ in _peer_copies(srcs, dsts, gathers, send_sems, recv_sems, ks):
            send.start()
        token[...] = jnp.zeros_like(token)

    hbm_like = [pltpu.HBM(a.shape, a.dtype) for a in list(arrays) + lands]
    res = pl.pallas_call(
        body, name=name,
        in_specs=[_HBM] * (2 * n) + [_ANY],
        out_specs=(_SEM, _SEM, _SEM, *[_HBM] * (2 * n), pl.BlockSpec(memory_space=pltpu.VMEM)),
        out_shape=(pltpu.SemaphoreType.DMA((n * len(ks),)), pltpu.SemaphoreType.DMA((n * len(ks),)),
                   pltpu.SemaphoreType.DMA((n,)), *hbm_like,
                   jax.ShapeDtypeStruct((8, LANES), F32)),
        input_output_aliases={i: 3 + i for i in range(2 * n)},
        compiler_params=pltpu.CompilerParams(has_side_effects=_DATAFLOW),
    )(*[_in_hbm(a) for a in list(arrays) + lands], after)
    return (res[0], res[1], res[2], res[3:3 + n], res[3 + n:3 + 2 * n]), res[-1]


def _exchange_wait(started, gathers, after, name, ks=ALL_PEERS):
    send_sems, recv_sems, own_sems, arrays, lands = started
    n = len(arrays)

    def body(*refs):
        srcs, dsts = refs[:n], refs[n:2 * n]
        for _, _, send, arrival in _peer_copies(srcs, dsts, gathers, refs[2 * n], refs[2 * n + 1],
                                                ks):
            arrival.wait_recv()
            send.wait_send()
        for cp in _own_copies(srcs, dsts, gathers, refs[2 * n + 2]):
            cp.wait()

    hbm_like = [pltpu.HBM(a.shape, a.dtype) for a in list(arrays) + list(lands)]
    res = pl.pallas_call(
        body, name=name,
        in_specs=[_HBM] * (2 * n) + [_SEM, _SEM, _SEM, _ANY],
        out_specs=[_HBM] * (2 * n), out_shape=hbm_like,
        input_output_aliases={i: i for i in range(2 * n)},
        compiler_params=pltpu.CompilerParams(has_side_effects=_DATAFLOW),
    )(*arrays, *lands, send_sems, recv_sems, own_sems, after)
    return res[n:]


def _forward_start(lands, after, name):
    n = len(lands)

    def body(*refs):
        send_sems, recv_sems = refs[n + 1:n + 3]
        for _, _, send, _ in _forward_copies(refs[:n], send_sems, recv_sems):
            send.start()
        refs[-1][...] = jnp.zeros_like(refs[-1])

    n_sem = n * len(FAR_CHIPS)
    res = pl.pallas_call(
        body, name=name,
        in_specs=[_HBM] * n + [_ANY],
        out_specs=(_SEM, _SEM, *[_HBM] * n, pl.BlockSpec(memory_space=pltpu.VMEM)),
        out_shape=(pltpu.SemaphoreType.DMA((n_sem,)), pltpu.SemaphoreType.DMA((n_sem,)),
                   *[pltpu.HBM(a.shape, a.dtype) for a in lands],
                   jax.ShapeDtypeStruct((8, LANES), F32)),
        input_output_aliases={i: 2 + i for i in range(n)},
        compiler_params=pltpu.CompilerParams(has_side_effects=_DATAFLOW),
    )(*[_in_hbm(a) for a in lands], after)
    return (res[0], res[1], res[2:2 + n]), res[-1]


def _forward_wait(started, after, name):
    send_sems, recv_sems, lands = started
    n = len(lands)

    def body(*refs):
        for _, _, send, arrival in _forward_copies(refs[:n], refs[n], refs[n + 1]):
            arrival.wait_recv()
            send.wait_send()

    return pl.pallas_call(
        body, name=name,
        in_specs=[_HBM] * n + [_SEM, _SEM, _ANY],
        out_specs=[_HBM] * n, out_shape=[pltpu.HBM(a.shape, a.dtype) for a in lands],
        input_output_aliases={i: i for i in range(n)},
        compiler_params=pltpu.CompilerParams(has_side_effects=_DATAFLOW),
    )(*lands, send_sems, recv_sems, after)


def _reduce_adamw(parts, w, m, v, name):
    n_layer = len(parts)
    n, R, C = parts[0].shape
    tr = 256 if R % 256 == 0 else R
    n_t = R // tr

    def body(*refs):
        p_refs = refs[:n_layer]
        w_ref, m_ref, v_ref, g_ref, d_ref, nm_ref, nv_ref = refs[n_layer:]

        def update(p_ref):
            g = p_ref[0].astype(F32)
            for s in range(1, n):
                g = g + p_ref[s].astype(F32)
            g_ref[...] = g
            m_new = ADAM_B1 * m_ref[...] + (1.0 - ADAM_B1) * g
            v_new = ADAM_B2 * v_ref[...] + (1.0 - ADAM_B2) * (g * g)
            nm_ref[...] = m_new
            nv_ref[...] = v_new
            m_hat = m_new / (1.0 - ADAM_B1 ** ADAM_STEP)
            v_hat = v_new / (1.0 - ADAM_B2 ** ADAM_STEP)
            d_ref[...] = -ADAM_LR * (m_hat / (jnp.sqrt(v_hat) + ADAM_EPS) + ADAM_WD * w_ref[...])

        for layer, p_ref in enumerate(p_refs):
            pl.when(pl.program_id(0) == layer)(functools.partial(update, p_ref))

    def parts_spec(layer):
        def index(l, i):
            return 0, jnp.where(l < layer, 0, jnp.where(l > layer, n_t - 1, i)), 0
        return pl.BlockSpec((n, tr, C), index)

    blk = pl.BlockSpec((None, tr, C), lambda l, i: (l, i, 0))
    out = jax.ShapeDtypeStruct((n_layer, R, C), F32)
    return pl.pallas_call(
        body, name=name, grid=(n_layer, n_t),
        in_specs=[parts_spec(layer) for layer in range(n_layer)] + [blk, blk, blk],
        out_specs=[blk] * 4, out_shape=[out] * 4,
        compiler_params=_params("arbitrary", "arbitrary"),
    )(*parts, w, m, v)


def _pack(arrays):
    flat = jnp.concatenate([a.reshape(-1).astype(F32) for a in arrays])
    pad = (-flat.shape[0]) % (8 * LANES)
    return jnp.pad(flat, (0, pad)).reshape(-1, LANES)


def _unpack(buf, shapes):
    flat = buf.reshape(-1)
    out, off = [], 0
    for shp in shapes:
        size = 1
        for s in shp:
            size *= s
        out.append(flat[off:off + size].reshape(shp))
        off += size
    return out


def _block_diag_pairs(w):
    w = w.reshape(N_CBLK, 2, LRU_BLOCK_DIM, LRU_BLOCK_DIM)
    z = jnp.zeros_like(w[:, 0])
    top = jnp.concatenate([w[:, 0], z], axis=2)
    bot = jnp.concatenate([z, w[:, 1]], axis=2)
    return jnp.concatenate([top, bot], axis=1)


def _diag_pairs(m):
    h = LRU_BLOCK_DIM
    return jnp.stack([m[:, :h, :h], m[:, h:, h:]], axis=1).reshape(2 * N_CBLK, h, h)


SMALL = ("mlp_norm", "lru_conv_b", "lru_w_r", "lru_b_r", "lru_w_i", "lru_b_i",
         "lru_lambda", "fox_b_f", "fox_q_gain", "fox_k_gain")
WEIGHTS = ("mix_norm", "mlp_norm", "mlp_w1", "mlp_w2", "lru_w_in", "lru_conv_w", "lru_conv_b",
           "lru_w_r", "lru_b_r", "lru_w_i", "lru_b_i", "lru_lambda", "lru_w_out", "fox_w_in",
           "fox_b_f", "fox_q_gain", "fox_k_gain", "fox_w_out")


def kernel(x, mix_norm, mlp_norm, mlp_w1, mlp_w2, lru_w_in, lru_conv_w, lru_conv_b, lru_w_r, lru_b_r, lru_w_i, lru_b_i, lru_lambda, lru_w_out, fox_w_in, fox_b_f, fox_q_gain, fox_k_gain, fox_w_out, loss_target, m_mix_norm, m_mlp_norm, m_mlp_w1, m_mlp_w2, m_lru_w_in, m_lru_conv_w, m_lru_conv_b, m_lru_w_r, m_lru_b_r, m_lru_w_i, m_lru_b_i, m_lru_lambda, m_lru_w_out, m_fox_w_in, m_fox_b_f, m_fox_q_gain, m_fox_k_gain, m_fox_w_out, v_mix_norm, v_mlp_norm, v_mlp_w1, v_mlp_w2, v_lru_w_in, v_lru_conv_w, v_lru_conv_b, v_lru_w_r, v_lru_b_r, v_lru_w_i, v_lru_b_i, v_lru_lambda, v_lru_w_out, v_fox_w_in, v_fox_b_f, v_fox_q_gain, v_fox_k_gain, v_fox_w_out):
    w_in = dict(mix_norm=mix_norm, mlp_norm=mlp_norm, mlp_w1=mlp_w1, mlp_w2=mlp_w2,
                lru_w_in=lru_w_in, lru_conv_w=lru_conv_w, lru_conv_b=lru_conv_b, lru_w_r=lru_w_r,
                lru_b_r=lru_b_r, lru_w_i=lru_w_i, lru_b_i=lru_b_i, lru_lambda=lru_lambda,
                lru_w_out=lru_w_out, fox_w_in=fox_w_in, fox_b_f=fox_b_f, fox_q_gain=fox_q_gain,
                fox_k_gain=fox_k_gain, fox_w_out=fox_w_out)
    m_in = dict(mix_norm=m_mix_norm, mlp_norm=m_mlp_norm, mlp_w1=m_mlp_w1, mlp_w2=m_mlp_w2,
                lru_w_in=m_lru_w_in, lru_conv_w=m_lru_conv_w, lru_conv_b=m_lru_conv_b,
                lru_w_r=m_lru_w_r, lru_b_r=m_lru_b_r, lru_w_i=m_lru_w_i, lru_b_i=m_lru_b_i,
                lru_lambda=m_lru_lambda, lru_w_out=m_lru_w_out, fox_w_in=m_fox_w_in,
                fox_b_f=m_fox_b_f, fox_q_gain=m_fox_q_gain, fox_k_gain=m_fox_k_gain,
                fox_w_out=m_fox_w_out)
    v_in = dict(mix_norm=v_mix_norm, mlp_norm=v_mlp_norm, mlp_w1=v_mlp_w1, mlp_w2=v_mlp_w2,
                lru_w_in=v_lru_w_in, lru_conv_w=v_lru_conv_w, lru_conv_b=v_lru_conv_b,
                lru_w_r=v_lru_w_r, lru_b_r=v_lru_b_r, lru_w_i=v_lru_w_i, lru_b_i=v_lru_b_i,
                lru_lambda=v_lru_lambda, lru_w_out=v_lru_w_out, fox_w_in=v_fox_w_in,
                fox_b_f=v_fox_b_f, fox_q_gain=v_fox_q_gain, fox_k_gain=v_fox_k_gain,
                fox_w_out=v_fox_w_out)
    D = D_MODEL
    S = x.shape[1]
    x0, target = x[0], loss_target[0]
    me = 4 * lax.axis_index("x") + 2 * lax.axis_index("y") + lax.axis_index("c")

    def bf16(a):
        return a.astype(BF16)

    (lru_in_g,) = _exchange([bf16(lru_w_in[0])], [True], "gather_lru_in", two_level=True)
    gather_lru, tok = _exchange_start([bf16(lru_w_out[0]), lru_conv_w[0]], [True] * 2, lru_in_g,
                                      "gather_lru_start")
    gather_mlp0, tok = _exchange_start([bf16(mlp_w1[0]), bf16(mlp_w2[0])], [True] * 2, tok,
                                       "gather_mlp0_start", NEAR_PEERS)
    gather_fox, tok = _exchange_start([bf16(fox_w_in[0]), bf16(fox_w_out[0])], [True] * 2, tok,
                                      "gather_fox_start")
    gather_mlp1, tok = _exchange_start([bf16(mlp_w1[1]), bf16(mlp_w2[1])], [True] * 2, tok,
                                       "gather_mlp1_start", NEAR_PEERS)

    def pass_on(started, after, name):
        lands = _exchange_wait(started, [True] * 2, after, name + "_wait", NEAR_PEERS)
        return _forward_start(lands, after, name + "_pass_start")
    wr =_block_diag_pairs(lru_w_r[0]).astype(BF16)
    wi = _block_diag_pairs(lru_w_i[0]).astype(BF16)
    b_r, b_i = lru_b_r.reshape(1, D), lru_b_i.reshape(1, D)
    q_gain, k_gain = jnp.tile(fox_q_gain, (1, 2)), jnp.tile(fox_k_gain, (1, 2))
    b_f = jnp.pad(fox_b_f, ((0, 0), (0, LANES - N_HEADS)))
    g_mix0, g_mix1 = mix_norm[0:1] + tok[0, 0], mix_norm[1:2]
    g_mlp0, g_mlp1 = mlp_norm[0:1], mlp_norm[1:2]

    (u0,), h0 = _norm_matmul(x0, g_mix0, [lru_in_g], "lru_in_proj")
    lru_out_g, conv_g = _exchange_wait(gather_lru, [True] * 2, u0, "gather_lru_wait")
    lru_out_w = lru_out_g.reshape(D, D)
    conv_w = conv_g.transpose(1, 0, 2).reshape(CONV_WIDTH, D)
    y_lru, hs =_lru_fwd(u0, conv_w, lru_conv_b, wr, b_r, wi, b_i, lru_lambda, "lru_core")
    pass_mlp0, tok = pass_on(gather_mlp0, y_lru, "gather_mlp0")
    x1 = _matmul_res(y_lru, lru_out_w, x0, "lru_out_proj", tok)
    w1g0, w2g0 = _forward_wait(pass_mlp0, x1, "gather_mlp0_pass_wait")
    x2, h1, r1 = _mlp_fwd(x1, g_mlp0, w1g0, w2g0, "mlp0")
    fox_in_g, fox_out_g = _exchange_wait(gather_fox, [True] * 2, x2, "gather_fox_wait")
    fox_out_w = fox_out_g.reshape(D, D)
    fox_full = jnp.concatenate([fox_in_g[d] for d in range(N_DEV)], axis=1)
    wqkv = fox_full[:, :3 * D].reshape(D, 3, D).transpose(1, 0, 2)
    wf = jnp.pad(fox_full[:, 3 * D:], ((0, 0), (0, LANES - N_HEADS)))[None]
    (u_qkv, f), h2 = _norm_matmul(x2, g_mix1, [wqkv, wf], "fox_in_proj")
    qn, kn, vb = _qk_prep(u_qkv, q_gain, k_gain, "fox_qk_norm")
    c_col = _forget_fwd(f, b_f, "fox_forget")
    c_row = c_col[:, :N_HEADS].T.reshape(N_CBLK, 2, S)
    o, lse = _attn_fwd(qn, kn, vb, c_row, "fox_attn")
    pass_mlp1, tok = pass_on(gather_mlp1, o, "gather_mlp1")
    x3 = _matmul_res(o, fox_out_w, x2, "fox_out_proj", tok)
    w1g1, w2g1 = _forward_wait(pass_mlp1, x3, "gather_mlp1_pass_wait")
    loss_local, dx4, h3, r3 = _mlp_fwd(x3, g_mlp1, w1g1, w2g1, "mlp1", target)

    dx3, dg_mlp1, da3 = _mlp_bwd(dx4, x3, g_mlp1, r3, w1g1, w2g1, "mlp1_bwd")
    dw1_1 = _matmul_tn(h3, da3, "mlp1_dw1", cols=2, col_blocks=N_DEV)
    dw2_1 = _matmul_tn(r3, dx4, "mlp1_dw2", rows=2, a_square=True).reshape(N_DEV, -1, D)
    grads_mlp1, tok = _exchange_start([dw1_1, dw2_1], [False] * 2, tok, "grads_mlp1_start")
    do = _matmul_nt(dx3, fox_out_w, "fox_out_bwd", BF16, tok)
    d_fox_out = _matmul_tn(o, dx3, "fox_out_dw").reshape(N_DEV, -1, D)
    dqn, dkn, dv, dc_row, rho = _attn_bwd(qn, kn, vb, do, o, lse, c_row, "fox_attn_bwd")
    duq, duk, dq_gain, dk_gain = _qk_bwd(u_qkv, dqn, dkn, q_gain, k_gain, "fox_qk_norm_bwd")
    dc_k = jnp.pad(dc_row.reshape(N_HEADS, S).T, ((0, 0), (0, LANES - N_HEADS)))
    df, db_f = _forget_bwd(dc_k, rho, f, b_f, "fox_forget_bwd")
    dx2, dg_mix1 = _proj_bwd([[duq, duk, dv], [df]], [wqkv, wf], x2, g_mix1, dx3, "fox_in_bwd")
    d_fox_in = jnp.concatenate(
        [_matmul_tn(h2, duq, "fox_in_dwq"), _matmul_tn(h2, duk, "fox_in_dwk"),
         _matmul_tn(h2, dv, "fox_in_dwv"), _matmul_tn(h2, df, "fox_in_dwf")[:, :N_HEADS]], axis=1)
    shard = (3 * D + N_HEADS) // N_DEV
    d_fox_in = jnp.stack([d_fox_in[:, d * shard:(d + 1) * shard] for d in range(N_DEV)])
    grads_fox, tok = _exchange_start([d_fox_in, d_fox_out], [False] * 2, tok, "grads_fox_start")
    dx1, dg_mlp0, da1 = _mlp_bwd(dx2, x1, g_mlp0 + tok[0, 0], r1, w1g0, w2g0, "mlp0_bwd")
    dw1_0 = _matmul_tn(h1, da1, "mlp0_dw1", cols=2, col_blocks=N_DEV)
    dw2_0 = _matmul_tn(r1, dx2, "mlp0_dw2", rows=2, a_square=True).reshape(N_DEV, -1, D)
    grads_mlp0, tok = _exchange_start([dw1_0, dw2_0], [False] * 2, tok, "grads_mlp0_start")
    dy_lru = _matmul_nt(dx1, lru_out_w, "lru_out_bwd", F32, tok)
    d_lru_out = _matmul_tn(y_lru, dx1, "lru_out_dw").reshape(N_DEV, -1, D)
    dgp, dxb, d_conv_w, d_conv_b, d_b_r, d_b_i, d_lam, d_wr, d_wi = _lru_bwd(
        dy_lru, u0, hs, conv_w, lru_conv_b, wr, b_r, wi, b_i, lru_lambda, "lru_core_bwd")

    small_grads = dict(
        mlp_norm=jnp.concatenate([dg_mlp0, dg_mlp1], axis=0),
        lru_conv_b=d_conv_b, lru_w_r=_diag_pairs(d_wr), lru_b_r=d_b_r, lru_w_i=_diag_pairs(d_wi),
        lru_b_i=d_b_i, lru_lambda=d_lam, fox_b_f=db_f[:, :N_HEADS],
        fox_q_gain=dq_gain[:, :HEAD_DIM], fox_k_gain=dk_gain[:, :HEAD_DIM])
    small_partial = _pack([dg_mix1] + [small_grads[n] for n in SMALL] + [d_conv_w, loss_local])
    grads_lru_out, tok = _exchange_start([d_lru_out, small_partial], [False, True], tok,
                                         "grads_lru_out_start")
    dx0, dg_mix0 = _proj_bwd([[dgp, dxb]], [lru_in_g], x0, mix_norm[0:1] + tok[0, 0], dx1,
                             "lru_in_bwd")
    d_lru_in = jnp.concatenate([_matmul_tn(h0, dgp, "lru_in_dw_gate", col_blocks=4),
                                _matmul_tn(h0, dxb, "lru_in_dw_x", col_blocks=4)], axis=0)
    grads_lru_in, tok = _exchange_start([d_lru_in, dg_mix0], [False, True], tok,
                                        "grads_lru_in_start")

    grads, deltas, new_m, new_v = {}, {}, {}, {}

    def update(name, parts):
        w, m, v = w_in[name], m_in[name], v_in[name]
        shape = w.shape
        stacked = (len(parts), -1, shape[-1])
        w3 = w.reshape(stacked)
        res = _reduce_adamw([p.reshape((N_DEV,) + w3.shape[1:]) for p in parts], w3,
                            m.reshape(stacked), v.reshape(stacked), "adamw_" + name)
        return [r.reshape(shape) for r in res]

    def store(name, res):
        grads[name], deltas[name], new_m[name], new_v[name] = res

    p_w1_1, p_w2_1 = _exchange_wait(grads_mlp1, [False] * 2, tok, "grads_mlp1_wait")
    p_fox_in, p_fox_out = _exchange_wait(grads_fox, [False] * 2, p_w1_1, "grads_fox_wait")
    store("fox_w_in", update("fox_w_in", [p_fox_in]))
    store("fox_w_out", update("fox_w_out", [p_fox_out]))
    p_w1_0, p_w2_0 = _exchange_wait(grads_mlp0, [False] * 2, grads["fox_w_out"], "grads_mlp0_wait")
    store("mlp_w1", update("mlp_w1", [p_w1_0, p_w1_1]))
    store("mlp_w2", update("mlp_w2", [p_w2_0, p_w2_1]))
    p_lru_out, p_small = _exchange_wait(grads_lru_out, [False, True], grads["mlp_w2"],
                                        "grads_lru_out_wait")
    store("lru_w_out", update("lru_w_out", [p_lru_out]))
    p_lru_in, p_mix0 = _exchange_wait(grads_lru_in, [False, True], grads["lru_w_out"],
                                      "grads_lru_in_wait")
    store("lru_w_in", update("lru_w_in", [p_lru_in]))

    mix0 = [r[0] for r in _reduce_adamw([p_mix0], mix_norm[None, 0:1], m_mix_norm[None, 0:1],
                                        v_mix_norm[None, 0:1], "adamw_mix0")]
    packed = lambda src, first: _pack([first] + [src[n] for n in SMALL]
                                      + [jnp.zeros((CONV_WIDTH, D)), jnp.zeros((1, 1))])[None]
    small_shapes = [(1, D)] + [w_in[n].shape for n in SMALL]
    n_small = sum(math.prod(s) for s in small_shapes)
    res_small = _reduce_adamw([p_small], packed(w_in, mix_norm[1:2]), packed(m_in, m_mix_norm[1:2]),
                              packed(v_in, v_mix_norm[1:2]), "adamw_small")
    for name, *vals in zip(("mix1",) + SMALL, *[_unpack(r, small_shapes) for r in res_small]):
        if name == "mix1":
            vals = [jnp.concatenate([r0, r1], axis=0) for r0, r1 in zip(mix0, vals)]
            name = "mix_norm"
        store(name, vals)
    conv_parts = p_small.reshape(N_DEV, -1)[:, n_small:n_small + CONV_WIDTH * D]
    conv_parts = conv_parts.reshape(N_DEV, CONV_WIDTH, N_DEV, LANES)
    conv_parts = lax.dynamic_index_in_dim(conv_parts, me, axis=2, keepdims=False)
    store("lru_conv_w", update("lru_conv_w", [conv_parts]))

    loss = res_small[0].reshape(-1)[n_small + CONV_WIDTH * D]
    return (loss, dx0[None], *[grads[n] for n in WEIGHTS], *[deltas[n] for n in WEIGHTS],
            *[new_m[n] for n in WEIGHTS], *[new_v[n] for n in WEIGHTS])
```

```python
import functools
import math

import jax
import jax.numpy as jnp
from jax import lax
from jax.experimental import pallas as pl
from jax.experimental.pallas import tpu as pltpu

F32 = jnp.float32
BF16 = jnp.bfloat16

N_DEV = 8
D_MODEL = 1024
D_FF = 4096
N_HEADS = 16
HEAD_DIM = 64
LRU_BLOCK_DIM = 64
CONV_WIDTH = 4
LRU_C = 8.0
EPS = 1e-6
NEG_INF = -1e30
ATTN_SCALE = HEAD_DIM ** -0.5
LANES = 128
N_CBLK = D_MODEL // LANES
VMEM_LIMIT = 52 * 2 ** 20

ADAM_LR = 0.001
ADAM_B1 = 0.9
ADAM_B2 = 0.999
ADAM_EPS = 1e-08
ADAM_WD = 0.01
ADAM_STEP = 10

_NT = (((1,), (1,)), ((), ()))
_TN = (((0,), (0,)), ((), ()))


def _params(*sem):
    return pltpu.CompilerParams(dimension_semantics=sem, vmem_limit_bytes=VMEM_LIMIT)


def _resident(shape):
    zeros = (0,) * len(shape)
    return pl.BlockSpec(shape, lambda *_: zeros, pipeline_mode=pl.Buffered(1))


def _dot(a, b):
    return jnp.dot(a, b, preferred_element_type=F32)


def _dot_nt(a, b):
    return lax.dot_general(a, b, _NT, preferred_element_type=F32)


def _dot_tn(a, b):
    return lax.dot_general(a, b, _TN, preferred_element_type=F32)


def _sigmoid(x):
    return 1.0 / (1.0 + jnp.exp(-x))


def _log_sigmoid(x):
    return -(jnp.maximum(-x, 0.0) + jnp.log1p(jnp.exp(-jnp.abs(x))))


def _expm1(x):
    poly = x * (1.0 + x * (0.5 + x * (1.0 / 6.0 + x * (1.0 / 24.0 + x * (1.0 / 120.0)))))
    return jnp.where(jnp.abs(x) < 0.1, poly, jnp.exp(x) - 1.0)


_GELU_K = 0.7978845608028654


def _gelu(x):
    return 0.5 * x * (1.0 + jnp.tanh(_GELU_K * (x + 0.044715 * (x * x * x))))


def _gelu_grad(x):
    t = jnp.tanh(_GELU_K * (x + 0.044715 * (x * x * x)))
    return 0.5 * (1.0 + t) + 0.5 * x * (1.0 - t * t) * (_GELU_K * (1.0 + 3 * 0.044715 * x * x))


def _rms_scale(x):
    return lax.rsqrt(jnp.mean(x * x, axis=-1, keepdims=True) + EPS)


def _norm_bwd(dh, x, g):
    rs = _rms_scale(x)
    xhat = x * rs
    dxhat = dh * g
    dx = rs * (dxhat - xhat * jnp.mean(dxhat * xhat, axis=-1, keepdims=True))
    return dx, jnp.sum(dh * xhat, axis=0, keepdims=True)


def _token_tile(S, want):
    tm = min(S, want)
    assert S % tm == 0
    return tm


def _norm_matmul(x, g, ws, name, tm=512):
    S, D = x.shape
    tm = _token_tile(S, tm)
    n = len(ws)

    def body(x_ref, g_ref, *refs):
        w_refs, o_refs, h_ref = refs[:n], refs[n:2 * n], refs[2 * n]
        xv = x_ref[...]
        h = (xv * _rms_scale(xv) * g_ref[...]).astype(BF16)
        h_ref[...] = h
        for w_ref, o_ref in zip(w_refs, o_refs):
            nb, _, nw = w_ref.shape
            for d in range(nb):
                o_ref[:, d * nw:(d + 1) * nw] = _dot(h, w_ref[d])

    widths = [w.shape[0] * w.shape[2] for w in ws]
    outs = pl.pallas_call(
        body, name=name, grid=(S // tm,),
        in_specs=[pl.BlockSpec((tm, D), lambda i: (i, 0)), _resident((1, D))]
        + [_resident(w.shape) for w in ws],
        out_specs=[pl.BlockSpec((tm, n_), lambda i: (i, 0)) for n_ in widths]
        + [pl.BlockSpec((tm, D), lambda i: (i, 0))],
        out_shape=[jax.ShapeDtypeStruct((S, n_), F32) for n_ in widths]
        + [jax.ShapeDtypeStruct((S, D), BF16)],
        compiler_params=_params("parallel"),
    )(x, g, *ws)
    return outs[:n], outs[n]


def _matmul_res(a, w, res, name, after, tm=512):
    S, K = a.shape
    N = w.shape[1]
    tm = _token_tile(S, tm)

    def body(a_ref, w_ref, r_ref, after_ref, o_ref):
        o_ref[...] = r_ref[...] + _dot(a_ref[...], w_ref[...])

    return pl.pallas_call(
        body, name=name, grid=(S // tm,),
        in_specs=[pl.BlockSpec((tm, K), lambda i: (i, 0)), _resident((K, N)),
                  pl.BlockSpec((tm, N), lambda i: (i, 0)), pl.BlockSpec(memory_space=pl.ANY)],
        out_specs=pl.BlockSpec((tm, N), lambda i: (i, 0)),
        out_shape=jax.ShapeDtypeStruct((S, N), F32),
        compiler_params=_params("parallel"),
    )(a, w, res, after)


def _matmul_nt(a, w, name, out_dtype, after, tm=1024):
    S, N = a.shape
    K = w.shape[0]
    tm = _token_tile(S, tm)

    def body(a_ref, w_ref, after_ref, o_ref):
        o_ref[...] = _dot_nt(a_ref[...].astype(BF16), w_ref[...]).astype(out_dtype)

    return pl.pallas_call(
        body, name=name, grid=(S // tm,),
        in_specs=[pl.BlockSpec((tm, N), lambda i: (i, 0)), _resident((K, N)),
                  pl.BlockSpec(memory_space=pl.ANY)],
        out_specs=pl.BlockSpec((tm, K), lambda i: (i, 0)),
        out_shape=jax.ShapeDtypeStruct((S, K), out_dtype),
        compiler_params=_params("parallel"),
    )(a, w, after)


def _proj_bwd(a_lists, w_list, x, g, res, name, tm=512):
    S, D = x.shape
    tm = _token_tile(S, tm)
    a_list = [a for group in a_lists for a in group]
    n, n_w = len(a_list), len(w_list)

    def body(*refs):
        a_refs, w_refs = list(refs[:n]), refs[n:n + n_w]
        x_ref, g_ref, r_ref, dx_ref, dg_ref = refs[n + n_w:]
        dh = jnp.zeros((tm, D), F32)
        for group, w_ref in zip(a_lists, w_refs):
            nw = w_ref.shape[2]
            d = 0
            for _ in group:
                a_ref = a_refs.pop(0)
                for j in range(a_ref.shape[1] // nw):
                    dh = dh + _dot_nt(a_ref[:, j * nw:(j + 1) * nw].astype(BF16), w_ref[d])
                    d += 1
        dx, dg = _norm_bwd(dh, x_ref[...], g_ref[...])
        dx_ref[...] = r_ref[...] + dx

        @pl.when(pl.program_id(0) == 0)
        def _():
            dg_ref[...] = jnp.zeros_like(dg_ref)
        dg_ref[...] += dg

    tok = lambda width: pl.BlockSpec((tm, width), lambda i: (i, 0))
    return pl.pallas_call(
        body, name=name, grid=(S // tm,),
        in_specs=[tok(a.shape[1]) for a in a_list] + [_resident(w.shape) for w in w_list]
        + [tok(D), _resident((1, D)), tok(D)],
        out_specs=[tok(D), pl.BlockSpec((1, D), lambda i: (0, 0))],
        out_shape=[jax.ShapeDtypeStruct((S, D), F32), jax.ShapeDtypeStruct((1, D), F32)],
        compiler_params=_params("arbitrary"),
    )(*a_list, *w_list, x, g, res)


def _matmul_tn(a, b, name, rows=1, cols=1, col_blocks=None, a_square=False, tm=2048):
    S, K = a.shape
    N = b.shape[1]
    tm = _token_tile(S, tm)
    n_tok = S // tm
    kr, nc = K // rows, N // cols

    def body(a_ref, b_ref, o_ref, acc_ref):
        av = a_ref[...]
        if a_square:
            av = av.astype(F32)
            av = av * av
        part = _dot_tn(av.astype(BF16), b_ref[...].astype(BF16))
        step = pl.program_id(2)

        @pl.when(step == 0)
        def _():
            acc_ref[...] = part

        @pl.when(step > 0)
        def _():
            acc_ref[...] += part

        @pl.when(step == n_tok - 1)
        def _():
            if col_blocks is None:
                o_ref[...] = acc_ref[...].astype(BF16)
            else:
                nw = N // col_blocks
                for d in range(col_blocks // cols):
                    o_ref[d] = acc_ref[:, d * nw:(d + 1) * nw].astype(BF16)

    if col_blocks is None:
        out_spec = pl.BlockSpec((kr, nc), lambda r, c, i: (r, c))
        out_shape = jax.ShapeDtypeStruct((K, N), BF16)
    else:
        assert rows == 1 and col_blocks % cols == 0
        per = col_blocks // cols
        out_spec = pl.BlockSpec((per, K, N // col_blocks), lambda r, c, i: (c, 0, 0))
        out_shape = jax.ShapeDtypeStruct((col_blocks, K, N // col_blocks), BF16)
    return pl.pallas_call(
        body, name=name, grid=(rows, cols, n_tok),
        in_specs=[pl.BlockSpec((tm, kr), lambda r, c, i: (i, r)),
                  pl.BlockSpec((tm, nc), lambda r, c, i: (i, c))],
        out_specs=out_spec, out_shape=out_shape,
        scratch_shapes=[pltpu.VMEM((kr, nc), F32)],
        compiler_params=_params("parallel", "parallel", "arbitrary"),
    )(a, b)


def _mlp_fwd(x, g, w1, w2, name, target=None, tm=512):
    S, D = x.shape
    nb, _, fb = w1.shape
    tm = _token_tile(S, tm)
    with_loss = target is not None

    def body(x_ref, g_ref, w1_ref, w2_ref, *refs):
        h_ref, r_ref = refs[-2:]
        xv = x_ref[...]
        h = (xv * _rms_scale(xv) * g_ref[...]).astype(BF16)
        h_ref[...] = h
        acc = xv
        for d in range(nb):
            r = jnp.maximum(_dot(h, w1_ref[d]), 0.0)
            r_ref[:, d * fb:(d + 1) * fb] = r.astype(BF16)
            acc = acc + _dot((r * r).astype(BF16), w2_ref[d])
        if not with_loss:
            refs[0][...] = acc
            return
        t_ref, loss_ref, dy_ref = refs[:3]
        err = acc - t_ref[...]
        dy_ref[...] = err / D

        @pl.when(pl.program_id(0) == 0)
        def _():
            loss_ref[...] = jnp.zeros_like(loss_ref)
        row_loss = jnp.mean(err * err, axis=1, keepdims=True)
        loss_ref[...] += 0.5 * jnp.sum(row_loss, axis=0, keepdims=True)

    tok = lambda width: pl.BlockSpec((tm, width), lambda i: (i, 0))
    saved_specs = [tok(D), tok(nb * fb)]
    saved_shapes = [jax.ShapeDtypeStruct((S, D), BF16), jax.ShapeDtypeStruct((S, nb * fb), BF16)]
    wide = jax.ShapeDtypeStruct((S, D), F32)
    if with_loss:
        head_specs = [pl.BlockSpec((1, 1), lambda i: (0, 0)), tok(D)]
        head_shapes = [jax.ShapeDtypeStruct((1, 1), F32), wide]
    else:
        head_specs, head_shapes = [tok(D)], [wide]
    return pl.pallas_call(
        body, name=name, grid=(S // tm,),
        in_specs=[tok(D), _resident((1, D)), _resident(w1.shape), _resident(w2.shape)]
        + ([tok(D)] if with_loss else []),
        out_specs=head_specs + saved_specs, out_shape=head_shapes + saved_shapes,
        compiler_params=_params("arbitrary" if with_loss else "parallel"),
    )(x, g, w1, w2, *([target] if with_loss else []))


def _mlp_bwd(dout, x, g, r, w1, w2, name, tm=512):
    S, D = x.shape
    nb, _, fb = w1.shape
    tm = _token_tile(S, tm)

    def body(do_ref, x_ref, g_ref, r_ref, w1_ref, w2_ref, dx_ref, dg_ref, da_ref):
        dov = do_ref[...]
        dob = dov.astype(BF16)
        dh = jnp.zeros((tm, D), F32)
        for d in range(nb):
            dz = _dot_nt(dob, w2_ref[d])
            da = (dz * (2.0 * r_ref[:, d * fb:(d + 1) * fb].astype(F32))).astype(BF16)
            da_ref[:, d * fb:(d + 1) * fb] = da
            dh = dh + _dot_nt(da, w1_ref[d])
        dx, dg = _norm_bwd(dh, x_ref[...], g_ref[...])
        dx_ref[...] = dov + dx

        @pl.when(pl.program_id(0) == 0)
        def _():
            dg_ref[...] = jnp.zeros_like(dg_ref)
        dg_ref[...] += dg

    tok = lambda width: pl.BlockSpec((tm, width), lambda i: (i, 0))
    return pl.pallas_call(
        body, name=name, grid=(S // tm,),
        in_specs=[tok(D), tok(D), _resident((1, D)), tok(nb * fb), _resident(w1.shape),
                  _resident(w2.shape)],
        out_specs=[tok(D), pl.BlockSpec((1, D), lambda i: (0, 0)), tok(nb * fb)],
        out_shape=[jax.ShapeDtypeStruct((S, D), F32), jax.ShapeDtypeStruct((1, D), F32),
                   jax.ShapeDtypeStruct((S, nb * fb), BF16)],
        compiler_params=_params("arbitrary"),
    )(dout, x, g, r, w1, w2)


def _scan_chunk(a, b, row, T, reverse):
    s = 1
    while s < T:
        if reverse:
            keep, shift = row < T - s, T - s
        else:
            keep, shift = row >= s, s
        a_sh = jnp.where(keep, pltpu.roll(a, shift, 0), 1.0)
        b_sh = jnp.where(keep, pltpu.roll(b, shift, 0), 0.0)
        b = a * b_sh + b
        a = a * a_sh
        s *= 2
    return a, b


def _row_of(x, row, r):
    return jnp.sum(jnp.where(row == r, x, 0.0), axis=0, keepdims=True)


def _shift_down(x, prev, row, k):
    if k == 0:
        return x
    return jnp.where(row < k, pltpu.roll(prev, k, 0), pltpu.roll(x, k, 0))


def _shift_up(x, nxt, row, k, T):
    if k == 0:
        return x
    return jnp.where(row < T - k, pltpu.roll(x, T - k, 0), pltpu.roll(nxt, T - k, 0))


def _lru_gates(xb, prev_xb, row, cw_ref, cb, wr, br, wi, bi, ls):
    xc = cb + cw_ref[pl.ds(0, 1), :] * _shift_down(xb, prev_xb, row, 3)
    for k in (2, 1, 0):
        xc = xc + cw_ref[pl.ds(3 - k, 1), :] * _shift_down(xb, prev_xb, row, k)
    xcb = xc.astype(BF16)
    r = _sigmoid(_dot(xcb, wr) + br)
    i = _sigmoid(_dot(xcb, wi) + bi)
    la = (LRU_C * r) * ls
    a = jnp.exp(la)
    m = jnp.sqrt(-_expm1(2.0 * la))
    return xc, xcb, r, i, a, m


def _lru_specs(S):
    col = lambda off: pl.BlockSpec((S, LANES), lambda j: (0, j + off))
    vec = pl.BlockSpec((1, LANES), lambda j: (0, j))
    mat = pl.BlockSpec((None, LANES, LANES), lambda j: (j, 0, 0))
    cwm = pl.BlockSpec((CONV_WIDTH, LANES), lambda j: (0, j))
    return col, vec, mat, cwm


def _lru_fwd(u, conv_w, conv_b, wr, br, wi, bi, lam, name):
    S = u.shape[0]
    T = _token_tile(S, 512)
    col, vec, mat, cwm = _lru_specs(S)

    def body(gp_ref, xb_ref, cw_ref, cb_ref, wr_ref, br_ref, wi_ref, bi_ref, lam_ref,
             y_ref, hs_ref):
        row = lax.broadcasted_iota(jnp.int32, (T, LANES), 0)
        ls = _log_sigmoid(lam_ref[...])
        cb, br, bi = cb_ref[...], br_ref[...], bi_ref[...]
        wr, wi = wr_ref[...], wi_ref[...]

        def chunk(ci, carry):
            prev_xb, hc = carry
            rows = pl.ds(pl.multiple_of(ci * T, T), T)
            xb = xb_ref[rows, :]
            xc, _, _, i, a, m = _lru_gates(xb, prev_xb, row, cw_ref, cb, wr, br, wi, bi, ls)
            ca, cbv = _scan_chunk(a, m * (i * xc), row, T, reverse=False)
            h = ca * hc + cbv
            hs_ref[rows, :] = h
            y_ref[rows, :] = (_gelu(gp_ref[rows, :]) * h).astype(BF16)
            return xb, _row_of(h, row, T - 1)

        lax.fori_loop(0, S // T, chunk,
                      (jnp.zeros((T, LANES), F32), jnp.zeros((1, LANES), F32)))

    return pl.pallas_call(
        body, name=name, grid=(N_CBLK,),
        in_specs=[col(0), col(N_CBLK), cwm, vec, mat, vec, mat, vec, vec],
        out_specs=[col(0), col(0)],
        out_shape=[jax.ShapeDtypeStruct((S, D_MODEL), BF16), jax.ShapeDtypeStruct((S, D_MODEL), F32)],
        compiler_params=_params("parallel"),
    )(u, u, conv_w, conv_b, wr, br, wi, bi, lam)


def _lru_bwd(dy, u, hs, conv_w, conv_b, wr, br, wi, bi, lam, name):
    S = u.shape[0]
    T = _token_tile(S, 512)
    n_chunk = S // T
    col, vec, mat, cwm = _lru_specs(S)

    def body(dy_ref, gp_ref, xb_ref, hs_ref, cw_ref, cb_ref, wr_ref, br_ref, wi_ref, bi_ref,
             lam_ref, dgp_ref, dxb_ref, dcw_ref, dcb_ref, dbr_ref, dbi_ref, dlam_ref, dwr_ref,
             dwi_ref):
        row = lax.broadcasted_iota(jnp.int32, (T, LANES), 0)
        lam = lam_ref[...]
        ls = _log_sigmoid(lam)
        cb, br, bi = cb_ref[...], br_ref[...], bi_ref[...]
        wr, wi = wr_ref[...], wi_ref[...]
        for ref in (dcw_ref, dcb_ref, dbr_ref, dbi_ref, dlam_ref, dwr_ref, dwi_ref):
            ref[...] = jnp.zeros_like(ref)

        def chunk(it, carry):
            g_next, dxc_next = carry
            ci = n_chunk - 1 - it
            rows = pl.ds(pl.multiple_of(ci * T, T), T)
            before = pl.ds(pl.multiple_of(jnp.maximum(ci - 1, 0) * T, T), T)
            first = ci == 0
            xb = xb_ref[rows, :]
            prev_xb = jnp.where(first, 0.0, xb_ref[before, :])
            xc, xcb, r, i, a, m = _lru_gates(xb, prev_xb, row, cw_ref, cb, wr, br, wi, bi, ls)
            h = hs_ref[rows, :]
            h_prev = _shift_down(h, jnp.where(first, 0.0, hs_ref[before, :]), row, 1)
            gp = gp_ref[rows, :]
            dyv = dy_ref[rows, :]
            dgp_ref[rows, :] = (dyv * h * _gelu_grad(gp)).astype(BF16)
            dh = dyv * _gelu(gp)
            ca, cbv = _scan_chunk(a, a * dh, row, T, reverse=True)
            gp_acc = ca * g_next + cbv
            g = dh + jnp.where(row < T - 1, pltpu.roll(gp_acc, T - 1, 0), g_next)
            da = g * h_prev - (g * (i * xc)) * a / m
            dla = da * a
            dlam_ref[...] += jnp.sum(dla * (LRU_C * r), axis=0, keepdims=True)
            dpr = (dla * (LRU_C * ls)) * r * (1.0 - r)
            dpi = (g * m * xc) * i * (1.0 - i)
            dbr_ref[...] += jnp.sum(dpr, axis=0, keepdims=True)
            dbi_ref[...] += jnp.sum(dpi, axis=0, keepdims=True)
            dprb, dpib = dpr.astype(BF16), dpi.astype(BF16)
            dwr_ref[...] += _dot_tn(xcb, dprb)
            dwi_ref[...] += _dot_tn(xcb, dpib)
            dxc = g * m * i + _dot_nt(dprb, wr) + _dot_nt(dpib, wi)
            dcb_ref[...] += jnp.sum(dxc, axis=0, keepdims=True)
            dxb = jnp.zeros((T, LANES), F32)
            for k in range(CONV_WIDTH):
                tap = pl.ds(CONV_WIDTH - 1 - k, 1)
                dcw_ref[tap, :] += jnp.sum(dxc * _shift_down(xb, prev_xb, row, k), axis=0,
                                           keepdims=True)
                dxb = dxb + cw_ref[tap, :] * _shift_up(dxc, dxc_next, row, k, T)
            dxb_ref[rows, :] = dxb.astype(BF16)
            return _row_of(gp_acc, row, 0), dxc

        lax.fori_loop(0, n_chunk, chunk,
                      (jnp.zeros((1, LANES), F32), jnp.zeros((T, LANES), F32)))
        dlam_ref[...] = dlam_ref[...] * _sigmoid(-lam)

    vec_out = jax.ShapeDtypeStruct((1, D_MODEL), F32)
    mat_out = jax.ShapeDtypeStruct((N_CBLK, LANES, LANES), F32)
    return pl.pallas_call(
        body, name=name, grid=(N_CBLK,),
        in_specs=[col(0), col(0), col(N_CBLK), col(0), cwm, vec, mat, vec, mat, vec, vec],
        out_specs=[col(0), col(0), cwm, vec, vec, vec, vec, mat, mat],
        out_shape=[jax.ShapeDtypeStruct((S, D_MODEL), BF16), jax.ShapeDtypeStruct((S, D_MODEL), BF16),
                   jax.ShapeDtypeStruct((CONV_WIDTH, D_MODEL), F32),
                   vec_out, vec_out, vec_out, vec_out, mat_out, mat_out],
        compiler_params=_params("parallel"),
    )(dy, u, u, hs, conv_w, conv_b, wr, br, wi, bi, lam)


def _head_group_matrix(value):
    r = lax.broadcasted_iota(jnp.int32, (LANES, LANES), 0) // HEAD_DIM
    c = lax.broadcasted_iota(jnp.int32, (LANES, LANES), 1) // HEAD_DIM
    return jnp.where(r == c, value, 0.0).astype(BF16)


def _group_dot(x, p):
    hi = x.astype(BF16)
    lo = (x - hi.astype(F32)).astype(BF16)
    return _dot(hi, p) + _dot(lo, p)


def _head_mean(x, p):
    return _group_dot(x, p)


def _qk_prep(u, q_gain, k_gain, name, tm=512):
    S = u.shape[0]
    tm = _token_tile(S, tm)

    def body(q_ref, k_ref, v_ref, qg_ref, kg_ref, qn_ref, kn_ref, vb_ref):
        p = _head_group_matrix(1.0 / HEAD_DIM)
        for j in range(N_CBLK):
            cl = slice(j * LANES, (j + 1) * LANES)
            for x_ref, g_ref, o_ref, scale in ((q_ref, qg_ref, qn_ref, ATTN_SCALE),
                                               (k_ref, kg_ref, kn_ref, 1.0)):
                xv = x_ref[:, cl]
                rs = lax.rsqrt(_head_mean(xv * xv, p) + EPS)
                o_ref[:, cl] = (xv * rs * g_ref[...]).astype(BF16) * scale
        vb_ref[...] = v_ref[...].astype(BF16)

    blk = lambda off: pl.BlockSpec((tm, D_MODEL), lambda i: (i, off))
    out = jax.ShapeDtypeStruct((S, D_MODEL), BF16)
    return pl.pallas_call(
        body, name=name, grid=(S // tm,),
        in_specs=[blk(0), blk(1), blk(2), _resident((1, LANES)), _resident((1, LANES))],
        out_specs=[blk(0), blk(0), blk(0)],
        out_shape=[out, out, out],
        compiler_params=_params("parallel"),
    )(u, u, u, q_gain, k_gain)


def _qk_bwd(u, dqn, dkn, q_gain, k_gain, name, tm=512):
    S = u.shape[0]
    tm = _token_tile(S, tm)

    def body(q_ref, k_ref, dqn_ref, dkn_ref, qg_ref, kg_ref, dq_ref, dk_ref, dqg_ref, dkg_ref):
        p = _head_group_matrix(1.0 / HEAD_DIM)
        for x_ref, dn_ref, g_ref, dx_ref, dg_ref, scale in (
                (q_ref, dqn_ref, qg_ref, dq_ref, dqg_ref, ATTN_SCALE),
                (k_ref, dkn_ref, kg_ref, dk_ref, dkg_ref, 1.0)):
            dg = jnp.zeros((1, LANES), F32)
            for j in range(N_CBLK):
                cl = slice(j * LANES, (j + 1) * LANES)
                xv, dn = x_ref[:, cl], dn_ref[:, cl] * scale
                rs = lax.rsqrt(_head_mean(xv * xv, p) + EPS)
                xhat = xv * rs
                dxhat = dn * g_ref[...]
                dx_ref[:, cl] = (rs * (dxhat - xhat * _head_mean(dxhat * xhat, p))).astype(BF16)
                dg = dg + jnp.sum(dn * xhat, axis=0, keepdims=True)

            @pl.when(pl.program_id(0) == 0)
            def _():
                dg_ref[...] = jnp.zeros_like(dg_ref)
            dg_ref[...] += dg

            @pl.when(pl.program_id(0) == S // tm - 1)
            def _():
                dg_ref[...] += pltpu.roll(dg_ref[...], HEAD_DIM, 1)

    blk = lambda off: pl.BlockSpec((tm, D_MODEL), lambda i: (i, off))
    acc = pl.BlockSpec((1, LANES), lambda i: (0, 0))
    out = jax.ShapeDtypeStruct((S, D_MODEL), BF16)
    vec = jax.ShapeDtypeStruct((1, LANES), F32)
    return pl.pallas_call(
        body, name=name, grid=(S // tm,),
        in_specs=[blk(0), blk(1), blk(0), blk(0), _resident((1, LANES)), _resident((1, LANES))],
        out_specs=[blk(0), blk(0), acc, acc],
        out_shape=[out, out, vec, vec],
        compiler_params=_params("arbitrary"),
    )(u, u, dqn, dkn, q_gain, k_gain)


def _forget_fwd(f, b_f, name):
    S = f.shape[0]
    T = _token_tile(S, 256)

    def body(f_ref, b_ref, c_ref):
        row = lax.broadcasted_iota(jnp.int32, (T, LANES), 0)
        ones = jnp.ones((T, LANES), F32)
        bias = b_ref[...]

        def chunk(ci, carry):
            rows = pl.ds(pl.multiple_of(ci * T, T), T)
            _, c = _scan_chunk(ones, _log_sigmoid(f_ref[rows, :] + bias), row, T, reverse=False)
            c = c + carry
            c_ref[rows, :] = c
            return _row_of(c, row, T - 1)

        lax.fori_loop(0, S // T, chunk, jnp.zeros((1, LANES), F32))

    return pl.pallas_call(
        body, name=name,
        in_specs=[pl.BlockSpec(memory_space=pltpu.VMEM)] * 2,
        out_specs=pl.BlockSpec(memory_space=pltpu.VMEM),
        out_shape=jax.ShapeDtypeStruct((S, LANES), F32),
        compiler_params=pltpu.CompilerParams(vmem_limit_bytes=VMEM_LIMIT),
    )(f, b_f)


def _forget_bwd(dc_k, rho, f, b_f, name):
    S = f.shape[0]
    T = _token_tile(S, 256)
    n_chunk = S // T

    def body(dck_ref, rho_ref, f_ref, b_ref, df_ref, db_ref):
        row = lax.broadcasted_iota(jnp.int32, (T, LANES), 0)
        ones = jnp.ones((T, LANES), F32)
        bias = b_ref[...]
        pick = (lax.broadcasted_iota(jnp.int32, (D_MODEL, LANES), 0)
                == HEAD_DIM * lax.broadcasted_iota(jnp.int32, (D_MODEL, LANES), 1))
        pick = jnp.where(pick, 1.0, 0.0).astype(BF16)

        def chunk(it, carry):
            tail, db = carry
            rows = pl.ds(pl.multiple_of((n_chunk - 1 - it) * T, T), T)
            dc = dck_ref[rows, :] + _group_dot(rho_ref[rows, :], pick)
            _, dlf = _scan_chunk(ones, dc, row, T, reverse=True)
            dlf = dlf + tail
            df = dlf * _sigmoid(-(f_ref[rows, :] + bias))
            df_ref[rows, :] = df
            return _row_of(dlf, row, 0), db + jnp.sum(df, axis=0, keepdims=True)

        zero = jnp.zeros((1, LANES), F32)
        _, db = lax.fori_loop(0, n_chunk, chunk, (zero, zero))
        db_ref[...] = db

    return pl.pallas_call(
        body, name=name,
        in_specs=[pl.BlockSpec(memory_space=pltpu.VMEM)] * 4,
        out_specs=[pl.BlockSpec(memory_space=pltpu.VMEM)] * 2,
        out_shape=[jax.ShapeDtypeStruct((S, LANES), F32), jax.ShapeDtypeStruct((1, LANES), F32)],
        compiler_params=pltpu.CompilerParams(vmem_limit_bytes=VMEM_LIMIT),
    )(dc_k, rho, f, b_f)


ATTN_TILE = 512
ATTN_ROWS_FWD = 32


def _attn_tiles(S):
    t = _token_tile(S, ATTN_TILE)
    return t, S // t


def _causal(T):
    return (lax.broadcasted_iota(jnp.int32, (T, T), 1)
            <= lax.broadcasted_iota(jnp.int32, (T, T), 0))


def _attn_fwd(qs_, kn, vb, c_row, name):
    S = qs_.shape[0]
    T, n_t = _attn_tiles(S)
    RB = min(T, ATTN_ROWS_FWD)

    def body(q_ref, k_ref, v_ref, cr_ref, o_ref, lse_ref, sa_ref, sb_ref, p_ref, m_ref, l_ref,
             acc_ref, a_ref):
        qi = pl.program_id(1)
        lanes = [slice(h2 * HEAD_DIM, (h2 + 1) * HEAD_DIM) for h2 in range(2)]
        col = lax.broadcasted_iota(jnp.int32, (RB, T), 1)
        row = lax.broadcasted_iota(jnp.int32, (RB, T), 0)
        m_ref[...] = jnp.full(m_ref.shape, NEG_INF, F32)
        l_ref[...] = jnp.zeros_like(l_ref)
        acc_ref[...] = jnp.zeros_like(acc_ref)

        def logits_into(s_ref, kj):
            ks = pl.ds(pl.multiple_of(kj * T, T), T)
            for h2, hl in enumerate(lanes):
                s_ref[h2] = _dot_nt(q_ref[:, hl], k_ref[ks, hl]) - cr_ref[h2:h2 + 1, ks]

        def consume(s_ref, kj, masked):
            ks = pl.ds(pl.multiple_of(kj * T, T), T)
            for h2, hl in enumerate(lanes):
                blocks = [slice(i * RB, (i + 1) * RB) for i in range(T // RB)]

                def logits(i, rows):
                    s = s_ref[h2, rows, :]
                    return jnp.where(col <= row + i * RB, s, NEG_INF) if masked else s

                wide = lambda x: jnp.broadcast_to(x, (RB, LANES))
                for i, rows in enumerate(blocks):
                    mx = wide(jnp.max(logits(i, rows), axis=1, keepdims=True))
                    a_ref[h2, rows, :] = m_ref[h2, rows, :]
                    m_ref[h2, rows, :] = jnp.maximum(m_ref[h2, rows, :], mx)
                for i, rows in enumerate(blocks):
                    m_new = m_ref[h2, rows, :]
                    p = jnp.exp(logits(i, rows) - jnp.tile(m_new, (1, T // LANES)))
                    alpha = jnp.exp(a_ref[h2, rows, :] - m_new)
                    a_ref[h2, rows, :] = alpha
                    l_ref[h2, rows, :] = (alpha * l_ref[h2, rows, :]
                                          + wide(jnp.sum(p, axis=1, keepdims=True)))
                    p_ref[h2, rows, :] = p.astype(BF16)
                acc_ref[h2] = (a_ref[h2, :, :HEAD_DIM] * acc_ref[h2]
                               + _dot(p_ref[h2], v_ref[ks, hl]))

        logits_into(sa_ref, 0)

        def pair(i, _):
            logits_into(sb_ref, 2 * i + 1)
            consume(sa_ref, 2 * i, False)
            logits_into(sa_ref, 2 * i + 2)
            consume(sb_ref, 2 * i + 1, False)
            return 0

        lax.fori_loop(0, qi // 2, pair, 0)

        @pl.when(qi % 2 == 1)
        def _():
            logits_into(sb_ref, qi)
            consume(sa_ref, qi - 1, False)
            consume(sb_ref, qi, True)

        @pl.when(qi % 2 == 0)
        def _():
            consume(sa_ref, qi, True)

        for h2, hl in enumerate(lanes):
            o_ref[:, hl] = (acc_ref[h2] / l_ref[h2, :, :HEAD_DIM]).astype(BF16)
            lse_ref[:, hl] = m_ref[h2, :, :HEAD_DIM] + jnp.log(l_ref[h2, :, :HEAD_DIM])

    qblk = pl.BlockSpec((T, LANES), lambda h, i: (i, h))
    kv = pl.BlockSpec((S, LANES), lambda h, i: (0, h))
    return pl.pallas_call(
        body, name=name, grid=(N_CBLK, n_t),
        in_specs=[qblk, kv, kv, pl.BlockSpec((None, 2, S), lambda h, i: (h, 0, 0))],
        out_specs=[qblk, qblk],
        out_shape=[jax.ShapeDtypeStruct((S, D_MODEL), BF16),
                   jax.ShapeDtypeStruct((S, D_MODEL), F32)],
        scratch_shapes=[pltpu.VMEM((2, T, T), F32), pltpu.VMEM((2, T, T), F32),
                        pltpu.VMEM((2, T, T), BF16),
                        pltpu.VMEM((2, T, LANES), F32), pltpu.VMEM((2, T, LANES), F32),
                        pltpu.VMEM((2, T, HEAD_DIM), F32), pltpu.VMEM((2, T, LANES), F32)],
        compiler_params=_params("parallel", "parallel"),
    )(qs_, kn, vb, c_row)


def _attn_bwd(qs_, kn, vb, do, o, lse, c_row, name):
    S = qs_.shape[0]
    T, n_t = _attn_tiles(S)

    def body(q_ref, k_ref, v_ref, do_ref, o_ref, lse_ref, cr_ref,
             dq_ref, dk_ref, dv_ref, dc_ref, rho_ref, dd_ref):
        kj = pl.program_id(1)
        causal = _causal(T)
        lanes = [slice(h2 * HEAD_DIM, (h2 + 1) * HEAD_DIM) for h2 in range(2)]
        ones = [slice(h2 * HEAD_DIM, h2 * HEAD_DIM + 1) for h2 in range(2)]

        @pl.when(kj == 0)
        def _():
            dq_ref[...] = jnp.zeros_like(dq_ref)
            rho_ref[...] = jnp.zeros_like(rho_ref)
            p_sum = _head_group_matrix(1.0)

            def fill(ci, _):
                rows = pl.ds(pl.multiple_of(ci * T, T), T)
                dd_ref[rows, :] = _group_dot(do_ref[rows, :].astype(F32) * o_ref[rows, :].astype(F32),
                                             p_sum)
                return 0

            lax.fori_loop(0, n_t, fill, 0)

        kh = [k_ref[:, hl] for hl in lanes]
        vh = [v_ref[:, hl] for hl in lanes]
        ck = [cr_ref[h2:h2 + 1, :] for h2 in range(2)]

        def step(qi, carry, masked):
            qs = pl.ds(pl.multiple_of(qi * T, T), T)
            out = []
            for h2, hl in enumerate(lanes):
                dk, dv, dc = carry[h2]
                qh, doh = q_ref[qs, hl], do_ref[qs, hl]
                s = _dot_nt(qh, kh[h2]) - ck[h2]
                if masked:
                    s = jnp.where(causal, s, NEG_INF)
                p = jnp.exp(s - lse_ref[qs, ones[h2]])
                ds = p * (_dot_nt(doh, vh[h2]) - dd_ref[qs, ones[h2]])
                dsb = ds.astype(BF16)
                dq_ref[qs, hl] += _dot(dsb, kh[h2])
                rho_ref[qs, hl] += jnp.broadcast_to(jnp.sum(ds, axis=1, keepdims=True),
                                                    (T, HEAD_DIM))
                out.append((dk + _dot_tn(dsb, qh), dv + _dot_tn(p.astype(BF16), doh),
                            dc - jnp.sum(ds, axis=0, keepdims=True)))
            return tuple(out)

        init = tuple((jnp.zeros((T, HEAD_DIM), F32), jnp.zeros((T, HEAD_DIM), F32),
                      jnp.zeros((1, T), F32)) for _ in lanes)
        carry = step(kj, init, True)
        carry = lax.fori_loop(kj + 1, n_t, lambda qi, c: step(qi, c, False), carry)
        for h2, ((dk, dv, dc), hl) in enumerate(zip(carry, lanes)):
            dk_ref[:, hl] = dk
            dv_ref[:, hl] = dv.astype(BF16)
            dc_ref[h2:h2 + 1, :] = dc

    kblk = pl.BlockSpec((T, LANES), lambda h, j: (j, h))
    full = pl.BlockSpec((S, LANES), lambda h, j: (0, h))
    crow = pl.BlockSpec((None, 2, T), lambda h, j: (h, 0, j))
    wide = jax.ShapeDtypeStruct((S, D_MODEL), F32)
    return pl.pallas_call(
        body, name=name, grid=(N_CBLK, n_t),
        in_specs=[full, kblk, kblk, full, full, full, crow],
        out_specs=[full, kblk, kblk, crow, full],
        out_shape=[wide, wide, jax.ShapeDtypeStruct((S, D_MODEL), BF16),
                   jax.ShapeDtypeStruct((N_CBLK, 2, S), F32), wide],
        scratch_shapes=[pltpu.VMEM((S, LANES), F32)],
        compiler_params=_params("parallel", "arbitrary"),
    )(qs_, kn, vb, do, o, lse, c_row)


ALL_PEERS = tuple(range(1, N_DEV))
NEAR_PEERS = (1, 2, 4, 6)
FAR_CHIPS = (2, 4, 6)


def _landing_shapes(arrays, gathers):
    return [jax.ShapeDtypeStruct((N_DEV,) + a.shape if g else a.shape, a.dtype)
            for a, g in zip(arrays, gathers)]


def _my_index():
    return 4 * lax.axis_index("x") + 2 * lax.axis_index("y") + lax.axis_index("c")


def _own_copies(srcs, lands, gathers, sems):
    me = _my_index()
    return [pltpu.make_async_copy(src if g else src.at[me], land.at[me], sems.at[a])
            for a, (src, land, g) in enumerate(zip(srcs, lands, gathers))]


def _peer_copies(srcs, lands, gathers, send_sems, recv_sems, ks=ALL_PEERS):
    x, y, c = lax.axis_index("x"), lax.axis_index("y"), lax.axis_index("c")
    me = 4 * x + 2 * y + c
    out = []
    for j, k in enumerate(ks):
        to = (1 - x if k & 4 else x, 1 - y if k & 2 else y, 1 - c if k & 1 else c)
        peer = 4 * to[0] + 2 * to[1] + to[2]
        for a, (src, land, g) in enumerate(zip(srcs, lands, gathers)):
            sem = a * len(ks) + j
            src_blk = src if g else src.at[peer]

            def copy(slot, src_blk=src_blk, land=land, sem=sem, to=to):
                return pltpu.make_async_remote_copy(
                    src_ref=src_blk, dst_ref=land.at[slot], send_sem=send_sems.at[sem],
                    recv_sem=recv_sems.at[sem], device_id=to,
                    device_id_type=pl.DeviceIdType.MESH)

            out.append((k, a, copy(me), copy(peer)))
    return out


def _forward_copies(lands, send_sems, recv_sems):
    x, y, c = lax.axis_index("x"), lax.axis_index("y"), lax.axis_index("c")
    out = []
    for j, f in enumerate(FAR_CHIPS):
        chip = 4 * (1 - x if f & 4 else x) + 2 * (1 - y if f & 2 else y)
        for a, land in enumerate(lands):
            sem = a * len(FAR_CHIPS) + j

            def copy(slot, land=land, sem=sem):
                return pltpu.make_async_remote_copy(
                    src_ref=land.at[slot], dst_ref=land.at[slot], send_sem=send_sems.at[sem],
                    recv_sem=recv_sems.at[sem], device_id=(x, y, 1 - c),
                    device_id_type=pl.DeviceIdType.MESH)

            out.append((f, a, copy(chip + c), copy(chip + 1 - c)))
    return out


def _exchange(arrays, gathers, name, two_level=False):
    n = len(arrays)
    ks = NEAR_PEERS if two_level else ALL_PEERS
    assert not two_level or all(gathers)

    def body(*refs):
        ins, outs = refs[:n], refs[n:2 * n]
        send_sems, recv_sems, own_sems, fwd_send_sems, fwd_recv_sems = refs[2 * n:]
        own = _own_copies(ins, outs, gathers, own_sems)
        for cp in own:
            cp.start()
        copies = _peer_copies(ins, outs, gathers, send_sems, recv_sems, ks)
        for _, _, send, _ in copies:
            send.start()
        passed = {}
        if two_level:
            passed = {(f, a): (send, arrival)
                      for f, a, send, arrival in _forward_copies(outs, fwd_send_sems, fwd_recv_sems)}
        for k, a, _, arrival in copies:
            arrival.wait_recv()
            if (k, a) in passed:
                passed[k, a][0].start()
        for send, arrival in passed.values():
            arrival.wait_recv()
            send.wait_send()
        for _, _, send, _ in copies:
            send.wait_send()
        for cp in own:
            cp.wait()

    hbm = pl.BlockSpec(memory_space=pl.ANY)
    return pl.pallas_call(
        body, name=name,
        in_specs=[hbm] * n, out_specs=[hbm] * n, out_shape=_landing_shapes(arrays, gathers),
        scratch_shapes=[pltpu.SemaphoreType.DMA((n * len(ks),)),
                        pltpu.SemaphoreType.DMA((n * len(ks),)),
                        pltpu.SemaphoreType.DMA((n,)),
                        pltpu.SemaphoreType.DMA((n * len(FAR_CHIPS),)),
                        pltpu.SemaphoreType.DMA((n * len(FAR_CHIPS),))],
        compiler_params=pltpu.CompilerParams(has_side_effects=True),
    )(*arrays)


_HBM = pl.BlockSpec(memory_space=pltpu.HBM)
_SEM = pl.BlockSpec(memory_space=pltpu.SEMAPHORE)
_ANY = pl.BlockSpec(memory_space=pl.ANY)
_DATAFLOW = pltpu.SideEffectType.DATAFLOW_SIDE_EFFECTING


def _in_hbm(a):
    return pltpu.with_memory_space_constraint(a, pltpu.HBM)


def _exchange_start(arrays, gathers, after, name, ks=ALL_PEERS):
    n = len(arrays)
    lands = [lax.empty(s.shape, s.dtype) for s in _landing_shapes(arrays, gathers)]

    def body(*refs):
        srcs, dsts = refs[:n], refs[n:2 * n]
        send_sems, recv_sems, own_sems = refs[2 * n + 1:2 * n + 4]
        token = refs[-1]
        for cp in _own_copies(srcs, dsts, gathers, own_sems):
            cp.start()
        for _, _, send, _ in _peer_copies(srcs, dsts, gathers, send_sems, recv_sems, ks):
            send.start()
        token[...] = jnp.zeros_like(token)

    hbm_like = [pltpu.HBM(a.shape, a.dtype) for a in list(arrays) + lands]
    res = pl.pallas_call(
        body, name=name,
        in_specs=[_HBM] * (2 * n) + [_ANY],
        out_specs=(_SEM, _SEM, _SEM, *[_HBM] * (2 * n), pl.BlockSpec(memory_space=pltpu.VMEM)),
        out_shape=(pltpu.SemaphoreType.DMA((n * len(ks),)), pltpu.SemaphoreType.DMA((n * len(ks),)),
                   pltpu.SemaphoreType.DMA((n,)), *hbm_like,
                   jax.ShapeDtypeStruct((8, LANES), F32)),
        input_output_aliases={i: 3 + i for i in range(2 * n)},
        compiler_params=pltpu.CompilerParams(has_side_effects=_DATAFLOW),
    )(*[_in_hbm(a) for a in list(arrays) + lands], after)
    return (res[0], res[1], res[2], res[3:3 + n], res[3 + n:3 + 2 * n]), res[-1]


def _exchange_wait(started, gathers, after, name, ks=ALL_PEERS):
    send_sems, recv_sems, own_sems, arrays, lands = started
    n = len(arrays)

    def body(*refs):
        srcs, dsts = refs[:n], refs[n:2 * n]
        for _, _, send, arrival in _peer_copies(srcs, dsts, gathers, refs[2 * n], refs[2 * n + 1],
                                                ks):
            arrival.wait_recv()
            send.wait_send()
        for cp in _own_copies(srcs, dsts, gathers, refs[2 * n + 2]):
            cp.wait()

    hbm_like = [pltpu.HBM(a.shape, a.dtype) for a in list(arrays) + list(lands)]
    res = pl.pallas_call(
        body, name=name,
        in_specs=[_HBM] * (2 * n) + [_SEM, _SEM, _SEM, _ANY],
        out_specs=[_HBM] * (2 * n), out_shape=hbm_like,
        input_output_aliases={i: i for i in range(2 * n)},
        compiler_params=pltpu.CompilerParams(has_side_effects=_DATAFLOW),
    )(*arrays, *lands, send_sems, recv_sems, own_sems, after)
    return res[n:]


def _forward_start(lands, after, name):
    n = len(lands)

    def body(*refs):
        send_sems, recv_sems = refs[n + 1:n + 3]
        for _, _, send, _ in _forward_copies(refs[:n], send_sems, recv_sems):
            send.start()
        refs[-1][...] = jnp.zeros_like(refs[-1])

    n_sem = n * len(FAR_CHIPS)
    res = pl.pallas_call(
        body, name=name,
        in_specs=[_HBM] * n + [_ANY],
        out_specs=(_SEM, _SEM, *[_HBM] * n, pl.BlockSpec(memory_space=pltpu.VMEM)),
        out_shape=(pltpu.SemaphoreType.DMA((n_sem,)), pltpu.SemaphoreType.DMA((n_sem,)),
                   *[pltpu.HBM(a.shape, a.dtype) for a in lands],
                   jax.ShapeDtypeStruct((8, LANES), F32)),
        input_output_aliases={i: 2 + i for i in range(n)},
        compiler_params=pltpu.CompilerParams(has_side_effects=_DATAFLOW),
    )(*[_in_hbm(a) for a in lands], after)
    return (res[0], res[1], res[2:2 + n]), res[-1]


def _forward_wait(started, after, name):
    send_sems, recv_sems, lands = started
    n = len(lands)

    def body(*refs):
        for _, _, send, arrival in _forward_copies(refs[:n], refs[n], refs[n + 1]):
            arrival.wait_recv()
            send.wait_send()

    return pl.pallas_call(
        body, name=name,
        in_specs=[_HBM] * n + [_SEM, _SEM, _ANY],
        out_specs=[_HBM] * n, out_shape=[pltpu.HBM(a.shape, a.dtype) for a in lands],
        input_output_aliases={i: i for i in range(n)},
        compiler_params=pltpu.CompilerParams(has_side_effects=_DATAFLOW),
    )(*lands, send_sems, recv_sems, after)


def _reduce_adamw(parts, w, m, v, name):
    n_layer = len(parts)
    n, R, C = parts[0].shape
    tr = 256 if R % 256 == 0 else R
    n_t = R // tr

    def body(*refs):
        p_refs = refs[:n_layer]
        w_ref, m_ref, v_ref, g_ref, d_ref, nm_ref, nv_ref = refs[n_layer:]

        def update(p_ref):
            g = p_ref[0].astype(F32)
            for s in range(1, n):
                g = g + p_ref[s].astype(F32)
            g_ref[...] = g
            m_new = ADAM_B1 * m_ref[...] + (1.0 - ADAM_B1) * g
            v_new = ADAM_B2 * v_ref[...] + (1.0 - ADAM_B2) * (g * g)
            nm_ref[...] = m_new
            nv_ref[...] = v_new
            m_hat = m_new / (1.0 - ADAM_B1 ** ADAM_STEP)
            v_hat = v_new / (1.0 - ADAM_B2 ** ADAM_STEP)
            d_ref[...] = -ADAM_LR * (m_hat / (jnp.sqrt(v_hat) + ADAM_EPS) + ADAM_WD * w_ref[...])

        for layer, p_ref in enumerate(p_refs):
            pl.when(pl.program_id(0) == layer)(functools.partial(update, p_ref))

    def parts_spec(layer):
        def index(l, i):
            return 0, jnp.where(l < layer, 0, jnp.where(l > layer, n_t - 1, i)), 0
        return pl.BlockSpec((n, tr, C), index)

    blk = pl.BlockSpec((None, tr, C), lambda l, i: (l, i, 0))
    out = jax.ShapeDtypeStruct((n_layer, R, C), F32)
    return pl.pallas_call(
        body, name=name, grid=(n_layer, n_t),
        in_specs=[parts_spec(layer) for layer in range(n_layer)] + [blk, blk, blk],
        out_specs=[blk] * 4, out_shape=[out] * 4,
        compiler_params=_params("arbitrary", "arbitrary"),
    )(*parts, w, m, v)


def _pack(arrays):
    flat = jnp.concatenate([a.reshape(-1).astype(F32) for a in arrays])
    pad = (-flat.shape[0]) % (8 * LANES)
    return jnp.pad(flat, (0, pad)).reshape(-1, LANES)


def _unpack(buf, shapes):
    flat = buf.reshape(-1)
    out, off = [], 0
    for shp in shapes:
        size = 1
        for s in shp:
            size *= s
        out.append(flat[off:off + size].reshape(shp))
        off += size
    return out


def _block_diag_pairs(w):
    w = w.reshape(N_CBLK, 2, LRU_BLOCK_DIM, LRU_BLOCK_DIM)
    z = jnp.zeros_like(w[:, 0])
    top = jnp.concatenate([w[:, 0], z], axis=2)
    bot = jnp.concatenate([z, w[:, 1]], axis=2)
    return jnp.concatenate([top, bot], axis=1)


def _diag_pairs(m):
    h = LRU_BLOCK_DIM
    return jnp.stack([m[:, :h, :h], m[:, h:, h:]], axis=1).reshape(2 * N_CBLK, h, h)


SMALL = ("mlp_norm", "lru_conv_b", "lru_w_r", "lru_b_r", "lru_w_i", "lru_b_i",
         "lru_lambda", "fox_b_f", "fox_q_gain", "fox_k_gain")
WEIGHTS = ("mix_norm", "mlp_norm", "mlp_w1", "mlp_w2", "lru_w_in", "lru_conv_w", "lru_conv_b",
           "lru_w_r", "lru_b_r", "lru_w_i", "lru_b_i", "lru_lambda", "lru_w_out", "fox_w_in",
           "fox_b_f", "fox_q_gain", "fox_k_gain", "fox_w_out")


def kernel(x, mix_norm, mlp_norm, mlp_w1, mlp_w2, lru_w_in, lru_conv_w, lru_conv_b, lru_w_r, lru_b_r, lru_w_i, lru_b_i, lru_lambda, lru_w_out, fox_w_in, fox_b_f, fox_q_gain, fox_k_gain, fox_w_out, loss_target, m_mix_norm, m_mlp_norm, m_mlp_w1, m_mlp_w2, m_lru_w_in, m_lru_conv_w, m_lru_conv_b, m_lru_w_r, m_lru_b_r, m_lru_w_i, m_lru_b_i, m_lru_lambda, m_lru_w_out, m_fox_w_in, m_fox_b_f, m_fox_q_gain, m_fox_k_gain, m_fox_w_out, v_mix_norm, v_mlp_norm, v_mlp_w1, v_mlp_w2, v_lru_w_in, v_lru_conv_w, v_lru_conv_b, v_lru_w_r, v_lru_b_r, v_lru_w_i, v_lru_b_i, v_lru_lambda, v_lru_w_out, v_fox_w_in, v_fox_b_f, v_fox_q_gain, v_fox_k_gain, v_fox_w_out):
    w_in = dict(mix_norm=mix_norm, mlp_norm=mlp_norm, mlp_w1=mlp_w1, mlp_w2=mlp_w2,
                lru_w_in=lru_w_in, lru_conv_w=lru_conv_w, lru_conv_b=lru_conv_b, lru_w_r=lru_w_r,
                lru_b_r=lru_b_r, lru_w_i=lru_w_i, lru_b_i=lru_b_i, lru_lambda=lru_lambda,
                lru_w_out=lru_w_out, fox_w_in=fox_w_in, fox_b_f=fox_b_f, fox_q_gain=fox_q_gain,
                fox_k_gain=fox_k_gain, fox_w_out=fox_w_out)
    m_in = dict(mix_norm=m_mix_norm, mlp_norm=m_mlp_norm, mlp_w1=m_mlp_w1, mlp_w2=m_mlp_w2,
                lru_w_in=m_lru_w_in, lru_conv_w=m_lru_conv_w, lru_conv_b=m_lru_conv_b,
                lru_w_r=m_lru_w_r, lru_b_r=m_lru_b_r, lru_w_i=m_lru_w_i, lru_b_i=m_lru_b_i,
                lru_lambda=m_lru_lambda, lru_w_out=m_lru_w_out, fox_w_in=m_fox_w_in,
                fox_b_f=m_fox_b_f, fox_q_gain=m_fox_q_gain, fox_k_gain=m_fox_k_gain,
                fox_w_out=m_fox_w_out)
    v_in = dict(mix_norm=v_mix_norm, mlp_norm=v_mlp_norm, mlp_w1=v_mlp_w1, mlp_w2=v_mlp_w2,
                lru_w_in=v_lru_w_in, lru_conv_w=v_lru_conv_w, lru_conv_b=v_lru_conv_b,
                lru_w_r=v_lru_w_r, lru_b_r=v_lru_b_r, lru_w_i=v_lru_w_i, lru_b_i=v_lru_b_i,
                lru_lambda=v_lru_lambda, lru_w_out=v_lru_w_out, fox_w_in=v_fox_w_in,
                fox_b_f=v_fox_b_f, fox_q_gain=v_fox_q_gain, fox_k_gain=v_fox_k_gain,
                fox_w_out=v_fox_w_out)
    D = D_MODEL
    S = x.shape[1]
    x0, target = x[0], loss_target[0]
    me = 4 * lax.axis_index("x") + 2 * lax.axis_index("y") + lax.axis_index("c")

    def bf16(a):
        return a.astype(BF16)

    (lru_in_g,) = _exchange([bf16(lru_w_in[0])], [True], "gather_lru_in", two_level=True)
    gather_lru, tok = _exchange_start([bf16(lru_w_out[0]), lru_conv_w[0]], [True] * 2, lru_in_g,
                                      "gather_lru_start")
    gather_mlp0, tok = _exchange_start([bf16(mlp_w1[0]), bf16(mlp_w2[0])], [True] * 2, tok,
                                       "gather_mlp0_start", NEAR_PEERS)
    gather_fox, tok = _exchange_start([bf16(fox_w_in[0]), bf16(fox_w_out[0])], [True] * 2, tok,
                                      "gather_fox_start")
    gather_mlp1, tok = _exchange_start([bf16(mlp_w1[1]), bf16(mlp_w2[1])], [True] * 2, tok,
                                       "gather_mlp1_start", NEAR_PEERS)

    def pass_on(started, after, name):
        lands = _exchange_wait(started, [True] * 2, after, name + "_wait", NEAR_PEERS)
        return _forward_start(lands, after, name + "_pass_start")
    wr =_block_diag_pairs(lru_w_r[0]).astype(BF16)
    wi = _block_diag_pairs(lru_w_i[0]).astype(BF16)
    b_r, b_i = lru_b_r.reshape(1, D), lru_b_i.reshape(1, D)
    q_gain, k_gain = jnp.tile(fox_q_gain, (1, 2)), jnp.tile(fox_k_gain, (1, 2))
    b_f = jnp.pad(fox_b_f, ((0, 0), (0, LANES - N_HEADS)))
    g_mix0, g_mix1 = mix_norm[0:1] + tok[0, 0], mix_norm[1:2]
    g_mlp0, g_mlp1 = mlp_norm[0:1], mlp_norm[1:2]

    (u0,), h0 = _norm_matmul(x0, g_mix0, [lru_in_g], "lru_in_proj")
    lru_out_g, conv_g = _exchange_wait(gather_lru, [True] * 2, u0, "gather_lru_wait")
    lru_out_w = lru_out_g.reshape(D, D)
    conv_w = conv_g.transpose(1, 0, 2).reshape(CONV_WIDTH, D)
    y_lru, hs =_lru_fwd(u0, conv_w, lru_conv_b, wr, b_r, wi, b_i, lru_lambda, "lru_core")
    pass_mlp0, tok = pass_on(gather_mlp0, y_lru, "gather_mlp0")
    x1 = _matmul_res(y_lru, lru_out_w, x0, "lru_out_proj", tok)
    w1g0, w2g0 = _forward_wait(pass_mlp0, x1, "gather_mlp0_pass_wait")
    x2, h1, r1 = _mlp_fwd(x1, g_mlp0, w1g0, w2g0, "mlp0")
    fox_in_g, fox_out_g = _exchange_wait(gather_fox, [True] * 2, x2, "gather_fox_wait")
    fox_out_w = fox_out_g.reshape(D, D)
    fox_full = jnp.concatenate([fox_in_g[d] for d in range(N_DEV)], axis=1)
    wqkv = fox_full[:, :3 * D].reshape(D, 3, D).transpose(1, 0, 2)
    wf = jnp.pad(fox_full[:, 3 * D:], ((0, 0), (0, LANES - N_HEADS)))[None]
    (u_qkv, f), h2 = _norm_matmul(x2, g_mix1, [wqkv, wf], "fox_in_proj")
    qn, kn, vb = _qk_prep(u_qkv, q_gain, k_gain, "fox_qk_norm")
    c_col = _forget_fwd(f, b_f, "fox_forget")
    c_row = c_col[:, :N_HEADS].T.reshape(N_CBLK, 2, S)
    o, lse = _attn_fwd(qn, kn, vb, c_row, "fox_attn")
    pass_mlp1, tok = pass_on(gather_mlp1, o, "gather_mlp1")
    x3 = _matmul_res(o, fox_out_w, x2, "fox_out_proj", tok)
    w1g1, w2g1 = _forward_wait(pass_mlp1, x3, "gather_mlp1_pass_wait")
    loss_local, dx4, h3, r3 = _mlp_fwd(x3, g_mlp1, w1g1, w2g1, "mlp1", target)

    dx3, dg_mlp1, da3 = _mlp_bwd(dx4, x3, g_mlp1, r3, w1g1, w2g1, "mlp1_bwd")
    dw1_1 = _matmul_tn(h3, da3, "mlp1_dw1", cols=2, col_blocks=N_DEV)
    dw2_1 = _matmul_tn(r3, dx4, "mlp1_dw2", rows=2, a_square=True, tm=1024).reshape(N_DEV, -1, D)
    grads_mlp1, tok = _exchange_start([dw1_1, dw2_1], [False] * 2, tok, "grads_mlp1_start")
    do = _matmul_nt(dx3, fox_out_w, "fox_out_bwd", BF16, tok)
    d_fox_out = _matmul_tn(o, dx3, "fox_out_dw").reshape(N_DEV, -1, D)
    dqn, dkn, dv, dc_row, rho = _attn_bwd(qn, kn, vb, do, o, lse, c_row, "fox_attn_bwd")
    duq, duk, dq_gain, dk_gain = _qk_bwd(u_qkv, dqn, dkn, q_gain, k_gain, "fox_qk_norm_bwd")
    dc_k = jnp.pad(dc_row.reshape(N_HEADS, S).T, ((0, 0), (0, LANES - N_HEADS)))
    df, db_f = _forget_bwd(dc_k, rho, f, b_f, "fox_forget_bwd")
    dx2, dg_mix1 = _proj_bwd([[duq, duk, dv], [df]], [wqkv, wf], x2, g_mix1, dx3, "fox_in_bwd")
    d_fox_in = jnp.concatenate(
        [_matmul_tn(h2, duq, "fox_in_dwq"), _matmul_tn(h2, duk, "fox_in_dwk"),
         _matmul_tn(h2, dv, "fox_in_dwv"), _matmul_tn(h2, df, "fox_in_dwf")[:, :N_HEADS]], axis=1)
    shard = (3 * D + N_HEADS) // N_DEV
    d_fox_in = jnp.stack([d_fox_in[:, d * shard:(d + 1) * shard] for d in range(N_DEV)])
    grads_fox, tok = _exchange_start([d_fox_in, d_fox_out], [False] * 2, tok, "grads_fox_start")
    dx1, dg_mlp0, da1 = _mlp_bwd(dx2, x1, g_mlp0 + tok[0, 0], r1, w1g0, w2g0, "mlp0_bwd")
    dw1_0 = _matmul_tn(h1, da1, "mlp0_dw1", cols=2, col_blocks=N_DEV)
    dw2_0 = _matmul_tn(r1, dx2, "mlp0_dw2", rows=2, a_square=True, tm=1024).reshape(N_DEV, -1, D)
    grads_mlp0, tok = _exchange_start([dw1_0, dw2_0], [False] * 2, tok, "grads_mlp0_start")
    dy_lru = _matmul_nt(dx1, lru_out_w, "lru_out_bwd", F32, tok)
    d_lru_out = _matmul_tn(y_lru, dx1, "lru_out_dw").reshape(N_DEV, -1, D)
    dgp, dxb, d_conv_w, d_conv_b, d_b_r, d_b_i, d_lam, d_wr, d_wi = _lru_bwd(
        dy_lru, u0, hs, conv_w, lru_conv_b, wr, b_r, wi, b_i, lru_lambda, "lru_core_bwd")

    small_grads = dict(
        mlp_norm=jnp.concatenate([dg_mlp0, dg_mlp1], axis=0),
        lru_conv_b=d_conv_b, lru_w_r=_diag_pairs(d_wr), lru_b_r=d_b_r, lru_w_i=_diag_pairs(d_wi),
        lru_b_i=d_b_i, lru_lambda=d_lam, fox_b_f=db_f[:, :N_HEADS],
        fox_q_gain=dq_gain[:, :HEAD_DIM], fox_k_gain=dk_gain[:, :HEAD_DIM])
    small_partial = _pack([dg_mix1] + [small_grads[n] for n in SMALL] + [d_conv_w])
    grads_lru_out, tok = _exchange_start([d_lru_out, small_partial], [False, True], tok,
                                         "grads_lru_out_start")
    dx0, dg_mix0 = _proj_bwd([[dgp, dxb]], [lru_in_g], x0, mix_norm[0:1] + tok[0, 0], dx1,
                             "lru_in_bwd")
    d_lru_in = jnp.concatenate([_matmul_tn(h0, dgp, "lru_in_dw_gate", col_blocks=4),
                                _matmul_tn(h0, dxb, "lru_in_dw_x", col_blocks=4)], axis=0)
    grads_lru_in, tok = _exchange_start([d_lru_in, dg_mix0], [False, True], tok,
                                        "grads_lru_in_start")

    grads, deltas, new_m, new_v = {}, {}, {}, {}

    def update(name, parts):
        w, m, v = w_in[name], m_in[name], v_in[name]
        shape = w.shape
        stacked = (len(parts), -1, shape[-1])
        w3 = w.reshape(stacked)
        res = _reduce_adamw([p.reshape((N_DEV,) + w3.shape[1:]) for p in parts], w3,
                            m.reshape(stacked), v.reshape(stacked), "adamw_" + name)
        return [r.reshape(shape) for r in res]

    def store(name, res):
        grads[name], deltas[name], new_m[name], new_v[name] = res

    p_w1_1, p_w2_1 = _exchange_wait(grads_mlp1, [False] * 2, tok, "grads_mlp1_wait")
    p_fox_in, p_fox_out = _exchange_wait(grads_fox, [False] * 2, p_w1_1, "grads_fox_wait")
    store("fox_w_in", update("fox_w_in", [p_fox_in]))
    store("fox_w_out", update("fox_w_out", [p_fox_out]))
    p_w1_0, p_w2_0 = _exchange_wait(grads_mlp0, [False] * 2, grads["fox_w_out"], "grads_mlp0_wait")
    store("mlp_w1", update("mlp_w1", [p_w1_0, p_w1_1]))
    store("mlp_w2", update("mlp_w2", [p_w2_0, p_w2_1]))
    p_lru_out, p_small = _exchange_wait(grads_lru_out, [False, True], grads["mlp_w2"],
                                        "grads_lru_out_wait")
    store("lru_w_out", update("lru_w_out", [p_lru_out]))
    p_lru_in, p_mix0 = _exchange_wait(grads_lru_in, [False, True], grads["lru_w_out"],
                                      "grads_lru_in_wait")
    store("lru_w_in", update("lru_w_in", [p_lru_in]))

    mix0 = [r[0] for r in _reduce_adamw([p_mix0], mix_norm[None, 0:1], m_mix_norm[None, 0:1],
                                        v_mix_norm[None, 0:1], "adamw_mix0")]
    packed = lambda src, first: _pack([first] + [src[n] for n in SMALL]
                                      + [jnp.zeros((CONV_WIDTH, D))])[None]
    small_shapes = [(1, D)] + [w_in[n].shape for n in SMALL]
    n_small = sum(math.prod(s) for s in small_shapes)
    res_small = _reduce_adamw([p_small], packed(w_in, mix_norm[1:2]), packed(m_in, m_mix_norm[1:2]),
                              packed(v_in, v_mix_norm[1:2]), "adamw_small")
    for name, *vals in zip(("mix1",) + SMALL, *[_unpack(r, small_shapes) for r in res_small]):
        if name == "mix1":
            vals = [jnp.concatenate([r0, r1], axis=0) for r0, r1 in zip(mix0, vals)]
            name = "mix_norm"
        store(name, vals)
    conv_parts = p_small.reshape(N_DEV, -1)[:, n_small:n_small + CONV_WIDTH * D]
    conv_parts = conv_parts.reshape(N_DEV, CONV_WIDTH, N_DEV, LANES)
    conv_parts = lax.dynamic_index_in_dim(conv_parts, me, axis=2, keepdims=False)
    store("lru_conv_w", update("lru_conv_w", [conv_parts]))

    loss = lax.psum(loss_local[0, 0], ("x", "y", "c"))
    return (loss, dx0[None], *[grads[n] for n in WEIGHTS], *[deltas[n] for n in WEIGHTS],
            *[new_m[n] for n in WEIGHTS], *[new_v[n] for n in WEIGHTS])
```

```python
import functools
import math

import jax
import jax.numpy as jnp
from jax import lax
from jax.experimental import pallas as pl
from jax.experimental.pallas import tpu as pltpu

F32 = jnp.float32
BF16 = jnp.bfloat16

N_DEV = 8
D_MODEL = 1024
D_FF = 4096
N_HEADS = 16
HEAD_DIM = 64
LRU_BLOCK_DIM = 64
CONV_WIDTH = 4
LRU_C = 8.0
EPS = 1e-6
NEG_INF = -1e30
ATTN_SCALE = HEAD_DIM ** -0.5
LANES = 128
N_CBLK = D_MODEL // LANES
VMEM_LIMIT = 52 * 2 ** 20

ADAM_LR = 0.001
ADAM_B1 = 0.9
ADAM_B2 = 0.999
ADAM_EPS = 1e-08
ADAM_WD = 0.01
ADAM_STEP = 10

_NT = (((1,), (1,)), ((), ()))
_TN = (((0,), (0,)), ((), ()))


def _params(*sem):
    return pltpu.CompilerParams(dimension_semantics=sem, vmem_limit_bytes=VMEM_LIMIT)


def _resident(shape):
    zeros = (0,) * len(shape)
    return pl.BlockSpec(shape, lambda *_: zeros, pipeline_mode=pl.Buffered(1))


def _dot(a, b):
    return jnp.dot(a, b, preferred_element_type=F32)


def _dot_nt(a, b):
    return lax.dot_general(a, b, _NT, preferred_element_type=F32)


def _dot_tn(a, b):
    return lax.dot_general(a, b, _TN, preferred_element_type=F32)


def _sigmoid(x):
    return 1.0 / (1.0 + jnp.exp(-x))


def _log_sigmoid(x):
    return -(jnp.maximum(-x, 0.0) + jnp.log1p(jnp.exp(-jnp.abs(x))))


def _expm1(x):
    poly = x * (1.0 + x * (0.5 + x * (1.0 / 6.0 + x * (1.0 / 24.0 + x * (1.0 / 120.0)))))
    return jnp.where(jnp.abs(x) < 0.1, poly, jnp.exp(x) - 1.0)


_GELU_K = 0.7978845608028654


def _gelu(x):
    return 0.5 * x * (1.0 + jnp.tanh(_GELU_K * (x + 0.044715 * (x * x * x))))


def _gelu_grad(x):
    t = jnp.tanh(_GELU_K * (x + 0.044715 * (x * x * x)))
    return 0.5 * (1.0 + t) + 0.5 * x * (1.0 - t * t) * (_GELU_K * (1.0 + 3 * 0.044715 * x * x))


def _rms_scale(x):
    return lax.rsqrt(jnp.mean(x * x, axis=-1, keepdims=True) + EPS)


def _norm_bwd(dh, x, g):
    rs = _rms_scale(x)
    xhat = x * rs
    dxhat = dh * g
    dx = rs * (dxhat - xhat * jnp.mean(dxhat * xhat, axis=-1, keepdims=True))
    return dx, jnp.sum(dh * xhat, axis=0, keepdims=True)


def _token_tile(S, want):
    tm = min(S, want)
    assert S % tm == 0
    return tm


def _norm_matmul(x, g, ws, name, tm=512):
    S, D = x.shape
    tm = _token_tile(S, tm)
    n = len(ws)

    def body(x_ref, g_ref, *refs):
        w_refs, o_refs, h_ref = refs[:n], refs[n:2 * n], refs[2 * n]
        xv = x_ref[...]
        h = (xv * _rms_scale(xv) * g_ref[...]).astype(BF16)
        h_ref[...] = h
        for w_ref, o_ref in zip(w_refs, o_refs):
            nb, _, nw = w_ref.shape
            for d in range(nb):
                o_ref[:, d * nw:(d + 1) * nw] = _dot(h, w_ref[d])

    widths = [w.shape[0] * w.shape[2] for w in ws]
    outs = pl.pallas_call(
        body, name=name, grid=(S // tm,),
        in_specs=[pl.BlockSpec((tm, D), lambda i: (i, 0)), _resident((1, D))]
        + [_resident(w.shape) for w in ws],
        out_specs=[pl.BlockSpec((tm, n_), lambda i: (i, 0)) for n_ in widths]
        + [pl.BlockSpec((tm, D), lambda i: (i, 0))],
        out_shape=[jax.ShapeDtypeStruct((S, n_), F32) for n_ in widths]
        + [jax.ShapeDtypeStruct((S, D), BF16)],
        compiler_params=_params("parallel"),
    )(x, g, *ws)
    return outs[:n], outs[n]


def _matmul_res(a, w, res, name, after, tm=512):
    S, K = a.shape
    N = w.shape[1]
    tm = _token_tile(S, tm)

    def body(a_ref, w_ref, r_ref, after_ref, o_ref):
        o_ref[...] = r_ref[...] + _dot(a_ref[...], w_ref[...])

    return pl.pallas_call(
        body, name=name, grid=(S // tm,),
        in_specs=[pl.BlockSpec((tm, K), lambda i: (i, 0)), _resident((K, N)),
                  pl.BlockSpec((tm, N), lambda i: (i, 0)), pl.BlockSpec(memory_space=pl.ANY)],
        out_specs=pl.BlockSpec((tm, N), lambda i: (i, 0)),
        out_shape=jax.ShapeDtypeStruct((S, N), F32),
        compiler_params=_params("parallel"),
    )(a, w, res, after)


def _matmul_nt(a, w, name, out_dtype, after, tm=1024):
    S, N = a.shape
    K = w.shape[0]
    tm = _token_tile(S, tm)

    def body(a_ref, w_ref, after_ref, o_ref):
        o_ref[...] = _dot_nt(a_ref[...].astype(BF16), w_ref[...]).astype(out_dtype)

    return pl.pallas_call(
        body, name=name, grid=(S // tm,),
        in_specs=[pl.BlockSpec((tm, N), lambda i: (i, 0)), _resident((K, N)),
                  pl.BlockSpec(memory_space=pl.ANY)],
        out_specs=pl.BlockSpec((tm, K), lambda i: (i, 0)),
        out_shape=jax.ShapeDtypeStruct((S, K), out_dtype),
        compiler_params=_params("parallel"),
    )(a, w, after)


def _proj_bwd(a_lists, w_list, x, g, res, name, tm=512):
    S, D = x.shape
    tm = _token_tile(S, tm)
    a_list = [a for group in a_lists for a in group]
    n, n_w = len(a_list), len(w_list)

    def body(*refs):
        a_refs, w_refs = list(refs[:n]), refs[n:n + n_w]
        x_ref, g_ref, r_ref, dx_ref, dg_ref = refs[n + n_w:]
        dh = jnp.zeros((tm, D), F32)
        for group, w_ref in zip(a_lists, w_refs):
            nw = w_ref.shape[2]
            d = 0
            for _ in group:
                a_ref = a_refs.pop(0)
                for j in range(a_ref.shape[1] // nw):
                    dh = dh + _dot_nt(a_ref[:, j * nw:(j + 1) * nw].astype(BF16), w_ref[d])
                    d += 1
        dx, dg = _norm_bwd(dh, x_ref[...], g_ref[...])
        dx_ref[...] = r_ref[...] + dx

        @pl.when(pl.program_id(0) == 0)
        def _():
            dg_ref[...] = jnp.zeros_like(dg_ref)
        dg_ref[...] += dg

    tok = lambda width: pl.BlockSpec((tm, width), lambda i: (i, 0))
    return pl.pallas_call(
        body, name=name, grid=(S // tm,),
        in_specs=[tok(a.shape[1]) for a in a_list] + [_resident(w.shape) for w in w_list]
        + [tok(D), _resident((1, D)), tok(D)],
        out_specs=[tok(D), pl.BlockSpec((1, D), lambda i: (0, 0))],
        out_shape=[jax.ShapeDtypeStruct((S, D), F32), jax.ShapeDtypeStruct((1, D), F32)],
        compiler_params=_params("arbitrary"),
    )(*a_list, *w_list, x, g, res)


def _matmul_tn(a, b, name, rows=1, cols=1, col_blocks=None, a_square=False, tm=2048):
    S, K = a.shape
    N = b.shape[1]
    tm = _token_tile(S, tm)
    n_tok = S // tm
    kr, nc = K // rows, N // cols

    def body(a_ref, b_ref, o_ref, acc_ref):
        av = a_ref[...]
        if a_square:
            av = av.astype(F32)
            av = av * av
        part = _dot_tn(av.astype(BF16), b_ref[...].astype(BF16))
        step = pl.program_id(2)

        @pl.when(step == 0)
        def _():
            acc_ref[...] = part

        @pl.when(step > 0)
        def _():
            acc_ref[...] += part

        @pl.when(step == n_tok - 1)
        def _():
            if col_blocks is None:
                o_ref[...] = acc_ref[...].astype(BF16)
            else:
                nw = N // col_blocks
                for d in range(col_blocks // cols):
                    o_ref[d] = acc_ref[:, d * nw:(d + 1) * nw].astype(BF16)

    if col_blocks is None:
        out_spec = pl.BlockSpec((kr, nc), lambda r, c, i: (r, c))
        out_shape = jax.ShapeDtypeStruct((K, N), BF16)
    else:
        assert rows == 1 and col_blocks % cols == 0
        per = col_blocks // cols
        out_spec = pl.BlockSpec((per, K, N // col_blocks), lambda r, c, i: (c, 0, 0))
        out_shape = jax.ShapeDtypeStruct((col_blocks, K, N // col_blocks), BF16)
    return pl.pallas_call(
        body, name=name, grid=(rows, cols, n_tok),
        in_specs=[pl.BlockSpec((tm, kr), lambda r, c, i: (i, r)),
                  pl.BlockSpec((tm, nc), lambda r, c, i: (i, c))],
        out_specs=out_spec, out_shape=out_shape,
        scratch_shapes=[pltpu.VMEM((kr, nc), F32)],
        compiler_params=_params("parallel", "parallel", "arbitrary"),
    )(a, b)


def _mlp_fwd(x, g, w1, w2, name, target=None, tm=512):
    S, D = x.shape
    nb, _, fb = w1.shape
    tm = _token_tile(S, tm)
    with_loss = target is not None

    def body(x_ref, g_ref, w1_ref, w2_ref, *refs):
        h_ref, r_ref = refs[-2:]
        xv = x_ref[...]
        h = (xv * _rms_scale(xv) * g_ref[...]).astype(BF16)
        h_ref[...] = h
        acc = xv
        for d in range(nb):
            r = jnp.maximum(_dot(h, w1_ref[d]), 0.0)
            r_ref[:, d * fb:(d + 1) * fb] = r.astype(BF16)
            acc = acc + _dot((r * r).astype(BF16), w2_ref[d])
        if not with_loss:
            refs[0][...] = acc
            return
        t_ref, loss_ref, dy_ref = refs[:3]
        err = acc - t_ref[...]
        dy_ref[...] = err / D

        @pl.when(pl.program_id(0) == 0)
        def _():
            loss_ref[...] = jnp.zeros_like(loss_ref)
        row_loss = jnp.mean(err * err, axis=1, keepdims=True)
        loss_ref[...] += 0.5 * jnp.sum(row_loss, axis=0, keepdims=True)

    tok = lambda width: pl.BlockSpec((tm, width), lambda i: (i, 0))
    saved_specs = [tok(D), tok(nb * fb)]
    saved_shapes = [jax.ShapeDtypeStruct((S, D), BF16), jax.ShapeDtypeStruct((S, nb * fb), BF16)]
    wide = jax.ShapeDtypeStruct((S, D), F32)
    if with_loss:
        head_specs = [pl.BlockSpec((1, 1), lambda i: (0, 0)), tok(D)]
        head_shapes = [jax.ShapeDtypeStruct((1, 1), F32), wide]
    else:
        head_specs, head_shapes = [tok(D)], [wide]
    return pl.pallas_call(
        body, name=name, grid=(S // tm,),
        in_specs=[tok(D), _resident((1, D)), _resident(w1.shape), _resident(w2.shape)]
        + ([tok(D)] if with_loss else []),
        out_specs=head_specs + saved_specs, out_shape=head_shapes + saved_shapes,
        compiler_params=_params("arbitrary" if with_loss else "parallel"),
    )(x, g, w1, w2, *([target] if with_loss else []))


def _mlp_bwd(dout, x, g, r, w1, w2, name, tm=512):
    S, D = x.shape
    nb, _, fb = w1.shape
    tm = _token_tile(S, tm)

    def body(do_ref, x_ref, g_ref, r_ref, w1_ref, w2_ref, dx_ref, dg_ref, da_ref):
        dov = do_ref[...]
        dob = dov.astype(BF16)
        dh = jnp.zeros((tm, D), F32)
        for d in range(nb):
            dz = _dot_nt(dob, w2_ref[d])
            da = (dz * (2.0 * r_ref[:, d * fb:(d + 1) * fb].astype(F32))).astype(BF16)
            da_ref[:, d * fb:(d + 1) * fb] = da
            dh = dh + _dot_nt(da, w1_ref[d])
        dx, dg = _norm_bwd(dh, x_ref[...], g_ref[...])
        dx_ref[...] = dov + dx

        @pl.when(pl.program_id(0) == 0)
        def _():
            dg_ref[...] = jnp.zeros_like(dg_ref)
        dg_ref[...] += dg

    tok = lambda width: pl.BlockSpec((tm, width), lambda i: (i, 0))
    return pl.pallas_call(
        body, name=name, grid=(S // tm,),
        in_specs=[tok(D), tok(D), _resident((1, D)), tok(nb * fb), _resident(w1.shape),
                  _resident(w2.shape)],
        out_specs=[tok(D), pl.BlockSpec((1, D), lambda i: (0, 0)), tok(nb * fb)],
        out_shape=[jax.ShapeDtypeStruct((S, D), F32), jax.ShapeDtypeStruct((1, D), F32),
                   jax.ShapeDtypeStruct((S, nb * fb), BF16)],
        compiler_params=_params("arbitrary"),
    )(dout, x, g, r, w1, w2)


def _scan_chunk(a, b, row, T, reverse):
    s = 1
    while s < T:
        if reverse:
            keep, shift = row < T - s, T - s
        else:
            keep, shift = row >= s, s
        a_sh = jnp.where(keep, pltpu.roll(a, shift, 0), 1.0)
        b_sh = jnp.where(keep, pltpu.roll(b, shift, 0), 0.0)
        b = a * b_sh + b
        a = a * a_sh
        s *= 2
    return a, b


def _row_of(x, row, r):
    return jnp.sum(jnp.where(row == r, x, 0.0), axis=0, keepdims=True)


def _shift_down(x, prev, row, k):
    if k == 0:
        return x
    return jnp.where(row < k, pltpu.roll(prev, k, 0), pltpu.roll(x, k, 0))


def _shift_up(x, nxt, row, k, T):
    if k == 0:
        return x
    return jnp.where(row < T - k, pltpu.roll(x, T - k, 0), pltpu.roll(nxt, T - k, 0))


def _lru_gates(xb, prev_xb, row, cw_ref, cb, wr, br, wi, bi, ls):
    xc = cb + cw_ref[pl.ds(0, 1), :] * _shift_down(xb, prev_xb, row, 3)
    for k in (2, 1, 0):
        xc = xc + cw_ref[pl.ds(3 - k, 1), :] * _shift_down(xb, prev_xb, row, k)
    xcb = xc.astype(BF16)
    r = _sigmoid(_dot(xcb, wr) + br)
    i = _sigmoid(_dot(xcb, wi) + bi)
    la = (LRU_C * r) * ls
    a = jnp.exp(la)
    m = jnp.sqrt(-_expm1(2.0 * la))
    return xc, xcb, r, i, a, m


def _lru_specs(S):
    col = lambda off: pl.BlockSpec((S, LANES), lambda j: (0, j + off))
    vec = pl.BlockSpec((1, LANES), lambda j: (0, j))
    mat = pl.BlockSpec((None, LANES, LANES), lambda j: (j, 0, 0))
    cwm = pl.BlockSpec((CONV_WIDTH, LANES), lambda j: (0, j))
    return col, vec, mat, cwm


def _lru_fwd(u, conv_w, conv_b, wr, br, wi, bi, lam, name):
    S = u.shape[0]
    T = _token_tile(S, 512)
    col, vec, mat, cwm = _lru_specs(S)

    def body(gp_ref, xb_ref, cw_ref, cb_ref, wr_ref, br_ref, wi_ref, bi_ref, lam_ref,
             y_ref, hs_ref):
        row = lax.broadcasted_iota(jnp.int32, (T, LANES), 0)
        ls = _log_sigmoid(lam_ref[...])
        cb, br, bi = cb_ref[...], br_ref[...], bi_ref[...]
        wr, wi = wr_ref[...], wi_ref[...]

        def chunk(ci, carry):
            prev_xb, hc = carry
            rows = pl.ds(pl.multiple_of(ci * T, T), T)
            xb = xb_ref[rows, :]
            xc, _, _, i, a, m = _lru_gates(xb, prev_xb, row, cw_ref, cb, wr, br, wi, bi, ls)
            ca, cbv = _scan_chunk(a, m * (i * xc), row, T, reverse=False)
            h = ca * hc + cbv
            hs_ref[rows, :] = h
            y_ref[rows, :] = (_gelu(gp_ref[rows, :]) * h).astype(BF16)
            return xb, _row_of(h, row, T - 1)

        lax.fori_loop(0, S // T, chunk,
                      (jnp.zeros((T, LANES), F32), jnp.zeros((1, LANES), F32)))

    return pl.pallas_call(
        body, name=name, grid=(N_CBLK,),
        in_specs=[col(0), col(N_CBLK), cwm, vec, mat, vec, mat, vec, vec],
        out_specs=[col(0), col(0)],
        out_shape=[jax.ShapeDtypeStruct((S, D_MODEL), BF16), jax.ShapeDtypeStruct((S, D_MODEL), F32)],
        compiler_params=_params("parallel"),
    )(u, u, conv_w, conv_b, wr, br, wi, bi, lam)


def _lru_bwd(dy, u, hs, conv_w, conv_b, wr, br, wi, bi, lam, name):
    S = u.shape[0]
    T = _token_tile(S, 512)
    n_chunk = S // T
    col, vec, mat, cwm = _lru_specs(S)

    def body(dy_ref, gp_ref, xb_ref, hs_ref, cw_ref, cb_ref, wr_ref, br_ref, wi_ref, bi_ref,
             lam_ref, dgp_ref, dxb_ref, dcw_ref, dcb_ref, dbr_ref, dbi_ref, dlam_ref, dwr_ref,
             dwi_ref):
        row = lax.broadcasted_iota(jnp.int32, (T, LANES), 0)
        lam = lam_ref[...]
        ls = _log_sigmoid(lam)
        cb, br, bi = cb_ref[...], br_ref[...], bi_ref[...]
        wr, wi = wr_ref[...], wi_ref[...]
        for ref in (dcw_ref, dcb_ref, dbr_ref, dbi_ref, dlam_ref, dwr_ref, dwi_ref):
            ref[...] = jnp.zeros_like(ref)

        def chunk(it, carry):
            g_next, dxc_next = carry
            ci = n_chunk - 1 - it
            rows = pl.ds(pl.multiple_of(ci * T, T), T)
            before = pl.ds(pl.multiple_of(jnp.maximum(ci - 1, 0) * T, T), T)
            first = ci == 0
            xb = xb_ref[rows, :]
            prev_xb = jnp.where(first, 0.0, xb_ref[before, :])
            xc, xcb, r, i, a, m = _lru_gates(xb, prev_xb, row, cw_ref, cb, wr, br, wi, bi, ls)
            h = hs_ref[rows, :]
            h_prev = _shift_down(h, jnp.where(first, 0.0, hs_ref[before, :]), row, 1)
            gp = gp_ref[rows, :]
            dyv = dy_ref[rows, :]
            dgp_ref[rows, :] = (dyv * h * _gelu_grad(gp)).astype(BF16)
            dh = dyv * _gelu(gp)
            ca, cbv = _scan_chunk(a, a * dh, row, T, reverse=True)
            gp_acc = ca * g_next + cbv
            g = dh + jnp.where(row < T - 1, pltpu.roll(gp_acc, T - 1, 0), g_next)
            da = g * h_prev - (g * (i * xc)) * a / m
            dla = da * a
            dlam_ref[...] += jnp.sum(dla * (LRU_C * r), axis=0, keepdims=True)
            dpr = (dla * (LRU_C * ls)) * r * (1.0 - r)
            dpi = (g * m * xc) * i * (1.0 - i)
            dbr_ref[...] += jnp.sum(dpr, axis=0, keepdims=True)
            dbi_ref[...] += jnp.sum(dpi, axis=0, keepdims=True)
            dprb, dpib = dpr.astype(BF16), dpi.astype(BF16)
            dwr_ref[...] += _dot_tn(xcb, dprb)
            dwi_ref[...] += _dot_tn(xcb, dpib)
            dxc = g * m * i + _dot_nt(dprb, wr) + _dot_nt(dpib, wi)
            dcb_ref[...] += jnp.sum(dxc, axis=0, keepdims=True)
            dxb = jnp.zeros((T, LANES), F32)
            for k in range(CONV_WIDTH):
                tap = pl.ds(CONV_WIDTH - 1 - k, 1)
                dcw_ref[tap, :] += jnp.sum(dxc * _shift_down(xb, prev_xb, row, k), axis=0,
                                           keepdims=True)
                dxb = dxb + cw_ref[tap, :] * _shift_up(dxc, dxc_next, row, k, T)
            dxb_ref[rows, :] = dxb.astype(BF16)
            return _row_of(gp_acc, row, 0), dxc

        lax.fori_loop(0, n_chunk, chunk,
                      (jnp.zeros((1, LANES), F32), jnp.zeros((T, LANES), F32)))
        dlam_ref[...] = dlam_ref[...] * _sigmoid(-lam)

    vec_out = jax.ShapeDtypeStruct((1, D_MODEL), F32)
    mat_out = jax.ShapeDtypeStruct((N_CBLK, LANES, LANES), F32)
    return pl.pallas_call(
        body, name=name, grid=(N_CBLK,),
        in_specs=[col(0), col(0), col(N_CBLK), col(0), cwm, vec, mat, vec, mat, vec, vec],
        out_specs=[col(0), col(0), cwm, vec, vec, vec, vec, mat, mat],
        out_shape=[jax.ShapeDtypeStruct((S, D_MODEL), BF16), jax.ShapeDtypeStruct((S, D_MODEL), BF16),
                   jax.ShapeDtypeStruct((CONV_WIDTH, D_MODEL), F32),
                   vec_out, vec_out, vec_out, vec_out, mat_out, mat_out],
        compiler_params=_params("parallel"),
    )(dy, u, u, hs, conv_w, conv_b, wr, br, wi, bi, lam)


def _head_group_matrix(value):
    r = lax.broadcasted_iota(jnp.int32, (LANES, LANES), 0) // HEAD_DIM
    c = lax.broadcasted_iota(jnp.int32, (LANES, LANES), 1) // HEAD_DIM
    return jnp.where(r == c, value, 0.0).astype(BF16)


def _group_dot(x, p):
    hi = x.astype(BF16)
    lo = (x - hi.astype(F32)).astype(BF16)
    return _dot(hi, p) + _dot(lo, p)


def _head_mean(x, p):
    return _group_dot(x, p)


def _fox_in_proj(x, g, wqkv, wf, q_gain, k_gain, name, tm=512):
    S, D = x.shape
    tm = _token_tile(S, tm)

    def body(x_ref, g_ref, w_ref, wf_ref, qg_ref, kg_ref,
             uq_ref, uk_ref, f_ref, h_ref, qn_ref, kn_ref, vb_ref):
        xv = x_ref[...]
        h = (xv * _rms_scale(xv) * g_ref[...]).astype(BF16)
        h_ref[...] = h
        p = _head_group_matrix(1.0 / HEAD_DIM)
        for which, u_ref, gain_ref, n_ref, scale in ((0, uq_ref, qg_ref, qn_ref, ATTN_SCALE),
                                                     (1, uk_ref, kg_ref, kn_ref, 1.0)):
            u = _dot(h, w_ref[which])
            u_ref[...] = u
            for j in range(N_CBLK):
                cl = slice(j * LANES, (j + 1) * LANES)
                uv = u[:, cl]
                rs = lax.rsqrt(_head_mean(uv * uv, p) + EPS)
                n_ref[:, cl] = (uv * rs * gain_ref[...]).astype(BF16) * scale
        vb_ref[...] = _dot(h, w_ref[2]).astype(BF16)
        f_ref[...] = _dot(h, wf_ref[0])

    tok = lambda width: pl.BlockSpec((tm, width), lambda i: (i, 0))
    wide = jax.ShapeDtypeStruct((S, D), F32)
    half = jax.ShapeDtypeStruct((S, D), BF16)
    return pl.pallas_call(
        body, name=name, grid=(S // tm,),
        in_specs=[tok(D), _resident((1, D)), _resident(wqkv.shape), _resident(wf.shape),
                  _resident((1, LANES)), _resident((1, LANES))],
        out_specs=[tok(D), tok(D), tok(LANES), tok(D), tok(D), tok(D), tok(D)],
        out_shape=[wide, wide, jax.ShapeDtypeStruct((S, LANES), F32), half, half, half, half],
        compiler_params=_params("parallel"),
    )(x, g, wqkv, wf, q_gain, k_gain)


def _qk_bwd(uq, uk, dqn, dkn, q_gain, k_gain, name, tm=512):
    S = uq.shape[0]
    tm = _token_tile(S, tm)

    def body(q_ref, k_ref, dqn_ref, dkn_ref, qg_ref, kg_ref, dq_ref, dk_ref, dqg_ref, dkg_ref):
        p = _head_group_matrix(1.0 / HEAD_DIM)
        for x_ref, dn_ref, g_ref, dx_ref, dg_ref, scale in (
                (q_ref, dqn_ref, qg_ref, dq_ref, dqg_ref, ATTN_SCALE),
                (k_ref, dkn_ref, kg_ref, dk_ref, dkg_ref, 1.0)):
            dg = jnp.zeros((1, LANES), F32)
            for j in range(N_CBLK):
                cl = slice(j * LANES, (j + 1) * LANES)
                xv, dn = x_ref[:, cl], dn_ref[:, cl] * scale
                rs = lax.rsqrt(_head_mean(xv * xv, p) + EPS)
                xhat = xv * rs
                dxhat = dn * g_ref[...]
                dx_ref[:, cl] = (rs * (dxhat - xhat * _head_mean(dxhat * xhat, p))).astype(BF16)
                dg = dg + jnp.sum(dn * xhat, axis=0, keepdims=True)

            @pl.when(pl.program_id(0) == 0)
            def _():
                dg_ref[...] = jnp.zeros_like(dg_ref)
            dg_ref[...] += dg

            @pl.when(pl.program_id(0) == S // tm - 1)
            def _():
                dg_ref[...] += pltpu.roll(dg_ref[...], HEAD_DIM, 1)

    blk = lambda off: pl.BlockSpec((tm, D_MODEL), lambda i: (i, off))
    acc = pl.BlockSpec((1, LANES), lambda i: (0, 0))
    out = jax.ShapeDtypeStruct((S, D_MODEL), BF16)
    vec = jax.ShapeDtypeStruct((1, LANES), F32)
    return pl.pallas_call(
        body, name=name, grid=(S // tm,),
        in_specs=[blk(0), blk(0), blk(0), blk(0), _resident((1, LANES)), _resident((1, LANES))],
        out_specs=[blk(0), blk(0), acc, acc],
        out_shape=[out, out, vec, vec],
        compiler_params=_params("arbitrary"),
    )(uq, uk, dqn, dkn, q_gain, k_gain)


def _forget_fwd(f, b_f, name):
    S = f.shape[0]
    T = _token_tile(S, 256)

    def body(f_ref, b_ref, c_ref):
        row = lax.broadcasted_iota(jnp.int32, (T, LANES), 0)
        ones = jnp.ones((T, LANES), F32)
        bias = b_ref[...]

        def chunk(ci, carry):
            rows = pl.ds(pl.multiple_of(ci * T, T), T)
            _, c = _scan_chunk(ones, _log_sigmoid(f_ref[rows, :] + bias), row, T, reverse=False)
            c = c + carry
            c_ref[rows, :] = c
            return _row_of(c, row, T - 1)

        lax.fori_loop(0, S // T, chunk, jnp.zeros((1, LANES), F32))

    return pl.pallas_call(
        body, name=name,
        in_specs=[pl.BlockSpec(memory_space=pltpu.VMEM)] * 2,
        out_specs=pl.BlockSpec(memory_space=pltpu.VMEM),
        out_shape=jax.ShapeDtypeStruct((S, LANES), F32),
        compiler_params=pltpu.CompilerParams(vmem_limit_bytes=VMEM_LIMIT),
    )(f, b_f)


def _forget_bwd(dc_k, rho, f, b_f, name):
    S = f.shape[0]
    T = _token_tile(S, 256)
    n_chunk = S // T

    def body(dck_ref, rho_ref, f_ref, b_ref, df_ref, db_ref):
        row = lax.broadcasted_iota(jnp.int32, (T, LANES), 0)
        ones = jnp.ones((T, LANES), F32)
        bias = b_ref[...]
        pick = (lax.broadcasted_iota(jnp.int32, (D_MODEL, LANES), 0)
                == HEAD_DIM * lax.broadcasted_iota(jnp.int32, (D_MODEL, LANES), 1))
        pick = jnp.where(pick, 1.0, 0.0).astype(BF16)

        def chunk(it, carry):
            tail, db = carry
            rows = pl.ds(pl.multiple_of((n_chunk - 1 - it) * T, T), T)
            dc = dck_ref[rows, :] + _group_dot(rho_ref[rows, :], pick)
            _, dlf = _scan_chunk(ones, dc, row, T, reverse=True)
            dlf = dlf + tail
            df = dlf * _sigmoid(-(f_ref[rows, :] + bias))
            df_ref[rows, :] = df
            return _row_of(dlf, row, 0), db + jnp.sum(df, axis=0, keepdims=True)

        zero = jnp.zeros((1, LANES), F32)
        _, db = lax.fori_loop(0, n_chunk, chunk, (zero, zero))
        db_ref[...] = db

    return pl.pallas_call(
        body, name=name,
        in_specs=[pl.BlockSpec(memory_space=pltpu.VMEM)] * 4,
        out_specs=[pl.BlockSpec(memory_space=pltpu.VMEM)] * 2,
        out_shape=[jax.ShapeDtypeStruct((S, LANES), F32), jax.ShapeDtypeStruct((1, LANES), F32)],
        compiler_params=pltpu.CompilerParams(vmem_limit_bytes=VMEM_LIMIT),
    )(dc_k, rho, f, b_f)


ATTN_TILE = 512
ATTN_ROWS_FWD = 32


def _attn_tiles(S):
    t = _token_tile(S, ATTN_TILE)
    return t, S // t


def _causal(T):
    return (lax.broadcasted_iota(jnp.int32, (T, T), 1)
            <= lax.broadcasted_iota(jnp.int32, (T, T), 0))


def _attn_fwd(qs_, kn, vb, c_row, name):
    S = qs_.shape[0]
    T, n_t = _attn_tiles(S)
    RB = min(T, ATTN_ROWS_FWD)

    def body(q_ref, k_ref, v_ref, cr_ref, o_ref, lse_ref, sa_ref, sb_ref, p_ref, m_ref, l_ref,
             acc_ref, a_ref):
        qi = pl.program_id(1)
        lanes = [slice(h2 * HEAD_DIM, (h2 + 1) * HEAD_DIM) for h2 in range(2)]
        col = lax.broadcasted_iota(jnp.int32, (RB, T), 1)
        row = lax.broadcasted_iota(jnp.int32, (RB, T), 0)
        m_ref[...] = jnp.full(m_ref.shape, NEG_INF, F32)
        l_ref[...] = jnp.zeros_like(l_ref)
        acc_ref[...] = jnp.zeros_like(acc_ref)

        def logits_into(s_ref, kj):
            ks = pl.ds(pl.multiple_of(kj * T, T), T)
            for h2, hl in enumerate(lanes):
                s_ref[h2] = _dot_nt(q_ref[:, hl], k_ref[ks, hl]) - cr_ref[h2:h2 + 1, ks]

        def consume(s_ref, kj, masked):
            ks = pl.ds(pl.multiple_of(kj * T, T), T)
            for h2, hl in enumerate(lanes):
                blocks = [slice(i * RB, (i + 1) * RB) for i in range(T // RB)]

                def logits(i, rows):
                    s = s_ref[h2, rows, :]
                    return jnp.where(col <= row + i * RB, s, NEG_INF) if masked else s

                wide = lambda x: jnp.broadcast_to(x, (RB, LANES))
                for i, rows in enumerate(blocks):
                    mx = wide(jnp.max(logits(i, rows), axis=1, keepdims=True))
                    a_ref[h2, rows, :] = m_ref[h2, rows, :]
                    m_ref[h2, rows, :] = jnp.maximum(m_ref[h2, rows, :], mx)
                for i, rows in enumerate(blocks):
                    m_new = m_ref[h2, rows, :]
                    p = jnp.exp(logits(i, rows) - jnp.tile(m_new, (1, T // LANES)))
                    alpha = jnp.exp(a_ref[h2, rows, :] - m_new)
                    a_ref[h2, rows, :] = alpha
                    l_ref[h2, rows, :] = (alpha * l_ref[h2, rows, :]
                                          + wide(jnp.sum(p, axis=1, keepdims=True)))
                    p_ref[h2, rows, :] = p.astype(BF16)
                acc_ref[h2] = (a_ref[h2, :, :HEAD_DIM] * acc_ref[h2]
                               + _dot(p_ref[h2], v_ref[ks, hl]))

        logits_into(sa_ref, 0)

        def pair(i, _):
            logits_into(sb_ref, 2 * i + 1)
            consume(sa_ref, 2 * i, False)
            logits_into(sa_ref, 2 * i + 2)
            consume(sb_ref, 2 * i + 1, False)
            return 0

        lax.fori_loop(0, qi // 2, pair, 0)

        @pl.when(qi % 2 == 1)
        def _():
            logits_into(sb_ref, qi)
            consume(sa_ref, qi - 1, False)
            consume(sb_ref, qi, True)

        @pl.when(qi % 2 == 0)
        def _():
            consume(sa_ref, qi, True)

        for h2, hl in enumerate(lanes):
            o_ref[:, hl] = (acc_ref[h2] / l_ref[h2, :, :HEAD_DIM]).astype(BF16)
            lse_ref[:, hl] = m_ref[h2, :, :HEAD_DIM] + jnp.log(l_ref[h2, :, :HEAD_DIM])

    qblk = pl.BlockSpec((T, LANES), lambda h, i: (i, h))
    kv = pl.BlockSpec((S, LANES), lambda h, i: (0, h))
    return pl.pallas_call(
        body, name=name, grid=(N_CBLK, n_t),
        in_specs=[qblk, kv, kv, pl.BlockSpec((None, 2, S), lambda h, i: (h, 0, 0))],
        out_specs=[qblk, qblk],
        out_shape=[jax.ShapeDtypeStruct((S, D_MODEL), BF16),
                   jax.ShapeDtypeStruct((S, D_MODEL), F32)],
        scratch_shapes=[pltpu.VMEM((2, T, T), F32), pltpu.VMEM((2, T, T), F32),
                        pltpu.VMEM((2, T, T), BF16),
                        pltpu.VMEM((2, T, LANES), F32), pltpu.VMEM((2, T, LANES), F32),
                        pltpu.VMEM((2, T, HEAD_DIM), F32), pltpu.VMEM((2, T, LANES), F32)],
        compiler_params=_params("parallel", "parallel"),
    )(qs_, kn, vb, c_row)


def _attn_bwd(qs_, kn, vb, do, o, lse, c_row, name):
    S = qs_.shape[0]
    T, n_t = _attn_tiles(S)

    def body(q_ref, k_ref, v_ref, do_ref, o_ref, lse_ref, cr_ref,
             dq_ref, dk_ref, dv_ref, dc_ref, rho_ref, dd_ref):
        kj = pl.program_id(1)
        causal = _causal(T)
        lanes = [slice(h2 * HEAD_DIM, (h2 + 1) * HEAD_DIM) for h2 in range(2)]
        ones = [slice(h2 * HEAD_DIM, h2 * HEAD_DIM + 1) for h2 in range(2)]

        @pl.when(kj == 0)
        def _():
            dq_ref[...] = jnp.zeros_like(dq_ref)
            rho_ref[...] = jnp.zeros_like(rho_ref)
            p_sum = _head_group_matrix(1.0)

            def fill(ci, _):
                rows = pl.ds(pl.multiple_of(ci * T, T), T)
                dd_ref[rows, :] = _group_dot(do_ref[rows, :].astype(F32) * o_ref[rows, :].astype(F32),
                                             p_sum)
                return 0

            lax.fori_loop(0, n_t, fill, 0)

        kh = [k_ref[:, hl] for hl in lanes]
        vh = [v_ref[:, hl] for hl in lanes]
        ck = [cr_ref[h2:h2 + 1, :] for h2 in range(2)]

        def step(qi, carry, masked):
            qs = pl.ds(pl.multiple_of(qi * T, T), T)
            out = []
            for h2, hl in enumerate(lanes):
                dk, dv, dc = carry[h2]
                qh, doh = q_ref[qs, hl], do_ref[qs, hl]
                s = _dot_nt(qh, kh[h2]) - ck[h2]
                if masked:
                    s = jnp.where(causal, s, NEG_INF)
                p = jnp.exp(s - lse_ref[qs, ones[h2]])
                ds = p * (_dot_nt(doh, vh[h2]) - dd_ref[qs, ones[h2]])
                dsb = ds.astype(BF16)
                dq_ref[qs, hl] += _dot(dsb, kh[h2])
                rho_ref[qs, hl] += jnp.broadcast_to(jnp.sum(ds, axis=1, keepdims=True),
                                                    (T, HEAD_DIM))
                out.append((dk + _dot_tn(dsb, qh), dv + _dot_tn(p.astype(BF16), doh),
                            dc - jnp.sum(ds, axis=0, keepdims=True)))
            return tuple(out)

        init = tuple((jnp.zeros((T, HEAD_DIM), F32), jnp.zeros((T, HEAD_DIM), F32),
                      jnp.zeros((1, T), F32)) for _ in lanes)
        carry = step(kj, init, True)
        carry = lax.fori_loop(kj + 1, n_t, lambda qi, c: step(qi, c, False), carry)
        for h2, ((dk, dv, dc), hl) in enumerate(zip(carry, lanes)):
            dk_ref[:, hl] = dk
            dv_ref[:, hl] = dv.astype(BF16)
            dc_ref[h2:h2 + 1, :] = dc

    kblk = pl.BlockSpec((T, LANES), lambda h, j: (j, h))
    full = pl.BlockSpec((S, LANES), lambda h, j: (0, h))
    crow = pl.BlockSpec((None, 2, T), lambda h, j: (h, 0, j))
    wide = jax.ShapeDtypeStruct((S, D_MODEL), F32)
    return pl.pallas_call(
        body, name=name, grid=(N_CBLK, n_t),
        in_specs=[full, kblk, kblk, full, full, full, crow],
        out_specs=[full, kblk, kblk, crow, full],
        out_shape=[wide, wide, jax.ShapeDtypeStruct((S, D_MODEL), BF16),
                   jax.ShapeDtypeStruct((N_CBLK, 2, S), F32), wide],
        scratch_shapes=[pltpu.VMEM((S, LANES), F32)],
        compiler_params=_params("parallel", "arbitrary"),
    )(qs_, kn, vb, do, o, lse, c_row)


ALL_PEERS = tuple(range(1, N_DEV))
NEAR_PEERS = (1, 2, 4, 6)
FAR_CHIPS = (2, 4, 6)


def _landing_shapes(arrays, gathers):
    return [jax.ShapeDtypeStruct((N_DEV,) + a.shape if g else a.shape, a.dtype)
            for a, g in zip(arrays, gathers)]


def _my_index():
    return 4 * lax.axis_index("x") + 2 * lax.axis_index("y") + lax.axis_index("c")


def _own_copies(srcs, lands, gathers, sems):
    me = _my_index()
    return [pltpu.make_async_copy(src if g else src.at[me], land.at[me], sems.at[a])
            for a, (src, land, g) in enumerate(zip(srcs, lands, gathers))]


def _peer_copies(srcs, lands, gathers, send_sems, recv_sems, ks=ALL_PEERS):
    x, y, c = lax.axis_index("x"), lax.axis_index("y"), lax.axis_index("c")
    me = 4 * x + 2 * y + c
    out = []
    for j, k in enumerate(ks):
        to = (1 - x if k & 4 else x, 1 - y if k & 2 else y, 1 - c if k & 1 else c)
        peer = 4 * to[0] + 2 * to[1] + to[2]
        for a, (src, land, g) in enumerate(zip(srcs, lands, gathers)):
            sem = a * len(ks) + j
            src_blk = src if g else src.at[peer]

            def copy(slot, src_blk=src_blk, land=land, sem=sem, to=to):
                return pltpu.make_async_remote_copy(
                    src_ref=src_blk, dst_ref=land.at[slot], send_sem=send_sems.at[sem],
                    recv_sem=recv_sems.at[sem], device_id=to,
                    device_id_type=pl.DeviceIdType.MESH)

            out.append((k, a, copy(me), copy(peer)))
    return out


def _forward_copies(lands, send_sems, recv_sems):
    x, y, c = lax.axis_index("x"), lax.axis_index("y"), lax.axis_index("c")
    out = []
    for j, f in enumerate(FAR_CHIPS):
        chip = 4 * (1 - x if f & 4 else x) + 2 * (1 - y if f & 2 else y)
        for a, land in enumerate(lands):
            sem = a * len(FAR_CHIPS) + j

            def copy(slot, land=land, sem=sem):
                return pltpu.make_async_remote_copy(
                    src_ref=land.at[slot], dst_ref=land.at[slot], send_sem=send_sems.at[sem],
                    recv_sem=recv_sems.at[sem], device_id=(x, y, 1 - c),
                    device_id_type=pl.DeviceIdType.MESH)

            out.append((f, a, copy(chip + c), copy(chip + 1 - c)))
    return out


def _exchange(arrays, gathers, name, two_level=False):
    n = len(arrays)
    ks = NEAR_PEERS if two_level else ALL_PEERS
    assert not two_level or all(gathers)

    def body(*refs):
        ins, outs = refs[:n], refs[n:2 * n]
        send_sems, recv_sems, own_sems, fwd_send_sems, fwd_recv_sems = refs[2 * n:]
        own = _own_copies(ins, outs, gathers, own_sems)
        for cp in own:
            cp.start()
        copies = _peer_copies(ins, outs, gathers, send_sems, recv_sems, ks)
        for _, _, send, _ in copies:
            send.start()
        passed = {}
        if two_level:
            passed = {(f, a): (send, arrival)
                      for f, a, send, arrival in _forward_copies(outs, fwd_send_sems, fwd_recv_sems)}
        for k, a, _, arrival in copies:
            arrival.wait_recv()
            if (k, a) in passed:
                passed[k, a][0].start()
        for send, arrival in passed.values():
            arrival.wait_recv()
            send.wait_send()
        for _, _, send, _ in copies:
            send.wait_send()
        for cp in own:
            cp.wait()

    hbm = pl.BlockSpec(memory_space=pl.ANY)
    return pl.pallas_call(
        body, name=name,
        in_specs=[hbm] * n, out_specs=[hbm] * n, out_shape=_landing_shapes(arrays, gathers),
        scratch_shapes=[pltpu.SemaphoreType.DMA((n * len(ks),)),
                        pltpu.SemaphoreType.DMA((n * len(ks),)),
                        pltpu.SemaphoreType.DMA((n,)),
                        pltpu.SemaphoreType.DMA((n * len(FAR_CHIPS),)),
                        pltpu.SemaphoreType.DMA((n * len(FAR_CHIPS),))],
        compiler_params=pltpu.CompilerParams(has_side_effects=True),
    )(*arrays)


_HBM = pl.BlockSpec(memory_space=pltpu.HBM)
_SEM = pl.BlockSpec(memory_space=pltpu.SEMAPHORE)
_ANY = pl.BlockSpec(memory_space=pl.ANY)
_DATAFLOW = pltpu.SideEffectType.DATAFLOW_SIDE_EFFECTING


def _in_hbm(a):
    return pltpu.with_memory_space_constraint(a, pltpu.HBM)


def _exchange_start(arrays, gathers, after, name, ks=ALL_PEERS):
    n = len(arrays)
    lands = [lax.empty(s.shape, s.dtype) for s in _landing_shapes(arrays, gathers)]

    def body(*refs):
        srcs, dsts = refs[:n], refs[n:2 * n]
        send_sems, recv_sems, own_sems = refs[2 * n + 1:2 * n + 4]
        token = refs[-1]
        for cp in _own_copies(srcs, dsts, gathers, own_sems):
            cp.start()
        for _, _, send, _ in _peer_copies(srcs, dsts, gathers, send_sems, recv_sems, ks):
            send.start()
        token[...] = jnp.zeros_like(token)

    hbm_like = [pltpu.HBM(a.shape, a.dtype) for a in list(arrays) + lands]
    res = pl.pallas_call(
        body, name=name,
        in_specs=[_HBM] * (2 * n) + [_ANY],
        out_specs=(_SEM, _SEM, _SEM, *[_HBM] * (2 * n), pl.BlockSpec(memory_space=pltpu.VMEM)),
        out_shape=(pltpu.SemaphoreType.DMA((n * len(ks),)), pltpu.SemaphoreType.DMA((n * len(ks),)),
                   pltpu.SemaphoreType.DMA((n,)), *hbm_like,
                   jax.ShapeDtypeStruct((8, LANES), F32)),
        input_output_aliases={i: 3 + i for i in range(2 * n)},
        compiler_params=pltpu.CompilerParams(has_side_effects=_DATAFLOW),
    )(*[_in_hbm(a) for a in list(arrays) + lands], after)
    return (res[0], res[1], res[2], res[3:3 + n], res[3 + n:3 + 2 * n]), res[-1]


def _exchange_wait(started, gathers, after, name, ks=ALL_PEERS):
    send_sems, recv_sems, own_sems, arrays, lands = started
    n = len(arrays)

    def body(*refs):
        srcs, dsts = refs[:n], refs[n:2 * n]
        for _, _, send, arrival in _peer_copies(srcs, dsts, gathers, refs[2 * n], refs[2 * n + 1],
                                                ks):
            arrival.wait_recv()
            send.wait_send()
        for cp in _own_copies(srcs, dsts, gathers, refs[2 * n + 2]):
            cp.wait()

    hbm_like = [pltpu.HBM(a.shape, a.dtype) for a in list(arrays) + list(lands)]
    res = pl.pallas_call(
        body, name=name,
        in_specs=[_HBM] * (2 * n) + [_SEM, _SEM, _SEM, _ANY],
        out_specs=[_HBM] * (2 * n), out_shape=hbm_like,
        input_output_aliases={i: i for i in range(2 * n)},
        compiler_params=pltpu.CompilerParams(has_side_effects=_DATAFLOW),
    )(*arrays, *lands, send_sems, recv_sems, own_sems, after)
    return res[n:]


def _forward_start(lands, after, name):
    n = len(lands)

    def body(*refs):
        send_sems, recv_sems = refs[n + 1:n + 3]
        for _, _, send, _ in _forward_copies(refs[:n], send_sems, recv_sems):
            send.start()
        refs[-1][...] = jnp.zeros_like(refs[-1])

    n_sem = n * len(FAR_CHIPS)
    res = pl.pallas_call(
        body, name=name,
        in_specs=[_HBM] * n + [_ANY],
        out_specs=(_SEM, _SEM, *[_HBM] * n, pl.BlockSpec(memory_space=pltpu.VMEM)),
        out_shape=(pltpu.SemaphoreType.DMA((n_sem,)), pltpu.SemaphoreType.DMA((n_sem,)),
                   *[pltpu.HBM(a.shape, a.dtype) for a in lands],
                   jax.ShapeDtypeStruct((8, LANES), F32)),
        input_output_aliases={i: 2 + i for i in range(n)},
        compiler_params=pltpu.CompilerParams(has_side_effects=_DATAFLOW),
    )(*[_in_hbm(a) for a in lands], after)
    return (res[0], res[1], res[2:2 + n]), res[-1]


def _forward_wait(started, after, name):
    send_sems, recv_sems, lands = started
    n = len(lands)

    def body(*refs):
        for _, _, send, arrival in _forward_copies(refs[:n], refs[n], refs[n + 1]):
            arrival.wait_recv()
            send.wait_send()

    return pl.pallas_call(
        body, name=name,
        in_specs=[_HBM] * n + [_SEM, _SEM, _ANY],
        out_specs=[_HBM] * n, out_shape=[pltpu.HBM(a.shape, a.dtype) for a in lands],
        input_output_aliases={i: i for i in range(n)},
        compiler_params=pltpu.CompilerParams(has_side_effects=_DATAFLOW),
    )(*lands, send_sems, recv_sems, after)


def _reduce_adamw(parts, w, m, v, name):
    n_layer = len(parts)
    n, R, C = parts[0].shape
    tr = 256 if R % 256 == 0 else R
    n_t = R // tr

    def body(*refs):
        p_refs = refs[:n_layer]
        w_ref, m_ref, v_ref, g_ref, d_ref, nm_ref, nv_ref = refs[n_layer:]

        def update(p_ref):
            g = p_ref[0].astype(F32)
            for s in range(1, n):
                g = g + p_ref[s].astype(F32)
            g_ref[...] = g
            m_new = ADAM_B1 * m_ref[...] + (1.0 - ADAM_B1) * g
            v_new = ADAM_B2 * v_ref[...] + (1.0 - ADAM_B2) * (g * g)
            nm_ref[...] = m_new
            nv_ref[...] = v_new
            m_hat = m_new / (1.0 - ADAM_B1 ** ADAM_STEP)
            v_hat = v_new / (1.0 - ADAM_B2 ** ADAM_STEP)
            d_ref[...] = -ADAM_LR * (m_hat / (jnp.sqrt(v_hat) + ADAM_EPS) + ADAM_WD * w_ref[...])

        for layer, p_ref in enumerate(p_refs):
            pl.when(pl.program_id(0) == layer)(functools.partial(update, p_ref))

    def parts_spec(layer):
        def index(l, i):
            return 0, jnp.where(l < layer, 0, jnp.where(l > layer, n_t - 1, i)), 0
        return pl.BlockSpec((n, tr, C), index)

    blk = pl.BlockSpec((None, tr, C), lambda l, i: (l, i, 0))
    out = jax.ShapeDtypeStruct((n_layer, R, C), F32)
    return pl.pallas_call(
        body, name=name, grid=(n_layer, n_t),
        in_specs=[parts_spec(layer) for layer in range(n_layer)] + [blk, blk, blk],
        out_specs=[blk] * 4, out_shape=[out] * 4,
        compiler_params=_params("arbitrary", "arbitrary"),
    )(*parts, w, m, v)


def _pack(arrays):
    flat = jnp.concatenate([a.reshape(-1).astype(F32) for a in arrays])
    pad = (-flat.shape[0]) % (8 * LANES)
    return jnp.pad(flat, (0, pad)).reshape(-1, LANES)


def _unpack(buf, shapes):
    flat = buf.reshape(-1)
    out, off = [], 0
    for shp in shapes:
        size = 1
        for s in shp:
            size *= s
        out.append(flat[off:off + size].reshape(shp))
        off += size
    return out


def _block_diag_pairs(w):
    w = w.reshape(N_CBLK, 2, LRU_BLOCK_DIM, LRU_BLOCK_DIM)
    z = jnp.zeros_like(w[:, 0])
    top = jnp.concatenate([w[:, 0], z], axis=2)
    bot = jnp.concatenate([z, w[:, 1]], axis=2)
    return jnp.concatenate([top, bot], axis=1)


def _diag_pairs(m):
    h = LRU_BLOCK_DIM
    return jnp.stack([m[:, :h, :h], m[:, h:, h:]], axis=1).reshape(2 * N_CBLK, h, h)


SMALL = ("mlp_norm", "lru_conv_b", "lru_w_r", "lru_b_r", "lru_w_i", "lru_b_i",
         "lru_lambda", "fox_b_f", "fox_q_gain", "fox_k_gain")
WEIGHTS = ("mix_norm", "mlp_norm", "mlp_w1", "mlp_w2", "lru_w_in", "lru_conv_w", "lru_conv_b",
           "lru_w_r", "lru_b_r", "lru_w_i", "lru_b_i", "lru_lambda", "lru_w_out", "fox_w_in",
           "fox_b_f", "fox_q_gain", "fox_k_gain", "fox_w_out")


def kernel(x, mix_norm, mlp_norm, mlp_w1, mlp_w2, lru_w_in, lru_conv_w, lru_conv_b, lru_w_r, lru_b_r, lru_w_i, lru_b_i, lru_lambda, lru_w_out, fox_w_in, fox_b_f, fox_q_gain, fox_k_gain, fox_w_out, loss_target, m_mix_norm, m_mlp_norm, m_mlp_w1, m_mlp_w2, m_lru_w_in, m_lru_conv_w, m_lru_conv_b, m_lru_w_r, m_lru_b_r, m_lru_w_i, m_lru_b_i, m_lru_lambda, m_lru_w_out, m_fox_w_in, m_fox_b_f, m_fox_q_gain, m_fox_k_gain, m_fox_w_out, v_mix_norm, v_mlp_norm, v_mlp_w1, v_mlp_w2, v_lru_w_in, v_lru_conv_w, v_lru_conv_b, v_lru_w_r, v_lru_b_r, v_lru_w_i, v_lru_b_i, v_lru_lambda, v_lru_w_out, v_fox_w_in, v_fox_b_f, v_fox_q_gain, v_fox_k_gain, v_fox_w_out):
    w_in = dict(mix_norm=mix_norm, mlp_norm=mlp_norm, mlp_w1=mlp_w1, mlp_w2=mlp_w2,
                lru_w_in=lru_w_in, lru_conv_w=lru_conv_w, lru_conv_b=lru_conv_b, lru_w_r=lru_w_r,
                lru_b_r=lru_b_r, lru_w_i=lru_w_i, lru_b_i=lru_b_i, lru_lambda=lru_lambda,
                lru_w_out=lru_w_out, fox_w_in=fox_w_in, fox_b_f=fox_b_f, fox_q_gain=fox_q_gain,
                fox_k_gain=fox_k_gain, fox_w_out=fox_w_out)
    m_in = dict(mix_norm=m_mix_norm, mlp_norm=m_mlp_norm, mlp_w1=m_mlp_w1, mlp_w2=m_mlp_w2,
                lru_w_in=m_lru_w_in, lru_conv_w=m_lru_conv_w, lru_conv_b=m_lru_conv_b,
                lru_w_r=m_lru_w_r, lru_b_r=m_lru_b_r, lru_w_i=m_lru_w_i, lru_b_i=m_lru_b_i,
                lru_lambda=m_lru_lambda, lru_w_out=m_lru_w_out, fox_w_in=m_fox_w_in,
                fox_b_f=m_fox_b_f, fox_q_gain=m_fox_q_gain, fox_k_gain=m_fox_k_gain,
                fox_w_out=m_fox_w_out)
    v_in = dict(mix_norm=v_mix_norm, mlp_norm=v_mlp_norm, mlp_w1=v_mlp_w1, mlp_w2=v_mlp_w2,
                lru_w_in=v_lru_w_in, lru_conv_w=v_lru_conv_w, lru_conv_b=v_lru_conv_b,
                lru_w_r=v_lru_w_r, lru_b_r=v_lru_b_r, lru_w_i=v_lru_w_i, lru_b_i=v_lru_b_i,
                lru_lambda=v_lru_lambda, lru_w_out=v_lru_w_out, fox_w_in=v_fox_w_in,
                fox_b_f=v_fox_b_f, fox_q_gain=v_fox_q_gain, fox_k_gain=v_fox_k_gain,
                fox_w_out=v_fox_w_out)
    D = D_MODEL
    S = x.shape[1]
    x0, target = x[0], loss_target[0]
    me = 4 * lax.axis_index("x") + 2 * lax.axis_index("y") + lax.axis_index("c")

    def bf16(a):
        return a.astype(BF16)

    (lru_in_g,) = _exchange([bf16(lru_w_in[0])], [True], "gather_lru_in", two_level=True)
    gather_lru, tok = _exchange_start([bf16(lru_w_out[0]), lru_conv_w[0]], [True] * 2, lru_in_g,
                                      "gather_lru_start")
    gather_mlp0, tok = _exchange_start([bf16(mlp_w1[0]), bf16(mlp_w2[0])], [True] * 2, tok,
                                       "gather_mlp0_start", NEAR_PEERS)
    gather_fox, tok = _exchange_start([bf16(fox_w_in[0]), bf16(fox_w_out[0])], [True] * 2, tok,
                                      "gather_fox_start")
    gather_mlp1, tok = _exchange_start([bf16(mlp_w1[1]), bf16(mlp_w2[1])], [True] * 2, tok,
                                       "gather_mlp1_start", NEAR_PEERS)

    def pass_on(started, after, name):
        lands = _exchange_wait(started, [True] * 2, after, name + "_wait", NEAR_PEERS)
        return _forward_start(lands, after, name + "_pass_start")
    wr =_block_diag_pairs(lru_w_r[0]).astype(BF16)
    wi = _block_diag_pairs(lru_w_i[0]).astype(BF16)
    b_r, b_i = lru_b_r.reshape(1, D), lru_b_i.reshape(1, D)
    q_gain, k_gain = jnp.tile(fox_q_gain, (1, 2)), jnp.tile(fox_k_gain, (1, 2))
    b_f = jnp.pad(fox_b_f, ((0, 0), (0, LANES - N_HEADS)))
    g_mix0, g_mix1 = mix_norm[0:1] + tok[0, 0], mix_norm[1:2]
    g_mlp0, g_mlp1 = mlp_norm[0:1], mlp_norm[1:2]

    (u0,), h0 = _norm_matmul(x0, g_mix0, [lru_in_g], "lru_in_proj")
    lru_out_g, conv_g = _exchange_wait(gather_lru, [True] * 2, u0, "gather_lru_wait")
    lru_out_w = lru_out_g.reshape(D, D)
    conv_w = conv_g.transpose(1, 0, 2).reshape(CONV_WIDTH, D)
    y_lru, hs =_lru_fwd(u0, conv_w, lru_conv_b, wr, b_r, wi, b_i, lru_lambda, "lru_core")
    pass_mlp0, tok = pass_on(gather_mlp0, y_lru, "gather_mlp0")
    x1 = _matmul_res(y_lru, lru_out_w, x0, "lru_out_proj", tok)
    w1g0, w2g0 = _forward_wait(pass_mlp0, x1, "gather_mlp0_pass_wait")
    x2, h1, r1 = _mlp_fwd(x1, g_mlp0, w1g0, w2g0, "mlp0")
    fox_in_g, fox_out_g = _exchange_wait(gather_fox, [True] * 2, x2, "gather_fox_wait")
    fox_out_w = fox_out_g.reshape(D, D)
    fox_full = jnp.concatenate([fox_in_g[d] for d in range(N_DEV)], axis=1)
    wqkv = fox_full[:, :3 * D].reshape(D, 3, D).transpose(1, 0, 2)
    wf = jnp.pad(fox_full[:, 3 * D:], ((0, 0), (0, LANES - N_HEADS)))[None]
    uq, uk, f, h2, qn, kn, vb = _fox_in_proj(x2, g_mix1, wqkv, wf, q_gain, k_gain, "fox_in_proj")
    c_col = _forget_fwd(f, b_f, "fox_forget")
    c_row = c_col[:, :N_HEADS].T.reshape(N_CBLK, 2, S)
    o, lse = _attn_fwd(qn, kn, vb, c_row, "fox_attn")
    pass_mlp1, tok = pass_on(gather_mlp1, o, "gather_mlp1")
    x3 = _matmul_res(o, fox_out_w, x2, "fox_out_proj", tok)
    w1g1, w2g1 = _forward_wait(pass_mlp1, x3, "gather_mlp1_pass_wait")
    loss_local, dx4, h3, r3 = _mlp_fwd(x3, g_mlp1, w1g1, w2g1, "mlp1", target)

    dx3, dg_mlp1, da3 = _mlp_bwd(dx4, x3, g_mlp1, r3, w1g1, w2g1, "mlp1_bwd")
    dw1_1 = _matmul_tn(h3, da3, "mlp1_dw1", cols=2, col_blocks=N_DEV)
    dw2_1 = _matmul_tn(r3, dx4, "mlp1_dw2", rows=2, a_square=True, tm=1024).reshape(N_DEV, -1, D)
    grads_mlp1, tok = _exchange_start([dw1_1, dw2_1], [False] * 2, tok, "grads_mlp1_start")
    do = _matmul_nt(dx3, fox_out_w, "fox_out_bwd", BF16, tok)
    d_fox_out = _matmul_tn(o, dx3, "fox_out_dw").reshape(N_DEV, -1, D)
    dqn, dkn, dv, dc_row, rho = _attn_bwd(qn, kn, vb, do, o, lse, c_row, "fox_attn_bwd")
    duq, duk, dq_gain, dk_gain = _qk_bwd(uq, uk, dqn, dkn, q_gain, k_gain, "fox_qk_norm_bwd")
    dc_k = jnp.pad(dc_row.reshape(N_HEADS, S).T, ((0, 0), (0, LANES - N_HEADS)))
    df, db_f = _forget_bwd(dc_k, rho, f, b_f, "fox_forget_bwd")
    dx2, dg_mix1 = _proj_bwd([[duq, duk, dv], [df]], [wqkv, wf], x2, g_mix1, dx3, "fox_in_bwd")
    d_fox_in = jnp.concatenate(
        [_matmul_tn(h2, duq, "fox_in_dwq"), _matmul_tn(h2, duk, "fox_in_dwk"),
         _matmul_tn(h2, dv, "fox_in_dwv"), _matmul_tn(h2, df, "fox_in_dwf")[:, :N_HEADS]], axis=1)
    shard = (3 * D + N_HEADS) // N_DEV
    d_fox_in = jnp.stack([d_fox_in[:, d * shard:(d + 1) * shard] for d in range(N_DEV)])
    grads_fox, tok = _exchange_start([d_fox_in, d_fox_out], [False] * 2, tok, "grads_fox_start")
    dx1, dg_mlp0, da1 = _mlp_bwd(dx2, x1, g_mlp0 + tok[0, 0], r1, w1g0, w2g0, "mlp0_bwd")
    dw1_0 = _matmul_tn(h1, da1, "mlp0_dw1", cols=2, col_blocks=N_DEV)
    dw2_0 = _matmul_tn(r1, dx2, "mlp0_dw2", rows=2, a_square=True, tm=1024).reshape(N_DEV, -1, D)
    grads_mlp0, tok = _exchange_start([dw1_0, dw2_0], [False] * 2, tok, "grads_mlp0_start")
    dy_lru = _matmul_nt(dx1, lru_out_w, "lru_out_bwd", F32, tok)
    d_lru_out = _matmul_tn(y_lru, dx1, "lru_out_dw").reshape(N_DEV, -1, D)
    dgp, dxb, d_conv_w, d_conv_b, d_b_r, d_b_i, d_lam, d_wr, d_wi = _lru_bwd(
        dy_lru, u0, hs, conv_w, lru_conv_b, wr, b_r, wi, b_i, lru_lambda, "lru_core_bwd")

    small_grads = dict(
        mlp_norm=jnp.concatenate([dg_mlp0, dg_mlp1], axis=0),
        lru_conv_b=d_conv_b, lru_w_r=_diag_pairs(d_wr), lru_b_r=d_b_r, lru_w_i=_diag_pairs(d_wi),
        lru_b_i=d_b_i, lru_lambda=d_lam, fox_b_f=db_f[:, :N_HEADS],
        fox_q_gain=dq_gain[:, :HEAD_DIM], fox_k_gain=dk_gain[:, :HEAD_DIM])
    small_partial = _pack([dg_mix1] + [small_grads[n] for n in SMALL] + [d_conv_w])
    grads_lru_out, tok = _exchange_start([d_lru_out, small_partial], [False, True], tok,
                                         "grads_lru_out_start")
    dx0, dg_mix0 = _proj_bwd([[dgp, dxb]], [lru_in_g], x0, mix_norm[0:1] + tok[0, 0], dx1,
                             "lru_in_bwd")
    d_lru_in = jnp.concatenate([_matmul_tn(h0, dgp, "lru_in_dw_gate", col_blocks=4),
                                _matmul_tn(h0, dxb, "lru_in_dw_x", col_blocks=4)], axis=0)
    grads_lru_in, tok = _exchange_start([d_lru_in, dg_mix0], [False, True], tok,
                                        "grads_lru_in_start")

    grads, deltas, new_m, new_v = {}, {}, {}, {}

    def update(name, parts):
        w, m, v = w_in[name], m_in[name], v_in[name]
        shape = w.shape
        stacked = (len(parts), -1, shape[-1])
        w3 = w.reshape(stacked)
        res = _reduce_adamw([p.reshape((N_DEV,) + w3.shape[1:]) for p in parts], w3,
                            m.reshape(stacked), v.reshape(stacked), "adamw_" + name)
        return [r.reshape(shape) for r in res]

    def store(name, res):
        grads[name], deltas[name], new_m[name], new_v[name] = res

    p_w1_1, p_w2_1 = _exchange_wait(grads_mlp1, [False] * 2, tok, "grads_mlp1_wait")
    p_fox_in, p_fox_out = _exchange_wait(grads_fox, [False] * 2, p_w1_1, "grads_fox_wait")
    store("fox_w_in", update("fox_w_in", [p_fox_in]))
    store("fox_w_out", update("fox_w_out", [p_fox_out]))
    p_w1_0, p_w2_0 = _exchange_wait(grads_mlp0, [False] * 2, grads["fox_w_out"], "grads_mlp0_wait")
    store("mlp_w1", update("mlp_w1", [p_w1_0, p_w1_1]))
    store("mlp_w2", update("mlp_w2", [p_w2_0, p_w2_1]))
    p_lru_out, p_small = _exchange_wait(grads_lru_out, [False, True], grads["mlp_w2"],
                                        "grads_lru_out_wait")
    store("lru_w_out", update("lru_w_out", [p_lru_out]))
    p_lru_in, p_mix0 = _exchange_wait(grads_lru_in, [False, True], grads["lru_w_out"],
                                      "grads_lru_in_wait")
    store("lru_w_in", update("lru_w_in", [p_lru_in]))

    mix0 = [r[0] for r in _reduce_adamw([p_mix0], mix_norm[None, 0:1], m_mix_norm[None, 0:1],
                                        v_mix_norm[None, 0:1], "adamw_mix0")]
    packed = lambda src, first: _pack([first] + [src[n] for n in SMALL]
                                      + [jnp.zeros((CONV_WIDTH, D))])[None]
    small_shapes = [(1, D)] + [w_in[n].shape for n in SMALL]
    n_small = sum(math.prod(s) for s in small_shapes)
    res_small = _reduce_adamw([p_small], packed(w_in, mix_norm[1:2]), packed(m_in, m_mix_norm[1:2]),
                              packed(v_in, v_mix_norm[1:2]), "adamw_small")
    for name, *vals in zip(("mix1",) + SMALL, *[_unpack(r, small_shapes) for r in res_small]):
        if name == "mix1":
            vals = [jnp.concatenate([r0, r1], axis=0) for r0, r1 in zip(mix0, vals)]
            name = "mix_norm"
        store(name, vals)
    conv_parts = p_small.reshape(N_DEV, -1)[:, n_small:n_small + CONV_WIDTH * D]
    conv_parts = conv_parts.reshape(N_DEV, CONV_WIDTH, N_DEV, LANES)
    conv_parts = lax.dynamic_index_in_dim(conv_parts, me, axis=2, keepdims=False)
    store("lru_conv_w", update("lru_conv_w", [conv_parts]))

    loss = lax.psum(loss_local[0, 0], ("x", "y", "c"))
    return (loss, dx0[None], *[grads[n] for n in WEIGHTS], *[deltas[n] for n in WEIGHTS],
            *[new_m[n] for n in WEIGHTS], *[new_v[n] for n in WEIGHTS])
```

```python
import functools
import math

import jax
import jax.numpy as jnp
from jax import lax
from jax.experimental import pallas as pl
from jax.experimental.pallas import tpu as pltpu

F32 = jnp.float32
BF16 = jnp.bfloat16

N_DEV = 8
D_MODEL = 1024
D_FF = 4096
N_HEADS = 16
HEAD_DIM = 64
LRU_BLOCK_DIM = 64
CONV_WIDTH = 4
LRU_C = 8.0
EPS = 1e-6
NEG_INF = -1e30
ATTN_SCALE = HEAD_DIM ** -0.5
LANES = 128
N_CBLK = D_MODEL // LANES
VMEM_LIMIT = 52 * 2 ** 20

ADAM_LR = 0.001
ADAM_B1 = 0.9
ADAM_B2 = 0.999
ADAM_EPS = 1e-08
ADAM_WD = 0.01
ADAM_STEP = 10

_NT = (((1,), (1,)), ((), ()))
_TN = (((0,), (0,)), ((), ()))


def _params(*sem):
    return pltpu.CompilerParams(dimension_semantics=sem, vmem_limit_bytes=VMEM_LIMIT)


def _resident(shape):
    zeros = (0,) * len(shape)
    return pl.BlockSpec(shape, lambda *_: zeros, pipeline_mode=pl.Buffered(1))


def _dot(a, b):
    return jnp.dot(a, b, preferred_element_type=F32)


def _dot_nt(a, b):
    return lax.dot_general(a, b, _NT, preferred_element_type=F32)


def _dot_tn(a, b):
    return lax.dot_general(a, b, _TN, preferred_element_type=F32)


def _sigmoid(x):
    return 1.0 / (1.0 + jnp.exp(-x))


def _log_sigmoid(x):
    return -(jnp.maximum(-x, 0.0) + jnp.log1p(jnp.exp(-jnp.abs(x))))


def _expm1(x):
    poly = x * (1.0 + x * (0.5 + x * (1.0 / 6.0 + x * (1.0 / 24.0 + x * (1.0 / 120.0)))))
    return jnp.where(jnp.abs(x) < 0.1, poly, jnp.exp(x) - 1.0)


_GELU_K = 0.7978845608028654


def _gelu(x):
    return 0.5 * x * (1.0 + jnp.tanh(_GELU_K * (x + 0.044715 * (x * x * x))))


def _gelu_grad(x):
    t = jnp.tanh(_GELU_K * (x + 0.044715 * (x * x * x)))
    return 0.5 * (1.0 + t) + 0.5 * x * (1.0 - t * t) * (_GELU_K * (1.0 + 3 * 0.044715 * x * x))


def _rms_scale(x):
    return lax.rsqrt(jnp.mean(x * x, axis=-1, keepdims=True) + EPS)


def _norm_bwd(dh, x, g):
    rs = _rms_scale(x)
    xhat = x * rs
    dxhat = dh * g
    dx = rs * (dxhat - xhat * jnp.mean(dxhat * xhat, axis=-1, keepdims=True))
    return dx, jnp.sum(dh * xhat, axis=0, keepdims=True)


def _token_tile(S, want):
    tm = min(S, want)
    assert S % tm == 0
    return tm


def _norm_matmul(x, g, ws, name, tm=512):
    S, D = x.shape
    tm = _token_tile(S, tm)
    n = len(ws)

    def body(x_ref, g_ref, *refs):
        w_refs, o_refs, h_ref = refs[:n], refs[n:2 * n], refs[2 * n]
        xv = x_ref[...]
        h = (xv * _rms_scale(xv) * g_ref[...]).astype(BF16)
        h_ref[...] = h
        for w_ref, o_ref in zip(w_refs, o_refs):
            nb, _, nw = w_ref.shape
            for d in range(nb):
                o_ref[:, d * nw:(d + 1) * nw] = _dot(h, w_ref[d])

    widths = [w.shape[0] * w.shape[2] for w in ws]
    outs = pl.pallas_call(
        body, name=name, grid=(S // tm,),
        in_specs=[pl.BlockSpec((tm, D), lambda i: (i, 0)), _resident((1, D))]
        + [_resident(w.shape) for w in ws],
        out_specs=[pl.BlockSpec((tm, n_), lambda i: (i, 0)) for n_ in widths]
        + [pl.BlockSpec((tm, D), lambda i: (i, 0))],
        out_shape=[jax.ShapeDtypeStruct((S, n_), F32) for n_ in widths]
        + [jax.ShapeDtypeStruct((S, D), BF16)],
        compiler_params=_params("parallel"),
    )(x, g, *ws)
    return outs[:n], outs[n]


def _matmul_res(a, w, res, name, after, tm=512):
    S, K = a.shape
    N = w.shape[1]
    tm = _token_tile(S, tm)

    def body(a_ref, w_ref, r_ref, after_ref, o_ref):
        o_ref[...] = r_ref[...] + _dot(a_ref[...], w_ref[...])

    return pl.pallas_call(
        body, name=name, grid=(S // tm,),
        in_specs=[pl.BlockSpec((tm, K), lambda i: (i, 0)), _resident((K, N)),
                  pl.BlockSpec((tm, N), lambda i: (i, 0)), pl.BlockSpec(memory_space=pl.ANY)],
        out_specs=pl.BlockSpec((tm, N), lambda i: (i, 0)),
        out_shape=jax.ShapeDtypeStruct((S, N), F32),
        compiler_params=_params("parallel"),
    )(a, w, res, after)


def _matmul_nt(a, w, name, out_dtype, after, tm=1024):
    S, N = a.shape
    K = w.shape[0]
    tm = _token_tile(S, tm)

    def body(a_ref, w_ref, after_ref, o_ref):
        o_ref[...] = _dot_nt(a_ref[...].astype(BF16), w_ref[...]).astype(out_dtype)

    return pl.pallas_call(
        body, name=name, grid=(S // tm,),
        in_specs=[pl.BlockSpec((tm, N), lambda i: (i, 0)), _resident((K, N)),
                  pl.BlockSpec(memory_space=pl.ANY)],
        out_specs=pl.BlockSpec((tm, K), lambda i: (i, 0)),
        out_shape=jax.ShapeDtypeStruct((S, K), out_dtype),
        compiler_params=_params("parallel"),
    )(a, w, after)


def _proj_bwd(a_lists, w_list, x, g, res, name, tm=512):
    S, D = x.shape
    tm = _token_tile(S, tm)
    a_list = [a for group in a_lists for a in group]
    n, n_w = len(a_list), len(w_list)

    def body(*refs):
        a_refs, w_refs = list(refs[:n]), refs[n:n + n_w]
        x_ref, g_ref, r_ref, dx_ref, dg_ref = refs[n + n_w:]
        dh = jnp.zeros((tm, D), F32)
        for group, w_ref in zip(a_lists, w_refs):
            nw = w_ref.shape[2]
            d = 0
            for _ in group:
                a_ref = a_refs.pop(0)
                for j in range(a_ref.shape[1] // nw):
                    dh = dh + _dot_nt(a_ref[:, j * nw:(j + 1) * nw].astype(BF16), w_ref[d])
                    d += 1
        dx, dg = _norm_bwd(dh, x_ref[...], g_ref[...])
        dx_ref[...] = r_ref[...] + dx

        @pl.when(pl.program_id(0) == 0)
        def _():
            dg_ref[...] = jnp.zeros_like(dg_ref)
        dg_ref[...] += dg

    tok = lambda width: pl.BlockSpec((tm, width), lambda i: (i, 0))
    return pl.pallas_call(
        body, name=name, grid=(S // tm,),
        in_specs=[tok(a.shape[1]) for a in a_list] + [_resident(w.shape) for w in w_list]
        + [tok(D), _resident((1, D)), tok(D)],
        out_specs=[tok(D), pl.BlockSpec((1, D), lambda i: (0, 0))],
        out_shape=[jax.ShapeDtypeStruct((S, D), F32), jax.ShapeDtypeStruct((1, D), F32)],
        compiler_params=_params("arbitrary"),
    )(*a_list, *w_list, x, g, res)


def _matmul_tn(a, b, name, rows=1, cols=1, col_blocks=None, a_square=False, tm=2048):
    S, K = a.shape
    N = b.shape[1]
    tm = _token_tile(S, tm)
    n_tok = S // tm
    kr, nc = K // rows, N // cols

    def body(a_ref, b_ref, o_ref, acc_ref):
        av = a_ref[...]
        if a_square:
            av = av.astype(F32)
            av = av * av
        part = _dot_tn(av.astype(BF16), b_ref[...].astype(BF16))
        step = pl.program_id(2)

        @pl.when(step == 0)
        def _():
            acc_ref[...] = part

        @pl.when(step > 0)
        def _():
            acc_ref[...] += part

        @pl.when(step == n_tok - 1)
        def _():
            if col_blocks is None:
                o_ref[...] = acc_ref[...].astype(BF16)
            else:
                nw = N // col_blocks
                for d in range(col_blocks // cols):
                    o_ref[d] = acc_ref[:, d * nw:(d + 1) * nw].astype(BF16)

    if col_blocks is None:
        out_spec = pl.BlockSpec((kr, nc), lambda r, c, i: (r, c))
        out_shape = jax.ShapeDtypeStruct((K, N), BF16)
    else:
        assert rows == 1 and col_blocks % cols == 0
        per = col_blocks // cols
        out_spec = pl.BlockSpec((per, K, N // col_blocks), lambda r, c, i: (c, 0, 0))
        out_shape = jax.ShapeDtypeStruct((col_blocks, K, N // col_blocks), BF16)
    return pl.pallas_call(
        body, name=name, grid=(rows, cols, n_tok),
        in_specs=[pl.BlockSpec((tm, kr), lambda r, c, i: (i, r)),
                  pl.BlockSpec((tm, nc), lambda r, c, i: (i, c))],
        out_specs=out_spec, out_shape=out_shape,
        scratch_shapes=[pltpu.VMEM((kr, nc), F32)],
        compiler_params=_params("parallel", "parallel", "arbitrary"),
    )(a, b)


def _mlp_fwd(x, g, w1, w2, name, target=None, tm=512):
    S, D = x.shape
    nb, _, fb = w1.shape
    tm = _token_tile(S, tm)
    with_loss = target is not None

    def body(x_ref, g_ref, w1_ref, w2_ref, *refs):
        h_ref, r_ref = refs[-2:]
        xv = x_ref[...]
        h = (xv * _rms_scale(xv) * g_ref[...]).astype(BF16)
        h_ref[...] = h
        acc = xv
        for d in range(nb):
            r = jnp.maximum(_dot(h, w1_ref[d]), 0.0)
            r_ref[:, d * fb:(d + 1) * fb] = r.astype(BF16)
            acc = acc + _dot((r * r).astype(BF16), w2_ref[d])
        if not with_loss:
            refs[0][...] = acc
            return
        t_ref, loss_ref, dy_ref = refs[:3]
        err = acc - t_ref[...]
        dy_ref[...] = err / D

        @pl.when(pl.program_id(0) == 0)
        def _():
            loss_ref[...] = jnp.zeros_like(loss_ref)
        row_loss = jnp.mean(err * err, axis=1, keepdims=True)
        loss_ref[...] += 0.5 * jnp.sum(row_loss, axis=0, keepdims=True)

    tok = lambda width: pl.BlockSpec((tm, width), lambda i: (i, 0))
    saved_specs = [tok(D), tok(nb * fb)]
    saved_shapes = [jax.ShapeDtypeStruct((S, D), BF16), jax.ShapeDtypeStruct((S, nb * fb), BF16)]
    wide = jax.ShapeDtypeStruct((S, D), F32)
    if with_loss:
        head_specs = [pl.BlockSpec((1, 1), lambda i: (0, 0)), tok(D)]
        head_shapes = [jax.ShapeDtypeStruct((1, 1), F32), wide]
    else:
        head_specs, head_shapes = [tok(D)], [wide]
    return pl.pallas_call(
        body, name=name, grid=(S // tm,),
        in_specs=[tok(D), _resident((1, D)), _resident(w1.shape), _resident(w2.shape)]
        + ([tok(D)] if with_loss else []),
        out_specs=head_specs + saved_specs, out_shape=head_shapes + saved_shapes,
        compiler_params=_params("arbitrary" if with_loss else "parallel"),
    )(x, g, w1, w2, *([target] if with_loss else []))


def _mlp_bwd(dout, x, g, r, w1, w2, name, tm=512):
    S, D = x.shape
    nb, _, fb = w1.shape
    tm = _token_tile(S, tm)

    def body(do_ref, x_ref, g_ref, r_ref, w1_ref, w2_ref, dx_ref, dg_ref, da_ref, dob_ref):
        dov = do_ref[...]
        dob = dov.astype(BF16)
        dob_ref[...] = dob
        dh = jnp.zeros((tm, D), F32)
        for d in range(nb):
            dz = _dot_nt(dob, w2_ref[d])
            da = (dz * (2.0 * r_ref[:, d * fb:(d + 1) * fb].astype(F32))).astype(BF16)
            da_ref[:, d * fb:(d + 1) * fb] = da
            dh = dh + _dot_nt(da, w1_ref[d])
        dx, dg = _norm_bwd(dh, x_ref[...], g_ref[...])
        dx_ref[...] = dov + dx

        @pl.when(pl.program_id(0) == 0)
        def _():
            dg_ref[...] = jnp.zeros_like(dg_ref)
        dg_ref[...] += dg

    tok = lambda width: pl.BlockSpec((tm, width), lambda i: (i, 0))
    return pl.pallas_call(
        body, name=name, grid=(S // tm,),
        in_specs=[tok(D), tok(D), _resident((1, D)), tok(nb * fb), _resident(w1.shape),
                  _resident(w2.shape)],
        out_specs=[tok(D), pl.BlockSpec((1, D), lambda i: (0, 0)), tok(nb * fb), tok(D)],
        out_shape=[jax.ShapeDtypeStruct((S, D), F32), jax.ShapeDtypeStruct((1, D), F32),
                   jax.ShapeDtypeStruct((S, nb * fb), BF16), jax.ShapeDtypeStruct((S, D), BF16)],
        compiler_params=_params("arbitrary"),
    )(dout, x, g, r, w1, w2)


def _scan_chunk(a, b, row, T, reverse):
    s = 1
    while s < T:
        if reverse:
            keep, shift = row < T - s, T - s
        else:
            keep, shift = row >= s, s
        a_sh = jnp.where(keep, pltpu.roll(a, shift, 0), 1.0)
        b_sh = jnp.where(keep, pltpu.roll(b, shift, 0), 0.0)
        b = a * b_sh + b
        a = a * a_sh
        s *= 2
    return a, b


def _row_of(x, row, r):
    return jnp.sum(jnp.where(row == r, x, 0.0), axis=0, keepdims=True)


def _shift_down(x, prev, row, k):
    if k == 0:
        return x
    return jnp.where(row < k, pltpu.roll(prev, k, 0), pltpu.roll(x, k, 0))


def _shift_up(x, nxt, row, k, T):
    if k == 0:
        return x
    return jnp.where(row < T - k, pltpu.roll(x, T - k, 0), pltpu.roll(nxt, T - k, 0))


def _lru_gates(xb, prev_xb, row, cw_ref, cb, wr, br, wi, bi, ls):
    xc = cb + cw_ref[pl.ds(0, 1), :] * _shift_down(xb, prev_xb, row, 3)
    for k in (2, 1, 0):
        xc = xc + cw_ref[pl.ds(3 - k, 1), :] * _shift_down(xb, prev_xb, row, k)
    xcb = xc.astype(BF16)
    r = _sigmoid(_dot(xcb, wr) + br)
    i = _sigmoid(_dot(xcb, wi) + bi)
    la = (LRU_C * r) * ls
    a = jnp.exp(la)
    m = jnp.sqrt(-_expm1(2.0 * la))
    return xc, xcb, r, i, a, m


def _lru_specs(S):
    col = lambda off: pl.BlockSpec((S, LANES), lambda j: (0, j + off))
    vec = pl.BlockSpec((1, LANES), lambda j: (0, j))
    mat = pl.BlockSpec((None, LANES, LANES), lambda j: (j, 0, 0))
    cwm = pl.BlockSpec((CONV_WIDTH, LANES), lambda j: (0, j))
    return col, vec, mat, cwm


def _lru_fwd(u, conv_w, conv_b, wr, br, wi, bi, lam, name):
    S = u.shape[0]
    T = _token_tile(S, 512)
    col, vec, mat, cwm = _lru_specs(S)

    def body(gp_ref, xb_ref, cw_ref, cb_ref, wr_ref, br_ref, wi_ref, bi_ref, lam_ref,
             y_ref, hs_ref):
        row = lax.broadcasted_iota(jnp.int32, (T, LANES), 0)
        ls = _log_sigmoid(lam_ref[...])
        cb, br, bi = cb_ref[...], br_ref[...], bi_ref[...]
        wr, wi = wr_ref[...], wi_ref[...]

        def chunk(ci, carry):
            prev_xb, hc = carry
            rows = pl.ds(pl.multiple_of(ci * T, T), T)
            xb = xb_ref[rows, :]
            xc, _, _, i, a, m = _lru_gates(xb, prev_xb, row, cw_ref, cb, wr, br, wi, bi, ls)
            ca, cbv = _scan_chunk(a, m * (i * xc), row, T, reverse=False)
            h = ca * hc + cbv
            hs_ref[rows, :] = h
            y_ref[rows, :] = (_gelu(gp_ref[rows, :]) * h).astype(BF16)
            return xb, _row_of(h, row, T - 1)

        lax.fori_loop(0, S // T, chunk,
                      (jnp.zeros((T, LANES), F32), jnp.zeros((1, LANES), F32)))

    return pl.pallas_call(
        body, name=name, grid=(N_CBLK,),
        in_specs=[col(0), col(N_CBLK), cwm, vec, mat, vec, mat, vec, vec],
        out_specs=[col(0), col(0)],
        out_shape=[jax.ShapeDtypeStruct((S, D_MODEL), BF16), jax.ShapeDtypeStruct((S, D_MODEL), F32)],
        compiler_params=_params("parallel"),
    )(u, u, conv_w, conv_b, wr, br, wi, bi, lam)


def _lru_bwd(dy, u, hs, conv_w, conv_b, wr, br, wi, bi, lam, name):
    S = u.shape[0]
    T = _token_tile(S, 512)
    n_chunk = S // T
    col, vec, mat, cwm = _lru_specs(S)

    def body(dy_ref, gp_ref, xb_ref, hs_ref, cw_ref, cb_ref, wr_ref, br_ref, wi_ref, bi_ref,
             lam_ref, dgp_ref, dxb_ref, dcw_ref, dcb_ref, dbr_ref, dbi_ref, dlam_ref, dwr_ref,
             dwi_ref):
        row = lax.broadcasted_iota(jnp.int32, (T, LANES), 0)
        lam = lam_ref[...]
        ls = _log_sigmoid(lam)
        cb, br, bi = cb_ref[...], br_ref[...], bi_ref[...]
        wr, wi = wr_ref[...], wi_ref[...]
        for ref in (dcw_ref, dcb_ref, dbr_ref, dbi_ref, dlam_ref, dwr_ref, dwi_ref):
            ref[...] = jnp.zeros_like(ref)

        def chunk(it, carry):
            g_next, dxc_next = carry
            ci = n_chunk - 1 - it
            rows = pl.ds(pl.multiple_of(ci * T, T), T)
            before = pl.ds(pl.multiple_of(jnp.maximum(ci - 1, 0) * T, T), T)
            first = ci == 0
            xb = xb_ref[rows, :]
            prev_xb = jnp.where(first, 0.0, xb_ref[before, :])
            xc, xcb, r, i, a, m = _lru_gates(xb, prev_xb, row, cw_ref, cb, wr, br, wi, bi, ls)
            h = hs_ref[rows, :]
            h_prev = _shift_down(h, jnp.where(first, 0.0, hs_ref[before, :]), row, 1)
            gp = gp_ref[rows, :]
            dyv = dy_ref[rows, :]
            dgp_ref[rows, :] = (dyv * h * _gelu_grad(gp)).astype(BF16)
            dh = dyv * _gelu(gp)
            ca, cbv = _scan_chunk(a, a * dh, row, T, reverse=True)
            gp_acc = ca * g_next + cbv
            g = dh + jnp.where(row < T - 1, pltpu.roll(gp_acc, T - 1, 0), g_next)
            da = g * h_prev - (g * (i * xc)) * a / m
            dla = da * a
            dlam_ref[...] += jnp.sum(dla * (LRU_C * r), axis=0, keepdims=True)
            dpr = (dla * (LRU_C * ls)) * r * (1.0 - r)
            dpi = (g * m * xc) * i * (1.0 - i)
            dbr_ref[...] += jnp.sum(dpr, axis=0, keepdims=True)
            dbi_ref[...] += jnp.sum(dpi, axis=0, keepdims=True)
            dprb, dpib = dpr.astype(BF16), dpi.astype(BF16)
            dwr_ref[...] += _dot_tn(xcb, dprb)
            dwi_ref[...] += _dot_tn(xcb, dpib)
            dxc = g * m * i + _dot_nt(dprb, wr) + _dot_nt(dpib, wi)
            dcb_ref[...] += jnp.sum(dxc, axis=0, keepdims=True)
            dxb = jnp.zeros((T, LANES), F32)
            for k in range(CONV_WIDTH):
                tap = pl.ds(CONV_WIDTH - 1 - k, 1)
                dcw_ref[tap, :] += jnp.sum(dxc * _shift_down(xb, prev_xb, row, k), axis=0,
                                           keepdims=True)
                dxb = dxb + cw_ref[tap, :] * _shift_up(dxc, dxc_next, row, k, T)
            dxb_ref[rows, :] = dxb.astype(BF16)
            return _row_of(gp_acc, row, 0), dxc

        lax.fori_loop(0, n_chunk, chunk,
                      (jnp.zeros((1, LANES), F32), jnp.zeros((T, LANES), F32)))
        dlam_ref[...] = dlam_ref[...] * _sigmoid(-lam)

    vec_out = jax.ShapeDtypeStruct((1, D_MODEL), F32)
    mat_out = jax.ShapeDtypeStruct((N_CBLK, LANES, LANES), F32)
    return pl.pallas_call(
        body, name=name, grid=(N_CBLK,),
        in_specs=[col(0), col(0), col(N_CBLK), col(0), cwm, vec, mat, vec, mat, vec, vec],
        out_specs=[col(0), col(0), cwm, vec, vec, vec, vec, mat, mat],
        out_shape=[jax.ShapeDtypeStruct((S, D_MODEL), BF16), jax.ShapeDtypeStruct((S, D_MODEL), BF16),
                   jax.ShapeDtypeStruct((CONV_WIDTH, D_MODEL), F32),
                   vec_out, vec_out, vec_out, vec_out, mat_out, mat_out],
        compiler_params=_params("parallel"),
    )(dy, u, u, hs, conv_w, conv_b, wr, br, wi, bi, lam)


def _head_group_matrix(value):
    r = lax.broadcasted_iota(jnp.int32, (LANES, LANES), 0) // HEAD_DIM
    c = lax.broadcasted_iota(jnp.int32, (LANES, LANES), 1) // HEAD_DIM
    return jnp.where(r == c, value, 0.0).astype(BF16)


def _group_dot(x, p):
    hi = x.astype(BF16)
    lo = (x - hi.astype(F32)).astype(BF16)
    return _dot(hi, p) + _dot(lo, p)


def _head_mean(x, p):
    return _group_dot(x, p)


def _fox_in_proj(x, g, wqkv, wf, q_gain, k_gain, name, tm=512):
    S, D = x.shape
    tm = _token_tile(S, tm)

    def body(x_ref, g_ref, w_ref, wf_ref, qg_ref, kg_ref,
             uq_ref, uk_ref, f_ref, h_ref, qn_ref, kn_ref, vb_ref):
        xv = x_ref[...]
        h = (xv * _rms_scale(xv) * g_ref[...]).astype(BF16)
        h_ref[...] = h
        p = _head_group_matrix(1.0 / HEAD_DIM)
        for which, u_ref, gain_ref, n_ref, scale in ((0, uq_ref, qg_ref, qn_ref, ATTN_SCALE),
                                                     (1, uk_ref, kg_ref, kn_ref, 1.0)):
            u = _dot(h, w_ref[which])
            u_ref[...] = u
            for j in range(N_CBLK):
                cl = slice(j * LANES, (j + 1) * LANES)
                uv = u[:, cl]
                rs = lax.rsqrt(_head_mean(uv * uv, p) + EPS)
                n_ref[:, cl] = (uv * rs * gain_ref[...]).astype(BF16) * scale
        vb_ref[...] = _dot(h, w_ref[2]).astype(BF16)
        f_ref[...] = _dot(h, wf_ref[0])

    tok = lambda width: pl.BlockSpec((tm, width), lambda i: (i, 0))
    wide = jax.ShapeDtypeStruct((S, D), F32)
    half = jax.ShapeDtypeStruct((S, D), BF16)
    return pl.pallas_call(
        body, name=name, grid=(S // tm,),
        in_specs=[tok(D), _resident((1, D)), _resident(wqkv.shape), _resident(wf.shape),
                  _resident((1, LANES)), _resident((1, LANES))],
        out_specs=[tok(D), tok(D), tok(LANES), tok(D), tok(D), tok(D), tok(D)],
        out_shape=[wide, wide, jax.ShapeDtypeStruct((S, LANES), F32), half, half, half, half],
        compiler_params=_params("parallel"),
    )(x, g, wqkv, wf, q_gain, k_gain)


def _qk_bwd(uq, uk, dqn, dkn, q_gain, k_gain, name, tm=512):
    S = uq.shape[0]
    tm = _token_tile(S, tm)

    def body(q_ref, k_ref, dqn_ref, dkn_ref, qg_ref, kg_ref, dq_ref, dk_ref, dqg_ref, dkg_ref):
        p = _head_group_matrix(1.0 / HEAD_DIM)
        for x_ref, dn_ref, g_ref, dx_ref, dg_ref, scale in (
                (q_ref, dqn_ref, qg_ref, dq_ref, dqg_ref, ATTN_SCALE),
                (k_ref, dkn_ref, kg_ref, dk_ref, dkg_ref, 1.0)):
            dg = jnp.zeros((1, LANES), F32)
            for j in range(N_CBLK):
                cl = slice(j * LANES, (j + 1) * LANES)
                xv, dn = x_ref[:, cl], dn_ref[:, cl] * scale
                rs = lax.rsqrt(_head_mean(xv * xv, p) + EPS)
                xhat = xv * rs
                dxhat = dn * g_ref[...]
                dx_ref[:, cl] = (rs * (dxhat - xhat * _head_mean(dxhat * xhat, p))).astype(BF16)
                dg = dg + jnp.sum(dn * xhat, axis=0, keepdims=True)

            @pl.when(pl.program_id(0) == 0)
            def _():
                dg_ref[...] = jnp.zeros_like(dg_ref)
            dg_ref[...] += dg

            @pl.when(pl.program_id(0) == S // tm - 1)
            def _():
                dg_ref[...] += pltpu.roll(dg_ref[...], HEAD_DIM, 1)

    blk = lambda off: pl.BlockSpec((tm, D_MODEL), lambda i: (i, off))
    acc = pl.BlockSpec((1, LANES), lambda i: (0, 0))
    out = jax.ShapeDtypeStruct((S, D_MODEL), BF16)
    vec = jax.ShapeDtypeStruct((1, LANES), F32)
    return pl.pallas_call(
        body, name=name, grid=(S // tm,),
        in_specs=[blk(0), blk(0), blk(0), blk(0), _resident((1, LANES)), _resident((1, LANES))],
        out_specs=[blk(0), blk(0), acc, acc],
        out_shape=[out, out, vec, vec],
        compiler_params=_params("arbitrary"),
    )(uq, uk, dqn, dkn, q_gain, k_gain)


def _forget_fwd(f, b_f, name):
    S = f.shape[0]
    T = _token_tile(S, 256)

    def body(f_ref, b_ref, c_ref):
        row = lax.broadcasted_iota(jnp.int32, (T, LANES), 0)
        ones = jnp.ones((T, LANES), F32)
        bias = b_ref[...]

        def chunk(ci, carry):
            rows = pl.ds(pl.multiple_of(ci * T, T), T)
            _, c = _scan_chunk(ones, _log_sigmoid(f_ref[rows, :] + bias), row, T, reverse=False)
            c = c + carry
            c_ref[rows, :] = c
            return _row_of(c, row, T - 1)

        lax.fori_loop(0, S // T, chunk, jnp.zeros((1, LANES), F32))

    return pl.pallas_call(
        body, name=name,
        in_specs=[pl.BlockSpec(memory_space=pltpu.VMEM)] * 2,
        out_specs=pl.BlockSpec(memory_space=pltpu.VMEM),
        out_shape=jax.ShapeDtypeStruct((S, LANES), F32),
        compiler_params=pltpu.CompilerParams(vmem_limit_bytes=VMEM_LIMIT),
    )(f, b_f)


def _forget_bwd(dc_k, rho, f, b_f, name):
    S = f.shape[0]
    T = _token_tile(S, 256)
    n_chunk = S // T

    def body(dck_ref, rho_ref, f_ref, b_ref, df_ref, db_ref):
        row = lax.broadcasted_iota(jnp.int32, (T, LANES), 0)
        ones = jnp.ones((T, LANES), F32)
        bias = b_ref[...]
        pick = (lax.broadcasted_iota(jnp.int32, (D_MODEL, LANES), 0)
                == HEAD_DIM * lax.broadcasted_iota(jnp.int32, (D_MODEL, LANES), 1))
        pick = jnp.where(pick, 1.0, 0.0).astype(BF16)

        def chunk(it, carry):
            tail, db = carry
            rows = pl.ds(pl.multiple_of((n_chunk - 1 - it) * T, T), T)
            dc = dck_ref[rows, :] + _group_dot(rho_ref[rows, :], pick)
            _, dlf = _scan_chunk(ones, dc, row, T, reverse=True)
            dlf = dlf + tail
            df = dlf * _sigmoid(-(f_ref[rows, :] + bias))
            df_ref[rows, :] = df
            return _row_of(dlf, row, 0), db + jnp.sum(df, axis=0, keepdims=True)

        zero = jnp.zeros((1, LANES), F32)
        _, db = lax.fori_loop(0, n_chunk, chunk, (zero, zero))
        db_ref[...] = db

    return pl.pallas_call(
        body, name=name,
        in_specs=[pl.BlockSpec(memory_space=pltpu.VMEM)] * 4,
        out_specs=[pl.BlockSpec(memory_space=pltpu.VMEM)] * 2,
        out_shape=[jax.ShapeDtypeStruct((S, LANES), F32), jax.ShapeDtypeStruct((1, LANES), F32)],
        compiler_params=pltpu.CompilerParams(vmem_limit_bytes=VMEM_LIMIT),
    )(dc_k, rho, f, b_f)


ATTN_TILE = 512
ATTN_ROWS_FWD = 32


def _attn_tiles(S):
    t = _token_tile(S, ATTN_TILE)
    return t, S // t


def _causal(T):
    return (lax.broadcasted_iota(jnp.int32, (T, T), 1)
            <= lax.broadcasted_iota(jnp.int32, (T, T), 0))


def _attn_fwd(qs_, kn, vb, c_row, name):
    S = qs_.shape[0]
    T, n_t = _attn_tiles(S)
    RB = min(T, ATTN_ROWS_FWD)

    def body(q_ref, k_ref, v_ref, cr_ref, o_ref, lse_ref, sa_ref, sb_ref, p_ref, m_ref, l_ref,
             acc_ref, a_ref):
        qi = pl.program_id(1)
        lanes = [slice(h2 * HEAD_DIM, (h2 + 1) * HEAD_DIM) for h2 in range(2)]
        col = lax.broadcasted_iota(jnp.int32, (RB, T), 1)
        row = lax.broadcasted_iota(jnp.int32, (RB, T), 0)
        m_ref[...] = jnp.full(m_ref.shape, NEG_INF, F32)
        l_ref[...] = jnp.zeros_like(l_ref)
        acc_ref[...] = jnp.zeros_like(acc_ref)

        def logits_into(s_ref, kj):
            ks = pl.ds(pl.multiple_of(kj * T, T), T)
            for h2, hl in enumerate(lanes):
                s_ref[h2] = _dot_nt(q_ref[:, hl], k_ref[ks, hl]) - cr_ref[h2:h2 + 1, ks]

        def consume(s_ref, kj, masked):
            ks = pl.ds(pl.multiple_of(kj * T, T), T)
            for h2, hl in enumerate(lanes):
                blocks = [slice(i * RB, (i + 1) * RB) for i in range(T // RB)]

                def logits(i, rows):
                    s = s_ref[h2, rows, :]
                    return jnp.where(col <= row + i * RB, s, NEG_INF) if masked else s

                wide = lambda x: jnp.broadcast_to(x, (RB, LANES))
                for i, rows in enumerate(blocks):
                    mx = wide(jnp.max(logits(i, rows), axis=1, keepdims=True))
                    a_ref[h2, rows, :] = m_ref[h2, rows, :]
                    m_ref[h2, rows, :] = jnp.maximum(m_ref[h2, rows, :], mx)
                for i, rows in enumerate(blocks):
                    m_new = m_ref[h2, rows, :]
                    p = jnp.exp(logits(i, rows) - jnp.tile(m_new, (1, T // LANES)))
                    alpha = jnp.exp(a_ref[h2, rows, :] - m_new)
                    a_ref[h2, rows, :] = alpha
                    l_ref[h2, rows, :] = (alpha * l_ref[h2, rows, :]
                                          + wide(jnp.sum(p, axis=1, keepdims=True)))
                    p_ref[h2, rows, :] = p.astype(BF16)
                acc_ref[h2] = (a_ref[h2, :, :HEAD_DIM] * acc_ref[h2]
                               + _dot(p_ref[h2], v_ref[ks, hl]))

        logits_into(sa_ref, 0)

        def pair(i, _):
            logits_into(sb_ref, 2 * i + 1)
            consume(sa_ref, 2 * i, False)
            logits_into(sa_ref, 2 * i + 2)
            consume(sb_ref, 2 * i + 1, False)
            return 0

        lax.fori_loop(0, qi // 2, pair, 0)

        @pl.when(qi % 2 == 1)
        def _():
            logits_into(sb_ref, qi)
            consume(sa_ref, qi - 1, False)
            consume(sb_ref, qi, True)

        @pl.when(qi % 2 == 0)
        def _():
            consume(sa_ref, qi, True)

        for h2, hl in enumerate(lanes):
            o_ref[:, hl] = (acc_ref[h2] / l_ref[h2, :, :HEAD_DIM]).astype(BF16)
            lse_ref[:, hl] = m_ref[h2, :, :HEAD_DIM] + jnp.log(l_ref[h2, :, :HEAD_DIM])

    qblk = pl.BlockSpec((T, LANES), lambda h, i: (i, h))
    kv = pl.BlockSpec((S, LANES), lambda h, i: (0, h))
    return pl.pallas_call(
        body, name=name, grid=(N_CBLK, n_t),
        in_specs=[qblk, kv, kv, pl.BlockSpec((None, 2, S), lambda h, i: (h, 0, 0))],
        out_specs=[qblk, qblk],
        out_shape=[jax.ShapeDtypeStruct((S, D_MODEL), BF16),
                   jax.ShapeDtypeStruct((S, D_MODEL), F32)],
        scratch_shapes=[pltpu.VMEM((2, T, T), F32), pltpu.VMEM((2, T, T), F32),
                        pltpu.VMEM((2, T, T), BF16),
                        pltpu.VMEM((2, T, LANES), F32), pltpu.VMEM((2, T, LANES), F32),
                        pltpu.VMEM((2, T, HEAD_DIM), F32), pltpu.VMEM((2, T, LANES), F32)],
        compiler_params=_params("parallel", "parallel"),
    )(qs_, kn, vb, c_row)


def _attn_bwd(qs_, kn, vb, do, o, lse, c_row, name):
    S = qs_.shape[0]
    T, n_t = _attn_tiles(S)

    def body(q_ref, k_ref, v_ref, do_ref, o_ref, lse_ref, cr_ref,
             dq_ref, dk_ref, dv_ref, dc_ref, rho_ref, dd_ref):
        kj = pl.program_id(1)
        causal = _causal(T)
        lanes = [slice(h2 * HEAD_DIM, (h2 + 1) * HEAD_DIM) for h2 in range(2)]
        ones = [slice(h2 * HEAD_DIM, h2 * HEAD_DIM + 1) for h2 in range(2)]

        @pl.when(kj == 0)
        def _():
            dq_ref[...] = jnp.zeros_like(dq_ref)
            rho_ref[...] = jnp.zeros_like(rho_ref)
            p_sum = _head_group_matrix(1.0)

            def fill(ci, _):
                rows = pl.ds(pl.multiple_of(ci * T, T), T)
                dd_ref[rows, :] = _group_dot(do_ref[rows, :].astype(F32) * o_ref[rows, :].astype(F32),
                                             p_sum)
                return 0

            lax.fori_loop(0, n_t, fill, 0)

        kh = [k_ref[:, hl] for hl in lanes]
        vh = [v_ref[:, hl] for hl in lanes]
        ck = [cr_ref[h2:h2 + 1, :] for h2 in range(2)]

        def step(qi, carry, masked):
            qs = pl.ds(pl.multiple_of(qi * T, T), T)
            out = []
            for h2, hl in enumerate(lanes):
                dk, dv, dc = carry[h2]
                qh, doh = q_ref[qs, hl], do_ref[qs, hl]
                s = _dot_nt(qh, kh[h2]) - ck[h2]
                if masked:
                    s = jnp.where(causal, s, NEG_INF)
                p = jnp.exp(s - lse_ref[qs, ones[h2]])
                ds = p * (_dot_nt(doh, vh[h2]) - dd_ref[qs, ones[h2]])
                dsb = ds.astype(BF16)
                dq_ref[qs, hl] += _dot(dsb, kh[h2])
                rho_ref[qs, hl] += jnp.broadcast_to(jnp.sum(ds, axis=1, keepdims=True),
                                                    (T, HEAD_DIM))
                out.append((dk + _dot_tn(dsb, qh), dv + _dot_tn(p.astype(BF16), doh),
                            dc - jnp.sum(ds, axis=0, keepdims=True)))
            return tuple(out)

        init = tuple((jnp.zeros((T, HEAD_DIM), F32), jnp.zeros((T, HEAD_DIM), F32),
                      jnp.zeros((1, T), F32)) for _ in lanes)
        carry = step(kj, init, True)
        carry = lax.fori_loop(kj + 1, n_t, lambda qi, c: step(qi, c, False), carry)
        for h2, ((dk, dv, dc), hl) in enumerate(zip(carry, lanes)):
            dk_ref[:, hl] = dk
            dv_ref[:, hl] = dv.astype(BF16)
            dc_ref[h2:h2 + 1, :] = dc

    kblk = pl.BlockSpec((T, LANES), lambda h, j: (j, h))
    full = pl.BlockSpec((S, LANES), lambda h, j: (0, h))
    crow = pl.BlockSpec((None, 2, T), lambda h, j: (h, 0, j))
    wide = jax.ShapeDtypeStruct((S, D_MODEL), F32)
    return pl.pallas_call(
        body, name=name, grid=(N_CBLK, n_t),
        in_specs=[full, kblk, kblk, full, full, full, crow],
        out_specs=[full, kblk, kblk, crow, full],
        out_shape=[wide, wide, jax.ShapeDtypeStruct((S, D_MODEL), BF16),
                   jax.ShapeDtypeStruct((N_CBLK, 2, S), F32), wide],
        scratch_shapes=[pltpu.VMEM((S, LANES), F32)],
        compiler_params=_params("parallel", "arbitrary"),
    )(qs_, kn, vb, do, o, lse, c_row)


ALL_PEERS = tuple(range(1, N_DEV))
NEAR_PEERS = (1, 2, 4, 6)
FAR_CHIPS = (2, 4, 6)


def _landing_shapes(arrays, gathers):
    return [jax.ShapeDtypeStruct((N_DEV,) + a.shape if g else a.shape, a.dtype)
            for a, g in zip(arrays, gathers)]


def _my_index():
    return 4 * lax.axis_index("x") + 2 * lax.axis_index("y") + lax.axis_index("c")


def _own_copies(srcs, lands, gathers, sems):
    me = _my_index()
    return [pltpu.make_async_copy(src if g else src.at[me], land.at[me], sems.at[a])
            for a, (src, land, g) in enumerate(zip(srcs, lands, gathers))]


def _peer_copies(srcs, lands, gathers, send_sems, recv_sems, ks=ALL_PEERS):
    x, y, c = lax.axis_index("x"), lax.axis_index("y"), lax.axis_index("c")
    me = 4 * x + 2 * y + c
    out = []
    for j, k in enumerate(ks):
        to = (1 - x if k & 4 else x, 1 - y if k & 2 else y, 1 - c if k & 1 else c)
        peer = 4 * to[0] + 2 * to[1] + to[2]
        for a, (src, land, g) in enumerate(zip(srcs, lands, gathers)):
            sem = a * len(ks) + j
            src_blk = src if g else src.at[peer]

            def copy(slot, src_blk=src_blk, land=land, sem=sem, to=to):
                return pltpu.make_async_remote_copy(
                    src_ref=src_blk, dst_ref=land.at[slot], send_sem=send_sems.at[sem],
                    recv_sem=recv_sems.at[sem], device_id=to,
                    device_id_type=pl.DeviceIdType.MESH)

            out.append((k, a, copy(me), copy(peer)))
    return out


def _forward_copies(lands, send_sems, recv_sems):
    x, y, c = lax.axis_index("x"), lax.axis_index("y"), lax.axis_index("c")
    out = []
    for j, f in enumerate(FAR_CHIPS):
        chip = 4 * (1 - x if f & 4 else x) + 2 * (1 - y if f & 2 else y)
        for a, land in enumerate(lands):
            sem = a * len(FAR_CHIPS) + j

            def copy(slot, land=land, sem=sem):
                return pltpu.make_async_remote_copy(
                    src_ref=land.at[slot], dst_ref=land.at[slot], send_sem=send_sems.at[sem],
                    recv_sem=recv_sems.at[sem], device_id=(x, y, 1 - c),
                    device_id_type=pl.DeviceIdType.MESH)

            out.append((f, a, copy(chip + c), copy(chip + 1 - c)))
    return out


def _exchange(arrays, gathers, name, two_level=False):
    n = len(arrays)
    ks = NEAR_PEERS if two_level else ALL_PEERS
    assert not two_level or all(gathers)

    def body(*refs):
        ins, outs = refs[:n], refs[n:2 * n]
        send_sems, recv_sems, own_sems, fwd_send_sems, fwd_recv_sems = refs[2 * n:]
        own = _own_copies(ins, outs, gathers, own_sems)
        for cp in own:
            cp.start()
        copies = _peer_copies(ins, outs, gathers, send_sems, recv_sems, ks)
        for _, _, send, _ in copies:
            send.start()
        passed = {}
        if two_level:
            passed = {(f, a): (send, arrival)
                      for f, a, send, arrival in _forward_copies(outs, fwd_send_sems, fwd_recv_sems)}
        for k, a, _, arrival in copies:
            arrival.wait_recv()
            if (k, a) in passed:
                passed[k, a][0].start()
        for send, arrival in passed.values():
            arrival.wait_recv()
            send.wait_send()
        for _, _, send, _ in copies:
            send.wait_send()
        for cp in own:
            cp.wait()

    hbm = pl.BlockSpec(memory_space=pl.ANY)
    return pl.pallas_call(
        body, name=name,
        in_specs=[hbm] * n, out_specs=[hbm] * n, out_shape=_landing_shapes(arrays, gathers),
        scratch_shapes=[pltpu.SemaphoreType.DMA((n * len(ks),)),
                        pltpu.SemaphoreType.DMA((n * len(ks),)),
                        pltpu.SemaphoreType.DMA((n,)),
                        pltpu.SemaphoreType.DMA((n * len(FAR_CHIPS),)),
                        pltpu.SemaphoreType.DMA((n * len(FAR_CHIPS),))],
        compiler_params=pltpu.CompilerParams(has_side_effects=True),
    )(*arrays)


_HBM = pl.BlockSpec(memory_space=pltpu.HBM)
_SEM = pl.BlockSpec(memory_space=pltpu.SEMAPHORE)
_ANY = pl.BlockSpec(memory_space=pl.ANY)
_DATAFLOW = pltpu.SideEffectType.DATAFLOW_SIDE_EFFECTING


def _in_hbm(a):
    return pltpu.with_memory_space_constraint(a, pltpu.HBM)


def _exchange_start(arrays, gathers, after, name, ks=ALL_PEERS):
    n = len(arrays)
    lands = [lax.empty(s.shape, s.dtype) for s in _landing_shapes(arrays, gathers)]

    def body(*refs):
        srcs, dsts = refs[:n], refs[n:2 * n]
        send_sems, recv_sems, own_sems = refs[2 * n + 1:2 * n + 4]
        token = refs[-1]
        for cp in _own_copies(srcs, dsts, gathers, own_sems):
            cp.start()
        for _, _, send, _ in _peer_copies(srcs, dsts, gathers, send_sems, recv_sems, ks):
            send.start()
        token[...] = jnp.zeros_like(token)

    hbm_like = [pltpu.HBM(a.shape, a.dtype) for a in list(arrays) + lands]
    res = pl.pallas_call(
        body, name=name,
        in_specs=[_HBM] * (2 * n) + [_ANY],
        out_specs=(_SEM, _SEM, _SEM, *[_HBM] * (2 * n), pl.BlockSpec(memory_space=pltpu.VMEM)),
        out_shape=(pltpu.SemaphoreType.DMA((n * len(ks),)), pltpu.SemaphoreType.DMA((n * len(ks),)),
                   pltpu.SemaphoreType.DMA((n,)), *hbm_like,
                   jax.ShapeDtypeStruct((8, LANES), F32)),
        input_output_aliases={i: 3 + i for i in range(2 * n)},
        compiler_params=pltpu.CompilerParams(has_side_effects=_DATAFLOW),
    )(*[_in_hbm(a) for a in list(arrays) + lands], after)
    return (res[0], res[1], res[2], res[3:3 + n], res[3 + n:3 + 2 * n]), res[-1]


def _exchange_wait(started, gathers, after, name, ks=ALL_PEERS):
    send_sems, recv_sems, own_sems, arrays, lands = started
    n = len(arrays)

    def body(*refs):
        srcs, dsts = refs[:n], refs[n:2 * n]
        for _, _, send, arrival in _peer_copies(srcs, dsts, gathers, refs[2 * n], refs[2 * n + 1],
                                                ks):
            arrival.wait_recv()
            send.wait_send()
        for cp in _own_copies(srcs, dsts, gathers, refs[2 * n + 2]):
            cp.wait()

    hbm_like = [pltpu.HBM(a.shape, a.dtype) for a in list(arrays) + list(lands)]
    res = pl.pallas_call(
        body, name=name,
        in_specs=[_HBM] * (2 * n) + [_SEM, _SEM, _SEM, _ANY],
        out_specs=[_HBM] * (2 * n), out_shape=hbm_like,
        input_output_aliases={i: i for i in range(2 * n)},
        compiler_params=pltpu.CompilerParams(has_side_effects=_DATAFLOW),
    )(*arrays, *lands, send_sems, recv_sems, own_sems, after)
    return res[n:]


def _forward_start(lands, after, name):
    n = len(lands)

    def body(*refs):
        send_sems, recv_sems = refs[n + 1:n + 3]
        for _, _, send, _ in _forward_copies(refs[:n], send_sems, recv_sems):
            send.start()
        refs[-1][...] = jnp.zeros_like(refs[-1])

    n_sem = n * len(FAR_CHIPS)
    res = pl.pallas_call(
        body, name=name,
        in_specs=[_HBM] * n + [_ANY],
        out_specs=(_SEM, _SEM, *[_HBM] * n, pl.BlockSpec(memory_space=pltpu.VMEM)),
        out_shape=(pltpu.SemaphoreType.DMA((n_sem,)), pltpu.SemaphoreType.DMA((n_sem,)),
                   *[pltpu.HBM(a.shape, a.dtype) for a in lands],
                   jax.ShapeDtypeStruct((8, LANES), F32)),
        input_output_aliases={i: 2 + i for i in range(n)},
        compiler_params=pltpu.CompilerParams(has_side_effects=_DATAFLOW),
    )(*[_in_hbm(a) for a in lands], after)
    return (res[0], res[1], res[2:2 + n]), res[-1]


def _forward_wait(started, after, name):
    send_sems, recv_sems, lands = started
    n = len(lands)

    def body(*refs):
        for _, _, send, arrival in _forward_copies(refs[:n], refs[n], refs[n + 1]):
            arrival.wait_recv()
            send.wait_send()

    return pl.pallas_call(
        body, name=name,
        in_specs=[_HBM] * n + [_SEM, _SEM, _ANY],
        out_specs=[_HBM] * n, out_shape=[pltpu.HBM(a.shape, a.dtype) for a in lands],
        input_output_aliases={i: i for i in range(n)},
        compiler_params=pltpu.CompilerParams(has_side_effects=_DATAFLOW),
    )(*lands, send_sems, recv_sems, after)


def _reduce_adamw(parts, w, m, v, name):
    n_layer = len(parts)
    n, R, C = parts[0].shape
    tr = 256 if R % 256 == 0 else R
    n_t = R // tr

    def body(*refs):
        p_refs = refs[:n_layer]
        w_ref, m_ref, v_ref, g_ref, d_ref, nm_ref, nv_ref = refs[n_layer:]

        def update(p_ref):
            g = p_ref[0].astype(F32)
            for s in range(1, n):
                g = g + p_ref[s].astype(F32)
            g_ref[...] = g
            m_new = ADAM_B1 * m_ref[...] + (1.0 - ADAM_B1) * g
            v_new = ADAM_B2 * v_ref[...] + (1.0 - ADAM_B2) * (g * g)
            nm_ref[...] = m_new
            nv_ref[...] = v_new
            m_hat = m_new / (1.0 - ADAM_B1 ** ADAM_STEP)
            v_hat = v_new / (1.0 - ADAM_B2 ** ADAM_STEP)
            d_ref[...] = -ADAM_LR * (m_hat / (jnp.sqrt(v_hat) + ADAM_EPS) + ADAM_WD * w_ref[...])

        for layer, p_ref in enumerate(p_refs):
            pl.when(pl.program_id(0) == layer)(functools.partial(update, p_ref))

    def parts_spec(layer):
        def index(l, i):
            return 0, jnp.where(l < layer, 0, jnp.where(l > layer, n_t - 1, i)), 0
        return pl.BlockSpec((n, tr, C), index)

    blk = pl.BlockSpec((None, tr, C), lambda l, i: (l, i, 0))
    out = jax.ShapeDtypeStruct((n_layer, R, C), F32)
    return pl.pallas_call(
        body, name=name, grid=(n_layer, n_t),
        in_specs=[parts_spec(layer) for layer in range(n_layer)] + [blk, blk, blk],
        out_specs=[blk] * 4, out_shape=[out] * 4,
        compiler_params=_params("arbitrary", "arbitrary"),
    )(*parts, w, m, v)


def _pack(arrays):
    flat = jnp.concatenate([a.reshape(-1).astype(F32) for a in arrays])
    pad = (-flat.shape[0]) % (8 * LANES)
    return jnp.pad(flat, (0, pad)).reshape(-1, LANES)


def _unpack(buf, shapes):
    flat = buf.reshape(-1)
    out, off = [], 0
    for shp in shapes:
        size = 1
        for s in shp:
            size *= s
        out.append(flat[off:off + size].reshape(shp))
        off += size
    return out


def _block_diag_pairs(w):
    w = w.reshape(N_CBLK, 2, LRU_BLOCK_DIM, LRU_BLOCK_DIM)
    z = jnp.zeros_like(w[:, 0])
    top = jnp.concatenate([w[:, 0], z], axis=2)
    bot = jnp.concatenate([z, w[:, 1]], axis=2)
    return jnp.concatenate([top, bot], axis=1)


def _diag_pairs(m):
    h = LRU_BLOCK_DIM
    return jnp.stack([m[:, :h, :h], m[:, h:, h:]], axis=1).reshape(2 * N_CBLK, h, h)


SMALL = ("mlp_norm", "lru_conv_b", "lru_w_r", "lru_b_r", "lru_w_i", "lru_b_i",
         "lru_lambda", "fox_b_f", "fox_q_gain", "fox_k_gain")
WEIGHTS = ("mix_norm", "mlp_norm", "mlp_w1", "mlp_w2", "lru_w_in", "lru_conv_w", "lru_conv_b",
           "lru_w_r", "lru_b_r", "lru_w_i", "lru_b_i", "lru_lambda", "lru_w_out", "fox_w_in",
           "fox_b_f", "fox_q_gain", "fox_k_gain", "fox_w_out")


def kernel(x, mix_norm, mlp_norm, mlp_w1, mlp_w2, lru_w_in, lru_conv_w, lru_conv_b, lru_w_r, lru_b_r, lru_w_i, lru_b_i, lru_lambda, lru_w_out, fox_w_in, fox_b_f, fox_q_gain, fox_k_gain, fox_w_out, loss_target, m_mix_norm, m_mlp_norm, m_mlp_w1, m_mlp_w2, m_lru_w_in, m_lru_conv_w, m_lru_conv_b, m_lru_w_r, m_lru_b_r, m_lru_w_i, m_lru_b_i, m_lru_lambda, m_lru_w_out, m_fox_w_in, m_fox_b_f, m_fox_q_gain, m_fox_k_gain, m_fox_w_out, v_mix_norm, v_mlp_norm, v_mlp_w1, v_mlp_w2, v_lru_w_in, v_lru_conv_w, v_lru_conv_b, v_lru_w_r, v_lru_b_r, v_lru_w_i, v_lru_b_i, v_lru_lambda, v_lru_w_out, v_fox_w_in, v_fox_b_f, v_fox_q_gain, v_fox_k_gain, v_fox_w_out):
    w_in = dict(mix_norm=mix_norm, mlp_norm=mlp_norm, mlp_w1=mlp_w1, mlp_w2=mlp_w2,
                lru_w_in=lru_w_in, lru_conv_w=lru_conv_w, lru_conv_b=lru_conv_b, lru_w_r=lru_w_r,
                lru_b_r=lru_b_r, lru_w_i=lru_w_i, lru_b_i=lru_b_i, lru_lambda=lru_lambda,
                lru_w_out=lru_w_out, fox_w_in=fox_w_in, fox_b_f=fox_b_f, fox_q_gain=fox_q_gain,
                fox_k_gain=fox_k_gain, fox_w_out=fox_w_out)
    m_in = dict(mix_norm=m_mix_norm, mlp_norm=m_mlp_norm, mlp_w1=m_mlp_w1, mlp_w2=m_mlp_w2,
                lru_w_in=m_lru_w_in, lru_conv_w=m_lru_conv_w, lru_conv_b=m_lru_conv_b,
                lru_w_r=m_lru_w_r, lru_b_r=m_lru_b_r, lru_w_i=m_lru_w_i, lru_b_i=m_lru_b_i,
                lru_lambda=m_lru_lambda, lru_w_out=m_lru_w_out, fox_w_in=m_fox_w_in,
                fox_b_f=m_fox_b_f, fox_q_gain=m_fox_q_gain, fox_k_gain=m_fox_k_gain,
                fox_w_out=m_fox_w_out)
    v_in = dict(mix_norm=v_mix_norm, mlp_norm=v_mlp_norm, mlp_w1=v_mlp_w1, mlp_w2=v_mlp_w2,
                lru_w_in=v_lru_w_in, lru_conv_w=v_lru_conv_w, lru_conv_b=v_lru_conv_b,
                lru_w_r=v_lru_w_r, lru_b_r=v_lru_b_r, lru_w_i=v_lru_w_i, lru_b_i=v_lru_b_i,
                lru_lambda=v_lru_lambda, lru_w_out=v_lru_w_out, fox_w_in=v_fox_w_in,
                fox_b_f=v_fox_b_f, fox_q_gain=v_fox_q_gain, fox_k_gain=v_fox_k_gain,
                fox_w_out=v_fox_w_out)
    D = D_MODEL
    S = x.shape[1]
    x0, target = x[0], loss_target[0]
    me = 4 * lax.axis_index("x") + 2 * lax.axis_index("y") + lax.axis_index("c")

    def bf16(a):
        return a.astype(BF16)

    (lru_in_g,) = _exchange([bf16(lru_w_in[0])], [True], "gather_lru_in", two_level=True)
    gather_lru, tok = _exchange_start([bf16(lru_w_out[0]), lru_conv_w[0]], [True] * 2, lru_in_g,
                                      "gather_lru_start")
    gather_mlp0, tok = _exchange_start([bf16(mlp_w1[0]), bf16(mlp_w2[0])], [True] * 2, tok,
                                       "gather_mlp0_start", NEAR_PEERS)
    gather_fox, tok = _exchange_start([bf16(fox_w_in[0]), bf16(fox_w_out[0])], [True] * 2, tok,
                                      "gather_fox_start")
    gather_mlp1, tok = _exchange_start([bf16(mlp_w1[1]), bf16(mlp_w2[1])], [True] * 2, tok,
                                       "gather_mlp1_start", NEAR_PEERS)

    def pass_on(started, after, name):
        lands = _exchange_wait(started, [True] * 2, after, name + "_wait", NEAR_PEERS)
        return _forward_start(lands, after, name + "_pass_start")
    wr =_block_diag_pairs(lru_w_r[0]).astype(BF16)
    wi = _block_diag_pairs(lru_w_i[0]).astype(BF16)
    b_r, b_i = lru_b_r.reshape(1, D), lru_b_i.reshape(1, D)
    q_gain, k_gain = jnp.tile(fox_q_gain, (1, 2)), jnp.tile(fox_k_gain, (1, 2))
    b_f = jnp.pad(fox_b_f, ((0, 0), (0, LANES - N_HEADS)))
    g_mix0, g_mix1 = mix_norm[0:1] + tok[0, 0], mix_norm[1:2]
    g_mlp0, g_mlp1 = mlp_norm[0:1], mlp_norm[1:2]

    (u0,), h0 = _norm_matmul(x0, g_mix0, [lru_in_g], "lru_in_proj")
    lru_out_g, conv_g = _exchange_wait(gather_lru, [True] * 2, u0, "gather_lru_wait")
    lru_out_w = lru_out_g.reshape(D, D)
    conv_w = conv_g.transpose(1, 0, 2).reshape(CONV_WIDTH, D)
    y_lru, hs =_lru_fwd(u0, conv_w, lru_conv_b, wr, b_r, wi, b_i, lru_lambda, "lru_core")
    pass_mlp0, tok = pass_on(gather_mlp0, y_lru, "gather_mlp0")
    x1 = _matmul_res(y_lru, lru_out_w, x0, "lru_out_proj", tok)
    w1g0, w2g0 = _forward_wait(pass_mlp0, x1, "gather_mlp0_pass_wait")
    x2, h1, r1 = _mlp_fwd(x1, g_mlp0, w1g0, w2g0, "mlp0")
    fox_in_g, fox_out_g = _exchange_wait(gather_fox, [True] * 2, x2, "gather_fox_wait")
    fox_out_w = fox_out_g.reshape(D, D)
    fox_full = jnp.concatenate([fox_in_g[d] for d in range(N_DEV)], axis=1)
    wqkv = fox_full[:, :3 * D].reshape(D, 3, D).transpose(1, 0, 2)
    wf = jnp.pad(fox_full[:, 3 * D:], ((0, 0), (0, LANES - N_HEADS)))[None]
    uq, uk, f, h2, qn, kn, vb = _fox_in_proj(x2, g_mix1, wqkv, wf, q_gain, k_gain, "fox_in_proj")
    c_col = _forget_fwd(f, b_f, "fox_forget")
    c_row = c_col[:, :N_HEADS].T.reshape(N_CBLK, 2, S)
    o, lse = _attn_fwd(qn, kn, vb, c_row, "fox_attn")
    pass_mlp1, tok = pass_on(gather_mlp1, o, "gather_mlp1")
    x3 = _matmul_res(o, fox_out_w, x2, "fox_out_proj", tok)
    w1g1, w2g1 = _forward_wait(pass_mlp1, x3, "gather_mlp1_pass_wait")
    loss_local, dx4, h3, r3 = _mlp_fwd(x3, g_mlp1, w1g1, w2g1, "mlp1", target)

    dx3, dg_mlp1, da3, dx4_b = _mlp_bwd(dx4, x3, g_mlp1, r3, w1g1, w2g1, "mlp1_bwd")
    dw1_1 = _matmul_tn(h3, da3, "mlp1_dw1", cols=2, col_blocks=N_DEV)
    dw2_1 = _matmul_tn(r3, dx4_b, "mlp1_dw2", rows=2, a_square=True, tm=1024).reshape(N_DEV, -1, D)
    grads_mlp1, tok = _exchange_start([dw1_1, dw2_1], [False] * 2, tok, "grads_mlp1_start")
    do = _matmul_nt(dx3, fox_out_w, "fox_out_bwd", BF16, tok)
    d_fox_out = _matmul_tn(o, dx3, "fox_out_dw").reshape(N_DEV, -1, D)
    dqn, dkn, dv, dc_row, rho = _attn_bwd(qn, kn, vb, do, o, lse, c_row, "fox_attn_bwd")
    duq, duk, dq_gain, dk_gain = _qk_bwd(uq, uk, dqn, dkn, q_gain, k_gain, "fox_qk_norm_bwd")
    dc_k = jnp.pad(dc_row.reshape(N_HEADS, S).T, ((0, 0), (0, LANES - N_HEADS)))
    df, db_f = _forget_bwd(dc_k, rho, f, b_f, "fox_forget_bwd")
    dx2, dg_mix1 = _proj_bwd([[duq, duk, dv], [df]], [wqkv, wf], x2, g_mix1, dx3, "fox_in_bwd")
    d_fox_in = jnp.concatenate(
        [_matmul_tn(h2, duq, "fox_in_dwq"), _matmul_tn(h2, duk, "fox_in_dwk"),
         _matmul_tn(h2, dv, "fox_in_dwv"), _matmul_tn(h2, df, "fox_in_dwf")[:, :N_HEADS]], axis=1)
    shard = (3 * D + N_HEADS) // N_DEV
    d_fox_in = jnp.stack([d_fox_in[:, d * shard:(d + 1) * shard] for d in range(N_DEV)])
    grads_fox, tok = _exchange_start([d_fox_in, d_fox_out], [False] * 2, tok, "grads_fox_start")
    dx1, dg_mlp0, da1, dx2_b = _mlp_bwd(dx2, x1, g_mlp0 + tok[0, 0], r1, w1g0, w2g0, "mlp0_bwd")
    dw1_0 = _matmul_tn(h1, da1, "mlp0_dw1", cols=2, col_blocks=N_DEV)
    dw2_0 = _matmul_tn(r1, dx2_b, "mlp0_dw2", rows=2, a_square=True, tm=1024).reshape(N_DEV, -1, D)
    grads_mlp0, tok = _exchange_start([dw1_0, dw2_0], [False] * 2, tok, "grads_mlp0_start")
    dy_lru = _matmul_nt(dx1, lru_out_w, "lru_out_bwd", F32, tok)
    d_lru_out = _matmul_tn(y_lru, dx1, "lru_out_dw").reshape(N_DEV, -1, D)
    dgp, dxb, d_conv_w, d_conv_b, d_b_r, d_b_i, d_lam, d_wr, d_wi = _lru_bwd(
        dy_lru, u0, hs, conv_w, lru_conv_b, wr, b_r, wi, b_i, lru_lambda, "lru_core_bwd")

    small_grads = dict(
        mlp_norm=jnp.concatenate([dg_mlp0, dg_mlp1], axis=0),
        lru_conv_b=d_conv_b, lru_w_r=_diag_pairs(d_wr), lru_b_r=d_b_r, lru_w_i=_diag_pairs(d_wi),
        lru_b_i=d_b_i, lru_lambda=d_lam, fox_b_f=db_f[:, :N_HEADS],
        fox_q_gain=dq_gain[:, :HEAD_DIM], fox_k_gain=dk_gain[:, :HEAD_DIM])
    small_partial = _pack([dg_mix1] + [small_grads[n] for n in SMALL] + [d_conv_w])
    grads_lru_out, tok = _exchange_start([d_lru_out, small_partial], [False, True], tok,
                                         "grads_lru_out_start")
    dx0, dg_mix0 = _proj_bwd([[dgp, dxb]], [lru_in_g], x0, mix_norm[0:1] + tok[0, 0], dx1,
                             "lru_in_bwd")
    d_lru_in = jnp.concatenate([_matmul_tn(h0, dgp, "lru_in_dw_gate", col_blocks=4),
                                _matmul_tn(h0, dxb, "lru_in_dw_x", col_blocks=4)], axis=0)
    grads_lru_in, tok = _exchange_start([d_lru_in, dg_mix0], [False, True], tok,
                                        "grads_lru_in_start")

    grads, deltas, new_m, new_v = {}, {}, {}, {}

    def update(name, parts):
        w, m, v = w_in[name], m_in[name], v_in[name]
        shape = w.shape
        stacked = (len(parts), -1, shape[-1])
        w3 = w.reshape(stacked)
        res = _reduce_adamw([p.reshape((N_DEV,) + w3.shape[1:]) for p in parts], w3,
                            m.reshape(stacked), v.reshape(stacked), "adamw_" + name)
        return [r.reshape(shape) for r in res]

    def store(name, res):
        grads[name], deltas[name], new_m[name], new_v[name] = res

    p_w1_1, p_w2_1 = _exchange_wait(grads_mlp1, [False] * 2, tok, "grads_mlp1_wait")
    p_fox_in, p_fox_out = _exchange_wait(grads_fox, [False] * 2, p_w1_1, "grads_fox_wait")
    store("fox_w_in", update("fox_w_in", [p_fox_in]))
    store("fox_w_out", update("fox_w_out", [p_fox_out]))
    p_w1_0, p_w2_0 = _exchange_wait(grads_mlp0, [False] * 2, grads["fox_w_out"], "grads_mlp0_wait")
    store("mlp_w1", update("mlp_w1", [p_w1_0, p_w1_1]))
    store("mlp_w2", update("mlp_w2", [p_w2_0, p_w2_1]))
    p_lru_out, p_small = _exchange_wait(grads_lru_out, [False, True], grads["mlp_w2"],
                                        "grads_lru_out_wait")
    store("lru_w_out", update("lru_w_out", [p_lru_out]))
    p_lru_in, p_mix0 = _exchange_wait(grads_lru_in, [False, True], grads["lru_w_out"],
                                      "grads_lru_in_wait")
    store("lru_w_in", update("lru_w_in", [p_lru_in]))

    mix0 = [r[0] for r in _reduce_adamw([p_mix0], mix_norm[None, 0:1], m_mix_norm[None, 0:1],
                                        v_mix_norm[None, 0:1], "adamw_mix0")]
    packed = lambda src, first: _pack([first] + [src[n] for n in SMALL]
                                      + [jnp.zeros((CONV_WIDTH, D))])[None]
    small_shapes = [(1, D)] + [w_in[n].shape for n in SMALL]
    n_small = sum(math.prod(s) for s in small_shapes)
    res_small = _reduce_adamw([p_small], packed(w_in, mix_norm[1:2]), packed(m_in, m_mix_norm[1:2]),
                              packed(v_in, v_mix_norm[1:2]), "adamw_small")
    for name, *vals in zip(("mix1",) + SMALL, *[_unpack(r, small_shapes) for r in res_small]):
        if name == "mix1":
            vals = [jnp.concatenate([r0, r1], axis=0) for r0, r1 in zip(mix0, vals)]
            name = "mix_norm"
        store(name, vals)
    conv_parts = p_small.reshape(N_DEV, -1)[:, n_small:n_small + CONV_WIDTH * D]
    conv_parts = conv_parts.reshape(N_DEV, CONV_WIDTH, N_DEV, LANES)
    conv_parts = lax.dynamic_index_in_dim(conv_parts, me, axis=2, keepdims=False)
    store("lru_conv_w", update("lru_conv_w", [conv_parts]))

    loss = lax.psum(loss_local[0, 0], ("x", "y", "c"))
    return (loss, dx0[None], *[grads[n] for n in WEIGHTS], *[deltas[n] for n in WEIGHTS],
            *[new_m[n] for n in WEIGHTS], *[new_v[n] for n in WEIGHTS])
```

```python
import functools
import math

import jax
import jax.numpy as jnp
from jax import lax
from jax.experimental import pallas as pl
from jax.experimental.pallas import tpu as pltpu

F32 = jnp.float32
BF16 = jnp.bfloat16

N_DEV = 8
D_MODEL = 1024
D_FF = 4096
N_HEADS = 16
HEAD_DIM = 64
LRU_BLOCK_DIM = 64
CONV_WIDTH = 4
LRU_C = 8.0
EPS = 1e-6
NEG_INF = -1e30
ATTN_SCALE = HEAD_DIM ** -0.5
LANES = 128
N_CBLK = D_MODEL // LANES
VMEM_LIMIT = 52 * 2 ** 20

ADAM_LR = 0.001
ADAM_B1 = 0.9
ADAM_B2 = 0.999
ADAM_EPS = 1e-08
ADAM_WD = 0.01
ADAM_STEP = 10

_NT = (((1,), (1,)), ((), ()))
_TN = (((0,), (0,)), ((), ()))


def _params(*sem):
    return pltpu.CompilerParams(dimension_semantics=sem, vmem_limit_bytes=VMEM_LIMIT)


def _resident(shape):
    zeros = (0,) * len(shape)
    return pl.BlockSpec(shape, lambda *_: zeros, pipeline_mode=pl.Buffered(1))


def _dot(a, b):
    return jnp.dot(a, b, preferred_element_type=F32)


def _dot_nt(a, b):
    return lax.dot_general(a, b, _NT, preferred_element_type=F32)


def _dot_tn(a, b):
    return lax.dot_general(a, b, _TN, preferred_element_type=F32)


def _sigmoid(x):
    return 1.0 / (1.0 + jnp.exp(-x))


def _log_sigmoid(x):
    return -(jnp.maximum(-x, 0.0) + jnp.log1p(jnp.exp(-jnp.abs(x))))


def _expm1(x):
    poly = x * (1.0 + x * (0.5 + x * (1.0 / 6.0 + x * (1.0 / 24.0 + x * (1.0 / 120.0)))))
    return jnp.where(jnp.abs(x) < 0.1, poly, jnp.exp(x) - 1.0)


_GELU_K = 0.7978845608028654


def _gelu(x):
    return 0.5 * x * (1.0 + jnp.tanh(_GELU_K * (x + 0.044715 * (x * x * x))))


def _gelu_grad(x):
    t = jnp.tanh(_GELU_K * (x + 0.044715 * (x * x * x)))
    return 0.5 * (1.0 + t) + 0.5 * x * (1.0 - t * t) * (_GELU_K * (1.0 + 3 * 0.044715 * x * x))


def _rms_scale(x):
    return lax.rsqrt(jnp.mean(x * x, axis=-1, keepdims=True) + EPS)


def _norm_bwd(dh, x, g):
    rs = _rms_scale(x)
    xhat = x * rs
    dxhat = dh * g
    dx = rs * (dxhat - xhat * jnp.mean(dxhat * xhat, axis=-1, keepdims=True))
    return dx, jnp.sum(dh * xhat, axis=0, keepdims=True)


def _token_tile(S, want):
    tm = min(S, want)
    assert S % tm == 0
    return tm


def _norm_matmul(x, g, ws, name, tm=512):
    S, D = x.shape
    tm = _token_tile(S, tm)
    n = len(ws)

    def body(x_ref, g_ref, *refs):
        w_refs, o_refs, h_ref = refs[:n], refs[n:2 * n], refs[2 * n]
        xv = x_ref[...]
        h = (xv * _rms_scale(xv) * g_ref[...]).astype(BF16)
        h_ref[...] = h
        for w_ref, o_ref in zip(w_refs, o_refs):
            nb, _, nw = w_ref.shape
            for d in range(nb):
                o_ref[:, d * nw:(d + 1) * nw] = _dot(h, w_ref[d])

    widths = [w.shape[0] * w.shape[2] for w in ws]
    outs = pl.pallas_call(
        body, name=name, grid=(S // tm,),
        in_specs=[pl.BlockSpec((tm, D), lambda i: (i, 0)), _resident((1, D))]
        + [_resident(w.shape) for w in ws],
        out_specs=[pl.BlockSpec((tm, n_), lambda i: (i, 0)) for n_ in widths]
        + [pl.BlockSpec((tm, D), lambda i: (i, 0))],
        out_shape=[jax.ShapeDtypeStruct((S, n_), F32) for n_ in widths]
        + [jax.ShapeDtypeStruct((S, D), BF16)],
        compiler_params=_params("parallel"),
    )(x, g, *ws)
    return outs[:n], outs[n]


def _matmul_res(a, w, res, name, after, tm=512):
    S, K = a.shape
    N = w.shape[1]
    tm = _token_tile(S, tm)

    def body(a_ref, w_ref, r_ref, after_ref, o_ref):
        o_ref[...] = r_ref[...] + _dot(a_ref[...], w_ref[...])

    return pl.pallas_call(
        body, name=name, grid=(S // tm,),
        in_specs=[pl.BlockSpec((tm, K), lambda i: (i, 0)), _resident((K, N)),
                  pl.BlockSpec((tm, N), lambda i: (i, 0)), pl.BlockSpec(memory_space=pl.ANY)],
        out_specs=pl.BlockSpec((tm, N), lambda i: (i, 0)),
        out_shape=jax.ShapeDtypeStruct((S, N), F32),
        compiler_params=_params("parallel"),
    )(a, w, res, after)


def _matmul_nt(a, w, name, out_dtype, after, tm=1024):
    S, N = a.shape
    K = w.shape[0]
    tm = _token_tile(S, tm)

    def body(a_ref, w_ref, after_ref, o_ref):
        o_ref[...] = _dot_nt(a_ref[...].astype(BF16), w_ref[...]).astype(out_dtype)

    return pl.pallas_call(
        body, name=name, grid=(S // tm,),
        in_specs=[pl.BlockSpec((tm, N), lambda i: (i, 0)), _resident((K, N)),
                  pl.BlockSpec(memory_space=pl.ANY)],
        out_specs=pl.BlockSpec((tm, K), lambda i: (i, 0)),
        out_shape=jax.ShapeDtypeStruct((S, K), out_dtype),
        compiler_params=_params("parallel"),
    )(a, w, after)


def _proj_bwd(a_lists, w_list, x, g, res, name, tm=512):
    S, D = x.shape
    tm = _token_tile(S, tm)
    a_list = [a for group in a_lists for a in group]
    n, n_w = len(a_list), len(w_list)

    def body(*refs):
        a_refs, w_refs = list(refs[:n]), refs[n:n + n_w]
        x_ref, g_ref, r_ref, dx_ref, dg_ref = refs[n + n_w:]
        dh = jnp.zeros((tm, D), F32)
        for group, w_ref in zip(a_lists, w_refs):
            nw = w_ref.shape[2]
            d = 0
            for _ in group:
                a_ref = a_refs.pop(0)
                for j in range(a_ref.shape[1] // nw):
                    dh = dh + _dot_nt(a_ref[:, j * nw:(j + 1) * nw].astype(BF16), w_ref[d])
                    d += 1
        dx, dg = _norm_bwd(dh, x_ref[...], g_ref[...])
        dx_ref[...] = r_ref[...] + dx

        @pl.when(pl.program_id(0) == 0)
        def _():
            dg_ref[...] = jnp.zeros_like(dg_ref)
        dg_ref[...] += dg

    tok = lambda width: pl.BlockSpec((tm, width), lambda i: (i, 0))
    return pl.pallas_call(
        body, name=name, grid=(S // tm,),
        in_specs=[tok(a.shape[1]) for a in a_list] + [_resident(w.shape) for w in w_list]
        + [tok(D), _resident((1, D)), tok(D)],
        out_specs=[tok(D), pl.BlockSpec((1, D), lambda i: (0, 0))],
        out_shape=[jax.ShapeDtypeStruct((S, D), F32), jax.ShapeDtypeStruct((1, D), F32)],
        compiler_params=_params("arbitrary"),
    )(*a_list, *w_list, x, g, res)


def _matmul_tn(a, b, name, rows=1, cols=1, col_blocks=None, a_square=False, tm=2048):
    S, K = a.shape
    N = b.shape[1]
    tm = _token_tile(S, tm)
    n_tok = S // tm
    kr, nc = K // rows, N // cols

    def body(a_ref, b_ref, o_ref, acc_ref):
        av = a_ref[...]
        if a_square:
            av = av.astype(F32)
            av = av * av
        part = _dot_tn(av.astype(BF16), b_ref[...].astype(BF16))
        step = pl.program_id(2)

        @pl.when(step == 0)
        def _():
            acc_ref[...] = part

        @pl.when(step > 0)
        def _():
            acc_ref[...] += part

        @pl.when(step == n_tok - 1)
        def _():
            if col_blocks is None:
                o_ref[...] = acc_ref[...].astype(BF16)
            else:
                nw = N // col_blocks
                for d in range(col_blocks // cols):
                    o_ref[d] = acc_ref[:, d * nw:(d + 1) * nw].astype(BF16)

    if col_blocks is None:
        out_spec = pl.BlockSpec((kr, nc), lambda r, c, i: (r, c))
        out_shape = jax.ShapeDtypeStruct((K, N), BF16)
    else:
        assert rows == 1 and col_blocks % cols == 0
        per = col_blocks // cols
        out_spec = pl.BlockSpec((per, K, N // col_blocks), lambda r, c, i: (c, 0, 0))
        out_shape = jax.ShapeDtypeStruct((col_blocks, K, N // col_blocks), BF16)
    return pl.pallas_call(
        body, name=name, grid=(rows, cols, n_tok),
        in_specs=[pl.BlockSpec((tm, kr), lambda r, c, i: (i, r)),
                  pl.BlockSpec((tm, nc), lambda r, c, i: (i, c))],
        out_specs=out_spec, out_shape=out_shape,
        scratch_shapes=[pltpu.VMEM((kr, nc), F32)],
        compiler_params=_params("parallel", "parallel", "arbitrary"),
    )(a, b)


def _mlp_fwd(x, g, w1, w2, name, target=None, tm=512):
    S, D = x.shape
    nb, _, fb = w1.shape
    tm = _token_tile(S, tm)
    with_loss = target is not None

    def body(x_ref, g_ref, w1_ref, w2_ref, *refs):
        h_ref, r_ref = refs[-2:]
        xv = x_ref[...]
        h = (xv * _rms_scale(xv) * g_ref[...]).astype(BF16)
        h_ref[...] = h
        acc = xv
        for d in range(nb):
            r = jnp.maximum(_dot(h, w1_ref[d]), 0.0)
            r_ref[:, d * fb:(d + 1) * fb] = r.astype(BF16)
            acc = acc + _dot((r * r).astype(BF16), w2_ref[d])
        if not with_loss:
            refs[0][...] = acc
            return
        t_ref, loss_ref, dy_ref = refs[:3]
        err = acc - t_ref[...]
        dy_ref[...] = err / D

        @pl.when(pl.program_id(0) == 0)
        def _():
            loss_ref[...] = jnp.zeros_like(loss_ref)
        row_loss = jnp.mean(err * err, axis=1, keepdims=True)
        loss_ref[...] += 0.5 * jnp.sum(row_loss, axis=0, keepdims=True)

    tok = lambda width: pl.BlockSpec((tm, width), lambda i: (i, 0))
    saved_specs = [tok(D), tok(nb * fb)]
    saved_shapes = [jax.ShapeDtypeStruct((S, D), BF16), jax.ShapeDtypeStruct((S, nb * fb), BF16)]
    wide = jax.ShapeDtypeStruct((S, D), F32)
    if with_loss:
        head_specs = [pl.BlockSpec((1, 1), lambda i: (0, 0)), tok(D)]
        head_shapes = [jax.ShapeDtypeStruct((1, 1), F32), wide]
    else:
        head_specs, head_shapes = [tok(D)], [wide]
    return pl.pallas_call(
        body, name=name, grid=(S // tm,),
        in_specs=[tok(D), _resident((1, D)), _resident(w1.shape), _resident(w2.shape)]
        + ([tok(D)] if with_loss else []),
        out_specs=head_specs + saved_specs, out_shape=head_shapes + saved_shapes,
        compiler_params=_params("arbitrary" if with_loss else "parallel"),
    )(x, g, w1, w2, *([target] if with_loss else []))


def _mlp_bwd(dout, x, g, r, w1, w2, name, tm=512):
    S, D = x.shape
    nb, _, fb = w1.shape
    tm = _token_tile(S, tm)

    def body(do_ref, x_ref, g_ref, r_ref, w1_ref, w2_ref, dx_ref, dg_ref, da_ref, dob_ref):
        dov = do_ref[...]
        dob = dov.astype(BF16)
        dob_ref[...] = dob
        dh = jnp.zeros((tm, D), F32)
        for d in range(nb):
            dz = _dot_nt(dob, w2_ref[d])
            da = (dz * (2.0 * r_ref[:, d * fb:(d + 1) * fb].astype(F32))).astype(BF16)
            da_ref[:, d * fb:(d + 1) * fb] = da
            dh = dh + _dot_nt(da, w1_ref[d])
        dx, dg = _norm_bwd(dh, x_ref[...], g_ref[...])
        dx_ref[...] = dov + dx

        @pl.when(pl.program_id(0) == 0)
        def _():
            dg_ref[...] = jnp.zeros_like(dg_ref)
        dg_ref[...] += dg

    tok = lambda width: pl.BlockSpec((tm, width), lambda i: (i, 0))
    return pl.pallas_call(
        body, name=name, grid=(S // tm,),
        in_specs=[tok(D), tok(D), _resident((1, D)), tok(nb * fb), _resident(w1.shape),
                  _resident(w2.shape)],
        out_specs=[tok(D), pl.BlockSpec((1, D), lambda i: (0, 0)), tok(nb * fb), tok(D)],
        out_shape=[jax.ShapeDtypeStruct((S, D), F32), jax.ShapeDtypeStruct((1, D), F32),
                   jax.ShapeDtypeStruct((S, nb * fb), BF16), jax.ShapeDtypeStruct((S, D), BF16)],
        compiler_params=_params("arbitrary"),
    )(dout, x, g, r, w1, w2)


def _scan_chunk(a, b, row, T, reverse):
    s = 1
    while s < T:
        if reverse:
            keep, shift = row < T - s, T - s
        else:
            keep, shift = row >= s, s
        a_sh = jnp.where(keep, pltpu.roll(a, shift, 0), 1.0)
        b_sh = jnp.where(keep, pltpu.roll(b, shift, 0), 0.0)
        b = a * b_sh + b
        a = a * a_sh
        s *= 2
    return a, b


def _row_of(x, row, r):
    return jnp.sum(jnp.where(row == r, x, 0.0), axis=0, keepdims=True)


def _shift_down(x, prev, row, k):
    if k == 0:
        return x
    return jnp.where(row < k, pltpu.roll(prev, k, 0), pltpu.roll(x, k, 0))


def _shift_up(x, nxt, row, k, T):
    if k == 0:
        return x
    return jnp.where(row < T - k, pltpu.roll(x, T - k, 0), pltpu.roll(nxt, T - k, 0))


def _lru_gates(xb, prev_xb, row, cw_ref, cb, wr, br, wi, bi, ls):
    xc = cb + cw_ref[pl.ds(0, 1), :] * _shift_down(xb, prev_xb, row, 3)
    for k in (2, 1, 0):
        xc = xc + cw_ref[pl.ds(3 - k, 1), :] * _shift_down(xb, prev_xb, row, k)
    xcb = xc.astype(BF16)
    r = _sigmoid(_dot(xcb, wr) + br)
    i = _sigmoid(_dot(xcb, wi) + bi)
    la = (LRU_C * r) * ls
    a = jnp.exp(la)
    m = jnp.sqrt(-_expm1(2.0 * la))
    return xc, xcb, r, i, a, m


def _lru_specs(S):
    col = lambda off: pl.BlockSpec((S, LANES), lambda j: (0, j + off))
    vec = pl.BlockSpec((1, LANES), lambda j: (0, j))
    mat = pl.BlockSpec((None, LANES, LANES), lambda j: (j, 0, 0))
    cwm = pl.BlockSpec((CONV_WIDTH, LANES), lambda j: (0, j))
    return col, vec, mat, cwm


def _lru_fwd(u, conv_w, conv_b, wr, br, wi, bi, lam, name):
    S = u.shape[0]
    T = _token_tile(S, 512)
    col, vec, mat, cwm = _lru_specs(S)

    def body(gp_ref, xb_ref, cw_ref, cb_ref, wr_ref, br_ref, wi_ref, bi_ref, lam_ref,
             y_ref, hs_ref):
        row = lax.broadcasted_iota(jnp.int32, (T, LANES), 0)
        ls = _log_sigmoid(lam_ref[...])
        cb, br, bi = cb_ref[...], br_ref[...], bi_ref[...]
        wr, wi = wr_ref[...], wi_ref[...]

        def chunk(ci, carry):
            prev_xb, hc = carry
            rows = pl.ds(pl.multiple_of(ci * T, T), T)
            xb = xb_ref[rows, :]
            xc, _, _, i, a, m = _lru_gates(xb, prev_xb, row, cw_ref, cb, wr, br, wi, bi, ls)
            ca, cbv = _scan_chunk(a, m * (i * xc), row, T, reverse=False)
            h = ca * hc + cbv
            hs_ref[rows, :] = h
            y_ref[rows, :] = (_gelu(gp_ref[rows, :]) * h).astype(BF16)
            return xb, _row_of(h, row, T - 1)

        lax.fori_loop(0, S // T, chunk,
                      (jnp.zeros((T, LANES), F32), jnp.zeros((1, LANES), F32)))

    return pl.pallas_call(
        body, name=name, grid=(N_CBLK,),
        in_specs=[col(0), col(N_CBLK), cwm, vec, mat, vec, mat, vec, vec],
        out_specs=[col(0), col(0)],
        out_shape=[jax.ShapeDtypeStruct((S, D_MODEL), BF16), jax.ShapeDtypeStruct((S, D_MODEL), F32)],
        compiler_params=_params("parallel"),
    )(u, u, conv_w, conv_b, wr, br, wi, bi, lam)


def _lru_bwd(dy, u, hs, conv_w, conv_b, wr, br, wi, bi, lam, name):
    S = u.shape[0]
    T = _token_tile(S, 512)
    n_chunk = S // T
    col, vec, mat, cwm = _lru_specs(S)

    def body(dy_ref, gp_ref, xb_ref, hs_ref, cw_ref, cb_ref, wr_ref, br_ref, wi_ref, bi_ref,
             lam_ref, dgp_ref, dxb_ref, dcw_ref, dcb_ref, dbr_ref, dbi_ref, dlam_ref, dwr_ref,
             dwi_ref):
        row = lax.broadcasted_iota(jnp.int32, (T, LANES), 0)
        lam = lam_ref[...]
        ls = _log_sigmoid(lam)
        cb, br, bi = cb_ref[...], br_ref[...], bi_ref[...]
        wr, wi = wr_ref[...], wi_ref[...]
        for ref in (dcw_ref, dcb_ref, dbr_ref, dbi_ref, dlam_ref, dwr_ref, dwi_ref):
            ref[...] = jnp.zeros_like(ref)

        def chunk(it, carry):
            g_next, dxc_next = carry
            ci = n_chunk - 1 - it
            rows = pl.ds(pl.multiple_of(ci * T, T), T)
            before = pl.ds(pl.multiple_of(jnp.maximum(ci - 1, 0) * T, T), T)
            first = ci == 0
            xb = xb_ref[rows, :]
            prev_xb = jnp.where(first, 0.0, xb_ref[before, :])
            xc, xcb, r, i, a, m = _lru_gates(xb, prev_xb, row, cw_ref, cb, wr, br, wi, bi, ls)
            h = hs_ref[rows, :]
            h_prev = _shift_down(h, jnp.where(first, 0.0, hs_ref[before, :]), row, 1)
            gp = gp_ref[rows, :]
            dyv = dy_ref[rows, :]
            dgp_ref[rows, :] = (dyv * h * _gelu_grad(gp)).astype(BF16)
            dh = dyv * _gelu(gp)
            ca, cbv = _scan_chunk(a, a * dh, row, T, reverse=True)
            gp_acc = ca * g_next + cbv
            g = dh + jnp.where(row < T - 1, pltpu.roll(gp_acc, T - 1, 0), g_next)
            da = g * h_prev - (g * (i * xc)) * a / m
            dla = da * a
            dlam_ref[...] += jnp.sum(dla * (LRU_C * r), axis=0, keepdims=True)
            dpr = (dla * (LRU_C * ls)) * r * (1.0 - r)
            dpi = (g * m * xc) * i * (1.0 - i)
            dbr_ref[...] += jnp.sum(dpr, axis=0, keepdims=True)
            dbi_ref[...] += jnp.sum(dpi, axis=0, keepdims=True)
            dprb, dpib = dpr.astype(BF16), dpi.astype(BF16)
            dwr_ref[...] += _dot_tn(xcb, dprb)
            dwi_ref[...] += _dot_tn(xcb, dpib)
            dxc = g * m * i + _dot_nt(dprb, wr) + _dot_nt(dpib, wi)
            dcb_ref[...] += jnp.sum(dxc, axis=0, keepdims=True)
            dxb = jnp.zeros((T, LANES), F32)
            for k in range(CONV_WIDTH):
                tap = pl.ds(CONV_WIDTH - 1 - k, 1)
                dcw_ref[tap, :] += jnp.sum(dxc * _shift_down(xb, prev_xb, row, k), axis=0,
                                           keepdims=True)
                dxb = dxb + cw_ref[tap, :] * _shift_up(dxc, dxc_next, row, k, T)
            dxb_ref[rows, :] = dxb.astype(BF16)
            return _row_of(gp_acc, row, 0), dxc

        lax.fori_loop(0, n_chunk, chunk,
                      (jnp.zeros((1, LANES), F32), jnp.zeros((T, LANES), F32)))
        dlam_ref[...] = dlam_ref[...] * _sigmoid(-lam)

    vec_out = jax.ShapeDtypeStruct((1, D_MODEL), F32)
    mat_out = jax.ShapeDtypeStruct((N_CBLK, LANES, LANES), F32)
    return pl.pallas_call(
        body, name=name, grid=(N_CBLK,),
        in_specs=[col(0), col(0), col(N_CBLK), col(0), cwm, vec, mat, vec, mat, vec, vec],
        out_specs=[col(0), col(0), cwm, vec, vec, vec, vec, mat, mat],
        out_shape=[jax.ShapeDtypeStruct((S, D_MODEL), BF16), jax.ShapeDtypeStruct((S, D_MODEL), BF16),
                   jax.ShapeDtypeStruct((CONV_WIDTH, D_MODEL), F32),
                   vec_out, vec_out, vec_out, vec_out, mat_out, mat_out],
        compiler_params=_params("parallel"),
    )(dy, u, u, hs, conv_w, conv_b, wr, br, wi, bi, lam)


def _head_group_matrix(value):
    r = lax.broadcasted_iota(jnp.int32, (LANES, LANES), 0) // HEAD_DIM
    c = lax.broadcasted_iota(jnp.int32, (LANES, LANES), 1) // HEAD_DIM
    return jnp.where(r == c, value, 0.0).astype(BF16)


def _group_dot(x, p):
    hi = x.astype(BF16)
    lo = (x - hi.astype(F32)).astype(BF16)
    return _dot(hi, p) + _dot(lo, p)


def _head_mean(x, p):
    return _group_dot(x, p)


def _fox_in_proj(x, g, wqkv, wf, q_gain, k_gain, name, tm=512):
    S, D = x.shape
    tm = _token_tile(S, tm)

    def body(x_ref, g_ref, w_ref, wf_ref, qg_ref, kg_ref,
             uq_ref, uk_ref, f_ref, h_ref, qn_ref, kn_ref, vb_ref):
        xv = x_ref[...]
        h = (xv * _rms_scale(xv) * g_ref[...]).astype(BF16)
        h_ref[...] = h
        p = _head_group_matrix(1.0 / HEAD_DIM)
        for which, u_ref, gain_ref, n_ref, scale in ((0, uq_ref, qg_ref, qn_ref, ATTN_SCALE),
                                                     (1, uk_ref, kg_ref, kn_ref, 1.0)):
            u = _dot(h, w_ref[which])
            u_ref[...] = u
            for j in range(N_CBLK):
                cl = slice(j * LANES, (j + 1) * LANES)
                uv = u[:, cl]
                rs = lax.rsqrt(_head_mean(uv * uv, p) + EPS)
                n_ref[:, cl] = (uv * rs * gain_ref[...]).astype(BF16) * scale
        vb_ref[...] = _dot(h, w_ref[2]).astype(BF16)
        f_ref[...] = _dot(h, wf_ref[0])

    tok = lambda width: pl.BlockSpec((tm, width), lambda i: (i, 0))
    wide = jax.ShapeDtypeStruct((S, D), F32)
    half = jax.ShapeDtypeStruct((S, D), BF16)
    return pl.pallas_call(
        body, name=name, grid=(S // tm,),
        in_specs=[tok(D), _resident((1, D)), _resident(wqkv.shape), _resident(wf.shape),
                  _resident((1, LANES)), _resident((1, LANES))],
        out_specs=[tok(D), tok(D), tok(LANES), tok(D), tok(D), tok(D), tok(D)],
        out_shape=[wide, wide, jax.ShapeDtypeStruct((S, LANES), F32), half, half, half, half],
        compiler_params=_params("parallel"),
    )(x, g, wqkv, wf, q_gain, k_gain)


def _qk_bwd(uq, uk, dqn, dkn, q_gain, k_gain, name, tm=512):
    S = uq.shape[0]
    tm = _token_tile(S, tm)

    def body(q_ref, k_ref, dqn_ref, dkn_ref, qg_ref, kg_ref, dq_ref, dk_ref, dqg_ref, dkg_ref):
        p = _head_group_matrix(1.0 / HEAD_DIM)
        for x_ref, dn_ref, g_ref, dx_ref, dg_ref, scale in (
                (q_ref, dqn_ref, qg_ref, dq_ref, dqg_ref, ATTN_SCALE),
                (k_ref, dkn_ref, kg_ref, dk_ref, dkg_ref, 1.0)):
            dg = jnp.zeros((1, LANES), F32)
            for j in range(N_CBLK):
                cl = slice(j * LANES, (j + 1) * LANES)
                xv, dn = x_ref[:, cl], dn_ref[:, cl] * scale
                rs = lax.rsqrt(_head_mean(xv * xv, p) + EPS)
                xhat = xv * rs
                dxhat = dn * g_ref[...]
                dx_ref[:, cl] = (rs * (dxhat - xhat * _head_mean(dxhat * xhat, p))).astype(BF16)
                dg = dg + jnp.sum(dn * xhat, axis=0, keepdims=True)

            @pl.when(pl.program_id(0) == 0)
            def _():
                dg_ref[...] = jnp.zeros_like(dg_ref)
            dg_ref[...] += dg

            @pl.when(pl.program_id(0) == S // tm - 1)
            def _():
                dg_ref[...] += pltpu.roll(dg_ref[...], HEAD_DIM, 1)

    blk = lambda off: pl.BlockSpec((tm, D_MODEL), lambda i: (i, off))
    acc = pl.BlockSpec((1, LANES), lambda i: (0, 0))
    out = jax.ShapeDtypeStruct((S, D_MODEL), BF16)
    vec = jax.ShapeDtypeStruct((1, LANES), F32)
    return pl.pallas_call(
        body, name=name, grid=(S // tm,),
        in_specs=[blk(0), blk(0), blk(0), blk(0), _resident((1, LANES)), _resident((1, LANES))],
        out_specs=[blk(0), blk(0), acc, acc],
        out_shape=[out, out, vec, vec],
        compiler_params=_params("arbitrary"),
    )(uq, uk, dqn, dkn, q_gain, k_gain)


def _forget_fwd(f, b_f, name):
    S = f.shape[0]
    T = _token_tile(S, 256)

    def body(f_ref, b_ref, c_ref):
        row = lax.broadcasted_iota(jnp.int32, (T, LANES), 0)
        ones = jnp.ones((T, LANES), F32)
        bias = b_ref[...]

        def chunk(ci, carry):
            rows = pl.ds(pl.multiple_of(ci * T, T), T)
            _, c = _scan_chunk(ones, _log_sigmoid(f_ref[rows, :] + bias), row, T, reverse=False)
            c = c + carry
            c_ref[rows, :] = c
            return _row_of(c, row, T - 1)

        lax.fori_loop(0, S // T, chunk, jnp.zeros((1, LANES), F32))

    return pl.pallas_call(
        body, name=name,
        in_specs=[pl.BlockSpec(memory_space=pltpu.VMEM)] * 2,
        out_specs=pl.BlockSpec(memory_space=pltpu.VMEM),
        out_shape=jax.ShapeDtypeStruct((S, LANES), F32),
        compiler_params=pltpu.CompilerParams(vmem_limit_bytes=VMEM_LIMIT),
    )(f, b_f)


def _forget_bwd(dc_k, rho, f, b_f, name):
    S = f.shape[0]
    T = _token_tile(S, 256)
    n_chunk = S // T

    def body(dck_ref, rho_ref, f_ref, b_ref, df_ref, db_ref):
        row = lax.broadcasted_iota(jnp.int32, (T, LANES), 0)
        ones = jnp.ones((T, LANES), F32)
        bias = b_ref[...]
        pick = (lax.broadcasted_iota(jnp.int32, (D_MODEL, LANES), 0)
                == HEAD_DIM * lax.broadcasted_iota(jnp.int32, (D_MODEL, LANES), 1))
        pick = jnp.where(pick, 1.0, 0.0).astype(BF16)

        def chunk(it, carry):
            tail, db = carry
            rows = pl.ds(pl.multiple_of((n_chunk - 1 - it) * T, T), T)
            dc = dck_ref[rows, :] + _group_dot(rho_ref[rows, :], pick)
            _, dlf = _scan_chunk(ones, dc, row, T, reverse=True)
            dlf = dlf + tail
            df = dlf * _sigmoid(-(f_ref[rows, :] + bias))
            df_ref[rows, :] = df
            return _row_of(dlf, row, 0), db + jnp.sum(df, axis=0, keepdims=True)

        zero = jnp.zeros((1, LANES), F32)
        _, db = lax.fori_loop(0, n_chunk, chunk, (zero, zero))
        db_ref[...] = db

    return pl.pallas_call(
        body, name=name,
        in_specs=[pl.BlockSpec(memory_space=pltpu.VMEM)] * 4,
        out_specs=[pl.BlockSpec(memory_space=pltpu.VMEM)] * 2,
        out_shape=[jax.ShapeDtypeStruct((S, LANES), F32), jax.ShapeDtypeStruct((1, LANES), F32)],
        compiler_params=pltpu.CompilerParams(vmem_limit_bytes=VMEM_LIMIT),
    )(dc_k, rho, f, b_f)


ATTN_TILE = 512
ATTN_ROWS_FWD = 32


def _attn_tiles(S):
    t = _token_tile(S, ATTN_TILE)
    return t, S // t


def _causal(T):
    return (lax.broadcasted_iota(jnp.int32, (T, T), 1)
            <= lax.broadcasted_iota(jnp.int32, (T, T), 0))


def _attn_fwd(qs_, kn, vb, c_row, name):
    S = qs_.shape[0]
    T, n_t = _attn_tiles(S)
    RB = min(T, ATTN_ROWS_FWD)

    def body(q_ref, k_ref, v_ref, cr_ref, o_ref, lse_ref, sa_ref, sb_ref, p_ref, m_ref, l_ref,
             acc_ref, a_ref):
        qi = pl.program_id(1)
        lanes = [slice(h2 * HEAD_DIM, (h2 + 1) * HEAD_DIM) for h2 in range(2)]
        col = lax.broadcasted_iota(jnp.int32, (RB, T), 1)
        row = lax.broadcasted_iota(jnp.int32, (RB, T), 0)
        m_ref[...] = jnp.full(m_ref.shape, NEG_INF, F32)
        l_ref[...] = jnp.zeros_like(l_ref)
        acc_ref[...] = jnp.zeros_like(acc_ref)

        def logits_into(s_ref, kj):
            ks = pl.ds(pl.multiple_of(kj * T, T), T)
            for h2, hl in enumerate(lanes):
                s_ref[h2] = _dot_nt(q_ref[:, hl], k_ref[ks, hl]) - cr_ref[h2:h2 + 1, ks]

        def consume(s_ref, kj, masked):
            ks = pl.ds(pl.multiple_of(kj * T, T), T)
            for h2, hl in enumerate(lanes):
                blocks = [slice(i * RB, (i + 1) * RB) for i in range(T // RB)]

                def logits(i, rows):
                    s = s_ref[h2, rows, :]
                    return jnp.where(col <= row + i * RB, s, NEG_INF) if masked else s

                wide = lambda x: jnp.broadcast_to(x, (RB, LANES))
                for i, rows in enumerate(blocks):
                    mx = wide(jnp.max(logits(i, rows), axis=1, keepdims=True))
                    a_ref[h2, rows, :] = m_ref[h2, rows, :]
                    m_ref[h2, rows, :] = jnp.maximum(m_ref[h2, rows, :], mx)
                for i, rows in enumerate(blocks):
                    m_new = m_ref[h2, rows, :]
                    p = jnp.exp(logits(i, rows) - jnp.tile(m_new, (1, T // LANES)))
                    alpha = jnp.exp(a_ref[h2, rows, :] - m_new)
                    a_ref[h2, rows, :] = alpha
                    l_ref[h2, rows, :] = (alpha * l_ref[h2, rows, :]
                                          + wide(jnp.sum(p, axis=1, keepdims=True)))
                    p_ref[h2, rows, :] = p.astype(BF16)
                acc_ref[h2] = (a_ref[h2, :, :HEAD_DIM] * acc_ref[h2]
                               + _dot(p_ref[h2], v_ref[ks, hl]))

        logits_into(sa_ref, 0)

        def pair(i, _):
            logits_into(sb_ref, 2 * i + 1)
            consume(sa_ref, 2 * i, False)
            logits_into(sa_ref, 2 * i + 2)
            consume(sb_ref, 2 * i + 1, False)
            return 0

        lax.fori_loop(0, qi // 2, pair, 0)

        @pl.when(qi % 2 == 1)
        def _():
            logits_into(sb_ref, qi)
            consume(sa_ref, qi - 1, False)
            consume(sb_ref, qi, True)

        @pl.when(qi % 2 == 0)
        def _():
            consume(sa_ref, qi, True)

        for h2, hl in enumerate(lanes):
            o_ref[:, hl] = (acc_ref[h2] / l_ref[h2, :, :HEAD_DIM]).astype(BF16)
            lse_ref[:, hl] = m_ref[h2, :, :HEAD_DIM] + jnp.log(l_ref[h2, :, :HEAD_DIM])

    qblk = pl.BlockSpec((T, LANES), lambda h, i: (i, h))
    kv = pl.BlockSpec((S, LANES), lambda h, i: (0, h))
    return pl.pallas_call(
        body, name=name, grid=(N_CBLK, n_t),
        in_specs=[qblk, kv, kv, pl.BlockSpec((None, 2, S), lambda h, i: (h, 0, 0))],
        out_specs=[qblk, qblk],
        out_shape=[jax.ShapeDtypeStruct((S, D_MODEL), BF16),
                   jax.ShapeDtypeStruct((S, D_MODEL), F32)],
        scratch_shapes=[pltpu.VMEM((2, T, T), F32), pltpu.VMEM((2, T, T), F32),
                        pltpu.VMEM((2, T, T), BF16),
                        pltpu.VMEM((2, T, LANES), F32), pltpu.VMEM((2, T, LANES), F32),
                        pltpu.VMEM((2, T, HEAD_DIM), F32), pltpu.VMEM((2, T, LANES), F32)],
        compiler_params=_params("parallel", "parallel"),
    )(qs_, kn, vb, c_row)


def _attn_bwd(qs_, kn, vb, do, o, lse, c_row, name):
    S = qs_.shape[0]
    T, n_t = _attn_tiles(S)

    def body(q_ref, k_ref, v_ref, do_ref, o_ref, lse_ref, cr_ref,
             dq_ref, dk_ref, dv_ref, dc_ref, rho_ref, dd_ref):
        kj = pl.program_id(1)
        causal = _causal(T)
        lanes = [slice(h2 * HEAD_DIM, (h2 + 1) * HEAD_DIM) for h2 in range(2)]
        ones = [slice(h2 * HEAD_DIM, h2 * HEAD_DIM + 1) for h2 in range(2)]

        @pl.when(kj == 0)
        def _():
            dq_ref[...] = jnp.zeros_like(dq_ref)
            rho_ref[...] = jnp.zeros_like(rho_ref)
            p_sum = _head_group_matrix(1.0)

            def fill(ci, _):
                rows = pl.ds(pl.multiple_of(ci * T, T), T)
                dd_ref[rows, :] = _group_dot(do_ref[rows, :].astype(F32) * o_ref[rows, :].astype(F32),
                                             p_sum)
                return 0

            lax.fori_loop(0, n_t, fill, 0)

        kh = [k_ref[:, hl] for hl in lanes]
        vh = [v_ref[:, hl] for hl in lanes]
        ck = [cr_ref[h2:h2 + 1, :] for h2 in range(2)]

        def step(qi, carry, masked):
            qs = pl.ds(pl.multiple_of(qi * T, T), T)
            out = []
            for h2, hl in enumerate(lanes):
                dk, dv, dc = carry[h2]
                qh, doh = q_ref[qs, hl], do_ref[qs, hl]
                s = _dot_nt(qh, kh[h2]) - ck[h2]
                if masked:
                    s = jnp.where(causal, s, NEG_INF)
                p = jnp.exp(s - lse_ref[qs, ones[h2]])
                ds = p * (_dot_nt(doh, vh[h2]) - dd_ref[qs, ones[h2]])
                dsb = ds.astype(BF16)
                dq_ref[qs, hl] += _dot(dsb, kh[h2])
                rho_ref[qs, hl] += jnp.broadcast_to(jnp.sum(ds, axis=1, keepdims=True),
                                                    (T, HEAD_DIM))
                out.append((dk + _dot_tn(dsb, qh), dv + _dot_tn(p.astype(BF16), doh),
                            dc - jnp.sum(ds, axis=0, keepdims=True)))
            return tuple(out)

        init = tuple((jnp.zeros((T, HEAD_DIM), F32), jnp.zeros((T, HEAD_DIM), F32),
                      jnp.zeros((1, T), F32)) for _ in lanes)
        carry = step(kj, init, True)
        carry = lax.fori_loop(kj + 1, n_t, lambda qi, c: step(qi, c, False), carry)
        for h2, ((dk, dv, dc), hl) in enumerate(zip(carry, lanes)):
            dk_ref[:, hl] = dk
            dv_ref[:, hl] = dv.astype(BF16)
            dc_ref[h2:h2 + 1, :] = dc

    kblk = pl.BlockSpec((T, LANES), lambda h, j: (j, h))
    full = pl.BlockSpec((S, LANES), lambda h, j: (0, h))
    crow = pl.BlockSpec((None, 2, T), lambda h, j: (h, 0, j))
    wide = jax.ShapeDtypeStruct((S, D_MODEL), F32)
    return pl.pallas_call(
        body, name=name, grid=(N_CBLK, n_t),
        in_specs=[full, kblk, kblk, full, full, full, crow],
        out_specs=[full, kblk, kblk, crow, full],
        out_shape=[wide, wide, jax.ShapeDtypeStruct((S, D_MODEL), BF16),
                   jax.ShapeDtypeStruct((N_CBLK, 2, S), F32), wide],
        scratch_shapes=[pltpu.VMEM((S, LANES), F32)],
        compiler_params=_params("parallel", "arbitrary"),
    )(qs_, kn, vb, do, o, lse, c_row)


ALL_PEERS = tuple(range(1, N_DEV))
NEAR_PEERS = (1, 2, 4, 6)
FAR_CHIPS = (2, 4, 6)


def _landing_shapes(arrays, gathers):
    return [jax.ShapeDtypeStruct((N_DEV,) + a.shape if g else a.shape, a.dtype)
            for a, g in zip(arrays, gathers)]


def _my_index():
    return 4 * lax.axis_index("x") + 2 * lax.axis_index("y") + lax.axis_index("c")


def _own_copies(srcs, lands, gathers, sems):
    me = _my_index()
    return [pltpu.make_async_copy(src if g else src.at[me], land.at[me], sems.at[a])
            for a, (src, land, g) in enumerate(zip(srcs, lands, gathers))]


def _peer_copies(srcs, lands, gathers, send_sems, recv_sems, ks=ALL_PEERS):
    x, y, c = lax.axis_index("x"), lax.axis_index("y"), lax.axis_index("c")
    me = 4 * x + 2 * y + c
    out = []
    for j, k in enumerate(ks):
        to = (1 - x if k & 4 else x, 1 - y if k & 2 else y, 1 - c if k & 1 else c)
        peer = 4 * to[0] + 2 * to[1] + to[2]
        for a, (src, land, g) in enumerate(zip(srcs, lands, gathers)):
            sem = a * len(ks) + j
            src_blk = src if g else src.at[peer]

            def copy(slot, src_blk=src_blk, land=land, sem=sem, to=to):
                return pltpu.make_async_remote_copy(
                    src_ref=src_blk, dst_ref=land.at[slot], send_sem=send_sems.at[sem],
                    recv_sem=recv_sems.at[sem], device_id=to,
                    device_id_type=pl.DeviceIdType.MESH)

            out.append((k, a, copy(me), copy(peer)))
    return out


def _forward_copies(lands, send_sems, recv_sems):
    x, y, c = lax.axis_index("x"), lax.axis_index("y"), lax.axis_index("c")
    out = []
    for j, f in enumerate(FAR_CHIPS):
        chip = 4 * (1 - x if f & 4 else x) + 2 * (1 - y if f & 2 else y)
        for a, land in enumerate(lands):
            sem = a * len(FAR_CHIPS) + j

            def copy(slot, land=land, sem=sem):
                return pltpu.make_async_remote_copy(
                    src_ref=land.at[slot], dst_ref=land.at[slot], send_sem=send_sems.at[sem],
                    recv_sem=recv_sems.at[sem], device_id=(x, y, 1 - c),
                    device_id_type=pl.DeviceIdType.MESH)

            out.append((f, a, copy(chip + c), copy(chip + 1 - c)))
    return out


def _exchange(arrays, gathers, name, two_level=False):
    n = len(arrays)
    ks = NEAR_PEERS if two_level else ALL_PEERS
    assert not two_level or all(gathers)

    def body(*refs):
        ins, outs = refs[:n], refs[n:2 * n]
        send_sems, recv_sems, own_sems, fwd_send_sems, fwd_recv_sems = refs[2 * n:]
        own = _own_copies(ins, outs, gathers, own_sems)
        for cp in own:
            cp.start()
        copies = _peer_copies(ins, outs, gathers, send_sems, recv_sems, ks)
        for _, _, send, _ in copies:
            send.start()
        passed = {}
        if two_level:
            passed = {(f, a): (send, arrival)
                      for f, a, send, arrival in _forward_copies(outs, fwd_send_sems, fwd_recv_sems)}
        for k, a, _, arrival in copies:
            arrival.wait_recv()
            if (k, a) in passed:
                passed[k, a][0].start()
        for send, arrival in passed.values():
            arrival.wait_recv()
            send.wait_send()
        for _, _, send, _ in copies:
            send.wait_send()
        for cp in own:
            cp.wait()

    hbm = pl.BlockSpec(memory_space=pl.ANY)
    return pl.pallas_call(
        body, name=name,
        in_specs=[hbm] * n, out_specs=[hbm] * n, out_shape=_landing_shapes(arrays, gathers),
        scratch_shapes=[pltpu.SemaphoreType.DMA((n * len(ks),)),
                        pltpu.SemaphoreType.DMA((n * len(ks),)),
                        pltpu.SemaphoreType.DMA((n,)),
                        pltpu.SemaphoreType.DMA((n * len(FAR_CHIPS),)),
                        pltpu.SemaphoreType.DMA((n * len(FAR_CHIPS),))],
        compiler_params=pltpu.CompilerParams(has_side_effects=True),
    )(*arrays)


_HBM = pl.BlockSpec(memory_space=pltpu.HBM)
_SEM = pl.BlockSpec(memory_space=pltpu.SEMAPHORE)
_ANY = pl.BlockSpec(memory_space=pl.ANY)
_DATAFLOW = pltpu.SideEffectType.DATAFLOW_SIDE_EFFECTING


def _in_hbm(a):
    return pltpu.with_memory_space_constraint(a, pltpu.HBM)


def _exchange_start(arrays, gathers, after, name, ks=ALL_PEERS):
    started, token = _exchange_start_groups([(arrays, gathers, ks)], after, name)
    return started[0], token


def _exchange_start_groups(groups, after, name):
    sizes = [len(arrays) for arrays, _, _ in groups]
    n_all, n_grp = sum(sizes), len(groups)
    srcs_all = [a for arrays, _, _ in groups for a in arrays]
    lands_all = [lax.empty(s.shape, s.dtype) for arrays, gathers, _ in groups
                 for s in _landing_shapes(arrays, gathers)]
    offsets = [sum(sizes[:g]) for g in range(n_grp)]

    def body(*refs):
        sems = refs[2 * n_all + 1:2 * n_all + 1 + 3 * n_grp]
        for g, (arrays, gathers, ks) in enumerate(groups):
            part = slice(offsets[g], offsets[g] + sizes[g])
            srcs, dsts = refs[:n_all][part], refs[n_all:2 * n_all][part]
            send_sems, recv_sems, own_sems = sems[3 * g:3 * g + 3]
            for cp in _own_copies(srcs, dsts, gathers, own_sems):
                cp.start()
            for _, _, send, _ in _peer_copies(srcs, dsts, gathers, send_sems, recv_sems, ks):
                send.start()
        refs[-1][...] = jnp.zeros_like(refs[-1])

    sem_shapes = []
    for arrays, _, ks in groups:
        sem_shapes += [pltpu.SemaphoreType.DMA((len(arrays) * len(ks),)),
                       pltpu.SemaphoreType.DMA((len(arrays) * len(ks),)),
                       pltpu.SemaphoreType.DMA((len(arrays),))]
    hbm_like = [pltpu.HBM(a.shape, a.dtype) for a in srcs_all + lands_all]
    n_sem = 3 * n_grp
    res = pl.pallas_call(
        body, name=name,
        in_specs=[_HBM] * (2 * n_all) + [_ANY],
        out_specs=(*[_SEM] * n_sem, *[_HBM] * (2 * n_all), pl.BlockSpec(memory_space=pltpu.VMEM)),
        out_shape=(*sem_shapes, *hbm_like, jax.ShapeDtypeStruct((8, LANES), F32)),
        input_output_aliases={i: n_sem + i for i in range(2 * n_all)},
        compiler_params=pltpu.CompilerParams(has_side_effects=_DATAFLOW),
    )(*[_in_hbm(a) for a in srcs_all + lands_all], after)
    started = []
    for g in range(n_grp):
        lo, hi = n_sem + offsets[g], n_sem + offsets[g] + sizes[g]
        started.append((res[3 * g], res[3 * g + 1], res[3 * g + 2], res[lo:hi],
                        res[n_all + lo:n_all + hi]))
    return started, res[-1]


def _exchange_wait(started, gathers, after, name, ks=ALL_PEERS):
    send_sems, recv_sems, own_sems, arrays, lands = started
    n = len(arrays)

    def body(*refs):
        srcs, dsts = refs[:n], refs[n:2 * n]
        for _, _, send, arrival in _peer_copies(srcs, dsts, gathers, refs[2 * n], refs[2 * n + 1],
                                                ks):
            arrival.wait_recv()
            send.wait_send()
        for cp in _own_copies(srcs, dsts, gathers, refs[2 * n + 2]):
            cp.wait()

    hbm_like = [pltpu.HBM(a.shape, a.dtype) for a in list(arrays) + list(lands)]
    res = pl.pallas_call(
        body, name=name,
        in_specs=[_HBM] * (2 * n) + [_SEM, _SEM, _SEM, _ANY],
        out_specs=[_HBM] * (2 * n), out_shape=hbm_like,
        input_output_aliases={i: i for i in range(2 * n)},
        compiler_params=pltpu.CompilerParams(has_side_effects=_DATAFLOW),
    )(*arrays, *lands, send_sems, recv_sems, own_sems, after)
    return res[n:]


def _forward_start(lands, after, name):
    n = len(lands)

    def body(*refs):
        send_sems, recv_sems = refs[n + 1:n + 3]
        for _, _, send, _ in _forward_copies(refs[:n], send_sems, recv_sems):
            send.start()
        refs[-1][...] = jnp.zeros_like(refs[-1])

    n_sem = n * len(FAR_CHIPS)
    res = pl.pallas_call(
        body, name=name,
        in_specs=[_HBM] * n + [_ANY],
        out_specs=(_SEM, _SEM, *[_HBM] * n, pl.BlockSpec(memory_space=pltpu.VMEM)),
        out_shape=(pltpu.SemaphoreType.DMA((n_sem,)), pltpu.SemaphoreType.DMA((n_sem,)),
                   *[pltpu.HBM(a.shape, a.dtype) for a in lands],
                   jax.ShapeDtypeStruct((8, LANES), F32)),
        input_output_aliases={i: 2 + i for i in range(n)},
        compiler_params=pltpu.CompilerParams(has_side_effects=_DATAFLOW),
    )(*[_in_hbm(a) for a in lands], after)
    return (res[0], res[1], res[2:2 + n]), res[-1]


def _forward_wait(started, after, name):
    send_sems, recv_sems, lands = started
    n = len(lands)

    def body(*refs):
        for _, _, send, arrival in _forward_copies(refs[:n], refs[n], refs[n + 1]):
            arrival.wait_recv()
            send.wait_send()

    return pl.pallas_call(
        body, name=name,
        in_specs=[_HBM] * n + [_SEM, _SEM, _ANY],
        out_specs=[_HBM] * n, out_shape=[pltpu.HBM(a.shape, a.dtype) for a in lands],
        input_output_aliases={i: i for i in range(n)},
        compiler_params=pltpu.CompilerParams(has_side_effects=_DATAFLOW),
    )(*lands, send_sems, recv_sems, after)


def _reduce_adamw(parts, w, m, v, name):
    n_layer = len(parts)
    n, R, C = parts[0].shape
    tr = 256 if R % 256 == 0 else R
    n_t = R // tr

    def body(*refs):
        p_refs = refs[:n_layer]
        w_ref, m_ref, v_ref, g_ref, d_ref, nm_ref, nv_ref = refs[n_layer:]

        def update(p_ref):
            g = p_ref[0].astype(F32)
            for s in range(1, n):
                g = g + p_ref[s].astype(F32)
            g_ref[...] = g
            m_new = ADAM_B1 * m_ref[...] + (1.0 - ADAM_B1) * g
            v_new = ADAM_B2 * v_ref[...] + (1.0 - ADAM_B2) * (g * g)
            nm_ref[...] = m_new
            nv_ref[...] = v_new
            m_hat = m_new / (1.0 - ADAM_B1 ** ADAM_STEP)
            v_hat = v_new / (1.0 - ADAM_B2 ** ADAM_STEP)
            d_ref[...] = -ADAM_LR * (m_hat / (jnp.sqrt(v_hat) + ADAM_EPS) + ADAM_WD * w_ref[...])

        for layer, p_ref in enumerate(p_refs):
            pl.when(pl.program_id(0) == layer)(functools.partial(update, p_ref))

    def parts_spec(layer):
        def index(l, i):
            return 0, jnp.where(l < layer, 0, jnp.where(l > layer, n_t - 1, i)), 0
        return pl.BlockSpec((n, tr, C), index)

    blk = pl.BlockSpec((None, tr, C), lambda l, i: (l, i, 0))
    out = jax.ShapeDtypeStruct((n_layer, R, C), F32)
    return pl.pallas_call(
        body, name=name, grid=(n_layer, n_t),
        in_specs=[parts_spec(layer) for layer in range(n_layer)] + [blk, blk, blk],
        out_specs=[blk] * 4, out_shape=[out] * 4,
        compiler_params=_params("arbitrary", "arbitrary"),
    )(*parts, w, m, v)


def _pack(arrays):
    flat = jnp.concatenate([a.reshape(-1).astype(F32) for a in arrays])
    pad = (-flat.shape[0]) % (8 * LANES)
    return jnp.pad(flat, (0, pad)).reshape(-1, LANES)


def _unpack(buf, shapes):
    flat = buf.reshape(-1)
    out, off = [], 0
    for shp in shapes:
        size = 1
        for s in shp:
            size *= s
        out.append(flat[off:off + size].reshape(shp))
        off += size
    return out


def _block_diag_pairs(w):
    w = w.reshape(N_CBLK, 2, LRU_BLOCK_DIM, LRU_BLOCK_DIM)
    z = jnp.zeros_like(w[:, 0])
    top = jnp.concatenate([w[:, 0], z], axis=2)
    bot = jnp.concatenate([z, w[:, 1]], axis=2)
    return jnp.concatenate([top, bot], axis=1)


def _diag_pairs(m):
    h = LRU_BLOCK_DIM
    return jnp.stack([m[:, :h, :h], m[:, h:, h:]], axis=1).reshape(2 * N_CBLK, h, h)


SMALL = ("mlp_norm", "lru_conv_b", "lru_w_r", "lru_b_r", "lru_w_i", "lru_b_i",
         "lru_lambda", "fox_b_f", "fox_q_gain", "fox_k_gain")
WEIGHTS = ("mix_norm", "mlp_norm", "mlp_w1", "mlp_w2", "lru_w_in", "lru_conv_w", "lru_conv_b",
           "lru_w_r", "lru_b_r", "lru_w_i", "lru_b_i", "lru_lambda", "lru_w_out", "fox_w_in",
           "fox_b_f", "fox_q_gain", "fox_k_gain", "fox_w_out")


def kernel(x, mix_norm, mlp_norm, mlp_w1, mlp_w2, lru_w_in, lru_conv_w, lru_conv_b, lru_w_r, lru_b_r, lru_w_i, lru_b_i, lru_lambda, lru_w_out, fox_w_in, fox_b_f, fox_q_gain, fox_k_gain, fox_w_out, loss_target, m_mix_norm, m_mlp_norm, m_mlp_w1, m_mlp_w2, m_lru_w_in, m_lru_conv_w, m_lru_conv_b, m_lru_w_r, m_lru_b_r, m_lru_w_i, m_lru_b_i, m_lru_lambda, m_lru_w_out, m_fox_w_in, m_fox_b_f, m_fox_q_gain, m_fox_k_gain, m_fox_w_out, v_mix_norm, v_mlp_norm, v_mlp_w1, v_mlp_w2, v_lru_w_in, v_lru_conv_w, v_lru_conv_b, v_lru_w_r, v_lru_b_r, v_lru_w_i, v_lru_b_i, v_lru_lambda, v_lru_w_out, v_fox_w_in, v_fox_b_f, v_fox_q_gain, v_fox_k_gain, v_fox_w_out):
    w_in = dict(mix_norm=mix_norm, mlp_norm=mlp_norm, mlp_w1=mlp_w1, mlp_w2=mlp_w2,
                lru_w_in=lru_w_in, lru_conv_w=lru_conv_w, lru_conv_b=lru_conv_b, lru_w_r=lru_w_r,
                lru_b_r=lru_b_r, lru_w_i=lru_w_i, lru_b_i=lru_b_i, lru_lambda=lru_lambda,
                lru_w_out=lru_w_out, fox_w_in=fox_w_in, fox_b_f=fox_b_f, fox_q_gain=fox_q_gain,
                fox_k_gain=fox_k_gain, fox_w_out=fox_w_out)
    m_in = dict(mix_norm=m_mix_norm, mlp_norm=m_mlp_norm, mlp_w1=m_mlp_w1, mlp_w2=m_mlp_w2,
                lru_w_in=m_lru_w_in, lru_conv_w=m_lru_conv_w, lru_conv_b=m_lru_conv_b,
                lru_w_r=m_lru_w_r, lru_b_r=m_lru_b_r, lru_w_i=m_lru_w_i, lru_b_i=m_lru_b_i,
                lru_lambda=m_lru_lambda, lru_w_out=m_lru_w_out, fox_w_in=m_fox_w_in,
                fox_b_f=m_fox_b_f, fox_q_gain=m_fox_q_gain, fox_k_gain=m_fox_k_gain,
                fox_w_out=m_fox_w_out)
    v_in = dict(mix_norm=v_mix_norm, mlp_norm=v_mlp_norm, mlp_w1=v_mlp_w1, mlp_w2=v_mlp_w2,
                lru_w_in=v_lru_w_in, lru_conv_w=v_lru_conv_w, lru_conv_b=v_lru_conv_b,
                lru_w_r=v_lru_w_r, lru_b_r=v_lru_b_r, lru_w_i=v_lru_w_i, lru_b_i=v_lru_b_i,
                lru_lambda=v_lru_lambda, lru_w_out=v_lru_w_out, fox_w_in=v_fox_w_in,
                fox_b_f=v_fox_b_f, fox_q_gain=v_fox_q_gain, fox_k_gain=v_fox_k_gain,
                fox_w_out=v_fox_w_out)
    D = D_MODEL
    S = x.shape[1]
    x0, target = x[0], loss_target[0]
    me = 4 * lax.axis_index("x") + 2 * lax.axis_index("y") + lax.axis_index("c")

    def bf16(a):
        return a.astype(BF16)

    (lru_in_g,) = _exchange([bf16(lru_w_in[0])], [True], "gather_lru_in", two_level=True)
    (gather_lru, gather_mlp0, gather_fox, gather_mlp1), tok = _exchange_start_groups(
        [([bf16(lru_w_out[0]), lru_conv_w[0]], [True] * 2, ALL_PEERS),
         ([bf16(mlp_w1[0]), bf16(mlp_w2[0])], [True] * 2, NEAR_PEERS),
         ([bf16(fox_w_in[0]), bf16(fox_w_out[0])], [True] * 2, ALL_PEERS),
         ([bf16(mlp_w1[1]), bf16(mlp_w2[1])], [True] * 2, NEAR_PEERS)],
        lru_in_g, "gather_rest_start")

    def pass_on(started, after, name):
        lands = _exchange_wait(started, [True] * 2, after, name + "_wait", NEAR_PEERS)
        return _forward_start(lands, after, name + "_pass_start")
    wr =_block_diag_pairs(lru_w_r[0]).astype(BF16)
    wi = _block_diag_pairs(lru_w_i[0]).astype(BF16)
    b_r, b_i = lru_b_r.reshape(1, D), lru_b_i.reshape(1, D)
    q_gain, k_gain = jnp.tile(fox_q_gain, (1, 2)), jnp.tile(fox_k_gain, (1, 2))
    b_f = jnp.pad(fox_b_f, ((0, 0), (0, LANES - N_HEADS)))
    g_mix0, g_mix1 = mix_norm[0:1] + tok[0, 0], mix_norm[1:2]
    g_mlp0, g_mlp1 = mlp_norm[0:1], mlp_norm[1:2]

    (u0,), h0 = _norm_matmul(x0, g_mix0, [lru_in_g], "lru_in_proj")
    lru_out_g, conv_g = _exchange_wait(gather_lru, [True] * 2, u0, "gather_lru_wait")
    lru_out_w = lru_out_g.reshape(D, D)
    conv_w = conv_g.transpose(1, 0, 2).reshape(CONV_WIDTH, D)
    y_lru, hs =_lru_fwd(u0, conv_w, lru_conv_b, wr, b_r, wi, b_i, lru_lambda, "lru_core")
    pass_mlp0, tok = pass_on(gather_mlp0, y_lru, "gather_mlp0")
    x1 = _matmul_res(y_lru, lru_out_w, x0, "lru_out_proj", tok)
    w1g0, w2g0 = _forward_wait(pass_mlp0, x1, "gather_mlp0_pass_wait")
    x2, h1, r1 = _mlp_fwd(x1, g_mlp0, w1g0, w2g0, "mlp0")
    fox_in_g, fox_out_g = _exchange_wait(gather_fox, [True] * 2, x2, "gather_fox_wait")
    fox_out_w = fox_out_g.reshape(D, D)
    fox_full = jnp.concatenate([fox_in_g[d] for d in range(N_DEV)], axis=1)
    wqkv = fox_full[:, :3 * D].reshape(D, 3, D).transpose(1, 0, 2)
    wf = jnp.pad(fox_full[:, 3 * D:], ((0, 0), (0, LANES - N_HEADS)))[None]
    uq, uk, f, h2, qn, kn, vb = _fox_in_proj(x2, g_mix1, wqkv, wf, q_gain, k_gain, "fox_in_proj")
    c_col = _forget_fwd(f, b_f, "fox_forget")
    c_row = c_col[:, :N_HEADS].T.reshape(N_CBLK, 2, S)
    o, lse = _attn_fwd(qn, kn, vb, c_row, "fox_attn")
    pass_mlp1, tok = pass_on(gather_mlp1, o, "gather_mlp1")
    x3 = _matmul_res(o, fox_out_w, x2, "fox_out_proj", tok)
    w1g1, w2g1 = _forward_wait(pass_mlp1, x3, "gather_mlp1_pass_wait")
    loss_local, dx4, h3, r3 = _mlp_fwd(x3, g_mlp1, w1g1, w2g1, "mlp1", target)

    dx3, dg_mlp1, da3, dx4_b = _mlp_bwd(dx4, x3, g_mlp1, r3, w1g1, w2g1, "mlp1_bwd")
    dw1_1 = _matmul_tn(h3, da3, "mlp1_dw1", cols=2, col_blocks=N_DEV)
    dw2_1 = _matmul_tn(r3, dx4_b, "mlp1_dw2", rows=2, a_square=True, tm=1024).reshape(N_DEV, -1, D)
    grads_mlp1, tok = _exchange_start([dw1_1, dw2_1], [False] * 2, tok, "grads_mlp1_start")
    do = _matmul_nt(dx3, fox_out_w, "fox_out_bwd", BF16, tok)
    d_fox_out = _matmul_tn(o, dx3, "fox_out_dw").reshape(N_DEV, -1, D)
    dqn, dkn, dv, dc_row, rho = _attn_bwd(qn, kn, vb, do, o, lse, c_row, "fox_attn_bwd")
    duq, duk, dq_gain, dk_gain = _qk_bwd(uq, uk, dqn, dkn, q_gain, k_gain, "fox_qk_norm_bwd")
    dc_k = jnp.pad(dc_row.reshape(N_HEADS, S).T, ((0, 0), (0, LANES - N_HEADS)))
    df, db_f = _forget_bwd(dc_k, rho, f, b_f, "fox_forget_bwd")
    dx2, dg_mix1 = _proj_bwd([[duq, duk, dv], [df]], [wqkv, wf], x2, g_mix1, dx3, "fox_in_bwd")
    d_fox_in = jnp.concatenate(
        [_matmul_tn(h2, duq, "fox_in_dwq"), _matmul_tn(h2, duk, "fox_in_dwk"),
         _matmul_tn(h2, dv, "fox_in_dwv"), _matmul_tn(h2, df, "fox_in_dwf")[:, :N_HEADS]], axis=1)
    shard = (3 * D + N_HEADS) // N_DEV
    d_fox_in = jnp.stack([d_fox_in[:, d * shard:(d + 1) * shard] for d in range(N_DEV)])
    grads_fox, tok = _exchange_start([d_fox_in, d_fox_out], [False] * 2, tok, "grads_fox_start")
    dx1, dg_mlp0, da1, dx2_b = _mlp_bwd(dx2, x1, g_mlp0 + tok[0, 0], r1, w1g0, w2g0, "mlp0_bwd")
    dw1_0 = _matmul_tn(h1, da1, "mlp0_dw1", cols=2, col_blocks=N_DEV)
    dw2_0 = _matmul_tn(r1, dx2_b, "mlp0_dw2", rows=2, a_square=True, tm=1024).reshape(N_DEV, -1, D)
    grads_mlp0, tok = _exchange_start([dw1_0, dw2_0], [False] * 2, tok, "grads_mlp0_start")
    dy_lru = _matmul_nt(dx1, lru_out_w, "lru_out_bwd", F32, tok)
    d_lru_out = _matmul_tn(y_lru, dx1, "lru_out_dw").reshape(N_DEV, -1, D)
    dgp, dxb, d_conv_w, d_conv_b, d_b_r, d_b_i, d_lam, d_wr, d_wi = _lru_bwd(
        dy_lru, u0, hs, conv_w, lru_conv_b, wr, b_r, wi, b_i, lru_lambda, "lru_core_bwd")

    small_grads = dict(
        mlp_norm=jnp.concatenate([dg_mlp0, dg_mlp1], axis=0),
        lru_conv_b=d_conv_b, lru_w_r=_diag_pairs(d_wr), lru_b_r=d_b_r, lru_w_i=_diag_pairs(d_wi),
        lru_b_i=d_b_i, lru_lambda=d_lam, fox_b_f=db_f[:, :N_HEADS],
        fox_q_gain=dq_gain[:, :HEAD_DIM], fox_k_gain=dk_gain[:, :HEAD_DIM])
    small_partial = _pack([dg_mix1] + [small_grads[n] for n in SMALL] + [d_conv_w])
    grads_lru_out, tok = _exchange_start([d_lru_out, small_partial], [False, True], tok,
                                         "grads_lru_out_start")
    dx0, dg_mix0 = _proj_bwd([[dgp, dxb]], [lru_in_g], x0, mix_norm[0:1] + tok[0, 0], dx1,
                             "lru_in_bwd")
    d_lru_in = jnp.concatenate([_matmul_tn(h0, dgp, "lru_in_dw_gate", col_blocks=4),
                                _matmul_tn(h0, dxb, "lru_in_dw_x", col_blocks=4)], axis=0)
    grads_lru_in, tok = _exchange_start([d_lru_in, dg_mix0], [False, True], tok,
                                        "grads_lru_in_start")

    grads, deltas, new_m, new_v = {}, {}, {}, {}

    def update(name, parts):
        w, m, v = w_in[name], m_in[name], v_in[name]
        shape = w.shape
        stacked = (len(parts), -1, shape[-1])
        w3 = w.reshape(stacked)
        res = _reduce_adamw([p.reshape((N_DEV,) + w3.shape[1:]) for p in parts], w3,
                            m.reshape(stacked), v.reshape(stacked), "adamw_" + name)
        return [r.reshape(shape) for r in res]

    def store(name, res):
        grads[name], deltas[name], new_m[name], new_v[name] = res

    p_w1_1, p_w2_1 = _exchange_wait(grads_mlp1, [False] * 2, tok, "grads_mlp1_wait")
    p_fox_in, p_fox_out = _exchange_wait(grads_fox, [False] * 2, p_w1_1, "grads_fox_wait")
    store("fox_w_in", update("fox_w_in", [p_fox_in]))
    store("fox_w_out", update("fox_w_out", [p_fox_out]))
    p_w1_0, p_w2_0 = _exchange_wait(grads_mlp0, [False] * 2, grads["fox_w_out"], "grads_mlp0_wait")
    store("mlp_w1", update("mlp_w1", [p_w1_0, p_w1_1]))
    store("mlp_w2", update("mlp_w2", [p_w2_0, p_w2_1]))
    p_lru_out, p_small = _exchange_wait(grads_lru_out, [False, True], grads["mlp_w2"],
                                        "grads_lru_out_wait")
    store("lru_w_out", update("lru_w_out", [p_lru_out]))
    p_lru_in, p_mix0 = _exchange_wait(grads_lru_in, [False, True], grads["lru_w_out"],
                                      "grads_lru_in_wait")
    store("lru_w_in", update("lru_w_in", [p_lru_in]))

    mix0 = [r[0] for r in _reduce_adamw([p_mix0], mix_norm[None, 0:1], m_mix_norm[None, 0:1],
                                        v_mix_norm[None, 0:1], "adamw_mix0")]
    packed = lambda src, first: _pack([first] + [src[n] for n in SMALL]
                                      + [jnp.zeros((CONV_WIDTH, D))])[None]
    small_shapes = [(1, D)] + [w_in[n].shape for n in SMALL]
    n_small = sum(math.prod(s) for s in small_shapes)
    res_small = _reduce_adamw([p_small], packed(w_in, mix_norm[1:2]), packed(m_in, m_mix_norm[1:2]),
                              packed(v_in, v_mix_norm[1:2]), "adamw_small")
    for name, *vals in zip(("mix1",) + SMALL, *[_unpack(r, small_shapes) for r in res_small]):
        if name == "mix1":
            vals = [jnp.concatenate([r0, r1], axis=0) for r0, r1 in zip(mix0, vals)]
            name = "mix_norm"
        store(name, vals)
    conv_parts = p_small.reshape(N_DEV, -1)[:, n_small:n_small + CONV_WIDTH * D]
    conv_parts = conv_parts.reshape(N_DEV, CONV_WIDTH, N_DEV, LANES)
    conv_parts = lax.dynamic_index_in_dim(conv_parts, me, axis=2, keepdims=False)
    store("lru_conv_w", update("lru_conv_w", [conv_parts]))

    loss = lax.psum(loss_local[0, 0], ("x", "y", "c"))
    return (loss, dx0[None], *[grads[n] for n in WEIGHTS], *[deltas[n] for n in WEIGHTS],
            *[new_m[n] for n in WEIGHTS], *[new_v[n] for n in WEIGHTS])
```

```python
import functools
import math

import jax
import jax.numpy as jnp
from jax import lax
from jax.experimental import pallas as pl
from jax.experimental.pallas import tpu as pltpu

F32 = jnp.float32
BF16 = jnp.bfloat16

N_DEV = 8
D_MODEL = 1024
D_FF = 4096
N_HEADS = 16
HEAD_DIM = 64
LRU_BLOCK_DIM = 64
CONV_WIDTH = 4
LRU_C = 8.0
EPS = 1e-6
NEG_INF = -1e30
ATTN_SCALE = HEAD_DIM ** -0.5
LANES = 128
N_CBLK = D_MODEL // LANES
VMEM_LIMIT = 52 * 2 ** 20

ADAM_LR = 0.001
ADAM_B1 = 0.9
ADAM_B2 = 0.999
ADAM_EPS = 1e-08
ADAM_WD = 0.01
ADAM_STEP = 10

_NT = (((1,), (1,)), ((), ()))
_TN = (((0,), (0,)), ((), ()))


def _params(*sem):
    return pltpu.CompilerParams(dimension_semantics=sem, vmem_limit_bytes=VMEM_LIMIT)


def _resident(shape):
    zeros = (0,) * len(shape)
    return pl.BlockSpec(shape, lambda *_: zeros, pipeline_mode=pl.Buffered(1))


def _dot(a, b):
    return jnp.dot(a, b, preferred_element_type=F32)


def _dot_nt(a, b):
    return lax.dot_general(a, b, _NT, preferred_element_type=F32)


def _dot_tn(a, b):
    return lax.dot_general(a, b, _TN, preferred_element_type=F32)


def _sigmoid(x):
    return 1.0 / (1.0 + jnp.exp(-x))


def _log_sigmoid(x):
    return -(jnp.maximum(-x, 0.0) + jnp.log1p(jnp.exp(-jnp.abs(x))))


_GELU_K = 0.7978845608028654


def _gelu(x):
    return 0.5 * x * (1.0 + jnp.tanh(_GELU_K * (x + 0.044715 * (x * x * x))))


def _gelu_grad(x):
    t = jnp.tanh(_GELU_K * (x + 0.044715 * (x * x * x)))
    return 0.5 * (1.0 + t) + 0.5 * x * (1.0 - t * t) * (_GELU_K * (1.0 + 3 * 0.044715 * x * x))


def _rms_scale(x):
    return lax.rsqrt(jnp.mean(x * x, axis=-1, keepdims=True) + EPS)


def _norm_bwd(dh, x, g):
    rs = _rms_scale(x)
    xhat = x * rs
    dxhat = dh * g
    dx = rs * (dxhat - xhat * jnp.mean(dxhat * xhat, axis=-1, keepdims=True))
    return dx, jnp.sum(dh * xhat, axis=0, keepdims=True)


def _token_tile(S, want):
    tm = min(S, want)
    assert S % tm == 0
    return tm


def _norm_matmul(x, g, ws, name, tm=512):
    S, D = x.shape
    tm = _token_tile(S, tm)
    n = len(ws)

    def body(x_ref, g_ref, *refs):
        w_refs, o_refs, h_ref = refs[:n], refs[n:2 * n], refs[2 * n]
        xv = x_ref[...]
        h = (xv * _rms_scale(xv) * g_ref[...]).astype(BF16)
        h_ref[...] = h
        for w_ref, o_ref in zip(w_refs, o_refs):
            nb, _, nw = w_ref.shape
            for d in range(nb):
                o_ref[:, d * nw:(d + 1) * nw] = _dot(h, w_ref[d])

    widths = [w.shape[0] * w.shape[2] for w in ws]
    outs = pl.pallas_call(
        body, name=name, grid=(S // tm,),
        in_specs=[pl.BlockSpec((tm, D), lambda i: (i, 0)), _resident((1, D))]
        + [_resident(w.shape) for w in ws],
        out_specs=[pl.BlockSpec((tm, n_), lambda i: (i, 0)) for n_ in widths]
        + [pl.BlockSpec((tm, D), lambda i: (i, 0))],
        out_shape=[jax.ShapeDtypeStruct((S, n_), F32) for n_ in widths]
        + [jax.ShapeDtypeStruct((S, D), BF16)],
        compiler_params=_params("parallel"),
    )(x, g, *ws)
    return outs[:n], outs[n]


def _matmul_res(a, w, res, name, after, tm=512):
    S, K = a.shape
    N = w.shape[1]
    tm = _token_tile(S, tm)

    def body(a_ref, w_ref, r_ref, after_ref, o_ref):
        o_ref[...] = r_ref[...] + _dot(a_ref[...], w_ref[...])

    return pl.pallas_call(
        body, name=name, grid=(S // tm,),
        in_specs=[pl.BlockSpec((tm, K), lambda i: (i, 0)), _resident((K, N)),
                  pl.BlockSpec((tm, N), lambda i: (i, 0)), pl.BlockSpec(memory_space=pl.ANY)],
        out_specs=pl.BlockSpec((tm, N), lambda i: (i, 0)),
        out_shape=jax.ShapeDtypeStruct((S, N), F32),
        compiler_params=_params("parallel"),
    )(a, w, res, after)


def _matmul_nt(a, w, name, out_dtype, after, tm=1024):
    S, N = a.shape
    K = w.shape[0]
    tm = _token_tile(S, tm)

    def body(a_ref, w_ref, after_ref, o_ref):
        o_ref[...] = _dot_nt(a_ref[...].astype(BF16), w_ref[...]).astype(out_dtype)

    return pl.pallas_call(
        body, name=name, grid=(S // tm,),
        in_specs=[pl.BlockSpec((tm, N), lambda i: (i, 0)), _resident((K, N)),
                  pl.BlockSpec(memory_space=pl.ANY)],
        out_specs=pl.BlockSpec((tm, K), lambda i: (i, 0)),
        out_shape=jax.ShapeDtypeStruct((S, K), out_dtype),
        compiler_params=_params("parallel"),
    )(a, w, after)


def _proj_bwd(a_lists, w_list, x, g, res, name, tm=512):
    S, D = x.shape
    tm = _token_tile(S, tm)
    a_list = [a for group in a_lists for a in group]
    n, n_w = len(a_list), len(w_list)

    def body(*refs):
        a_refs, w_refs = list(refs[:n]), refs[n:n + n_w]
        x_ref, g_ref, r_ref, dx_ref, dg_ref = refs[n + n_w:]
        dh = jnp.zeros((tm, D), F32)
        for group, w_ref in zip(a_lists, w_refs):
            nw = w_ref.shape[2]
            d = 0
            for _ in group:
                a_ref = a_refs.pop(0)
                for j in range(a_ref.shape[1] // nw):
                    dh = dh + _dot_nt(a_ref[:, j * nw:(j + 1) * nw].astype(BF16), w_ref[d])
                    d += 1
        dx, dg = _norm_bwd(dh, x_ref[...], g_ref[...])
        dx_ref[...] = r_ref[...] + dx

        @pl.when(pl.program_id(0) == 0)
        def _():
            dg_ref[...] = jnp.zeros_like(dg_ref)
        dg_ref[...] += dg

    tok = lambda width: pl.BlockSpec((tm, width), lambda i: (i, 0))
    return pl.pallas_call(
        body, name=name, grid=(S // tm,),
        in_specs=[tok(a.shape[1]) for a in a_list] + [_resident(w.shape) for w in w_list]
        + [tok(D), _resident((1, D)), tok(D)],
        out_specs=[tok(D), pl.BlockSpec((1, D), lambda i: (0, 0))],
        out_shape=[jax.ShapeDtypeStruct((S, D), F32), jax.ShapeDtypeStruct((1, D), F32)],
        compiler_params=_params("arbitrary"),
    )(*a_list, *w_list, x, g, res)


def _matmul_tn(a, b, name, rows=1, cols=1, col_blocks=None, a_square=False, tm=2048):
    S, K = a.shape
    N = b.shape[1]
    tm = _token_tile(S, tm)
    n_tok = S // tm
    kr, nc = K // rows, N // cols

    def body(a_ref, b_ref, o_ref, acc_ref):
        av = a_ref[...]
        if a_square:
            av = av.astype(F32)
            av = av * av
        part = _dot_tn(av.astype(BF16), b_ref[...].astype(BF16))
        step = pl.program_id(2)

        @pl.when(step == 0)
        def _():
            acc_ref[...] = part

        @pl.when(step > 0)
        def _():
            acc_ref[...] += part

        @pl.when(step == n_tok - 1)
        def _():
            if col_blocks is None:
                o_ref[...] = acc_ref[...].astype(BF16)
            else:
                nw = N // col_blocks
                for d in range(col_blocks // cols):
                    o_ref[d] = acc_ref[:, d * nw:(d + 1) * nw].astype(BF16)

    if col_blocks is None:
        out_spec = pl.BlockSpec((kr, nc), lambda r, c, i: (r, c))
        out_shape = jax.ShapeDtypeStruct((K, N), BF16)
    else:
        assert rows == 1 and col_blocks % cols == 0
        per = col_blocks // cols
        out_spec = pl.BlockSpec((per, K, N // col_blocks), lambda r, c, i: (c, 0, 0))
        out_shape = jax.ShapeDtypeStruct((col_blocks, K, N // col_blocks), BF16)
    return pl.pallas_call(
        body, name=name, grid=(rows, cols, n_tok),
        in_specs=[pl.BlockSpec((tm, kr), lambda r, c, i: (i, r)),
                  pl.BlockSpec((tm, nc), lambda r, c, i: (i, c))],
        out_specs=out_spec, out_shape=out_shape,
        scratch_shapes=[pltpu.VMEM((kr, nc), F32)],
        compiler_params=_params("parallel", "parallel", "arbitrary"),
    )(a, b)


def _mlp_fwd(x, g, w1, w2, name, target=None, tm=512):
    S, D = x.shape
    nb, _, fb = w1.shape
    tm = _token_tile(S, tm)
    with_loss = target is not None

    def body(x_ref, g_ref, w1_ref, w2_ref, *refs):
        h_ref, r_ref = refs[-2:]
        xv = x_ref[...]
        h = (xv * _rms_scale(xv) * g_ref[...]).astype(BF16)
        h_ref[...] = h
        acc = xv
        for d in range(nb):
            r = jnp.maximum(_dot(h, w1_ref[d]), 0.0)
            r_ref[:, d * fb:(d + 1) * fb] = r.astype(BF16)
            acc = acc + _dot((r * r).astype(BF16), w2_ref[d])
        if not with_loss:
            refs[0][...] = acc
            return
        t_ref, loss_ref, dy_ref = refs[:3]
        err = acc - t_ref[...]
        dy_ref[...] = err / D

        @pl.when(pl.program_id(0) == 0)
        def _():
            loss_ref[...] = jnp.zeros_like(loss_ref)
        row_loss = jnp.mean(err * err, axis=1, keepdims=True)
        loss_ref[...] += 0.5 * jnp.sum(row_loss, axis=0, keepdims=True)

    tok = lambda width: pl.BlockSpec((tm, width), lambda i: (i, 0))
    saved_specs = [tok(D), tok(nb * fb)]
    saved_shapes = [jax.ShapeDtypeStruct((S, D), BF16), jax.ShapeDtypeStruct((S, nb * fb), BF16)]
    wide = jax.ShapeDtypeStruct((S, D), F32)
    if with_loss:
        head_specs = [pl.BlockSpec((1, 1), lambda i: (0, 0)), tok(D)]
        head_shapes = [jax.ShapeDtypeStruct((1, 1), F32), wide]
    else:
        head_specs, head_shapes = [tok(D)], [wide]
    return pl.pallas_call(
        body, name=name, grid=(S // tm,),
        in_specs=[tok(D), _resident((1, D)), _resident(w1.shape), _resident(w2.shape)]
        + ([tok(D)] if with_loss else []),
        out_specs=head_specs + saved_specs, out_shape=head_shapes + saved_shapes,
        compiler_params=_params("arbitrary" if with_loss else "parallel"),
    )(x, g, w1, w2, *([target] if with_loss else []))


def _mlp_bwd(dout, x, g, r, w1, w2, name, tm=512):
    S, D = x.shape
    nb, _, fb = w1.shape
    tm = _token_tile(S, tm)

    def body(do_ref, x_ref, g_ref, r_ref, w1_ref, w2_ref, dx_ref, dg_ref, da_ref, dob_ref):
        dov = do_ref[...]
        dob = dov.astype(BF16)
        dob_ref[...] = dob
        dh = jnp.zeros((tm, D), F32)
        for d in range(nb):
            dz = _dot_nt(dob, w2_ref[d])
            da = (dz * (2.0 * r_ref[:, d * fb:(d + 1) * fb].astype(F32))).astype(BF16)
            da_ref[:, d * fb:(d + 1) * fb] = da
            dh = dh + _dot_nt(da, w1_ref[d])
        dx, dg = _norm_bwd(dh, x_ref[...], g_ref[...])
        dx_ref[...] = dov + dx

        @pl.when(pl.program_id(0) == 0)
        def _():
            dg_ref[...] = jnp.zeros_like(dg_ref)
        dg_ref[...] += dg

    tok = lambda width: pl.BlockSpec((tm, width), lambda i: (i, 0))
    return pl.pallas_call(
        body, name=name, grid=(S // tm,),
        in_specs=[tok(D), tok(D), _resident((1, D)), tok(nb * fb), _resident(w1.shape),
                  _resident(w2.shape)],
        out_specs=[tok(D), pl.BlockSpec((1, D), lambda i: (0, 0)), tok(nb * fb), tok(D)],
        out_shape=[jax.ShapeDtypeStruct((S, D), F32), jax.ShapeDtypeStruct((1, D), F32),
                   jax.ShapeDtypeStruct((S, nb * fb), BF16), jax.ShapeDtypeStruct((S, D), BF16)],
        compiler_params=_params("arbitrary"),
    )(dout, x, g, r, w1, w2)


def _scan_chunk(a, b, row, T, reverse):
    s = 1
    while s < T:
        if reverse:
            keep, shift = row < T - s, T - s
        else:
            keep, shift = row >= s, s
        a_sh = jnp.where(keep, pltpu.roll(a, shift, 0), 1.0)
        b_sh = jnp.where(keep, pltpu.roll(b, shift, 0), 0.0)
        b = a * b_sh + b
        a = a * a_sh
        s *= 2
    return a, b


def _row_of(x, row, r):
    return jnp.sum(jnp.where(row == r, x, 0.0), axis=0, keepdims=True)


def _shift_down(x, prev, row, k):
    if k == 0:
        return x
    return jnp.where(row < k, pltpu.roll(prev, k, 0), pltpu.roll(x, k, 0))


def _shift_up(x, nxt, row, k, T):
    if k == 0:
        return x
    return jnp.where(row < T - k, pltpu.roll(x, T - k, 0), pltpu.roll(nxt, T - k, 0))


def _lru_gates(xb, prev_xb, row, cw_ref, cb, wr, br, wi, bi, ls):
    xc = cb + cw_ref[pl.ds(0, 1), :] * _shift_down(xb, prev_xb, row, 3)
    for k in (2, 1, 0):
        xc = xc + cw_ref[pl.ds(3 - k, 1), :] * _shift_down(xb, prev_xb, row, k)
    xcb = xc.astype(BF16)
    r = _sigmoid(_dot(xcb, wr) + br)
    i = _sigmoid(_dot(xcb, wi) + bi)
    la = (LRU_C * r) * ls
    a = jnp.exp(la)
    t = jnp.tanh(la)
    m = jnp.sqrt(-2.0 * t / (1.0 - t))
    return xc, xcb, r, i, a, m


def _lru_specs(S):
    col = lambda off: pl.BlockSpec((S, LANES), lambda j: (0, j + off))
    vec = pl.BlockSpec((1, LANES), lambda j: (0, j))
    mat = pl.BlockSpec((None, LANES, LANES), lambda j: (j, 0, 0))
    cwm = pl.BlockSpec((CONV_WIDTH, LANES), lambda j: (0, j))
    return col, vec, mat, cwm


def _lru_fwd(u, conv_w, conv_b, wr, br, wi, bi, lam, name):
    S = u.shape[0]
    T = _token_tile(S, 512)
    col, vec, mat, cwm = _lru_specs(S)

    def body(gp_ref, xb_ref, cw_ref, cb_ref, wr_ref, br_ref, wi_ref, bi_ref, lam_ref,
             y_ref, hs_ref):
        row = lax.broadcasted_iota(jnp.int32, (T, LANES), 0)
        ls = _log_sigmoid(lam_ref[...])
        cb, br, bi = cb_ref[...], br_ref[...], bi_ref[...]
        wr, wi = wr_ref[...], wi_ref[...]

        def chunk(ci, carry):
            prev_xb, hc = carry
            rows = pl.ds(pl.multiple_of(ci * T, T), T)
            xb = xb_ref[rows, :]
            xc, _, _, i, a, m = _lru_gates(xb, prev_xb, row, cw_ref, cb, wr, br, wi, bi, ls)
            ca, cbv = _scan_chunk(a, m * (i * xc), row, T, reverse=False)
            h = ca * hc + cbv
            hs_ref[rows, :] = h
            y_ref[rows, :] = (_gelu(gp_ref[rows, :]) * h).astype(BF16)
            return xb, _row_of(h, row, T - 1)

        lax.fori_loop(0, S // T, chunk,
                      (jnp.zeros((T, LANES), F32), jnp.zeros((1, LANES), F32)))

    return pl.pallas_call(
        body, name=name, grid=(N_CBLK,),
        in_specs=[col(0), col(N_CBLK), cwm, vec, mat, vec, mat, vec, vec],
        out_specs=[col(0), col(0)],
        out_shape=[jax.ShapeDtypeStruct((S, D_MODEL), BF16), jax.ShapeDtypeStruct((S, D_MODEL), F32)],
        compiler_params=_params("parallel"),
    )(u, u, conv_w, conv_b, wr, br, wi, bi, lam)


def _lru_bwd(dy, u, hs, conv_w, conv_b, wr, br, wi, bi, lam, name):
    S = u.shape[0]
    T = _token_tile(S, 512)
    n_chunk = S // T
    col, vec, mat, cwm = _lru_specs(S)

    def body(dy_ref, gp_ref, xb_ref, hs_ref, cw_ref, cb_ref, wr_ref, br_ref, wi_ref, bi_ref,
             lam_ref, dgp_ref, dxb_ref, dcw_ref, dcb_ref, dbr_ref, dbi_ref, dlam_ref, dwr_ref,
             dwi_ref):
        row = lax.broadcasted_iota(jnp.int32, (T, LANES), 0)
        lam = lam_ref[...]
        ls = _log_sigmoid(lam)
        cb, br, bi = cb_ref[...], br_ref[...], bi_ref[...]
        wr, wi = wr_ref[...], wi_ref[...]
        for ref in (dcw_ref, dcb_ref, dbr_ref, dbi_ref, dlam_ref, dwr_ref, dwi_ref):
            ref[...] = jnp.zeros_like(ref)

        def chunk(it, carry):
            g_next, dxc_next = carry
            ci = n_chunk - 1 - it
            rows = pl.ds(pl.multiple_of(ci * T, T), T)
            before = pl.ds(pl.multiple_of(jnp.maximum(ci - 1, 0) * T, T), T)
            first = ci == 0
            xb = xb_ref[rows, :]
            prev_xb = jnp.where(first, 0.0, xb_ref[before, :])
            xc, xcb, r, i, a, m = _lru_gates(xb, prev_xb, row, cw_ref, cb, wr, br, wi, bi, ls)
            h = hs_ref[rows, :]
            h_prev = _shift_down(h, jnp.where(first, 0.0, hs_ref[before, :]), row, 1)
            gp = gp_ref[rows, :]
            dyv = dy_ref[rows, :]
            dgp_ref[rows, :] = (dyv * h * _gelu_grad(gp)).astype(BF16)
            dh = dyv * _gelu(gp)
            ca, cbv = _scan_chunk(a, a * dh, row, T, reverse=True)
            gp_acc = ca * g_next + cbv
            g = dh + jnp.where(row < T - 1, pltpu.roll(gp_acc, T - 1, 0), g_next)
            da = g * h_prev - (g * (i * xc)) * a / m
            dla = da * a
            dlam_ref[...] += jnp.sum(dla * (LRU_C * r), axis=0, keepdims=True)
            dpr = (dla * (LRU_C * ls)) * r * (1.0 - r)
            dpi = (g * m * xc) * i * (1.0 - i)
            dbr_ref[...] += jnp.sum(dpr, axis=0, keepdims=True)
            dbi_ref[...] += jnp.sum(dpi, axis=0, keepdims=True)
            dprb, dpib = dpr.astype(BF16), dpi.astype(BF16)
            dwr_ref[...] += _dot_tn(xcb, dprb)
            dwi_ref[...] += _dot_tn(xcb, dpib)
            dxc = g * m * i + _dot_nt(dprb, wr) + _dot_nt(dpib, wi)
            dcb_ref[...] += jnp.sum(dxc, axis=0, keepdims=True)
            dxb = jnp.zeros((T, LANES), F32)
            for k in range(CONV_WIDTH):
                tap = pl.ds(CONV_WIDTH - 1 - k, 1)
                dcw_ref[tap, :] += jnp.sum(dxc * _shift_down(xb, prev_xb, row, k), axis=0,
                                           keepdims=True)
                dxb = dxb + cw_ref[tap, :] * _shift_up(dxc, dxc_next, row, k, T)
            dxb_ref[rows, :] = dxb.astype(BF16)
            return _row_of(gp_acc, row, 0), dxc

        lax.fori_loop(0, n_chunk, chunk,
                      (jnp.zeros((1, LANES), F32), jnp.zeros((T, LANES), F32)))
        dlam_ref[...] = dlam_ref[...] * _sigmoid(-lam)

    vec_out = jax.ShapeDtypeStruct((1, D_MODEL), F32)
    mat_out = jax.ShapeDtypeStruct((N_CBLK, LANES, LANES), F32)
    return pl.pallas_call(
        body, name=name, grid=(N_CBLK,),
        in_specs=[col(0), col(0), col(N_CBLK), col(0), cwm, vec, mat, vec, mat, vec, vec],
        out_specs=[col(0), col(0), cwm, vec, vec, vec, vec, mat, mat],
        out_shape=[jax.ShapeDtypeStruct((S, D_MODEL), BF16), jax.ShapeDtypeStruct((S, D_MODEL), BF16),
                   jax.ShapeDtypeStruct((CONV_WIDTH, D_MODEL), F32),
                   vec_out, vec_out, vec_out, vec_out, mat_out, mat_out],
        compiler_params=_params("parallel"),
    )(dy, u, u, hs, conv_w, conv_b, wr, br, wi, bi, lam)


def _head_group_matrix(value):
    r = lax.broadcasted_iota(jnp.int32, (LANES, LANES), 0) // HEAD_DIM
    c = lax.broadcasted_iota(jnp.int32, (LANES, LANES), 1) // HEAD_DIM
    return jnp.where(r == c, value, 0.0).astype(BF16)


def _group_dot(x, p):
    hi = x.astype(BF16)
    lo = (x - hi.astype(F32)).astype(BF16)
    return _dot(hi, p) + _dot(lo, p)


def _head_mean(x, p):
    return _group_dot(x, p)


def _fox_in_proj(x, g, wqkv, wf, q_gain, k_gain, name, tm=512):
    S, D = x.shape
    tm = _token_tile(S, tm)

    def body(x_ref, g_ref, w_ref, wf_ref, qg_ref, kg_ref,
             uq_ref, uk_ref, f_ref, h_ref, qn_ref, kn_ref, vb_ref):
        xv = x_ref[...]
        h = (xv * _rms_scale(xv) * g_ref[...]).astype(BF16)
        h_ref[...] = h
        p = _head_group_matrix(1.0 / HEAD_DIM)
        for which, u_ref, gain_ref, n_ref, scale in ((0, uq_ref, qg_ref, qn_ref, ATTN_SCALE),
                                                     (1, uk_ref, kg_ref, kn_ref, 1.0)):
            u = _dot(h, w_ref[which])
            u_ref[...] = u
            for j in range(N_CBLK):
                cl = slice(j * LANES, (j + 1) * LANES)
                uv = u[:, cl]
                rs = lax.rsqrt(_head_mean(uv * uv, p) + EPS)
                n_ref[:, cl] = (uv * rs * gain_ref[...]).astype(BF16) * scale
        vb_ref[...] = _dot(h, w_ref[2]).astype(BF16)
        f_ref[...] = _dot(h, wf_ref[0])

    tok = lambda width: pl.BlockSpec((tm, width), lambda i: (i, 0))
    wide = jax.ShapeDtypeStruct((S, D), F32)
    half = jax.ShapeDtypeStruct((S, D), BF16)
    return pl.pallas_call(
        body, name=name, grid=(S // tm,),
        in_specs=[tok(D), _resident((1, D)), _resident(wqkv.shape), _resident(wf.shape),
                  _resident((1, LANES)), _resident((1, LANES))],
        out_specs=[tok(D), tok(D), tok(LANES), tok(D), tok(D), tok(D), tok(D)],
        out_shape=[wide, wide, jax.ShapeDtypeStruct((S, LANES), F32), half, half, half, half],
        compiler_params=_params("parallel"),
    )(x, g, wqkv, wf, q_gain, k_gain)


def _qk_bwd(uq, uk, dqn, dkn, q_gain, k_gain, name, tm=512):
    S = uq.shape[0]
    tm = _token_tile(S, tm)

    def body(q_ref, k_ref, dqn_ref, dkn_ref, qg_ref, kg_ref, dq_ref, dk_ref, dqg_ref, dkg_ref):
        p = _head_group_matrix(1.0 / HEAD_DIM)
        for x_ref, dn_ref, g_ref, dx_ref, dg_ref, scale in (
                (q_ref, dqn_ref, qg_ref, dq_ref, dqg_ref, ATTN_SCALE),
                (k_ref, dkn_ref, kg_ref, dk_ref, dkg_ref, 1.0)):
            dg = jnp.zeros((1, LANES), F32)
            for j in range(N_CBLK):
                cl = slice(j * LANES, (j + 1) * LANES)
                xv, dn = x_ref[:, cl], dn_ref[:, cl] * scale
                rs = lax.rsqrt(_head_mean(xv * xv, p) + EPS)
                xhat = xv * rs
                dxhat = dn * g_ref[...]
                dx_ref[:, cl] = (rs * (dxhat - xhat * _head_mean(dxhat * xhat, p))).astype(BF16)
                dg = dg + jnp.sum(dn * xhat, axis=0, keepdims=True)

            @pl.when(pl.program_id(0) == 0)
            def _():
                dg_ref[...] = jnp.zeros_like(dg_ref)
            dg_ref[...] += dg

            @pl.when(pl.program_id(0) == S // tm - 1)
            def _():
                dg_ref[...] += pltpu.roll(dg_ref[...], HEAD_DIM, 1)

    blk = lambda off: pl.BlockSpec((tm, D_MODEL), lambda i: (i, off))
    acc = pl.BlockSpec((1, LANES), lambda i: (0, 0))
    out = jax.ShapeDtypeStruct((S, D_MODEL), BF16)
    vec = jax.ShapeDtypeStruct((1, LANES), F32)
    return pl.pallas_call(
        body, name=name, grid=(S // tm,),
        in_specs=[blk(0), blk(0), blk(0), blk(0), _resident((1, LANES)), _resident((1, LANES))],
        out_specs=[blk(0), blk(0), acc, acc],
        out_shape=[out, out, vec, vec],
        compiler_params=_params("arbitrary"),
    )(uq, uk, dqn, dkn, q_gain, k_gain)


def _forget_fwd(f, b_f, name):
    S = f.shape[0]
    T = _token_tile(S, 256)

    def body(f_ref, b_ref, c_ref):
        row = lax.broadcasted_iota(jnp.int32, (T, LANES), 0)
        ones = jnp.ones((T, LANES), F32)
        bias = b_ref[...]

        def chunk(ci, carry):
            rows = pl.ds(pl.multiple_of(ci * T, T), T)
            _, c = _scan_chunk(ones, _log_sigmoid(f_ref[rows, :] + bias), row, T, reverse=False)
            c = c + carry
            c_ref[rows, :] = c
            return _row_of(c, row, T - 1)

        lax.fori_loop(0, S // T, chunk, jnp.zeros((1, LANES), F32))

    return pl.pallas_call(
        body, name=name,
        in_specs=[pl.BlockSpec(memory_space=pltpu.VMEM)] * 2,
        out_specs=pl.BlockSpec(memory_space=pltpu.VMEM),
        out_shape=jax.ShapeDtypeStruct((S, LANES), F32),
        compiler_params=pltpu.CompilerParams(vmem_limit_bytes=VMEM_LIMIT),
    )(f, b_f)


def _forget_bwd(dc_k, rho, f, b_f, name):
    S = f.shape[0]
    T = _token_tile(S, 256)
    n_chunk = S // T

    def body(dck_ref, rho_ref, f_ref, b_ref, df_ref, db_ref):
        row = lax.broadcasted_iota(jnp.int32, (T, LANES), 0)
        ones = jnp.ones((T, LANES), F32)
        bias = b_ref[...]
        pick = (lax.broadcasted_iota(jnp.int32, (D_MODEL, LANES), 0)
                == HEAD_DIM * lax.broadcasted_iota(jnp.int32, (D_MODEL, LANES), 1))
        pick = jnp.where(pick, 1.0, 0.0).astype(BF16)

        def chunk(it, carry):
            tail, db = carry
            rows = pl.ds(pl.multiple_of((n_chunk - 1 - it) * T, T), T)
            dc = dck_ref[rows, :] + _group_dot(rho_ref[rows, :], pick)
            _, dlf = _scan_chunk(ones, dc, row, T, reverse=True)
            dlf = dlf + tail
            df = dlf * _sigmoid(-(f_ref[rows, :] + bias))
            df_ref[rows, :] = df
            return _row_of(dlf, row, 0), db + jnp.sum(df, axis=0, keepdims=True)

        zero = jnp.zeros((1, LANES), F32)
        _, db = lax.fori_loop(0, n_chunk, chunk, (zero, zero))
        db_ref[...] = db

    return pl.pallas_call(
        body, name=name,
        in_specs=[pl.BlockSpec(memory_space=pltpu.VMEM)] * 4,
        out_specs=[pl.BlockSpec(memory_space=pltpu.VMEM)] * 2,
        out_shape=[jax.ShapeDtypeStruct((S, LANES), F32), jax.ShapeDtypeStruct((1, LANES), F32)],
        compiler_params=pltpu.CompilerParams(vmem_limit_bytes=VMEM_LIMIT),
    )(dc_k, rho, f, b_f)


ATTN_TILE = 512
ATTN_ROWS_FWD = 32


def _attn_tiles(S):
    t = _token_tile(S, ATTN_TILE)
    return t, S // t


def _causal(T):
    return (lax.broadcasted_iota(jnp.int32, (T, T), 1)
            <= lax.broadcasted_iota(jnp.int32, (T, T), 0))


def _attn_fwd(qs_, kn, vb, c_row, name):
    S = qs_.shape[0]
    T, n_t = _attn_tiles(S)
    RB = min(T, ATTN_ROWS_FWD)

    def body(q_ref, k_ref, v_ref, cr_ref, o_ref, lse_ref, sa_ref, sb_ref, p_ref, m_ref, l_ref,
             acc_ref, a_ref):
        qi = pl.program_id(1)
        lanes = [slice(h2 * HEAD_DIM, (h2 + 1) * HEAD_DIM) for h2 in range(2)]
        col = lax.broadcasted_iota(jnp.int32, (RB, T), 1)
        row = lax.broadcasted_iota(jnp.int32, (RB, T), 0)
        m_ref[...] = jnp.full(m_ref.shape, NEG_INF, F32)
        l_ref[...] = jnp.zeros_like(l_ref)
        acc_ref[...] = jnp.zeros_like(acc_ref)

        def logits_into(s_ref, kj):
            ks = pl.ds(pl.multiple_of(kj * T, T), T)
            for h2, hl in enumerate(lanes):
                s_ref[h2] = _dot_nt(q_ref[:, hl], k_ref[ks, hl]) - cr_ref[h2:h2 + 1, ks]

        def consume(s_ref, kj, masked):
            ks = pl.ds(pl.multiple_of(kj * T, T), T)
            for h2, hl in enumerate(lanes):
                blocks = [slice(i * RB, (i + 1) * RB) for i in range(T // RB)]

                def logits(i, rows):
                    s = s_ref[h2, rows, :]
                    return jnp.where(col <= row + i * RB, s, NEG_INF) if masked else s

                wide = lambda x: jnp.broadcast_to(x, (RB, LANES))
                for i, rows in enumerate(blocks):
                    mx = wide(jnp.max(logits(i, rows), axis=1, keepdims=True))
                    a_ref[h2, rows, :] = m_ref[h2, rows, :]
                    m_ref[h2, rows, :] = jnp.maximum(m_ref[h2, rows, :], mx)
                for i, rows in enumerate(blocks):
                    m_new = m_ref[h2, rows, :]
                    p = jnp.exp(logits(i, rows) - jnp.tile(m_new, (1, T // LANES)))
                    alpha = jnp.exp(a_ref[h2, rows, :] - m_new)
                    a_ref[h2, rows, :] = alpha
                    l_ref[h2, rows, :] = (alpha * l_ref[h2, rows, :]
                                          + wide(jnp.sum(p, axis=1, keepdims=True)))
                    p_ref[h2, rows, :] = p.astype(BF16)
                acc_ref[h2] = (a_ref[h2, :, :HEAD_DIM] * acc_ref[h2]
                               + _dot(p_ref[h2], v_ref[ks, hl]))

        logits_into(sa_ref, 0)

        def pair(i, _):
            logits_into(sb_ref, 2 * i + 1)
            consume(sa_ref, 2 * i, False)
            logits_into(sa_ref, 2 * i + 2)
            consume(sb_ref, 2 * i + 1, False)
            return 0

        lax.fori_loop(0, qi // 2, pair, 0)

        @pl.when(qi % 2 == 1)
        def _():
            logits_into(sb_ref, qi)
            consume(sa_ref, qi - 1, False)
            consume(sb_ref, qi, True)

        @pl.when(qi % 2 == 0)
        def _():
            consume(sa_ref, qi, True)

        for h2, hl in enumerate(lanes):
            o_ref[:, hl] = (acc_ref[h2] / l_ref[h2, :, :HEAD_DIM]).astype(BF16)
            lse_ref[:, hl] = m_ref[h2, :, :HEAD_DIM] + jnp.log(l_ref[h2, :, :HEAD_DIM])

    qblk = pl.BlockSpec((T, LANES), lambda h, i: (i, h))
    kv = pl.BlockSpec((S, LANES), lambda h, i: (0, h))
    return pl.pallas_call(
        body, name=name, grid=(N_CBLK, n_t),
        in_specs=[qblk, kv, kv, pl.BlockSpec((None, 2, S), lambda h, i: (h, 0, 0))],
        out_specs=[qblk, qblk],
        out_shape=[jax.ShapeDtypeStruct((S, D_MODEL), BF16),
                   jax.ShapeDtypeStruct((S, D_MODEL), F32)],
        scratch_shapes=[pltpu.VMEM((2, T, T), F32), pltpu.VMEM((2, T, T), F32),
                        pltpu.VMEM((2, T, T), BF16),
                        pltpu.VMEM((2, T, LANES), F32), pltpu.VMEM((2, T, LANES), F32),
                        pltpu.VMEM((2, T, HEAD_DIM), F32), pltpu.VMEM((2, T, LANES), F32)],
        compiler_params=_params("parallel", "parallel"),
    )(qs_, kn, vb, c_row)


def _attn_bwd(qs_, kn, vb, do, o, lse, c_row, name):
    S = qs_.shape[0]
    T, n_t = _attn_tiles(S)

    def body(q_ref, k_ref, v_ref, do_ref, o_ref, lse_ref, cr_ref,
             dq_ref, dk_ref, dv_ref, dc_ref, rho_ref, dd_ref):
        kj = pl.program_id(1)
        causal = _causal(T)
        lanes = [slice(h2 * HEAD_DIM, (h2 + 1) * HEAD_DIM) for h2 in range(2)]
        ones = [slice(h2 * HEAD_DIM, h2 * HEAD_DIM + 1) for h2 in range(2)]

        @pl.when(kj == 0)
        def _():
            dq_ref[...] = jnp.zeros_like(dq_ref)
            rho_ref[...] = jnp.zeros_like(rho_ref)
            p_sum = _head_group_matrix(1.0)

            def fill(ci, _):
                rows = pl.ds(pl.multiple_of(ci * T, T), T)
                dd_ref[rows, :] = _group_dot(do_ref[rows, :].astype(F32) * o_ref[rows, :].astype(F32),
                                             p_sum)
                return 0

            lax.fori_loop(0, n_t, fill, 0)

        kh = [k_ref[:, hl] for hl in lanes]
        vh = [v_ref[:, hl] for hl in lanes]
        ck = [cr_ref[h2:h2 + 1, :] for h2 in range(2)]

        def step(qi, carry, masked):
            qs = pl.ds(pl.multiple_of(qi * T, T), T)
            out = []
            for h2, hl in enumerate(lanes):
                dk, dv, dc = carry[h2]
                qh, doh = q_ref[qs, hl], do_ref[qs, hl]
                s = _dot_nt(qh, kh[h2]) - ck[h2]
                if masked:
                    s = jnp.where(causal, s, NEG_INF)
                p = jnp.exp(s - lse_ref[qs, ones[h2]])
                ds = p * (_dot_nt(doh, vh[h2]) - dd_ref[qs, ones[h2]])
                dsb = ds.astype(BF16)
                dq_ref[qs, hl] += _dot(dsb, kh[h2])
                rho_ref[qs, hl] += jnp.broadcast_to(jnp.sum(ds, axis=1, keepdims=True),
                                                    (T, HEAD_DIM))
                out.append((dk + _dot_tn(dsb, qh), dv + _dot_tn(p.astype(BF16), doh),
                            dc - jnp.sum(ds, axis=0, keepdims=True)))
            return tuple(out)

        init = tuple((jnp.zeros((T, HEAD_DIM), F32), jnp.zeros((T, HEAD_DIM), F32),
                      jnp.zeros((1, T), F32)) for _ in lanes)
        carry = step(kj, init, True)
        carry = lax.fori_loop(kj + 1, n_t, lambda qi, c: step(qi, c, False), carry)
        for h2, ((dk, dv, dc), hl) in enumerate(zip(carry, lanes)):
            dk_ref[:, hl] = dk
            dv_ref[:, hl] = dv.astype(BF16)
            dc_ref[h2:h2 + 1, :] = dc

    kblk = pl.BlockSpec((T, LANES), lambda h, j: (j, h))
    full = pl.BlockSpec((S, LANES), lambda h, j: (0, h))
    crow = pl.BlockSpec((None, 2, T), lambda h, j: (h, 0, j))
    wide = jax.ShapeDtypeStruct((S, D_MODEL), F32)
    return pl.pallas_call(
        body, name=name, grid=(N_CBLK, n_t),
        in_specs=[full, kblk, kblk, full, full, full, crow],
        out_specs=[full, kblk, kblk, crow, full],
        out_shape=[wide, wide, jax.ShapeDtypeStruct((S, D_MODEL), BF16),
                   jax.ShapeDtypeStruct((N_CBLK, 2, S), F32), wide],
        scratch_shapes=[pltpu.VMEM((S, LANES), F32)],
        compiler_params=_params("parallel", "arbitrary"),
    )(qs_, kn, vb, do, o, lse, c_row)


ALL_PEERS = tuple(range(1, N_DEV))
NEAR_PEERS = (1, 2, 4, 6)
FAR_CHIPS = (2, 4, 6)


def _landing_shapes(arrays, gathers):
    return [jax.ShapeDtypeStruct((N_DEV,) + a.shape if g else a.shape, a.dtype)
            for a, g in zip(arrays, gathers)]


def _my_index():
    return 4 * lax.axis_index("x") + 2 * lax.axis_index("y") + lax.axis_index("c")


def _own_copies(srcs, lands, gathers, sems):
    me = _my_index()
    return [pltpu.make_async_copy(src if g else src.at[me], land.at[me], sems.at[a])
            for a, (src, land, g) in enumerate(zip(srcs, lands, gathers))]


def _peer_copies(srcs, lands, gathers, send_sems, recv_sems, ks=ALL_PEERS):
    x, y, c = lax.axis_index("x"), lax.axis_index("y"), lax.axis_index("c")
    me = 4 * x + 2 * y + c
    out = []
    for j, k in enumerate(ks):
        to = (1 - x if k & 4 else x, 1 - y if k & 2 else y, 1 - c if k & 1 else c)
        peer = 4 * to[0] + 2 * to[1] + to[2]
        for a, (src, land, g) in enumerate(zip(srcs, lands, gathers)):
            sem = a * len(ks) + j
            src_blk = src if g else src.at[peer]

            def copy(slot, src_blk=src_blk, land=land, sem=sem, to=to):
                return pltpu.make_async_remote_copy(
                    src_ref=src_blk, dst_ref=land.at[slot], send_sem=send_sems.at[sem],
                    recv_sem=recv_sems.at[sem], device_id=to,
                    device_id_type=pl.DeviceIdType.MESH)

            out.append((k, a, copy(me), copy(peer)))
    return out


def _forward_copies(lands, send_sems, recv_sems):
    x, y, c = lax.axis_index("x"), lax.axis_index("y"), lax.axis_index("c")
    out = []
    for j, f in enumerate(FAR_CHIPS):
        chip = 4 * (1 - x if f & 4 else x) + 2 * (1 - y if f & 2 else y)
        for a, land in enumerate(lands):
            sem = a * len(FAR_CHIPS) + j

            def copy(slot, land=land, sem=sem):
                return pltpu.make_async_remote_copy(
                    src_ref=land.at[slot], dst_ref=land.at[slot], send_sem=send_sems.at[sem],
                    recv_sem=recv_sems.at[sem], device_id=(x, y, 1 - c),
                    device_id_type=pl.DeviceIdType.MESH)

            out.append((f, a, copy(chip + c), copy(chip + 1 - c)))
    return out


def _exchange(arrays, gathers, name, two_level=False):
    n = len(arrays)
    ks = NEAR_PEERS if two_level else ALL_PEERS
    assert not two_level or all(gathers)

    def body(*refs):
        ins, outs = refs[:n], refs[n:2 * n]
        send_sems, recv_sems, own_sems, fwd_send_sems, fwd_recv_sems = refs[2 * n:]
        own = _own_copies(ins, outs, gathers, own_sems)
        for cp in own:
            cp.start()
        copies = _peer_copies(ins, outs, gathers, send_sems, recv_sems, ks)
        for _, _, send, _ in copies:
            send.start()
        passed = {}
        if two_level:
            passed = {(f, a): (send, arrival)
                      for f, a, send, arrival in _forward_copies(outs, fwd_send_sems, fwd_recv_sems)}
        for k, a, _, arrival in copies:
            arrival.wait_recv()
            if (k, a) in passed:
                passed[k, a][0].start()
        for send, arrival in passed.values():
            arrival.wait_recv()
            send.wait_send()
        for _, _, send, _ in copies:
            send.wait_send()
        for cp in own:
            cp.wait()

    hbm = pl.BlockSpec(memory_space=pl.ANY)
    return pl.pallas_call(
        body, name=name,
        in_specs=[hbm] * n, out_specs=[hbm] * n, out_shape=_landing_shapes(arrays, gathers),
        scratch_shapes=[pltpu.SemaphoreType.DMA((n * len(ks),)),
                        pltpu.SemaphoreType.DMA((n * len(ks),)),
                        pltpu.SemaphoreType.DMA((n,)),
                        pltpu.SemaphoreType.DMA((n * len(FAR_CHIPS),)),
                        pltpu.SemaphoreType.DMA((n * len(FAR_CHIPS),))],
        compiler_params=pltpu.CompilerParams(has_side_effects=True),
    )(*arrays)


_HBM = pl.BlockSpec(memory_space=pltpu.HBM)
_SEM = pl.BlockSpec(memory_space=pltpu.SEMAPHORE)
_ANY = pl.BlockSpec(memory_space=pl.ANY)
_DATAFLOW = pltpu.SideEffectType.DATAFLOW_SIDE_EFFECTING


def _in_hbm(a):
    return pltpu.with_memory_space_constraint(a, pltpu.HBM)


def _exchange_start(arrays, gathers, after, name, ks=ALL_PEERS):
    n = len(arrays)
    lands = [lax.empty(s.shape, s.dtype) for s in _landing_shapes(arrays, gathers)]

    def body(*refs):
        srcs, dsts = refs[:n], refs[n:2 * n]
        send_sems, recv_sems, own_sems = refs[2 * n + 1:2 * n + 4]
        token = refs[-1]
        for cp in _own_copies(srcs, dsts, gathers, own_sems):
            cp.start()
        for _, _, send, _ in _peer_copies(srcs, dsts, gathers, send_sems, recv_sems, ks):
            send.start()
        token[...] = jnp.zeros_like(token)

    hbm_like = [pltpu.HBM(a.shape, a.dtype) for a in list(arrays) + lands]
    res = pl.pallas_call(
        body, name=name,
        in_specs=[_HBM] * (2 * n) + [_ANY],
        out_specs=(_SEM, _SEM, _SEM, *[_HBM] * (2 * n), pl.BlockSpec(memory_space=pltpu.VMEM)),
        out_shape=(pltpu.SemaphoreType.DMA((n * len(ks),)), pltpu.SemaphoreType.DMA((n * len(ks),)),
                   pltpu.SemaphoreType.DMA((n,)), *hbm_like,
                   jax.ShapeDtypeStruct((8, LANES), F32)),
        input_output_aliases={i: 3 + i for i in range(2 * n)},
        compiler_params=pltpu.CompilerParams(has_side_effects=_DATAFLOW),
    )(*[_in_hbm(a) for a in list(arrays) + lands], after)
    return (res[0], res[1], res[2], res[3:3 + n], res[3 + n:3 + 2 * n]), res[-1]


def _exchange_wait(started, gathers, after, name, ks=ALL_PEERS):
    send_sems, recv_sems, own_sems, arrays, lands = started
    n = len(arrays)

    def body(*refs):
        srcs, dsts = refs[:n], refs[n:2 * n]
        for _, _, send, arrival in _peer_copies(srcs, dsts, gathers, refs[2 * n], refs[2 * n + 1],
                                                ks):
            arrival.wait_recv()
            send.wait_send()
        for cp in _own_copies(srcs, dsts, gathers, refs[2 * n + 2]):
            cp.wait()

    hbm_like = [pltpu.HBM(a.shape, a.dtype) for a in list(arrays) + list(lands)]
    res = pl.pallas_call(
        body, name=name,
        in_specs=[_HBM] * (2 * n) + [_SEM, _SEM, _SEM, _ANY],
        out_specs=[_HBM] * (2 * n), out_shape=hbm_like,
        input_output_aliases={i: i for i in range(2 * n)},
        compiler_params=pltpu.CompilerParams(has_side_effects=_DATAFLOW),
    )(*arrays, *lands, send_sems, recv_sems, own_sems, after)
    return res[n:]


def _forward_start(lands, after, name):
    n = len(lands)

    def body(*refs):
        send_sems, recv_sems = refs[n + 1:n + 3]
        for _, _, send, _ in _forward_copies(refs[:n], send_sems, recv_sems):
            send.start()
        refs[-1][...] = jnp.zeros_like(refs[-1])

    n_sem = n * len(FAR_CHIPS)
    res = pl.pallas_call(
        body, name=name,
        in_specs=[_HBM] * n + [_ANY],
        out_specs=(_SEM, _SEM, *[_HBM] * n, pl.BlockSpec(memory_space=pltpu.VMEM)),
        out_shape=(pltpu.SemaphoreType.DMA((n_sem,)), pltpu.SemaphoreType.DMA((n_sem,)),
                   *[pltpu.HBM(a.shape, a.dtype) for a in lands],
                   jax.ShapeDtypeStruct((8, LANES), F32)),
        input_output_aliases={i: 2 + i for i in range(n)},
        compiler_params=pltpu.CompilerParams(has_side_effects=_DATAFLOW),
    )(*[_in_hbm(a) for a in lands], after)
    return (res[0], res[1], res[2:2 + n]), res[-1]


def _forward_wait(started, after, name):
    send_sems, recv_sems, lands = started
    n = len(lands)

    def body(*refs):
        for _, _, send, arrival in _forward_copies(refs[:n], refs[n], refs[n + 1]):
            arrival.wait_recv()
            send.wait_send()

    return pl.pallas_call(
        body, name=name,
        in_specs=[_HBM] * n + [_SEM, _SEM, _ANY],
        out_specs=[_HBM] * n, out_shape=[pltpu.HBM(a.shape, a.dtype) for a in lands],
        input_output_aliases={i: i for i in range(n)},
        compiler_params=pltpu.CompilerParams(has_side_effects=_DATAFLOW),
    )(*lands, send_sems, recv_sems, after)


def _reduce_adamw(parts, w, m, v, name):
    n_layer = len(parts)
    n, R, C = parts[0].shape
    tr = 256 if R % 256 == 0 else R
    n_t = R // tr

    def body(*refs):
        p_refs = refs[:n_layer]
        w_ref, m_ref, v_ref, g_ref, d_ref, nm_ref, nv_ref = refs[n_layer:]

        def update(p_ref):
            g = p_ref[0].astype(F32)
            for s in range(1, n):
                g = g + p_ref[s].astype(F32)
            g_ref[...] = g
            m_new = ADAM_B1 * m_ref[...] + (1.0 - ADAM_B1) * g
            v_new = ADAM_B2 * v_ref[...] + (1.0 - ADAM_B2) * (g * g)
            nm_ref[...] = m_new
            nv_ref[...] = v_new
            m_hat = m_new / (1.0 - ADAM_B1 ** ADAM_STEP)
            v_hat = v_new / (1.0 - ADAM_B2 ** ADAM_STEP)
            d_ref[...] = -ADAM_LR * (m_hat / (jnp.sqrt(v_hat) + ADAM_EPS) + ADAM_WD * w_ref[...])

        for layer, p_ref in enumerate(p_refs):
            pl.when(pl.program_id(0) == layer)(functools.partial(update, p_ref))

    def parts_spec(layer):
        def index(l, i):
            return 0, jnp.where(l < layer, 0, jnp.where(l > layer, n_t - 1, i)), 0
        return pl.BlockSpec((n, tr, C), index)

    blk = pl.BlockSpec((None, tr, C), lambda l, i: (l, i, 0))
    out = jax.ShapeDtypeStruct((n_layer, R, C), F32)
    return pl.pallas_call(
        body, name=name, grid=(n_layer, n_t),
        in_specs=[parts_spec(layer) for layer in range(n_layer)] + [blk, blk, blk],
        out_specs=[blk] * 4, out_shape=[out] * 4,
        compiler_params=_params("arbitrary", "arbitrary"),
    )(*parts, w, m, v)


def _pack(arrays):
    flat = jnp.concatenate([a.reshape(-1).astype(F32) for a in arrays])
    pad = (-flat.shape[0]) % (8 * LANES)
    return jnp.pad(flat, (0, pad)).reshape(-1, LANES)


def _unpack(buf, shapes):
    flat = buf.reshape(-1)
    out, off = [], 0
    for shp in shapes:
        size = 1
        for s in shp:
            size *= s
        out.append(flat[off:off + size].reshape(shp))
        off += size
    return out


def _block_diag_pairs(w):
    w = w.reshape(N_CBLK, 2, LRU_BLOCK_DIM, LRU_BLOCK_DIM)
    z = jnp.zeros_like(w[:, 0])
    top = jnp.concatenate([w[:, 0], z], axis=2)
    bot = jnp.concatenate([z, w[:, 1]], axis=2)
    return jnp.concatenate([top, bot], axis=1)


def _diag_pairs(m):
    h = LRU_BLOCK_DIM
    return jnp.stack([m[:, :h, :h], m[:, h:, h:]], axis=1).reshape(2 * N_CBLK, h, h)


SMALL = ("mlp_norm", "lru_conv_b", "lru_w_r", "lru_b_r", "lru_w_i", "lru_b_i",
         "lru_lambda", "fox_b_f", "fox_q_gain", "fox_k_gain")
WEIGHTS = ("mix_norm", "mlp_norm", "mlp_w1", "mlp_w2", "lru_w_in", "lru_conv_w", "lru_conv_b",
           "lru_w_r", "lru_b_r", "lru_w_i", "lru_b_i", "lru_lambda", "lru_w_out", "fox_w_in",
           "fox_b_f", "fox_q_gain", "fox_k_gain", "fox_w_out")


def kernel(x, mix_norm, mlp_norm, mlp_w1, mlp_w2, lru_w_in, lru_conv_w, lru_conv_b, lru_w_r, lru_b_r, lru_w_i, lru_b_i, lru_lambda, lru_w_out, fox_w_in, fox_b_f, fox_q_gain, fox_k_gain, fox_w_out, loss_target, m_mix_norm, m_mlp_norm, m_mlp_w1, m_mlp_w2, m_lru_w_in, m_lru_conv_w, m_lru_conv_b, m_lru_w_r, m_lru_b_r, m_lru_w_i, m_lru_b_i, m_lru_lambda, m_lru_w_out, m_fox_w_in, m_fox_b_f, m_fox_q_gain, m_fox_k_gain, m_fox_w_out, v_mix_norm, v_mlp_norm, v_mlp_w1, v_mlp_w2, v_lru_w_in, v_lru_conv_w, v_lru_conv_b, v_lru_w_r, v_lru_b_r, v_lru_w_i, v_lru_b_i, v_lru_lambda, v_lru_w_out, v_fox_w_in, v_fox_b_f, v_fox_q_gain, v_fox_k_gain, v_fox_w_out):
    w_in = dict(mix_norm=mix_norm, mlp_norm=mlp_norm, mlp_w1=mlp_w1, mlp_w2=mlp_w2,
                lru_w_in=lru_w_in, lru_conv_w=lru_conv_w, lru_conv_b=lru_conv_b, lru_w_r=lru_w_r,
                lru_b_r=lru_b_r, lru_w_i=lru_w_i, lru_b_i=lru_b_i, lru_lambda=lru_lambda,
                lru_w_out=lru_w_out, fox_w_in=fox_w_in, fox_b_f=fox_b_f, fox_q_gain=fox_q_gain,
                fox_k_gain=fox_k_gain, fox_w_out=fox_w_out)
    m_in = dict(mix_norm=m_mix_norm, mlp_norm=m_mlp_norm, mlp_w1=m_mlp_w1, mlp_w2=m_mlp_w2,
                lru_w_in=m_lru_w_in, lru_conv_w=m_lru_conv_w, lru_conv_b=m_lru_conv_b,
                lru_w_r=m_lru_w_r, lru_b_r=m_lru_b_r, lru_w_i=m_lru_w_i, lru_b_i=m_lru_b_i,
                lru_lambda=m_lru_lambda, lru_w_out=m_lru_w_out, fox_w_in=m_fox_w_in,
                fox_b_f=m_fox_b_f, fox_q_gain=m_fox_q_gain, fox_k_gain=m_fox_k_gain,
                fox_w_out=m_fox_w_out)
    v_in = dict(mix_norm=v_mix_norm, mlp_norm=v_mlp_norm, mlp_w1=v_mlp_w1, mlp_w2=v_mlp_w2,
                lru_w_in=v_lru_w_in, lru_conv_w=v_lru_conv_w, lru_conv_b=v_lru_conv_b,
                lru_w_r=v_lru_w_r, lru_b_r=v_lru_b_r, lru_w_i=v_lru_w_i, lru_b_i=v_lru_b_i,
                lru_lambda=v_lru_lambda, lru_w_out=v_lru_w_out, fox_w_in=v_fox_w_in,
                fox_b_f=v_fox_b_f, fox_q_gain=v_fox_q_gain, fox_k_gain=v_fox_k_gain,
                fox_w_out=v_fox_w_out)
    D = D_MODEL
    S = x.shape[1]
    x0, target = x[0], loss_target[0]
    me = 4 * lax.axis_index("x") + 2 * lax.axis_index("y") + lax.axis_index("c")

    def bf16(a):
        return a.astype(BF16)

    (lru_in_g,) = _exchange([bf16(lru_w_in[0])], [True], "gather_lru_in", two_level=True)
    gather_lru, tok = _exchange_start([bf16(lru_w_out[0]), lru_conv_w[0]], [True] * 2, lru_in_g,
                                      "gather_lru_start")
    gather_mlp0, tok = _exchange_start([bf16(mlp_w1[0]), bf16(mlp_w2[0])], [True] * 2, tok,
                                       "gather_mlp0_start", NEAR_PEERS)
    gather_fox, tok = _exchange_start([bf16(fox_w_in[0]), bf16(fox_w_out[0])], [True] * 2, tok,
                                      "gather_fox_start")
    gather_mlp1, tok = _exchange_start([bf16(mlp_w1[1]), bf16(mlp_w2[1])], [True] * 2, tok,
                                       "gather_mlp1_start", NEAR_PEERS)

    def pass_on(started, after, name):
        lands = _exchange_wait(started, [True] * 2, after, name + "_wait", NEAR_PEERS)
        return _forward_start(lands, after, name + "_pass_start")
    wr =_block_diag_pairs(lru_w_r[0]).astype(BF16)
    wi = _block_diag_pairs(lru_w_i[0]).astype(BF16)
    b_r, b_i = lru_b_r.reshape(1, D), lru_b_i.reshape(1, D)
    q_gain, k_gain = jnp.tile(fox_q_gain, (1, 2)), jnp.tile(fox_k_gain, (1, 2))
    b_f = jnp.pad(fox_b_f, ((0, 0), (0, LANES - N_HEADS)))
    g_mix0, g_mix1 = mix_norm[0:1] + tok[0, 0], mix_norm[1:2]
    g_mlp0, g_mlp1 = mlp_norm[0:1], mlp_norm[1:2]

    (u0,), h0 = _norm_matmul(x0, g_mix0, [lru_in_g], "lru_in_proj")
    lru_out_g, conv_g = _exchange_wait(gather_lru, [True] * 2, u0, "gather_lru_wait")
    lru_out_w = lru_out_g.reshape(D, D)
    conv_w = conv_g.transpose(1, 0, 2).reshape(CONV_WIDTH, D)
    y_lru, hs =_lru_fwd(u0, conv_w, lru_conv_b, wr, b_r, wi, b_i, lru_lambda, "lru_core")
    pass_mlp0, tok = pass_on(gather_mlp0, y_lru, "gather_mlp0")
    x1 = _matmul_res(y_lru, lru_out_w, x0, "lru_out_proj", tok)
    w1g0, w2g0 = _forward_wait(pass_mlp0, x1, "gather_mlp0_pass_wait")
    x2, h1, r1 = _mlp_fwd(x1, g_mlp0, w1g0, w2g0, "mlp0")
    fox_in_g, fox_out_g = _exchange_wait(gather_fox, [True] * 2, x2, "gather_fox_wait")
    fox_out_w = fox_out_g.reshape(D, D)
    fox_full = jnp.concatenate([fox_in_g[d] for d in range(N_DEV)], axis=1)
    wqkv = fox_full[:, :3 * D].reshape(D, 3, D).transpose(1, 0, 2)
    wf = jnp.pad(fox_full[:, 3 * D:], ((0, 0), (0, LANES - N_HEADS)))[None]
    uq, uk, f, h2, qn, kn, vb = _fox_in_proj(x2, g_mix1, wqkv, wf, q_gain, k_gain, "fox_in_proj")
    c_col = _forget_fwd(f, b_f, "fox_forget")
    c_row = c_col[:, :N_HEADS].T.reshape(N_CBLK, 2, S)
    o, lse = _attn_fwd(qn, kn, vb, c_row, "fox_attn")
    pass_mlp1, tok = pass_on(gather_mlp1, o, "gather_mlp1")
    x3 = _matmul_res(o, fox_out_w, x2, "fox_out_proj", tok)
    w1g1, w2g1 = _forward_wait(pass_mlp1, x3, "gather_mlp1_pass_wait")
    loss_local, dx4, h3, r3 = _mlp_fwd(x3, g_mlp1, w1g1, w2g1, "mlp1", target)

    dx3, dg_mlp1, da3, dx4_b = _mlp_bwd(dx4, x3, g_mlp1, r3, w1g1, w2g1, "mlp1_bwd")
    dw1_1 = _matmul_tn(h3, da3, "mlp1_dw1", cols=2, col_blocks=N_DEV)
    dw2_1 = _matmul_tn(r3, dx4_b, "mlp1_dw2", rows=2, a_square=True, tm=1024).reshape(N_DEV, -1, D)
    grads_mlp1, tok = _exchange_start([dw1_1, dw2_1], [False] * 2, tok, "grads_mlp1_start")
    do = _matmul_nt(dx3, fox_out_w, "fox_out_bwd", BF16, tok)
    d_fox_out = _matmul_tn(o, dx3, "fox_out_dw").reshape(N_DEV, -1, D)
    dqn, dkn, dv, dc_row, rho = _attn_bwd(qn, kn, vb, do, o, lse, c_row, "fox_attn_bwd")
    duq, duk, dq_gain, dk_gain = _qk_bwd(uq, uk, dqn, dkn, q_gain, k_gain, "fox_qk_norm_bwd")
    dc_k = jnp.pad(dc_row.reshape(N_HEADS, S).T, ((0, 0), (0, LANES - N_HEADS)))
    df, db_f = _forget_bwd(dc_k, rho, f, b_f, "fox_forget_bwd")
    dx2, dg_mix1 = _proj_bwd([[duq, duk, dv], [df]], [wqkv, wf], x2, g_mix1, dx3, "fox_in_bwd")
    d_fox_in = jnp.concatenate(
        [_matmul_tn(h2, duq, "fox_in_dwq"), _matmul_tn(h2, duk, "fox_in_dwk"),
         _matmul_tn(h2, dv, "fox_in_dwv"), _matmul_tn(h2, df, "fox_in_dwf")[:, :N_HEADS]], axis=1)
    shard = (3 * D + N_HEADS) // N_DEV
    d_fox_in = jnp.stack([d_fox_in[:, d * shard:(d + 1) * shard] for d in range(N_DEV)])
    grads_fox, tok = _exchange_start([d_fox_in, d_fox_out], [False] * 2, tok, "grads_fox_start")
    dx1, dg_mlp0, da1, dx2_b = _mlp_bwd(dx2, x1, g_mlp0 + tok[0, 0], r1, w1g0, w2g0, "mlp0_bwd")
    dw1_0 = _matmul_tn(h1, da1, "mlp0_dw1", cols=2, col_blocks=N_DEV)
    dw2_0 = _matmul_tn(r1, dx2_b, "mlp0_dw2", rows=2, a_square=True, tm=1024).reshape(N_DEV, -1, D)
    grads_mlp0, tok = _exchange_start([dw1_0, dw2_0], [False] * 2, tok, "grads_mlp0_start")
    dy_lru = _matmul_nt(dx1, lru_out_w, "lru_out_bwd", F32, tok)
    d_lru_out = _matmul_tn(y_lru, dx1, "lru_out_dw").reshape(N_DEV, -1, D)
    dgp, dxb, d_conv_w, d_conv_b, d_b_r, d_b_i, d_lam, d_wr, d_wi = _lru_bwd(
        dy_lru, u0, hs, conv_w, lru_conv_b, wr, b_r, wi, b_i, lru_lambda, "lru_core_bwd")

    small_grads = dict(
        mlp_norm=jnp.concatenate([dg_mlp0, dg_mlp1], axis=0),
        lru_conv_b=d_conv_b, lru_w_r=_diag_pairs(d_wr), lru_b_r=d_b_r, lru_w_i=_diag_pairs(d_wi),
        lru_b_i=d_b_i, lru_lambda=d_lam, fox_b_f=db_f[:, :N_HEADS],
        fox_q_gain=dq_gain[:, :HEAD_DIM], fox_k_gain=dk_gain[:, :HEAD_DIM])
    small_partial = _pack([dg_mix1] + [small_grads[n] for n in SMALL] + [d_conv_w])
    grads_lru_out, tok = _exchange_start([d_lru_out, small_partial], [False, True], tok,
                                         "grads_lru_out_start")
    dx0, dg_mix0 = _proj_bwd([[dgp, dxb]], [lru_in_g], x0, mix_norm[0:1] + tok[0, 0], dx1,
                             "lru_in_bwd")
    d_lru_in = jnp.concatenate([_matmul_tn(h0, dgp, "lru_in_dw_gate", col_blocks=4),
                                _matmul_tn(h0, dxb, "lru_in_dw_x", col_blocks=4)], axis=0)
    grads_lru_in, tok = _exchange_start([d_lru_in, dg_mix0], [False, True], tok,
                                        "grads_lru_in_start")

    grads, deltas, new_m, new_v = {}, {}, {}, {}

    def update(name, parts):
        w, m, v = w_in[name], m_in[name], v_in[name]
        shape = w.shape
        stacked = (len(parts), -1, shape[-1])
        w3 = w.reshape(stacked)
        res = _reduce_adamw([p.reshape((N_DEV,) + w3.shape[1:]) for p in parts], w3,
                            m.reshape(stacked), v.reshape(stacked), "adamw_" + name)
        return [r.reshape(shape) for r in res]

    def store(name, res):
        grads[name], deltas[name], new_m[name], new_v[name] = res

    p_w1_1, p_w2_1 = _exchange_wait(grads_mlp1, [False] * 2, tok, "grads_mlp1_wait")
    p_fox_in, p_fox_out = _exchange_wait(grads_fox, [False] * 2, p_w1_1, "grads_fox_wait")
    store("fox_w_in", update("fox_w_in", [p_fox_in]))
    store("fox_w_out", update("fox_w_out", [p_fox_out]))
    p_w1_0, p_w2_0 = _exchange_wait(grads_mlp0, [False] * 2, grads["fox_w_out"], "grads_mlp0_wait")
    store("mlp_w1", update("mlp_w1", [p_w1_0, p_w1_1]))
    store("mlp_w2", update("mlp_w2", [p_w2_0, p_w2_1]))
    p_lru_out, p_small = _exchange_wait(grads_lru_out, [False, True], grads["mlp_w2"],
                                        "grads_lru_out_wait")
    store("lru_w_out", update("lru_w_out", [p_lru_out]))
    p_lru_in, p_mix0 = _exchange_wait(grads_lru_in, [False, True], grads["lru_w_out"],
                                      "grads_lru_in_wait")
    store("lru_w_in", update("lru_w_in", [p_lru_in]))

    mix0 = [r[0] for r in _reduce_adamw([p_mix0], mix_norm[None, 0:1], m_mix_norm[None, 0:1],
                                        v_mix_norm[None, 0:1], "adamw_mix0")]
    packed = lambda src, first: _pack([first] + [src[n] for n in SMALL]
                                      + [jnp.zeros((CONV_WIDTH, D))])[None]
    small_shapes = [(1, D)] + [w_in[n].shape for n in SMALL]
    n_small = sum(math.prod(s) for s in small_shapes)
    res_small = _reduce_adamw([p_small], packed(w_in, mix_norm[1:2]), packed(m_in, m_mix_norm[1:2]),
                              packed(v_in, v_mix_norm[1:2]), "adamw_small")
    for name, *vals in zip(("mix1",) + SMALL, *[_unpack(r, small_shapes) for r in res_small]):
        if name == "mix1":
            vals = [jnp.concatenate([r0, r1], axis=0) for r0, r1 in zip(mix0, vals)]
            name = "mix_norm"
        store(name, vals)
    conv_parts = p_small.reshape(N_DEV, -1)[:, n_small:n_small + CONV_WIDTH * D]
    conv_parts = conv_parts.reshape(N_DEV, CONV_WIDTH, N_DEV, LANES)
    conv_parts = lax.dynamic_index_in_dim(conv_parts, me, axis=2, keepdims=False)
    store("lru_conv_w", update("lru_conv_w", [conv_parts]))

    loss = lax.psum(loss_local[0, 0], ("x", "y", "c"))
    return (loss, dx0[None], *[grads[n] for n in WEIGHTS], *[deltas[n] for n in WEIGHTS],
            *[new_m[n] for n in WEIGHTS], *[new_v[n] for n in WEIGHTS])
```
